```python
import jax, jax.numpy as jnp
from jax import lax
import numpy as np

D_MODEL = 1024
BATCH = 8
SEQ = 4096
DEPTH = 1

GRID_W = 64
ATTN_HEADS = 8
ATTN_KV_HEADS = 2
ATTN_HEAD_DIM = 64
RET_HEADS = 4
RET_KEY_DIM = 64
RET_VALUE_DIM = 128
Q_BLOCK = 128
RET_CHUNK = 128
ROPE_THETA = 10000.0
EPS = 1e-6

ATTN_WIDTH = ATTN_HEADS * ATTN_HEAD_DIM
ATTN_KV_WIDTH = ATTN_KV_HEADS * ATTN_HEAD_DIM
RET_QK_WIDTH = RET_HEADS * RET_KEY_DIM
RET_V_WIDTH = RET_HEADS * RET_VALUE_DIM
MERGE_WIDTH = 2 * D_MODEL
IN_SIZES = (ATTN_WIDTH, ATTN_KV_WIDTH, ATTN_KV_WIDTH, ATTN_WIDTH,
            RET_QK_WIDTH, RET_QK_WIDTH, RET_V_WIDTH, RET_V_WIDTH, MERGE_WIDTH)
IN_WIDTH = int(sum(IN_SIZES))
IN_SPLITS = tuple(int(s) for s in np.cumsum(IN_SIZES)[:-1])

kernel_name = "hybrid_gqa_axialrope_bidir_retention_gated_merge"


def rmsnorm(x, g):
    xf = x.astype(jnp.float32)
    xf = xf * lax.rsqrt(jnp.mean(xf * xf, axis=-1, keepdims=True) + EPS)
    return xf.astype(x.dtype) * g


def axial_rope_tables(seq_len, head_dim):
    rows = seq_len // GRID_W
    r, cidx = jnp.meshgrid(jnp.arange(rows), jnp.arange(GRID_W), indexing='ij')
    row = r.reshape(-1).astype(jnp.float32)
    col = cidx.reshape(-1).astype(jnp.float32)
    half = head_dim // 2
    inv_freq = ROPE_THETA ** (-jnp.arange(0, half, 2, dtype=jnp.float32) / half)
    ang_r = row[:, None] * inv_freq[None, :]
    ang_c = col[:, None] * inv_freq[None, :]
    return (jnp.cos(ang_r), jnp.sin(ang_r), jnp.cos(ang_c), jnp.sin(ang_c))


def _rotate(xp, cos, sin):
    x1, x2 = jnp.split(xp, 2, axis=-1)
    cos = cos[None, :, None, :].astype(xp.dtype)
    sin = sin[None, :, None, :].astype(xp.dtype)
    return jnp.concatenate([x1 * cos - x2 * sin, x2 * cos + x1 * sin], axis=-1)


def apply_axial_rope(x, tables):
    cos_r, sin_r, cos_c, sin_c = tables
    xr, xc = jnp.split(x, 2, axis=-1)
    return jnp.concatenate([_rotate(xr, cos_r, sin_r), _rotate(xc, cos_c, sin_c)], axis=-1)


def gqa_attention(q, k, v, qn_g, kn_g, tables):
    B, S = q.shape[0], q.shape[1]
    G = ATTN_HEADS // ATTN_KV_HEADS
    q = apply_axial_rope(rmsnorm(q, qn_g), tables) * (ATTN_HEAD_DIM ** -0.5)
    k = apply_axial_rope(rmsnorm(k, kn_g), tables)
    nblk = S // Q_BLOCK
    qb = q.reshape(B, nblk, Q_BLOCK, ATTN_KV_HEADS, G, ATTN_HEAD_DIM).transpose(1, 0, 2, 3, 4, 5)

    def block(qi):
        s = jnp.einsum('bqkgd,bskd->bkgqs', qi, k).astype(jnp.float32)
        p = jax.nn.softmax(s, axis=-1).astype(v.dtype)
        return jnp.einsum('bkgqs,bskd->bqkgd', p, v)

    o = lax.map(block, qb)
    return o.transpose(1, 0, 2, 3, 4, 5).reshape(B, S, ATTN_WIDTH)


def retention_one_direction(q, k, v, log_gamma, strict):
    B, S, H, dk = q.shape
    dv = v.shape[-1]
    C = RET_CHUNK
    N = S // C
    dt = q.dtype
    qc = q.reshape(B, N, C, H, dk)
    kc = k.reshape(B, N, C, H, dk)
    vc = v.reshape(B, N, C, H, dv)
    idx = jnp.arange(C, dtype=jnp.float32)
    diff = idx[:, None] - idx[None, :]
    mask = (diff > 0) if strict else (diff >= 0)
    decay_intra = jnp.where(mask[None], jnp.exp(log_gamma[:, None, None] * jnp.maximum(diff, 0.0)[None]), 0.0)
    scores = jnp.einsum('bnihd,bnjhd->bnhij', qc, kc) * decay_intra.astype(dt)[None, None]
    o_intra = jnp.einsum('bnhij,bnjhe->bnihe', scores, vc)
    k_dec = jnp.exp(log_gamma[None, :] * (C - 1 - idx)[:, None]).astype(dt)
    kv = jnp.einsum('bnjhd,bnjhe->nbhde', kc * k_dec[:, :, None], vc)
    chunk_decay = jnp.exp(log_gamma * C).astype(kv.dtype)[None, :, None, None]

    def step(R, kv_n):
        return chunk_decay * R + kv_n, R

    _, R_prev = lax.scan(step, jnp.zeros_like(kv[0]), kv)
    q_dec = jnp.exp(log_gamma[None, :] * (idx + 1.0)[:, None]).astype(dt)
    o_inter = jnp.einsum('bnihd,nbhde->bnihe', qc * q_dec[:, :, None], R_prev)
    return (o_intra + o_inter).reshape(B, S, H, dv)


def bidirectional_retention(q, k, v, w_dec_f, w_dec_b, gn_g, tables):
    B, S = q.shape[0], q.shape[1]
    q = apply_axial_rope(q, tables)
    k = apply_axial_rope(k, tables) * (RET_KEY_DIM ** -0.5)
    lg_f = jax.nn.log_sigmoid(w_dec_f.astype(jnp.float32))
    lg_b = jax.nn.log_sigmoid(w_dec_b.astype(jnp.float32))
    o_f = retention_one_direction(q, k, v, lg_f, False)
    o_b = jnp.flip(retention_one_direction(jnp.flip(q, 1), jnp.flip(k, 1), jnp.flip(v, 1), lg_b, True), 1)
    o = (o_f + o_b).astype(jnp.float32)
    mu = jnp.mean(o, axis=-1, keepdims=True)
    var = jnp.mean(jnp.square(o - mu), axis=-1, keepdims=True)
    o = ((o - mu) * lax.rsqrt(var + EPS)).astype(v.dtype)
    return o.reshape(B, S, RET_V_WIDTH) * gn_g


def _fwd_setup_inputs(seed: int = 0) -> dict:
    key = jax.random.key(seed)
    ks = jax.random.split(key, 20)
    f32 = jnp.float32

    def w(k, shape, fan_in):
        return jax.random.normal(k, shape, f32) * (fan_in ** -0.5)

    def gain(k, shape):
        return 1.0 + 0.05 * jax.random.normal(k, shape, f32)

    base = jnp.log(2.0 ** (5.0 + jnp.arange(RET_HEADS, dtype=f32)) - 1.0)
    return {
        "x": jax.random.normal(ks[0], (BATCH, SEQ, D_MODEL), f32),
        "c": jax.random.normal(ks[1], (BATCH, D_MODEL), f32),
        "w_ada": w(ks[2], (DEPTH, D_MODEL, 3 * D_MODEL), D_MODEL) * 0.5,
        "b_ada": 0.02 * jax.random.normal(ks[3], (DEPTH, 3 * D_MODEL), f32),
        "g_pre": gain(ks[4], (DEPTH, D_MODEL)),
        "w_in": w(ks[5], (DEPTH, D_MODEL, IN_WIDTH), D_MODEL),
        "qn_g": gain(ks[6], (DEPTH, ATTN_HEAD_DIM)),
        "kn_g": gain(ks[7], (DEPTH, ATTN_HEAD_DIM)),
        "w_dec_f": base[None] + 0.1 * jax.random.normal(ks[8], (DEPTH, RET_HEADS), f32),
        "w_dec_b": base[None] + 0.1 * jax.random.normal(ks[9], (DEPTH, RET_HEADS), f32),
        "gn_g": gain(ks[10], (DEPTH, RET_V_WIDTH)),
        "w_pa": w(ks[11], (DEPTH, ATTN_WIDTH, D_MODEL), ATTN_WIDTH),
        "w_pr": w(ks[12], (DEPTH, RET_V_WIDTH, D_MODEL), RET_V_WIDTH),
        "w_out": w(ks[13], (DEPTH, D_MODEL, D_MODEL), D_MODEL),
        "g_post": gain(ks[14], (DEPTH, D_MODEL)),
    }


def _fwd_reference(x, c, w_ada, b_ada, g_pre, w_in, qn_g, kn_g, w_dec_f, w_dec_b, gn_g,
              w_pa, w_pr, w_out, g_post):
    B, S, D = x.shape
    tables = axial_rope_tables(S, ATTN_HEAD_DIM)
    c_act = jax.nn.silu(c)
    for l in range(DEPTH):
        mod = c_act @ w_ada[l] + b_ada[l]
        shift, scale, gate = jnp.split(mod, 3, axis=-1)
        h = rmsnorm(x, g_pre[l]) * (1.0 + scale[:, None, :]) + shift[:, None, :]
        p = h @ w_in[l]
        qa, ka, va, za, qr, kr, vr, zr, gl = jnp.split(p, IN_SPLITS, axis=-1)
        ya = gqa_attention(qa.reshape(B, S, ATTN_HEADS, ATTN_HEAD_DIM),
                           ka.reshape(B, S, ATTN_KV_HEADS, ATTN_HEAD_DIM),
                           va.reshape(B, S, ATTN_KV_HEADS, ATTN_HEAD_DIM),
                           qn_g[l], kn_g[l], tables) * jax.nn.silu(za)
        yr = bidirectional_retention(qr.reshape(B, S, RET_HEADS, RET_KEY_DIM),
                                     kr.reshape(B, S, RET_HEADS, RET_KEY_DIM),
                                     vr.reshape(B, S, RET_HEADS, RET_VALUE_DIM),
                                     w_dec_f[l], w_dec_b[l], gn_g[l], tables) * jax.nn.silu(zr)
        g_att, g_ret = jnp.split(jax.nn.sigmoid(gl), 2, axis=-1)
        merged = g_att * (ya @ w_pa[l]) + g_ret * (yr @ w_pr[l])
        y = rmsnorm(merged @ w_out[l], g_post[l])
        x = x + gate[:, None, :] * y
    return x


import jax as _jax
import jax.numpy as _jnp

TWIN_FORMAT = 'train_step'
FWD_PARAMS = ['x', 'c', 'w_ada', 'b_ada', 'g_pre', 'w_in', 'qn_g', 'kn_g', 'w_dec_f', 'w_dec_b', 'gn_g', 'w_pa', 'w_pr', 'w_out', 'g_post']
TWIN_WEIGHTS = ['w_ada', 'b_ada', 'g_pre', 'w_in', 'qn_g', 'kn_g', 'w_dec_f', 'w_dec_b', 'gn_g', 'w_pa', 'w_pr', 'w_out', 'g_post']
TWIN_DIFF_INPUT = 'x'
TWIN_INPUTS = ['x', 'c', 'w_ada', 'b_ada', 'g_pre', 'w_in', 'qn_g', 'kn_g', 'w_dec_f', 'w_dec_b', 'gn_g', 'w_pa', 'w_pr', 'w_out', 'g_post', 'loss_target', 'm_w_ada', 'm_b_ada', 'm_g_pre', 'm_w_in', 'm_qn_g', 'm_kn_g', 'm_w_dec_f', 'm_w_dec_b', 'm_gn_g', 'm_w_pa', 'm_w_pr', 'm_w_out', 'm_g_post', 'v_w_ada', 'v_b_ada', 'v_g_pre', 'v_w_in', 'v_qn_g', 'v_kn_g', 'v_w_dec_f', 'v_w_dec_b', 'v_gn_g', 'v_w_pa', 'v_w_pr', 'v_w_out', 'v_g_post']
TWIN_OUTPUTS = ['loss', 'grad_x', 'grad_w_ada', 'grad_b_ada', 'grad_g_pre', 'grad_w_in', 'grad_qn_g', 'grad_kn_g', 'grad_w_dec_f', 'grad_w_dec_b', 'grad_gn_g', 'grad_w_pa', 'grad_w_pr', 'grad_w_out', 'grad_g_post', 'delta_w_ada', 'delta_b_ada', 'delta_g_pre', 'delta_w_in', 'delta_qn_g', 'delta_kn_g', 'delta_w_dec_f', 'delta_w_dec_b', 'delta_gn_g', 'delta_w_pa', 'delta_w_pr', 'delta_w_out', 'delta_g_post', 'new_m_w_ada', 'new_m_b_ada', 'new_m_g_pre', 'new_m_w_in', 'new_m_qn_g', 'new_m_kn_g', 'new_m_w_dec_f', 'new_m_w_dec_b', 'new_m_gn_g', 'new_m_w_pa', 'new_m_w_pr', 'new_m_w_out', 'new_m_g_post', 'new_v_w_ada', 'new_v_b_ada', 'new_v_g_pre', 'new_v_w_in', 'new_v_qn_g', 'new_v_kn_g', 'new_v_w_dec_f', 'new_v_w_dec_b', 'new_v_gn_g', 'new_v_w_pa', 'new_v_w_pr', 'new_v_w_out', 'new_v_g_post']
TWIN_LEAF_KINDS = {'loss': 'loss', 'grad_x': 'grad_x', 'grad_w_ada': 'grad_w', 'grad_b_ada': 'grad_w', 'grad_g_pre': 'grad_w', 'grad_w_in': 'grad_w', 'grad_qn_g': 'grad_w', 'grad_kn_g': 'grad_w', 'grad_w_dec_f': 'grad_w', 'grad_w_dec_b': 'grad_w', 'grad_gn_g': 'grad_w', 'grad_w_pa': 'grad_w', 'grad_w_pr': 'grad_w', 'grad_w_out': 'grad_w', 'grad_g_post': 'grad_w', 'delta_w_ada': 'delta_w', 'delta_b_ada': 'delta_w', 'delta_g_pre': 'delta_w', 'delta_w_in': 'delta_w', 'delta_qn_g': 'delta_w', 'delta_kn_g': 'delta_w', 'delta_w_dec_f': 'delta_w', 'delta_w_dec_b': 'delta_w', 'delta_gn_g': 'delta_w', 'delta_w_pa': 'delta_w', 'delta_w_pr': 'delta_w', 'delta_w_out': 'delta_w', 'delta_g_post': 'delta_w', 'new_m_w_ada': 'new_m', 'new_m_b_ada': 'new_m', 'new_m_g_pre': 'new_m', 'new_m_w_in': 'new_m', 'new_m_qn_g': 'new_m', 'new_m_kn_g': 'new_m', 'new_m_w_dec_f': 'new_m', 'new_m_w_dec_b': 'new_m', 'new_m_gn_g': 'new_m', 'new_m_w_pa': 'new_m', 'new_m_w_pr': 'new_m', 'new_m_w_out': 'new_m', 'new_m_g_post': 'new_m', 'new_v_w_ada': 'new_v', 'new_v_b_ada': 'new_v', 'new_v_g_pre': 'new_v', 'new_v_w_in': 'new_v', 'new_v_qn_g': 'new_v', 'new_v_kn_g': 'new_v', 'new_v_w_dec_f': 'new_v', 'new_v_w_dec_b': 'new_v', 'new_v_gn_g': 'new_v', 'new_v_w_pa': 'new_v', 'new_v_w_pr': 'new_v', 'new_v_w_out': 'new_v', 'new_v_g_post': 'new_v'}


def _forward(args):
    return _fwd_reference(*[args[k] for k in FWD_PARAMS])


def _output_shape():
    out = _jax.eval_shape(lambda: _forward(_fwd_setup_inputs(0)))
    return out.shape, out.dtype

N_MICROBATCH = 1
ADAM_LR = 0.001
ADAM_B1 = 0.9
ADAM_B2 = 0.999
ADAM_EPS = 1e-08
ADAM_WD = 0.01
ADAM_STEP = 10
PER_EXAMPLE_BATCH_AXIS = {'x': 0, 'c': 0, 'loss_target': 0}
SHARED_INPUTS = []
_WEIGHT_DTYPES = {'w_ada': _jnp.float32, 'b_ada': _jnp.float32, 'g_pre': _jnp.float32, 'w_in': _jnp.float32, 'qn_g': _jnp.float32, 'kn_g': _jnp.float32, 'w_dec_f': _jnp.float32, 'w_dec_b': _jnp.float32, 'gn_g': _jnp.float32, 'w_pa': _jnp.float32, 'w_pr': _jnp.float32, 'w_out': _jnp.float32, 'g_post': _jnp.float32}
MOMENT_SCALE = {'w_ada': 1.691048e+00, 'b_ada': 3.154428e+00, 'g_pre': 1.715540e-01, 'w_in': 9.065108e-02, 'qn_g': 3.154976e-02, 'kn_g': 3.117440e-02, 'w_dec_f': 1.835307e-01, 'w_dec_b': 1.354543e+00, 'gn_g': 1.849300e-01, 'w_pa': 3.725098e-02, 'w_pr': 1.066410e-01, 'w_out': 1.201527e-01, 'g_post': 3.714420e+00}


def _to_microbatches(a, axis):
    t = _jnp.moveaxis(a, axis, 0)
    t = t.reshape((N_MICROBATCH, t.shape[0] // N_MICROBATCH) + t.shape[1:])
    return _jnp.moveaxis(t, 1, axis + 1)


def setup_inputs(seed: int = 0) -> dict:
    inp = _fwd_setup_inputs(seed)
    key = _jax.random.fold_in(_jax.random.key(seed), 7919)
    shape, _ = _output_shape()
    out = dict(inp)
    out["loss_target"] = _jax.random.normal(_jax.random.fold_in(key, 0), shape, _jnp.float32)
    for i, name in enumerate(TWIN_WEIGHTS):
        w = inp[name].astype(_jnp.float32)
        if MOMENT_SCALE is None:
            s = _jnp.sqrt(_jnp.mean(_jnp.square(w)) + 1e-30)
        else:
            s = MOMENT_SCALE[name]
        km, kv = _jax.random.split(_jax.random.fold_in(key, i + 1))
        out[name] = w
        out["m_" + name] = s * _jax.random.normal(km, w.shape, _jnp.float32)
        out["v_" + name] = (s * s) * _jax.random.uniform(kv, w.shape, _jnp.float32, 0.5, 1.5)
    if N_MICROBATCH > 1:
        for name, axis in PER_EXAMPLE_BATCH_AXIS.items():
            out[name] = _to_microbatches(out[name], axis)
    return {'x': out['x'], 'c': out['c'], 'w_ada': out['w_ada'], 'b_ada': out['b_ada'], 'g_pre': out['g_pre'], 'w_in': out['w_in'], 'qn_g': out['qn_g'], 'kn_g': out['kn_g'], 'w_dec_f': out['w_dec_f'], 'w_dec_b': out['w_dec_b'], 'gn_g': out['gn_g'], 'w_pa': out['w_pa'], 'w_pr': out['w_pr'], 'w_out': out['w_out'], 'g_post': out['g_post'], 'loss_target': out['loss_target'], 'm_w_ada': out['m_w_ada'], 'm_b_ada': out['m_b_ada'], 'm_g_pre': out['m_g_pre'], 'm_w_in': out['m_w_in'], 'm_qn_g': out['m_qn_g'], 'm_kn_g': out['m_kn_g'], 'm_w_dec_f': out['m_w_dec_f'], 'm_w_dec_b': out['m_w_dec_b'], 'm_gn_g': out['m_gn_g'], 'm_w_pa': out['m_w_pa'], 'm_w_pr': out['m_w_pr'], 'm_w_out': out['m_w_out'], 'm_g_post': out['m_g_post'], 'v_w_ada': out['v_w_ada'], 'v_b_ada': out['v_b_ada'], 'v_g_pre': out['v_g_pre'], 'v_w_in': out['v_w_in'], 'v_qn_g': out['v_qn_g'], 'v_kn_g': out['v_kn_g'], 'v_w_dec_f': out['v_w_dec_f'], 'v_w_dec_b': out['v_w_dec_b'], 'v_gn_g': out['v_gn_g'], 'v_w_pa': out['v_w_pa'], 'v_w_pr': out['v_w_pr'], 'v_w_out': out['v_w_out'], 'v_g_post': out['v_g_post']}


def _loss(weights, diff, rest, loss_target):
    with _jax.named_scope("forward"):
        args = {**rest, TWIN_DIFF_INPUT: diff, **{k: w.astype(_WEIGHT_DTYPES[k]) for k, w in weights.items()}}
        y = _forward(args)
    with _jax.named_scope("loss_head"):
        err = _jnp.square(y.astype(_jnp.float32) - loss_target)
        return 0.5 * _jnp.sum(_jnp.mean(err, axis=-1)) if err.ndim else 0.5 * err


def _adamw(w, g, m, v):
    m = ADAM_B1 * m + (1.0 - ADAM_B1) * g
    v = ADAM_B2 * v + (1.0 - ADAM_B2) * _jnp.square(g)
    m_hat = m / (1.0 - ADAM_B1 ** ADAM_STEP)
    v_hat = v / (1.0 - ADAM_B2 ** ADAM_STEP)
    delta = -ADAM_LR * (m_hat / (_jnp.sqrt(v_hat) + ADAM_EPS) + ADAM_WD * w)
    return delta, m, v


def reference(x, c, w_ada, b_ada, g_pre, w_in, qn_g, kn_g, w_dec_f, w_dec_b, gn_g, w_pa, w_pr, w_out, g_post, loss_target, m_w_ada, m_b_ada, m_g_pre, m_w_in, m_qn_g, m_kn_g, m_w_dec_f, m_w_dec_b, m_gn_g, m_w_pa, m_w_pr, m_w_out, m_g_post, v_w_ada, v_b_ada, v_g_pre, v_w_in, v_qn_g, v_kn_g, v_w_dec_f, v_w_dec_b, v_gn_g, v_w_pa, v_w_pr, v_w_out, v_g_post):
    given = dict(x=x, c=c, w_ada=w_ada, b_ada=b_ada, g_pre=g_pre, w_in=w_in, qn_g=qn_g, kn_g=kn_g, w_dec_f=w_dec_f, w_dec_b=w_dec_b, gn_g=gn_g, w_pa=w_pa, w_pr=w_pr, w_out=w_out, g_post=g_post, loss_target=loss_target, m_w_ada=m_w_ada, m_b_ada=m_b_ada, m_g_pre=m_g_pre, m_w_in=m_w_in, m_qn_g=m_qn_g, m_kn_g=m_kn_g, m_w_dec_f=m_w_dec_f, m_w_dec_b=m_w_dec_b, m_gn_g=m_gn_g, m_w_pa=m_w_pa, m_w_pr=m_w_pr, m_w_out=m_w_out, m_g_post=m_g_post, v_w_ada=v_w_ada, v_b_ada=v_b_ada, v_g_pre=v_g_pre, v_w_in=v_w_in, v_qn_g=v_qn_g, v_kn_g=v_kn_g, v_w_dec_f=v_w_dec_f, v_w_dec_b=v_w_dec_b, v_gn_g=v_gn_g, v_w_pa=v_w_pa, v_w_pr=v_w_pr, v_w_out=v_w_out, v_g_post=v_g_post)
    weights = {n: given[n] for n in TWIN_WEIGHTS}
    shared = {n: given[n] for n in SHARED_INPUTS}
    per_example = {n: given[n] for n in ['x', 'c']}
    grad_fn = _jax.value_and_grad(_loss, argnums=(0, 1))

    def one_microbatch(ex, loss_target):
        ex = dict(ex)
        diff = ex.pop(TWIN_DIFF_INPUT)
        return grad_fn(weights, diff, {**shared, **ex}, loss_target)

    if N_MICROBATCH == 1:
        loss, (grad_w, grad_x) = one_microbatch(per_example, given["loss_target"])
    else:
        def body(carry, xs):
            loss_sum, grad_sum = carry
            l_k, (gw_k, gx_k) = one_microbatch(xs[0], xs[1])
            with _jax.named_scope("update"):
                return (loss_sum + l_k, _jax.tree.map(_jnp.add, grad_sum, gw_k)), gx_k

        init = (_jnp.zeros((), _jnp.float32), _jax.tree.map(_jnp.zeros_like, weights))
        (loss, grad_w), grad_x = _jax.lax.scan(body, init, (per_example, given["loss_target"]))
    with _jax.named_scope("update"):
        delta_w, new_m, new_v = {}, {}, {}
        for n in TWIN_WEIGHTS:
            delta_w[n], new_m[n], new_v[n] = _adamw(weights[n], grad_w[n], given["m_" + n], given["v_" + n])
    return (loss, grad_x, *[grad_w[n] for n in TWIN_WEIGHTS], *[delta_w[n] for n in TWIN_WEIGHTS],
            *[new_m[n] for n in TWIN_WEIGHTS], *[new_v[n] for n in TWIN_WEIGHTS])
```

```python
import functools

import jax
import jax.numpy as jnp
from jax import lax
from jax.experimental import pallas as pl
from jax.experimental.pallas import tpu as pltpu

F32, BF16 = jnp.float32, jnp.bfloat16
D = 1024
DH = 64
DV = 128
HR = 4
CH = 128
EPS = 1e-6
ROPE_THETA = 10000.0
NDEV = 8
O_GL, O_ZA, O_QA, O_KA, O_VA, O_QR, O_ZR, O_VR, O_KR, P_W = 0, 2048, 2560, 3072, 3200, 3328, 3584, 4096, 4608, 4864
ORIG = dict(qa=(0, 512), ka=(512, 640), va=(640, 768), za=(768, 1280), qr=(1280, 1536), kr=(1536, 1792),
            vr=(1792, 2304), zr=(2304, 2816), gl=(2816, 4864))
P_ORDER = ("gl", "za", "qa", "ka", "va", "qr", "zr", "vr", "kr")
ADAM_LR, ADAM_B1, ADAM_B2, ADAM_EPS, ADAM_WD, ADAM_STEP = 0.001, 0.9, 0.999, 1e-08, 0.01, 10
VMEM_BIG = 56 * 1024 * 1024
MESH = pl.DeviceIdType.MESH

NT = (((1,), (1,)), ((), ()))
TN = (((0,), (0,)), ((), ()))


def _dot(a, b, dims=None):
    if dims is None:
        return jnp.dot(a, b, preferred_element_type=F32)
    return lax.dot_general(a, b, dims, preferred_element_type=F32)


def _cp(sem=None, vmem=None):
    kw = {}
    if sem is not None:
        kw["dimension_semantics"] = sem
    if vmem is not None:
        kw["vmem_limit_bytes"] = vmem
    return pltpu.CompilerParams(**kw)


def _sigmoid(z):
    return 1.0 / (1.0 + jnp.exp(-z))


def _sum11(m):
    return jnp.sum(jnp.sum(m, axis=-1, keepdims=True), axis=0, keepdims=True)


def _full(shape):
    n = len(shape)
    return pl.BlockSpec(shape, lambda *_: (0,) * n)


def _my_pos():
    return lax.axis_index("x"), lax.axis_index("y"), lax.axis_index("c")


def _peer(k, x, y, c):
    return ((1 - x) if k & 4 else x, (1 - y) if k & 2 else y, (1 - c) if k & 1 else c)


def _small_allgather(v, name):
    R, L = v.shape

    def body(v_ref, out_ref, send_sems, recv_sems):
        x, y, c = _my_pos()
        me = 4 * x + 2 * y + c
        out_ref[me] = v_ref[...]
        cps = []
        for k in range(1, NDEV):
            cp = pltpu.make_async_remote_copy(src_ref=v_ref, dst_ref=out_ref.at[me], send_sem=send_sems.at[k - 1],
                                              recv_sem=recv_sems.at[k - 1], device_id=_peer(k, x, y, c), device_id_type=MESH)
            cp.start()
            cps.append(cp)
        for cp in cps:
            cp.wait()

    return pl.pallas_call(
        body, name=name, out_shape=jax.ShapeDtypeStruct((NDEV, R, L), v.dtype),
        in_specs=[pl.BlockSpec(memory_space=pltpu.VMEM)], out_specs=pl.BlockSpec(memory_space=pltpu.VMEM),
        scratch_shapes=[pltpu.SemaphoreType.DMA((NDEV - 1,)), pltpu.SemaphoreType.DMA((NDEV - 1,))],
    )(v)


def _allgather_hbm(arrs, name):
    n = len(arrs)

    def body(*refs):
        ins, outs = refs[:n], refs[n:2 * n]
        send_sems, recv_sems, local_sems = refs[2 * n:]
        x, y, c = _my_pos()
        me = 4 * x + 2 * y + c
        cps = []
        for a in range(n):
            lc = pltpu.make_async_copy(ins[a], outs[a].at[me], local_sems.at[a])
            lc.start()
            cps.append(lc)
            for k in range(1, NDEV):
                cp = pltpu.make_async_remote_copy(src_ref=ins[a], dst_ref=outs[a].at[me], send_sem=send_sems.at[a, k - 1],
                                                  recv_sem=recv_sems.at[a, k - 1], device_id=_peer(k, x, y, c), device_id_type=MESH)
                cp.start()
                cps.append(cp)
        for cp in cps:
            cp.wait()

    return pl.pallas_call(
        body, name=name, out_shape=[jax.ShapeDtypeStruct((NDEV,) + a.shape, a.dtype) for a in arrs],
        in_specs=[pl.BlockSpec(memory_space=pl.ANY)] * n, out_specs=[pl.BlockSpec(memory_space=pl.ANY)] * n,
        scratch_shapes=[pltpu.SemaphoreType.DMA((n, NDEV - 1)), pltpu.SemaphoreType.DMA((n, NDEV - 1)), pltpu.SemaphoreType.DMA((n,))],
    )(*arrs)


def _all_to_all_hbm(parts, name):
    n = len(parts)

    def body(*refs):
        ins, outs = refs[:n], refs[n:2 * n]
        send_sems, recv_sems, local_sems = refs[2 * n:]
        x, y, c = _my_pos()
        me = 4 * x + 2 * y + c
        cps = []
        for a in range(n):
            lc = pltpu.make_async_copy(ins[a].at[me], outs[a].at[me], local_sems.at[a])
            lc.start()
            cps.append(lc)
            for k in range(1, NDEV):
                px, py, pc = _peer(k, x, y, c)
                cp = pltpu.make_async_remote_copy(src_ref=ins[a].at[4 * px + 2 * py + pc], dst_ref=outs[a].at[me],
                                                  send_sem=send_sems.at[a, k - 1], recv_sem=recv_sems.at[a, k - 1],
                                                  device_id=(px, py, pc), device_id_type=MESH)
                cp.start()
                cps.append(cp)
        for cp in cps:
            cp.wait()

    return pl.pallas_call(
        body, name=name, out_shape=[jax.ShapeDtypeStruct(a.shape, a.dtype) for a in parts],
        in_specs=[pl.BlockSpec(memory_space=pl.ANY)] * n, out_specs=[pl.BlockSpec(memory_space=pl.ANY)] * n,
        scratch_shapes=[pltpu.SemaphoreType.DMA((n, NDEV - 1)), pltpu.SemaphoreType.DMA((n, NDEV - 1)), pltpu.SemaphoreType.DMA((n,))],
    )(*parts)


def _mm_tn(a, b, name):
    S, M = a.shape
    N = b.shape[1]
    tk = min(512, S)
    tn = N if N <= 768 else (640 if N % 640 == 0 else 512)
    nk = S // tk

    def body(a_ref, b_ref, o_ref):
        @pl.when(pl.program_id(1) == 0)
        def _():
            o_ref[...] = jnp.zeros_like(o_ref)
        o_ref[...] += _dot(a_ref[...], b_ref[...], TN)

    return pl.pallas_call(
        body, name=name, out_shape=jax.ShapeDtypeStruct((M, N), F32), grid=(N // tn, nk),
        in_specs=[pl.BlockSpec((tk, M), lambda j, k: (k, 0)), pl.BlockSpec((tk, tn), lambda j, k: (k, j))],
        out_specs=pl.BlockSpec((M, tn), lambda j, k: (0, j)),
        compiler_params=_cp(("parallel", "arbitrary"), VMEM_BIG),
    )(a, b)


def _mod_shard(c_pad, w_ada_s, b_ada_s):
    def body(c_ref, w_ref, b_ref, o_ref, ca_ref):
        cv = c_ref[...]
        ca = (cv * _sigmoid(cv)).astype(BF16)
        ca_ref[...] = ca
        o_ref[...] = _dot(ca, w_ref[...].astype(BF16)) + b_ref[...]

    return pl.pallas_call(
        body, name="mod_shard", out_shape=[jax.ShapeDtypeStruct((16, w_ada_s.shape[1]), F32), jax.ShapeDtypeStruct((16, D), BF16)],
    )(c_pad, w_ada_s, b_ada_s)


def _fwd_in(x, mod, g_pre, w_p):
    S = x.shape[0]
    tm, tn = min(512, S), P_W // 2

    def body(x_ref, mod_ref, g_ref, w_ref, p_ref, h_ref):
        @pl.when(pl.program_id(1) == 0)
        def _():
            xv = x_ref[...]
            r = lax.rsqrt(jnp.mean(xv * xv, axis=-1, keepdims=True) + EPS)
            h = ((xv * r) * g_ref[...]) * (1.0 + mod_ref[1:2, :]) + mod_ref[0:1, :]
            h_ref[...] = h.astype(BF16)
        p_ref[...] = _dot(h_ref[...], w_ref[...])

    return pl.pallas_call(
        body, name="fwd_in", out_shape=[jax.ShapeDtypeStruct((S, P_W), F32), jax.ShapeDtypeStruct((S, D), BF16)],
        grid=(S // tm, P_W // tn),
        in_specs=[pl.BlockSpec((tm, D), lambda i, j: (i, 0)), _full((3, D)), _full((1, D)), pl.BlockSpec((D, tn), lambda i, j: (0, j))],
        out_specs=[pl.BlockSpec((tm, tn), lambda i, j: (i, j)), pl.BlockSpec((tm, D), lambda i, j: (i, 0))],
        compiler_params=_cp(("parallel", "arbitrary"), VMEM_BIG),
    )(x, mod, g_pre, w_p)


def _swap16(v):
    lane = lax.broadcasted_iota(jnp.int32, v.shape, 1)
    return jnp.where((lane % 32) < 16, pltpu.roll(v, 112, 1), pltpu.roll(v, 16, 1))


def _rope(v, cos, sin):
    return v * cos + _swap16(v) * sin


def _rope_t(v, cos, sin):
    return v * cos - _swap16(v) * sin


def _head_mean(v):
    lo = lax.broadcasted_iota(jnp.int32, v.shape, 1) < 64
    m0 = jnp.sum(jnp.where(lo, v, 0.0), axis=-1, keepdims=True)
    m1 = jnp.sum(jnp.where(lo, 0.0, v), axis=-1, keepdims=True)
    return jnp.where(lo, m0, m1) * (1.0 / 64.0)


def _prep(p, cos, sin, qg, kg):
    S = p.shape[0]
    tm = min(512, S)

    def body(qa_ref, kv_ref, qr_ref, kr_ref, cos_ref, sin_ref, qg_ref, kg_ref, qh_ref, kh_ref, vh_ref, qr2_ref, kr2_ref):
        cos_v, sin_v = cos_ref[...], sin_ref[...]
        for g in range(4):
            xv = qa_ref[:, 128 * g:128 * g + 128]
            r = lax.rsqrt(_head_mean(xv * xv) + EPS)
            yv = _rope((xv * r) * qg_ref[...], cos_v, sin_v) * 0.125
            qh_ref[2 * g] = yv[:, :64].astype(BF16)
            qh_ref[2 * g + 1] = yv[:, 64:].astype(BF16)
        xv = kv_ref[:, :128]
        r = lax.rsqrt(_head_mean(xv * xv) + EPS)
        yv = _rope((xv * r) * kg_ref[...], cos_v, sin_v)
        kh_ref[0] = yv[:, :64].astype(BF16)
        kh_ref[1] = yv[:, 64:].astype(BF16)
        vv = kv_ref[:, 128:]
        vh_ref[0] = vv[:, :64].astype(BF16)
        vh_ref[1] = vv[:, 64:].astype(BF16)
        for g in range(2):
            sl = slice(128 * g, 128 * g + 128)
            qr2_ref[:, sl] = _rope(qr_ref[:, sl], cos_v, sin_v)
            kr2_ref[:, sl] = _rope(kr_ref[:, sl], cos_v, sin_v) * 0.125

    return pl.pallas_call(
        body, name="prep",
        out_shape=[jax.ShapeDtypeStruct((8, S, DH), BF16), jax.ShapeDtypeStruct((2, S, DH), BF16), jax.ShapeDtypeStruct((2, S, DH), BF16),
                   jax.ShapeDtypeStruct((S, 256), F32), jax.ShapeDtypeStruct((S, 256), F32)],
        grid=(S // tm,),
        in_specs=[pl.BlockSpec((tm, 512), lambda i: (i, O_QA // 512)), pl.BlockSpec((tm, 256), lambda i: (i, O_KA // 256)),
                  pl.BlockSpec((tm, 256), lambda i: (i, O_QR // 256)), pl.BlockSpec((tm, 256), lambda i: (i, O_KR // 256)),
                  pl.BlockSpec((tm, 128), lambda i: (i, 0)), pl.BlockSpec((tm, 128), lambda i: (i, 0)), _full((1, 128)), _full((1, 128))],
        out_specs=[pl.BlockSpec((8, tm, DH), lambda i: (0, i, 0)), pl.BlockSpec((2, tm, DH), lambda i: (0, i, 0)),
                   pl.BlockSpec((2, tm, DH), lambda i: (0, i, 0)), pl.BlockSpec((tm, 256), lambda i: (i, 0)), pl.BlockSpec((tm, 256), lambda i: (i, 0))],
        compiler_params=_cp(("parallel",)),
    )(p, p, p, p, cos, sin, qg, kg)


def _attn_fwd(qh, kh, vh):
    S = qh.shape[1]
    tq, tk = min(256, S), min(512, S)
    nj = S // tk

    def body(q_ref, k_ref, v_ref, o_ref, lse_ref, m_s, l_s, acc_s):
        j = pl.program_id(2)

        @pl.when(j == 0)
        def _():
            m_s[...] = jnp.full_like(m_s, -jnp.inf)
            l_s[...] = jnp.zeros_like(l_s)
            acc_s[...] = jnp.zeros_like(acc_s)

        q = q_ref[...].reshape(4 * tq, DH)
        s = _dot(q, k_ref[0], NT)
        m_new = jnp.maximum(m_s[...], jnp.max(s, axis=-1, keepdims=True))
        alpha = jnp.exp(m_s[...] - m_new)
        pv = jnp.exp(s - m_new)
        l_s[...] = alpha * l_s[...] + jnp.sum(pv, axis=-1, keepdims=True)
        acc_s[...] = alpha * acc_s[...] + _dot(pv.astype(BF16), v_ref[0])
        m_s[...] = m_new

        @pl.when(j == nj - 1)
        def _():
            ov = acc_s[...] / l_s[...]
            for h in range(4):
                o_ref[:, DH * h:DH * h + DH] = ov[h * tq:(h + 1) * tq]
            lse_ref[...] = (m_s[...] + jnp.log(l_s[...])).reshape(4, tq, 1)

    return pl.pallas_call(
        body, name="attn_fwd", out_shape=[jax.ShapeDtypeStruct((S, 512), F32), jax.ShapeDtypeStruct((8, S, 1), F32)],
        grid=(2, S // tq, nj),
        in_specs=[pl.BlockSpec((4, tq, DH), lambda g, i, j: (g, i, 0)), pl.BlockSpec((1, tk, DH), lambda g, i, j: (g, j, 0)),
                  pl.BlockSpec((1, tk, DH), lambda g, i, j: (g, j, 0))],
        out_specs=[pl.BlockSpec((tq, 256), lambda g, i, j: (i, g)), pl.BlockSpec((4, tq, 1), lambda g, i, j: (g, i, 0))],
        scratch_shapes=[pltpu.VMEM((4 * tq, 1), F32), pltpu.VMEM((4 * tq, 1), F32), pltpu.VMEM((4 * tq, DH), F32)],
        compiler_params=_cp(("parallel", "parallel", "arbitrary"), VMEM_BIG),
    )(qh, kh, vh)


def _ret_tables(wf, wb):
    C = CH

    def body(wf_ref, wb_ref, dc_ref, qdf_ref, qdb_ref, kdf_ref, kdb_ref, a_ref):
        def logsig(w):
            z = jnp.exp(-jnp.abs(w))
            u = 1.0 + z
            l1p = jnp.where(u == 1.0, z, jnp.log(u) * (z / jnp.where(u == 1.0, 1.0, u - 1.0)))
            return jnp.minimum(w, 0.0) - l1p

        lgf, lgb = logsig(wf_ref[...]), logsig(wb_ref[...])
        lane4 = lax.broadcasted_iota(jnp.int32, (1, 4), 1)

        def pick(lg, h):
            return jnp.sum(jnp.where(lane4 == h, lg, 0.0), axis=-1, keepdims=True)

        ii = lax.broadcasted_iota(jnp.int32, (C, C), 0).astype(F32)
        jj = lax.broadcasted_iota(jnp.int32, (C, C), 1).astype(F32)
        dif = ii - jj
        hd = lax.broadcasted_iota(jnp.int32, (C, 256), 1) // DH
        lf_l = jnp.zeros((C, 256), F32)
        lb_l = jnp.zeros((C, 256), F32)
        for h in range(HR):
            lf, lb = pick(lgf, h), pick(lgb, h)
            dc_ref[h] = jnp.where(dif >= 0, jnp.exp(lf * jnp.maximum(dif, 0.0)), jnp.exp(lb * jnp.maximum(-dif, 0.0)))
            lf_l = jnp.where(hd == h, lf, lf_l)
            lb_l = jnp.where(hd == h, lb, lb_l)
            a_ref[h:h + 1, :] = jnp.broadcast_to(jnp.exp(lf * C), (1, 128))
            a_ref[HR + h:HR + h + 1, :] = jnp.broadcast_to(jnp.exp(lb * C), (1, 128))
        ri = lax.broadcasted_iota(jnp.int32, (C, 256), 0).astype(F32)
        qdf_ref[...] = jnp.exp(lf_l * (ri + 1.0))
        qdb_ref[...] = jnp.exp(lb_l * (C - ri))
        kdf_ref[...] = jnp.exp(lf_l * (C - 1.0 - ri))
        kdb_ref[...] = jnp.exp(lb_l * ri)

    t = jax.ShapeDtypeStruct((C, 256), F32)
    return pl.pallas_call(body, name="ret_tables",
                          out_shape=[jax.ShapeDtypeStruct((HR, C, C), F32), t, t, t, t, jax.ShapeDtypeStruct((8, 128), F32)])(wf, wb)


def _ret_states(kr2, p, kdf, kdb, adec):
    S = kr2.shape[0]
    C, N = CH, S // CH

    def body(kf_ref, vf_ref, kb_ref, vb_ref, kdf_ref, kdb_ref, a_ref, rf_ref, rb_ref, sf, sb):
        @pl.when(pl.program_id(0) == 0)
        def _():
            sf[...] = jnp.zeros_like(sf)
            sb[...] = jnp.zeros_like(sb)

        rf_ref[0] = sf[...]
        rb_ref[0] = sb[...]
        kdfw = (kf_ref[...] * kdf_ref[...]).astype(BF16)
        kdbw = (kb_ref[...] * kdb_ref[...]).astype(BF16)
        vf, vb = vf_ref[...].astype(BF16), vb_ref[...].astype(BF16)
        for h in range(HR):
            ks, vs = slice(DH * h, DH * h + DH), slice(DV * h, DV * h + DV)
            sf[h] = a_ref[h:h + 1, :] * sf[h] + _dot(kdfw[:, ks], vf[:, vs], TN)
            sb[h] = a_ref[HR + h:HR + h + 1, :] * sb[h] + _dot(kdbw[:, ks], vb[:, vs], TN)

    st = jax.ShapeDtypeStruct((N, HR, DH, DV), F32)
    return pl.pallas_call(
        body, name="ret_states", out_shape=[st, st], grid=(N,),
        in_specs=[pl.BlockSpec((C, 256), lambda t: (t, 0)), pl.BlockSpec((C, 512), lambda t: (t, O_VR // 512)),
                  pl.BlockSpec((C, 256), lambda t: (N - 1 - t, 0)), pl.BlockSpec((C, 512), lambda t: (N - 1 - t, O_VR // 512)),
                  _full((C, 256)), _full((C, 256)), _full((8, 128))],
        out_specs=[pl.BlockSpec((1, HR, DH, DV), lambda t: (t, 0, 0, 0)), pl.BlockSpec((1, HR, DH, DV), lambda t: (N - 1 - t, 0, 0, 0))],
        scratch_shapes=[pltpu.VMEM((HR, DH, DV), F32), pltpu.VMEM((HR, DH, DV), F32)],
        compiler_params=_cp(("arbitrary",)),
    )(kr2, p, kr2, p, kdf, kdb, adec)


def _ret_head_fwd(h, qb, kb, vb, qfw, qbw, dc_ref, rf_ref, rb_ref):
    ks, vs = slice(DH * h, DH * h + DH), slice(DV * h, DV * h + DV)
    sd = _dot(qb[:, ks], kb[:, ks], NT) * dc_ref[h]
    o = _dot(sd.astype(BF16), vb[:, vs]) + _dot(qfw[:, ks], rf_ref[0, h].astype(BF16)) + _dot(qbw[:, ks], rb_ref[0, h].astype(BF16))
    return sd, o


def _ret_out(qr2, kr2, p, rf, rb, dc, qdf, qdb, gn):
    S = qr2.shape[0]
    C, N = CH, S // CH

    def body(q_ref, k_ref, v_ref, z_ref, rf_ref, rb_ref, dc_ref, qdf_ref, qdb_ref, gn_ref, yr_ref):
        qv = q_ref[...]
        qb, kb, vb = qv.astype(BF16), k_ref[...].astype(BF16), v_ref[...].astype(BF16)
        qfw, qbw = (qv * qdf_ref[...]).astype(BF16), (qv * qdb_ref[...]).astype(BF16)
        for h in range(HR):
            vs = slice(DV * h, DV * h + DV)
            _, o = _ret_head_fwd(h, qb, kb, vb, qfw, qbw, dc_ref, rf_ref, rb_ref)
            mu = jnp.mean(o, axis=-1, keepdims=True)
            var = jnp.mean(jnp.square(o - mu), axis=-1, keepdims=True)
            on = (o - mu) * lax.rsqrt(var + EPS)
            z = z_ref[:, vs]
            yr_ref[:, vs] = ((on * gn_ref[:, vs]) * (z * _sigmoid(z))).astype(BF16)

    return pl.pallas_call(
        body, name="ret_out", out_shape=jax.ShapeDtypeStruct((S, 512), BF16), grid=(N,),
        in_specs=[pl.BlockSpec((C, 256), lambda t: (t, 0)), pl.BlockSpec((C, 256), lambda t: (t, 0)),
                  pl.BlockSpec((C, 512), lambda t: (t, O_VR // 512)), pl.BlockSpec((C, 512), lambda t: (t, O_ZR // 512)),
                  pl.BlockSpec((1, HR, DH, DV), lambda t: (t, 0, 0, 0)), pl.BlockSpec((1, HR, DH, DV), lambda t: (t, 0, 0, 0)),
                  _full((HR, C, C)), _full((C, 256)), _full((C, 256)), _full((1, 512))],
        out_specs=pl.BlockSpec((C, 512), lambda t: (t, 0)),
        compiler_params=_cp(("parallel",)),
    )(qr2, kr2, p, p, rf, rb, dc, qdf, qdb, gn)


def _mid(x, tgt, mod, g_post, o_att, p, yr, w_pa, w_pr, w_out):
    S = x.shape[0]
    tm = min(256, S)

    def body(x_ref, t_ref, mod_ref, gp_ref, o_ref, za_ref, gl_ref, yr_ref, wpa_ref, wpr_ref, wout_ref,
             dout_ref, do_ref, dpm_ref, dyr_ref, mb_ref, dub_ref, yab_ref, dab_ref, drb_ref, sums_ref):
        @pl.when(pl.program_id(0) == 0)
        def _():
            sums_ref[...] = jnp.zeros_like(sums_ref)

        za = za_ref[...]
        sa = _sigmoid(za)
        sil = za * sa
        ov = o_ref[...]
        ya_b = (ov * sil).astype(BF16)
        yr_b = yr_ref[...]
        av = _dot(ya_b, wpa_ref[...])
        rv = _dot(yr_b, wpr_ref[...])
        ga = _sigmoid(gl_ref[:, :D])
        gr = _sigmoid(gl_ref[:, D:])
        mb = (ga * av + gr * rv).astype(BF16)
        u = _dot(mb, wout_ref[...])
        r2 = lax.rsqrt(jnp.mean(u * u, axis=-1, keepdims=True) + EPS)
        un = u * r2
        gp = gp_ref[...]
        yv = un * gp
        gate = mod_ref[2:3, :]
        err = (x_ref[...] + gate * yv) - t_ref[...]
        dout = err * (1.0 / D)
        dout_ref[...] = dout
        dy = dout * gate
        sums_ref[0:1, :] += jnp.sum(dout * yv, axis=0, keepdims=True)
        sums_ref[1:2, :] += jnp.sum(dy * un, axis=0, keepdims=True)
        sums_ref[2:3, :] += jnp.sum(err * err, axis=0, keepdims=True)
        dyg = dy * gp
        du_b = (r2 * (dyg - un * jnp.mean(dyg * un, axis=-1, keepdims=True))).astype(BF16)
        dm = _dot(du_b, wout_ref[...], NT)
        da_b = (dm * ga).astype(BF16)
        dr_b = (dm * gr).astype(BF16)
        dpm_ref[:, :D] = (dm * av * (ga * (1.0 - ga))).astype(BF16)
        dpm_ref[:, D:2 * D] = (dm * rv * (gr * (1.0 - gr))).astype(BF16)
        dya = _dot(da_b, wpa_ref[...], NT)
        dyr_ref[...] = _dot(dr_b, wpr_ref[...], NT)
        do_ref[...] = (dya * sil).astype(BF16)
        dpm_ref[:, 2 * D:] = (dya * ov * (sa * (1.0 + za * (1.0 - sa)))).astype(BF16)
        mb_ref[...] = mb
        dub_ref[...] = du_b
        yab_ref[...] = ya_b
        dab_ref[...] = da_b
        drb_ref[...] = dr_b

    row = lambda w: pl.BlockSpec((tm, w), lambda i: (i, 0))
    sd = lambda w, dt: jax.ShapeDtypeStruct((S, w), dt)
    return pl.pallas_call(
        body, name="mid",
        out_shape=[sd(D, F32), sd(512, BF16), sd(2560, BF16), sd(512, F32), sd(D, BF16), sd(D, BF16), sd(512, BF16), sd(D, BF16), sd(D, BF16),
                   jax.ShapeDtypeStruct((8, D), F32)],
        grid=(S // tm,),
        in_specs=[row(D), row(D), _full((3, D)), _full((1, D)), row(512), pl.BlockSpec((tm, 512), lambda i: (i, O_ZA // 512)),
                  pl.BlockSpec((tm, 2048), lambda i: (i, 0)), row(512), _full((512, D)), _full((512, D)), _full((D, D))],
        out_specs=[row(D), row(512), row(2560), row(512), row(D), row(D), row(512), row(D), row(D), _full((8, D))],
        compiler_params=_cp(("arbitrary",), VMEM_BIG),
    )(x, tgt, mod, g_post, o_att, p, p, yr, w_pa, w_pr, w_out)


def _attn_bwd(qh, kh, vh, do, o_att, lse):
    S = qh.shape[1]
    tq, tk = min(256, S), min(512, S)

    def body(q_ref, k_ref, v_ref, do_ref, o_ref, lse_ref, dq_ref, dk_ref, dv_ref):
        j, i = pl.program_id(1), pl.program_id(2)
        rows = pl.ds(pl.multiple_of(i * tq, tq), tq)

        @pl.when(i == 0)
        def _():
            dk_ref[...] = jnp.zeros_like(dk_ref)
            dv_ref[...] = jnp.zeros_like(dv_ref)

        @pl.when(j == 0)
        def _():
            dq_ref[rows, :] = jnp.zeros((tq, 256), F32)

        q = q_ref[...].reshape(4 * tq, DH)
        k, v = k_ref[0], v_ref[0]
        dov = jnp.concatenate([do_ref[:, DH * h:DH * h + DH] for h in range(4)], axis=0)
        ov = jnp.concatenate([o_ref[:, DH * h:DH * h + DH] for h in range(4)], axis=0)
        delta = jnp.sum(dov.astype(F32) * ov, axis=-1, keepdims=True)
        pv = jnp.exp(_dot(q, k, NT) - lse_ref[...].reshape(4 * tq, 1))
        ds = (pv * (_dot(dov, v, NT) - delta)).astype(BF16)
        dv_ref[0] += _dot(pv.astype(BF16), dov, TN)
        dk_ref[0] += _dot(ds, q, TN)
        dqv = _dot(ds, k)
        for h in range(4):
            dq_ref[rows, DH * h:DH * h + DH] += dqv[h * tq:(h + 1) * tq]

    return pl.pallas_call(
        body, name="attn_bwd",
        out_shape=[jax.ShapeDtypeStruct((S, 512), F32), jax.ShapeDtypeStruct((2, S, DH), F32), jax.ShapeDtypeStruct((2, S, DH), F32)],
        grid=(2, S // tk, S // tq),
        in_specs=[pl.BlockSpec((4, tq, DH), lambda g, j, i: (g, i, 0)), pl.BlockSpec((1, tk, DH), lambda g, j, i: (g, j, 0)),
                  pl.BlockSpec((1, tk, DH), lambda g, j, i: (g, j, 0)), pl.BlockSpec((tq, 256), lambda g, j, i: (i, g)),
                  pl.BlockSpec((tq, 256), lambda g, j, i: (i, g)), pl.BlockSpec((4, tq, 1), lambda g, j, i: (g, i, 0))],
        out_specs=[pl.BlockSpec((S, 256), lambda g, j, i: (0, g)), pl.BlockSpec((1, tk, DH), lambda g, j, i: (g, j, 0)),
                   pl.BlockSpec((1, tk, DH), lambda g, j, i: (g, j, 0))],
        compiler_params=_cp(("arbitrary", "arbitrary", "arbitrary"), VMEM_BIG),
    )(qh, kh, vh, do, o_att, lse)


def _attn_prep_bwd(dq, dkh, dvh, p, cos, sin, qg, kg):
    S = dq.shape[0]
    tm = min(512, S)

    def body(dq_ref, dk_ref, dv_ref, qa_ref, ka_ref, cos_ref, sin_ref, qg_ref, kg_ref, dp_ref, gs_ref, ks):
        @pl.when(pl.program_id(0) == 0)
        def _():
            gs_ref[...] = jnp.zeros_like(gs_ref)

        cos_v, sin_v = cos_ref[...], sin_ref[...]

        def norm_bwd(dyv, xv, gv, row):
            r = lax.rsqrt(_head_mean(xv * xv) + EPS)
            xn = xv * r
            dxh = _rope_t(dyv, cos_v, sin_v)
            gs_ref[row:row + 1, :] += jnp.sum(dxh * xn, axis=0, keepdims=True)
            dg = dxh * gv
            return r * (dg - xn * _head_mean(dg * xn))

        for g in range(4):
            sl = slice(128 * g, 128 * g + 128)
            dp_ref[:, sl] = norm_bwd(dq_ref[:, sl] * 0.125, qa_ref[:, sl], qg_ref[...], 0).astype(BF16)
        ks[:, :DH] = dk_ref[0]
        ks[:, DH:] = dk_ref[1]
        dp_ref[:, 512:640] = norm_bwd(ks[...], ka_ref[...], kg_ref[...], 1).astype(BF16)
        dp_ref[:, 640:704] = dv_ref[0].astype(BF16)
        dp_ref[:, 704:768] = dv_ref[1].astype(BF16)

    return pl.pallas_call(
        body, name="attn_prep_bwd", out_shape=[jax.ShapeDtypeStruct((S, 768), BF16), jax.ShapeDtypeStruct((8, 128), F32)],
        grid=(S // tm,),
        in_specs=[pl.BlockSpec((tm, 512), lambda i: (i, 0)), pl.BlockSpec((2, tm, DH), lambda i: (0, i, 0)), pl.BlockSpec((2, tm, DH), lambda i: (0, i, 0)),
                  pl.BlockSpec((tm, 512), lambda i: (i, O_QA // 512)), pl.BlockSpec((tm, 128), lambda i: (i, O_KA // 128)),
                  pl.BlockSpec((tm, 128), lambda i: (i, 0)), pl.BlockSpec((tm, 128), lambda i: (i, 0)), _full((1, 128)), _full((1, 128))],
        out_specs=[pl.BlockSpec((tm, 768), lambda i: (i, 0)), _full((8, 128))],
        scratch_shapes=[pltpu.VMEM((tm, 128), F32)],
        compiler_params=_cp(("arbitrary",)),
    )(dq, dkh, dvh, p, p, cos, sin, qg, kg)


def _ret_bwd_chunk(qr2, kr2, p, rf, rb, dc, qdf, qdb, gn, dyr, cos, sin):
    S = qr2.shape[0]
    C, N = CH, S // CH

    def body(q_ref, k_ref, v_ref, z_ref, rf_ref, rb_ref, dc_ref, qdf_ref, qdb_ref, gn_ref, dyr_ref, cos_ref, sin_ref,
             dpa_ref, dk_ref, dv_ref, drf_ref, drb_ref, dgn_ref, dlg_ref, dqs):
        @pl.when(pl.program_id(0) == 0)
        def _():
            dgn_ref[...] = jnp.zeros_like(dgn_ref)
            dlg_ref[...] = jnp.zeros_like(dlg_ref)

        qv = q_ref[...]
        qb, kb, vb = qv.astype(BF16), k_ref[...].astype(BF16), v_ref[...].astype(BF16)
        qf32, qb32 = qv * qdf_ref[...], qv * qdb_ref[...]
        qfw, qbw = qf32.astype(BF16), qb32.astype(BF16)
        ii = lax.broadcasted_iota(jnp.int32, (C, C), 0).astype(F32)
        jj = lax.broadcasted_iota(jnp.int32, (C, C), 1).astype(F32)
        dif = ii - jj
        ri = lax.broadcasted_iota(jnp.int32, (C, 1), 0).astype(F32)
        for h in range(HR):
            ks, vs = slice(DH * h, DH * h + DH), slice(DV * h, DV * h + DV)
            sd, o = _ret_head_fwd(h, qb, kb, vb, qfw, qbw, dc_ref, rf_ref, rb_ref)
            mu = jnp.mean(o, axis=-1, keepdims=True)
            rstd = lax.rsqrt(jnp.mean(jnp.square(o - mu), axis=-1, keepdims=True) + EPS)
            on = (o - mu) * rstd
            z = z_ref[:, vs]
            sz = _sigmoid(z)
            dy = dyr_ref[:, vs]
            gnv = gn_ref[:, vs]
            dpa_ref[:, 256 + DV * h:256 + DV * h + DV] = (dy * (on * gnv) * (sz * (1.0 + z * (1.0 - sz)))).astype(BF16)
            dys = dy * (z * sz)
            dgn_ref[:, vs] += jnp.sum(dys * on, axis=0, keepdims=True)
            don = dys * gnv
            do = rstd * (don - jnp.mean(don, axis=-1, keepdims=True) - on * jnp.mean(don * on, axis=-1, keepdims=True))
            do_b = do.astype(BF16)
            dpm = _dot(do_b, vb[:, vs], NT)
            dv_ref[:, vs] = _dot(sd.astype(BF16), do_b, TN)
            dsd = (dpm * dc_ref[h]).astype(BF16)
            dqf = _dot(do_b, rf_ref[0, h].astype(BF16), NT)
            dqb = _dot(do_b, rb_ref[0, h].astype(BF16), NT)
            dqs[:, ks] = _dot(dsd, kb[:, ks]) + dqf * qdf_ref[:, ks] + dqb * qdb_ref[:, ks]
            dk_ref[:, ks] = _dot(dsd, qb[:, ks], TN)
            drf_ref[0, h] = _dot(qfw[:, ks], do_b, TN)
            drb_ref[0, h] = _dot(qbw[:, ks], do_b, TN)
            e = dpm * sd
            lf = _sum11(e * jnp.maximum(dif, 0.0)) + _sum11(jnp.sum(qf32[:, ks] * dqf, axis=-1, keepdims=True) * (ri + 1.0))
            lb = _sum11(e * jnp.maximum(-dif, 0.0)) + _sum11(jnp.sum(qb32[:, ks] * dqb, axis=-1, keepdims=True) * (C - ri))
            dlg_ref[h:h + 1, :] += jnp.broadcast_to(lf, (1, 128))
            dlg_ref[HR + h:HR + h + 1, :] += jnp.broadcast_to(lb, (1, 128))
        cos_v, sin_v = cos_ref[...], sin_ref[...]
        for g in range(2):
            sl = slice(128 * g, 128 * g + 128)
            dpa_ref[:, sl] = _rope_t(dqs[:, sl], cos_v, sin_v).astype(BF16)

    st = jax.ShapeDtypeStruct((N, HR, DH, DV), F32)
    stb = lambda: pl.BlockSpec((1, HR, DH, DV), lambda t: (t, 0, 0, 0))
    return pl.pallas_call(
        body, name="ret_bwd_chunk",
        out_shape=[jax.ShapeDtypeStruct((S, 768), BF16), jax.ShapeDtypeStruct((S, 256), F32), jax.ShapeDtypeStruct((S, 512), F32), st, st,
                   jax.ShapeDtypeStruct((1, 512), F32), jax.ShapeDtypeStruct((8, 128), F32)],
        grid=(N,),
        in_specs=[pl.BlockSpec((C, 256), lambda t: (t, 0)), pl.BlockSpec((C, 256), lambda t: (t, 0)),
                  pl.BlockSpec((C, 512), lambda t: (t, O_VR // 512)), pl.BlockSpec((C, 512), lambda t: (t, O_ZR // 512)),
                  stb(), stb(), _full((HR, C, C)), _full((C, 256)), _full((C, 256)), _full((1, 512)),
                  pl.BlockSpec((C, 512), lambda t: (t, 0)), pl.BlockSpec((C, 128), lambda t: (t, 0)), pl.BlockSpec((C, 128), lambda t: (t, 0))],
        out_specs=[pl.BlockSpec((C, 768), lambda t: (t, 0)), pl.BlockSpec((C, 256), lambda t: (t, 0)), pl.BlockSpec((C, 512), lambda t: (t, 0)),
                   stb(), stb(), _full((1, 512)), _full((8, 128))],
        scratch_shapes=[pltpu.VMEM((C, 256), F32)],
        compiler_params=_cp(("arbitrary",)),
    )(qr2, kr2, p, p, rf, rb, dc, qdf, qdb, gn, dyr, cos, sin)


def _ret_bwd_scan(kr2, p, rf, rb, drf, drb, kdf, kdb, adec):
    S = kr2.shape[0]
    C, N = CH, S // CH

    def body(kf_ref, vf_ref, kb_ref, vb_ref, rf_ref, rb_ref, drf_ref, drb_ref, kdf_ref, kdb_ref, a_ref,
             dkf_ref, dkb_ref, dvf_ref, dvb_ref, dlg_ref, gf, gb):
        @pl.when(pl.program_id(0) == 0)
        def _():
            gf[...] = jnp.zeros_like(gf)
            gb[...] = jnp.zeros_like(gb)
            dlg_ref[...] = jnp.zeros_like(dlg_ref)

        ri = lax.broadcasted_iota(jnp.int32, (C, 1), 0).astype(F32)

        def one(k_ref, v_ref, r_ref, dr_ref, kd_ref, g_s, dk_ref, dv_ref, row0, wexp):
            kd32 = k_ref[...] * kd_ref[...]
            kdw = kd32.astype(BF16)
            vb = v_ref[...].astype(BF16)
            for h in range(HR):
                ks, vs = slice(DH * h, DH * h + DH), slice(DV * h, DV * h + DV)
                gst = g_s[h]
                g_b = gst.astype(BF16)
                dkd = _dot(vb[:, vs], g_b, NT)
                dk_ref[:, ks] = dkd * kd_ref[:, ks]
                dv_ref[:, vs] = _dot(kdw[:, ks], g_b)
                av = a_ref[row0 + h:row0 + h + 1, :]
                lg = (_sum11(jnp.sum(kd32[:, ks] * dkd, axis=-1, keepdims=True) * wexp)
                      + C * av[:, 0:1] * _sum11(r_ref[0, h] * gst))
                dlg_ref[row0 + h:row0 + h + 1, :] += jnp.broadcast_to(lg, (1, 128))
                g_s[h] = dr_ref[0, h] + av * gst

        one(kf_ref, vf_ref, rf_ref, drf_ref, kdf_ref, gf, dkf_ref, dvf_ref, 0, C - 1.0 - ri)
        one(kb_ref, vb_ref, rb_ref, drb_ref, kdb_ref, gb, dkb_ref, dvb_ref, HR, ri)

    fwd = lambda w, off=0: pl.BlockSpec((C, w), lambda t: (N - 1 - t, off))
    bwd = lambda w, off=0: pl.BlockSpec((C, w), lambda t: (t, off))
    stf = lambda: pl.BlockSpec((1, HR, DH, DV), lambda t: (N - 1 - t, 0, 0, 0))
    stb = lambda: pl.BlockSpec((1, HR, DH, DV), lambda t: (t, 0, 0, 0))
    return pl.pallas_call(
        body, name="ret_bwd_scan",
        out_shape=[jax.ShapeDtypeStruct((S, 256), F32), jax.ShapeDtypeStruct((S, 256), F32), jax.ShapeDtypeStruct((S, 512), F32),
                   jax.ShapeDtypeStruct((S, 512), F32), jax.ShapeDtypeStruct((8, 128), F32)],
        grid=(N,),
        in_specs=[fwd(256), fwd(512, O_VR // 512), bwd(256), bwd(512, O_VR // 512), stf(), stb(), stf(), stb(),
                  _full((C, 256)), _full((C, 256)), _full((8, 128))],
        out_specs=[fwd(256), bwd(256), fwd(512), bwd(512), _full((8, 128))],
        scratch_shapes=[pltpu.VMEM((HR, DH, DV), F32), pltpu.VMEM((HR, DH, DV), F32)],
        compiler_params=_cp(("arbitrary",)),
    )(kr2, p, kr2, p, rf, rb, drf, drb, kdf, kdb, adec)


def _ret_bwd_final(dk_i, dkf, dkb, dv_i, dvf, dvb, cos, sin):
    S = dk_i.shape[0]
    tm = min(512, S)

    def body(a_ref, b_ref, c_ref, d_ref, e_ref, f_ref, cos_ref, sin_ref, o_ref):
        o_ref[:, :512] = (d_ref[...] + e_ref[...] + f_ref[...]).astype(BF16)
        cos_v, sin_v = cos_ref[...], sin_ref[...]
        for g in range(2):
            sl = slice(128 * g, 128 * g + 128)
            dk = a_ref[:, sl] + b_ref[:, sl] + c_ref[:, sl]
            o_ref[:, 512 + 128 * g:512 + 128 * g + 128] = (_rope_t(dk, cos_v, sin_v) * 0.125).astype(BF16)

    row = lambda w: pl.BlockSpec((tm, w), lambda i: (i, 0))
    return pl.pallas_call(
        body, name="ret_bwd_final", out_shape=jax.ShapeDtypeStruct((S, 768), BF16), grid=(S // tm,),
        in_specs=[row(256), row(256), row(256), row(512), row(512), row(512), row(128), row(128)], out_specs=row(768),
        compiler_params=_cp(("parallel",)),
    )(dk_i, dkf, dkb, dv_i, dvf, dvb, cos, sin)


def _bwd_in(dpm, dpa, dpra, dprb, w_p, x, dout, mod, g_pre):
    S = x.shape[0]
    tm = min(256, S)

    def body(a_ref, b_ref, c_ref, d_ref, w_ref, x_ref, dout_ref, mod_ref, g_ref, gx_ref, sums_ref):
        @pl.when(pl.program_id(0) == 0)
        def _():
            sums_ref[...] = jnp.zeros_like(sums_ref)

        dh = (_dot(a_ref[...], w_ref[:, :O_QA], NT) + _dot(b_ref[...], w_ref[:, O_QA:O_QR], NT)
              + _dot(c_ref[...], w_ref[:, O_QR:O_VR], NT) + _dot(d_ref[...], w_ref[:, O_VR:], NT))
        xv = x_ref[...]
        r = lax.rsqrt(jnp.mean(xv * xv, axis=-1, keepdims=True) + EPS)
        xn = xv * r
        gv = g_ref[...]
        sc1 = 1.0 + mod_ref[1:2, :]
        sums_ref[0:1, :] += jnp.sum(dh, axis=0, keepdims=True)
        sums_ref[1:2, :] += jnp.sum(dh * (xn * gv), axis=0, keepdims=True)
        sums_ref[2:3, :] += jnp.sum(dh * xn, axis=0, keepdims=True) * sc1
        dxn = dh * (gv * sc1)
        gx_ref[...] = dout_ref[...] + r * (dxn - xn * jnp.mean(dxn * xn, axis=-1, keepdims=True))

    row = lambda w: pl.BlockSpec((tm, w), lambda i: (i, 0))
    return pl.pallas_call(
        body, name="bwd_in", out_shape=[jax.ShapeDtypeStruct((S, D), F32), jax.ShapeDtypeStruct((8, D), F32)], grid=(S // tm,),
        in_specs=[row(2560), row(768), row(768), row(768), _full((D, P_W)), row(D), row(D), _full((3, D)), _full((1, D))],
        out_specs=[row(D), _full((8, D))],
        compiler_params=_cp(("arbitrary",), VMEM_BIG),
    )(dpm, dpa, dpra, dprb, w_p, x, dout, mod, g_pre)


def _small_reduce(gath, wdec128):
    def body(g_ref, w_ref, o_ref):
        acc = g_ref[0]
        for d in range(1, NDEV):
            acc = acc + g_ref[d]
        o_ref[...] = acc
        pk = o_ref[1:2, :]
        qn = pk[:, 2560:2688]
        kn = pk[:, 2688:2816]
        dlg = pk[:, 2816:2944] + pk[:, 2944:3072]
        ddec = dlg * _sigmoid(-w_ref[...])
        loss = (0.5 / D) * jnp.sum(o_ref[2:3, :], axis=-1, keepdims=True)
        o_ref[2:3, :] = jnp.zeros((1, 3072), F32)
        o_ref[2:3, 0:64] = qn[:, :64] + qn[:, 64:]
        o_ref[2:3, 64:128] = kn[:, :64] + kn[:, 64:]
        o_ref[2:3, 128:256] = ddec
        o_ref[2:3, 256:384] = jnp.broadcast_to(loss, (1, 128))

    return pl.pallas_call(body, name="small_reduce", out_shape=jax.ShapeDtypeStruct((8, 3072), F32))(gath, wdec128)


def _adamw(parts, w, m, v, name):
    n, R, L = parts.shape
    tr = 256 if (R % 256 == 0 and R > 256) else R

    def body(p_ref, w_ref, m_ref, v_ref, g_ref, d_ref, nm_ref, nv_ref):
        g = p_ref[0]
        for k in range(1, n):
            g = g + p_ref[k]
        g_ref[...] = g
        m2 = ADAM_B1 * m_ref[...] + (1.0 - ADAM_B1) * g
        v2 = ADAM_B2 * v_ref[...] + (1.0 - ADAM_B2) * jnp.square(g)
        m_hat = m2 / (1.0 - ADAM_B1 ** ADAM_STEP)
        v_hat = v2 / (1.0 - ADAM_B2 ** ADAM_STEP)
        d_ref[...] = -ADAM_LR * (m_hat / (jnp.sqrt(v_hat) + ADAM_EPS) + ADAM_WD * w_ref[...])
        nm_ref[...] = m2
        nv_ref[...] = v2

    blk = pl.BlockSpec((tr, L), lambda i: (i, 0))
    o = jax.ShapeDtypeStruct((R, L), F32)
    return pl.pallas_call(
        body, name=name, out_shape=[o, o, o, o], grid=(R // tr,),
        in_specs=[pl.BlockSpec((n, tr, L), lambda i: (0, i, 0)), blk, blk, blk], out_specs=[blk, blk, blk, blk],
        compiler_params=_cp(("parallel",), VMEM_BIG),
    )(parts, w, m, v)


def _rope_tables(S):
    t = jnp.arange(S)
    row = (t // 64).astype(F32)
    col = (t % 64).astype(F32)
    half = DH // 2
    inv = ROPE_THETA ** (-jnp.arange(0, half, 2, dtype=F32) / half)
    ar, ac = row[:, None] * inv[None, :], col[:, None] * inv[None, :]
    cos64 = jnp.concatenate([jnp.cos(ar), jnp.cos(ar), jnp.cos(ac), jnp.cos(ac)], axis=1)
    sin64 = jnp.concatenate([-jnp.sin(ar), jnp.sin(ar), -jnp.sin(ac), jnp.sin(ac)], axis=1)
    return jnp.tile(cos64, (1, 2)), jnp.tile(sin64, (1, 2))


def _to_p_order(w_orig):
    return jnp.concatenate([w_orig[:, ORIG[n][0]:ORIG[n][1]] for n in P_ORDER], axis=1)


def _pad_lanes(v, n):
    return jnp.pad(v, ((0, 0), (0, n - v.shape[1])))


def kernel(x, c, w_ada, b_ada, g_pre, w_in, qn_g, kn_g, w_dec_f, w_dec_b, gn_g, w_pa, w_pr, w_out, g_post, loss_target, m_w_ada, m_b_ada, m_g_pre, m_w_in, m_qn_g, m_kn_g, m_w_dec_f, m_w_dec_b, m_gn_g, m_w_pa, m_w_pr, m_w_out, m_g_post, v_w_ada, v_b_ada, v_g_pre, v_w_in, v_qn_g, v_kn_g, v_w_dec_f, v_w_dec_b, v_gn_g, v_w_pa, v_w_pr, v_w_out, v_g_post):
    S = x.shape[1]
    me = 4 * lax.axis_index("x") + 2 * lax.axis_index("y") + lax.axis_index("c")
    xs, tgt = x[0], loss_target[0]
    ncol_ada = w_ada.shape[2]
    ncol_in = w_in.shape[2]

    c_all = _small_allgather(jnp.pad(c, ((0, 7), (0, 0))), "ag_c")[:, 0, :]
    b_ada_s = lax.dynamic_slice(b_ada, (0, me * ncol_ada), (1, ncol_ada))
    mod_s, c_act = _mod_shard(jnp.pad(c_all, ((0, 8), (0, 0))), w_ada[0], b_ada_s)
    mod_all = _small_allgather(mod_s[:8], "ag_mod")
    mod = lax.dynamic_index_in_dim(mod_all, me, axis=1, keepdims=False).reshape(3, D)

    wg_in, wg_pa, wg_pr, wg_out = _allgather_hbm(
        [w_in[0].astype(BF16), w_pa[0].astype(BF16), w_pr[0].astype(BF16), w_out[0].astype(BF16)], "ag_weights")
    w_p = _to_p_order(wg_in.transpose(1, 0, 2).reshape(D, NDEV * ncol_in))
    w_pa_f = wg_pa.transpose(1, 0, 2).reshape(512, D)
    w_pr_f = wg_pr.transpose(1, 0, 2).reshape(512, D)
    w_out_f = wg_out.reshape(D, D)

    cos, sin = _rope_tables(S)
    qg, kg = jnp.tile(qn_g, (1, 2)), jnp.tile(kn_g, (1, 2))

    p, h = _fwd_in(xs, mod, g_pre, w_p)
    qh, kh, vh, qr2, kr2 = _prep(p, cos, sin, qg, kg)
    o_att, lse = _attn_fwd(qh, kh, vh)
    dc, qdf, qdb, kdf, kdb, adec = _ret_tables(w_dec_f, w_dec_b)
    rf, rb = _ret_states(kr2, p, kdf, kdb, adec)
    yr = _ret_out(qr2, kr2, p, rf, rb, dc, qdf, qdb, gn_g)

    dout, do, dpm, dyr, mb, dub, yab, dab, drb_, sums_mid = _mid(xs, tgt, mod, g_post, o_att, p, yr, w_pa_f, w_pr_f, w_out_f)
    gw_out = _mm_tn(mb, dub, "gw_out")
    gw_pa = _mm_tn(yab, dab, "gw_pa")
    gw_pr = _mm_tn(yr, drb_, "gw_pr")
    dq, dkh, dvh = _attn_bwd(qh, kh, vh, do, o_att, lse)
    dpa, gs_att = _attn_prep_bwd(dq, dkh, dvh, p, cos, sin, qg, kg)
    dpra, dk_i, dv_i, drf, drb, dgn, dlg1 = _ret_bwd_chunk(qr2, kr2, p, rf, rb, dc, qdf, qdb, gn_g, dyr, cos, sin)
    dkf, dkb, dvf, dvb, dlg2 = _ret_bwd_scan(kr2, p, rf, rb, drf, drb, kdf, kdb, adec)
    dprb = _ret_bwd_final(dk_i, dkf, dkb, dv_i, dvf, dvb, cos, sin)
    grad_x, sums_in = _bwd_in(dpm, dpa, dpra, dprb, w_p, xs, dout, mod, g_pre)
    gi_m = _mm_tn(h, dpm, "gw_in_mid")
    gi_a = _mm_tn(h, dpa, "gw_in_att")
    gi_ra = _mm_tn(h, dpra, "gw_in_reta")
    gi_rb = _mm_tn(h, dprb, "gw_in_retb")

    gw_in_orig = jnp.concatenate([gi_a, gi_m[:, 2048:2560], gi_ra[:, :256], gi_rb[:, 512:768], gi_rb[:, :512], gi_ra[:, 256:768],
                                  gi_m[:, :2048]], axis=1)
    rs_in, rs_pa, rs_pr, rs_out = _all_to_all_hbm(
        [gw_in_orig.reshape(D, NDEV, ncol_in).transpose(1, 0, 2), gw_pa.reshape(512, NDEV, 128).transpose(1, 0, 2),
         gw_pr.reshape(512, NDEV, 128).transpose(1, 0, 2), gw_out.reshape(NDEV, 128, D)], "a2a_grads")

    z128 = jnp.zeros((1, 120), F32)
    packed = jnp.concatenate([sums_in[2:3], sums_mid[1:2], dgn, gs_att[0:1], gs_att[1:2],
                              dlg1[:, 0].reshape(1, 8), z128, dlg2[:, 0].reshape(1, 8), z128], axis=1)
    mine = jnp.concatenate([jnp.concatenate([sums_in[0:1], sums_in[1:2], sums_mid[0:1]], axis=1), packed,
                            _pad_lanes(sums_mid[2:3], 3072), jnp.zeros((5, 3072), F32)], axis=0)
    gath = _small_allgather(mine, "ag_small")
    wdec128 = _pad_lanes(jnp.concatenate([w_dec_f, w_dec_b], axis=1), 128)
    red = _small_reduce(gath, wdec128)
    loss = red[2, 256]
    g_b_ada = red[0:1]
    g_small = jnp.concatenate([red[0:1], red[1:2, 0:2560], red[2:3, 0:128], red[2:3, 128:136]], axis=1)
    n_small = g_small.shape[1]
    cat = lambda *a: _pad_lanes(jnp.concatenate(a, axis=1), 5888)
    sm_g, sm_d, sm_m, sm_v = _adamw(_pad_lanes(g_small, 5888)[None],
                                    cat(b_ada, g_pre, g_post, gn_g, qn_g, kn_g, w_dec_f, w_dec_b),
                                    cat(m_b_ada, m_g_pre, m_g_post, m_gn_g, m_qn_g, m_kn_g, m_w_dec_f, m_w_dec_b),
                                    cat(v_b_ada, v_g_pre, v_g_post, v_gn_g, v_qn_g, v_kn_g, v_w_dec_f, v_w_dec_b), "adamw_small")
    offs = dict(b_ada=(0, 3072), g_pre=(3072, 4096), g_post=(4096, 5120), gn_g=(5120, 5632), qn_g=(5632, 5696), kn_g=(5696, 5760),
                w_dec_f=(5760, 5764), w_dec_b=(5764, 5768))
    del n_small, g_b_ada

    dmod_all = lax.dynamic_slice(gath[:, 0, :], (0, me * ncol_ada), (NDEV, ncol_ada))
    g_ada = _mm_tn(c_act, jnp.pad(dmod_all, ((0, 8), (0, 0))).astype(BF16), "gw_ada")

    res = dict(
        w_ada=_adamw(g_ada[None], w_ada[0], m_w_ada[0], v_w_ada[0], "adamw_ada"),
        w_in=_adamw(rs_in, w_in[0], m_w_in[0], v_w_in[0], "adamw_in"),
        w_pa=_adamw(rs_pa, w_pa[0], m_w_pa[0], v_w_pa[0], "adamw_pa"),
        w_pr=_adamw(rs_pr, w_pr[0], m_w_pr[0], v_w_pr[0], "adamw_pr"),
        w_out=_adamw(rs_out, w_out[0], m_w_out[0], v_w_out[0], "adamw_out"),
    )
    names = ["w_ada", "b_ada", "g_pre", "w_in", "qn_g", "kn_g", "w_dec_f", "w_dec_b", "gn_g", "w_pa", "w_pr", "w_out", "g_post"]
    outs = [[], [], [], []]
    for nme in names:
        for q in range(4):
            if nme in res:
                outs[q].append(res[nme][q][None])
            else:
                lo, hi = offs[nme]
                outs[q].append((sm_g, sm_d, sm_m, sm_v)[q][:, lo:hi])
    return (loss, grad_x[None], *outs[0], *outs[1], *outs[2], *outs[3])
```

```python
import functools

import jax
import jax.numpy as jnp
from jax import lax
from jax.experimental import pallas as pl
from jax.experimental.pallas import tpu as pltpu

F32, BF16 = jnp.float32, jnp.bfloat16
D = 1024
DH = 64
DHA = 80
DV = 128
LOG2E = 1.4426950408889634
LN2 = 0.6931471805599453
HR = 4
CH = 128
EPS = 1e-6
ROPE_THETA = 10000.0
NDEV = 8
O_GL, O_ZA, O_QA, O_KA, O_VA, O_QR, O_ZR, O_VR, O_KR, P_W = 0, 2048, 2560, 3072, 3200, 3328, 3584, 4096, 4608, 4864
ORIG = dict(qa=(0, 512), ka=(512, 640), va=(640, 768), za=(768, 1280), qr=(1280, 1536), kr=(1536, 1792),
            vr=(1792, 2304), zr=(2304, 2816), gl=(2816, 4864))
P_ORDER = ("gl", "za", "qa", "ka", "va", "qr", "zr", "vr", "kr")
ADAM_LR, ADAM_B1, ADAM_B2, ADAM_EPS, ADAM_WD, ADAM_STEP = 0.001, 0.9, 0.999, 1e-08, 0.01, 10
VMEM_BIG = 56 * 1024 * 1024
MESH = pl.DeviceIdType.MESH

NT = (((1,), (1,)), ((), ()))
TN = (((0,), (0,)), ((), ()))


def _dot(a, b, dims=None):
    if dims is None:
        return jnp.dot(a, b, preferred_element_type=F32)
    return lax.dot_general(a, b, dims, preferred_element_type=F32)


def _cp(sem=None, vmem=None):
    kw = {}
    if sem is not None:
        kw["dimension_semantics"] = sem
    if vmem is not None:
        kw["vmem_limit_bytes"] = vmem
    return pltpu.CompilerParams(**kw)


def _sigmoid(z):
    return 1.0 / (1.0 + jnp.exp(-z))


def _sum11(m):
    return jnp.sum(jnp.sum(m, axis=-1, keepdims=True), axis=0, keepdims=True)


def _full(shape):
    n = len(shape)
    return pl.BlockSpec(shape, lambda *_: (0,) * n)


def _my_pos():
    return lax.axis_index("x"), lax.axis_index("y"), lax.axis_index("c")


def _peer(k, x, y, c):
    return ((1 - x) if k & 4 else x, (1 - y) if k & 2 else y, (1 - c) if k & 1 else c)


def _small_allgather(v, name):
    R, L = v.shape

    def body(v_ref, out_ref, send_sems, recv_sems):
        x, y, c = _my_pos()
        me = 4 * x + 2 * y + c
        out_ref[me] = v_ref[...]
        cps = []
        for k in range(1, NDEV):
            cp = pltpu.make_async_remote_copy(src_ref=v_ref, dst_ref=out_ref.at[me], send_sem=send_sems.at[k - 1],
                                              recv_sem=recv_sems.at[k - 1], device_id=_peer(k, x, y, c), device_id_type=MESH)
            cp.start()
            cps.append(cp)
        for cp in cps:
            cp.wait()

    return pl.pallas_call(
        body, name=name, out_shape=jax.ShapeDtypeStruct((NDEV, R, L), v.dtype),
        in_specs=[pl.BlockSpec(memory_space=pltpu.VMEM)], out_specs=pl.BlockSpec(memory_space=pltpu.VMEM),
        scratch_shapes=[pltpu.SemaphoreType.DMA((NDEV - 1,)), pltpu.SemaphoreType.DMA((NDEV - 1,))],
    )(v)


def _allgather_hbm(arrs, name):
    n = len(arrs)

    def body(*refs):
        ins, outs = refs[:n], refs[n:2 * n]
        send_sems, recv_sems, local_sems = refs[2 * n:]
        x, y, c = _my_pos()
        me = 4 * x + 2 * y + c
        cps = []
        for a in range(n):
            lc = pltpu.make_async_copy(ins[a], outs[a].at[me], local_sems.at[a])
            lc.start()
            cps.append(lc)
            for k in range(1, NDEV):
                cp = pltpu.make_async_remote_copy(src_ref=ins[a], dst_ref=outs[a].at[me], send_sem=send_sems.at[a, k - 1],
                                                  recv_sem=recv_sems.at[a, k - 1], device_id=_peer(k, x, y, c), device_id_type=MESH)
                cp.start()
                cps.append(cp)
        for cp in cps:
            cp.wait()

    return pl.pallas_call(
        body, name=name, out_shape=[jax.ShapeDtypeStruct((NDEV,) + a.shape, a.dtype) for a in arrs],
        in_specs=[pl.BlockSpec(memory_space=pl.ANY)] * n, out_specs=[pl.BlockSpec(memory_space=pl.ANY)] * n,
        scratch_shapes=[pltpu.SemaphoreType.DMA((n, NDEV - 1)), pltpu.SemaphoreType.DMA((n, NDEV - 1)), pltpu.SemaphoreType.DMA((n,))],
    )(*arrs)


def _all_to_all_hbm(parts, name):
    n = len(parts)

    def body(*refs):
        ins, outs = refs[:n], refs[n:2 * n]
        send_sems, recv_sems, local_sems = refs[2 * n:]
        x, y, c = _my_pos()
        me = 4 * x + 2 * y + c
        cps = []
        for a in range(n):
            lc = pltpu.make_async_copy(ins[a].at[me], outs[a].at[me], local_sems.at[a])
            lc.start()
            cps.append(lc)
            for k in range(1, NDEV):
                px, py, pc = _peer(k, x, y, c)
                cp = pltpu.make_async_remote_copy(src_ref=ins[a].at[4 * px + 2 * py + pc], dst_ref=outs[a].at[me],
                                                  send_sem=send_sems.at[a, k - 1], recv_sem=recv_sems.at[a, k - 1],
                                                  device_id=(px, py, pc), device_id_type=MESH)
                cp.start()
                cps.append(cp)
        for cp in cps:
            cp.wait()

    return pl.pallas_call(
        body, name=name, out_shape=[jax.ShapeDtypeStruct(a.shape, a.dtype) for a in parts],
        in_specs=[pl.BlockSpec(memory_space=pl.ANY)] * n, out_specs=[pl.BlockSpec(memory_space=pl.ANY)] * n,
        scratch_shapes=[pltpu.SemaphoreType.DMA((n, NDEV - 1)), pltpu.SemaphoreType.DMA((n, NDEV - 1)), pltpu.SemaphoreType.DMA((n,))],
    )(*parts)


def _mm_tn(a, b, name):
    S, M = a.shape
    N = b.shape[1]
    tk = min(512, S)
    tn = N if N <= 768 else (640 if N % 640 == 0 else 512)
    nk = S // tk

    def body(a_ref, b_ref, o_ref):
        @pl.when(pl.program_id(1) == 0)
        def _():
            o_ref[...] = jnp.zeros_like(o_ref)
        o_ref[...] += _dot(a_ref[...], b_ref[...], TN)

    return pl.pallas_call(
        body, name=name, out_shape=jax.ShapeDtypeStruct((M, N), F32), grid=(N // tn, nk),
        in_specs=[pl.BlockSpec((tk, M), lambda j, k: (k, 0)), pl.BlockSpec((tk, tn), lambda j, k: (k, j))],
        out_specs=pl.BlockSpec((M, tn), lambda j, k: (0, j)),
        compiler_params=_cp(("parallel", "arbitrary"), VMEM_BIG),
    )(a, b)


def _mod_shard(c_pad, w_ada_s, b_ada_s):
    def body(c_ref, w_ref, b_ref, o_ref, ca_ref):
        cv = c_ref[...]
        ca = (cv * _sigmoid(cv)).astype(BF16)
        ca_ref[...] = ca
        o_ref[...] = _dot(ca, w_ref[...].astype(BF16)) + b_ref[...]

    return pl.pallas_call(
        body, name="mod_shard", out_shape=[jax.ShapeDtypeStruct((16, w_ada_s.shape[1]), F32), jax.ShapeDtypeStruct((16, D), BF16)],
    )(c_pad, w_ada_s, b_ada_s)


def _fwd_in(x, mod, g_pre, w_p):
    S = x.shape[0]
    tm, tn = min(512, S), P_W // 2

    def body(x_ref, mod_ref, g_ref, w_ref, p_ref, h_ref):
        @pl.when(pl.program_id(1) == 0)
        def _():
            xv = x_ref[...]
            r = lax.rsqrt(jnp.mean(xv * xv, axis=-1, keepdims=True) + EPS)
            h = ((xv * r) * g_ref[...]) * (1.0 + mod_ref[1:2, :]) + mod_ref[0:1, :]
            h_ref[...] = h.astype(BF16)
        p_ref[...] = _dot(h_ref[...], w_ref[...])

    return pl.pallas_call(
        body, name="fwd_in", out_shape=[jax.ShapeDtypeStruct((S, P_W), F32), jax.ShapeDtypeStruct((S, D), BF16)],
        grid=(S // tm, P_W // tn),
        in_specs=[pl.BlockSpec((tm, D), lambda i, j: (i, 0)), _full((3, D)), _full((1, D)), pl.BlockSpec((D, tn), lambda i, j: (0, j))],
        out_specs=[pl.BlockSpec((tm, tn), lambda i, j: (i, j)), pl.BlockSpec((tm, D), lambda i, j: (i, 0))],
        compiler_params=_cp(("parallel", "arbitrary"), VMEM_BIG),
    )(x, mod, g_pre, w_p)


def _swap16(v):
    lane = lax.broadcasted_iota(jnp.int32, v.shape, 1)
    return jnp.where((lane % 32) < 16, pltpu.roll(v, 112, 1), pltpu.roll(v, 16, 1))


def _rope(v, cos, sin):
    return v * cos + _swap16(v) * sin


def _rope_t(v, cos, sin):
    return v * cos - _swap16(v) * sin


def _head_mean(v):
    lo = lax.broadcasted_iota(jnp.int32, v.shape, 1) < 64
    m0 = jnp.sum(jnp.where(lo, v, 0.0), axis=-1, keepdims=True)
    m1 = jnp.sum(jnp.where(lo, 0.0, v), axis=-1, keepdims=True)
    return jnp.where(lo, m0, m1) * (1.0 / 64.0)


def _prep(p, cos, sin, qg, kg):
    S = p.shape[0]
    tm = min(512, S)

    def body(qa_ref, kv_ref, qr_ref, kr_ref, cos_ref, sin_ref, qg_ref, kg_ref, qt_ref, kh_ref, kt_ref, vh_ref, vta_ref, qr2_ref, kr2_ref):
        cos_v, sin_v = cos_ref[...], sin_ref[...]
        for g in range(4):
            xv = qa_ref[:, 128 * g:128 * g + 128]
            r = lax.rsqrt(_head_mean(xv * xv) + EPS)
            yt = (_rope((xv * r) * qg_ref[...], cos_v, sin_v) * (0.125 * LOG2E)).T
            qt_ref[2 * g] = yt[:DH].astype(BF16)
            qt_ref[2 * g + 1] = yt[DH:].astype(BF16)
        xv = kv_ref[:, :128]
        r = lax.rsqrt(_head_mean(xv * xv) + EPS)
        yv = _rope((xv * r) * kg_ref[...], cos_v, sin_v)
        kh_ref[0] = yv[:, :64].astype(BF16)
        kh_ref[1] = yv[:, 64:].astype(BF16)
        yt = yv.T
        kt_ref[0] = yt[:DH].astype(BF16)
        kt_ref[1] = yt[DH:].astype(BF16)
        vv = kv_ref[:, 128:]
        vh_ref[0] = vv[:, :64].astype(BF16)
        vh_ref[1] = vv[:, 64:].astype(BF16)
        vt = vv.T
        tail = (lax.broadcasted_iota(jnp.int32, (DHA - DH, tm), 0) == 0).astype(BF16)
        for kvh in range(2):
            vta_ref[kvh, 0:DH, :] = vt[DH * kvh:DH * kvh + DH].astype(BF16)
            vta_ref[kvh, DH:DHA, :] = tail
        for g in range(2):
            sl = slice(128 * g, 128 * g + 128)
            qr2_ref[:, sl] = _rope(qr_ref[:, sl], cos_v, sin_v)
            kr2_ref[:, sl] = _rope(kr_ref[:, sl], cos_v, sin_v) * 0.125

    hm = lambda n: pl.BlockSpec((n, tm, DH), lambda i: (0, i, 0))
    ht = lambda n, r: pl.BlockSpec((n, r, tm), lambda i: (0, 0, i))
    return pl.pallas_call(
        body, name="prep",
        out_shape=[jax.ShapeDtypeStruct((8, DH, S), BF16), jax.ShapeDtypeStruct((2, S, DH), BF16), jax.ShapeDtypeStruct((2, DH, S), BF16),
                   jax.ShapeDtypeStruct((2, S, DH), BF16), jax.ShapeDtypeStruct((2, DHA, S), BF16),
                   jax.ShapeDtypeStruct((S, 256), F32), jax.ShapeDtypeStruct((S, 256), F32)],
        grid=(S // tm,),
        in_specs=[pl.BlockSpec((tm, 512), lambda i: (i, O_QA // 512)), pl.BlockSpec((tm, 256), lambda i: (i, O_KA // 256)),
                  pl.BlockSpec((tm, 256), lambda i: (i, O_QR // 256)), pl.BlockSpec((tm, 256), lambda i: (i, O_KR // 256)),
                  pl.BlockSpec((tm, 128), lambda i: (i, 0)), pl.BlockSpec((tm, 128), lambda i: (i, 0)), _full((1, 128)), _full((1, 128))],
        out_specs=[ht(8, DH), hm(2), ht(2, DH), hm(2), ht(2, DHA), pl.BlockSpec((tm, 256), lambda i: (i, 0)), pl.BlockSpec((tm, 256), lambda i: (i, 0))],
        compiler_params=_cp(("parallel",)),
    )(p, p, p, p, cos, sin, qg, kg)


def _attn_fwd(qt, kh, vta):
    S = qt.shape[2]
    tq, tk = min(512, S), min(512, S)
    nj = S // tk

    def body(q_ref, k_ref, v_ref, o_ref, ot_ref, lse_ref, m_s, acc_s):
        j = pl.program_id(2)

        @pl.when(j == 0)
        def _():
            m_s[...] = jnp.full_like(m_s, -jnp.inf)
            acc_s[...] = jnp.zeros_like(acc_s)

        k, v = k_ref[0], v_ref[0]
        m_all = m_s[...]
        st = {0: _dot(k, q_ref[0])}
        m_new, acc_new = [], []
        for h in range(4):
            if h + 1 < 4:
                st[h + 1] = _dot(k, q_ref[h + 1])
            m_old = m_all[h:h + 1, :]
            mn = jnp.maximum(m_old, jnp.max(st[h], axis=0, keepdims=True))
            pt = jnp.exp2(st[h] - mn).astype(BF16)
            acc_new.append(jnp.exp2(m_old - mn) * acc_s[h] + _dot(v, pt))
            m_new.append(mn)
            del st[h]
        for h in range(4):
            acc_s[h] = acc_new[h]
            m_s[h:h + 1, :] = m_new[h]

        @pl.when(j == nj - 1)
        def _():
            for h in range(4):
                ot = acc_s[h, 0:DH, :] / acc_s[h, DH:DH + 1, :]
                ot_ref[h] = ot
                o_ref[:, DH * h:DH * h + DH] = ot.T
                lse_ref[0, h:h + 1, :] = m_s[h:h + 1, :] + jnp.log2(acc_s[h, DH:DH + 1, :])

    return pl.pallas_call(
        body, name="attn_fwd",
        out_shape=[jax.ShapeDtypeStruct((S, 512), F32), jax.ShapeDtypeStruct((8, DH, S), F32), jax.ShapeDtypeStruct((2, 4, S), F32)],
        grid=(2, S // tq, nj),
        in_specs=[pl.BlockSpec((4, DH, tq), lambda g, i, j: (g, 0, i)), pl.BlockSpec((1, tk, DH), lambda g, i, j: (g, j, 0)),
                  pl.BlockSpec((1, DHA, tk), lambda g, i, j: (g, 0, j))],
        out_specs=[pl.BlockSpec((tq, 256), lambda g, i, j: (i, g)), pl.BlockSpec((4, DH, tq), lambda g, i, j: (g, 0, i)),
                   pl.BlockSpec((1, 4, tq), lambda g, i, j: (g, 0, i))],
        scratch_shapes=[pltpu.VMEM((8, tq), F32), pltpu.VMEM((4, DHA, tq), F32)],
        compiler_params=_cp(("parallel", "parallel", "arbitrary"), VMEM_BIG),
    )(qt, kh, vta)


def _ret_tables(wf, wb):
    C = CH

    def body(wf_ref, wb_ref, dc_ref, qdf_ref, qdb_ref, kdf_ref, kdb_ref, a_ref):
        def logsig(w):
            z = jnp.exp(-jnp.abs(w))
            u = 1.0 + z
            l1p = jnp.where(u == 1.0, z, jnp.log(u) * (z / jnp.where(u == 1.0, 1.0, u - 1.0)))
            return jnp.minimum(w, 0.0) - l1p

        lgf, lgb = logsig(wf_ref[...]), logsig(wb_ref[...])
        lane4 = lax.broadcasted_iota(jnp.int32, (1, 4), 1)

        def pick(lg, h):
            return jnp.sum(jnp.where(lane4 == h, lg, 0.0), axis=-1, keepdims=True)

        ii = lax.broadcasted_iota(jnp.int32, (C, C), 0).astype(F32)
        jj = lax.broadcasted_iota(jnp.int32, (C, C), 1).astype(F32)
        dif = ii - jj
        hd = lax.broadcasted_iota(jnp.int32, (C, 256), 1) // DH
        lf_l = jnp.zeros((C, 256), F32)
        lb_l = jnp.zeros((C, 256), F32)
        for h in range(HR):
            lf, lb = pick(lgf, h), pick(lgb, h)
            dc_ref[h] = jnp.where(dif >= 0, jnp.exp(lf * jnp.maximum(dif, 0.0)), jnp.exp(lb * jnp.maximum(-dif, 0.0)))
            lf_l = jnp.where(hd == h, lf, lf_l)
            lb_l = jnp.where(hd == h, lb, lb_l)
            a_ref[h:h + 1, :] = jnp.broadcast_to(jnp.exp(lf * C), (1, 128))
            a_ref[HR + h:HR + h + 1, :] = jnp.broadcast_to(jnp.exp(lb * C), (1, 128))
        ri = lax.broadcasted_iota(jnp.int32, (C, 256), 0).astype(F32)
        qdf_ref[...] = jnp.exp(lf_l * (ri + 1.0))
        qdb_ref[...] = jnp.exp(lb_l * (C - ri))
        kdf_ref[...] = jnp.exp(lf_l * (C - 1.0 - ri))
        kdb_ref[...] = jnp.exp(lb_l * ri)

    t = jax.ShapeDtypeStruct((C, 256), F32)
    return pl.pallas_call(body, name="ret_tables",
                          out_shape=[jax.ShapeDtypeStruct((HR, C, C), F32), t, t, t, t, jax.ShapeDtypeStruct((8, 128), F32)])(wf, wb)


def _ret_states(kr2, p, kdf, kdb, adec):
    S = kr2.shape[0]
    C, N = CH, S // CH

    def body(kf_ref, vf_ref, kb_ref, vb_ref, kdf_ref, kdb_ref, a_ref, rf_ref, rb_ref, sf, sb):
        @pl.when(pl.program_id(0) == 0)
        def _():
            sf[...] = jnp.zeros_like(sf)
            sb[...] = jnp.zeros_like(sb)

        rf_ref[0] = sf[...]
        rb_ref[0] = sb[...]
        kdfw = (kf_ref[...] * kdf_ref[...]).astype(BF16)
        kdbw = (kb_ref[...] * kdb_ref[...]).astype(BF16)
        vf, vb = vf_ref[...].astype(BF16), vb_ref[...].astype(BF16)
        for h in range(HR):
            ks, vs = slice(DH * h, DH * h + DH), slice(DV * h, DV * h + DV)
            sf[h] = a_ref[h:h + 1, :] * sf[h] + _dot(kdfw[:, ks], vf[:, vs], TN)
            sb[h] = a_ref[HR + h:HR + h + 1, :] * sb[h] + _dot(kdbw[:, ks], vb[:, vs], TN)

    st = jax.ShapeDtypeStruct((N, HR, DH, DV), F32)
    return pl.pallas_call(
        body, name="ret_states", out_shape=[st, st], grid=(N,),
        in_specs=[pl.BlockSpec((C, 256), lambda t: (t, 0)), pl.BlockSpec((C, 512), lambda t: (t, O_VR // 512)),
                  pl.BlockSpec((C, 256), lambda t: (N - 1 - t, 0)), pl.BlockSpec((C, 512), lambda t: (N - 1 - t, O_VR // 512)),
                  _full((C, 256)), _full((C, 256)), _full((8, 128))],
        out_specs=[pl.BlockSpec((1, HR, DH, DV), lambda t: (t, 0, 0, 0)), pl.BlockSpec((1, HR, DH, DV), lambda t: (N - 1 - t, 0, 0, 0))],
        scratch_shapes=[pltpu.VMEM((HR, DH, DV), F32), pltpu.VMEM((HR, DH, DV), F32)],
        compiler_params=_cp(("arbitrary",)),
    )(kr2, p, kr2, p, kdf, kdb, adec)


def _ret_head_fwd(h, qb, kb, vb, qfw, qbw, dc_ref, rf_ref, rb_ref):
    ks, vs = slice(DH * h, DH * h + DH), slice(DV * h, DV * h + DV)
    sd = _dot(qb[:, ks], kb[:, ks], NT) * dc_ref[h]
    o = _dot(sd.astype(BF16), vb[:, vs]) + _dot(qfw[:, ks], rf_ref[0, h].astype(BF16)) + _dot(qbw[:, ks], rb_ref[0, h].astype(BF16))
    return sd, o


def _ret_out(qr2, kr2, p, rf, rb, dc, qdf, qdb, gn):
    S = qr2.shape[0]
    C, N = CH, S // CH

    def body(q_ref, k_ref, v_ref, z_ref, rf_ref, rb_ref, dc_ref, qdf_ref, qdb_ref, gn_ref, yr_ref):
        qv = q_ref[...]
        qb, kb, vb = qv.astype(BF16), k_ref[...].astype(BF16), v_ref[...].astype(BF16)
        qfw, qbw = (qv * qdf_ref[...]).astype(BF16), (qv * qdb_ref[...]).astype(BF16)
        for h in range(HR):
            vs = slice(DV * h, DV * h + DV)
            _, o = _ret_head_fwd(h, qb, kb, vb, qfw, qbw, dc_ref, rf_ref, rb_ref)
            mu = jnp.mean(o, axis=-1, keepdims=True)
            var = jnp.mean(jnp.square(o - mu), axis=-1, keepdims=True)
            on = (o - mu) * lax.rsqrt(var + EPS)
            z = z_ref[:, vs]
            yr_ref[:, vs] = ((on * gn_ref[:, vs]) * (z * _sigmoid(z))).astype(BF16)

    return pl.pallas_call(
        body, name="ret_out", out_shape=jax.ShapeDtypeStruct((S, 512), BF16), grid=(N,),
        in_specs=[pl.BlockSpec((C, 256), lambda t: (t, 0)), pl.BlockSpec((C, 256), lambda t: (t, 0)),
                  pl.BlockSpec((C, 512), lambda t: (t, O_VR // 512)), pl.BlockSpec((C, 512), lambda t: (t, O_ZR // 512)),
                  pl.BlockSpec((1, HR, DH, DV), lambda t: (t, 0, 0, 0)), pl.BlockSpec((1, HR, DH, DV), lambda t: (t, 0, 0, 0)),
                  _full((HR, C, C)), _full((C, 256)), _full((C, 256)), _full((1, 512))],
        out_specs=pl.BlockSpec((C, 512), lambda t: (t, 0)),
        compiler_params=_cp(("parallel",)),
    )(qr2, kr2, p, p, rf, rb, dc, qdf, qdb, gn)


def _mid(x, tgt, mod, g_post, o_att, p, yr, w_pa, w_pr, w_out):
    S = x.shape[0]
    tm = min(256, S)

    def body(x_ref, t_ref, mod_ref, gp_ref, o_ref, za_ref, gl_ref, yr_ref, wpa_ref, wpr_ref, wout_ref,
             dout_ref, do_ref, dpm_ref, dyr_ref, mb_ref, dub_ref, yab_ref, dab_ref, drb_ref, sums_ref):
        @pl.when(pl.program_id(0) == 0)
        def _():
            sums_ref[...] = jnp.zeros_like(sums_ref)

        za = za_ref[...]
        sa = _sigmoid(za)
        sil = za * sa
        ov = o_ref[...]
        ya_b = (ov * sil).astype(BF16)
        yr_b = yr_ref[...]
        av = _dot(ya_b, wpa_ref[...])
        rv = _dot(yr_b, wpr_ref[...])
        ga = _sigmoid(gl_ref[:, :D])
        gr = _sigmoid(gl_ref[:, D:])
        mb = (ga * av + gr * rv).astype(BF16)
        u = _dot(mb, wout_ref[...])
        r2 = lax.rsqrt(jnp.mean(u * u, axis=-1, keepdims=True) + EPS)
        un = u * r2
        gp = gp_ref[...]
        yv = un * gp
        gate = mod_ref[2:3, :]
        err = (x_ref[...] + gate * yv) - t_ref[...]
        dout = err * (1.0 / D)
        dout_ref[...] = dout
        dy = dout * gate
        sums_ref[0:1, :] += jnp.sum(dout * yv, axis=0, keepdims=True)
        sums_ref[1:2, :] += jnp.sum(dy * un, axis=0, keepdims=True)
        sums_ref[2:3, :] += jnp.sum(err * err, axis=0, keepdims=True)
        dyg = dy * gp
        du_b = (r2 * (dyg - un * jnp.mean(dyg * un, axis=-1, keepdims=True))).astype(BF16)
        dm = _dot(du_b, wout_ref[...], NT)
        da_b = (dm * ga).astype(BF16)
        dr_b = (dm * gr).astype(BF16)
        dpm_ref[:, :D] = (dm * av * (ga * (1.0 - ga))).astype(BF16)
        dpm_ref[:, D:2 * D] = (dm * rv * (gr * (1.0 - gr))).astype(BF16)
        dya = _dot(da_b, wpa_ref[...], NT)
        dyr_ref[...] = _dot(dr_b, wpr_ref[...], NT)
        dov = dya * sil
        for g in range(4):
            dt = dov[:, 128 * g:128 * g + 128].T
            do_ref[2 * g] = dt[:DH].astype(BF16)
            do_ref[2 * g + 1] = dt[DH:].astype(BF16)
        dpm_ref[:, 2 * D:] = (dya * ov * (sa * (1.0 + za * (1.0 - sa)))).astype(BF16)
        mb_ref[...] = mb
        dub_ref[...] = du_b
        yab_ref[...] = ya_b
        dab_ref[...] = da_b
        drb_ref[...] = dr_b

    row = lambda w: pl.BlockSpec((tm, w), lambda i: (i, 0))
    sd = lambda w, dt: jax.ShapeDtypeStruct((S, w), dt)
    return pl.pallas_call(
        body, name="mid",
        out_shape=[sd(D, F32), jax.ShapeDtypeStruct((8, DH, S), BF16), sd(2560, BF16), sd(512, F32), sd(D, BF16), sd(D, BF16), sd(512, BF16),
                   sd(D, BF16), sd(D, BF16), jax.ShapeDtypeStruct((8, D), F32)],
        grid=(S // tm,),
        in_specs=[row(D), row(D), _full((3, D)), _full((1, D)), row(512), pl.BlockSpec((tm, 512), lambda i: (i, O_ZA // 512)),
                  pl.BlockSpec((tm, 2048), lambda i: (i, 0)), row(512), _full((512, D)), _full((512, D)), _full((D, D))],
        out_specs=[row(D), pl.BlockSpec((8, DH, tm), lambda i: (0, 0, i)), row(2560), row(512), row(D), row(D), row(512), row(D), row(D),
                   _full((8, D))],
        compiler_params=_cp(("arbitrary",), VMEM_BIG),
    )(x, tgt, mod, g_post, o_att, p, p, yr, w_pa, w_pr, w_out)


def _attn_bwd(qt, kh, kt, vh, dot_, ot, lse):
    S = qt.shape[2]
    tq, tk = min(512, S), min(512, S)

    def body(q_ref, k_ref, kt_ref, v_ref, do_ref, o_ref, lse_ref, dq_ref, dk_ref, dv_ref):
        j, i = pl.program_id(1), pl.program_id(2)
        cols = pl.ds(pl.multiple_of(i * tq, tq), tq)
        k, kt, v = k_ref[0], kt_ref[0], v_ref[0]
        lse_all = lse_ref[0]
        st = {0: _dot(k, q_ref[0])}
        dpt = {0: _dot(v, do_ref[0])}
        dk_acc, dv_acc, dqs = None, None, []
        for h in range(4):
            if h + 1 < 4:
                st[h + 1] = _dot(k, q_ref[h + 1])
                dpt[h + 1] = _dot(v, do_ref[h + 1])
            qt_h, dot_h = q_ref[h], do_ref[h]
            delta = jnp.sum(dot_h.astype(F32) * o_ref[h], axis=0, keepdims=True)
            pt = jnp.exp2(st[h] - lse_all[h:h + 1, :])
            dst = (pt * (dpt[h] - delta)).astype(BF16)
            dv_h = _dot(dot_h, pt.astype(BF16), NT)
            dk_h = _dot(qt_h, dst, NT)
            dqs.append(_dot(kt, dst))
            dv_acc = dv_h if dv_acc is None else dv_acc + dv_h
            dk_acc = dk_h if dk_acc is None else dk_acc + dk_h
            del st[h], dpt[h]

        @pl.when(i == 0)
        def _():
            dk_ref[0] = dk_acc
            dv_ref[0] = dv_acc

        @pl.when(i > 0)
        def _():
            dk_ref[0] += dk_acc
            dv_ref[0] += dv_acc

        @pl.when(j == 0)
        def _():
            for h in range(4):
                dq_ref[h, :, cols] = dqs[h]

        @pl.when(j > 0)
        def _():
            for h in range(4):
                dq_ref[h, :, cols] += dqs[h]

    return pl.pallas_call(
        body, name="attn_bwd",
        out_shape=[jax.ShapeDtypeStruct((8, DH, S), F32), jax.ShapeDtypeStruct((2, DH, S), F32), jax.ShapeDtypeStruct((2, DH, S), F32)],
        grid=(2, S // tk, S // tq),
        in_specs=[pl.BlockSpec((4, DH, tq), lambda g, j, i: (g, 0, i)), pl.BlockSpec((1, tk, DH), lambda g, j, i: (g, j, 0)),
                  pl.BlockSpec((1, DH, tk), lambda g, j, i: (g, 0, j)), pl.BlockSpec((1, tk, DH), lambda g, j, i: (g, j, 0)),
                  pl.BlockSpec((4, DH, tq), lambda g, j, i: (g, 0, i)), pl.BlockSpec((4, DH, tq), lambda g, j, i: (g, 0, i)),
                  pl.BlockSpec((1, 4, tq), lambda g, j, i: (g, 0, i))],
        out_specs=[pl.BlockSpec((4, DH, S), lambda g, j, i: (g, 0, 0)), pl.BlockSpec((1, DH, tk), lambda g, j, i: (g, 0, j)),
                   pl.BlockSpec((1, DH, tk), lambda g, j, i: (g, 0, j))],
        compiler_params=_cp(("arbitrary", "arbitrary", "arbitrary"), VMEM_BIG),
    )(qt, kh, kt, vh, dot_, ot, lse)


def _attn_prep_bwd(dqt, dkt, dvt, p, cos, sin, qg, kg):
    S = dqt.shape[2]
    tm = min(512, S)

    def body(dq_ref, dk_ref, dv_ref, qa_ref, ka_ref, cos_ref, sin_ref, qg_ref, kg_ref, dp_ref, gs_ref):
        @pl.when(pl.program_id(0) == 0)
        def _():
            gs_ref[...] = jnp.zeros_like(gs_ref)

        cos_v, sin_v = cos_ref[...], sin_ref[...]

        def pair(ref, a):
            return jnp.concatenate([ref[a], ref[a + 1]], axis=0).T

        def norm_bwd(dyv, xv, gv, row):
            r = lax.rsqrt(_head_mean(xv * xv) + EPS)
            xn = xv * r
            dxh = _rope_t(dyv, cos_v, sin_v)
            gs_ref[row:row + 1, :] += jnp.sum(dxh * xn, axis=0, keepdims=True)
            dg = dxh * gv
            return r * (dg - xn * _head_mean(dg * xn))

        for g in range(4):
            sl = slice(128 * g, 128 * g + 128)
            dp_ref[:, sl] = norm_bwd(pair(dq_ref, 2 * g) * 0.125, qa_ref[:, sl], qg_ref[...], 0).astype(BF16)
        dp_ref[:, 512:640] = norm_bwd(pair(dk_ref, 0) * LN2, ka_ref[...], kg_ref[...], 1).astype(BF16)
        dp_ref[:, 640:768] = pair(dv_ref, 0).astype(BF16)

    ht = lambda n: pl.BlockSpec((n, DH, tm), lambda i: (0, 0, i))
    return pl.pallas_call(
        body, name="attn_prep_bwd", out_shape=[jax.ShapeDtypeStruct((S, 768), BF16), jax.ShapeDtypeStruct((8, 128), F32)],
        grid=(S // tm,),
        in_specs=[ht(8), ht(2), ht(2),
                  pl.BlockSpec((tm, 512), lambda i: (i, O_QA // 512)), pl.BlockSpec((tm, 128), lambda i: (i, O_KA // 128)),
                  pl.BlockSpec((tm, 128), lambda i: (i, 0)), pl.BlockSpec((tm, 128), lambda i: (i, 0)), _full((1, 128)), _full((1, 128))],
        out_specs=[pl.BlockSpec((tm, 768), lambda i: (i, 0)), _full((8, 128))],
        compiler_params=_cp(("arbitrary",)),
    )(dqt, dkt, dvt, p, p, cos, sin, qg, kg)


def _ret_bwd_chunk(qr2, kr2, p, rf, rb, dc, qdf, qdb, gn, dyr, cos, sin):
    S = qr2.shape[0]
    C, N = CH, S // CH

    def body(q_ref, k_ref, v_ref, z_ref, rf_ref, rb_ref, dc_ref, qdf_ref, qdb_ref, gn_ref, dyr_ref, cos_ref, sin_ref,
             dpa_ref, dk_ref, dv_ref, drf_ref, drb_ref, dgn_ref, dlg_ref, dqs):
        @pl.when(pl.program_id(0) == 0)
        def _():
            dgn_ref[...] = jnp.zeros_like(dgn_ref)
            dlg_ref[...] = jnp.zeros_like(dlg_ref)

        qv = q_ref[...]
        qb, kb, vb = qv.astype(BF16), k_ref[...].astype(BF16), v_ref[...].astype(BF16)
        qf32, qb32 = qv * qdf_ref[...], qv * qdb_ref[...]
        qfw, qbw = qf32.astype(BF16), qb32.astype(BF16)
        ii = lax.broadcasted_iota(jnp.int32, (C, C), 0).astype(F32)
        jj = lax.broadcasted_iota(jnp.int32, (C, C), 1).astype(F32)
        dif = ii - jj
        ri = lax.broadcasted_iota(jnp.int32, (C, 1), 0).astype(F32)
        for h in range(HR):
            ks, vs = slice(DH * h, DH * h + DH), slice(DV * h, DV * h + DV)
            sd, o = _ret_head_fwd(h, qb, kb, vb, qfw, qbw, dc_ref, rf_ref, rb_ref)
            mu = jnp.mean(o, axis=-1, keepdims=True)
            rstd = lax.rsqrt(jnp.mean(jnp.square(o - mu), axis=-1, keepdims=True) + EPS)
            on = (o - mu) * rstd
            z = z_ref[:, vs]
            sz = _sigmoid(z)
            dy = dyr_ref[:, vs]
            gnv = gn_ref[:, vs]
            dpa_ref[:, 256 + DV * h:256 + DV * h + DV] = (dy * (on * gnv) * (sz * (1.0 + z * (1.0 - sz)))).astype(BF16)
            dys = dy * (z * sz)
            dgn_ref[:, vs] += jnp.sum(dys * on, axis=0, keepdims=True)
            don = dys * gnv
            do = rstd * (don - jnp.mean(don, axis=-1, keepdims=True) - on * jnp.mean(don * on, axis=-1, keepdims=True))
            do_b = do.astype(BF16)
            dpm = _dot(do_b, vb[:, vs], NT)
            dv_ref[:, vs] = _dot(sd.astype(BF16), do_b, TN)
            dsd = (dpm * dc_ref[h]).astype(BF16)
            dqf = _dot(do_b, rf_ref[0, h].astype(BF16), NT)
            dqb = _dot(do_b, rb_ref[0, h].astype(BF16), NT)
            dqs[:, ks] = _dot(dsd, kb[:, ks]) + dqf * qdf_ref[:, ks] + dqb * qdb_ref[:, ks]
            dk_ref[:, ks] = _dot(dsd, qb[:, ks], TN)
            drf_ref[0, h] = _dot(qfw[:, ks], do_b, TN)
            drb_ref[0, h] = _dot(qbw[:, ks], do_b, TN)
            e = dpm * sd
            lf = _sum11(e * jnp.maximum(dif, 0.0)) + _sum11(jnp.sum(qf32[:, ks] * dqf, axis=-1, keepdims=True) * (ri + 1.0))
            lb = _sum11(e * jnp.maximum(-dif, 0.0)) + _sum11(jnp.sum(qb32[:, ks] * dqb, axis=-1, keepdims=True) * (C - ri))
            dlg_ref[h:h + 1, :] += jnp.broadcast_to(lf, (1, 128))
            dlg_ref[HR + h:HR + h + 1, :] += jnp.broadcast_to(lb, (1, 128))
        cos_v, sin_v = cos_ref[...], sin_ref[...]
        for g in range(2):
            sl = slice(128 * g, 128 * g + 128)
            dpa_ref[:, sl] = _rope_t(dqs[:, sl], cos_v, sin_v).astype(BF16)

    st = jax.ShapeDtypeStruct((N, HR, DH, DV), F32)
    stb = lambda: pl.BlockSpec((1, HR, DH, DV), lambda t: (t, 0, 0, 0))
    return pl.pallas_call(
        body, name="ret_bwd_chunk",
        out_shape=[jax.ShapeDtypeStruct((S, 768), BF16), jax.ShapeDtypeStruct((S, 256), F32), jax.ShapeDtypeStruct((S, 512), F32), st, st,
                   jax.ShapeDtypeStruct((1, 512), F32), jax.ShapeDtypeStruct((8, 128), F32)],
        grid=(N,),
        in_specs=[pl.BlockSpec((C, 256), lambda t: (t, 0)), pl.BlockSpec((C, 256), lambda t: (t, 0)),
                  pl.BlockSpec((C, 512), lambda t: (t, O_VR // 512)), pl.BlockSpec((C, 512), lambda t: (t, O_ZR // 512)),
                  stb(), stb(), _full((HR, C, C)), _full((C, 256)), _full((C, 256)), _full((1, 512)),
                  pl.BlockSpec((C, 512), lambda t: (t, 0)), pl.BlockSpec((C, 128), lambda t: (t, 0)), pl.BlockSpec((C, 128), lambda t: (t, 0))],
        out_specs=[pl.BlockSpec((C, 768), lambda t: (t, 0)), pl.BlockSpec((C, 256), lambda t: (t, 0)), pl.BlockSpec((C, 512), lambda t: (t, 0)),
                   stb(), stb(), _full((1, 512)), _full((8, 128))],
        scratch_shapes=[pltpu.VMEM((C, 256), F32)],
        compiler_params=_cp(("arbitrary",)),
    )(qr2, kr2, p, p, rf, rb, dc, qdf, qdb, gn, dyr, cos, sin)


def _ret_bwd_scan(kr2, p, rf, rb, drf, drb, kdf, kdb, adec):
    S = kr2.shape[0]
    C, N = CH, S // CH

    def body(kf_ref, vf_ref, kb_ref, vb_ref, rf_ref, rb_ref, drf_ref, drb_ref, kdf_ref, kdb_ref, a_ref,
             dkf_ref, dkb_ref, dvf_ref, dvb_ref, dlg_ref, gf, gb):
        @pl.when(pl.program_id(0) == 0)
        def _():
            gf[...] = jnp.zeros_like(gf)
            gb[...] = jnp.zeros_like(gb)
            dlg_ref[...] = jnp.zeros_like(dlg_ref)

        ri = lax.broadcasted_iota(jnp.int32, (C, 1), 0).astype(F32)

        def one(k_ref, v_ref, r_ref, dr_ref, kd_ref, g_s, dk_ref, dv_ref, row0, wexp):
            kd32 = k_ref[...] * kd_ref[...]
            kdw = kd32.astype(BF16)
            vb = v_ref[...].astype(BF16)
            for h in range(HR):
                ks, vs = slice(DH * h, DH * h + DH), slice(DV * h, DV * h + DV)
                gst = g_s[h]
                g_b = gst.astype(BF16)
                dkd = _dot(vb[:, vs], g_b, NT)
                dk_ref[:, ks] = dkd * kd_ref[:, ks]
                dv_ref[:, vs] = _dot(kdw[:, ks], g_b)
                av = a_ref[row0 + h:row0 + h + 1, :]
                lg = (_sum11(jnp.sum(kd32[:, ks] * dkd, axis=-1, keepdims=True) * wexp)
                      + C * av[:, 0:1] * _sum11(r_ref[0, h] * gst))
                dlg_ref[row0 + h:row0 + h + 1, :] += jnp.broadcast_to(lg, (1, 128))
                g_s[h] = dr_ref[0, h] + av * gst

        one(kf_ref, vf_ref, rf_ref, drf_ref, kdf_ref, gf, dkf_ref, dvf_ref, 0, C - 1.0 - ri)
        one(kb_ref, vb_ref, rb_ref, drb_ref, kdb_ref, gb, dkb_ref, dvb_ref, HR, ri)

    fwd = lambda w, off=0: pl.BlockSpec((C, w), lambda t: (N - 1 - t, off))
    bwd = lambda w, off=0: pl.BlockSpec((C, w), lambda t: (t, off))
    stf = lambda: pl.BlockSpec((1, HR, DH, DV), lambda t: (N - 1 - t, 0, 0, 0))
    stb = lambda: pl.BlockSpec((1, HR, DH, DV), lambda t: (t, 0, 0, 0))
    return pl.pallas_call(
        body, name="ret_bwd_scan",
        out_shape=[jax.ShapeDtypeStruct((S, 256), F32), jax.ShapeDtypeStruct((S, 256), F32), jax.ShapeDtypeStruct((S, 512), F32),
                   jax.ShapeDtypeStruct((S, 512), F32), jax.ShapeDtypeStruct((8, 128), F32)],
        grid=(N,),
        in_specs=[fwd(256), fwd(512, O_VR // 512), bwd(256), bwd(512, O_VR // 512), stf(), stb(), stf(), stb(),
                  _full((C, 256)), _full((C, 256)), _full((8, 128))],
        out_specs=[fwd(256), bwd(256), fwd(512), bwd(512), _full((8, 128))],
        scratch_shapes=[pltpu.VMEM((HR, DH, DV), F32), pltpu.VMEM((HR, DH, DV), F32)],
        compiler_params=_cp(("arbitrary",)),
    )(kr2, p, kr2, p, rf, rb, drf, drb, kdf, kdb, adec)


def _ret_bwd_final(dk_i, dkf, dkb, dv_i, dvf, dvb, cos, sin):
    S = dk_i.shape[0]
    tm = min(512, S)

    def body(a_ref, b_ref, c_ref, d_ref, e_ref, f_ref, cos_ref, sin_ref, o_ref):
        o_ref[:, :512] = (d_ref[...] + e_ref[...] + f_ref[...]).astype(BF16)
        cos_v, sin_v = cos_ref[...], sin_ref[...]
        for g in range(2):
            sl = slice(128 * g, 128 * g + 128)
            dk = a_ref[:, sl] + b_ref[:, sl] + c_ref[:, sl]
            o_ref[:, 512 + 128 * g:512 + 128 * g + 128] = (_rope_t(dk, cos_v, sin_v) * 0.125).astype(BF16)

    row = lambda w: pl.BlockSpec((tm, w), lambda i: (i, 0))
    return pl.pallas_call(
        body, name="ret_bwd_final", out_shape=jax.ShapeDtypeStruct((S, 768), BF16), grid=(S // tm,),
        in_specs=[row(256), row(256), row(256), row(512), row(512), row(512), row(128), row(128)], out_specs=row(768),
        compiler_params=_cp(("parallel",)),
    )(dk_i, dkf, dkb, dv_i, dvf, dvb, cos, sin)


def _bwd_in(dpm, dpa, dpra, dprb, w_p, x, dout, mod, g_pre):
    S = x.shape[0]
    tm = min(256, S)

    def body(a_ref, b_ref, c_ref, d_ref, w_ref, x_ref, dout_ref, mod_ref, g_ref, gx_ref, sums_ref):
        @pl.when(pl.program_id(0) == 0)
        def _():
            sums_ref[...] = jnp.zeros_like(sums_ref)

        dh = (_dot(a_ref[...], w_ref[:, :O_QA], NT) + _dot(b_ref[...], w_ref[:, O_QA:O_QR], NT)
              + _dot(c_ref[...], w_ref[:, O_QR:O_VR], NT) + _dot(d_ref[...], w_ref[:, O_VR:], NT))
        xv = x_ref[...]
        r = lax.rsqrt(jnp.mean(xv * xv, axis=-1, keepdims=True) + EPS)
        xn = xv * r
        gv = g_ref[...]
        sc1 = 1.0 + mod_ref[1:2, :]
        sums_ref[0:1, :] += jnp.sum(dh, axis=0, keepdims=True)
        sums_ref[1:2, :] += jnp.sum(dh * (xn * gv), axis=0, keepdims=True)
        sums_ref[2:3, :] += jnp.sum(dh * xn, axis=0, keepdims=True) * sc1
        dxn = dh * (gv * sc1)
        gx_ref[...] = dout_ref[...] + r * (dxn - xn * jnp.mean(dxn * xn, axis=-1, keepdims=True))

    row = lambda w: pl.BlockSpec((tm, w), lambda i: (i, 0))
    return pl.pallas_call(
        body, name="bwd_in", out_shape=[jax.ShapeDtypeStruct((S, D), F32), jax.ShapeDtypeStruct((8, D), F32)], grid=(S // tm,),
        in_specs=[row(2560), row(768), row(768), row(768), _full((D, P_W)), row(D), row(D), _full((3, D)), _full((1, D))],
        out_specs=[row(D), _full((8, D))],
        compiler_params=_cp(("arbitrary",), VMEM_BIG),
    )(dpm, dpa, dpra, dprb, w_p, x, dout, mod, g_pre)


def _small_reduce(gath, wdec128):
    def body(g_ref, w_ref, o_ref):
        acc = g_ref[0]
        for d in range(1, NDEV):
            acc = acc + g_ref[d]
        o_ref[...] = acc
        pk = o_ref[1:2, :]
        qn = pk[:, 2560:2688]
        kn = pk[:, 2688:2816]
        dlg = pk[:, 2816:2944] + pk[:, 2944:3072]
        ddec = dlg * _sigmoid(-w_ref[...])
        loss = (0.5 / D) * jnp.sum(o_ref[2:3, :], axis=-1, keepdims=True)
        o_ref[2:3, :] = jnp.zeros((1, 3072), F32)
        o_ref[2:3, 0:64] = qn[:, :64] + qn[:, 64:]
        o_ref[2:3, 64:128] = kn[:, :64] + kn[:, 64:]
        o_ref[2:3, 128:256] = ddec
        o_ref[2:3, 256:384] = jnp.broadcast_to(loss, (1, 128))

    return pl.pallas_call(body, name="small_reduce", out_shape=jax.ShapeDtypeStruct((8, 3072), F32))(gath, wdec128)


def _adamw(parts, w, m, v, name):
    n, R, L = parts.shape
    tr = 256 if (R % 256 == 0 and R > 256) else R

    def body(p_ref, w_ref, m_ref, v_ref, g_ref, d_ref, nm_ref, nv_ref):
        g = p_ref[0]
        for k in range(1, n):
            g = g + p_ref[k]
        g_ref[...] = g
        m2 = ADAM_B1 * m_ref[...] + (1.0 - ADAM_B1) * g
        v2 = ADAM_B2 * v_ref[...] + (1.0 - ADAM_B2) * jnp.square(g)
        m_hat = m2 / (1.0 - ADAM_B1 ** ADAM_STEP)
        v_hat = v2 / (1.0 - ADAM_B2 ** ADAM_STEP)
        d_ref[...] = -ADAM_LR * (m_hat / (jnp.sqrt(v_hat) + ADAM_EPS) + ADAM_WD * w_ref[...])
        nm_ref[...] = m2
        nv_ref[...] = v2

    blk = pl.BlockSpec((tr, L), lambda i: (i, 0))
    o = jax.ShapeDtypeStruct((R, L), F32)
    return pl.pallas_call(
        body, name=name, out_shape=[o, o, o, o], grid=(R // tr,),
        in_specs=[pl.BlockSpec((n, tr, L), lambda i: (0, i, 0)), blk, blk, blk], out_specs=[blk, blk, blk, blk],
        compiler_params=_cp(("parallel",), VMEM_BIG),
    )(parts, w, m, v)


def _rope_tables(S):
    t = jnp.arange(S)
    row = (t // 64).astype(F32)
    col = (t % 64).astype(F32)
    half = DH // 2
    inv = ROPE_THETA ** (-jnp.arange(0, half, 2, dtype=F32) / half)
    ar, ac = row[:, None] * inv[None, :], col[:, None] * inv[None, :]
    cos64 = jnp.concatenate([jnp.cos(ar), jnp.cos(ar), jnp.cos(ac), jnp.cos(ac)], axis=1)
    sin64 = jnp.concatenate([-jnp.sin(ar), jnp.sin(ar), -jnp.sin(ac), jnp.sin(ac)], axis=1)
    return jnp.tile(cos64, (1, 2)), jnp.tile(sin64, (1, 2))


def _to_p_order(w_orig):
    return jnp.concatenate([w_orig[:, ORIG[n][0]:ORIG[n][1]] for n in P_ORDER], axis=1)


def _pad_lanes(v, n):
    return jnp.pad(v, ((0, 0), (0, n - v.shape[1])))


def kernel(x, c, w_ada, b_ada, g_pre, w_in, qn_g, kn_g, w_dec_f, w_dec_b, gn_g, w_pa, w_pr, w_out, g_post, loss_target, m_w_ada, m_b_ada, m_g_pre, m_w_in, m_qn_g, m_kn_g, m_w_dec_f, m_w_dec_b, m_gn_g, m_w_pa, m_w_pr, m_w_out, m_g_post, v_w_ada, v_b_ada, v_g_pre, v_w_in, v_qn_g, v_kn_g, v_w_dec_f, v_w_dec_b, v_gn_g, v_w_pa, v_w_pr, v_w_out, v_g_post):
    S = x.shape[1]
    me = 4 * lax.axis_index("x") + 2 * lax.axis_index("y") + lax.axis_index("c")
    xs, tgt = x[0], loss_target[0]
    ncol_ada = w_ada.shape[2]
    ncol_in = w_in.shape[2]

    c_all = _small_allgather(jnp.pad(c, ((0, 7), (0, 0))), "ag_c")[:, 0, :]
    b_ada_s = lax.dynamic_slice(b_ada, (0, me * ncol_ada), (1, ncol_ada))
    mod_s, c_act = _mod_shard(jnp.pad(c_all, ((0, 8), (0, 0))), w_ada[0], b_ada_s)
    mod_all = _small_allgather(mod_s[:8], "ag_mod")
    mod = lax.dynamic_index_in_dim(mod_all, me, axis=1, keepdims=False).reshape(3, D)

    wg_in, wg_pa, wg_pr, wg_out = _allgather_hbm(
        [w_in[0].astype(BF16), w_pa[0].astype(BF16), w_pr[0].astype(BF16), w_out[0].astype(BF16)], "ag_weights")
    w_p = _to_p_order(wg_in.transpose(1, 0, 2).reshape(D, NDEV * ncol_in))
    w_pa_f = wg_pa.transpose(1, 0, 2).reshape(512, D)
    w_pr_f = wg_pr.transpose(1, 0, 2).reshape(512, D)
    w_out_f = wg_out.reshape(D, D)

    cos, sin = _rope_tables(S)
    qg, kg = jnp.tile(qn_g, (1, 2)), jnp.tile(kn_g, (1, 2))

    p, h = _fwd_in(xs, mod, g_pre, w_p)
    qt, kh, kt, vh, vta, qr2, kr2 = _prep(p, cos, sin, qg, kg)
    o_att, o_t, lse = _attn_fwd(qt, kh, vta)
    dc, qdf, qdb, kdf, kdb, adec = _ret_tables(w_dec_f, w_dec_b)
    rf, rb = _ret_states(kr2, p, kdf, kdb, adec)
    yr = _ret_out(qr2, kr2, p, rf, rb, dc, qdf, qdb, gn_g)

    dout, do, dpm, dyr, mb, dub, yab, dab, drb_, sums_mid = _mid(xs, tgt, mod, g_post, o_att, p, yr, w_pa_f, w_pr_f, w_out_f)
    gw_out = _mm_tn(mb, dub, "gw_out")
    gw_pa = _mm_tn(yab, dab, "gw_pa")
    gw_pr = _mm_tn(yr, drb_, "gw_pr")
    dqt, dkt, dvt = _attn_bwd(qt, kh, kt, vh, do, o_t, lse)
    dpa, gs_att = _attn_prep_bwd(dqt, dkt, dvt, p, cos, sin, qg, kg)
    dpra, dk_i, dv_i, drf, drb, dgn, dlg1 = _ret_bwd_chunk(qr2, kr2, p, rf, rb, dc, qdf, qdb, gn_g, dyr, cos, sin)
    dkf, dkb, dvf, dvb, dlg2 = _ret_bwd_scan(kr2, p, rf, rb, drf, drb, kdf, kdb, adec)
    dprb = _ret_bwd_final(dk_i, dkf, dkb, dv_i, dvf, dvb, cos, sin)
    grad_x, sums_in = _bwd_in(dpm, dpa, dpra, dprb, w_p, xs, dout, mod, g_pre)
    gi_m = _mm_tn(h, dpm, "gw_in_mid")
    gi_a = _mm_tn(h, dpa, "gw_in_att")
    gi_ra = _mm_tn(h, dpra, "gw_in_reta")
    gi_rb = _mm_tn(h, dprb, "gw_in_retb")

    gw_in_orig = jnp.concatenate([gi_a, gi_m[:, 2048:2560], gi_ra[:, :256], gi_rb[:, 512:768], gi_rb[:, :512], gi_ra[:, 256:768],
                                  gi_m[:, :2048]], axis=1)
    rs_in, rs_pa, rs_pr, rs_out = _all_to_all_hbm(
        [gw_in_orig.reshape(D, NDEV, ncol_in).transpose(1, 0, 2), gw_pa.reshape(512, NDEV, 128).transpose(1, 0, 2),
         gw_pr.reshape(512, NDEV, 128).transpose(1, 0, 2), gw_out.reshape(NDEV, 128, D)], "a2a_grads")

    z128 = jnp.zeros((1, 120), F32)
    packed = jnp.concatenate([sums_in[2:3], sums_mid[1:2], dgn, gs_att[0:1], gs_att[1:2],
                              dlg1[:, 0].reshape(1, 8), z128, dlg2[:, 0].reshape(1, 8), z128], axis=1)
    mine = jnp.concatenate([jnp.concatenate([sums_in[0:1], sums_in[1:2], sums_mid[0:1]], axis=1), packed,
                            _pad_lanes(sums_mid[2:3], 3072), jnp.zeros((5, 3072), F32)], axis=0)
    gath = _small_allgather(mine, "ag_small")
    wdec128 = _pad_lanes(jnp.concatenate([w_dec_f, w_dec_b], axis=1), 128)
    red = _small_reduce(gath, wdec128)
    loss = red[2, 256]
    g_b_ada = red[0:1]
    g_small = jnp.concatenate([red[0:1], red[1:2, 0:2560], red[2:3, 0:128], red[2:3, 128:136]], axis=1)
    n_small = g_small.shape[1]
    cat = lambda *a: _pad_lanes(jnp.concatenate(a, axis=1), 5888)
    sm_g, sm_d, sm_m, sm_v = _adamw(_pad_lanes(g_small, 5888)[None],
                                    cat(b_ada, g_pre, g_post, gn_g, qn_g, kn_g, w_dec_f, w_dec_b),
                                    cat(m_b_ada, m_g_pre, m_g_post, m_gn_g, m_qn_g, m_kn_g, m_w_dec_f, m_w_dec_b),
                                    cat(v_b_ada, v_g_pre, v_g_post, v_gn_g, v_qn_g, v_kn_g, v_w_dec_f, v_w_dec_b), "adamw_small")
    offs = dict(b_ada=(0, 3072), g_pre=(3072, 4096), g_post=(4096, 5120), gn_g=(5120, 5632), qn_g=(5632, 5696), kn_g=(5696, 5760),
                w_dec_f=(5760, 5764), w_dec_b=(5764, 5768))
    del n_small, g_b_ada

    dmod_all = lax.dynamic_slice(gath[:, 0, :], (0, me * ncol_ada), (NDEV, ncol_ada))
    g_ada = _mm_tn(c_act, jnp.pad(dmod_all, ((0, 8), (0, 0))).astype(BF16), "gw_ada")

    res = dict(
        w_ada=_adamw(g_ada[None], w_ada[0], m_w_ada[0], v_w_ada[0], "adamw_ada"),
        w_in=_adamw(rs_in, w_in[0], m_w_in[0], v_w_in[0], "adamw_in"),
        w_pa=_adamw(rs_pa, w_pa[0], m_w_pa[0], v_w_pa[0], "adamw_pa"),
        w_pr=_adamw(rs_pr, w_pr[0], m_w_pr[0], v_w_pr[0], "adamw_pr"),
        w_out=_adamw(rs_out, w_out[0], m_w_out[0], v_w_out[0], "adamw_out"),
    )
    names = ["w_ada", "b_ada", "g_pre", "w_in", "qn_g", "kn_g", "w_dec_f", "w_dec_b", "gn_g", "w_pa", "w_pr", "w_out", "g_post"]
    outs = [[], [], [], []]
    for nme in names:
        for q in range(4):
            if nme in res:
                outs[q].append(res[nme][q][None])
            else:
                lo, hi = offs[nme]
                outs[q].append((sm_g, sm_d, sm_m, sm_v)[q][:, lo:hi])
    return (loss, grad_x[None], *outs[0], *outs[1], *outs[2], *outs[3])
```

```python
import functools

import jax
import jax.numpy as jnp
from jax import lax
from jax.experimental import pallas as pl
from jax.experimental.pallas import tpu as pltpu

F32, BF16 = jnp.float32, jnp.bfloat16
D = 1024
DH = 64
DHA = 80
DV = 128
LOG2E = 1.4426950408889634
LN2 = 0.6931471805599453
HR = 4
CH = 128
EPS = 1e-6
ROPE_THETA = 10000.0
NDEV = 8
O_GL, O_ZA, O_QA, O_KA, O_VA, O_QR, O_ZR, O_VR, O_KR, P_W = 0, 2048, 2560, 3072, 3200, 3328, 3584, 4096, 4608, 4864
ORIG = dict(qa=(0, 512), ka=(512, 640), va=(640, 768), za=(768, 1280), qr=(1280, 1536), kr=(1536, 1792),
            vr=(1792, 2304), zr=(2304, 2816), gl=(2816, 4864))
P_ORDER = ("gl", "za", "qa", "ka", "va", "qr", "zr", "vr", "kr")
ADAM_LR, ADAM_B1, ADAM_B2, ADAM_EPS, ADAM_WD, ADAM_STEP = 0.001, 0.9, 0.999, 1e-08, 0.01, 10
VMEM_BIG = 56 * 1024 * 1024
MESH = pl.DeviceIdType.MESH

NT = (((1,), (1,)), ((), ()))
TN = (((0,), (0,)), ((), ()))


def _dot(a, b, dims=None):
    if dims is None:
        return jnp.dot(a, b, preferred_element_type=F32)
    return lax.dot_general(a, b, dims, preferred_element_type=F32)


def _cp(sem=None, vmem=None):
    kw = {}
    if sem is not None:
        kw["dimension_semantics"] = sem
    if vmem is not None:
        kw["vmem_limit_bytes"] = vmem
    return pltpu.CompilerParams(**kw)


def _sigmoid(z):
    return 1.0 / (1.0 + jnp.exp(-z))


def _sum11(m):
    return jnp.sum(jnp.sum(m, axis=-1, keepdims=True), axis=0, keepdims=True)


def _full(shape):
    n = len(shape)
    return pl.BlockSpec(shape, lambda *_: (0,) * n)


def _my_pos():
    return lax.axis_index("x"), lax.axis_index("y"), lax.axis_index("c")


def _peer(k, x, y, c):
    return ((1 - x) if k & 4 else x, (1 - y) if k & 2 else y, (1 - c) if k & 1 else c)


def _small_allgather(v, name):
    R, L = v.shape

    def body(v_ref, out_ref, send_sems, recv_sems):
        x, y, c = _my_pos()
        me = 4 * x + 2 * y + c
        out_ref[me] = v_ref[...]
        cps = []
        for k in range(1, NDEV):
            cp = pltpu.make_async_remote_copy(src_ref=v_ref, dst_ref=out_ref.at[me], send_sem=send_sems.at[k - 1],
                                              recv_sem=recv_sems.at[k - 1], device_id=_peer(k, x, y, c), device_id_type=MESH)
            cp.start()
            cps.append(cp)
        for cp in cps:
            cp.wait()

    return pl.pallas_call(
        body, name=name, out_shape=jax.ShapeDtypeStruct((NDEV, R, L), v.dtype),
        in_specs=[pl.BlockSpec(memory_space=pltpu.VMEM)], out_specs=pl.BlockSpec(memory_space=pltpu.VMEM),
        scratch_shapes=[pltpu.SemaphoreType.DMA((NDEV - 1,)), pltpu.SemaphoreType.DMA((NDEV - 1,))],
    )(v)


def _allgather_hbm(arrs, name):
    n = len(arrs)

    def body(*refs):
        ins, outs = refs[:n], refs[n:2 * n]
        send_sems, recv_sems, local_sems = refs[2 * n:]
        x, y, c = _my_pos()
        me, sibling = (x, y, c), (x, y, 1 - c)
        chips = [(1 - x, y), (x, 1 - y), (1 - x, 1 - y)]

        def blk(a, px, py, pc):
            return outs[a].at[4 * px + 2 * py + pc]

        def copy(a, k, block, to, src=None):
            return pltpu.make_async_remote_copy(src_ref=blk(a, *block) if src is None else src, dst_ref=blk(a, *block),
                                                send_sem=send_sems.at[a, k], recv_sem=recv_sems.at[a, k], device_id=to, device_id_type=MESH)

        local, sent = [], []
        for a in range(n):
            mine = pltpu.make_async_copy(ins[a], blk(a, *me), local_sems.at[a])
            mine.start()
            local.append(mine)
            first = [copy(a, 0, me, sibling, src=ins[a])] + [copy(a, 1 + j, me, (*chip, c), src=ins[a]) for j, chip in enumerate(chips)]
            for cp in first:
                cp.start()
            sent += first
        for j, chip in enumerate(chips):
            for a in range(n):
                copy(a, 1 + j, (*chip, c), me).wait_recv()
                cp = copy(a, 4 + j, (*chip, c), sibling)
                cp.start()
                sent.append(cp)
        for a in range(n):
            copy(a, 0, sibling, me).wait_recv()
            for j, chip in enumerate(chips):
                copy(a, 4 + j, (*chip, 1 - c), me).wait_recv()
        for cp in sent:
            cp.wait_send()
        for cp in local:
            cp.wait()

    return pl.pallas_call(
        body, name=name, out_shape=[jax.ShapeDtypeStruct((NDEV,) + a.shape, a.dtype) for a in arrs],
        in_specs=[pl.BlockSpec(memory_space=pl.ANY)] * n, out_specs=[pl.BlockSpec(memory_space=pl.ANY)] * n,
        scratch_shapes=[pltpu.SemaphoreType.DMA((n, NDEV - 1)), pltpu.SemaphoreType.DMA((n, NDEV - 1)), pltpu.SemaphoreType.DMA((n,))],
    )(*arrs)


def _in_set(idx, dests):
    p = idx == dests[0]
    for d in dests[1:]:
        p = jnp.logical_or(p, idx == d)
    return p


def _host_call(body, xs, *, name, grid, in_specs, out_specs, out_shape, scratch_shapes, operands, compiler_params):
    nx, nin, nout, nscr = len(xs), len(operands), len(out_shape), len(scratch_shapes)
    ops, specs, aliases = list(operands), list(in_specs), {}
    oshape, ospecs = list(out_shape), list(out_specs)
    any_spec = pl.BlockSpec(memory_space=pl.ANY)
    for a, (send, dests, recv) in enumerate(xs):
        ops.append(send)
        specs.append(any_spec)
        if recv is not None:
            aliases[len(ops)] = nout + a
            ops.append(recv)
            specs.append(any_spec)
            oshape.append(jax.ShapeDtypeStruct(recv.shape, recv.dtype))
        else:
            oshape.append(jax.ShapeDtypeStruct((NDEV,) + send.shape[1:], send.dtype))
        ospecs.append(any_spec)
    ntot_in = len(ops)

    def wrapped(*refs):
        host_in = refs[:nin]
        sends, pos = [], nin
        for (_, _, recv) in xs:
            sends.append(refs[pos])
            pos += 1 if recv is None else 2
        host_out = refs[ntot_in:ntot_in + nout]
        recvs = refs[ntot_in + nout:ntot_in + nout + nx]
        host_scr = refs[ntot_in + nout + nx:ntot_in + nout + nx + nscr]
        send_sems, recv_sems, local_sems = refs[ntot_in + nout + nx + nscr:]
        first = pl.program_id(0) == 0
        last = pl.program_id(0) == grid[0] - 1
        for ax in range(1, len(grid)):
            first = jnp.logical_and(first, pl.program_id(ax) == 0)
            last = jnp.logical_and(last, pl.program_id(ax) == grid[ax] - 1)
        x, y, c = _my_pos()
        me = 4 * x + 2 * y + c

        def each(fn_remote, fn_local):
            for a, (_, dests, _) in enumerate(xs):
                lo, nd = dests[0], len(dests)
                for k in range(1, NDEV):
                    px, py, pc = _peer(k, x, y, c)
                    pidx = 4 * px + 2 * py + pc
                    cp = pltpu.make_async_remote_copy(src_ref=sends[a].at[jnp.clip(pidx - lo, 0, nd - 1)], dst_ref=recvs[a].at[me],
                                                      send_sem=send_sems.at[a, k - 1], recv_sem=recv_sems.at[a, k - 1],
                                                      device_id=(px, py, pc), device_id_type=MESH)
                    fn_remote(cp, _in_set(pidx, dests), _in_set(me, dests))
                lc = pltpu.make_async_copy(sends[a].at[jnp.clip(me - lo, 0, nd - 1)], recvs[a].at[me], local_sems.at[a])
                fn_local(lc, _in_set(me, dests))

        def start_remote(cp, to_dest, _):
            pl.when(jnp.logical_and(first, to_dest))(cp.start)

        def start_local(lc, i_am_dest):
            pl.when(jnp.logical_and(first, i_am_dest))(lc.start)

        def wait_remote(cp, to_dest, i_am_dest):
            pl.when(jnp.logical_and(last, to_dest))(cp.wait_send)
            pl.when(jnp.logical_and(last, i_am_dest))(cp.wait_recv)

        def wait_local(lc, i_am_dest):
            pl.when(jnp.logical_and(last, i_am_dest))(lc.wait)

        each(start_remote, start_local)
        body(*host_in, *host_out, *host_scr)
        each(wait_remote, wait_local)

    res = pl.pallas_call(
        wrapped, name=name, grid=grid, in_specs=specs, out_specs=ospecs, out_shape=oshape, input_output_aliases=aliases,
        scratch_shapes=list(scratch_shapes) + [pltpu.SemaphoreType.DMA((nx, NDEV - 1)), pltpu.SemaphoreType.DMA((nx, NDEV - 1)),
                                               pltpu.SemaphoreType.DMA((nx,))],
        compiler_params=compiler_params,
    )(*ops)
    return res[:nout], res[nout:]


def _mm_tn(a, b, name):
    S, M = a.shape
    N = b.shape[1]
    tk = min(512, S)
    tn = N if N <= 768 else (640 if N % 640 == 0 else 512)
    nk = S // tk

    def body(a_ref, b_ref, o_ref):
        @pl.when(pl.program_id(1) == 0)
        def _():
            o_ref[...] = jnp.zeros_like(o_ref)
        o_ref[...] += _dot(a_ref[...], b_ref[...], TN)

    return pl.pallas_call(
        body, name=name, out_shape=jax.ShapeDtypeStruct((M, N), F32), grid=(N // tn, nk),
        in_specs=[pl.BlockSpec((tk, M), lambda j, k: (k, 0)), pl.BlockSpec((tk, tn), lambda j, k: (k, j))],
        out_specs=pl.BlockSpec((M, tn), lambda j, k: (0, j)),
        compiler_params=_cp(("parallel", "arbitrary"), VMEM_BIG),
    )(a, b)


def _mod_shard(c_pad, w_ada_s, b_ada_s):
    def body(c_ref, w_ref, b_ref, o_ref, ca_ref):
        cv = c_ref[...]
        ca = (cv * _sigmoid(cv)).astype(BF16)
        ca_ref[...] = ca
        o_ref[...] = _dot(ca, w_ref[...].astype(BF16)) + b_ref[...]

    return pl.pallas_call(
        body, name="mod_shard", out_shape=[jax.ShapeDtypeStruct((16, w_ada_s.shape[1]), F32), jax.ShapeDtypeStruct((16, D), BF16)],
    )(c_pad, w_ada_s, b_ada_s)


def _fwd_in(x, mod, g_pre, w_p):
    S = x.shape[0]
    tm, tn = min(512, S), P_W // 2

    def body(x_ref, mod_ref, g_ref, w_ref, p_ref, h_ref):
        @pl.when(pl.program_id(1) == 0)
        def _():
            xv = x_ref[...]
            r = lax.rsqrt(jnp.mean(xv * xv, axis=-1, keepdims=True) + EPS)
            h = ((xv * r) * g_ref[...]) * (1.0 + mod_ref[1:2, :]) + mod_ref[0:1, :]
            h_ref[...] = h.astype(BF16)
        p_ref[...] = _dot(h_ref[...], w_ref[...])

    return pl.pallas_call(
        body, name="fwd_in", out_shape=[jax.ShapeDtypeStruct((S, P_W), F32), jax.ShapeDtypeStruct((S, D), BF16)],
        grid=(S // tm, P_W // tn),
        in_specs=[pl.BlockSpec((tm, D), lambda i, j: (i, 0)), _full((3, D)), _full((1, D)), pl.BlockSpec((D, tn), lambda i, j: (0, j))],
        out_specs=[pl.BlockSpec((tm, tn), lambda i, j: (i, j)), pl.BlockSpec((tm, D), lambda i, j: (i, 0))],
        compiler_params=_cp(("parallel", "arbitrary"), VMEM_BIG),
    )(x, mod, g_pre, w_p)


def _swap16(v):
    lane = lax.broadcasted_iota(jnp.int32, v.shape, 1)
    return jnp.where((lane % 32) < 16, pltpu.roll(v, 112, 1), pltpu.roll(v, 16, 1))


def _rope(v, cos, sin):
    return v * cos + _swap16(v) * sin


def _rope_t(v, cos, sin):
    return v * cos - _swap16(v) * sin


def _head_mean(v):
    lo = lax.broadcasted_iota(jnp.int32, v.shape, 1) < 64
    m0 = jnp.sum(jnp.where(lo, v, 0.0), axis=-1, keepdims=True)
    m1 = jnp.sum(jnp.where(lo, 0.0, v), axis=-1, keepdims=True)
    return jnp.where(lo, m0, m1) * (1.0 / 64.0)


def _prep(p, cos, sin, qg, kg):
    S = p.shape[0]
    tm = min(512, S)

    def body(qa_ref, kv_ref, qr_ref, kr_ref, cos_ref, sin_ref, qg_ref, kg_ref, qt_ref, kh_ref, kt_ref, vh_ref, vta_ref, qr2_ref, kr2_ref):
        cos_v, sin_v = cos_ref[...], sin_ref[...]
        for g in range(4):
            xv = qa_ref[:, 128 * g:128 * g + 128]
            r = lax.rsqrt(_head_mean(xv * xv) + EPS)
            yt = (_rope((xv * r) * qg_ref[...], cos_v, sin_v) * (0.125 * LOG2E)).T
            qt_ref[2 * g] = yt[:DH].astype(BF16)
            qt_ref[2 * g + 1] = yt[DH:].astype(BF16)
        xv = kv_ref[:, :128]
        r = lax.rsqrt(_head_mean(xv * xv) + EPS)
        yv = _rope((xv * r) * kg_ref[...], cos_v, sin_v)
        kh_ref[0] = yv[:, :64].astype(BF16)
        kh_ref[1] = yv[:, 64:].astype(BF16)
        yt = yv.T
        kt_ref[0] = yt[:DH].astype(BF16)
        kt_ref[1] = yt[DH:].astype(BF16)
        vv = kv_ref[:, 128:]
        vh_ref[0] = vv[:, :64].astype(BF16)
        vh_ref[1] = vv[:, 64:].astype(BF16)
        vt = vv.T
        tail = (lax.broadcasted_iota(jnp.int32, (DHA - DH, tm), 0) == 0).astype(BF16)
        for kvh in range(2):
            vta_ref[kvh, 0:DH, :] = vt[DH * kvh:DH * kvh + DH].astype(BF16)
            vta_ref[kvh, DH:DHA, :] = tail
        for g in range(2):
            sl = slice(128 * g, 128 * g + 128)
            qr2_ref[:, sl] = _rope(qr_ref[:, sl], cos_v, sin_v)
            kr2_ref[:, sl] = _rope(kr_ref[:, sl], cos_v, sin_v) * 0.125

    hm = lambda n: pl.BlockSpec((n, tm, DH), lambda i: (0, i, 0))
    ht = lambda n, r: pl.BlockSpec((n, r, tm), lambda i: (0, 0, i))
    return pl.pallas_call(
        body, name="prep",
        out_shape=[jax.ShapeDtypeStruct((8, DH, S), BF16), jax.ShapeDtypeStruct((2, S, DH), BF16), jax.ShapeDtypeStruct((2, DH, S), BF16),
                   jax.ShapeDtypeStruct((2, S, DH), BF16), jax.ShapeDtypeStruct((2, DHA, S), BF16),
                   jax.ShapeDtypeStruct((S, 256), F32), jax.ShapeDtypeStruct((S, 256), F32)],
        grid=(S // tm,),
        in_specs=[pl.BlockSpec((tm, 512), lambda i: (i, O_QA // 512)), pl.BlockSpec((tm, 256), lambda i: (i, O_KA // 256)),
                  pl.BlockSpec((tm, 256), lambda i: (i, O_QR // 256)), pl.BlockSpec((tm, 256), lambda i: (i, O_KR // 256)),
                  pl.BlockSpec((tm, 128), lambda i: (i, 0)), pl.BlockSpec((tm, 128), lambda i: (i, 0)), _full((1, 128)), _full((1, 128))],
        out_specs=[ht(8, DH), hm(2), ht(2, DH), hm(2), ht(2, DHA), pl.BlockSpec((tm, 256), lambda i: (i, 0)), pl.BlockSpec((tm, 256), lambda i: (i, 0))],
        compiler_params=_cp(("parallel",)),
    )(p, p, p, p, cos, sin, qg, kg)


def _attn_fwd(qt, kh, vta):
    S = qt.shape[2]
    tq, tk = min(512, S), min(512, S)
    nj = S // tk

    def body(q_ref, k_ref, v_ref, o_ref, ot_ref, lse_ref, m_s, acc_s):
        j = pl.program_id(2)

        @pl.when(j == 0)
        def _():
            m_s[...] = jnp.full_like(m_s, -jnp.inf)
            acc_s[...] = jnp.zeros_like(acc_s)

        k, v = k_ref[0], v_ref[0]
        m_all = m_s[...]
        st = {0: _dot(k, q_ref[0])}
        m_new, acc_new = [], []
        for h in range(4):
            if h + 1 < 4:
                st[h + 1] = _dot(k, q_ref[h + 1])
            m_old = m_all[h:h + 1, :]
            mn = jnp.maximum(m_old, jnp.max(st[h], axis=0, keepdims=True))
            pt = jnp.exp2(st[h] - mn).astype(BF16)
            acc_new.append(jnp.exp2(m_old - mn) * acc_s[h] + _dot(v, pt))
            m_new.append(mn)
            del st[h]
        for h in range(4):
            acc_s[h] = acc_new[h]
            m_s[h:h + 1, :] = m_new[h]

        @pl.when(j == nj - 1)
        def _():
            for h in range(4):
                ot = acc_s[h, 0:DH, :] / acc_s[h, DH:DH + 1, :]
                ot_ref[h] = ot
                o_ref[:, DH * h:DH * h + DH] = ot.T
                lse_ref[0, h:h + 1, :] = m_s[h:h + 1, :] + jnp.log2(acc_s[h, DH:DH + 1, :])

    return pl.pallas_call(
        body, name="attn_fwd",
        out_shape=[jax.ShapeDtypeStruct((S, 512), F32), jax.ShapeDtypeStruct((8, DH, S), F32), jax.ShapeDtypeStruct((2, 4, S), F32)],
        grid=(2, S // tq, nj),
        in_specs=[pl.BlockSpec((4, DH, tq), lambda g, i, j: (g, 0, i)), pl.BlockSpec((1, tk, DH), lambda g, i, j: (g, j, 0)),
                  pl.BlockSpec((1, DHA, tk), lambda g, i, j: (g, 0, j))],
        out_specs=[pl.BlockSpec((tq, 256), lambda g, i, j: (i, g)), pl.BlockSpec((4, DH, tq), lambda g, i, j: (g, 0, i)),
                   pl.BlockSpec((1, 4, tq), lambda g, i, j: (g, 0, i))],
        scratch_shapes=[pltpu.VMEM((8, tq), F32), pltpu.VMEM((4, DHA, tq), F32)],
        compiler_params=_cp(("parallel", "parallel", "arbitrary"), VMEM_BIG),
    )(qt, kh, vta)


def _ret_tables(wf, wb):
    C = CH

    def body(wf_ref, wb_ref, dc_ref, qdf_ref, qdb_ref, kdf_ref, kdb_ref, a_ref):
        def logsig(w):
            z = jnp.exp(-jnp.abs(w))
            u = 1.0 + z
            l1p = jnp.where(u == 1.0, z, jnp.log(u) * (z / jnp.where(u == 1.0, 1.0, u - 1.0)))
            return jnp.minimum(w, 0.0) - l1p

        lgf, lgb = logsig(wf_ref[...]), logsig(wb_ref[...])
        lane4 = lax.broadcasted_iota(jnp.int32, (1, 4), 1)

        def pick(lg, h):
            return jnp.sum(jnp.where(lane4 == h, lg, 0.0), axis=-1, keepdims=True)

        ii = lax.broadcasted_iota(jnp.int32, (C, C), 0).astype(F32)
        jj = lax.broadcasted_iota(jnp.int32, (C, C), 1).astype(F32)
        dif = ii - jj
        hd = lax.broadcasted_iota(jnp.int32, (C, 256), 1) // DH
        lf_l = jnp.zeros((C, 256), F32)
        lb_l = jnp.zeros((C, 256), F32)
        for h in range(HR):
            lf, lb = pick(lgf, h), pick(lgb, h)
            dc_ref[h] = jnp.where(dif >= 0, jnp.exp(lf * jnp.maximum(dif, 0.0)), jnp.exp(lb * jnp.maximum(-dif, 0.0)))
            lf_l = jnp.where(hd == h, lf, lf_l)
            lb_l = jnp.where(hd == h, lb, lb_l)
            a_ref[h:h + 1, :] = jnp.broadcast_to(jnp.exp(lf * C), (1, 128))
            a_ref[HR + h:HR + h + 1, :] = jnp.broadcast_to(jnp.exp(lb * C), (1, 128))
        ri = lax.broadcasted_iota(jnp.int32, (C, 256), 0).astype(F32)
        qdf_ref[...] = jnp.exp(lf_l * (ri + 1.0))
        qdb_ref[...] = jnp.exp(lb_l * (C - ri))
        kdf_ref[...] = jnp.exp(lf_l * (C - 1.0 - ri))
        kdb_ref[...] = jnp.exp(lb_l * ri)

    t = jax.ShapeDtypeStruct((C, 256), F32)
    return pl.pallas_call(body, name="ret_tables",
                          out_shape=[jax.ShapeDtypeStruct((HR, C, C), F32), t, t, t, t, jax.ShapeDtypeStruct((8, 128), F32)])(wf, wb)


def _ret_states(kr2, p, kdf, kdb, adec):
    S = kr2.shape[0]
    C, N = CH, S // CH

    def body(kf_ref, vf_ref, kb_ref, vb_ref, kdf_ref, kdb_ref, a_ref, rf_ref, rb_ref, sf, sb):
        @pl.when(pl.program_id(0) == 0)
        def _():
            sf[...] = jnp.zeros_like(sf)
            sb[...] = jnp.zeros_like(sb)

        rf_ref[0] = sf[...]
        rb_ref[0] = sb[...]
        kdfw = (kf_ref[...] * kdf_ref[...]).astype(BF16)
        kdbw = (kb_ref[...] * kdb_ref[...]).astype(BF16)
        vf, vb = vf_ref[...].astype(BF16), vb_ref[...].astype(BF16)
        for h in range(HR):
            ks, vs = slice(DH * h, DH * h + DH), slice(DV * h, DV * h + DV)
            sf[h] = a_ref[h:h + 1, :] * sf[h] + _dot(kdfw[:, ks], vf[:, vs], TN)
            sb[h] = a_ref[HR + h:HR + h + 1, :] * sb[h] + _dot(kdbw[:, ks], vb[:, vs], TN)

    st = jax.ShapeDtypeStruct((N, HR, DH, DV), F32)
    return pl.pallas_call(
        body, name="ret_states", out_shape=[st, st], grid=(N,),
        in_specs=[pl.BlockSpec((C, 256), lambda t: (t, 0)), pl.BlockSpec((C, 512), lambda t: (t, O_VR // 512)),
                  pl.BlockSpec((C, 256), lambda t: (N - 1 - t, 0)), pl.BlockSpec((C, 512), lambda t: (N - 1 - t, O_VR // 512)),
                  _full((C, 256)), _full((C, 256)), _full((8, 128))],
        out_specs=[pl.BlockSpec((1, HR, DH, DV), lambda t: (t, 0, 0, 0)), pl.BlockSpec((1, HR, DH, DV), lambda t: (N - 1 - t, 0, 0, 0))],
        scratch_shapes=[pltpu.VMEM((HR, DH, DV), F32), pltpu.VMEM((HR, DH, DV), F32)],
        compiler_params=_cp(("arbitrary",)),
    )(kr2, p, kr2, p, kdf, kdb, adec)


def _ret_head_fwd(h, qb, kb, vb, qfw, qbw, dc_ref, rf_ref, rb_ref):
    ks, vs = slice(DH * h, DH * h + DH), slice(DV * h, DV * h + DV)
    sd = _dot(qb[:, ks], kb[:, ks], NT) * dc_ref[h]
    o = _dot(sd.astype(BF16), vb[:, vs]) + _dot(qfw[:, ks], rf_ref[0, h].astype(BF16)) + _dot(qbw[:, ks], rb_ref[0, h].astype(BF16))
    return sd, o


def _ret_out(qr2, kr2, p, rf, rb, dc, qdf, qdb, gn):
    S = qr2.shape[0]
    C, N = CH, S // CH

    def body(q_ref, k_ref, v_ref, z_ref, rf_ref, rb_ref, dc_ref, qdf_ref, qdb_ref, gn_ref, yr_ref):
        qv = q_ref[...]
        qb, kb, vb = qv.astype(BF16), k_ref[...].astype(BF16), v_ref[...].astype(BF16)
        qfw, qbw = (qv * qdf_ref[...]).astype(BF16), (qv * qdb_ref[...]).astype(BF16)
        for h in range(HR):
            vs = slice(DV * h, DV * h + DV)
            _, o = _ret_head_fwd(h, qb, kb, vb, qfw, qbw, dc_ref, rf_ref, rb_ref)
            mu = jnp.mean(o, axis=-1, keepdims=True)
            var = jnp.mean(jnp.square(o - mu), axis=-1, keepdims=True)
            on = (o - mu) * lax.rsqrt(var + EPS)
            z = z_ref[:, vs]
            yr_ref[:, vs] = ((on * gn_ref[:, vs]) * (z * _sigmoid(z))).astype(BF16)

    return pl.pallas_call(
        body, name="ret_out", out_shape=jax.ShapeDtypeStruct((S, 512), BF16), grid=(N,),
        in_specs=[pl.BlockSpec((C, 256), lambda t: (t, 0)), pl.BlockSpec((C, 256), lambda t: (t, 0)),
                  pl.BlockSpec((C, 512), lambda t: (t, O_VR // 512)), pl.BlockSpec((C, 512), lambda t: (t, O_ZR // 512)),
                  pl.BlockSpec((1, HR, DH, DV), lambda t: (t, 0, 0, 0)), pl.BlockSpec((1, HR, DH, DV), lambda t: (t, 0, 0, 0)),
                  _full((HR, C, C)), _full((C, 256)), _full((C, 256)), _full((1, 512))],
        out_specs=pl.BlockSpec((C, 512), lambda t: (t, 0)),
        compiler_params=_cp(("parallel",)),
    )(qr2, kr2, p, p, rf, rb, dc, qdf, qdb, gn)


def _mid(x, tgt, mod, g_post, o_att, p, yr, w_pa, w_pr, w_out):
    S = x.shape[0]
    tm = min(256, S)

    def body(x_ref, t_ref, mod_ref, gp_ref, o_ref, za_ref, gl_ref, yr_ref, wpa_ref, wpr_ref, wout_ref,
             dout_ref, do_ref, dpm_ref, dyr_ref, mb_ref, dub_ref, yab_ref, dab_ref, drb_ref, sums_ref):
        @pl.when(pl.program_id(0) == 0)
        def _():
            sums_ref[...] = jnp.zeros_like(sums_ref)

        za = za_ref[...]
        sa = _sigmoid(za)
        sil = za * sa
        ov = o_ref[...]
        ya_b = (ov * sil).astype(BF16)
        yr_b = yr_ref[...]
        av = _dot(ya_b, wpa_ref[...])
        rv = _dot(yr_b, wpr_ref[...])
        ga = _sigmoid(gl_ref[:, :D])
        gr = _sigmoid(gl_ref[:, D:])
        mb = (ga * av + gr * rv).astype(BF16)
        u = _dot(mb, wout_ref[...])
        r2 = lax.rsqrt(jnp.mean(u * u, axis=-1, keepdims=True) + EPS)
        un = u * r2
        gp = gp_ref[...]
        yv = un * gp
        gate = mod_ref[2:3, :]
        err = (x_ref[...] + gate * yv) - t_ref[...]
        dout = err * (1.0 / D)
        dout_ref[...] = dout
        dy = dout * gate
        sums_ref[0:1, :] += jnp.sum(dout * yv, axis=0, keepdims=True)
        sums_ref[1:2, :] += jnp.sum(dy * un, axis=0, keepdims=True)
        sums_ref[2:3, :] += jnp.sum(err * err, axis=0, keepdims=True)
        dyg = dy * gp
        du_b = (r2 * (dyg - un * jnp.mean(dyg * un, axis=-1, keepdims=True))).astype(BF16)
        dm = _dot(du_b, wout_ref[...], NT)
        da_b = (dm * ga).astype(BF16)
        dr_b = (dm * gr).astype(BF16)
        dpm_ref[:, :D] = (dm * av * (ga * (1.0 - ga))).astype(BF16)
        dpm_ref[:, D:2 * D] = (dm * rv * (gr * (1.0 - gr))).astype(BF16)
        dya = _dot(da_b, wpa_ref[...], NT)
        dyr_ref[...] = _dot(dr_b, wpr_ref[...], NT)
        dov = dya * sil
        for g in range(4):
            dt = dov[:, 128 * g:128 * g + 128].T
            do_ref[2 * g] = dt[:DH].astype(BF16)
            do_ref[2 * g + 1] = dt[DH:].astype(BF16)
        dpm_ref[:, 2 * D:] = (dya * ov * (sa * (1.0 + za * (1.0 - sa)))).astype(BF16)
        mb_ref[...] = mb
        dub_ref[...] = du_b
        yab_ref[...] = ya_b
        dab_ref[...] = da_b
        drb_ref[...] = dr_b

    row = lambda w: pl.BlockSpec((tm, w), lambda i: (i, 0))
    sd = lambda w, dt: jax.ShapeDtypeStruct((S, w), dt)
    return pl.pallas_call(
        body, name="mid",
        out_shape=[sd(D, F32), jax.ShapeDtypeStruct((8, DH, S), BF16), sd(2560, BF16), sd(512, F32), sd(D, BF16), sd(D, BF16), sd(512, BF16),
                   sd(D, BF16), sd(D, BF16), jax.ShapeDtypeStruct((8, D), F32)],
        grid=(S // tm,),
        in_specs=[row(D), row(D), _full((3, D)), _full((1, D)), row(512), pl.BlockSpec((tm, 512), lambda i: (i, O_ZA // 512)),
                  pl.BlockSpec((tm, 2048), lambda i: (i, 0)), row(512), _full((512, D)), _full((512, D)), _full((D, D))],
        out_specs=[row(D), pl.BlockSpec((8, DH, tm), lambda i: (0, 0, i)), row(2560), row(512), row(D), row(D), row(512), row(D), row(D),
                   _full((8, D))],
        compiler_params=_cp(("arbitrary",), VMEM_BIG),
    )(x, tgt, mod, g_post, o_att, p, p, yr, w_pa, w_pr, w_out)


def _attn_bwd(qt, kh, kt, vh, dot_, ot, lse, xs):
    S = qt.shape[2]
    tq, tk = min(512, S), min(512, S)

    def body(q_ref, k_ref, kt_ref, v_ref, do_ref, o_ref, lse_ref, dq_ref, dk_ref, dv_ref):
        j, i = pl.program_id(1), pl.program_id(2)
        cols = pl.ds(pl.multiple_of(i * tq, tq), tq)
        k, kt, v = k_ref[0], kt_ref[0], v_ref[0]
        lse_all = lse_ref[0]
        st = {0: _dot(k, q_ref[0])}
        dpt = {0: _dot(v, do_ref[0])}
        dk_acc, dv_acc, dqs = None, None, []
        for h in range(4):
            if h + 1 < 4:
                st[h + 1] = _dot(k, q_ref[h + 1])
                dpt[h + 1] = _dot(v, do_ref[h + 1])
            qt_h, dot_h = q_ref[h], do_ref[h]
            delta = jnp.sum(dot_h.astype(F32) * o_ref[h], axis=0, keepdims=True)
            pt = jnp.exp2(st[h] - lse_all[h:h + 1, :])
            dst = (pt * (dpt[h] - delta)).astype(BF16)
            dv_h = _dot(dot_h, pt.astype(BF16), NT)
            dk_h = _dot(qt_h, dst, NT)
            dqs.append(_dot(kt, dst))
            dv_acc = dv_h if dv_acc is None else dv_acc + dv_h
            dk_acc = dk_h if dk_acc is None else dk_acc + dk_h
            del st[h], dpt[h]

        @pl.when(i == 0)
        def _():
            dk_ref[0] = dk_acc
            dv_ref[0] = dv_acc

        @pl.when(i > 0)
        def _():
            dk_ref[0] += dk_acc
            dv_ref[0] += dv_acc

        @pl.when(j == 0)
        def _():
            for h in range(4):
                dq_ref[h, :, cols] = dqs[h]

        @pl.when(j > 0)
        def _():
            for h in range(4):
                dq_ref[h, :, cols] += dqs[h]

    return _host_call(
        body, xs, name="attn_bwd",
        out_shape=[jax.ShapeDtypeStruct((8, DH, S), F32), jax.ShapeDtypeStruct((2, DH, S), F32), jax.ShapeDtypeStruct((2, DH, S), F32)],
        grid=(2, S // tk, S // tq),
        in_specs=[pl.BlockSpec((4, DH, tq), lambda g, j, i: (g, 0, i)), pl.BlockSpec((1, tk, DH), lambda g, j, i: (g, j, 0)),
                  pl.BlockSpec((1, DH, tk), lambda g, j, i: (g, 0, j)), pl.BlockSpec((1, tk, DH), lambda g, j, i: (g, j, 0)),
                  pl.BlockSpec((4, DH, tq), lambda g, j, i: (g, 0, i)), pl.BlockSpec((4, DH, tq), lambda g, j, i: (g, 0, i)),
                  pl.BlockSpec((1, 4, tq), lambda g, j, i: (g, 0, i))],
        out_specs=[pl.BlockSpec((4, DH, S), lambda g, j, i: (g, 0, 0)), pl.BlockSpec((1, DH, tk), lambda g, j, i: (g, 0, j)),
                   pl.BlockSpec((1, DH, tk), lambda g, j, i: (g, 0, j))],
        scratch_shapes=[], operands=(qt, kh, kt, vh, dot_, ot, lse),
        compiler_params=_cp(("arbitrary", "arbitrary", "arbitrary"), VMEM_BIG),
    )


def _attn_prep_bwd(dqt, dkt, dvt, p, cos, sin, qg, kg):
    S = dqt.shape[2]
    tm = min(512, S)

    def body(dq_ref, dk_ref, dv_ref, qa_ref, ka_ref, cos_ref, sin_ref, qg_ref, kg_ref, dp_ref, gs_ref):
        @pl.when(pl.program_id(0) == 0)
        def _():
            gs_ref[...] = jnp.zeros_like(gs_ref)

        cos_v, sin_v = cos_ref[...], sin_ref[...]

        def pair(ref, a):
            return jnp.concatenate([ref[a], ref[a + 1]], axis=0).T

        def norm_bwd(dyv, xv, gv, row):
            r = lax.rsqrt(_head_mean(xv * xv) + EPS)
            xn = xv * r
            dxh = _rope_t(dyv, cos_v, sin_v)
            gs_ref[row:row + 1, :] += jnp.sum(dxh * xn, axis=0, keepdims=True)
            dg = dxh * gv
            return r * (dg - xn * _head_mean(dg * xn))

        for g in range(4):
            sl = slice(128 * g, 128 * g + 128)
            dp_ref[:, sl] = norm_bwd(pair(dq_ref, 2 * g) * 0.125, qa_ref[:, sl], qg_ref[...], 0).astype(BF16)
        dp_ref[:, 512:640] = norm_bwd(pair(dk_ref, 0) * LN2, ka_ref[...], kg_ref[...], 1).astype(BF16)
        dp_ref[:, 640:768] = pair(dv_ref, 0).astype(BF16)

    ht = lambda n: pl.BlockSpec((n, DH, tm), lambda i: (0, 0, i))
    return pl.pallas_call(
        body, name="attn_prep_bwd", out_shape=[jax.ShapeDtypeStruct((S, 768), BF16), jax.ShapeDtypeStruct((8, 128), F32)],
        grid=(S // tm,),
        in_specs=[ht(8), ht(2), ht(2),
                  pl.BlockSpec((tm, 512), lambda i: (i, O_QA // 512)), pl.BlockSpec((tm, 128), lambda i: (i, O_KA // 128)),
                  pl.BlockSpec((tm, 128), lambda i: (i, 0)), pl.BlockSpec((tm, 128), lambda i: (i, 0)), _full((1, 128)), _full((1, 128))],
        out_specs=[pl.BlockSpec((tm, 768), lambda i: (i, 0)), _full((8, 128))],
        compiler_params=_cp(("arbitrary",)),
    )(dqt, dkt, dvt, p, p, cos, sin, qg, kg)


def _ret_bwd_chunk(qr2, kr2, p, rf, rb, dc, qdf, qdb, gn, dyr, cos, sin, xs):
    S = qr2.shape[0]
    C, N = CH, S // CH

    def body(q_ref, k_ref, v_ref, z_ref, rf_ref, rb_ref, dc_ref, qdf_ref, qdb_ref, gn_ref, dyr_ref, cos_ref, sin_ref,
             dpa_ref, dk_ref, dv_ref, drf_ref, drb_ref, dgn_ref, dlg_ref, dqs):
        @pl.when(pl.program_id(0) == 0)
        def _():
            dgn_ref[...] = jnp.zeros_like(dgn_ref)
            dlg_ref[...] = jnp.zeros_like(dlg_ref)

        qv = q_ref[...]
        qb, kb, vb = qv.astype(BF16), k_ref[...].astype(BF16), v_ref[...].astype(BF16)
        qf32, qb32 = qv * qdf_ref[...], qv * qdb_ref[...]
        qfw, qbw = qf32.astype(BF16), qb32.astype(BF16)
        ii = lax.broadcasted_iota(jnp.int32, (C, C), 0).astype(F32)
        jj = lax.broadcasted_iota(jnp.int32, (C, C), 1).astype(F32)
        dif = ii - jj
        ri = lax.broadcasted_iota(jnp.int32, (C, 1), 0).astype(F32)
        for h in range(HR):
            ks, vs = slice(DH * h, DH * h + DH), slice(DV * h, DV * h + DV)
            sd, o = _ret_head_fwd(h, qb, kb, vb, qfw, qbw, dc_ref, rf_ref, rb_ref)
            mu = jnp.mean(o, axis=-1, keepdims=True)
            rstd = lax.rsqrt(jnp.mean(jnp.square(o - mu), axis=-1, keepdims=True) + EPS)
            on = (o - mu) * rstd
            z = z_ref[:, vs]
            sz = _sigmoid(z)
            dy = dyr_ref[:, vs]
            gnv = gn_ref[:, vs]
            dpa_ref[:, 256 + DV * h:256 + DV * h + DV] = (dy * (on * gnv) * (sz * (1.0 + z * (1.0 - sz)))).astype(BF16)
            dys = dy * (z * sz)
            dgn_ref[:, vs] += jnp.sum(dys * on, axis=0, keepdims=True)
            don = dys * gnv
            do = rstd * (don - jnp.mean(don, axis=-1, keepdims=True) - on * jnp.mean(don * on, axis=-1, keepdims=True))
            do_b = do.astype(BF16)
            dpm = _dot(do_b, vb[:, vs], NT)
            dv_ref[:, vs] = _dot(sd.astype(BF16), do_b, TN)
            dsd = (dpm * dc_ref[h]).astype(BF16)
            dqf = _dot(do_b, rf_ref[0, h].astype(BF16), NT)
            dqb = _dot(do_b, rb_ref[0, h].astype(BF16), NT)
            dqs[:, ks] = _dot(dsd, kb[:, ks]) + dqf * qdf_ref[:, ks] + dqb * qdb_ref[:, ks]
            dk_ref[:, ks] = _dot(dsd, qb[:, ks], TN)
            drf_ref[0, h] = _dot(qfw[:, ks], do_b, TN)
            drb_ref[0, h] = _dot(qbw[:, ks], do_b, TN)
            e = dpm * sd
            lf = _sum11(e * jnp.maximum(dif, 0.0)) + _sum11(jnp.sum(qf32[:, ks] * dqf, axis=-1, keepdims=True) * (ri + 1.0))
            lb = _sum11(e * jnp.maximum(-dif, 0.0)) + _sum11(jnp.sum(qb32[:, ks] * dqb, axis=-1, keepdims=True) * (C - ri))
            dlg_ref[h:h + 1, :] += jnp.broadcast_to(lf, (1, 128))
            dlg_ref[HR + h:HR + h + 1, :] += jnp.broadcast_to(lb, (1, 128))
        cos_v, sin_v = cos_ref[...], sin_ref[...]
        for g in range(2):
            sl = slice(128 * g, 128 * g + 128)
            dpa_ref[:, sl] = _rope_t(dqs[:, sl], cos_v, sin_v).astype(BF16)

    st = jax.ShapeDtypeStruct((N, HR, DH, DV), F32)
    stb = lambda: pl.BlockSpec((1, HR, DH, DV), lambda t: (t, 0, 0, 0))
    return _host_call(
        body, xs, name="ret_bwd_chunk",
        out_shape=[jax.ShapeDtypeStruct((S, 768), BF16), jax.ShapeDtypeStruct((S, 256), F32), jax.ShapeDtypeStruct((S, 512), F32), st, st,
                   jax.ShapeDtypeStruct((1, 512), F32), jax.ShapeDtypeStruct((8, 128), F32)],
        grid=(N,),
        in_specs=[pl.BlockSpec((C, 256), lambda t: (t, 0)), pl.BlockSpec((C, 256), lambda t: (t, 0)),
                  pl.BlockSpec((C, 512), lambda t: (t, O_VR // 512)), pl.BlockSpec((C, 512), lambda t: (t, O_ZR // 512)),
                  stb(), stb(), _full((HR, C, C)), _full((C, 256)), _full((C, 256)), _full((1, 512)),
                  pl.BlockSpec((C, 512), lambda t: (t, 0)), pl.BlockSpec((C, 128), lambda t: (t, 0)), pl.BlockSpec((C, 128), lambda t: (t, 0))],
        out_specs=[pl.BlockSpec((C, 768), lambda t: (t, 0)), pl.BlockSpec((C, 256), lambda t: (t, 0)), pl.BlockSpec((C, 512), lambda t: (t, 0)),
                   stb(), stb(), _full((1, 512)), _full((8, 128))],
        scratch_shapes=[pltpu.VMEM((C, 256), F32)], operands=(qr2, kr2, p, p, rf, rb, dc, qdf, qdb, gn, dyr, cos, sin),
        compiler_params=_cp(("arbitrary",)),
    )


def _ret_bwd_scan(kr2, p, rf, rb, drf, drb, kdf, kdb, adec):
    S = kr2.shape[0]
    C, N = CH, S // CH

    def body(kf_ref, vf_ref, kb_ref, vb_ref, rf_ref, rb_ref, drf_ref, drb_ref, kdf_ref, kdb_ref, a_ref,
             dkf_ref, dkb_ref, dvf_ref, dvb_ref, dlg_ref, gf, gb):
        @pl.when(pl.program_id(0) == 0)
        def _():
            gf[...] = jnp.zeros_like(gf)
            gb[...] = jnp.zeros_like(gb)
            dlg_ref[...] = jnp.zeros_like(dlg_ref)

        ri = lax.broadcasted_iota(jnp.int32, (C, 1), 0).astype(F32)

        def one(k_ref, v_ref, r_ref, dr_ref, kd_ref, g_s, dk_ref, dv_ref, row0, wexp):
            kd32 = k_ref[...] * kd_ref[...]
            kdw = kd32.astype(BF16)
            vb = v_ref[...].astype(BF16)
            for h in range(HR):
                ks, vs = slice(DH * h, DH * h + DH), slice(DV * h, DV * h + DV)
                gst = g_s[h]
                g_b = gst.astype(BF16)
                dkd = _dot(vb[:, vs], g_b, NT)
                dk_ref[:, ks] = dkd * kd_ref[:, ks]
                dv_ref[:, vs] = _dot(kdw[:, ks], g_b)
                av = a_ref[row0 + h:row0 + h + 1, :]
                lg = (_sum11(jnp.sum(kd32[:, ks] * dkd, axis=-1, keepdims=True) * wexp)
                      + C * av[:, 0:1] * _sum11(r_ref[0, h] * gst))
                dlg_ref[row0 + h:row0 + h + 1, :] += jnp.broadcast_to(lg, (1, 128))
                g_s[h] = dr_ref[0, h] + av * gst

        one(kf_ref, vf_ref, rf_ref, drf_ref, kdf_ref, gf, dkf_ref, dvf_ref, 0, C - 1.0 - ri)
        one(kb_ref, vb_ref, rb_ref, drb_ref, kdb_ref, gb, dkb_ref, dvb_ref, HR, ri)

    fwd = lambda w, off=0: pl.BlockSpec((C, w), lambda t: (N - 1 - t, off))
    bwd = lambda w, off=0: pl.BlockSpec((C, w), lambda t: (t, off))
    stf = lambda: pl.BlockSpec((1, HR, DH, DV), lambda t: (N - 1 - t, 0, 0, 0))
    stb = lambda: pl.BlockSpec((1, HR, DH, DV), lambda t: (t, 0, 0, 0))
    return pl.pallas_call(
        body, name="ret_bwd_scan",
        out_shape=[jax.ShapeDtypeStruct((S, 256), F32), jax.ShapeDtypeStruct((S, 256), F32), jax.ShapeDtypeStruct((S, 512), F32),
                   jax.ShapeDtypeStruct((S, 512), F32), jax.ShapeDtypeStruct((8, 128), F32)],
        grid=(N,),
        in_specs=[fwd(256), fwd(512, O_VR // 512), bwd(256), bwd(512, O_VR // 512), stf(), stb(), stf(), stb(),
                  _full((C, 256)), _full((C, 256)), _full((8, 128))],
        out_specs=[fwd(256), bwd(256), fwd(512), bwd(512), _full((8, 128))],
        scratch_shapes=[pltpu.VMEM((HR, DH, DV), F32), pltpu.VMEM((HR, DH, DV), F32)],
        compiler_params=_cp(("arbitrary",)),
    )(kr2, p, kr2, p, rf, rb, drf, drb, kdf, kdb, adec)


def _ret_bwd_final(dk_i, dkf, dkb, dv_i, dvf, dvb, cos, sin):
    S = dk_i.shape[0]
    tm = min(512, S)

    def body(a_ref, b_ref, c_ref, d_ref, e_ref, f_ref, cos_ref, sin_ref, o_ref):
        o_ref[:, :512] = (d_ref[...] + e_ref[...] + f_ref[...]).astype(BF16)
        cos_v, sin_v = cos_ref[...], sin_ref[...]
        for g in range(2):
            sl = slice(128 * g, 128 * g + 128)
            dk = a_ref[:, sl] + b_ref[:, sl] + c_ref[:, sl]
            o_ref[:, 512 + 128 * g:512 + 128 * g + 128] = (_rope_t(dk, cos_v, sin_v) * 0.125).astype(BF16)

    row = lambda w: pl.BlockSpec((tm, w), lambda i: (i, 0))
    return pl.pallas_call(
        body, name="ret_bwd_final", out_shape=jax.ShapeDtypeStruct((S, 768), BF16), grid=(S // tm,),
        in_specs=[row(256), row(256), row(256), row(512), row(512), row(512), row(128), row(128)], out_specs=row(768),
        compiler_params=_cp(("parallel",)),
    )(dk_i, dkf, dkb, dv_i, dvf, dvb, cos, sin)


def _bwd_in(dpm, dpa, dpra, dprb, w_p, x, dout, mod, g_pre, xs):
    S = x.shape[0]
    tm = min(256, S)

    def body(a_ref, b_ref, c_ref, d_ref, w_ref, x_ref, dout_ref, mod_ref, g_ref, gx_ref, sums_ref):
        @pl.when(pl.program_id(0) == 0)
        def _():
            sums_ref[...] = jnp.zeros_like(sums_ref)

        dh = (_dot(a_ref[...], w_ref[:, :O_QA], NT) + _dot(b_ref[...], w_ref[:, O_QA:O_QR], NT)
              + _dot(c_ref[...], w_ref[:, O_QR:O_VR], NT) + _dot(d_ref[...], w_ref[:, O_VR:], NT))
        xv = x_ref[...]
        r = lax.rsqrt(jnp.mean(xv * xv, axis=-1, keepdims=True) + EPS)
        xn = xv * r
        gv = g_ref[...]
        sc1 = 1.0 + mod_ref[1:2, :]
        sums_ref[0:1, :] += jnp.sum(dh, axis=0, keepdims=True)
        sums_ref[1:2, :] += jnp.sum(dh * (xn * gv), axis=0, keepdims=True)
        sums_ref[2:3, :] += jnp.sum(dh * xn, axis=0, keepdims=True) * sc1
        dxn = dh * (gv * sc1)
        gx_ref[...] = dout_ref[...] + r * (dxn - xn * jnp.mean(dxn * xn, axis=-1, keepdims=True))

    row = lambda w: pl.BlockSpec((tm, w), lambda i: (i, 0))
    return _host_call(
        body, xs, name="bwd_in", out_shape=[jax.ShapeDtypeStruct((S, D), F32), jax.ShapeDtypeStruct((8, D), F32)], grid=(S // tm,),
        in_specs=[row(2560), row(768), row(768), row(768), _full((D, P_W)), row(D), row(D), _full((3, D)), _full((1, D))],
        out_specs=[row(D), _full((8, D))], scratch_shapes=[], operands=(dpm, dpa, dpra, dprb, w_p, x, dout, mod, g_pre),
        compiler_params=_cp(("arbitrary",), VMEM_BIG),
    )


def _small_reduce(gath, wdec128):
    def body(g_ref, w_ref, o_ref):
        acc = g_ref[0]
        for d in range(1, NDEV):
            acc = acc + g_ref[d]
        o_ref[...] = acc
        pk = o_ref[1:2, :]
        qn = pk[:, 2560:2688]
        kn = pk[:, 2688:2816]
        dlg = pk[:, 2816:2944] + pk[:, 2944:3072]
        ddec = dlg * _sigmoid(-w_ref[...])
        loss = (0.5 / D) * jnp.sum(o_ref[2:3, :], axis=-1, keepdims=True)
        o_ref[2:3, :] = jnp.zeros((1, 3072), F32)
        o_ref[2:3, 0:64] = qn[:, :64] + qn[:, 64:]
        o_ref[2:3, 64:128] = kn[:, :64] + kn[:, 64:]
        o_ref[2:3, 128:256] = ddec
        o_ref[2:3, 256:384] = jnp.broadcast_to(loss, (1, 128))

    return pl.pallas_call(body, name="small_reduce", out_shape=jax.ShapeDtypeStruct((8, 3072), F32))(gath, wdec128)


def _adamw(parts, w, m, v, name):
    n, R, L = parts.shape
    tr = 256 if (R % 256 == 0 and R > 256) else R

    def body(p_ref, w_ref, m_ref, v_ref, g_ref, d_ref, nm_ref, nv_ref):
        g = p_ref[0].astype(F32)
        for k in range(1, n):
            g = g + p_ref[k].astype(F32)
        g_ref[...] = g
        m2 = ADAM_B1 * m_ref[...] + (1.0 - ADAM_B1) * g
        v2 = ADAM_B2 * v_ref[...] + (1.0 - ADAM_B2) * jnp.square(g)
        m_hat = m2 / (1.0 - ADAM_B1 ** ADAM_STEP)
        v_hat = v2 / (1.0 - ADAM_B2 ** ADAM_STEP)
        d_ref[...] = -ADAM_LR * (m_hat / (jnp.sqrt(v_hat) + ADAM_EPS) + ADAM_WD * w_ref[...])
        nm_ref[...] = m2
        nv_ref[...] = v2

    blk = pl.BlockSpec((tr, L), lambda i: (i, 0))
    o = jax.ShapeDtypeStruct((R, L), F32)
    return pl.pallas_call(
        body, name=name, out_shape=[o, o, o, o], grid=(R // tr,),
        in_specs=[pl.BlockSpec((n, tr, L), lambda i: (0, i, 0)), blk, blk, blk], out_specs=[blk, blk, blk, blk],
        compiler_params=_cp(("parallel",), VMEM_BIG),
    )(parts, w, m, v)


def _rope_tables(S):
    t = jnp.arange(S)
    row = (t // 64).astype(F32)
    col = (t % 64).astype(F32)
    half = DH // 2
    inv = ROPE_THETA ** (-jnp.arange(0, half, 2, dtype=F32) / half)
    ar, ac = row[:, None] * inv[None, :], col[:, None] * inv[None, :]
    cos64 = jnp.concatenate([jnp.cos(ar), jnp.cos(ar), jnp.cos(ac), jnp.cos(ac)], axis=1)
    sin64 = jnp.concatenate([-jnp.sin(ar), jnp.sin(ar), -jnp.sin(ac), jnp.sin(ac)], axis=1)
    return jnp.tile(cos64, (1, 2)), jnp.tile(sin64, (1, 2))


def _to_p_order(w_orig):
    return jnp.concatenate([w_orig[:, ORIG[n][0]:ORIG[n][1]] for n in P_ORDER], axis=1)


def _pad_lanes(v, n):
    return jnp.pad(v, ((0, 0), (0, n - v.shape[1])))


def kernel(x, c, w_ada, b_ada, g_pre, w_in, qn_g, kn_g, w_dec_f, w_dec_b, gn_g, w_pa, w_pr, w_out, g_post, loss_target, m_w_ada, m_b_ada, m_g_pre, m_w_in, m_qn_g, m_kn_g, m_w_dec_f, m_w_dec_b, m_gn_g, m_w_pa, m_w_pr, m_w_out, m_g_post, v_w_ada, v_b_ada, v_g_pre, v_w_in, v_qn_g, v_kn_g, v_w_dec_f, v_w_dec_b, v_gn_g, v_w_pa, v_w_pr, v_w_out, v_g_post):
    S = x.shape[1]
    me = 4 * lax.axis_index("x") + 2 * lax.axis_index("y") + lax.axis_index("c")
    xs, tgt = x[0], loss_target[0]
    ncol_ada = w_ada.shape[2]
    ncol_in = w_in.shape[2]

    c_all = _small_allgather(jnp.pad(c, ((0, 7), (0, 0))), "ag_c")[:, 0, :]
    b_ada_s = lax.dynamic_slice(b_ada, (0, me * ncol_ada), (1, ncol_ada))
    mod_s, c_act = _mod_shard(jnp.pad(c_all, ((0, 8), (0, 0))), w_ada[0], b_ada_s)
    mod_all = _small_allgather(mod_s[:8], "ag_mod")
    mod = lax.dynamic_index_in_dim(mod_all, me, axis=1, keepdims=False).reshape(3, D)

    wg_in, wg_pa, wg_pr, wg_out = _allgather_hbm(
        [w_in[0].astype(BF16), w_pa[0].astype(BF16), w_pr[0].astype(BF16), w_out[0].astype(BF16)], "ag_weights")
    w_p = _to_p_order(wg_in.transpose(1, 0, 2).reshape(D, NDEV * ncol_in))
    w_pa_f = wg_pa.transpose(1, 0, 2).reshape(512, D)
    w_pr_f = wg_pr.transpose(1, 0, 2).reshape(512, D)
    w_out_f = wg_out.reshape(D, D)

    cos, sin = _rope_tables(S)
    qg, kg = jnp.tile(qn_g, (1, 2)), jnp.tile(kn_g, (1, 2))

    p, h = _fwd_in(xs, mod, g_pre, w_p)
    qt, kh, kt, vh, vta, qr2, kr2 = _prep(p, cos, sin, qg, kg)
    o_att, o_t, lse = _attn_fwd(qt, kh, vta)
    dc, qdf, qdb, kdf, kdb, adec = _ret_tables(w_dec_f, w_dec_b)
    rf, rb = _ret_states(kr2, p, kdf, kdb, adec)
    yr = _ret_out(qr2, kr2, p, rf, rb, dc, qdf, qdb, gn_g)

    dout, do, dpm, dyr, mb, dub, yab, dab, drb_, sums_mid = _mid(xs, tgt, mod, g_post, o_att, p, yr, w_pa_f, w_pr_f, w_out_f)
    gw_out = _mm_tn(mb, dub, "gw_out")
    gw_pa = _mm_tn(yab, dab, "gw_pa")
    gw_pr = _mm_tn(yr, drb_, "gw_pr")
    gi_m = _mm_tn(h, dpm, "gw_in_mid")

    def shards(cols, nd):
        return cols.astype(BF16).reshape(D, nd, ncol_in).transpose(1, 0, 2)

    all_dev = tuple(range(NDEV))
    (dqt, dkt, dvt), (rs_out, rs_pa, rs_pr, rs_in) = _attn_bwd(qt, kh, kt, vh, do, o_t, lse, [
        (gw_out.astype(BF16).reshape(NDEV, 128, D), all_dev, None),
        (gw_pa.astype(BF16).reshape(512, NDEV, 128).transpose(1, 0, 2), all_dev, None),
        (gw_pr.astype(BF16).reshape(512, NDEV, 128).transpose(1, 0, 2), all_dev, None),
        (shards(gi_m[:, 224:2048], 3), (5, 6, 7), None)])
    dpa, gs_att = _attn_prep_bwd(dqt, dkt, dvt, p, cos, sin, qg, kg)
    gi_a = _mm_tn(h, dpa, "gw_in_att")
    (dpra, dk_i, dv_i, drf, drb, dgn, dlg1), (rs_in,) = _ret_bwd_chunk(qr2, kr2, p, rf, rb, dc, qdf, qdb, gn_g, dyr, cos, sin, [
        (shards(jnp.concatenate([gi_a, gi_m[:, 2048:2496]], axis=1), 2), (0, 1), rs_in)])
    dkf, dkb, dvf, dvb, dlg2 = _ret_bwd_scan(kr2, p, rf, rb, drf, drb, kdf, kdb, adec)
    dprb = _ret_bwd_final(dk_i, dkf, dkb, dv_i, dvf, dvb, cos, sin)
    gi_ra = _mm_tn(h, dpra, "gw_in_reta")
    gi_rb = _mm_tn(h, dprb, "gw_in_retb")
    (grad_x, sums_in), (rs_in,) = _bwd_in(dpm, dpa, dpra, dprb, w_p, xs, dout, mod, g_pre, [
        (shards(jnp.concatenate([gi_m[:, 2496:2560], gi_ra[:, :256], gi_rb[:, 512:768], gi_rb[:, :512], gi_ra[:, 256:768],
                                 gi_m[:, :224]], axis=1), 3), (2, 3, 4), rs_in)])

    z128 = jnp.zeros((1, 120), F32)
    packed = jnp.concatenate([sums_in[2:3], sums_mid[1:2], dgn, gs_att[0:1], gs_att[1:2],
                              dlg1[:, 0].reshape(1, 8), z128, dlg2[:, 0].reshape(1, 8), z128], axis=1)
    mine = jnp.concatenate([jnp.concatenate([sums_in[0:1], sums_in[1:2], sums_mid[0:1]], axis=1), packed,
                            _pad_lanes(sums_mid[2:3], 3072), jnp.zeros((5, 3072), F32)], axis=0)
    gath = _small_allgather(mine, "ag_small")
    wdec128 = _pad_lanes(jnp.concatenate([w_dec_f, w_dec_b], axis=1), 128)
    red = _small_reduce(gath, wdec128)
    loss = red[2, 256]
    g_b_ada = red[0:1]
    g_small = jnp.concatenate([red[0:1], red[1:2, 0:2560], red[2:3, 0:128], red[2:3, 128:136]], axis=1)
    n_small = g_small.shape[1]
    cat = lambda *a: _pad_lanes(jnp.concatenate(a, axis=1), 5888)
    sm_g, sm_d, sm_m, sm_v = _adamw(_pad_lanes(g_small, 5888)[None],
                                    cat(b_ada, g_pre, g_post, gn_g, qn_g, kn_g, w_dec_f, w_dec_b),
                                    cat(m_b_ada, m_g_pre, m_g_post, m_gn_g, m_qn_g, m_kn_g, m_w_dec_f, m_w_dec_b),
                                    cat(v_b_ada, v_g_pre, v_g_post, v_gn_g, v_qn_g, v_kn_g, v_w_dec_f, v_w_dec_b), "adamw_small")
    offs = dict(b_ada=(0, 3072), g_pre=(3072, 4096), g_post=(4096, 5120), gn_g=(5120, 5632), qn_g=(5632, 5696), kn_g=(5696, 5760),
                w_dec_f=(5760, 5764), w_dec_b=(5764, 5768))
    del n_small, g_b_ada

    dmod_all = lax.dynamic_slice(gath[:, 0, :], (0, me * ncol_ada), (NDEV, ncol_ada))
    g_ada = _mm_tn(c_act, jnp.pad(dmod_all, ((0, 8), (0, 0))).astype(BF16), "gw_ada")

    res = dict(
        w_ada=_adamw(g_ada[None], w_ada[0], m_w_ada[0], v_w_ada[0], "adamw_ada"),
        w_in=_adamw(rs_in, w_in[0], m_w_in[0], v_w_in[0], "adamw_in"),
        w_pa=_adamw(rs_pa, w_pa[0], m_w_pa[0], v_w_pa[0], "adamw_pa"),
        w_pr=_adamw(rs_pr, w_pr[0], m_w_pr[0], v_w_pr[0], "adamw_pr"),
        w_out=_adamw(rs_out, w_out[0], m_w_out[0], v_w_out[0], "adamw_out"),
    )
    names = ["w_ada", "b_ada", "g_pre", "w_in", "qn_g", "kn_g", "w_dec_f", "w_dec_b", "gn_g", "w_pa", "w_pr", "w_out", "g_post"]
    outs = [[], [], [], []]
    for nme in names:
        for q in range(4):
            if nme in res:
                outs[q].append(res[nme][q][None])
            else:
                lo, hi = offs[nme]
                outs[q].append((sm_g, sm_d, sm_m, sm_v)[q][:, lo:hi])
    return (loss, grad_x[None], *outs[0], *outs[1], *outs[2], *outs[3])
```

```python
import jax
import jax.numpy as jnp
import numpy as np
from jax import lax
from jax.experimental import pallas as pl
from jax.experimental.pallas import tpu as pltpu

F32, BF16 = jnp.float32, jnp.bfloat16
D = 1024
DH = 64
DHA = 80
DV = 128
LOG2E = 1.4426950408889634
LN2 = 0.6931471805599453
HR = 4
CH = 128
EPS = 1e-6
ROPE_THETA = 10000.0
NDEV = 8
O_GL, O_ZA, O_QA, O_KA, O_VA, O_QR, O_ZR, O_VR, O_KR, P_W = 0, 2048, 2560, 3072, 3200, 3328, 3584, 4096, 4608, 4864
ORIG = dict(qa=(0, 512), ka=(512, 640), va=(640, 768), za=(768, 1280), qr=(1280, 1536), kr=(1536, 1792),
            vr=(1792, 2304), zr=(2304, 2816), gl=(2816, 4864))
P_ORDER = ("gl", "za", "qa", "ka", "va", "qr", "zr", "vr", "kr")
ADAM_LR, ADAM_B1, ADAM_B2, ADAM_EPS, ADAM_WD, ADAM_STEP = 0.001, 0.9, 0.999, 1e-08, 0.01, 10
VMEM_BIG = 56 * 1024 * 1024
MESH = pl.DeviceIdType.MESH

NT = (((1,), (1,)), ((), ()))
TN = (((0,), (0,)), ((), ()))


def _dot(a, b, dims=None):
    if dims is None:
        return jnp.dot(a, b, preferred_element_type=F32)
    return lax.dot_general(a, b, dims, preferred_element_type=F32)


def _cp(sem=None, vmem=None):
    kw = {}
    if sem is not None:
        kw["dimension_semantics"] = sem
    if vmem is not None:
        kw["vmem_limit_bytes"] = vmem
    return pltpu.CompilerParams(**kw)


def _sigmoid(z):
    return 1.0 / (1.0 + jnp.exp(-z))


def _sum11(m):
    return jnp.sum(jnp.sum(m, axis=-1, keepdims=True), axis=0, keepdims=True)


def _full(shape):
    n = len(shape)
    return pl.BlockSpec(shape, lambda *_: (0,) * n)


def _my_pos():
    return lax.axis_index("x"), lax.axis_index("y"), lax.axis_index("c")


def _peer(k, x, y, c):
    return ((1 - x) if k & 4 else x, (1 - y) if k & 2 else y, (1 - c) if k & 1 else c)


def _small_allgather(vs, name):
    n = len(vs)

    def body(*refs):
        v_refs, out_refs = refs[:n], refs[n:2 * n]
        send_sems, recv_sems = refs[2 * n:]
        x, y, c = _my_pos()
        me = 4 * x + 2 * y + c
        cps = []
        for a in range(n):
            out_refs[a][me] = v_refs[a][...]
            for k in range(1, NDEV):
                cp = pltpu.make_async_remote_copy(src_ref=v_refs[a], dst_ref=out_refs[a].at[me], send_sem=send_sems.at[a, k - 1],
                                                  recv_sem=recv_sems.at[a, k - 1], device_id=_peer(k, x, y, c), device_id_type=MESH)
                cp.start()
                cps.append(cp)
        for cp in cps:
            cp.wait()

    vm = pl.BlockSpec(memory_space=pltpu.VMEM)
    return pl.pallas_call(
        body, name=name, out_shape=[jax.ShapeDtypeStruct((NDEV,) + v.shape, v.dtype) for v in vs],
        in_specs=[vm] * n, out_specs=[vm] * n,
        scratch_shapes=[pltpu.SemaphoreType.DMA((n, NDEV - 1)), pltpu.SemaphoreType.DMA((n, NDEV - 1))],
    )(*vs)


def _allgather_hbm(arrs, name):
    n = len(arrs)

    def body(*refs):
        ins, outs = refs[:n], refs[n:2 * n]
        send_sems, recv_sems, local_sems = refs[2 * n:]
        x, y, c = _my_pos()
        me, sibling = (x, y, c), (x, y, 1 - c)
        chips = [(1 - x, y), (x, 1 - y), (1 - x, 1 - y)]

        def blk(a, px, py, pc):
            return outs[a].at[4 * px + 2 * py + pc]

        def copy(a, k, block, to, src=None):
            return pltpu.make_async_remote_copy(src_ref=blk(a, *block) if src is None else src, dst_ref=blk(a, *block),
                                                send_sem=send_sems.at[a, k], recv_sem=recv_sems.at[a, k], device_id=to, device_id_type=MESH)

        local, sent = [], []
        for a in range(n):
            mine = pltpu.make_async_copy(ins[a], blk(a, *me), local_sems.at[a])
            mine.start()
            local.append(mine)
            first = [copy(a, 0, me, sibling, src=ins[a])] + [copy(a, 1 + j, me, (*chip, c), src=ins[a]) for j, chip in enumerate(chips)]
            for cp in first:
                cp.start()
            sent += first
        for j, chip in enumerate(chips):
            for a in range(n):
                copy(a, 1 + j, (*chip, c), me).wait_recv()
                cp = copy(a, 4 + j, (*chip, c), sibling)
                cp.start()
                sent.append(cp)
        for a in range(n):
            copy(a, 0, sibling, me).wait_recv()
            for j, chip in enumerate(chips):
                copy(a, 4 + j, (*chip, 1 - c), me).wait_recv()
        for cp in sent:
            cp.wait_send()
        for cp in local:
            cp.wait()

    return pl.pallas_call(
        body, name=name, out_shape=[jax.ShapeDtypeStruct((NDEV,) + a.shape, a.dtype) for a in arrs],
        in_specs=[pl.BlockSpec(memory_space=pl.ANY)] * n, out_specs=[pl.BlockSpec(memory_space=pl.ANY)] * n,
        scratch_shapes=[pltpu.SemaphoreType.DMA((n, NDEV - 1)), pltpu.SemaphoreType.DMA((n, NDEV - 1)), pltpu.SemaphoreType.DMA((n,))],
    )(*arrs)


def _in_set(idx, dests):
    p = idx == dests[0]
    for d in dests[1:]:
        p = jnp.logical_or(p, idx == d)
    return p


def _host_call(body, xs, *, name, grid, in_specs, out_specs, out_shape, scratch_shapes, operands, compiler_params):
    nx, nin, nout, nscr = len(xs), len(operands), len(out_shape), len(scratch_shapes)
    ops, specs, aliases = list(operands), list(in_specs), {}
    oshape, ospecs = list(out_shape), list(out_specs)
    any_spec = pl.BlockSpec(memory_space=pl.ANY)
    for a, (send, dests, recv) in enumerate(xs):
        ops.append(send)
        specs.append(any_spec)
        if recv is not None:
            aliases[len(ops)] = nout + a
            ops.append(recv)
            specs.append(any_spec)
            oshape.append(jax.ShapeDtypeStruct(recv.shape, recv.dtype))
        else:
            oshape.append(jax.ShapeDtypeStruct((NDEV,) + send.shape[1:], send.dtype))
        ospecs.append(any_spec)
    ntot_in = len(ops)

    def wrapped(*refs):
        host_in = refs[:nin]
        sends, pos = [], nin
        for (_, _, recv) in xs:
            sends.append(refs[pos])
            pos += 1 if recv is None else 2
        host_out = refs[ntot_in:ntot_in + nout]
        recvs = refs[ntot_in + nout:ntot_in + nout + nx]
        host_scr = refs[ntot_in + nout + nx:ntot_in + nout + nx + nscr]
        send_sems, recv_sems, local_sems = refs[ntot_in + nout + nx + nscr:]
        first = pl.program_id(0) == 0
        last = pl.program_id(0) == grid[0] - 1
        for ax in range(1, len(grid)):
            first = jnp.logical_and(first, pl.program_id(ax) == 0)
            last = jnp.logical_and(last, pl.program_id(ax) == grid[ax] - 1)
        x, y, c = _my_pos()
        me = 4 * x + 2 * y + c

        def each(fn_remote, fn_local):
            for a, (_, dests, _) in enumerate(xs):
                lo, nd = dests[0], len(dests)
                for k in range(1, NDEV):
                    px, py, pc = _peer(k, x, y, c)
                    pidx = 4 * px + 2 * py + pc
                    cp = pltpu.make_async_remote_copy(src_ref=sends[a].at[jnp.clip(pidx - lo, 0, nd - 1)], dst_ref=recvs[a].at[me],
                                                      send_sem=send_sems.at[a, k - 1], recv_sem=recv_sems.at[a, k - 1],
                                                      device_id=(px, py, pc), device_id_type=MESH)
                    fn_remote(cp, _in_set(pidx, dests), _in_set(me, dests))
                lc = pltpu.make_async_copy(sends[a].at[jnp.clip(me - lo, 0, nd - 1)], recvs[a].at[me], local_sems.at[a])
                fn_local(lc, _in_set(me, dests))

        def start_remote(cp, to_dest, _):
            pl.when(jnp.logical_and(first, to_dest))(cp.start)

        def start_local(lc, i_am_dest):
            pl.when(jnp.logical_and(first, i_am_dest))(lc.start)

        def wait_remote(cp, to_dest, i_am_dest):
            pl.when(jnp.logical_and(last, to_dest))(cp.wait_send)
            pl.when(jnp.logical_and(last, i_am_dest))(cp.wait_recv)

        def wait_local(lc, i_am_dest):
            pl.when(jnp.logical_and(last, i_am_dest))(lc.wait)

        each(start_remote, start_local)
        body(*host_in, *host_out, *host_scr)
        each(wait_remote, wait_local)

    res = pl.pallas_call(
        wrapped, name=name, grid=grid, in_specs=specs, out_specs=ospecs, out_shape=oshape, input_output_aliases=aliases,
        scratch_shapes=list(scratch_shapes) + [pltpu.SemaphoreType.DMA((nx, NDEV - 1)), pltpu.SemaphoreType.DMA((nx, NDEV - 1)),
                                               pltpu.SemaphoreType.DMA((nx,))],
        compiler_params=compiler_params,
    )(*ops)
    return res[:nout], res[nout:]


def _mm_tn(a, b, name):
    S, M = a.shape
    N = b.shape[1]
    tk = min(512, S)
    tn = N if N <= 768 else (640 if N % 640 == 0 else 512)
    nk = S // tk

    def body(a_ref, b_ref, o_ref):
        @pl.when(pl.program_id(1) == 0)
        def _():
            o_ref[...] = jnp.zeros_like(o_ref)
        o_ref[...] += _dot(a_ref[...], b_ref[...], TN)

    return pl.pallas_call(
        body, name=name, out_shape=jax.ShapeDtypeStruct((M, N), F32), grid=(N // tn, nk),
        in_specs=[pl.BlockSpec((tk, M), lambda j, k: (k, 0)), pl.BlockSpec((tk, tn), lambda j, k: (k, j))],
        out_specs=pl.BlockSpec((M, tn), lambda j, k: (0, j)),
        compiler_params=_cp(("parallel", "arbitrary"), VMEM_BIG),
    )(a, b)


def _mod_shard(c_pad, w_ada_s, b_ada_s):
    def body(c_ref, w_ref, b_ref, o_ref, ca_ref):
        cv = c_ref[...]
        ca = (cv * _sigmoid(cv)).astype(BF16)
        ca_ref[...] = ca
        o_ref[...] = _dot(ca, w_ref[...].astype(BF16)) + b_ref[...]

    return pl.pallas_call(
        body, name="mod_shard", out_shape=[jax.ShapeDtypeStruct((16, w_ada_s.shape[1]), F32), jax.ShapeDtypeStruct((16, D), BF16)],
    )(c_pad, w_ada_s, b_ada_s)


def _fwd_in(x, mod, g_pre, w_p):
    S = x.shape[0]
    tm, tn = min(512, S), P_W // 2

    def body(x_ref, mod_ref, g_ref, w_ref, p_ref, h_ref):
        @pl.when(pl.program_id(1) == 0)
        def _():
            xv = x_ref[...]
            r = lax.rsqrt(jnp.mean(xv * xv, axis=-1, keepdims=True) + EPS)
            h = ((xv * r) * g_ref[...]) * (1.0 + mod_ref[1:2, :]) + mod_ref[0:1, :]
            h_ref[...] = h.astype(BF16)
        p_ref[...] = _dot(h_ref[...], w_ref[...])

    return pl.pallas_call(
        body, name="fwd_in", out_shape=[jax.ShapeDtypeStruct((S, P_W), F32), jax.ShapeDtypeStruct((S, D), BF16)],
        grid=(S // tm, P_W // tn),
        in_specs=[pl.BlockSpec((tm, D), lambda i, j: (i, 0)), _full((3, D)), _full((1, D)), pl.BlockSpec((D, tn), lambda i, j: (0, j))],
        out_specs=[pl.BlockSpec((tm, tn), lambda i, j: (i, j)), pl.BlockSpec((tm, D), lambda i, j: (i, 0))],
        compiler_params=_cp(("parallel", "arbitrary"), VMEM_BIG),
    )(x, mod, g_pre, w_p)


def _swap16(v):
    lane = lax.broadcasted_iota(jnp.int32, v.shape, 1)
    return jnp.where((lane % 32) < 16, pltpu.roll(v, 112, 1), pltpu.roll(v, 16, 1))


def _rope(v, cos, sin):
    return v * cos + _swap16(v) * sin


def _rope_t(v, cos, sin):
    return v * cos - _swap16(v) * sin


def _head_mean(v):
    lo = lax.broadcasted_iota(jnp.int32, v.shape, 1) < 64
    m0 = jnp.sum(jnp.where(lo, v, 0.0), axis=-1, keepdims=True)
    m1 = jnp.sum(jnp.where(lo, 0.0, v), axis=-1, keepdims=True)
    return jnp.where(lo, m0, m1) * (1.0 / 64.0)


def _prep(p, cos, sin, qg, kg):
    S = p.shape[0]
    tm = min(512, S)

    def body(qa_ref, kv_ref, qr_ref, kr_ref, cos_ref, sin_ref, qg_ref, kg_ref, qt_ref, kh_ref, kt_ref, vh_ref, vta_ref, qr2_ref, kr2_ref):
        cos_v, sin_v = cos_ref[...], sin_ref[...]
        for g in range(4):
            xv = qa_ref[:, 128 * g:128 * g + 128]
            r = lax.rsqrt(_head_mean(xv * xv) + EPS)
            yt = (_rope((xv * r) * qg_ref[...], cos_v, sin_v) * (0.125 * LOG2E)).T
            qt_ref[2 * g] = yt[:DH].astype(BF16)
            qt_ref[2 * g + 1] = yt[DH:].astype(BF16)
        xv = kv_ref[:, :128]
        r = lax.rsqrt(_head_mean(xv * xv) + EPS)
        yv = _rope((xv * r) * kg_ref[...], cos_v, sin_v)
        kh_ref[0] = yv[:, :64].astype(BF16)
        kh_ref[1] = yv[:, 64:].astype(BF16)
        yt = yv.T
        kt_ref[0] = yt[:DH].astype(BF16)
        kt_ref[1] = yt[DH:].astype(BF16)
        vv = kv_ref[:, 128:]
        vh_ref[0] = vv[:, :64].astype(BF16)
        vh_ref[1] = vv[:, 64:].astype(BF16)
        vt = vv.T
        tail = (lax.broadcasted_iota(jnp.int32, (DHA - DH, tm), 0) == 0).astype(BF16)
        for kvh in range(2):
            vta_ref[kvh, 0:DH, :] = vt[DH * kvh:DH * kvh + DH].astype(BF16)
            vta_ref[kvh, DH:DHA, :] = tail
        for g in range(2):
            sl = slice(128 * g, 128 * g + 128)
            qr2_ref[:, sl] = _rope(qr_ref[:, sl], cos_v, sin_v)
            kr2_ref[:, sl] = _rope(kr_ref[:, sl], cos_v, sin_v) * 0.125

    hm = lambda n: pl.BlockSpec((n, tm, DH), lambda i: (0, i, 0))
    ht = lambda n, r: pl.BlockSpec((n, r, tm), lambda i: (0, 0, i))
    return pl.pallas_call(
        body, name="prep",
        out_shape=[jax.ShapeDtypeStruct((8, DH, S), BF16), jax.ShapeDtypeStruct((2, S, DH), BF16), jax.ShapeDtypeStruct((2, DH, S), BF16),
                   jax.ShapeDtypeStruct((2, S, DH), BF16), jax.ShapeDtypeStruct((2, DHA, S), BF16),
                   jax.ShapeDtypeStruct((S, 256), F32), jax.ShapeDtypeStruct((S, 256), F32)],
        grid=(S // tm,),
        in_specs=[pl.BlockSpec((tm, 512), lambda i: (i, O_QA // 512)), pl.BlockSpec((tm, 256), lambda i: (i, O_KA // 256)),
                  pl.BlockSpec((tm, 256), lambda i: (i, O_QR // 256)), pl.BlockSpec((tm, 256), lambda i: (i, O_KR // 256)),
                  pl.BlockSpec((tm, 128), lambda i: (i, 0)), pl.BlockSpec((tm, 128), lambda i: (i, 0)), _full((1, 128)), _full((1, 128))],
        out_specs=[ht(8, DH), hm(2), ht(2, DH), hm(2), ht(2, DHA), pl.BlockSpec((tm, 256), lambda i: (i, 0)), pl.BlockSpec((tm, 256), lambda i: (i, 0))],
        compiler_params=_cp(("parallel",)),
    )(p, p, p, p, cos, sin, qg, kg)


def _attn_fwd(qt, kh, vta):
    S = qt.shape[2]
    tq, tk = min(512, S), min(512, S)
    nj = S // tk

    def body(q_ref, k_ref, v_ref, o_ref, ot_ref, lse_ref, m_s, acc_s):
        j = pl.program_id(2)

        @pl.when(j == 0)
        def _():
            m_s[...] = jnp.full_like(m_s, -jnp.inf)
            acc_s[...] = jnp.zeros_like(acc_s)

        k, v = k_ref[0], v_ref[0]
        m_all = m_s[...]
        st = {0: _dot(k, q_ref[0])}
        m_new, acc_new = [], []
        for h in range(4):
            if h + 1 < 4:
                st[h + 1] = _dot(k, q_ref[h + 1])
            m_old = m_all[h:h + 1, :]
            mn = jnp.maximum(m_old, jnp.max(st[h], axis=0, keepdims=True))
            pt = jnp.exp2(st[h] - mn).astype(BF16)
            acc_new.append(jnp.exp2(m_old - mn) * acc_s[h] + _dot(v, pt))
            m_new.append(mn)
            del st[h]
        for h in range(4):
            acc_s[h] = acc_new[h]
            m_s[h:h + 1, :] = m_new[h]

        @pl.when(j == nj - 1)
        def _():
            for h in range(4):
                ot = acc_s[h, 0:DH, :] / acc_s[h, DH:DH + 1, :]
                ot_ref[h] = ot
                o_ref[:, DH * h:DH * h + DH] = ot.T
                lse_ref[0, h:h + 1, :] = m_s[h:h + 1, :] + jnp.log2(acc_s[h, DH:DH + 1, :])

    return pl.pallas_call(
        body, name="attn_fwd",
        out_shape=[jax.ShapeDtypeStruct((S, 512), F32), jax.ShapeDtypeStruct((8, DH, S), F32), jax.ShapeDtypeStruct((2, 4, S), F32)],
        grid=(2, S // tq, nj),
        in_specs=[pl.BlockSpec((4, DH, tq), lambda g, i, j: (g, 0, i)), pl.BlockSpec((1, tk, DH), lambda g, i, j: (g, j, 0)),
                  pl.BlockSpec((1, DHA, tk), lambda g, i, j: (g, 0, j))],
        out_specs=[pl.BlockSpec((tq, 256), lambda g, i, j: (i, g)), pl.BlockSpec((4, DH, tq), lambda g, i, j: (g, 0, i)),
                   pl.BlockSpec((1, 4, tq), lambda g, i, j: (g, 0, i))],
        scratch_shapes=[pltpu.VMEM((8, tq), F32), pltpu.VMEM((4, DHA, tq), F32)],
        compiler_params=_cp(("parallel", "parallel", "arbitrary"), VMEM_BIG),
    )(qt, kh, vta)


def _ret_tables(wf, wb):
    C = CH

    def body(wf_ref, wb_ref, dc_ref, qdf_ref, qdb_ref, kdf_ref, kdb_ref, a_ref):
        def logsig(w):
            z = jnp.exp(-jnp.abs(w))
            u = 1.0 + z
            l1p = jnp.where(u == 1.0, z, jnp.log(u) * (z / jnp.where(u == 1.0, 1.0, u - 1.0)))
            return jnp.minimum(w, 0.0) - l1p

        lgf, lgb = logsig(wf_ref[...]), logsig(wb_ref[...])
        lane4 = lax.broadcasted_iota(jnp.int32, (1, 4), 1)

        def pick(lg, h):
            return jnp.sum(jnp.where(lane4 == h, lg, 0.0), axis=-1, keepdims=True)

        ii = lax.broadcasted_iota(jnp.int32, (C, C), 0).astype(F32)
        jj = lax.broadcasted_iota(jnp.int32, (C, C), 1).astype(F32)
        dif = ii - jj
        hd = lax.broadcasted_iota(jnp.int32, (C, 256), 1) // DH
        lf_l = jnp.zeros((C, 256), F32)
        lb_l = jnp.zeros((C, 256), F32)
        for h in range(HR):
            lf, lb = pick(lgf, h), pick(lgb, h)
            dc_ref[h] = jnp.where(dif >= 0, jnp.exp(lf * jnp.maximum(dif, 0.0)), jnp.exp(lb * jnp.maximum(-dif, 0.0)))
            lf_l = jnp.where(hd == h, lf, lf_l)
            lb_l = jnp.where(hd == h, lb, lb_l)
            a_ref[h:h + 1, :] = jnp.broadcast_to(jnp.exp(lf * C), (1, 128))
            a_ref[HR + h:HR + h + 1, :] = jnp.broadcast_to(jnp.exp(lb * C), (1, 128))
        ri = lax.broadcasted_iota(jnp.int32, (C, 256), 0).astype(F32)
        qdf_ref[...] = jnp.exp(lf_l * (ri + 1.0))
        qdb_ref[...] = jnp.exp(lb_l * (C - ri))
        kdf_ref[...] = jnp.exp(lf_l * (C - 1.0 - ri))
        kdb_ref[...] = jnp.exp(lb_l * ri)

    t = jax.ShapeDtypeStruct((C, 256), F32)
    return pl.pallas_call(body, name="ret_tables",
                          out_shape=[jax.ShapeDtypeStruct((HR, C, C), F32), t, t, t, t, jax.ShapeDtypeStruct((8, 128), F32)])(wf, wb)


def _ret_states(kr2, p, kdf, kdb, adec):
    S = kr2.shape[0]
    C, N = CH, S // CH

    def body(kf_ref, vf_ref, kb_ref, vb_ref, kdf_ref, kdb_ref, a_ref, rf_ref, rb_ref, sf, sb):
        @pl.when(pl.program_id(0) == 0)
        def _():
            sf[...] = jnp.zeros_like(sf)
            sb[...] = jnp.zeros_like(sb)

        rf_ref[0] = sf[...]
        rb_ref[0] = sb[...]
        kdfw = (kf_ref[...] * kdf_ref[...]).astype(BF16)
        kdbw = (kb_ref[...] * kdb_ref[...]).astype(BF16)
        vf, vb = vf_ref[...].astype(BF16), vb_ref[...].astype(BF16)
        kvf = [_dot(kdfw[:, _ks(h)], vf[:, _vs(h)], TN) for h in range(HR)]
        kvb = [_dot(kdbw[:, _ks(h)], vb[:, _vs(h)], TN) for h in range(HR)]
        for h in range(HR):
            sf[h] = a_ref[h:h + 1, :] * sf[h] + kvf[h]
            sb[h] = a_ref[HR + h:HR + h + 1, :] * sb[h] + kvb[h]

    st = jax.ShapeDtypeStruct((N, HR, DH, DV), F32)
    return pl.pallas_call(
        body, name="ret_states", out_shape=[st, st], grid=(N,),
        in_specs=[pl.BlockSpec((C, 256), lambda t: (t, 0)), pl.BlockSpec((C, 512), lambda t: (t, O_VR // 512)),
                  pl.BlockSpec((C, 256), lambda t: (N - 1 - t, 0)), pl.BlockSpec((C, 512), lambda t: (N - 1 - t, O_VR // 512)),
                  _full((C, 256)), _full((C, 256)), _full((8, 128))],
        out_specs=[pl.BlockSpec((1, HR, DH, DV), lambda t: (t, 0, 0, 0)), pl.BlockSpec((1, HR, DH, DV), lambda t: (N - 1 - t, 0, 0, 0))],
        scratch_shapes=[pltpu.VMEM((HR, DH, DV), F32), pltpu.VMEM((HR, DH, DV), F32)],
        compiler_params=_cp(("arbitrary",)),
    )(kr2, p, kr2, p, kdf, kdb, adec)


def _ks(h):
    return slice(DH * h, DH * h + DH)


def _vs(h):
    return slice(DV * h, DV * h + DV)


def _ret_heads_fwd(qb, kb, vb, qfw, qbw, dc_ref, rf_ref, rb_ref):
    hs = range(HR)
    s = [_dot(qb[:, _ks(h)], kb[:, _ks(h)], NT) for h in hs]
    inter = [_dot(qfw[:, _ks(h)], rf_ref[0, h].astype(BF16)) + _dot(qbw[:, _ks(h)], rb_ref[0, h].astype(BF16)) for h in hs]
    sd = [s[h] * dc_ref[h] for h in hs]
    o = [_dot(sd[h].astype(BF16), vb[:, _vs(h)]) + inter[h] for h in hs]
    return sd, o


def _ret_out(qr2, kr2, p, rf, rb, dc, qdf, qdb, gn):
    S = qr2.shape[0]
    C, N = CH, S // CH

    def body(q_ref, k_ref, v_ref, z_ref, rf_ref, rb_ref, dc_ref, qdf_ref, qdb_ref, gn_ref, yr_ref):
        qv = q_ref[...]
        qb, kb, vb = qv.astype(BF16), k_ref[...].astype(BF16), v_ref[...].astype(BF16)
        qfw, qbw = (qv * qdf_ref[...]).astype(BF16), (qv * qdb_ref[...]).astype(BF16)
        _, o = _ret_heads_fwd(qb, kb, vb, qfw, qbw, dc_ref, rf_ref, rb_ref)
        for h in range(HR):
            vs = _vs(h)
            mu = jnp.mean(o[h], axis=-1, keepdims=True)
            var = jnp.mean(jnp.square(o[h] - mu), axis=-1, keepdims=True)
            on = (o[h] - mu) * lax.rsqrt(var + EPS)
            z = z_ref[:, vs]
            yr_ref[:, vs] = ((on * gn_ref[:, vs]) * (z * _sigmoid(z))).astype(BF16)

    return pl.pallas_call(
        body, name="ret_out", out_shape=jax.ShapeDtypeStruct((S, 512), BF16), grid=(N,),
        in_specs=[pl.BlockSpec((C, 256), lambda t: (t, 0)), pl.BlockSpec((C, 256), lambda t: (t, 0)),
                  pl.BlockSpec((C, 512), lambda t: (t, O_VR // 512)), pl.BlockSpec((C, 512), lambda t: (t, O_ZR // 512)),
                  pl.BlockSpec((1, HR, DH, DV), lambda t: (t, 0, 0, 0)), pl.BlockSpec((1, HR, DH, DV), lambda t: (t, 0, 0, 0)),
                  _full((HR, C, C)), _full((C, 256)), _full((C, 256)), _full((1, 512))],
        out_specs=pl.BlockSpec((C, 512), lambda t: (t, 0)),
        compiler_params=_cp(("parallel",)),
    )(qr2, kr2, p, p, rf, rb, dc, qdf, qdb, gn)


def _mid(x, tgt, mod, g_post, o_att, p, yr, w_pa, w_pr, w_out):
    S = x.shape[0]
    tm = min(256, S)

    def body(x_ref, t_ref, mod_ref, gp_ref, o_ref, za_ref, gl_ref, yr_ref, wpa_ref, wpr_ref, wout_ref,
             dout_ref, do_ref, dpm_ref, dyr_ref, mb_ref, dub_ref, yab_ref, dab_ref, drb_ref, sums_ref):
        @pl.when(pl.program_id(0) == 0)
        def _():
            sums_ref[...] = jnp.zeros_like(sums_ref)

        za = za_ref[...]
        sa = _sigmoid(za)
        sil = za * sa
        ov = o_ref[...]
        ya_b = (ov * sil).astype(BF16)
        yr_b = yr_ref[...]
        av = _dot(ya_b, wpa_ref[...])
        rv = _dot(yr_b, wpr_ref[...])
        ga = _sigmoid(gl_ref[:, :D])
        gr = _sigmoid(gl_ref[:, D:])
        mb = (ga * av + gr * rv).astype(BF16)
        u = _dot(mb, wout_ref[...])
        r2 = lax.rsqrt(jnp.mean(u * u, axis=-1, keepdims=True) + EPS)
        un = u * r2
        gp = gp_ref[...]
        yv = un * gp
        gate = mod_ref[2:3, :]
        err = (x_ref[...] + gate * yv) - t_ref[...]
        dout = err * (1.0 / D)
        dout_ref[...] = dout
        dy = dout * gate
        sums_ref[0:1, :] += jnp.sum(dout * yv, axis=0, keepdims=True)
        sums_ref[1:2, :] += jnp.sum(dy * un, axis=0, keepdims=True)
        sums_ref[2:3, :] += jnp.sum(err * err, axis=0, keepdims=True)
        dyg = dy * gp
        du_b = (r2 * (dyg - un * jnp.mean(dyg * un, axis=-1, keepdims=True))).astype(BF16)
        dm = _dot(du_b, wout_ref[...], NT)
        da_b = (dm * ga).astype(BF16)
        dr_b = (dm * gr).astype(BF16)
        dpm_ref[:, :D] = (dm * av * (ga * (1.0 - ga))).astype(BF16)
        dpm_ref[:, D:2 * D] = (dm * rv * (gr * (1.0 - gr))).astype(BF16)
        dya = _dot(da_b, wpa_ref[...], NT)
        dyr_ref[...] = _dot(dr_b, wpr_ref[...], NT)
        dov = dya * sil
        for g in range(4):
            dt = dov[:, 128 * g:128 * g + 128].T
            do_ref[2 * g] = dt[:DH].astype(BF16)
            do_ref[2 * g + 1] = dt[DH:].astype(BF16)
        dpm_ref[:, 2 * D:] = (dya * ov * (sa * (1.0 + za * (1.0 - sa)))).astype(BF16)
        mb_ref[...] = mb
        dub_ref[...] = du_b
        yab_ref[...] = ya_b
        dab_ref[...] = da_b
        drb_ref[...] = dr_b

    row = lambda w: pl.BlockSpec((tm, w), lambda i: (i, 0))
    sd = lambda w, dt: jax.ShapeDtypeStruct((S, w), dt)
    return pl.pallas_call(
        body, name="mid",
        out_shape=[sd(D, F32), jax.ShapeDtypeStruct((8, DH, S), BF16), sd(2560, BF16), sd(512, F32), sd(D, BF16), sd(D, BF16), sd(512, BF16),
                   sd(D, BF16), sd(D, BF16), jax.ShapeDtypeStruct((8, D), F32)],
        grid=(S // tm,),
        in_specs=[row(D), row(D), _full((3, D)), _full((1, D)), row(512), pl.BlockSpec((tm, 512), lambda i: (i, O_ZA // 512)),
                  pl.BlockSpec((tm, 2048), lambda i: (i, 0)), row(512), _full((512, D)), _full((512, D)), _full((D, D))],
        out_specs=[row(D), pl.BlockSpec((8, DH, tm), lambda i: (0, 0, i)), row(2560), row(512), row(D), row(D), row(512), row(D), row(D),
                   _full((8, D))],
        compiler_params=_cp(("arbitrary",), VMEM_BIG),
    )(x, tgt, mod, g_post, o_att, p, p, yr, w_pa, w_pr, w_out)


def _attn_bwd(qt, kh, kt, vh, dot_, ot, lse, xs):
    S = qt.shape[2]
    tq, tk = min(512, S), min(512, S)

    def body(q_ref, k_ref, kt_ref, v_ref, do_ref, o_ref, lse_ref, dq_ref, dk_ref, dv_ref):
        j, i = pl.program_id(1), pl.program_id(2)
        cols = pl.ds(pl.multiple_of(i * tq, tq), tq)
        k, kt, v = k_ref[0], kt_ref[0], v_ref[0]
        lse_all = lse_ref[0]
        st = {0: _dot(k, q_ref[0])}
        dpt = {0: _dot(v, do_ref[0])}
        dk_acc, dv_acc, dqs = None, None, []
        for h in range(4):
            if h + 1 < 4:
                st[h + 1] = _dot(k, q_ref[h + 1])
                dpt[h + 1] = _dot(v, do_ref[h + 1])
            qt_h, dot_h = q_ref[h], do_ref[h]
            delta = jnp.sum(dot_h.astype(F32) * o_ref[h], axis=0, keepdims=True)
            pt = jnp.exp2(st[h] - lse_all[h:h + 1, :])
            dst = (pt * (dpt[h] - delta)).astype(BF16)
            dv_h = _dot(dot_h, pt.astype(BF16), NT)
            dk_h = _dot(qt_h, dst, NT)
            dqs.append(_dot(kt, dst))
            dv_acc = dv_h if dv_acc is None else dv_acc + dv_h
            dk_acc = dk_h if dk_acc is None else dk_acc + dk_h
            del st[h], dpt[h]

        @pl.when(i == 0)
        def _():
            dk_ref[0] = dk_acc
            dv_ref[0] = dv_acc

        @pl.when(i > 0)
        def _():
            dk_ref[0] += dk_acc
            dv_ref[0] += dv_acc

        @pl.when(j == 0)
        def _():
            for h in range(4):
                dq_ref[h, :, cols] = dqs[h]

        @pl.when(j > 0)
        def _():
            for h in range(4):
                dq_ref[h, :, cols] += dqs[h]

    return _host_call(
        body, xs, name="attn_bwd",
        out_shape=[jax.ShapeDtypeStruct((8, DH, S), F32), jax.ShapeDtypeStruct((2, DH, S), F32), jax.ShapeDtypeStruct((2, DH, S), F32)],
        grid=(2, S // tk, S // tq),
        in_specs=[pl.BlockSpec((4, DH, tq), lambda g, j, i: (g, 0, i)), pl.BlockSpec((1, tk, DH), lambda g, j, i: (g, j, 0)),
                  pl.BlockSpec((1, DH, tk), lambda g, j, i: (g, 0, j)), pl.BlockSpec((1, tk, DH), lambda g, j, i: (g, j, 0)),
                  pl.BlockSpec((4, DH, tq), lambda g, j, i: (g, 0, i)), pl.BlockSpec((4, DH, tq), lambda g, j, i: (g, 0, i)),
                  pl.BlockSpec((1, 4, tq), lambda g, j, i: (g, 0, i))],
        out_specs=[pl.BlockSpec((4, DH, S), lambda g, j, i: (g, 0, 0)), pl.BlockSpec((1, DH, tk), lambda g, j, i: (g, 0, j)),
                   pl.BlockSpec((1, DH, tk), lambda g, j, i: (g, 0, j))],
        scratch_shapes=[], operands=(qt, kh, kt, vh, dot_, ot, lse),
        compiler_params=_cp(("arbitrary", "arbitrary", "arbitrary"), VMEM_BIG),
    )


def _attn_prep_bwd(dqt, dkt, dvt, p, cos, sin, qg, kg):
    S = dqt.shape[2]
    tm = min(512, S)

    def body(dq_ref, dk_ref, dv_ref, qa_ref, ka_ref, cos_ref, sin_ref, qg_ref, kg_ref, dp_ref, gs_ref):
        @pl.when(pl.program_id(0) == 0)
        def _():
            gs_ref[...] = jnp.zeros_like(gs_ref)

        cos_v, sin_v = cos_ref[...], sin_ref[...]

        def pair(ref, a):
            return jnp.concatenate([ref[a], ref[a + 1]], axis=0).T

        def norm_bwd(dyv, xv, gv, row):
            r = lax.rsqrt(_head_mean(xv * xv) + EPS)
            xn = xv * r
            dxh = _rope_t(dyv, cos_v, sin_v)
            gs_ref[row:row + 1, :] += jnp.sum(dxh * xn, axis=0, keepdims=True)
            dg = dxh * gv
            return r * (dg - xn * _head_mean(dg * xn))

        for g in range(4):
            sl = slice(128 * g, 128 * g + 128)
            dp_ref[:, sl] = norm_bwd(pair(dq_ref, 2 * g) * 0.125, qa_ref[:, sl], qg_ref[...], 0).astype(BF16)
        dp_ref[:, 512:640] = norm_bwd(pair(dk_ref, 0) * LN2, ka_ref[...], kg_ref[...], 1).astype(BF16)
        dp_ref[:, 640:768] = pair(dv_ref, 0).astype(BF16)

    ht = lambda n: pl.BlockSpec((n, DH, tm), lambda i: (0, 0, i))
    return pl.pallas_call(
        body, name="attn_prep_bwd", out_shape=[jax.ShapeDtypeStruct((S, 768), BF16), jax.ShapeDtypeStruct((8, 128), F32)],
        grid=(S // tm,),
        in_specs=[ht(8), ht(2), ht(2),
                  pl.BlockSpec((tm, 512), lambda i: (i, O_QA // 512)), pl.BlockSpec((tm, 128), lambda i: (i, O_KA // 128)),
                  pl.BlockSpec((tm, 128), lambda i: (i, 0)), pl.BlockSpec((tm, 128), lambda i: (i, 0)), _full((1, 128)), _full((1, 128))],
        out_specs=[pl.BlockSpec((tm, 768), lambda i: (i, 0)), _full((8, 128))],
        compiler_params=_cp(("arbitrary",)),
    )(dqt, dkt, dvt, p, p, cos, sin, qg, kg)


def _ret_bwd_chunk(qr2, kr2, p, rf, rb, dc, qdf, qdb, gn, dyr, cos, sin, xs):
    S = qr2.shape[0]
    C, N = CH, S // CH

    def body(q_ref, k_ref, v_ref, z_ref, rf_ref, rb_ref, dc_ref, qdf_ref, qdb_ref, gn_ref, dyr_ref, cos_ref, sin_ref,
             dpa_ref, dk_ref, dv_ref, drf_ref, drb_ref, dgn_ref, dlg_ref, dqs):
        @pl.when(pl.program_id(0) == 0)
        def _():
            dgn_ref[...] = jnp.zeros_like(dgn_ref)
            dlg_ref[...] = jnp.zeros_like(dlg_ref)

        qv = q_ref[...]
        qb, kb, vb = qv.astype(BF16), k_ref[...].astype(BF16), v_ref[...].astype(BF16)
        qf32, qb32 = qv * qdf_ref[...], qv * qdb_ref[...]
        qfw, qbw = qf32.astype(BF16), qb32.astype(BF16)
        ii = lax.broadcasted_iota(jnp.int32, (C, C), 0).astype(F32)
        jj = lax.broadcasted_iota(jnp.int32, (C, C), 1).astype(F32)
        dif = ii - jj
        ri = lax.broadcasted_iota(jnp.int32, (C, 1), 0).astype(F32)
        hs = range(HR)
        sd, o = _ret_heads_fwd(qb, kb, vb, qfw, qbw, dc_ref, rf_ref, rb_ref)
        do_b = []
        for h in hs:
            vs = _vs(h)
            mu = jnp.mean(o[h], axis=-1, keepdims=True)
            rstd = lax.rsqrt(jnp.mean(jnp.square(o[h] - mu), axis=-1, keepdims=True) + EPS)
            on = (o[h] - mu) * rstd
            z = z_ref[:, vs]
            sz = _sigmoid(z)
            dy = dyr_ref[:, vs]
            gnv = gn_ref[:, vs]
            dpa_ref[:, 256 + DV * h:256 + DV * h + DV] = (dy * (on * gnv) * (sz * (1.0 + z * (1.0 - sz)))).astype(BF16)
            dys = dy * (z * sz)
            dgn_ref[:, vs] += jnp.sum(dys * on, axis=0, keepdims=True)
            don = dys * gnv
            do = rstd * (don - jnp.mean(don, axis=-1, keepdims=True) - on * jnp.mean(don * on, axis=-1, keepdims=True))
            do_b.append(do.astype(BF16))
        dpm = [_dot(do_b[h], vb[:, _vs(h)], NT) for h in hs]
        dqf = [_dot(do_b[h], rf_ref[0, h].astype(BF16), NT) for h in hs]
        dqb = [_dot(do_b[h], rb_ref[0, h].astype(BF16), NT) for h in hs]
        for h in hs:
            dv_ref[:, _vs(h)] = _dot(sd[h].astype(BF16), do_b[h], TN)
            drf_ref[0, h] = _dot(qfw[:, _ks(h)], do_b[h], TN)
            drb_ref[0, h] = _dot(qbw[:, _ks(h)], do_b[h], TN)
        dsd = [(dpm[h] * dc_ref[h]).astype(BF16) for h in hs]
        for h in hs:
            ks = _ks(h)
            dqs[:, ks] = _dot(dsd[h], kb[:, ks]) + dqf[h] * qdf_ref[:, ks] + dqb[h] * qdb_ref[:, ks]
            dk_ref[:, ks] = _dot(dsd[h], qb[:, ks], TN)
        for h in hs:
            ks = _ks(h)
            e = dpm[h] * sd[h]
            lf = _sum11(e * jnp.maximum(dif, 0.0)) + _sum11(jnp.sum(qf32[:, ks] * dqf[h], axis=-1, keepdims=True) * (ri + 1.0))
            lb = _sum11(e * jnp.maximum(-dif, 0.0)) + _sum11(jnp.sum(qb32[:, ks] * dqb[h], axis=-1, keepdims=True) * (C - ri))
            dlg_ref[h:h + 1, :] += jnp.broadcast_to(lf, (1, 128))
            dlg_ref[HR + h:HR + h + 1, :] += jnp.broadcast_to(lb, (1, 128))
        cos_v, sin_v = cos_ref[...], sin_ref[...]
        for g in range(2):
            sl = slice(128 * g, 128 * g + 128)
            dpa_ref[:, sl] = _rope_t(dqs[:, sl], cos_v, sin_v).astype(BF16)

    st = jax.ShapeDtypeStruct((N, HR, DH, DV), F32)
    stb = lambda: pl.BlockSpec((1, HR, DH, DV), lambda t: (t, 0, 0, 0))
    return _host_call(
        body, xs, name="ret_bwd_chunk",
        out_shape=[jax.ShapeDtypeStruct((S, 768), BF16), jax.ShapeDtypeStruct((S, 256), F32), jax.ShapeDtypeStruct((S, 512), F32), st, st,
                   jax.ShapeDtypeStruct((1, 512), F32), jax.ShapeDtypeStruct((8, 128), F32)],
        grid=(N,),
        in_specs=[pl.BlockSpec((C, 256), lambda t: (t, 0)), pl.BlockSpec((C, 256), lambda t: (t, 0)),
                  pl.BlockSpec((C, 512), lambda t: (t, O_VR // 512)), pl.BlockSpec((C, 512), lambda t: (t, O_ZR // 512)),
                  stb(), stb(), _full((HR, C, C)), _full((C, 256)), _full((C, 256)), _full((1, 512)),
                  pl.BlockSpec((C, 512), lambda t: (t, 0)), pl.BlockSpec((C, 128), lambda t: (t, 0)), pl.BlockSpec((C, 128), lambda t: (t, 0))],
        out_specs=[pl.BlockSpec((C, 768), lambda t: (t, 0)), pl.BlockSpec((C, 256), lambda t: (t, 0)), pl.BlockSpec((C, 512), lambda t: (t, 0)),
                   stb(), stb(), _full((1, 512)), _full((8, 128))],
        scratch_shapes=[pltpu.VMEM((C, 256), F32)], operands=(qr2, kr2, p, p, rf, rb, dc, qdf, qdb, gn, dyr, cos, sin),
        compiler_params=_cp(("arbitrary",)),
    )


def _ret_bwd_scan(kr2, p, rf, rb, drf, drb, kdf, kdb, adec):
    S = kr2.shape[0]
    C, N = CH, S // CH

    def body(kf_ref, vf_ref, kb_ref, vb_ref, rf_ref, rb_ref, drf_ref, drb_ref, kdf_ref, kdb_ref, a_ref,
             dkf_ref, dkb_ref, dvf_ref, dvb_ref, dlg_ref, gf, gb):
        @pl.when(pl.program_id(0) == 0)
        def _():
            gf[...] = jnp.zeros_like(gf)
            gb[...] = jnp.zeros_like(gb)
            dlg_ref[...] = jnp.zeros_like(dlg_ref)

        ri = lax.broadcasted_iota(jnp.int32, (C, 1), 0).astype(F32)

        def one(k_ref, v_ref, r_ref, dr_ref, kd_ref, g_s, dk_ref, dv_ref, row0, wexp):
            kd32 = k_ref[...] * kd_ref[...]
            kdw = kd32.astype(BF16)
            vb = v_ref[...].astype(BF16)
            for h in range(HR):
                ks, vs = _ks(h), _vs(h)
                gst = g_s[h]
                g_b = gst.astype(BF16)
                dkd = _dot(vb[:, vs], g_b, NT)
                dk_ref[:, ks] = dkd * kd_ref[:, ks]
                dv_ref[:, vs] = _dot(kdw[:, ks], g_b)
                av = a_ref[row0 + h:row0 + h + 1, :]
                lg = (_sum11(jnp.sum(kd32[:, ks] * dkd, axis=-1, keepdims=True) * wexp)
                      + C * av[:, 0:1] * _sum11(r_ref[0, h] * gst))
                dlg_ref[row0 + h:row0 + h + 1, :] += jnp.broadcast_to(lg, (1, 128))
                g_s[h] = dr_ref[0, h] + av * gst

        one(kf_ref, vf_ref, rf_ref, drf_ref, kdf_ref, gf, dkf_ref, dvf_ref, 0, C - 1.0 - ri)
        one(kb_ref, vb_ref, rb_ref, drb_ref, kdb_ref, gb, dkb_ref, dvb_ref, HR, ri)

    fwd = lambda w, off=0: pl.BlockSpec((C, w), lambda t: (N - 1 - t, off))
    bwd = lambda w, off=0: pl.BlockSpec((C, w), lambda t: (t, off))
    stf = lambda: pl.BlockSpec((1, HR, DH, DV), lambda t: (N - 1 - t, 0, 0, 0))
    stb = lambda: pl.BlockSpec((1, HR, DH, DV), lambda t: (t, 0, 0, 0))
    return pl.pallas_call(
        body, name="ret_bwd_scan",
        out_shape=[jax.ShapeDtypeStruct((S, 256), F32), jax.ShapeDtypeStruct((S, 256), F32), jax.ShapeDtypeStruct((S, 512), F32),
                   jax.ShapeDtypeStruct((S, 512), F32), jax.ShapeDtypeStruct((8, 128), F32)],
        grid=(N,),
        in_specs=[fwd(256), fwd(512, O_VR // 512), bwd(256), bwd(512, O_VR // 512), stf(), stb(), stf(), stb(),
                  _full((C, 256)), _full((C, 256)), _full((8, 128))],
        out_specs=[fwd(256), bwd(256), fwd(512), bwd(512), _full((8, 128))],
        scratch_shapes=[pltpu.VMEM((HR, DH, DV), F32), pltpu.VMEM((HR, DH, DV), F32)],
        compiler_params=_cp(("arbitrary",)),
    )(kr2, p, kr2, p, rf, rb, drf, drb, kdf, kdb, adec)


def _ret_bwd_final(dk_i, dkf, dkb, dv_i, dvf, dvb, cos, sin):
    S = dk_i.shape[0]
    tm = min(512, S)

    def body(a_ref, b_ref, c_ref, d_ref, e_ref, f_ref, cos_ref, sin_ref, o_ref):
        o_ref[:, :512] = (d_ref[...] + e_ref[...] + f_ref[...]).astype(BF16)
        cos_v, sin_v = cos_ref[...], sin_ref[...]
        for g in range(2):
            sl = slice(128 * g, 128 * g + 128)
            dk = a_ref[:, sl] + b_ref[:, sl] + c_ref[:, sl]
            o_ref[:, 512 + 128 * g:512 + 128 * g + 128] = (_rope_t(dk, cos_v, sin_v) * 0.125).astype(BF16)

    row = lambda w: pl.BlockSpec((tm, w), lambda i: (i, 0))
    return pl.pallas_call(
        body, name="ret_bwd_final", out_shape=jax.ShapeDtypeStruct((S, 768), BF16), grid=(S // tm,),
        in_specs=[row(256), row(256), row(256), row(512), row(512), row(512), row(128), row(128)], out_specs=row(768),
        compiler_params=_cp(("parallel",)),
    )(dk_i, dkf, dkb, dv_i, dvf, dvb, cos, sin)


def _bwd_in(dpm, dpa, dpra, dprb, w_p, x, dout, mod, g_pre, xs):
    S = x.shape[0]
    tm = min(256, S)

    def body(a_ref, b_ref, c_ref, d_ref, w_ref, x_ref, dout_ref, mod_ref, g_ref, gx_ref, sums_ref):
        @pl.when(pl.program_id(0) == 0)
        def _():
            sums_ref[...] = jnp.zeros_like(sums_ref)

        dh = (_dot(a_ref[...], w_ref[:, :O_QA], NT) + _dot(b_ref[...], w_ref[:, O_QA:O_QR], NT)
              + _dot(c_ref[...], w_ref[:, O_QR:O_VR], NT) + _dot(d_ref[...], w_ref[:, O_VR:], NT))
        xv = x_ref[...]
        r = lax.rsqrt(jnp.mean(xv * xv, axis=-1, keepdims=True) + EPS)
        xn = xv * r
        gv = g_ref[...]
        sc1 = 1.0 + mod_ref[1:2, :]
        sums_ref[0:1, :] += jnp.sum(dh, axis=0, keepdims=True)
        sums_ref[1:2, :] += jnp.sum(dh * (xn * gv), axis=0, keepdims=True)
        sums_ref[2:3, :] += jnp.sum(dh * xn, axis=0, keepdims=True) * sc1
        dxn = dh * (gv * sc1)
        gx_ref[...] = dout_ref[...] + r * (dxn - xn * jnp.mean(dxn * xn, axis=-1, keepdims=True))

    row = lambda w: pl.BlockSpec((tm, w), lambda i: (i, 0))
    return _host_call(
        body, xs, name="bwd_in", out_shape=[jax.ShapeDtypeStruct((S, D), F32), jax.ShapeDtypeStruct((8, D), F32)], grid=(S // tm,),
        in_specs=[row(2560), row(768), row(768), row(768), _full((D, P_W)), row(D), row(D), _full((3, D)), _full((1, D))],
        out_specs=[row(D), _full((8, D))], scratch_shapes=[], operands=(dpm, dpa, dpra, dprb, w_p, x, dout, mod, g_pre),
        compiler_params=_cp(("arbitrary",), VMEM_BIG),
    )


SMALL = ("b_ada", "g_pre", "qn_g", "kn_g", "w_dec_f", "w_dec_b", "gn_g", "g_post")


def _small_update(gathered, wmv):
    ns = len(SMALL)

    def body(*refs):
        gin_ref, gmid_ref, ggn_ref, gatt_ref, gl1_ref, gl2_ref = refs[:6]
        wmv_refs = refs[6:6 + 3 * ns]
        loss_ref = refs[6 + 3 * ns]
        out_refs = refs[7 + 3 * ns:]

        def dsum(ref, r=None):
            rows = slice(None) if r is None else slice(r, r + 1)
            acc = ref[0, rows, :]
            for d in range(1, NDEV):
                acc = acc + ref[d, rows, :]
            return acc

        s_lg = dsum(gl1_ref) + dsum(gl2_ref)
        loss_ref[...] = (0.5 / D) * jnp.sum(dsum(gmid_ref, 2), axis=-1, keepdims=True)
        eye = lax.broadcasted_iota(jnp.int32, (8, 128), 0) == lax.broadcasted_iota(jnp.int32, (8, 128), 1)
        dlg = jnp.sum(jnp.where(eye, s_lg, 0.0), axis=0, keepdims=True)
        w_f, w_b = wmv_refs[3 * SMALL.index("w_dec_f")][...], wmv_refs[3 * SMALL.index("w_dec_b")][...]
        s_q, s_k = dsum(gatt_ref, 0), dsum(gatt_ref, 1)
        grads = dict(
            b_ada=jnp.concatenate([dsum(gin_ref, 0), dsum(gin_ref, 1), dsum(gmid_ref, 0)], axis=1),
            g_pre=dsum(gin_ref, 2), g_post=dsum(gmid_ref, 1), gn_g=dsum(ggn_ref),
            qn_g=s_q[:, :DH] + s_q[:, DH:], kn_g=s_k[:, :DH] + s_k[:, DH:],
            w_dec_f=dlg[:, 0:HR] * _sigmoid(-w_f), w_dec_b=dlg[:, HR:2 * HR] * _sigmoid(-w_b))
        for i, nme in enumerate(SMALL):
            g = grads[nme]
            w_ref, m_ref, v_ref = wmv_refs[3 * i:3 * i + 3]
            g_ref, d_ref, nm_ref, nv_ref = out_refs[4 * i:4 * i + 4]
            g_ref[...] = g
            m2 = ADAM_B1 * m_ref[...] + (1.0 - ADAM_B1) * g
            v2 = ADAM_B2 * v_ref[...] + (1.0 - ADAM_B2) * jnp.square(g)
            m_hat = m2 / (1.0 - ADAM_B1 ** ADAM_STEP)
            v_hat = v2 / (1.0 - ADAM_B2 ** ADAM_STEP)
            d_ref[...] = -ADAM_LR * (m_hat / (jnp.sqrt(v_hat) + ADAM_EPS) + ADAM_WD * w_ref[...])
            nm_ref[...] = m2
            nv_ref[...] = v2

    out_shape = [jax.ShapeDtypeStruct((1, 1), F32)]
    for i in range(ns):
        out_shape += [jax.ShapeDtypeStruct(wmv[3 * i].shape, F32)] * 4
    return pl.pallas_call(body, name="small_update", out_shape=out_shape)(*gathered, *wmv)


def _adamw(parts, w, m, v, name):
    n, R, L = parts.shape
    tr = 256 if (R % 256 == 0 and R > 256) else R

    def body(p_ref, w_ref, m_ref, v_ref, g_ref, d_ref, nm_ref, nv_ref):
        g = p_ref[0].astype(F32)
        for k in range(1, n):
            g = g + p_ref[k].astype(F32)
        g_ref[...] = g
        m2 = ADAM_B1 * m_ref[...] + (1.0 - ADAM_B1) * g
        v2 = ADAM_B2 * v_ref[...] + (1.0 - ADAM_B2) * jnp.square(g)
        m_hat = m2 / (1.0 - ADAM_B1 ** ADAM_STEP)
        v_hat = v2 / (1.0 - ADAM_B2 ** ADAM_STEP)
        d_ref[...] = -ADAM_LR * (m_hat / (jnp.sqrt(v_hat) + ADAM_EPS) + ADAM_WD * w_ref[...])
        nm_ref[...] = m2
        nv_ref[...] = v2

    blk = pl.BlockSpec((tr, L), lambda i: (i, 0))
    o = jax.ShapeDtypeStruct((R, L), F32)
    return pl.pallas_call(
        body, name=name, out_shape=[o, o, o, o], grid=(R // tr,),
        in_specs=[pl.BlockSpec((n, tr, L), lambda i: (0, i, 0)), blk, blk, blk], out_specs=[blk, blk, blk, blk],
        compiler_params=_cp(("parallel",), VMEM_BIG),
    )(parts, w, m, v)


def _rope_tables(S):
    f = np.float32
    t = np.arange(S)
    row, col = (t // 64).astype(f), (t % 64).astype(f)
    half = DH // 2
    inv = np.power(f(ROPE_THETA), -np.arange(0, half, 2, dtype=f) / f(half)).astype(f)
    ar, ac = (row[:, None] * inv[None, :]).astype(f), (col[:, None] * inv[None, :]).astype(f)
    cos64 = np.concatenate([np.cos(ar), np.cos(ar), np.cos(ac), np.cos(ac)], axis=1).astype(f)
    sin64 = np.concatenate([-np.sin(ar), np.sin(ar), -np.sin(ac), np.sin(ac)], axis=1).astype(f)
    return jnp.asarray(np.tile(cos64, (1, 2))), jnp.asarray(np.tile(sin64, (1, 2)))


def _to_p_order(w_orig):
    return jnp.concatenate([w_orig[:, ORIG[n][0]:ORIG[n][1]] for n in P_ORDER], axis=1)


def _pad_lanes(v, n):
    return jnp.pad(v, ((0, 0), (0, n - v.shape[1])))


def kernel(x, c, w_ada, b_ada, g_pre, w_in, qn_g, kn_g, w_dec_f, w_dec_b, gn_g, w_pa, w_pr, w_out, g_post, loss_target, m_w_ada, m_b_ada, m_g_pre, m_w_in, m_qn_g, m_kn_g, m_w_dec_f, m_w_dec_b, m_gn_g, m_w_pa, m_w_pr, m_w_out, m_g_post, v_w_ada, v_b_ada, v_g_pre, v_w_in, v_qn_g, v_kn_g, v_w_dec_f, v_w_dec_b, v_gn_g, v_w_pa, v_w_pr, v_w_out, v_g_post):
    S = x.shape[1]
    me = 4 * lax.axis_index("x") + 2 * lax.axis_index("y") + lax.axis_index("c")
    xs, tgt = x[0], loss_target[0]
    ncol_ada = w_ada.shape[2]
    ncol_in = w_in.shape[2]

    c_all = _small_allgather([jnp.pad(c, ((0, 7), (0, 0)))], "ag_c")[0][:, 0, :]
    b_ada_s = lax.dynamic_slice(b_ada, (0, me * ncol_ada), (1, ncol_ada))
    mod_s, c_act = _mod_shard(jnp.pad(c_all, ((0, 8), (0, 0))), w_ada[0], b_ada_s)
    mod_all = _small_allgather([mod_s[:8]], "ag_mod")[0]
    mod = lax.dynamic_index_in_dim(mod_all, me, axis=1, keepdims=False).reshape(3, D)

    wg_in, wg_pa, wg_pr, wg_out = _allgather_hbm(
        [w_in[0].astype(BF16), w_pa[0].astype(BF16), w_pr[0].astype(BF16), w_out[0].astype(BF16)], "ag_weights")
    w_p = _to_p_order(wg_in.transpose(1, 0, 2).reshape(D, NDEV * ncol_in))
    w_pa_f = wg_pa.transpose(1, 0, 2).reshape(512, D)
    w_pr_f = wg_pr.transpose(1, 0, 2).reshape(512, D)
    w_out_f = wg_out.reshape(D, D)

    cos, sin = _rope_tables(S)
    qg, kg = jnp.tile(qn_g, (1, 2)), jnp.tile(kn_g, (1, 2))

    p, h = _fwd_in(xs, mod, g_pre, w_p)
    qt, kh, kt, vh, vta, qr2, kr2 = _prep(p, cos, sin, qg, kg)
    o_att, o_t, lse = _attn_fwd(qt, kh, vta)
    dc, qdf, qdb, kdf, kdb, adec = _ret_tables(w_dec_f, w_dec_b)
    rf, rb = _ret_states(kr2, p, kdf, kdb, adec)
    yr = _ret_out(qr2, kr2, p, rf, rb, dc, qdf, qdb, gn_g)

    dout, do, dpm, dyr, mb, dub, yab, dab, drb_, sums_mid = _mid(xs, tgt, mod, g_post, o_att, p, yr, w_pa_f, w_pr_f, w_out_f)
    gw_out = _mm_tn(mb, dub, "gw_out")
    gw_pa = _mm_tn(yab, dab, "gw_pa")
    gw_pr = _mm_tn(yr, drb_, "gw_pr")
    gi_m = _mm_tn(h, dpm, "gw_in_mid")

    def shards(cols, nd):
        return cols.astype(BF16).reshape(D, nd, ncol_in).transpose(1, 0, 2)

    all_dev = tuple(range(NDEV))
    (dqt, dkt, dvt), (rs_out, rs_pa, rs_pr, rs_in) = _attn_bwd(qt, kh, kt, vh, do, o_t, lse, [
        (gw_out.astype(BF16).reshape(NDEV, 128, D), all_dev, None),
        (gw_pa.astype(BF16).reshape(512, NDEV, 128).transpose(1, 0, 2), all_dev, None),
        (gw_pr.astype(BF16).reshape(512, NDEV, 128).transpose(1, 0, 2), all_dev, None),
        (shards(gi_m[:, 224:2048], 3), (5, 6, 7), None)])
    dpa, gs_att = _attn_prep_bwd(dqt, dkt, dvt, p, cos, sin, qg, kg)
    gi_a = _mm_tn(h, dpa, "gw_in_att")
    (dpra, dk_i, dv_i, drf, drb, dgn, dlg1), (rs_in,) = _ret_bwd_chunk(qr2, kr2, p, rf, rb, dc, qdf, qdb, gn_g, dyr, cos, sin, [
        (shards(jnp.concatenate([gi_a, gi_m[:, 2048:2496]], axis=1), 2), (0, 1), rs_in)])
    dkf, dkb, dvf, dvb, dlg2 = _ret_bwd_scan(kr2, p, rf, rb, drf, drb, kdf, kdb, adec)
    dprb = _ret_bwd_final(dk_i, dkf, dkb, dv_i, dvf, dvb, cos, sin)
    gi_ra = _mm_tn(h, dpra, "gw_in_reta")
    gi_rb = _mm_tn(h, dprb, "gw_in_retb")
    (grad_x, sums_in), (rs_in,) = _bwd_in(dpm, dpa, dpra, dprb, w_p, xs, dout, mod, g_pre, [
        (shards(jnp.concatenate([gi_m[:, 2496:2560], gi_ra[:, :256], gi_rb[:, 512:768], gi_rb[:, :512], gi_ra[:, 256:768],
                                 gi_m[:, :224]], axis=1), 3), (2, 3, 4), rs_in)])

    gathered = _small_allgather([sums_in, sums_mid, dgn, gs_att, dlg1, dlg2], "ag_small")
    given = dict(b_ada=(b_ada, m_b_ada, v_b_ada), g_pre=(g_pre, m_g_pre, v_g_pre), qn_g=(qn_g, m_qn_g, v_qn_g), kn_g=(kn_g, m_kn_g, v_kn_g),
                 w_dec_f=(w_dec_f, m_w_dec_f, v_w_dec_f), w_dec_b=(w_dec_b, m_w_dec_b, v_w_dec_b), gn_g=(gn_g, m_gn_g, v_gn_g),
                 g_post=(g_post, m_g_post, v_g_post))
    small = _small_update(gathered, [a for nme in SMALL for a in given[nme]])
    loss = small[0][0, 0]

    g_in_all, g_mid_all = gathered[0], gathered[1]
    dmod_all = lax.dynamic_slice(jnp.concatenate([g_in_all[:, 0, :], g_in_all[:, 1, :], g_mid_all[:, 0, :]], axis=1),
                                 (0, me * ncol_ada), (NDEV, ncol_ada))
    g_ada = _mm_tn(c_act, jnp.pad(dmod_all, ((0, 8), (0, 0))).astype(BF16), "gw_ada")

    res = dict(
        w_ada=_adamw(g_ada[None], w_ada[0], m_w_ada[0], v_w_ada[0], "adamw_ada"),
        w_in=_adamw(rs_in, w_in[0], m_w_in[0], v_w_in[0], "adamw_in"),
        w_pa=_adamw(rs_pa, w_pa[0], m_w_pa[0], v_w_pa[0], "adamw_pa"),
        w_pr=_adamw(rs_pr, w_pr[0], m_w_pr[0], v_w_pr[0], "adamw_pr"),
        w_out=_adamw(rs_out, w_out[0], m_w_out[0], v_w_out[0], "adamw_out"),
    )
    names = ["w_ada", "b_ada", "g_pre", "w_in", "qn_g", "kn_g", "w_dec_f", "w_dec_b", "gn_g", "w_pa", "w_pr", "w_out", "g_post"]
    outs = [[], [], [], []]
    for nme in names:
        for q in range(4):
            if nme in res:
                outs[q].append(res[nme][q][None])
            else:
                outs[q].append(small[1 + 4 * SMALL.index(nme) + q])
    return (loss, grad_x[None], *outs[0], *outs[1], *outs[2], *outs[3])
```

```python
import jax
import jax.numpy as jnp
import numpy as np
from jax import lax
from jax.experimental import pallas as pl
from jax.experimental.pallas import tpu as pltpu

F32, BF16 = jnp.float32, jnp.bfloat16
D = 1024
DH = 64
DHA = 80
DV = 128
LOG2E = 1.4426950408889634
LN2 = 0.6931471805599453
HR = 4
CH = 128
EPS = 1e-6
ROPE_THETA = 10000.0
NDEV = 8
O_GL, O_ZA, O_QA, O_KA, O_VA, O_QR, O_ZR, O_VR, O_KR, P_W = 0, 2048, 2560, 3072, 3200, 3328, 3584, 4096, 4608, 4864
ORIG = dict(qa=(0, 512), ka=(512, 640), va=(640, 768), za=(768, 1280), qr=(1280, 1536), kr=(1536, 1792),
            vr=(1792, 2304), zr=(2304, 2816), gl=(2816, 4864))
P_ORDER = ("gl", "za", "qa", "ka", "va", "qr", "zr", "vr", "kr")
ADAM_LR, ADAM_B1, ADAM_B2, ADAM_EPS, ADAM_WD, ADAM_STEP = 0.001, 0.9, 0.999, 1e-08, 0.01, 10
VMEM_BIG = 56 * 1024 * 1024
MESH = pl.DeviceIdType.MESH

NT = (((1,), (1,)), ((), ()))
TN = (((0,), (0,)), ((), ()))


def _dot(a, b, dims=None):
    if dims is None:
        return jnp.dot(a, b, preferred_element_type=F32)
    return lax.dot_general(a, b, dims, preferred_element_type=F32)


def _cp(sem=None, vmem=None):
    kw = {}
    if sem is not None:
        kw["dimension_semantics"] = sem
    if vmem is not None:
        kw["vmem_limit_bytes"] = vmem
    return pltpu.CompilerParams(**kw)


def _sigmoid(z):
    return 1.0 / (1.0 + jnp.exp(-z))


def _sum11(m):
    return jnp.sum(jnp.sum(m, axis=-1, keepdims=True), axis=0, keepdims=True)


def _full(shape):
    n = len(shape)
    return pl.BlockSpec(shape, lambda *_: (0,) * n)


def _my_pos():
    return lax.axis_index("x"), lax.axis_index("y"), lax.axis_index("c")


def _peer(k, x, y, c):
    return ((1 - x) if k & 4 else x, (1 - y) if k & 2 else y, (1 - c) if k & 1 else c)


def _small_allgather(vs, name):
    n = len(vs)

    def body(*refs):
        v_refs, out_refs = refs[:n], refs[n:2 * n]
        send_sems, recv_sems = refs[2 * n:]
        x, y, c = _my_pos()
        me = 4 * x + 2 * y + c
        cps = []
        for a in range(n):
            out_refs[a][me] = v_refs[a][...]
            for k in range(1, NDEV):
                cp = pltpu.make_async_remote_copy(src_ref=v_refs[a], dst_ref=out_refs[a].at[me], send_sem=send_sems.at[a, k - 1],
                                                  recv_sem=recv_sems.at[a, k - 1], device_id=_peer(k, x, y, c), device_id_type=MESH)
                cp.start()
                cps.append(cp)
        for cp in cps:
            cp.wait()

    vm = pl.BlockSpec(memory_space=pltpu.VMEM)
    return pl.pallas_call(
        body, name=name, out_shape=[jax.ShapeDtypeStruct((NDEV,) + v.shape, v.dtype) for v in vs],
        in_specs=[vm] * n, out_specs=[vm] * n,
        scratch_shapes=[pltpu.SemaphoreType.DMA((n, NDEV - 1)), pltpu.SemaphoreType.DMA((n, NDEV - 1))],
    )(*vs)


def _prologue(c8, w_ada_s, b_ada_s, arrs):
    n = len(arrs)
    ncol = w_ada_s.shape[1]

    def body(*refs):
        c_ref, wa_ref, ba_ref = refs[:3]
        ins = refs[3:3 + n]
        mod_ref, cact_ref = refs[3 + n:5 + n]
        outs = refs[5 + n:5 + 2 * n]
        call_ref, send_sems, recv_sems, local_sems, s_send, s_recv = refs[5 + 2 * n:]
        x, y, c = _my_pos()
        me, sibling = (x, y, c), (x, y, 1 - c)
        chips = [(1 - x, y), (x, 1 - y), (1 - x, 1 - y)]
        me_i = 4 * x + 2 * y + c

        def small_gather(src_ref, dst_ref, row):
            cps = []
            for k in range(1, NDEV):
                cp = pltpu.make_async_remote_copy(src_ref=src_ref, dst_ref=dst_ref.at[me_i], send_sem=s_send.at[row, k - 1],
                                                  recv_sem=s_recv.at[row, k - 1], device_id=_peer(k, x, y, c), device_id_type=MESH)
                cp.start()
                cps.append(cp)
            for cp in cps:
                cp.wait()

        def blk(a, px, py, pc):
            return outs[a].at[4 * px + 2 * py + pc]

        def copy(a, k, block, to, src=None):
            return pltpu.make_async_remote_copy(src_ref=blk(a, *block) if src is None else src, dst_ref=blk(a, *block),
                                                send_sem=send_sems.at[a, k], recv_sem=recv_sems.at[a, k], device_id=to, device_id_type=MESH)

        local, sent = [], []
        for a in range(n):
            mine = pltpu.make_async_copy(ins[a], blk(a, *me), local_sems.at[a])
            mine.start()
            local.append(mine)
            first = [copy(a, 0, me, sibling, src=ins[a])] + [copy(a, 1 + j, me, (*chip, c), src=ins[a]) for j, chip in enumerate(chips)]
            for cp in first:
                cp.start()
            sent += first

        call_ref[me_i] = c_ref[...]
        small_gather(c_ref, call_ref, 0)
        cv = call_ref[:, 0, :]
        ca = jnp.concatenate([cv * _sigmoid(cv), jnp.zeros_like(cv)], axis=0).astype(BF16)
        cact_ref[...] = ca
        mod_ref[me_i] = (_dot(ca, wa_ref[...].astype(BF16)) + ba_ref[...])[:8]
        small_gather(mod_ref.at[me_i], mod_ref, 1)

        for j, chip in enumerate(chips):
            for a in range(n):
                copy(a, 1 + j, (*chip, c), me).wait_recv()
                cp = copy(a, 4 + j, (*chip, c), sibling)
                cp.start()
                sent.append(cp)
        for a in range(n):
            copy(a, 0, sibling, me).wait_recv()
            for j, chip in enumerate(chips):
                copy(a, 4 + j, (*chip, 1 - c), me).wait_recv()
        for cp in sent:
            cp.wait_send()
        for cp in local:
            cp.wait()

    vm, hbm = pl.BlockSpec(memory_space=pltpu.VMEM), pl.BlockSpec(memory_space=pl.ANY)
    res = pl.pallas_call(
        body, name="prologue",
        out_shape=[jax.ShapeDtypeStruct((NDEV, 8, ncol), F32), jax.ShapeDtypeStruct((16, D), BF16)]
        + [jax.ShapeDtypeStruct((NDEV,) + a.shape, a.dtype) for a in arrs],
        in_specs=[vm, vm, vm] + [hbm] * n, out_specs=[vm, vm] + [hbm] * n,
        scratch_shapes=[pltpu.VMEM((NDEV, 8, D), F32), pltpu.SemaphoreType.DMA((n, NDEV - 1)), pltpu.SemaphoreType.DMA((n, NDEV - 1)),
                        pltpu.SemaphoreType.DMA((n,)), pltpu.SemaphoreType.DMA((2, NDEV - 1)), pltpu.SemaphoreType.DMA((2, NDEV - 1))],
    )(c8, w_ada_s, b_ada_s, *arrs)
    return res[0], res[1], res[2:]


def _in_set(idx, dests):
    p = idx == dests[0]
    for d in dests[1:]:
        p = jnp.logical_or(p, idx == d)
    return p


def _host_call(body, xs, *, name, grid, in_specs, out_specs, out_shape, scratch_shapes, operands, compiler_params):
    nx, nin, nout, nscr = len(xs), len(operands), len(out_shape), len(scratch_shapes)
    if nx == 0:
        res = pl.pallas_call(body, name=name, grid=grid, in_specs=in_specs, out_specs=out_specs, out_shape=out_shape,
                             scratch_shapes=scratch_shapes, compiler_params=compiler_params)(*operands)
        return res, []
    ops, specs, aliases = list(operands), list(in_specs), {}
    oshape, ospecs = list(out_shape), list(out_specs)
    any_spec = pl.BlockSpec(memory_space=pl.ANY)
    for a, (send, dests, recv) in enumerate(xs):
        ops.append(send)
        specs.append(any_spec)
        if recv is not None:
            aliases[len(ops)] = nout + a
            ops.append(recv)
            specs.append(any_spec)
            oshape.append(jax.ShapeDtypeStruct(recv.shape, recv.dtype))
        else:
            oshape.append(jax.ShapeDtypeStruct((NDEV,) + send.shape[1:], send.dtype))
        ospecs.append(any_spec)
    ntot_in = len(ops)

    def wrapped(*refs):
        host_in = refs[:nin]
        sends, pos = [], nin
        for (_, _, recv) in xs:
            sends.append(refs[pos])
            pos += 1 if recv is None else 2
        host_out = refs[ntot_in:ntot_in + nout]
        recvs = refs[ntot_in + nout:ntot_in + nout + nx]
        host_scr = refs[ntot_in + nout + nx:ntot_in + nout + nx + nscr]
        send_sems, recv_sems, local_sems = refs[ntot_in + nout + nx + nscr:]
        first = pl.program_id(0) == 0
        last = pl.program_id(0) == grid[0] - 1
        for ax in range(1, len(grid)):
            first = jnp.logical_and(first, pl.program_id(ax) == 0)
            last = jnp.logical_and(last, pl.program_id(ax) == grid[ax] - 1)
        x, y, c = _my_pos()
        me = 4 * x + 2 * y + c

        def each(fn_remote, fn_local):
            for a, (_, dests, _) in enumerate(xs):
                lo, nd = dests[0], len(dests)
                for k in range(1, NDEV):
                    px, py, pc = _peer(k, x, y, c)
                    pidx = 4 * px + 2 * py + pc
                    cp = pltpu.make_async_remote_copy(src_ref=sends[a].at[jnp.clip(pidx - lo, 0, nd - 1)], dst_ref=recvs[a].at[me],
                                                      send_sem=send_sems.at[a, k - 1], recv_sem=recv_sems.at[a, k - 1],
                                                      device_id=(px, py, pc), device_id_type=MESH)
                    fn_remote(cp, _in_set(pidx, dests), _in_set(me, dests))
                lc = pltpu.make_async_copy(sends[a].at[jnp.clip(me - lo, 0, nd - 1)], recvs[a].at[me], local_sems.at[a])
                fn_local(lc, _in_set(me, dests))

        def start_remote(cp, to_dest, _):
            pl.when(jnp.logical_and(first, to_dest))(cp.start)

        def start_local(lc, i_am_dest):
            pl.when(jnp.logical_and(first, i_am_dest))(lc.start)

        def wait_remote(cp, to_dest, i_am_dest):
            pl.when(jnp.logical_and(last, to_dest))(cp.wait_send)
            pl.when(jnp.logical_and(last, i_am_dest))(cp.wait_recv)

        def wait_local(lc, i_am_dest):
            pl.when(jnp.logical_and(last, i_am_dest))(lc.wait)

        each(start_remote, start_local)
        body(*host_in, *host_out, *host_scr)
        each(wait_remote, wait_local)

    res = pl.pallas_call(
        wrapped, name=name, grid=grid, in_specs=specs, out_specs=ospecs, out_shape=oshape, input_output_aliases=aliases,
        scratch_shapes=list(scratch_shapes) + [pltpu.SemaphoreType.DMA((nx, NDEV - 1)), pltpu.SemaphoreType.DMA((nx, NDEV - 1)),
                                               pltpu.SemaphoreType.DMA((nx,))],
        compiler_params=compiler_params,
    )(*ops)
    return res[:nout], res[nout:]


def _mm_tn(a, b, name):
    S, M = a.shape
    N = b.shape[1]
    tk = min(512, S)
    tn = N if N <= 768 else (640 if N % 640 == 0 else 512)
    nk = S // tk

    def body(a_ref, b_ref, o_ref):
        @pl.when(pl.program_id(1) == 0)
        def _():
            o_ref[...] = jnp.zeros_like(o_ref)
        o_ref[...] += _dot(a_ref[...], b_ref[...], TN)

    return pl.pallas_call(
        body, name=name, out_shape=jax.ShapeDtypeStruct((M, N), F32), grid=(N // tn, nk),
        in_specs=[pl.BlockSpec((tk, M), lambda j, k: (k, 0)), pl.BlockSpec((tk, tn), lambda j, k: (k, j))],
        out_specs=pl.BlockSpec((M, tn), lambda j, k: (0, j)),
        compiler_params=_cp(("parallel", "arbitrary"), VMEM_BIG),
    )(a, b)


def _fwd_in(x, mod, g_pre, w_p):
    S = x.shape[0]
    tm, tn = min(512, S), P_W // 2

    def body(x_ref, mod_ref, g_ref, w_ref, p_ref, h_ref):
        @pl.when(pl.program_id(1) == 0)
        def _():
            xv = x_ref[...]
            r = lax.rsqrt(jnp.mean(xv * xv, axis=-1, keepdims=True) + EPS)
            h = ((xv * r) * g_ref[...]) * (1.0 + mod_ref[1:2, :]) + mod_ref[0:1, :]
            h_ref[...] = h.astype(BF16)
        p_ref[...] = _dot(h_ref[...], w_ref[...])

    return pl.pallas_call(
        body, name="fwd_in", out_shape=[jax.ShapeDtypeStruct((S, P_W), F32), jax.ShapeDtypeStruct((S, D), BF16)],
        grid=(S // tm, P_W // tn),
        in_specs=[pl.BlockSpec((tm, D), lambda i, j: (i, 0)), _full((3, D)), _full((1, D)), pl.BlockSpec((D, tn), lambda i, j: (0, j))],
        out_specs=[pl.BlockSpec((tm, tn), lambda i, j: (i, j)), pl.BlockSpec((tm, D), lambda i, j: (i, 0))],
        compiler_params=_cp(("parallel", "arbitrary"), VMEM_BIG),
    )(x, mod, g_pre, w_p)


def _swap16(v):
    lane = lax.broadcasted_iota(jnp.int32, v.shape, 1)
    return jnp.where((lane % 32) < 16, pltpu.roll(v, 112, 1), pltpu.roll(v, 16, 1))


def _rope(v, cos, sin):
    return v * cos + _swap16(v) * sin


def _rope_t(v, cos, sin):
    return v * cos - _swap16(v) * sin


def _head_mean(v):
    lo = lax.broadcasted_iota(jnp.int32, v.shape, 1) < 64
    m0 = jnp.sum(jnp.where(lo, v, 0.0), axis=-1, keepdims=True)
    m1 = jnp.sum(jnp.where(lo, 0.0, v), axis=-1, keepdims=True)
    return jnp.where(lo, m0, m1) * (1.0 / 64.0)


def _prep(p, cos, sin, qg, kg):
    S = p.shape[0]
    tm = min(512, S)

    def body(qa_ref, kv_ref, qr_ref, kr_ref, cos_ref, sin_ref, qg_ref, kg_ref, qt_ref, kh_ref, kt_ref, vh_ref, vta_ref, qr2_ref, kr2_ref):
        cos_v, sin_v = cos_ref[...], sin_ref[...]
        for g in range(4):
            xv = qa_ref[:, 128 * g:128 * g + 128]
            r = lax.rsqrt(_head_mean(xv * xv) + EPS)
            yt = (_rope((xv * r) * qg_ref[...], cos_v, sin_v) * (0.125 * LOG2E)).T
            qt_ref[2 * g] = yt[:DH].astype(BF16)
            qt_ref[2 * g + 1] = yt[DH:].astype(BF16)
        xv = kv_ref[:, :128]
        r = lax.rsqrt(_head_mean(xv * xv) + EPS)
        yv = _rope((xv * r) * kg_ref[...], cos_v, sin_v)
        kh_ref[0] = yv[:, :64].astype(BF16)
        kh_ref[1] = yv[:, 64:].astype(BF16)
        yt = yv.T
        kt_ref[0] = yt[:DH].astype(BF16)
        kt_ref[1] = yt[DH:].astype(BF16)
        vv = kv_ref[:, 128:]
        vh_ref[0] = vv[:, :64].astype(BF16)
        vh_ref[1] = vv[:, 64:].astype(BF16)
        vt = vv.T
        tail = (lax.broadcasted_iota(jnp.int32, (DHA - DH, tm), 0) == 0).astype(BF16)
        for kvh in range(2):
            vta_ref[kvh, 0:DH, :] = vt[DH * kvh:DH * kvh + DH].astype(BF16)
            vta_ref[kvh, DH:DHA, :] = tail
        for g in range(2):
            sl = slice(128 * g, 128 * g + 128)
            qr2_ref[:, sl] = _rope(qr_ref[:, sl], cos_v, sin_v)
            kr2_ref[:, sl] = _rope(kr_ref[:, sl], cos_v, sin_v) * 0.125

    hm = lambda n: pl.BlockSpec((n, tm, DH), lambda i: (0, i, 0))
    ht = lambda n, r: pl.BlockSpec((n, r, tm), lambda i: (0, 0, i))
    return pl.pallas_call(
        body, name="prep",
        out_shape=[jax.ShapeDtypeStruct((8, DH, S), BF16), jax.ShapeDtypeStruct((2, S, DH), BF16), jax.ShapeDtypeStruct((2, DH, S), BF16),
                   jax.ShapeDtypeStruct((2, S, DH), BF16), jax.ShapeDtypeStruct((2, DHA, S), BF16),
                   jax.ShapeDtypeStruct((S, 256), F32), jax.ShapeDtypeStruct((S, 256), F32)],
        grid=(S // tm,),
        in_specs=[pl.BlockSpec((tm, 512), lambda i: (i, O_QA // 512)), pl.BlockSpec((tm, 256), lambda i: (i, O_KA // 256)),
                  pl.BlockSpec((tm, 256), lambda i: (i, O_QR // 256)), pl.BlockSpec((tm, 256), lambda i: (i, O_KR // 256)),
                  pl.BlockSpec((tm, 128), lambda i: (i, 0)), pl.BlockSpec((tm, 128), lambda i: (i, 0)), _full((1, 128)), _full((1, 128))],
        out_specs=[ht(8, DH), hm(2), ht(2, DH), hm(2), ht(2, DHA), pl.BlockSpec((tm, 256), lambda i: (i, 0)), pl.BlockSpec((tm, 256), lambda i: (i, 0))],
        compiler_params=_cp(("parallel",)),
    )(p, p, p, p, cos, sin, qg, kg)


def _attn_fwd(qt, kh, vta):
    S = qt.shape[2]
    tq, tk = min(512, S), min(512, S)
    nj = S // tk

    def body(q_ref, k_ref, v_ref, o_ref, ot_ref, lse_ref, m_s, acc_s):
        j = pl.program_id(2)

        @pl.when(j == 0)
        def _():
            m_s[...] = jnp.full_like(m_s, -jnp.inf)
            acc_s[...] = jnp.zeros_like(acc_s)

        k, v = k_ref[0], v_ref[0]
        m_all = m_s[...]
        st = {0: _dot(k, q_ref[0])}
        m_new, acc_new = [], []
        for h in range(4):
            if h + 1 < 4:
                st[h + 1] = _dot(k, q_ref[h + 1])
            m_old = m_all[h:h + 1, :]
            mn = jnp.maximum(m_old, jnp.max(st[h], axis=0, keepdims=True))
            pt = jnp.exp2(st[h] - mn).astype(BF16)
            acc_new.append(jnp.exp2(m_old - mn) * acc_s[h] + _dot(v, pt))
            m_new.append(mn)
            del st[h]
        for h in range(4):
            acc_s[h] = acc_new[h]
            m_s[h:h + 1, :] = m_new[h]

        @pl.when(j == nj - 1)
        def _():
            for h in range(4):
                ot = acc_s[h, 0:DH, :] / acc_s[h, DH:DH + 1, :]
                ot_ref[h] = ot
                o_ref[:, DH * h:DH * h + DH] = ot.T
                lse_ref[0, h:h + 1, :] = m_s[h:h + 1, :] + jnp.log2(acc_s[h, DH:DH + 1, :])

    return pl.pallas_call(
        body, name="attn_fwd",
        out_shape=[jax.ShapeDtypeStruct((S, 512), F32), jax.ShapeDtypeStruct((8, DH, S), F32), jax.ShapeDtypeStruct((2, 4, S), F32)],
        grid=(2, S // tq, nj),
        in_specs=[pl.BlockSpec((4, DH, tq), lambda g, i, j: (g, 0, i)), pl.BlockSpec((1, tk, DH), lambda g, i, j: (g, j, 0)),
                  pl.BlockSpec((1, DHA, tk), lambda g, i, j: (g, 0, j))],
        out_specs=[pl.BlockSpec((tq, 256), lambda g, i, j: (i, g)), pl.BlockSpec((4, DH, tq), lambda g, i, j: (g, 0, i)),
                   pl.BlockSpec((1, 4, tq), lambda g, i, j: (g, 0, i))],
        scratch_shapes=[pltpu.VMEM((8, tq), F32), pltpu.VMEM((4, DHA, tq), F32)],
        compiler_params=_cp(("parallel", "parallel", "arbitrary"), VMEM_BIG),
    )(qt, kh, vta)


def _ret_tables(wf, wb):
    C = CH

    def body(wf_ref, wb_ref, dc_ref, qdf_ref, qdb_ref, kdf_ref, kdb_ref, a_ref):
        def logsig(w):
            z = jnp.exp(-jnp.abs(w))
            u = 1.0 + z
            l1p = jnp.where(u == 1.0, z, jnp.log(u) * (z / jnp.where(u == 1.0, 1.0, u - 1.0)))
            return jnp.minimum(w, 0.0) - l1p

        lgf, lgb = logsig(wf_ref[...]), logsig(wb_ref[...])
        lane4 = lax.broadcasted_iota(jnp.int32, (1, 4), 1)

        def pick(lg, h):
            return jnp.sum(jnp.where(lane4 == h, lg, 0.0), axis=-1, keepdims=True)

        ii = lax.broadcasted_iota(jnp.int32, (C, C), 0).astype(F32)
        jj = lax.broadcasted_iota(jnp.int32, (C, C), 1).astype(F32)
        dif = ii - jj
        hd = lax.broadcasted_iota(jnp.int32, (C, 256), 1) // DH
        lf_l = jnp.zeros((C, 256), F32)
        lb_l = jnp.zeros((C, 256), F32)
        for h in range(HR):
            lf, lb = pick(lgf, h), pick(lgb, h)
            dc_ref[h] = jnp.where(dif >= 0, jnp.exp(lf * jnp.maximum(dif, 0.0)), jnp.exp(lb * jnp.maximum(-dif, 0.0)))
            lf_l = jnp.where(hd == h, lf, lf_l)
            lb_l = jnp.where(hd == h, lb, lb_l)
            a_ref[h:h + 1, :] = jnp.broadcast_to(jnp.exp(lf * C), (1, 128))
            a_ref[HR + h:HR + h + 1, :] = jnp.broadcast_to(jnp.exp(lb * C), (1, 128))
        ri = lax.broadcasted_iota(jnp.int32, (C, 256), 0).astype(F32)
        qdf_ref[...] = jnp.exp(lf_l * (ri + 1.0))
        qdb_ref[...] = jnp.exp(lb_l * (C - ri))
        kdf_ref[...] = jnp.exp(lf_l * (C - 1.0 - ri))
        kdb_ref[...] = jnp.exp(lb_l * ri)

    t = jax.ShapeDtypeStruct((C, 256), F32)
    return pl.pallas_call(body, name="ret_tables",
                          out_shape=[jax.ShapeDtypeStruct((HR, C, C), F32), t, t, t, t, jax.ShapeDtypeStruct((8, 128), F32)])(wf, wb)


def _ret_states(kr2, p, kdf, kdb, adec):
    S = kr2.shape[0]
    C, N = CH, S // CH

    def body(kf_ref, vf_ref, kb_ref, vb_ref, kdf_ref, kdb_ref, a_ref, rf_ref, rb_ref, sf, sb):
        @pl.when(pl.program_id(0) == 0)
        def _():
            sf[...] = jnp.zeros_like(sf)
            sb[...] = jnp.zeros_like(sb)

        rf_ref[0] = sf[...]
        rb_ref[0] = sb[...]
        kdfw = (kf_ref[...] * kdf_ref[...]).astype(BF16)
        kdbw = (kb_ref[...] * kdb_ref[...]).astype(BF16)
        vf, vb = vf_ref[...].astype(BF16), vb_ref[...].astype(BF16)
        kvf = [_dot(kdfw[:, _ks(h)], vf[:, _vs(h)], TN) for h in range(HR)]
        kvb = [_dot(kdbw[:, _ks(h)], vb[:, _vs(h)], TN) for h in range(HR)]
        for h in range(HR):
            sf[h] = a_ref[h:h + 1, :] * sf[h] + kvf[h]
            sb[h] = a_ref[HR + h:HR + h + 1, :] * sb[h] + kvb[h]

    st = jax.ShapeDtypeStruct((N, HR, DH, DV), F32)
    return pl.pallas_call(
        body, name="ret_states", out_shape=[st, st], grid=(N,),
        in_specs=[pl.BlockSpec((C, 256), lambda t: (t, 0)), pl.BlockSpec((C, 512), lambda t: (t, O_VR // 512)),
                  pl.BlockSpec((C, 256), lambda t: (N - 1 - t, 0)), pl.BlockSpec((C, 512), lambda t: (N - 1 - t, O_VR // 512)),
                  _full((C, 256)), _full((C, 256)), _full((8, 128))],
        out_specs=[pl.BlockSpec((1, HR, DH, DV), lambda t: (t, 0, 0, 0)), pl.BlockSpec((1, HR, DH, DV), lambda t: (N - 1 - t, 0, 0, 0))],
        scratch_shapes=[pltpu.VMEM((HR, DH, DV), F32), pltpu.VMEM((HR, DH, DV), F32)],
        compiler_params=_cp(("arbitrary",)),
    )(kr2, p, kr2, p, kdf, kdb, adec)


def _ks(h):
    return slice(DH * h, DH * h + DH)


def _vs(h):
    return slice(DV * h, DV * h + DV)


def _ret_heads_fwd(qb, kb, vb, qfw, qbw, dc_ref, rf_ref, rb_ref):
    hs = range(HR)
    s = [_dot(qb[:, _ks(h)], kb[:, _ks(h)], NT) for h in hs]
    inter = [_dot(qfw[:, _ks(h)], rf_ref[0, h].astype(BF16)) + _dot(qbw[:, _ks(h)], rb_ref[0, h].astype(BF16)) for h in hs]
    sd = [s[h] * dc_ref[h] for h in hs]
    o = [_dot(sd[h].astype(BF16), vb[:, _vs(h)]) + inter[h] for h in hs]
    return sd, o


def _ret_out(qr2, kr2, p, rf, rb, dc, qdf, qdb, gn):
    S = qr2.shape[0]
    C, N = CH, S // CH

    def body(q_ref, k_ref, v_ref, z_ref, rf_ref, rb_ref, dc_ref, qdf_ref, qdb_ref, gn_ref, yr_ref):
        qv = q_ref[...]
        qb, kb, vb = qv.astype(BF16), k_ref[...].astype(BF16), v_ref[...].astype(BF16)
        qfw, qbw = (qv * qdf_ref[...]).astype(BF16), (qv * qdb_ref[...]).astype(BF16)
        _, o = _ret_heads_fwd(qb, kb, vb, qfw, qbw, dc_ref, rf_ref, rb_ref)
        for h in range(HR):
            vs = _vs(h)
            mu = jnp.mean(o[h], axis=-1, keepdims=True)
            var = jnp.mean(jnp.square(o[h] - mu), axis=-1, keepdims=True)
            on = (o[h] - mu) * lax.rsqrt(var + EPS)
            z = z_ref[:, vs]
            yr_ref[:, vs] = ((on * gn_ref[:, vs]) * (z * _sigmoid(z))).astype(BF16)

    return pl.pallas_call(
        body, name="ret_out", out_shape=jax.ShapeDtypeStruct((S, 512), BF16), grid=(N,),
        in_specs=[pl.BlockSpec((C, 256), lambda t: (t, 0)), pl.BlockSpec((C, 256), lambda t: (t, 0)),
                  pl.BlockSpec((C, 512), lambda t: (t, O_VR // 512)), pl.BlockSpec((C, 512), lambda t: (t, O_ZR // 512)),
                  pl.BlockSpec((1, HR, DH, DV), lambda t: (t, 0, 0, 0)), pl.BlockSpec((1, HR, DH, DV), lambda t: (t, 0, 0, 0)),
                  _full((HR, C, C)), _full((C, 256)), _full((C, 256)), _full((1, 512))],
        out_specs=pl.BlockSpec((C, 512), lambda t: (t, 0)),
        compiler_params=_cp(("parallel",)),
    )(qr2, kr2, p, p, rf, rb, dc, qdf, qdb, gn)


def _mid(x, tgt, mod, g_post, o_att, p, yr, w_pa, w_pr, w_out):
    S = x.shape[0]
    tm = min(256, S)

    def body(x_ref, t_ref, mod_ref, gp_ref, o_ref, za_ref, gl_ref, yr_ref, wpa_ref, wpr_ref, wout_ref,
             dout_ref, do_ref, dpm_ref, dyr_ref, mb_ref, dub_ref, yab_ref, dab_ref, drb_ref, sums_ref):
        @pl.when(pl.program_id(0) == 0)
        def _():
            sums_ref[...] = jnp.zeros_like(sums_ref)

        za = za_ref[...]
        sa = _sigmoid(za)
        sil = za * sa
        ov = o_ref[...]
        ya_b = (ov * sil).astype(BF16)
        yr_b = yr_ref[...]
        av = _dot(ya_b, wpa_ref[...])
        rv = _dot(yr_b, wpr_ref[...])
        ga = _sigmoid(gl_ref[:, :D])
        gr = _sigmoid(gl_ref[:, D:])
        mb = (ga * av + gr * rv).astype(BF16)
        u = _dot(mb, wout_ref[...])
        r2 = lax.rsqrt(jnp.mean(u * u, axis=-1, keepdims=True) + EPS)
        un = u * r2
        gp = gp_ref[...]
        yv = un * gp
        gate = mod_ref[2:3, :]
        err = (x_ref[...] + gate * yv) - t_ref[...]
        dout = err * (1.0 / D)
        dout_ref[...] = dout
        dy = dout * gate
        sums_ref[0:1, :] += jnp.sum(dout * yv, axis=0, keepdims=True)
        sums_ref[1:2, :] += jnp.sum(dy * un, axis=0, keepdims=True)
        sums_ref[2:3, :] += jnp.sum(err * err, axis=0, keepdims=True)
        dyg = dy * gp
        du_b = (r2 * (dyg - un * jnp.mean(dyg * un, axis=-1, keepdims=True))).astype(BF16)
        dm = _dot(du_b, wout_ref[...], NT)
        da_b = (dm * ga).astype(BF16)
        dr_b = (dm * gr).astype(BF16)
        dpm_ref[:, :D] = (dm * av * (ga * (1.0 - ga))).astype(BF16)
        dpm_ref[:, D:2 * D] = (dm * rv * (gr * (1.0 - gr))).astype(BF16)
        dya = _dot(da_b, wpa_ref[...], NT)
        dyr_ref[...] = _dot(dr_b, wpr_ref[...], NT)
        dov = dya * sil
        for g in range(4):
            dt = dov[:, 128 * g:128 * g + 128].T
            do_ref[2 * g] = dt[:DH].astype(BF16)
            do_ref[2 * g + 1] = dt[DH:].astype(BF16)
        dpm_ref[:, 2 * D:] = (dya * ov * (sa * (1.0 + za * (1.0 - sa)))).astype(BF16)
        mb_ref[...] = mb
        dub_ref[...] = du_b
        yab_ref[...] = ya_b
        dab_ref[...] = da_b
        drb_ref[...] = dr_b

    row = lambda w: pl.BlockSpec((tm, w), lambda i: (i, 0))
    sd = lambda w, dt: jax.ShapeDtypeStruct((S, w), dt)
    return pl.pallas_call(
        body, name="mid",
        out_shape=[sd(D, F32), jax.ShapeDtypeStruct((8, DH, S), BF16), sd(2560, BF16), sd(512, F32), sd(D, BF16), sd(D, BF16), sd(512, BF16),
                   sd(D, BF16), sd(D, BF16), jax.ShapeDtypeStruct((8, D), F32)],
        grid=(S // tm,),
        in_specs=[row(D), row(D), _full((3, D)), _full((1, D)), row(512), pl.BlockSpec((tm, 512), lambda i: (i, O_ZA // 512)),
                  pl.BlockSpec((tm, 2048), lambda i: (i, 0)), row(512), _full((512, D)), _full((512, D)), _full((D, D))],
        out_specs=[row(D), pl.BlockSpec((8, DH, tm), lambda i: (0, 0, i)), row(2560), row(512), row(D), row(D), row(512), row(D), row(D),
                   _full((8, D))],
        compiler_params=_cp(("arbitrary",), VMEM_BIG),
    )(x, tgt, mod, g_post, o_att, p, p, yr, w_pa, w_pr, w_out)


def _attn_bwd(qt, kh, kt, vh, dot_, ot, lse, xs):
    S = qt.shape[2]
    tq, tk = min(512, S), min(512, S)

    def body(q_ref, k_ref, kt_ref, v_ref, do_ref, o_ref, lse_ref, dq_ref, dk_ref, dv_ref):
        j, i = pl.program_id(1), pl.program_id(2)
        cols = pl.ds(pl.multiple_of(i * tq, tq), tq)
        k, kt, v = k_ref[0], kt_ref[0], v_ref[0]
        lse_all = lse_ref[0]
        st = {0: _dot(k, q_ref[0])}
        dpt = {0: _dot(v, do_ref[0])}
        dk_acc, dv_acc, dqs = None, None, []
        for h in range(4):
            if h + 1 < 4:
                st[h + 1] = _dot(k, q_ref[h + 1])
                dpt[h + 1] = _dot(v, do_ref[h + 1])
            qt_h, dot_h = q_ref[h], do_ref[h]
            delta = jnp.sum(dot_h.astype(F32) * o_ref[h], axis=0, keepdims=True)
            pt = jnp.exp2(st[h] - lse_all[h:h + 1, :])
            dst = (pt * (dpt[h] - delta)).astype(BF16)
            dv_h = _dot(dot_h, pt.astype(BF16), NT)
            dk_h = _dot(qt_h, dst, NT)
            dqs.append(_dot(kt, dst))
            dv_acc = dv_h if dv_acc is None else dv_acc + dv_h
            dk_acc = dk_h if dk_acc is None else dk_acc + dk_h
            del st[h], dpt[h]

        @pl.when(i == 0)
        def _():
            dk_ref[0] = dk_acc
            dv_ref[0] = dv_acc

        @pl.when(i > 0)
        def _():
            dk_ref[0] += dk_acc
            dv_ref[0] += dv_acc

        @pl.when(j == 0)
        def _():
            for h in range(4):
                dq_ref[h, :, cols] = dqs[h]

        @pl.when(j > 0)
        def _():
            for h in range(4):
                dq_ref[h, :, cols] += dqs[h]

    return _host_call(
        body, xs, name="attn_bwd",
        out_shape=[jax.ShapeDtypeStruct((8, DH, S), F32), jax.ShapeDtypeStruct((2, DH, S), F32), jax.ShapeDtypeStruct((2, DH, S), F32)],
        grid=(2, S // tk, S // tq),
        in_specs=[pl.BlockSpec((4, DH, tq), lambda g, j, i: (g, 0, i)), pl.BlockSpec((1, tk, DH), lambda g, j, i: (g, j, 0)),
                  pl.BlockSpec((1, DH, tk), lambda g, j, i: (g, 0, j)), pl.BlockSpec((1, tk, DH), lambda g, j, i: (g, j, 0)),
                  pl.BlockSpec((4, DH, tq), lambda g, j, i: (g, 0, i)), pl.BlockSpec((4, DH, tq), lambda g, j, i: (g, 0, i)),
                  pl.BlockSpec((1, 4, tq), lambda g, j, i: (g, 0, i))],
        out_specs=[pl.BlockSpec((4, DH, S), lambda g, j, i: (g, 0, 0)), pl.BlockSpec((1, DH, tk), lambda g, j, i: (g, 0, j)),
                   pl.BlockSpec((1, DH, tk), lambda g, j, i: (g, 0, j))],
        scratch_shapes=[], operands=(qt, kh, kt, vh, dot_, ot, lse),
        compiler_params=_cp(("arbitrary", "arbitrary", "arbitrary"), VMEM_BIG),
    )


def _attn_prep_bwd(dqt, dkt, dvt, p, cos, sin, qg, kg):
    S = dqt.shape[2]
    tm = min(512, S)

    def body(dq_ref, dk_ref, dv_ref, qa_ref, ka_ref, cos_ref, sin_ref, qg_ref, kg_ref, dp_ref, gs_ref):
        @pl.when(pl.program_id(0) == 0)
        def _():
            gs_ref[...] = jnp.zeros_like(gs_ref)

        cos_v, sin_v = cos_ref[...], sin_ref[...]

        def pair(ref, a):
            return jnp.concatenate([ref[a], ref[a + 1]], axis=0).T

        def norm_bwd(dyv, xv, gv, row):
            r = lax.rsqrt(_head_mean(xv * xv) + EPS)
            xn = xv * r
            dxh = _rope_t(dyv, cos_v, sin_v)
            gs_ref[row:row + 1, :] += jnp.sum(dxh * xn, axis=0, keepdims=True)
            dg = dxh * gv
            return r * (dg - xn * _head_mean(dg * xn))

        for g in range(4):
            sl = slice(128 * g, 128 * g + 128)
            dp_ref[:, sl] = norm_bwd(pair(dq_ref, 2 * g) * 0.125, qa_ref[:, sl], qg_ref[...], 0).astype(BF16)
        dp_ref[:, 512:640] = norm_bwd(pair(dk_ref, 0) * LN2, ka_ref[...], kg_ref[...], 1).astype(BF16)
        dp_ref[:, 640:768] = pair(dv_ref, 0).astype(BF16)

    ht = lambda n: pl.BlockSpec((n, DH, tm), lambda i: (0, 0, i))
    return pl.pallas_call(
        body, name="attn_prep_bwd", out_shape=[jax.ShapeDtypeStruct((S, 768), BF16), jax.ShapeDtypeStruct((8, 128), F32)],
        grid=(S // tm,),
        in_specs=[ht(8), ht(2), ht(2),
                  pl.BlockSpec((tm, 512), lambda i: (i, O_QA // 512)), pl.BlockSpec((tm, 128), lambda i: (i, O_KA // 128)),
                  pl.BlockSpec((tm, 128), lambda i: (i, 0)), pl.BlockSpec((tm, 128), lambda i: (i, 0)), _full((1, 128)), _full((1, 128))],
        out_specs=[pl.BlockSpec((tm, 768), lambda i: (i, 0)), _full((8, 128))],
        compiler_params=_cp(("arbitrary",)),
    )(dqt, dkt, dvt, p, p, cos, sin, qg, kg)


def _ret_bwd_chunk(qr2, kr2, p, rf, rb, dc, qdf, qdb, gn, dyr, cos, sin, xs):
    S = qr2.shape[0]
    C, N = CH, S // CH

    def body(q_ref, k_ref, v_ref, z_ref, rf_ref, rb_ref, dc_ref, qdf_ref, qdb_ref, gn_ref, dyr_ref, cos_ref, sin_ref,
             dpa_ref, dk_ref, dv_ref, drf_ref, drb_ref, dgn_ref, dlg_ref, dqs):
        @pl.when(pl.program_id(0) == 0)
        def _():
            dgn_ref[...] = jnp.zeros_like(dgn_ref)
            dlg_ref[...] = jnp.zeros_like(dlg_ref)

        qv = q_ref[...]
        qb, kb, vb = qv.astype(BF16), k_ref[...].astype(BF16), v_ref[...].astype(BF16)
        qf32, qb32 = qv * qdf_ref[...], qv * qdb_ref[...]
        qfw, qbw = qf32.astype(BF16), qb32.astype(BF16)
        ii = lax.broadcasted_iota(jnp.int32, (C, C), 0).astype(F32)
        jj = lax.broadcasted_iota(jnp.int32, (C, C), 1).astype(F32)
        dif = ii - jj
        ri = lax.broadcasted_iota(jnp.int32, (C, 1), 0).astype(F32)
        hs = range(HR)
        sd, o = _ret_heads_fwd(qb, kb, vb, qfw, qbw, dc_ref, rf_ref, rb_ref)
        do_b = []
        for h in hs:
            vs = _vs(h)
            mu = jnp.mean(o[h], axis=-1, keepdims=True)
            rstd = lax.rsqrt(jnp.mean(jnp.square(o[h] - mu), axis=-1, keepdims=True) + EPS)
            on = (o[h] - mu) * rstd
            z = z_ref[:, vs]
            sz = _sigmoid(z)
            dy = dyr_ref[:, vs]
            gnv = gn_ref[:, vs]
            dpa_ref[:, 256 + DV * h:256 + DV * h + DV] = (dy * (on * gnv) * (sz * (1.0 + z * (1.0 - sz)))).astype(BF16)
            dys = dy * (z * sz)
            dgn_ref[:, vs] += jnp.sum(dys * on, axis=0, keepdims=True)
            don = dys * gnv
            do = rstd * (don - jnp.mean(don, axis=-1, keepdims=True) - on * jnp.mean(don * on, axis=-1, keepdims=True))
            do_b.append(do.astype(BF16))
        dpm = [_dot(do_b[h], vb[:, _vs(h)], NT) for h in hs]
        dqf = [_dot(do_b[h], rf_ref[0, h].astype(BF16), NT) for h in hs]
        dqb = [_dot(do_b[h], rb_ref[0, h].astype(BF16), NT) for h in hs]
        for h in hs:
            dv_ref[:, _vs(h)] = _dot(sd[h].astype(BF16), do_b[h], TN)
            drf_ref[0, h] = _dot(qfw[:, _ks(h)], do_b[h], TN)
            drb_ref[0, h] = _dot(qbw[:, _ks(h)], do_b[h], TN)
        dsd = [(dpm[h] * dc_ref[h]).astype(BF16) for h in hs]
        for h in hs:
            ks = _ks(h)
            dqs[:, ks] = _dot(dsd[h], kb[:, ks]) + dqf[h] * qdf_ref[:, ks] + dqb[h] * qdb_ref[:, ks]
            dk_ref[:, ks] = _dot(dsd[h], qb[:, ks], TN)
        for h in hs:
            ks = _ks(h)
            e = dpm[h] * sd[h]
            lf = _sum11(e * jnp.maximum(dif, 0.0)) + _sum11(jnp.sum(qf32[:, ks] * dqf[h], axis=-1, keepdims=True) * (ri + 1.0))
            lb = _sum11(e * jnp.maximum(-dif, 0.0)) + _sum11(jnp.sum(qb32[:, ks] * dqb[h], axis=-1, keepdims=True) * (C - ri))
            dlg_ref[h:h + 1, :] += jnp.broadcast_to(lf, (1, 128))
            dlg_ref[HR + h:HR + h + 1, :] += jnp.broadcast_to(lb, (1, 128))
        cos_v, sin_v = cos_ref[...], sin_ref[...]
        for g in range(2):
            sl = slice(128 * g, 128 * g + 128)
            dpa_ref[:, sl] = _rope_t(dqs[:, sl], cos_v, sin_v).astype(BF16)

    st = jax.ShapeDtypeStruct((N, HR, DH, DV), F32)
    stb = lambda: pl.BlockSpec((1, HR, DH, DV), lambda t: (t, 0, 0, 0))
    return _host_call(
        body, xs, name="ret_bwd_chunk",
        out_shape=[jax.ShapeDtypeStruct((S, 768), BF16), jax.ShapeDtypeStruct((S, 256), F32), jax.ShapeDtypeStruct((S, 512), F32), st, st,
                   jax.ShapeDtypeStruct((1, 512), F32), jax.ShapeDtypeStruct((8, 128), F32)],
        grid=(N,),
        in_specs=[pl.BlockSpec((C, 256), lambda t: (t, 0)), pl.BlockSpec((C, 256), lambda t: (t, 0)),
                  pl.BlockSpec((C, 512), lambda t: (t, O_VR // 512)), pl.BlockSpec((C, 512), lambda t: (t, O_ZR // 512)),
                  stb(), stb(), _full((HR, C, C)), _full((C, 256)), _full((C, 256)), _full((1, 512)),
                  pl.BlockSpec((C, 512), lambda t: (t, 0)), pl.BlockSpec((C, 128), lambda t: (t, 0)), pl.BlockSpec((C, 128), lambda t: (t, 0))],
        out_specs=[pl.BlockSpec((C, 768), lambda t: (t, 0)), pl.BlockSpec((C, 256), lambda t: (t, 0)), pl.BlockSpec((C, 512), lambda t: (t, 0)),
                   stb(), stb(), _full((1, 512)), _full((8, 128))],
        scratch_shapes=[pltpu.VMEM((C, 256), F32)], operands=(qr2, kr2, p, p, rf, rb, dc, qdf, qdb, gn, dyr, cos, sin),
        compiler_params=_cp(("arbitrary",)),
    )


def _ret_bwd_scan(kr2, p, rf, rb, drf, drb, kdf, kdb, adec):
    S = kr2.shape[0]
    C, N = CH, S // CH

    def body(kf_ref, vf_ref, kb_ref, vb_ref, rf_ref, rb_ref, drf_ref, drb_ref, kdf_ref, kdb_ref, a_ref,
             dkf_ref, dkb_ref, dvf_ref, dvb_ref, dlg_ref, gf, gb):
        @pl.when(pl.program_id(0) == 0)
        def _():
            gf[...] = jnp.zeros_like(gf)
            gb[...] = jnp.zeros_like(gb)
            dlg_ref[...] = jnp.zeros_like(dlg_ref)

        ri = lax.broadcasted_iota(jnp.int32, (C, 1), 0).astype(F32)

        def one(k_ref, v_ref, r_ref, dr_ref, kd_ref, g_s, dk_ref, dv_ref, row0, wexp):
            kd32 = k_ref[...] * kd_ref[...]
            kdw = kd32.astype(BF16)
            vb = v_ref[...].astype(BF16)
            for h in range(HR):
                ks, vs = _ks(h), _vs(h)
                gst = g_s[h]
                g_b = gst.astype(BF16)
                dkd = _dot(vb[:, vs], g_b, NT)
                dk_ref[:, ks] = dkd * kd_ref[:, ks]
                dv_ref[:, vs] = _dot(kdw[:, ks], g_b)
                av = a_ref[row0 + h:row0 + h + 1, :]
                lg = (_sum11(jnp.sum(kd32[:, ks] * dkd, axis=-1, keepdims=True) * wexp)
                      + C * av[:, 0:1] * _sum11(r_ref[0, h] * gst))
                dlg_ref[row0 + h:row0 + h + 1, :] += jnp.broadcast_to(lg, (1, 128))
                g_s[h] = dr_ref[0, h] + av * gst

        one(kf_ref, vf_ref, rf_ref, drf_ref, kdf_ref, gf, dkf_ref, dvf_ref, 0, C - 1.0 - ri)
        one(kb_ref, vb_ref, rb_ref, drb_ref, kdb_ref, gb, dkb_ref, dvb_ref, HR, ri)

    fwd = lambda w, off=0: pl.BlockSpec((C, w), lambda t: (N - 1 - t, off))
    bwd = lambda w, off=0: pl.BlockSpec((C, w), lambda t: (t, off))
    stf = lambda: pl.BlockSpec((1, HR, DH, DV), lambda t: (N - 1 - t, 0, 0, 0))
    stb = lambda: pl.BlockSpec((1, HR, DH, DV), lambda t: (t, 0, 0, 0))
    return pl.pallas_call(
        body, name="ret_bwd_scan",
        out_shape=[jax.ShapeDtypeStruct((S, 256), F32), jax.ShapeDtypeStruct((S, 256), F32), jax.ShapeDtypeStruct((S, 512), F32),
                   jax.ShapeDtypeStruct((S, 512), F32), jax.ShapeDtypeStruct((8, 128), F32)],
        grid=(N,),
        in_specs=[fwd(256), fwd(512, O_VR // 512), bwd(256), bwd(512, O_VR // 512), stf(), stb(), stf(), stb(),
                  _full((C, 256)), _full((C, 256)), _full((8, 128))],
        out_specs=[fwd(256), bwd(256), fwd(512), bwd(512), _full((8, 128))],
        scratch_shapes=[pltpu.VMEM((HR, DH, DV), F32), pltpu.VMEM((HR, DH, DV), F32)],
        compiler_params=_cp(("arbitrary",)),
    )(kr2, p, kr2, p, rf, rb, drf, drb, kdf, kdb, adec)


def _ret_bwd_final(dk_i, dkf, dkb, dv_i, dvf, dvb, cos, sin):
    S = dk_i.shape[0]
    tm = min(512, S)

    def body(a_ref, b_ref, c_ref, d_ref, e_ref, f_ref, cos_ref, sin_ref, o_ref):
        o_ref[:, :512] = (d_ref[...] + e_ref[...] + f_ref[...]).astype(BF16)
        cos_v, sin_v = cos_ref[...], sin_ref[...]
        for g in range(2):
            sl = slice(128 * g, 128 * g + 128)
            dk = a_ref[:, sl] + b_ref[:, sl] + c_ref[:, sl]
            o_ref[:, 512 + 128 * g:512 + 128 * g + 128] = (_rope_t(dk, cos_v, sin_v) * 0.125).astype(BF16)

    row = lambda w: pl.BlockSpec((tm, w), lambda i: (i, 0))
    return pl.pallas_call(
        body, name="ret_bwd_final", out_shape=jax.ShapeDtypeStruct((S, 768), BF16), grid=(S // tm,),
        in_specs=[row(256), row(256), row(256), row(512), row(512), row(512), row(128), row(128)], out_specs=row(768),
        compiler_params=_cp(("parallel",)),
    )(dk_i, dkf, dkb, dv_i, dvf, dvb, cos, sin)


def _bwd_in(dpm, dpa, dpra, dprb, w_p, x, dout, mod, g_pre, xs):
    S = x.shape[0]
    tm = min(256, S)

    def body(a_ref, b_ref, c_ref, d_ref, w_ref, x_ref, dout_ref, mod_ref, g_ref, gx_ref, sums_ref):
        @pl.when(pl.program_id(0) == 0)
        def _():
            sums_ref[...] = jnp.zeros_like(sums_ref)

        dh = (_dot(a_ref[...], w_ref[:, :O_QA], NT) + _dot(b_ref[...], w_ref[:, O_QA:O_QR], NT)
              + _dot(c_ref[...], w_ref[:, O_QR:O_VR], NT) + _dot(d_ref[...], w_ref[:, O_VR:], NT))
        xv = x_ref[...]
        r = lax.rsqrt(jnp.mean(xv * xv, axis=-1, keepdims=True) + EPS)
        xn = xv * r
        gv = g_ref[...]
        sc1 = 1.0 + mod_ref[1:2, :]
        sums_ref[0:1, :] += jnp.sum(dh, axis=0, keepdims=True)
        sums_ref[1:2, :] += jnp.sum(dh * (xn * gv), axis=0, keepdims=True)
        sums_ref[2:3, :] += jnp.sum(dh * xn, axis=0, keepdims=True) * sc1
        dxn = dh * (gv * sc1)
        gx_ref[...] = dout_ref[...] + r * (dxn - xn * jnp.mean(dxn * xn, axis=-1, keepdims=True))

    row = lambda w: pl.BlockSpec((tm, w), lambda i: (i, 0))
    return _host_call(
        body, xs, name="bwd_in", out_shape=[jax.ShapeDtypeStruct((S, D), F32), jax.ShapeDtypeStruct((8, D), F32)], grid=(S // tm,),
        in_specs=[row(2560), row(768), row(768), row(768), _full((D, P_W)), row(D), row(D), _full((3, D)), _full((1, D))],
        out_specs=[row(D), _full((8, D))], scratch_shapes=[], operands=(dpm, dpa, dpra, dprb, w_p, x, dout, mod, g_pre),
        compiler_params=_cp(("arbitrary",), VMEM_BIG),
    )


SMALL = ("b_ada", "g_pre", "qn_g", "kn_g", "w_dec_f", "w_dec_b", "gn_g", "g_post")


def _small_update(gathered, wmv):
    ns = len(SMALL)

    def body(*refs):
        gin_ref, gmid_ref, ggn_ref, gatt_ref, gl1_ref, gl2_ref = refs[:6]
        wmv_refs = refs[6:6 + 3 * ns]
        loss_ref = refs[6 + 3 * ns]
        out_refs = refs[7 + 3 * ns:]

        def dsum(ref, r=None):
            rows = slice(None) if r is None else slice(r, r + 1)
            acc = ref[0, rows, :]
            for d in range(1, NDEV):
                acc = acc + ref[d, rows, :]
            return acc

        s_lg = dsum(gl1_ref) + dsum(gl2_ref)
        loss_ref[...] = (0.5 / D) * jnp.sum(dsum(gmid_ref, 2), axis=-1, keepdims=True)
        eye = lax.broadcasted_iota(jnp.int32, (8, 128), 0) == lax.broadcasted_iota(jnp.int32, (8, 128), 1)
        dlg = jnp.sum(jnp.where(eye, s_lg, 0.0), axis=0, keepdims=True)
        w_f, w_b = wmv_refs[3 * SMALL.index("w_dec_f")][...], wmv_refs[3 * SMALL.index("w_dec_b")][...]
        s_q, s_k = dsum(gatt_ref, 0), dsum(gatt_ref, 1)
        grads = dict(
            b_ada=jnp.concatenate([dsum(gin_ref, 0), dsum(gin_ref, 1), dsum(gmid_ref, 0)], axis=1),
            g_pre=dsum(gin_ref, 2), g_post=dsum(gmid_ref, 1), gn_g=dsum(ggn_ref),
            qn_g=s_q[:, :DH] + s_q[:, DH:], kn_g=s_k[:, :DH] + s_k[:, DH:],
            w_dec_f=dlg[:, 0:HR] * _sigmoid(-w_f), w_dec_b=dlg[:, HR:2 * HR] * _sigmoid(-w_b))
        for i, nme in enumerate(SMALL):
            g = grads[nme]
            w_ref, m_ref, v_ref = wmv_refs[3 * i:3 * i + 3]
            g_ref, d_ref, nm_ref, nv_ref = out_refs[4 * i:4 * i + 4]
            g_ref[...] = g
            m2 = ADAM_B1 * m_ref[...] + (1.0 - ADAM_B1) * g
            v2 = ADAM_B2 * v_ref[...] + (1.0 - ADAM_B2) * jnp.square(g)
            m_hat = m2 / (1.0 - ADAM_B1 ** ADAM_STEP)
            v_hat = v2 / (1.0 - ADAM_B2 ** ADAM_STEP)
            d_ref[...] = -ADAM_LR * (m_hat / (jnp.sqrt(v_hat) + ADAM_EPS) + ADAM_WD * w_ref[...])
            nm_ref[...] = m2
            nv_ref[...] = v2

    out_shape = [jax.ShapeDtypeStruct((1, 1), F32)]
    for i in range(ns):
        out_shape += [jax.ShapeDtypeStruct(wmv[3 * i].shape, F32)] * 4
    return pl.pallas_call(body, name="small_update", out_shape=out_shape)(*gathered, *wmv)


def _adamw(parts, w, m, v, name):
    n, R, L = parts.shape
    tr = 256 if (R % 256 == 0 and R > 256) else R

    def body(p_ref, w_ref, m_ref, v_ref, g_ref, d_ref, nm_ref, nv_ref):
        g = p_ref[0].astype(F32)
        for k in range(1, n):
            g = g + p_ref[k].astype(F32)
        g_ref[...] = g
        m2 = ADAM_B1 * m_ref[...] + (1.0 - ADAM_B1) * g
        v2 = ADAM_B2 * v_ref[...] + (1.0 - ADAM_B2) * jnp.square(g)
        m_hat = m2 / (1.0 - ADAM_B1 ** ADAM_STEP)
        v_hat = v2 / (1.0 - ADAM_B2 ** ADAM_STEP)
        d_ref[...] = -ADAM_LR * (m_hat / (jnp.sqrt(v_hat) + ADAM_EPS) + ADAM_WD * w_ref[...])
        nm_ref[...] = m2
        nv_ref[...] = v2

    blk = pl.BlockSpec((tr, L), lambda i: (i, 0))
    o = jax.ShapeDtypeStruct((R, L), F32)
    return pl.pallas_call(
        body, name=name, out_shape=[o, o, o, o], grid=(R // tr,),
        in_specs=[pl.BlockSpec((n, tr, L), lambda i: (0, i, 0)), blk, blk, blk], out_specs=[blk, blk, blk, blk],
        compiler_params=_cp(("parallel",), VMEM_BIG),
    )(parts, w, m, v)


def _rope_tables(S):
    f = np.float32
    t = np.arange(S)
    row, col = (t // 64).astype(f), (t % 64).astype(f)
    half = DH // 2
    inv = np.power(f(ROPE_THETA), -np.arange(0, half, 2, dtype=f) / f(half)).astype(f)
    ar, ac = (row[:, None] * inv[None, :]).astype(f), (col[:, None] * inv[None, :]).astype(f)
    cos64 = np.concatenate([np.cos(ar), np.cos(ar), np.cos(ac), np.cos(ac)], axis=1).astype(f)
    sin64 = np.concatenate([-np.sin(ar), np.sin(ar), -np.sin(ac), np.sin(ac)], axis=1).astype(f)
    return jnp.asarray(np.tile(cos64, (1, 2))), jnp.asarray(np.tile(sin64, (1, 2)))


def _to_p_order(w_orig):
    return jnp.concatenate([w_orig[:, ORIG[n][0]:ORIG[n][1]] for n in P_ORDER], axis=1)


def _pad_lanes(v, n):
    return jnp.pad(v, ((0, 0), (0, n - v.shape[1])))


def kernel(x, c, w_ada, b_ada, g_pre, w_in, qn_g, kn_g, w_dec_f, w_dec_b, gn_g, w_pa, w_pr, w_out, g_post, loss_target, m_w_ada, m_b_ada, m_g_pre, m_w_in, m_qn_g, m_kn_g, m_w_dec_f, m_w_dec_b, m_gn_g, m_w_pa, m_w_pr, m_w_out, m_g_post, v_w_ada, v_b_ada, v_g_pre, v_w_in, v_qn_g, v_kn_g, v_w_dec_f, v_w_dec_b, v_gn_g, v_w_pa, v_w_pr, v_w_out, v_g_post):
    S = x.shape[1]
    me = 4 * lax.axis_index("x") + 2 * lax.axis_index("y") + lax.axis_index("c")
    xs, tgt = x[0], loss_target[0]
    ncol_ada = w_ada.shape[2]
    ncol_in = w_in.shape[2]

    b_ada_s = lax.dynamic_slice(b_ada, (0, me * ncol_ada), (1, ncol_ada))
    mod_all, c_act, (wg_in, wg_pa, wg_pr, wg_out) = _prologue(
        jnp.pad(c, ((0, 7), (0, 0))), w_ada[0], b_ada_s,
        [w_in[0].astype(BF16), w_pa[0].astype(BF16), w_pr[0].astype(BF16), w_out[0].astype(BF16)])
    mod = lax.dynamic_index_in_dim(mod_all, me, axis=1, keepdims=False).reshape(3, D)
    w_p = _to_p_order(wg_in.transpose(1, 0, 2).reshape(D, NDEV * ncol_in))
    w_pa_f = wg_pa.transpose(1, 0, 2).reshape(512, D)
    w_pr_f = wg_pr.transpose(1, 0, 2).reshape(512, D)
    w_out_f = wg_out.reshape(D, D)

    cos, sin = _rope_tables(S)
    qg, kg = jnp.tile(qn_g, (1, 2)), jnp.tile(kn_g, (1, 2))

    p, h = _fwd_in(xs, mod, g_pre, w_p)
    qt, kh, kt, vh, vta, qr2, kr2 = _prep(p, cos, sin, qg, kg)
    o_att, o_t, lse = _attn_fwd(qt, kh, vta)
    dc, qdf, qdb, kdf, kdb, adec = _ret_tables(w_dec_f, w_dec_b)
    rf, rb = _ret_states(kr2, p, kdf, kdb, adec)
    yr = _ret_out(qr2, kr2, p, rf, rb, dc, qdf, qdb, gn_g)

    dout, do, dpm, dyr, mb, dub, yab, dab, drb_, sums_mid = _mid(xs, tgt, mod, g_post, o_att, p, yr, w_pa_f, w_pr_f, w_out_f)
    gw_out = _mm_tn(mb, dub, "gw_out")
    gw_pa = _mm_tn(yab, dab, "gw_pa")
    gw_pr = _mm_tn(yr, drb_, "gw_pr")
    gi_m = _mm_tn(h, dpm, "gw_in_mid")

    def shards(cols, nd):
        return cols.astype(BF16).reshape(D, nd, ncol_in).transpose(1, 0, 2)

    all_dev = tuple(range(NDEV))
    (dqt, dkt, dvt), (rs_out, rs_pa, rs_pr, rs_in) = _attn_bwd(qt, kh, kt, vh, do, o_t, lse, [
        (gw_out.astype(BF16).reshape(NDEV, 128, D), all_dev, None),
        (gw_pa.astype(BF16).reshape(512, NDEV, 128).transpose(1, 0, 2), all_dev, None),
        (gw_pr.astype(BF16).reshape(512, NDEV, 128).transpose(1, 0, 2), all_dev, None),
        (shards(gi_m[:, 224:2048], 3), (5, 6, 7), None)])
    dpa, gs_att = _attn_prep_bwd(dqt, dkt, dvt, p, cos, sin, qg, kg)
    gi_a = _mm_tn(h, dpa, "gw_in_att")
    (dpra, dk_i, dv_i, drf, drb, dgn, dlg1), _ = _ret_bwd_chunk(qr2, kr2, p, rf, rb, dc, qdf, qdb, gn_g, dyr, cos, sin, [])
    dkf, dkb, dvf, dvb, dlg2 = _ret_bwd_scan(kr2, p, rf, rb, drf, drb, kdf, kdb, adec)
    dprb = _ret_bwd_final(dk_i, dkf, dkb, dv_i, dvf, dvb, cos, sin)
    gi_ra = _mm_tn(h, dpra, "gw_in_reta")
    gi_rb = _mm_tn(h, dprb, "gw_in_retb")
    (grad_x, sums_in), (rs_in,) = _bwd_in(dpm, dpa, dpra, dprb, w_p, xs, dout, mod, g_pre, [
        (shards(jnp.concatenate([gi_a, gi_m[:, 2048:2560], gi_ra[:, :256], gi_rb[:, 512:768], gi_rb[:, :512], gi_ra[:, 256:768],
                                 gi_m[:, :224]], axis=1), 5), (0, 1, 2, 3, 4), rs_in)])

    gathered = _small_allgather([sums_in, sums_mid, dgn, gs_att, dlg1, dlg2], "ag_small")
    given = dict(b_ada=(b_ada, m_b_ada, v_b_ada), g_pre=(g_pre, m_g_pre, v_g_pre), qn_g=(qn_g, m_qn_g, v_qn_g), kn_g=(kn_g, m_kn_g, v_kn_g),
                 w_dec_f=(w_dec_f, m_w_dec_f, v_w_dec_f), w_dec_b=(w_dec_b, m_w_dec_b, v_w_dec_b), gn_g=(gn_g, m_gn_g, v_gn_g),
                 g_post=(g_post, m_g_post, v_g_post))
    small = _small_update(gathered, [a for nme in SMALL for a in given[nme]])
    loss = small[0][0, 0]

    g_in_all, g_mid_all = gathered[0], gathered[1]
    dmod_all = lax.dynamic_slice(jnp.concatenate([g_in_all[:, 0, :], g_in_all[:, 1, :], g_mid_all[:, 0, :]], axis=1),
                                 (0, me * ncol_ada), (NDEV, ncol_ada))
    g_ada = _mm_tn(c_act, jnp.pad(dmod_all, ((0, 8), (0, 0))).astype(BF16), "gw_ada")

    res = dict(
        w_ada=_adamw(g_ada[None], w_ada[0], m_w_ada[0], v_w_ada[0], "adamw_ada"),
        w_in=_adamw(rs_in, w_in[0], m_w_in[0], v_w_in[0], "adamw_in"),
        w_pa=_adamw(rs_pa, w_pa[0], m_w_pa[0], v_w_pa[0], "adamw_pa"),
        w_pr=_adamw(rs_pr, w_pr[0], m_w_pr[0], v_w_pr[0], "adamw_pr"),
        w_out=_adamw(rs_out, w_out[0], m_w_out[0], v_w_out[0], "adamw_out"),
    )
    names = ["w_ada", "b_ada", "g_pre", "w_in", "qn_g", "kn_g", "w_dec_f", "w_dec_b", "gn_g", "w_pa", "w_pr", "w_out", "g_post"]
    outs = [[], [], [], []]
    for nme in names:
        for q in range(4):
            if nme in res:
                outs[q].append(res[nme][q][None])
            else:
                outs[q].append(small[1 + 4 * SMALL.index(nme) + q])
    return (loss, grad_x[None], *outs[0], *outs[1], *outs[2], *outs[3])
```

```python
import jax
import jax.numpy as jnp
import numpy as np
from jax import lax
from jax.experimental import pallas as pl
from jax.experimental.pallas import tpu as pltpu

F32, BF16 = jnp.float32, jnp.bfloat16
D = 1024
DH = 64
DHA = 80
DV = 128
LOG2E = 1.4426950408889634
LN2 = 0.6931471805599453
HR = 4
CH = 128
EPS = 1e-6
ROPE_THETA = 10000.0
NDEV = 8
O_GL, O_ZA, O_QA, O_KA, O_VA, O_QR, O_ZR, O_VR, O_KR, P_W = 0, 2048, 2560, 3072, 3200, 3328, 3584, 4096, 4608, 4864
ORIG = dict(qa=(0, 512), ka=(512, 640), va=(640, 768), za=(768, 1280), qr=(1280, 1536), kr=(1536, 1792),
            vr=(1792, 2304), zr=(2304, 2816), gl=(2816, 4864))
P_ORDER = ("gl", "za", "qa", "ka", "va", "qr", "zr", "vr", "kr")
ADAM_LR, ADAM_B1, ADAM_B2, ADAM_EPS, ADAM_WD, ADAM_STEP = 0.001, 0.9, 0.999, 1e-08, 0.01, 10
VMEM_BIG = 56 * 1024 * 1024
MESH = pl.DeviceIdType.MESH

NT = (((1,), (1,)), ((), ()))
TN = (((0,), (0,)), ((), ()))


def _dot(a, b, dims=None):
    if dims is None:
        return jnp.dot(a, b, preferred_element_type=F32)
    return lax.dot_general(a, b, dims, preferred_element_type=F32)


def _cp(sem=None, vmem=None):
    kw = {}
    if sem is not None:
        kw["dimension_semantics"] = sem
    if vmem is not None:
        kw["vmem_limit_bytes"] = vmem
    return pltpu.CompilerParams(**kw)


def _sigmoid(z):
    return 1.0 / (1.0 + jnp.exp(-z))


def _sum11(m):
    return jnp.sum(jnp.sum(m, axis=-1, keepdims=True), axis=0, keepdims=True)


def _full(shape):
    n = len(shape)
    return pl.BlockSpec(shape, lambda *_: (0,) * n)


def _my_pos():
    return lax.axis_index("x"), lax.axis_index("y"), lax.axis_index("c")


def _peer(k, x, y, c):
    return ((1 - x) if k & 4 else x, (1 - y) if k & 2 else y, (1 - c) if k & 1 else c)


def _small_allgather(vs, name):
    n = len(vs)

    def body(*refs):
        v_refs, out_refs = refs[:n], refs[n:2 * n]
        send_sems, recv_sems = refs[2 * n:]
        x, y, c = _my_pos()
        me = 4 * x + 2 * y + c
        cps = []
        for a in range(n):
            out_refs[a][me] = v_refs[a][...]
            for k in range(1, NDEV):
                cp = pltpu.make_async_remote_copy(src_ref=v_refs[a], dst_ref=out_refs[a].at[me], send_sem=send_sems.at[a, k - 1],
                                                  recv_sem=recv_sems.at[a, k - 1], device_id=_peer(k, x, y, c), device_id_type=MESH)
                cp.start()
                cps.append(cp)
        for cp in cps:
            cp.wait()

    vm = pl.BlockSpec(memory_space=pltpu.VMEM)
    return pl.pallas_call(
        body, name=name, out_shape=[jax.ShapeDtypeStruct((NDEV,) + v.shape, v.dtype) for v in vs],
        in_specs=[vm] * n, out_specs=[vm] * n,
        scratch_shapes=[pltpu.SemaphoreType.DMA((n, NDEV - 1)), pltpu.SemaphoreType.DMA((n, NDEV - 1))],
    )(*vs)


def _prologue(c8, w_ada_s, b_ada_s, arrs):
    n = len(arrs)
    ncol = w_ada_s.shape[1]

    def body(*refs):
        c_ref, wa_ref, ba_ref = refs[:3]
        ins = refs[3:3 + n]
        mod_ref, cact_ref = refs[3 + n:5 + n]
        outs = refs[5 + n:5 + 2 * n]
        call_ref, send_sems, recv_sems, local_sems, s_send, s_recv = refs[5 + 2 * n:]
        x, y, c = _my_pos()
        me, sibling = (x, y, c), (x, y, 1 - c)
        chips = [(1 - x, y), (x, 1 - y), (1 - x, 1 - y)]
        me_i = 4 * x + 2 * y + c

        def small_gather(src_ref, dst_ref, row):
            cps = []
            for k in range(1, NDEV):
                cp = pltpu.make_async_remote_copy(src_ref=src_ref, dst_ref=dst_ref.at[me_i], send_sem=s_send.at[row, k - 1],
                                                  recv_sem=s_recv.at[row, k - 1], device_id=_peer(k, x, y, c), device_id_type=MESH)
                cp.start()
                cps.append(cp)
            return cps

        def blk(a, px, py, pc):
            return outs[a].at[4 * px + 2 * py + pc]

        def copy(a, k, block, to, src=None):
            return pltpu.make_async_remote_copy(src_ref=blk(a, *block) if src is None else src, dst_ref=blk(a, *block),
                                                send_sem=send_sems.at[a, k], recv_sem=recv_sems.at[a, k], device_id=to, device_id_type=MESH)

        call_ref[me_i] = c_ref[...]
        for cp in small_gather(c_ref, call_ref, 0):
            cp.wait()

        local, sent = [], []
        for a in range(n):
            mine = pltpu.make_async_copy(ins[a], blk(a, *me), local_sems.at[a])
            mine.start()
            local.append(mine)
            first = [copy(a, 0, me, sibling, src=ins[a])] + [copy(a, 1 + j, me, (*chip, c), src=ins[a]) for j, chip in enumerate(chips)]
            for cp in first:
                cp.start()
            sent += first

        cv = call_ref[:, 0, :]
        ca = jnp.concatenate([cv * _sigmoid(cv), jnp.zeros_like(cv)], axis=0).astype(BF16)
        cact_ref[...] = ca
        mod_ref[me_i] = (_dot(ca, wa_ref[...].astype(BF16)) + ba_ref[...])[:8]
        mod_copies = small_gather(mod_ref.at[me_i], mod_ref, 1)

        for j, chip in enumerate(chips):
            for a in range(n):
                copy(a, 1 + j, (*chip, c), me).wait_recv()
                cp = copy(a, 4 + j, (*chip, c), sibling)
                cp.start()
                sent.append(cp)
        for a in range(n):
            copy(a, 0, sibling, me).wait_recv()
            for j, chip in enumerate(chips):
                copy(a, 4 + j, (*chip, 1 - c), me).wait_recv()
        for cp in sent:
            cp.wait_send()
        for cp in local + mod_copies:
            cp.wait()

    vm, hbm = pl.BlockSpec(memory_space=pltpu.VMEM), pl.BlockSpec(memory_space=pl.ANY)
    res = pl.pallas_call(
        body, name="prologue",
        out_shape=[jax.ShapeDtypeStruct((NDEV, 8, ncol), F32), jax.ShapeDtypeStruct((16, D), BF16)]
        + [jax.ShapeDtypeStruct((NDEV,) + a.shape, a.dtype) for a in arrs],
        in_specs=[vm, vm, vm] + [hbm] * n, out_specs=[vm, vm] + [hbm] * n,
        scratch_shapes=[pltpu.VMEM((NDEV, 8, D), F32), pltpu.SemaphoreType.DMA((n, NDEV - 1)), pltpu.SemaphoreType.DMA((n, NDEV - 1)),
                        pltpu.SemaphoreType.DMA((n,)), pltpu.SemaphoreType.DMA((2, NDEV - 1)), pltpu.SemaphoreType.DMA((2, NDEV - 1))],
    )(c8, w_ada_s, b_ada_s, *arrs)
    return res[0], res[1], res[2:]


def _in_set(idx, dests):
    p = idx == dests[0]
    for d in dests[1:]:
        p = jnp.logical_or(p, idx == d)
    return p


def _host_call(body, xs, *, name, grid, in_specs, out_specs, out_shape, scratch_shapes, operands, compiler_params):
    nx, nin, nout, nscr = len(xs), len(operands), len(out_shape), len(scratch_shapes)
    if nx == 0:
        res = pl.pallas_call(body, name=name, grid=grid, in_specs=in_specs, out_specs=out_specs, out_shape=out_shape,
                             scratch_shapes=scratch_shapes, compiler_params=compiler_params)(*operands)
        return res, []
    ops, specs, aliases = list(operands), list(in_specs), {}
    oshape, ospecs = list(out_shape), list(out_specs)
    any_spec = pl.BlockSpec(memory_space=pl.ANY)
    for a, (send, dests, recv) in enumerate(xs):
        ops.append(send)
        specs.append(any_spec)
        if recv is not None:
            aliases[len(ops)] = nout + a
            ops.append(recv)
            specs.append(any_spec)
            oshape.append(jax.ShapeDtypeStruct(recv.shape, recv.dtype))
        else:
            oshape.append(jax.ShapeDtypeStruct((NDEV,) + send.shape[1:], send.dtype))
        ospecs.append(any_spec)
    ntot_in = len(ops)

    def wrapped(*refs):
        host_in = refs[:nin]
        sends, pos = [], nin
        for (_, _, recv) in xs:
            sends.append(refs[pos])
            pos += 1 if recv is None else 2
        host_out = refs[ntot_in:ntot_in + nout]
        recvs = refs[ntot_in + nout:ntot_in + nout + nx]
        host_scr = refs[ntot_in + nout + nx:ntot_in + nout + nx + nscr]
        send_sems, recv_sems, local_sems = refs[ntot_in + nout + nx + nscr:]
        first = pl.program_id(0) == 0
        last = pl.program_id(0) == grid[0] - 1
        for ax in range(1, len(grid)):
            first = jnp.logical_and(first, pl.program_id(ax) == 0)
            last = jnp.logical_and(last, pl.program_id(ax) == grid[ax] - 1)
        x, y, c = _my_pos()
        me = 4 * x + 2 * y + c

        def each(fn_remote, fn_local):
            for a, (_, dests, _) in enumerate(xs):
                lo, nd = dests[0], len(dests)
                for k in range(1, NDEV):
                    px, py, pc = _peer(k, x, y, c)
                    pidx = 4 * px + 2 * py + pc
                    cp = pltpu.make_async_remote_copy(src_ref=sends[a].at[jnp.clip(pidx - lo, 0, nd - 1)], dst_ref=recvs[a].at[me],
                                                      send_sem=send_sems.at[a, k - 1], recv_sem=recv_sems.at[a, k - 1],
                                                      device_id=(px, py, pc), device_id_type=MESH)
                    fn_remote(cp, _in_set(pidx, dests), _in_set(me, dests))
                lc = pltpu.make_async_copy(sends[a].at[jnp.clip(me - lo, 0, nd - 1)], recvs[a].at[me], local_sems.at[a])
                fn_local(lc, _in_set(me, dests))

        def start_remote(cp, to_dest, _):
            pl.when(jnp.logical_and(first, to_dest))(cp.start)

        def start_local(lc, i_am_dest):
            pl.when(jnp.logical_and(first, i_am_dest))(lc.start)

        def wait_remote(cp, to_dest, i_am_dest):
            pl.when(jnp.logical_and(last, to_dest))(cp.wait_send)
            pl.when(jnp.logical_and(last, i_am_dest))(cp.wait_recv)

        def wait_local(lc, i_am_dest):
            pl.when(jnp.logical_and(last, i_am_dest))(lc.wait)

        each(start_remote, start_local)
        body(*host_in, *host_out, *host_scr)
        each(wait_remote, wait_local)

    res = pl.pallas_call(
        wrapped, name=name, grid=grid, in_specs=specs, out_specs=ospecs, out_shape=oshape, input_output_aliases=aliases,
        scratch_shapes=list(scratch_shapes) + [pltpu.SemaphoreType.DMA((nx, NDEV - 1)), pltpu.SemaphoreType.DMA((nx, NDEV - 1)),
                                               pltpu.SemaphoreType.DMA((nx,))],
        compiler_params=compiler_params,
    )(*ops)
    return res[:nout], res[nout:]


def _mm_tn(a, b, name):
    S, M = a.shape
    N = b.shape[1]
    tk = min(2048, S)
    tn = N if N <= 768 else (640 if N % 640 == 0 else 512)
    nk = S // tk

    def body(a_ref, b_ref, o_ref):
        @pl.when(pl.program_id(1) == 0)
        def _():
            o_ref[...] = jnp.zeros_like(o_ref)
        o_ref[...] += _dot(a_ref[...], b_ref[...], TN)

    return pl.pallas_call(
        body, name=name, out_shape=jax.ShapeDtypeStruct((M, N), F32), grid=(N // tn, nk),
        in_specs=[pl.BlockSpec((tk, M), lambda j, k: (k, 0)), pl.BlockSpec((tk, tn), lambda j, k: (k, j))],
        out_specs=pl.BlockSpec((M, tn), lambda j, k: (0, j)),
        compiler_params=_cp(("parallel", "arbitrary"), VMEM_BIG),
    )(a, b)


def _fwd_in(x, mod, g_pre, w_p):
    S = x.shape[0]
    tm, tn = min(512, S), P_W // 2

    def body(x_ref, mod_ref, g_ref, w_ref, p_ref, h_ref):
        @pl.when(pl.program_id(1) == 0)
        def _():
            xv = x_ref[...]
            r = lax.rsqrt(jnp.mean(xv * xv, axis=-1, keepdims=True) + EPS)
            h = ((xv * r) * g_ref[...]) * (1.0 + mod_ref[1:2, :]) + mod_ref[0:1, :]
            h_ref[...] = h.astype(BF16)
        p_ref[...] = _dot(h_ref[...], w_ref[...])

    return pl.pallas_call(
        body, name="fwd_in", out_shape=[jax.ShapeDtypeStruct((S, P_W), F32), jax.ShapeDtypeStruct((S, D), BF16)],
        grid=(S // tm, P_W // tn),
        in_specs=[pl.BlockSpec((tm, D), lambda i, j: (i, 0)), _full((3, D)), _full((1, D)), pl.BlockSpec((D, tn), lambda i, j: (0, j))],
        out_specs=[pl.BlockSpec((tm, tn), lambda i, j: (i, j)), pl.BlockSpec((tm, D), lambda i, j: (i, 0))],
        compiler_params=_cp(("parallel", "arbitrary"), VMEM_BIG),
    )(x, mod, g_pre, w_p)


def _swap16(v):
    lane = lax.broadcasted_iota(jnp.int32, v.shape, 1)
    return jnp.where((lane % 32) < 16, pltpu.roll(v, 112, 1), pltpu.roll(v, 16, 1))


def _rope(v, cos, sin):
    return v * cos + _swap16(v) * sin


def _rope_t(v, cos, sin):
    return v * cos - _swap16(v) * sin


def _head_mean(v):
    lo = lax.broadcasted_iota(jnp.int32, v.shape, 1) < 64
    m0 = jnp.sum(jnp.where(lo, v, 0.0), axis=-1, keepdims=True)
    m1 = jnp.sum(jnp.where(lo, 0.0, v), axis=-1, keepdims=True)
    return jnp.where(lo, m0, m1) * (1.0 / 64.0)


def _prep(p, cos, sin, qg, kg):
    S = p.shape[0]
    tm = min(512, S)

    def body(qa_ref, kv_ref, qr_ref, kr_ref, cos_ref, sin_ref, qg_ref, kg_ref, qt_ref, kh_ref, kt_ref, vh_ref, vta_ref, qr2_ref, kr2_ref):
        cos_v, sin_v = cos_ref[...], sin_ref[...]
        for g in range(4):
            xv = qa_ref[:, 128 * g:128 * g + 128]
            r = lax.rsqrt(_head_mean(xv * xv) + EPS)
            yt = (_rope((xv * r) * qg_ref[...], cos_v, sin_v) * (0.125 * LOG2E)).T
            qt_ref[2 * g] = yt[:DH].astype(BF16)
            qt_ref[2 * g + 1] = yt[DH:].astype(BF16)
        xv = kv_ref[:, :128]
        r = lax.rsqrt(_head_mean(xv * xv) + EPS)
        yv = _rope((xv * r) * kg_ref[...], cos_v, sin_v)
        kh_ref[0] = yv[:, :64].astype(BF16)
        kh_ref[1] = yv[:, 64:].astype(BF16)
        yt = yv.T
        kt_ref[0] = yt[:DH].astype(BF16)
        kt_ref[1] = yt[DH:].astype(BF16)
        vv = kv_ref[:, 128:]
        vh_ref[0] = vv[:, :64].astype(BF16)
        vh_ref[1] = vv[:, 64:].astype(BF16)
        vt = vv.T
        tail = (lax.broadcasted_iota(jnp.int32, (DHA - DH, tm), 0) == 0).astype(BF16)
        for kvh in range(2):
            vta_ref[kvh, 0:DH, :] = vt[DH * kvh:DH * kvh + DH].astype(BF16)
            vta_ref[kvh, DH:DHA, :] = tail
        for g in range(2):
            sl = slice(128 * g, 128 * g + 128)
            qr2_ref[:, sl] = _rope(qr_ref[:, sl], cos_v, sin_v)
            kr2_ref[:, sl] = _rope(kr_ref[:, sl], cos_v, sin_v) * 0.125

    hm = lambda n: pl.BlockSpec((n, tm, DH), lambda i: (0, i, 0))
    ht = lambda n, r: pl.BlockSpec((n, r, tm), lambda i: (0, 0, i))
    return pl.pallas_call(
        body, name="prep",
        out_shape=[jax.ShapeDtypeStruct((8, DH, S), BF16), jax.ShapeDtypeStruct((2, S, DH), BF16), jax.ShapeDtypeStruct((2, DH, S), BF16),
                   jax.ShapeDtypeStruct((2, S, DH), BF16), jax.ShapeDtypeStruct((2, DHA, S), BF16),
                   jax.ShapeDtypeStruct((S, 256), F32), jax.ShapeDtypeStruct((S, 256), F32)],
        grid=(S // tm,),
        in_specs=[pl.BlockSpec((tm, 512), lambda i: (i, O_QA // 512)), pl.BlockSpec((tm, 256), lambda i: (i, O_KA // 256)),
                  pl.BlockSpec((tm, 256), lambda i: (i, O_QR // 256)), pl.BlockSpec((tm, 256), lambda i: (i, O_KR // 256)),
                  pl.BlockSpec((tm, 128), lambda i: (i, 0)), pl.BlockSpec((tm, 128), lambda i: (i, 0)), _full((1, 128)), _full((1, 128))],
        out_specs=[ht(8, DH), hm(2), ht(2, DH), hm(2), ht(2, DHA), pl.BlockSpec((tm, 256), lambda i: (i, 0)), pl.BlockSpec((tm, 256), lambda i: (i, 0))],
        compiler_params=_cp(("parallel",)),
    )(p, p, p, p, cos, sin, qg, kg)


def _attn_fwd(qt, kh, vta):
    S = qt.shape[2]
    tq, tk = min(512, S), min(512, S)
    nj = S // tk

    def body(q_ref, k_ref, v_ref, o_ref, ot_ref, lse_ref, m_s, acc_s):
        j = pl.program_id(1)

        @pl.when(j == 0)
        def _():
            m_s[...] = jnp.full_like(m_s, -jnp.inf)
            acc_s[...] = jnp.zeros_like(acc_s)

        m_all = m_s[...]
        st = {0: _dot(k_ref[0], q_ref[0])}
        m_new, acc_new = [], []
        for h in range(8):
            if h + 1 < 8:
                st[h + 1] = _dot(k_ref[(h + 1) // 4], q_ref[h + 1])
            m_old = m_all[h:h + 1, :]
            mn = jnp.maximum(m_old, jnp.max(st[h], axis=0, keepdims=True))
            pt = jnp.exp2(st[h] - mn).astype(BF16)
            acc_new.append(jnp.exp2(m_old - mn) * acc_s[h] + _dot(v_ref[h // 4], pt))
            m_new.append(mn)
            del st[h]
        for h in range(8):
            acc_s[h] = acc_new[h]
            m_s[h:h + 1, :] = m_new[h]

        @pl.when(j == nj - 1)
        def _():
            for h in range(8):
                ot = acc_s[h, 0:DH, :] / acc_s[h, DH:DH + 1, :]
                ot_ref[h] = ot
                o_ref[:, DH * h:DH * h + DH] = ot.T
                lse_ref[h // 4, h % 4:h % 4 + 1, :] = m_s[h:h + 1, :] + jnp.log2(acc_s[h, DH:DH + 1, :])

    return pl.pallas_call(
        body, name="attn_fwd",
        out_shape=[jax.ShapeDtypeStruct((S, 512), F32), jax.ShapeDtypeStruct((8, DH, S), F32), jax.ShapeDtypeStruct((2, 4, S), F32)],
        grid=(S // tq, nj),
        in_specs=[pl.BlockSpec((8, DH, tq), lambda i, j: (0, 0, i)), pl.BlockSpec((2, tk, DH), lambda i, j: (0, j, 0)),
                  pl.BlockSpec((2, DHA, tk), lambda i, j: (0, 0, j))],
        out_specs=[pl.BlockSpec((tq, 512), lambda i, j: (i, 0)), pl.BlockSpec((8, DH, tq), lambda i, j: (0, 0, i)),
                   pl.BlockSpec((2, 4, tq), lambda i, j: (0, 0, i))],
        scratch_shapes=[pltpu.VMEM((8, tq), F32), pltpu.VMEM((8, DHA, tq), F32)],
        compiler_params=_cp(("parallel", "arbitrary"), VMEM_BIG),
    )(qt, kh, vta)


def _ret_tables(wf, wb):
    C = CH

    def body(wf_ref, wb_ref, dc_ref, qdf_ref, qdb_ref, kdf_ref, kdb_ref, a_ref):
        def logsig(w):
            z = jnp.exp(-jnp.abs(w))
            u = 1.0 + z
            l1p = jnp.where(u == 1.0, z, jnp.log(u) * (z / jnp.where(u == 1.0, 1.0, u - 1.0)))
            return jnp.minimum(w, 0.0) - l1p

        lgf, lgb = logsig(wf_ref[...]), logsig(wb_ref[...])
        lane4 = lax.broadcasted_iota(jnp.int32, (1, 4), 1)

        def pick(lg, h):
            return jnp.sum(jnp.where(lane4 == h, lg, 0.0), axis=-1, keepdims=True)

        ii = lax.broadcasted_iota(jnp.int32, (C, C), 0).astype(F32)
        jj = lax.broadcasted_iota(jnp.int32, (C, C), 1).astype(F32)
        dif = ii - jj
        hd = lax.broadcasted_iota(jnp.int32, (C, 256), 1) // DH
        lf_l = jnp.zeros((C, 256), F32)
        lb_l = jnp.zeros((C, 256), F32)
        for h in range(HR):
            lf, lb = pick(lgf, h), pick(lgb, h)
            dc_ref[h] = jnp.where(dif >= 0, jnp.exp(lf * jnp.maximum(dif, 0.0)), jnp.exp(lb * jnp.maximum(-dif, 0.0)))
            lf_l = jnp.where(hd == h, lf, lf_l)
            lb_l = jnp.where(hd == h, lb, lb_l)
            a_ref[h:h + 1, :] = jnp.broadcast_to(jnp.exp(lf * C), (1, 128))
            a_ref[HR + h:HR + h + 1, :] = jnp.broadcast_to(jnp.exp(lb * C), (1, 128))
        ri = lax.broadcasted_iota(jnp.int32, (C, 256), 0).astype(F32)
        qdf_ref[...] = jnp.exp(lf_l * (ri + 1.0))
        qdb_ref[...] = jnp.exp(lb_l * (C - ri))
        kdf_ref[...] = jnp.exp(lf_l * (C - 1.0 - ri))
        kdb_ref[...] = jnp.exp(lb_l * ri)

    t = jax.ShapeDtypeStruct((C, 256), F32)
    return pl.pallas_call(body, name="ret_tables",
                          out_shape=[jax.ShapeDtypeStruct((HR, C, C), F32), t, t, t, t, jax.ShapeDtypeStruct((8, 128), F32)])(wf, wb)


def _ret_states(kr2, p, kdf, kdb, adec):
    S = kr2.shape[0]
    C, N = CH, S // CH

    def body(kf_ref, vf_ref, kb_ref, vb_ref, kdf_ref, kdb_ref, a_ref, rf_ref, rb_ref, sf, sb):
        @pl.when(pl.program_id(0) == 0)
        def _():
            sf[...] = jnp.zeros_like(sf)
            sb[...] = jnp.zeros_like(sb)

        rf_ref[0] = sf[...]
        rb_ref[0] = sb[...]
        kdfw = (kf_ref[...] * kdf_ref[...]).astype(BF16)
        kdbw = (kb_ref[...] * kdb_ref[...]).astype(BF16)
        vf, vb = vf_ref[...].astype(BF16), vb_ref[...].astype(BF16)
        kvf = [_dot(kdfw[:, _ks(h)], vf[:, _vs(h)], TN) for h in range(HR)]
        kvb = [_dot(kdbw[:, _ks(h)], vb[:, _vs(h)], TN) for h in range(HR)]
        for h in range(HR):
            sf[h] = a_ref[h:h + 1, :] * sf[h] + kvf[h]
            sb[h] = a_ref[HR + h:HR + h + 1, :] * sb[h] + kvb[h]

    st = jax.ShapeDtypeStruct((N, HR, DH, DV), F32)
    return pl.pallas_call(
        body, name="ret_states", out_shape=[st, st], grid=(N,),
        in_specs=[pl.BlockSpec((C, 256), lambda t: (t, 0)), pl.BlockSpec((C, 512), lambda t: (t, O_VR // 512)),
                  pl.BlockSpec((C, 256), lambda t: (N - 1 - t, 0)), pl.BlockSpec((C, 512), lambda t: (N - 1 - t, O_VR // 512)),
                  _full((C, 256)), _full((C, 256)), _full((8, 128))],
        out_specs=[pl.BlockSpec((1, HR, DH, DV), lambda t: (t, 0, 0, 0)), pl.BlockSpec((1, HR, DH, DV), lambda t: (N - 1 - t, 0, 0, 0))],
        scratch_shapes=[pltpu.VMEM((HR, DH, DV), F32), pltpu.VMEM((HR, DH, DV), F32)],
        compiler_params=_cp(("arbitrary",)),
    )(kr2, p, kr2, p, kdf, kdb, adec)


def _ks(h):
    return slice(DH * h, DH * h + DH)


def _vs(h):
    return slice(DV * h, DV * h + DV)


def _ret_heads_fwd(qb, kb, vb, qfw, qbw, dc_ref, rf_ref, rb_ref):
    hs = range(HR)
    s = [_dot(qb[:, _ks(h)], kb[:, _ks(h)], NT) for h in hs]
    inter = [_dot(qfw[:, _ks(h)], rf_ref[0, h].astype(BF16)) + _dot(qbw[:, _ks(h)], rb_ref[0, h].astype(BF16)) for h in hs]
    sd = [s[h] * dc_ref[h] for h in hs]
    o = [_dot(sd[h].astype(BF16), vb[:, _vs(h)]) + inter[h] for h in hs]
    return sd, o


def _ret_out(qr2, kr2, p, rf, rb, dc, qdf, qdb, gn):
    S = qr2.shape[0]
    C, N = CH, S // CH

    def body(q_ref, k_ref, v_ref, z_ref, rf_ref, rb_ref, dc_ref, qdf_ref, qdb_ref, gn_ref, yr_ref):
        qv = q_ref[...]
        qb, kb, vb = qv.astype(BF16), k_ref[...].astype(BF16), v_ref[...].astype(BF16)
        qfw, qbw = (qv * qdf_ref[...]).astype(BF16), (qv * qdb_ref[...]).astype(BF16)
        _, o = _ret_heads_fwd(qb, kb, vb, qfw, qbw, dc_ref, rf_ref, rb_ref)
        for h in range(HR):
            vs = _vs(h)
            mu = jnp.mean(o[h], axis=-1, keepdims=True)
            var = jnp.mean(jnp.square(o[h] - mu), axis=-1, keepdims=True)
            on = (o[h] - mu) * lax.rsqrt(var + EPS)
            z = z_ref[:, vs]
            yr_ref[:, vs] = ((on * gn_ref[:, vs]) * (z * _sigmoid(z))).astype(BF16)

    return pl.pallas_call(
        body, name="ret_out", out_shape=jax.ShapeDtypeStruct((S, 512), BF16), grid=(N,),
        in_specs=[pl.BlockSpec((C, 256), lambda t: (t, 0)), pl.BlockSpec((C, 256), lambda t: (t, 0)),
                  pl.BlockSpec((C, 512), lambda t: (t, O_VR // 512)), pl.BlockSpec((C, 512), lambda t: (t, O_ZR // 512)),
                  pl.BlockSpec((1, HR, DH, DV), lambda t: (t, 0, 0, 0)), pl.BlockSpec((1, HR, DH, DV), lambda t: (t, 0, 0, 0)),
                  _full((HR, C, C)), _full((C, 256)), _full((C, 256)), _full((1, 512))],
        out_specs=pl.BlockSpec((C, 512), lambda t: (t, 0)),
        compiler_params=_cp(("parallel",)),
    )(qr2, kr2, p, p, rf, rb, dc, qdf, qdb, gn)


def _mid(x, tgt, mod, g_post, o_att, p, yr, w_pa, w_pr, w_out):
    S = x.shape[0]
    tm = min(256, S)

    def body(x_ref, t_ref, mod_ref, gp_ref, o_ref, za_ref, gl_ref, yr_ref, wpa_ref, wpr_ref, wout_ref,
             dout_ref, do_ref, dpm_ref, dyr_ref, mb_ref, dub_ref, yab_ref, dab_ref, drb_ref, sums_ref):
        @pl.when(pl.program_id(0) == 0)
        def _():
            sums_ref[...] = jnp.zeros_like(sums_ref)

        za = za_ref[...]
        sa = _sigmoid(za)
        sil = za * sa
        ov = o_ref[...]
        ya_b = (ov * sil).astype(BF16)
        yr_b = yr_ref[...]
        av = _dot(ya_b, wpa_ref[...])
        rv = _dot(yr_b, wpr_ref[...])
        ga = _sigmoid(gl_ref[:, :D])
        gr = _sigmoid(gl_ref[:, D:])
        mb = (ga * av + gr * rv).astype(BF16)
        u = _dot(mb, wout_ref[...])
        r2 = lax.rsqrt(jnp.mean(u * u, axis=-1, keepdims=True) + EPS)
        un = u * r2
        gp = gp_ref[...]
        yv = un * gp
        gate = mod_ref[2:3, :]
        err = (x_ref[...] + gate * yv) - t_ref[...]
        dout = err * (1.0 / D)
        dout_ref[...] = dout
        dy = dout * gate
        sums_ref[0:1, :] += jnp.sum(dout * yv, axis=0, keepdims=True)
        sums_ref[1:2, :] += jnp.sum(dy * un, axis=0, keepdims=True)
        sums_ref[2:3, :] += jnp.sum(err * err, axis=0, keepdims=True)
        dyg = dy * gp
        du_b = (r2 * (dyg - un * jnp.mean(dyg * un, axis=-1, keepdims=True))).astype(BF16)
        dm = _dot(du_b, wout_ref[...], NT)
        da_b = (dm * ga).astype(BF16)
        dr_b = (dm * gr).astype(BF16)
        dpm_ref[:, :D] = (dm * av * (ga * (1.0 - ga))).astype(BF16)
        dpm_ref[:, D:2 * D] = (dm * rv * (gr * (1.0 - gr))).astype(BF16)
        dya = _dot(da_b, wpa_ref[...], NT)
        dyr_ref[...] = _dot(dr_b, wpr_ref[...], NT)
        dov = dya * sil
        for g in range(4):
            dt = dov[:, 128 * g:128 * g + 128].T
            do_ref[2 * g] = dt[:DH].astype(BF16)
            do_ref[2 * g + 1] = dt[DH:].astype(BF16)
        dpm_ref[:, 2 * D:] = (dya * ov * (sa * (1.0 + za * (1.0 - sa)))).astype(BF16)
        mb_ref[...] = mb
        dub_ref[...] = du_b
        yab_ref[...] = ya_b
        dab_ref[...] = da_b
        drb_ref[...] = dr_b

    row = lambda w: pl.BlockSpec((tm, w), lambda i: (i, 0))
    sd = lambda w, dt: jax.ShapeDtypeStruct((S, w), dt)
    return pl.pallas_call(
        body, name="mid",
        out_shape=[sd(D, F32), jax.ShapeDtypeStruct((8, DH, S), BF16), sd(2560, BF16), sd(512, F32), sd(D, BF16), sd(D, BF16), sd(512, BF16),
                   sd(D, BF16), sd(D, BF16), jax.ShapeDtypeStruct((8, D), F32)],
        grid=(S // tm,),
        in_specs=[row(D), row(D), _full((3, D)), _full((1, D)), row(512), pl.BlockSpec((tm, 512), lambda i: (i, O_ZA // 512)),
                  pl.BlockSpec((tm, 2048), lambda i: (i, 0)), row(512), _full((512, D)), _full((512, D)), _full((D, D))],
        out_specs=[row(D), pl.BlockSpec((8, DH, tm), lambda i: (0, 0, i)), row(2560), row(512), row(D), row(D), row(512), row(D), row(D),
                   _full((8, D))],
        compiler_params=_cp(("arbitrary",), VMEM_BIG),
    )(x, tgt, mod, g_post, o_att, p, p, yr, w_pa, w_pr, w_out)


def _attn_bwd(qt, kh, kt, vh, dot_, ot, lse, xs):
    S = qt.shape[2]
    tq, tk = min(512, S), min(512, S)

    def body(q_ref, k_ref, kt_ref, v_ref, do_ref, o_ref, lse_ref, dq_ref, dk_ref, dv_ref):
        j, i = pl.program_id(0), pl.program_id(1)
        cols = pl.ds(pl.multiple_of(i * tq, tq), tq)
        st = {0: _dot(k_ref[0], q_ref[0])}
        dpt = {0: _dot(v_ref[0], do_ref[0])}
        dk_acc, dv_acc, dqs = [None, None], [None, None], []
        for h in range(8):
            g = h // 4
            if h + 1 < 8:
                st[h + 1] = _dot(k_ref[(h + 1) // 4], q_ref[h + 1])
                dpt[h + 1] = _dot(v_ref[(h + 1) // 4], do_ref[h + 1])
            qt_h, dot_h = q_ref[h], do_ref[h]
            delta = jnp.sum(dot_h.astype(F32) * o_ref[h], axis=0, keepdims=True)
            pt = jnp.exp2(st[h] - lse_ref[g, h % 4:h % 4 + 1, :])
            dst = (pt * (dpt[h] - delta)).astype(BF16)
            dv_h = _dot(dot_h, pt.astype(BF16), NT)
            dk_h = _dot(qt_h, dst, NT)
            dqs.append(_dot(kt_ref[g], dst))
            dv_acc[g] = dv_h if dv_acc[g] is None else dv_acc[g] + dv_h
            dk_acc[g] = dk_h if dk_acc[g] is None else dk_acc[g] + dk_h
            del st[h], dpt[h]

        @pl.when(i == 0)
        def _():
            for g in range(2):
                dk_ref[g] = dk_acc[g]
                dv_ref[g] = dv_acc[g]

        @pl.when(i > 0)
        def _():
            for g in range(2):
                dk_ref[g] += dk_acc[g]
                dv_ref[g] += dv_acc[g]

        @pl.when(j == 0)
        def _():
            for h in range(8):
                dq_ref[h, :, cols] = dqs[h]

        @pl.when(j > 0)
        def _():
            for h in range(8):
                dq_ref[h, :, cols] += dqs[h]

    return _host_call(
        body, xs, name="attn_bwd",
        out_shape=[jax.ShapeDtypeStruct((8, DH, S), F32), jax.ShapeDtypeStruct((2, DH, S), F32), jax.ShapeDtypeStruct((2, DH, S), F32)],
        grid=(S // tk, S // tq),
        in_specs=[pl.BlockSpec((8, DH, tq), lambda j, i: (0, 0, i)), pl.BlockSpec((2, tk, DH), lambda j, i: (0, j, 0)),
                  pl.BlockSpec((2, DH, tk), lambda j, i: (0, 0, j)), pl.BlockSpec((2, tk, DH), lambda j, i: (0, j, 0)),
                  pl.BlockSpec((8, DH, tq), lambda j, i: (0, 0, i)), pl.BlockSpec((8, DH, tq), lambda j, i: (0, 0, i)),
                  pl.BlockSpec((2, 4, tq), lambda j, i: (0, 0, i))],
        out_specs=[pl.BlockSpec((8, DH, S), lambda j, i: (0, 0, 0)), pl.BlockSpec((2, DH, tk), lambda j, i: (0, 0, j)),
                   pl.BlockSpec((2, DH, tk), lambda j, i: (0, 0, j))],
        scratch_shapes=[], operands=(qt, kh, kt, vh, dot_, ot, lse),
        compiler_params=_cp(("arbitrary", "arbitrary"), VMEM_BIG),
    )


def _attn_prep_bwd(dqt, dkt, dvt, p, cos, sin, qg, kg):
    S = dqt.shape[2]
    tm = min(512, S)

    def body(dq_ref, dk_ref, dv_ref, qa_ref, ka_ref, cos_ref, sin_ref, qg_ref, kg_ref, dp_ref, gs_ref):
        @pl.when(pl.program_id(0) == 0)
        def _():
            gs_ref[...] = jnp.zeros_like(gs_ref)

        cos_v, sin_v = cos_ref[...], sin_ref[...]

        def pair(ref, a):
            return jnp.concatenate([ref[a], ref[a + 1]], axis=0).T

        def norm_bwd(dyv, xv, gv, row):
            r = lax.rsqrt(_head_mean(xv * xv) + EPS)
            xn = xv * r
            dxh = _rope_t(dyv, cos_v, sin_v)
            gs_ref[row:row + 1, :] += jnp.sum(dxh * xn, axis=0, keepdims=True)
            dg = dxh * gv
            return r * (dg - xn * _head_mean(dg * xn))

        for g in range(4):
            sl = slice(128 * g, 128 * g + 128)
            dp_ref[:, sl] = norm_bwd(pair(dq_ref, 2 * g) * 0.125, qa_ref[:, sl], qg_ref[...], 0).astype(BF16)
        dp_ref[:, 512:640] = norm_bwd(pair(dk_ref, 0) * LN2, ka_ref[...], kg_ref[...], 1).astype(BF16)
        dp_ref[:, 640:768] = pair(dv_ref, 0).astype(BF16)

    ht = lambda n: pl.BlockSpec((n, DH, tm), lambda i: (0, 0, i))
    return pl.pallas_call(
        body, name="attn_prep_bwd", out_shape=[jax.ShapeDtypeStruct((S, 768), BF16), jax.ShapeDtypeStruct((8, 128), F32)],
        grid=(S // tm,),
        in_specs=[ht(8), ht(2), ht(2),
                  pl.BlockSpec((tm, 512), lambda i: (i, O_QA // 512)), pl.BlockSpec((tm, 128), lambda i: (i, O_KA // 128)),
                  pl.BlockSpec((tm, 128), lambda i: (i, 0)), pl.BlockSpec((tm, 128), lambda i: (i, 0)), _full((1, 128)), _full((1, 128))],
        out_specs=[pl.BlockSpec((tm, 768), lambda i: (i, 0)), _full((8, 128))],
        compiler_params=_cp(("arbitrary",)),
    )(dqt, dkt, dvt, p, p, cos, sin, qg, kg)


def _ret_bwd_chunk(qr2, kr2, p, rf, rb, dc, qdf, qdb, gn, dyr, cos, sin, xs):
    S = qr2.shape[0]
    C, N = CH, S // CH

    def body(q_ref, k_ref, v_ref, z_ref, rf_ref, rb_ref, dc_ref, qdf_ref, qdb_ref, gn_ref, dyr_ref, cos_ref, sin_ref,
             dpa_ref, dk_ref, dv_ref, drf_ref, drb_ref, dgn_ref, dlg_ref, dqs):
        @pl.when(pl.program_id(0) == 0)
        def _():
            dgn_ref[...] = jnp.zeros_like(dgn_ref)
            dlg_ref[...] = jnp.zeros_like(dlg_ref)

        qv = q_ref[...]
        qb, kb, vb = qv.astype(BF16), k_ref[...].astype(BF16), v_ref[...].astype(BF16)
        qf32, qb32 = qv * qdf_ref[...], qv * qdb_ref[...]
        qfw, qbw = qf32.astype(BF16), qb32.astype(BF16)
        ii = lax.broadcasted_iota(jnp.int32, (C, C), 0).astype(F32)
        jj = lax.broadcasted_iota(jnp.int32, (C, C), 1).astype(F32)
        dif = ii - jj
        ri = lax.broadcasted_iota(jnp.int32, (C, 1), 0).astype(F32)
        hs = range(HR)
        sd, o = _ret_heads_fwd(qb, kb, vb, qfw, qbw, dc_ref, rf_ref, rb_ref)
        do_b = []
        for h in hs:
            vs = _vs(h)
            mu = jnp.mean(o[h], axis=-1, keepdims=True)
            rstd = lax.rsqrt(jnp.mean(jnp.square(o[h] - mu), axis=-1, keepdims=True) + EPS)
            on = (o[h] - mu) * rstd
            z = z_ref[:, vs]
            sz = _sigmoid(z)
            dy = dyr_ref[:, vs]
            gnv = gn_ref[:, vs]
            dpa_ref[:, 256 + DV * h:256 + DV * h + DV] = (dy * (on * gnv) * (sz * (1.0 + z * (1.0 - sz)))).astype(BF16)
            dys = dy * (z * sz)
            dgn_ref[:, vs] += jnp.sum(dys * on, axis=0, keepdims=True)
            don = dys * gnv
            do = rstd * (don - jnp.mean(don, axis=-1, keepdims=True) - on * jnp.mean(don * on, axis=-1, keepdims=True))
            do_b.append(do.astype(BF16))
        dpm = [_dot(do_b[h], vb[:, _vs(h)], NT) for h in hs]
        dqf = [_dot(do_b[h], rf_ref[0, h].astype(BF16), NT) for h in hs]
        dqb = [_dot(do_b[h], rb_ref[0, h].astype(BF16), NT) for h in hs]
        for h in hs:
            dv_ref[:, _vs(h)] = _dot(sd[h].astype(BF16), do_b[h], TN)
            drf_ref[0, h] = _dot(qfw[:, _ks(h)], do_b[h], TN)
            drb_ref[0, h] = _dot(qbw[:, _ks(h)], do_b[h], TN)
        dsd = [(dpm[h] * dc_ref[h]).astype(BF16) for h in hs]
        for h in hs:
            ks = _ks(h)
            dqs[:, ks] = _dot(dsd[h], kb[:, ks]) + dqf[h] * qdf_ref[:, ks] + dqb[h] * qdb_ref[:, ks]
            dk_ref[:, ks] = _dot(dsd[h], qb[:, ks], TN)
        for h in hs:
            ks = _ks(h)
            e = dpm[h] * sd[h]
            lf = _sum11(e * jnp.maximum(dif, 0.0)) + _sum11(jnp.sum(qf32[:, ks] * dqf[h], axis=-1, keepdims=True) * (ri + 1.0))
            lb = _sum11(e * jnp.maximum(-dif, 0.0)) + _sum11(jnp.sum(qb32[:, ks] * dqb[h], axis=-1, keepdims=True) * (C - ri))
            dlg_ref[h:h + 1, :] += jnp.broadcast_to(lf, (1, 128))
            dlg_ref[HR + h:HR + h + 1, :] += jnp.broadcast_to(lb, (1, 128))
        cos_v, sin_v = cos_ref[...], sin_ref[...]
        for g in range(2):
            sl = slice(128 * g, 128 * g + 128)
            dpa_ref[:, sl] = _rope_t(dqs[:, sl], cos_v, sin_v).astype(BF16)

    st = jax.ShapeDtypeStruct((N, HR, DH, DV), F32)
    stb = lambda: pl.BlockSpec((1, HR, DH, DV), lambda t: (t, 0, 0, 0))
    return _host_call(
        body, xs, name="ret_bwd_chunk",
        out_shape=[jax.ShapeDtypeStruct((S, 768), BF16), jax.ShapeDtypeStruct((S, 256), F32), jax.ShapeDtypeStruct((S, 512), F32), st, st,
                   jax.ShapeDtypeStruct((1, 512), F32), jax.ShapeDtypeStruct((8, 128), F32)],
        grid=(N,),
        in_specs=[pl.BlockSpec((C, 256), lambda t: (t, 0)), pl.BlockSpec((C, 256), lambda t: (t, 0)),
                  pl.BlockSpec((C, 512), lambda t: (t, O_VR // 512)), pl.BlockSpec((C, 512), lambda t: (t, O_ZR // 512)),
                  stb(), stb(), _full((HR, C, C)), _full((C, 256)), _full((C, 256)), _full((1, 512)),
                  pl.BlockSpec((C, 512), lambda t: (t, 0)), pl.BlockSpec((C, 128), lambda t: (t, 0)), pl.BlockSpec((C, 128), lambda t: (t, 0))],
        out_specs=[pl.BlockSpec((C, 768), lambda t: (t, 0)), pl.BlockSpec((C, 256), lambda t: (t, 0)), pl.BlockSpec((C, 512), lambda t: (t, 0)),
                   stb(), stb(), _full((1, 512)), _full((8, 128))],
        scratch_shapes=[pltpu.VMEM((C, 256), F32)], operands=(qr2, kr2, p, p, rf, rb, dc, qdf, qdb, gn, dyr, cos, sin),
        compiler_params=_cp(("arbitrary",)),
    )


def _ret_bwd_scan(kr2, p, rf, rb, drf, drb, kdf, kdb, adec):
    S = kr2.shape[0]
    C, N = CH, S // CH

    def body(kf_ref, vf_ref, kb_ref, vb_ref, rf_ref, rb_ref, drf_ref, drb_ref, kdf_ref, kdb_ref, a_ref,
             dkf_ref, dkb_ref, dvf_ref, dvb_ref, dlg_ref, gf, gb):
        @pl.when(pl.program_id(0) == 0)
        def _():
            gf[...] = jnp.zeros_like(gf)
            gb[...] = jnp.zeros_like(gb)
            dlg_ref[...] = jnp.zeros_like(dlg_ref)

        ri = lax.broadcasted_iota(jnp.int32, (C, 1), 0).astype(F32)

        def one(k_ref, v_ref, r_ref, dr_ref, kd_ref, g_s, dk_ref, dv_ref, row0, wexp):
            kd32 = k_ref[...] * kd_ref[...]
            kdw = kd32.astype(BF16)
            vb = v_ref[...].astype(BF16)
            for h in range(HR):
                ks, vs = _ks(h), _vs(h)
                gst = g_s[h]
                g_b = gst.astype(BF16)
                dkd = _dot(vb[:, vs], g_b, NT)
                dk_ref[:, ks] = dkd * kd_ref[:, ks]
                dv_ref[:, vs] = _dot(kdw[:, ks], g_b)
                av = a_ref[row0 + h:row0 + h + 1, :]
                lg = (_sum11(jnp.sum(kd32[:, ks] * dkd, axis=-1, keepdims=True) * wexp)
                      + C * av[:, 0:1] * _sum11(r_ref[0, h] * gst))
                dlg_ref[row0 + h:row0 + h + 1, :] += jnp.broadcast_to(lg, (1, 128))
                g_s[h] = dr_ref[0, h] + av * gst

        one(kf_ref, vf_ref, rf_ref, drf_ref, kdf_ref, gf, dkf_ref, dvf_ref, 0, C - 1.0 - ri)
        one(kb_ref, vb_ref, rb_ref, drb_ref, kdb_ref, gb, dkb_ref, dvb_ref, HR, ri)

    fwd = lambda w, off=0: pl.BlockSpec((C, w), lambda t: (N - 1 - t, off))
    bwd = lambda w, off=0: pl.BlockSpec((C, w), lambda t: (t, off))
    stf = lambda: pl.BlockSpec((1, HR, DH, DV), lambda t: (N - 1 - t, 0, 0, 0))
    stb = lambda: pl.BlockSpec((1, HR, DH, DV), lambda t: (t, 0, 0, 0))
    return pl.pallas_call(
        body, name="ret_bwd_scan",
        out_shape=[jax.ShapeDtypeStruct((S, 256), F32), jax.ShapeDtypeStruct((S, 256), F32), jax.ShapeDtypeStruct((S, 512), F32),
                   jax.ShapeDtypeStruct((S, 512), F32), jax.ShapeDtypeStruct((8, 128), F32)],
        grid=(N,),
        in_specs=[fwd(256), fwd(512, O_VR // 512), bwd(256), bwd(512, O_VR // 512), stf(), stb(), stf(), stb(),
                  _full((C, 256)), _full((C, 256)), _full((8, 128))],
        out_specs=[fwd(256), bwd(256), fwd(512), bwd(512), _full((8, 128))],
        scratch_shapes=[pltpu.VMEM((HR, DH, DV), F32), pltpu.VMEM((HR, DH, DV), F32)],
        compiler_params=_cp(("arbitrary",)),
    )(kr2, p, kr2, p, rf, rb, drf, drb, kdf, kdb, adec)


def _ret_bwd_final(dk_i, dkf, dkb, dv_i, dvf, dvb, cos, sin):
    S = dk_i.shape[0]
    tm = min(512, S)

    def body(a_ref, b_ref, c_ref, d_ref, e_ref, f_ref, cos_ref, sin_ref, o_ref):
        o_ref[:, :512] = (d_ref[...] + e_ref[...] + f_ref[...]).astype(BF16)
        cos_v, sin_v = cos_ref[...], sin_ref[...]
        for g in range(2):
            sl = slice(128 * g, 128 * g + 128)
            dk = a_ref[:, sl] + b_ref[:, sl] + c_ref[:, sl]
            o_ref[:, 512 + 128 * g:512 + 128 * g + 128] = (_rope_t(dk, cos_v, sin_v) * 0.125).astype(BF16)

    row = lambda w: pl.BlockSpec((tm, w), lambda i: (i, 0))
    return pl.pallas_call(
        body, name="ret_bwd_final", out_shape=jax.ShapeDtypeStruct((S, 768), BF16), grid=(S // tm,),
        in_specs=[row(256), row(256), row(256), row(512), row(512), row(512), row(128), row(128)], out_specs=row(768),
        compiler_params=_cp(("parallel",)),
    )(dk_i, dkf, dkb, dv_i, dvf, dvb, cos, sin)


def _bwd_in(dpm, dpa, dpra, dprb, w_p, x, dout, mod, g_pre, xs):
    S = x.shape[0]
    tm = min(256, S)

    def body(a_ref, b_ref, c_ref, d_ref, w_ref, x_ref, dout_ref, mod_ref, g_ref, gx_ref, sums_ref):
        @pl.when(pl.program_id(0) == 0)
        def _():
            sums_ref[...] = jnp.zeros_like(sums_ref)

        dh = (_dot(a_ref[...], w_ref[:, :O_QA], NT) + _dot(b_ref[...], w_ref[:, O_QA:O_QR], NT)
              + _dot(c_ref[...], w_ref[:, O_QR:O_VR], NT) + _dot(d_ref[...], w_ref[:, O_VR:], NT))
        xv = x_ref[...]
        r = lax.rsqrt(jnp.mean(xv * xv, axis=-1, keepdims=True) + EPS)
        xn = xv * r
        gv = g_ref[...]
        sc1 = 1.0 + mod_ref[1:2, :]
        sums_ref[0:1, :] += jnp.sum(dh, axis=0, keepdims=True)
        sums_ref[1:2, :] += jnp.sum(dh * (xn * gv), axis=0, keepdims=True)
        sums_ref[2:3, :] += jnp.sum(dh * xn, axis=0, keepdims=True) * sc1
        dxn = dh * (gv * sc1)
        gx_ref[...] = dout_ref[...] + r * (dxn - xn * jnp.mean(dxn * xn, axis=-1, keepdims=True))

    row = lambda w: pl.BlockSpec((tm, w), lambda i: (i, 0))
    return _host_call(
        body, xs, name="bwd_in", out_shape=[jax.ShapeDtypeStruct((S, D), F32), jax.ShapeDtypeStruct((8, D), F32)], grid=(S // tm,),
        in_specs=[row(2560), row(768), row(768), row(768), _full((D, P_W)), row(D), row(D), _full((3, D)), _full((1, D))],
        out_specs=[row(D), _full((8, D))], scratch_shapes=[], operands=(dpm, dpa, dpra, dprb, w_p, x, dout, mod, g_pre),
        compiler_params=_cp(("arbitrary",), VMEM_BIG),
    )


SMALL = ("b_ada", "g_pre", "qn_g", "kn_g", "w_dec_f", "w_dec_b", "gn_g", "g_post")


def _small_update(gathered, wmv):
    ns = len(SMALL)

    def body(*refs):
        gin_ref, gmid_ref, ggn_ref, gatt_ref, gl1_ref, gl2_ref = refs[:6]
        wmv_refs = refs[6:6 + 3 * ns]
        loss_ref = refs[6 + 3 * ns]
        out_refs = refs[7 + 3 * ns:]

        def dsum(ref, r=None):
            rows = slice(None) if r is None else slice(r, r + 1)
            acc = ref[0, rows, :]
            for d in range(1, NDEV):
                acc = acc + ref[d, rows, :]
            return acc

        s_lg = dsum(gl1_ref) + dsum(gl2_ref)
        loss_ref[...] = (0.5 / D) * jnp.sum(dsum(gmid_ref, 2), axis=-1, keepdims=True)
        eye = lax.broadcasted_iota(jnp.int32, (8, 128), 0) == lax.broadcasted_iota(jnp.int32, (8, 128), 1)
        dlg = jnp.sum(jnp.where(eye, s_lg, 0.0), axis=0, keepdims=True)
        w_f, w_b = wmv_refs[3 * SMALL.index("w_dec_f")][...], wmv_refs[3 * SMALL.index("w_dec_b")][...]
        s_q, s_k = dsum(gatt_ref, 0), dsum(gatt_ref, 1)
        grads = dict(
            b_ada=jnp.concatenate([dsum(gin_ref, 0), dsum(gin_ref, 1), dsum(gmid_ref, 0)], axis=1),
            g_pre=dsum(gin_ref, 2), g_post=dsum(gmid_ref, 1), gn_g=dsum(ggn_ref),
            qn_g=s_q[:, :DH] + s_q[:, DH:], kn_g=s_k[:, :DH] + s_k[:, DH:],
            w_dec_f=dlg[:, 0:HR] * _sigmoid(-w_f), w_dec_b=dlg[:, HR:2 * HR] * _sigmoid(-w_b))
        for i, nme in enumerate(SMALL):
            g = grads[nme]
            w_ref, m_ref, v_ref = wmv_refs[3 * i:3 * i + 3]
            g_ref, d_ref, nm_ref, nv_ref = out_refs[4 * i:4 * i + 4]
            g_ref[...] = g
            m2 = ADAM_B1 * m_ref[...] + (1.0 - ADAM_B1) * g
            v2 = ADAM_B2 * v_ref[...] + (1.0 - ADAM_B2) * jnp.square(g)
            m_hat = m2 / (1.0 - ADAM_B1 ** ADAM_STEP)
            v_hat = v2 / (1.0 - ADAM_B2 ** ADAM_STEP)
            d_ref[...] = -ADAM_LR * (m_hat / (jnp.sqrt(v_hat) + ADAM_EPS) + ADAM_WD * w_ref[...])
            nm_ref[...] = m2
            nv_ref[...] = v2

    out_shape = [jax.ShapeDtypeStruct((1, 1), F32)]
    for i in range(ns):
        out_shape += [jax.ShapeDtypeStruct(wmv[3 * i].shape, F32)] * 4
    return pl.pallas_call(body, name="small_update", out_shape=out_shape)(*gathered, *wmv)


def _adamw(parts, w, m, v, name):
    n, R, L = parts.shape
    tr = 256 if (R % 256 == 0 and R > 256) else R

    def body(p_ref, w_ref, m_ref, v_ref, g_ref, d_ref, nm_ref, nv_ref):
        g = p_ref[0].astype(F32)
        for k in range(1, n):
            g = g + p_ref[k].astype(F32)
        g_ref[...] = g
        m2 = ADAM_B1 * m_ref[...] + (1.0 - ADAM_B1) * g
        v2 = ADAM_B2 * v_ref[...] + (1.0 - ADAM_B2) * jnp.square(g)
        m_hat = m2 / (1.0 - ADAM_B1 ** ADAM_STEP)
        v_hat = v2 / (1.0 - ADAM_B2 ** ADAM_STEP)
        d_ref[...] = -ADAM_LR * (m_hat / (jnp.sqrt(v_hat) + ADAM_EPS) + ADAM_WD * w_ref[...])
        nm_ref[...] = m2
        nv_ref[...] = v2

    blk = pl.BlockSpec((tr, L), lambda i: (i, 0))
    o = jax.ShapeDtypeStruct((R, L), F32)
    return pl.pallas_call(
        body, name=name, out_shape=[o, o, o, o], grid=(R // tr,),
        in_specs=[pl.BlockSpec((n, tr, L), lambda i: (0, i, 0)), blk, blk, blk], out_specs=[blk, blk, blk, blk],
        compiler_params=_cp(("parallel",), VMEM_BIG),
    )(parts, w, m, v)


def _rope_tables(S):
    f = np.float32
    t = np.arange(S)
    row, col = (t // 64).astype(f), (t % 64).astype(f)
    half = DH // 2
    inv = np.power(f(ROPE_THETA), -np.arange(0, half, 2, dtype=f) / f(half)).astype(f)
    ar, ac = (row[:, None] * inv[None, :]).astype(f), (col[:, None] * inv[None, :]).astype(f)
    cos64 = np.concatenate([np.cos(ar), np.cos(ar), np.cos(ac), np.cos(ac)], axis=1).astype(f)
    sin64 = np.concatenate([-np.sin(ar), np.sin(ar), -np.sin(ac), np.sin(ac)], axis=1).astype(f)
    return jnp.asarray(np.tile(cos64, (1, 2))), jnp.asarray(np.tile(sin64, (1, 2)))


def _to_p_order(w_orig):
    return jnp.concatenate([w_orig[:, ORIG[n][0]:ORIG[n][1]] for n in P_ORDER], axis=1)


def _pad_lanes(v, n):
    return jnp.pad(v, ((0, 0), (0, n - v.shape[1])))


def kernel(x, c, w_ada, b_ada, g_pre, w_in, qn_g, kn_g, w_dec_f, w_dec_b, gn_g, w_pa, w_pr, w_out, g_post, loss_target, m_w_ada, m_b_ada, m_g_pre, m_w_in, m_qn_g, m_kn_g, m_w_dec_f, m_w_dec_b, m_gn_g, m_w_pa, m_w_pr, m_w_out, m_g_post, v_w_ada, v_b_ada, v_g_pre, v_w_in, v_qn_g, v_kn_g, v_w_dec_f, v_w_dec_b, v_gn_g, v_w_pa, v_w_pr, v_w_out, v_g_post):
    S = x.shape[1]
    me = 4 * lax.axis_index("x") + 2 * lax.axis_index("y") + lax.axis_index("c")
    xs, tgt = x[0], loss_target[0]
    ncol_ada = w_ada.shape[2]
    ncol_in = w_in.shape[2]

    b_ada_s = lax.dynamic_slice(b_ada, (0, me * ncol_ada), (1, ncol_ada))
    mod_all, c_act, (wg_in, wg_pa, wg_pr, wg_out) = _prologue(
        jnp.pad(c, ((0, 7), (0, 0))), w_ada[0], b_ada_s,
        [w_in[0].astype(BF16), w_pa[0].astype(BF16), w_pr[0].astype(BF16), w_out[0].astype(BF16)])
    mod = lax.dynamic_index_in_dim(mod_all, me, axis=1, keepdims=False).reshape(3, D)
    w_p = _to_p_order(wg_in.transpose(1, 0, 2).reshape(D, NDEV * ncol_in))
    w_pa_f = wg_pa.transpose(1, 0, 2).reshape(512, D)
    w_pr_f = wg_pr.transpose(1, 0, 2).reshape(512, D)
    w_out_f = wg_out.reshape(D, D)

    cos, sin = _rope_tables(S)
    qg, kg = jnp.tile(qn_g, (1, 2)), jnp.tile(kn_g, (1, 2))

    p, h = _fwd_in(xs, mod, g_pre, w_p)
    qt, kh, kt, vh, vta, qr2, kr2 = _prep(p, cos, sin, qg, kg)
    o_att, o_t, lse = _attn_fwd(qt, kh, vta)
    dc, qdf, qdb, kdf, kdb, adec = _ret_tables(w_dec_f, w_dec_b)
    rf, rb = _ret_states(kr2, p, kdf, kdb, adec)
    yr = _ret_out(qr2, kr2, p, rf, rb, dc, qdf, qdb, gn_g)

    dout, do, dpm, dyr, mb, dub, yab, dab, drb_, sums_mid = _mid(xs, tgt, mod, g_post, o_att, p, yr, w_pa_f, w_pr_f, w_out_f)
    gw_out = _mm_tn(mb, dub, "gw_out")
    gw_pa = _mm_tn(yab, dab, "gw_pa")
    gw_pr = _mm_tn(yr, drb_, "gw_pr")
    gi_m = _mm_tn(h, dpm, "gw_in_mid")

    def shards(cols, nd):
        return cols.astype(BF16).reshape(D, nd, ncol_in).transpose(1, 0, 2)

    all_dev = tuple(range(NDEV))
    (dqt, dkt, dvt), (rs_out, rs_pa, rs_pr, rs_in) = _attn_bwd(qt, kh, kt, vh, do, o_t, lse, [
        (gw_out.astype(BF16).reshape(NDEV, 128, D), all_dev, None),
        (gw_pa.astype(BF16).reshape(512, NDEV, 128).transpose(1, 0, 2), all_dev, None),
        (gw_pr.astype(BF16).reshape(512, NDEV, 128).transpose(1, 0, 2), all_dev, None),
        (shards(gi_m[:, 224:2048], 3), (5, 6, 7), None)])
    dpa, gs_att = _attn_prep_bwd(dqt, dkt, dvt, p, cos, sin, qg, kg)
    gi_a = _mm_tn(h, dpa, "gw_in_att")
    (dpra, dk_i, dv_i, drf, drb, dgn, dlg1), _ = _ret_bwd_chunk(qr2, kr2, p, rf, rb, dc, qdf, qdb, gn_g, dyr, cos, sin, [])
    dkf, dkb, dvf, dvb, dlg2 = _ret_bwd_scan(kr2, p, rf, rb, drf, drb, kdf, kdb, adec)
    dprb = _ret_bwd_final(dk_i, dkf, dkb, dv_i, dvf, dvb, cos, sin)
    gi_ra = _mm_tn(h, dpra, "gw_in_reta")
    gi_rb = _mm_tn(h, dprb, "gw_in_retb")
    (grad_x, sums_in), (rs_in,) = _bwd_in(dpm, dpa, dpra, dprb, w_p, xs, dout, mod, g_pre, [
        (shards(jnp.concatenate([gi_a, gi_m[:, 2048:2560], gi_ra[:, :256], gi_rb[:, 512:768], gi_rb[:, :512], gi_ra[:, 256:768],
                                 gi_m[:, :224]], axis=1), 5), (0, 1, 2, 3, 4), rs_in)])

    gathered = _small_allgather([sums_in, sums_mid, dgn, gs_att, dlg1, dlg2], "ag_small")
    given = dict(b_ada=(b_ada, m_b_ada, v_b_ada), g_pre=(g_pre, m_g_pre, v_g_pre), qn_g=(qn_g, m_qn_g, v_qn_g), kn_g=(kn_g, m_kn_g, v_kn_g),
                 w_dec_f=(w_dec_f, m_w_dec_f, v_w_dec_f), w_dec_b=(w_dec_b, m_w_dec_b, v_w_dec_b), gn_g=(gn_g, m_gn_g, v_gn_g),
                 g_post=(g_post, m_g_post, v_g_post))
    small = _small_update(gathered, [a for nme in SMALL for a in given[nme]])
    loss = small[0][0, 0]

    g_in_all, g_mid_all = gathered[0], gathered[1]
    dmod_all = lax.dynamic_slice(jnp.concatenate([g_in_all[:, 0, :], g_in_all[:, 1, :], g_mid_all[:, 0, :]], axis=1),
                                 (0, me * ncol_ada), (NDEV, ncol_ada))
    g_ada = _mm_tn(c_act, jnp.pad(dmod_all, ((0, 8), (0, 0))).astype(BF16), "gw_ada")

    res = dict(
        w_ada=_adamw(g_ada[None], w_ada[0], m_w_ada[0], v_w_ada[0], "adamw_ada"),
        w_in=_adamw(rs_in, w_in[0], m_w_in[0], v_w_in[0], "adamw_in"),
        w_pa=_adamw(rs_pa, w_pa[0], m_w_pa[0], v_w_pa[0], "adamw_pa"),
        w_pr=_adamw(rs_pr, w_pr[0], m_w_pr[0], v_w_pr[0], "adamw_pr"),
        w_out=_adamw(rs_out, w_out[0], m_w_out[0], v_w_out[0], "adamw_out"),
    )
    names = ["w_ada", "b_ada", "g_pre", "w_in", "qn_g", "kn_g", "w_dec_f", "w_dec_b", "gn_g", "w_pa", "w_pr", "w_out", "g_post"]
    outs = [[], [], [], []]
    for nme in names:
        for q in range(4):
            if nme in res:
                outs[q].append(res[nme][q][None])
            else:
                outs[q].append(small[1 + 4 * SMALL.index(nme) + q])
    return (loss, grad_x[None], *outs[0], *outs[1], *outs[2], *outs[3])
```

```python
import jax
import jax.numpy as jnp
import numpy as np
from jax import lax
from jax.experimental import pallas as pl
from jax.experimental.pallas import tpu as pltpu

F32, BF16 = jnp.float32, jnp.bfloat16
D = 1024
DH = 64
DHA = 80
DV = 128
LOG2E = 1.4426950408889634
LN2 = 0.6931471805599453
HR = 4
CH = 128
EPS = 1e-6
ROPE_THETA = 10000.0
NDEV = 8
O_GL, O_ZA, O_QA, O_KA, O_VA, O_QR, O_ZR, O_VR, O_KR, P_W = 0, 2048, 2560, 3072, 3200, 3328, 3584, 4096, 4608, 4864
ORIG = dict(qa=(0, 512), ka=(512, 640), va=(640, 768), za=(768, 1280), qr=(1280, 1536), kr=(1536, 1792),
            vr=(1792, 2304), zr=(2304, 2816), gl=(2816, 4864))
P_ORDER = ("gl", "za", "qa", "ka", "va", "qr", "zr", "vr", "kr")
ADAM_LR, ADAM_B1, ADAM_B2, ADAM_EPS, ADAM_WD, ADAM_STEP = 0.001, 0.9, 0.999, 1e-08, 0.01, 10
VMEM_BIG = 56 * 1024 * 1024
MESH = pl.DeviceIdType.MESH

NT = (((1,), (1,)), ((), ()))
TN = (((0,), (0,)), ((), ()))


def _dot(a, b, dims=None):
    if dims is None:
        return jnp.dot(a, b, preferred_element_type=F32)
    return lax.dot_general(a, b, dims, preferred_element_type=F32)


def _cp(sem=None, vmem=None):
    kw = {}
    if sem is not None:
        kw["dimension_semantics"] = sem
    if vmem is not None:
        kw["vmem_limit_bytes"] = vmem
    return pltpu.CompilerParams(**kw)


def _sigmoid(z):
    return 1.0 / (1.0 + jnp.exp(-z))


def _sum11(m):
    return jnp.sum(jnp.sum(m, axis=-1, keepdims=True), axis=0, keepdims=True)


def _full(shape):
    n = len(shape)
    return pl.BlockSpec(shape, lambda *_: (0,) * n)


def _my_pos():
    return lax.axis_index("x"), lax.axis_index("y"), lax.axis_index("c")


def _peer(k, x, y, c):
    return ((1 - x) if k & 4 else x, (1 - y) if k & 2 else y, (1 - c) if k & 1 else c)


def _small_allgather(vs, name):
    n = len(vs)

    def body(*refs):
        v_refs, out_refs = refs[:n], refs[n:2 * n]
        send_sems, recv_sems = refs[2 * n:]
        x, y, c = _my_pos()
        me = 4 * x + 2 * y + c
        cps = []
        for a in range(n):
            out_refs[a][me] = v_refs[a][...]
            for k in range(1, NDEV):
                cp = pltpu.make_async_remote_copy(src_ref=v_refs[a], dst_ref=out_refs[a].at[me], send_sem=send_sems.at[a, k - 1],
                                                  recv_sem=recv_sems.at[a, k - 1], device_id=_peer(k, x, y, c), device_id_type=MESH)
                cp.start()
                cps.append(cp)
        for cp in cps:
            cp.wait()

    vm = pl.BlockSpec(memory_space=pltpu.VMEM)
    return pl.pallas_call(
        body, name=name, out_shape=[jax.ShapeDtypeStruct((NDEV,) + v.shape, v.dtype) for v in vs],
        in_specs=[vm] * n, out_specs=[vm] * n,
        scratch_shapes=[pltpu.SemaphoreType.DMA((n, NDEV - 1)), pltpu.SemaphoreType.DMA((n, NDEV - 1))],
    )(*vs)


def _prologue(c8, w_ada_s, b_ada_s, arrs):
    n = len(arrs)
    ncol = w_ada_s.shape[1]

    def body(*refs):
        c_ref, wa_ref, ba_ref = refs[:3]
        ins = refs[3:3 + n]
        mod_ref, cact_ref = refs[3 + n:5 + n]
        outs = refs[5 + n:5 + 2 * n]
        call_ref, send_sems, recv_sems, local_sems, s_send, s_recv = refs[5 + 2 * n:]
        x, y, c = _my_pos()
        me, sibling = (x, y, c), (x, y, 1 - c)
        chips = [(1 - x, y), (x, 1 - y), (1 - x, 1 - y)]
        me_i = 4 * x + 2 * y + c

        def small_gather(src_ref, dst_ref, row):
            cps = []
            for k in range(1, NDEV):
                cp = pltpu.make_async_remote_copy(src_ref=src_ref, dst_ref=dst_ref.at[me_i], send_sem=s_send.at[row, k - 1],
                                                  recv_sem=s_recv.at[row, k - 1], device_id=_peer(k, x, y, c), device_id_type=MESH)
                cp.start()
                cps.append(cp)
            return cps

        def blk(a, px, py, pc):
            return outs[a].at[4 * px + 2 * py + pc]

        def copy(a, k, block, to, src=None):
            return pltpu.make_async_remote_copy(src_ref=blk(a, *block) if src is None else src, dst_ref=blk(a, *block),
                                                send_sem=send_sems.at[a, k], recv_sem=recv_sems.at[a, k], device_id=to, device_id_type=MESH)

        call_ref[me_i] = c_ref[...]
        for cp in small_gather(c_ref, call_ref, 0):
            cp.wait()

        local, sent = [], []
        for a in range(n):
            mine = pltpu.make_async_copy(ins[a], blk(a, *me), local_sems.at[a])
            mine.start()
            local.append(mine)
            first = [copy(a, 0, me, sibling, src=ins[a])] + [copy(a, 1 + j, me, (*chip, c), src=ins[a]) for j, chip in enumerate(chips)]
            for cp in first:
                cp.start()
            sent += first

        cv = call_ref[:, 0, :]
        ca = jnp.concatenate([cv * _sigmoid(cv), jnp.zeros_like(cv)], axis=0).astype(BF16)
        cact_ref[...] = ca
        mod_ref[me_i] = (_dot(ca, wa_ref[...].astype(BF16)) + ba_ref[...])[:8]
        mod_copies = small_gather(mod_ref.at[me_i], mod_ref, 1)

        for j, chip in enumerate(chips):
            for a in range(n):
                copy(a, 1 + j, (*chip, c), me).wait_recv()
                cp = copy(a, 4 + j, (*chip, c), sibling)
                cp.start()
                sent.append(cp)
        for a in range(n):
            copy(a, 0, sibling, me).wait_recv()
            for j, chip in enumerate(chips):
                copy(a, 4 + j, (*chip, 1 - c), me).wait_recv()
        for cp in sent:
            cp.wait_send()
        for cp in local + mod_copies:
            cp.wait()

    vm, hbm = pl.BlockSpec(memory_space=pltpu.VMEM), pl.BlockSpec(memory_space=pl.ANY)
    res = pl.pallas_call(
        body, name="prologue",
        out_shape=[jax.ShapeDtypeStruct((NDEV, 8, ncol), F32), jax.ShapeDtypeStruct((16, D), BF16)]
        + [jax.ShapeDtypeStruct((NDEV,) + a.shape, a.dtype) for a in arrs],
        in_specs=[vm, vm, vm] + [hbm] * n, out_specs=[vm, vm] + [hbm] * n,
        scratch_shapes=[pltpu.VMEM((NDEV, 8, D), F32), pltpu.SemaphoreType.DMA((n, NDEV - 1)), pltpu.SemaphoreType.DMA((n, NDEV - 1)),
                        pltpu.SemaphoreType.DMA((n,)), pltpu.SemaphoreType.DMA((2, NDEV - 1)), pltpu.SemaphoreType.DMA((2, NDEV - 1))],
    )(c8, w_ada_s, b_ada_s, *arrs)
    return res[0], res[1], res[2:]


def _in_set(idx, dests):
    p = idx == dests[0]
    for d in dests[1:]:
        p = jnp.logical_or(p, idx == d)
    return p


def _host_call(body, xs, *, name, grid, in_specs, out_specs, out_shape, scratch_shapes, operands, compiler_params):
    nx, nin, nout, nscr = len(xs), len(operands), len(out_shape), len(scratch_shapes)
    if nx == 0:
        res = pl.pallas_call(body, name=name, grid=grid, in_specs=in_specs, out_specs=out_specs, out_shape=out_shape,
                             scratch_shapes=scratch_shapes, compiler_params=compiler_params)(*operands)
        return res, []
    ops, specs, aliases = list(operands), list(in_specs), {}
    oshape, ospecs = list(out_shape), list(out_specs)
    any_spec = pl.BlockSpec(memory_space=pl.ANY)
    for a, (send, dests, recv) in enumerate(xs):
        ops.append(send)
        specs.append(any_spec)
        if recv is not None:
            aliases[len(ops)] = nout + a
            ops.append(recv)
            specs.append(any_spec)
            oshape.append(jax.ShapeDtypeStruct(recv.shape, recv.dtype))
        else:
            oshape.append(jax.ShapeDtypeStruct((NDEV,) + send.shape[1:], send.dtype))
        ospecs.append(any_spec)
    ntot_in = len(ops)

    def wrapped(*refs):
        host_in = refs[:nin]
        sends, pos = [], nin
        for (_, _, recv) in xs:
            sends.append(refs[pos])
            pos += 1 if recv is None else 2
        host_out = refs[ntot_in:ntot_in + nout]
        recvs = refs[ntot_in + nout:ntot_in + nout + nx]
        host_scr = refs[ntot_in + nout + nx:ntot_in + nout + nx + nscr]
        send_sems, recv_sems, local_sems = refs[ntot_in + nout + nx + nscr:]
        first = pl.program_id(0) == 0
        last = pl.program_id(0) == grid[0] - 1
        for ax in range(1, len(grid)):
            first = jnp.logical_and(first, pl.program_id(ax) == 0)
            last = jnp.logical_and(last, pl.program_id(ax) == grid[ax] - 1)
        x, y, c = _my_pos()
        me = 4 * x + 2 * y + c

        def each(fn_remote, fn_local):
            for a, (_, dests, _) in enumerate(xs):
                lo, nd = dests[0], len(dests)
                for k in range(1, NDEV):
                    px, py, pc = _peer(k, x, y, c)
                    pidx = 4 * px + 2 * py + pc
                    cp = pltpu.make_async_remote_copy(src_ref=sends[a].at[jnp.clip(pidx - lo, 0, nd - 1)], dst_ref=recvs[a].at[me],
                                                      send_sem=send_sems.at[a, k - 1], recv_sem=recv_sems.at[a, k - 1],
                                                      device_id=(px, py, pc), device_id_type=MESH)
                    fn_remote(cp, _in_set(pidx, dests), _in_set(me, dests))
                lc = pltpu.make_async_copy(sends[a].at[jnp.clip(me - lo, 0, nd - 1)], recvs[a].at[me], local_sems.at[a])
                fn_local(lc, _in_set(me, dests))

        def start_remote(cp, to_dest, _):
            pl.when(jnp.logical_and(first, to_dest))(cp.start)

        def start_local(lc, i_am_dest):
            pl.when(jnp.logical_and(first, i_am_dest))(lc.start)

        def wait_remote(cp, to_dest, i_am_dest):
            pl.when(jnp.logical_and(last, to_dest))(cp.wait_send)
            pl.when(jnp.logical_and(last, i_am_dest))(cp.wait_recv)

        def wait_local(lc, i_am_dest):
            pl.when(jnp.logical_and(last, i_am_dest))(lc.wait)

        each(start_remote, start_local)
        body(*host_in, *host_out, *host_scr)
        each(wait_remote, wait_local)

    res = pl.pallas_call(
        wrapped, name=name, grid=grid, in_specs=specs, out_specs=ospecs, out_shape=oshape, input_output_aliases=aliases,
        scratch_shapes=list(scratch_shapes) + [pltpu.SemaphoreType.DMA((nx, NDEV - 1)), pltpu.SemaphoreType.DMA((nx, NDEV - 1)),
                                               pltpu.SemaphoreType.DMA((nx,))],
        compiler_params=compiler_params,
    )(*ops)
    return res[:nout], res[nout:]


def _mm_tn(a, b, name):
    S, M = a.shape
    N = b.shape[1]
    tk = min(2048, S)
    tn = N if N <= 768 else (640 if N % 640 == 0 else 512)
    nk = S // tk

    def body(a_ref, b_ref, o_ref):
        @pl.when(pl.program_id(1) == 0)
        def _():
            o_ref[...] = jnp.zeros_like(o_ref)
        o_ref[...] += _dot(a_ref[...], b_ref[...], TN)

    return pl.pallas_call(
        body, name=name, out_shape=jax.ShapeDtypeStruct((M, N), F32), grid=(N // tn, nk),
        in_specs=[pl.BlockSpec((tk, M), lambda j, k: (k, 0)), pl.BlockSpec((tk, tn), lambda j, k: (k, j))],
        out_specs=pl.BlockSpec((M, tn), lambda j, k: (0, j)),
        compiler_params=_cp(("parallel", "arbitrary"), VMEM_BIG),
    )(a, b)


def _fwd_in(x, mod, g_pre, w_p):
    S = x.shape[0]
    tm = min(512, S)

    def body(x_ref, mod_ref, g_ref, w_ref, p_ref, h_ref):
        xv = x_ref[...]
        r = lax.rsqrt(jnp.mean(xv * xv, axis=-1, keepdims=True) + EPS)
        h = (((xv * r) * g_ref[...]) * (1.0 + mod_ref[1:2, :]) + mod_ref[0:1, :]).astype(BF16)
        h_ref[...] = h
        p_ref[...] = _dot(h, w_ref[...]).astype(BF16)

    return pl.pallas_call(
        body, name="fwd_in", out_shape=[jax.ShapeDtypeStruct((S, P_W), BF16), jax.ShapeDtypeStruct((S, D), BF16)],
        grid=(S // tm,),
        in_specs=[pl.BlockSpec((tm, D), lambda i: (i, 0)), _full((3, D)), _full((1, D)), _full((D, P_W))],
        out_specs=[pl.BlockSpec((tm, P_W), lambda i: (i, 0)), pl.BlockSpec((tm, D), lambda i: (i, 0))],
        compiler_params=_cp(("parallel",), VMEM_BIG),
    )(x, mod, g_pre, w_p)


def _swap16(v):
    lane = lax.broadcasted_iota(jnp.int32, v.shape, 1)
    return jnp.where((lane % 32) < 16, pltpu.roll(v, 112, 1), pltpu.roll(v, 16, 1))


def _rope(v, cos, sin):
    return v * cos + _swap16(v) * sin


def _rope_t(v, cos, sin):
    return v * cos - _swap16(v) * sin


def _head_mean(v):
    lo = lax.broadcasted_iota(jnp.int32, v.shape, 1) < 64
    m0 = jnp.sum(jnp.where(lo, v, 0.0), axis=-1, keepdims=True)
    m1 = jnp.sum(jnp.where(lo, 0.0, v), axis=-1, keepdims=True)
    return jnp.where(lo, m0, m1) * (1.0 / 64.0)


def _prep(p, cos, sin, qg, kg):
    S = p.shape[0]
    tm = min(512, S)

    def body(qa_ref, kv_ref, qr_ref, kr_ref, cos_ref, sin_ref, qg_ref, kg_ref, qt_ref, kh_ref, kt_ref, vh_ref, vta_ref, qr2_ref, kr2_ref):
        cos_v, sin_v = cos_ref[...], sin_ref[...]
        for g in range(4):
            xv = qa_ref[:, 128 * g:128 * g + 128].astype(F32)
            r = lax.rsqrt(_head_mean(xv * xv) + EPS)
            yt = (_rope((xv * r) * qg_ref[...], cos_v, sin_v) * (0.125 * LOG2E)).T
            qt_ref[2 * g] = yt[:DH].astype(BF16)
            qt_ref[2 * g + 1] = yt[DH:].astype(BF16)
        xv = kv_ref[:, :128].astype(F32)
        r = lax.rsqrt(_head_mean(xv * xv) + EPS)
        yv = _rope((xv * r) * kg_ref[...], cos_v, sin_v)
        kh_ref[0] = yv[:, :64].astype(BF16)
        kh_ref[1] = yv[:, 64:].astype(BF16)
        yt = yv.T
        kt_ref[0] = yt[:DH].astype(BF16)
        kt_ref[1] = yt[DH:].astype(BF16)
        vv = kv_ref[:, 128:].astype(F32)
        vh_ref[0] = vv[:, :64].astype(BF16)
        vh_ref[1] = vv[:, 64:].astype(BF16)
        vt = vv.T
        tail = (lax.broadcasted_iota(jnp.int32, (DHA - DH, tm), 0) == 0).astype(BF16)
        for kvh in range(2):
            vta_ref[kvh, 0:DH, :] = vt[DH * kvh:DH * kvh + DH].astype(BF16)
            vta_ref[kvh, DH:DHA, :] = tail
        for g in range(2):
            sl = slice(128 * g, 128 * g + 128)
            qr2_ref[:, sl] = _rope(qr_ref[:, sl].astype(F32), cos_v, sin_v)
            kr2_ref[:, sl] = _rope(kr_ref[:, sl].astype(F32), cos_v, sin_v) * 0.125

    hm = lambda n: pl.BlockSpec((n, tm, DH), lambda i: (0, i, 0))
    ht = lambda n, r: pl.BlockSpec((n, r, tm), lambda i: (0, 0, i))
    return pl.pallas_call(
        body, name="prep",
        out_shape=[jax.ShapeDtypeStruct((8, DH, S), BF16), jax.ShapeDtypeStruct((2, S, DH), BF16), jax.ShapeDtypeStruct((2, DH, S), BF16),
                   jax.ShapeDtypeStruct((2, S, DH), BF16), jax.ShapeDtypeStruct((2, DHA, S), BF16),
                   jax.ShapeDtypeStruct((S, 256), F32), jax.ShapeDtypeStruct((S, 256), F32)],
        grid=(S // tm,),
        in_specs=[pl.BlockSpec((tm, 512), lambda i: (i, O_QA // 512)), pl.BlockSpec((tm, 256), lambda i: (i, O_KA // 256)),
                  pl.BlockSpec((tm, 256), lambda i: (i, O_QR // 256)), pl.BlockSpec((tm, 256), lambda i: (i, O_KR // 256)),
                  pl.BlockSpec((tm, 128), lambda i: (i, 0)), pl.BlockSpec((tm, 128), lambda i: (i, 0)), _full((1, 128)), _full((1, 128))],
        out_specs=[ht(8, DH), hm(2), ht(2, DH), hm(2), ht(2, DHA), pl.BlockSpec((tm, 256), lambda i: (i, 0)), pl.BlockSpec((tm, 256), lambda i: (i, 0))],
        compiler_params=_cp(("parallel",)),
    )(p, p, p, p, cos, sin, qg, kg)


def _attn_fwd(qt, kh, vta):
    S = qt.shape[2]
    tq, tk = min(512, S), min(512, S)
    nj = S // tk

    def body(q_ref, k_ref, v_ref, o_ref, ot_ref, lse_ref, m_s, acc_s):
        j = pl.program_id(1)

        @pl.when(j == 0)
        def _():
            m_s[...] = jnp.full_like(m_s, -jnp.inf)
            acc_s[...] = jnp.zeros_like(acc_s)

        m_all = m_s[...]
        st = {0: _dot(k_ref[0], q_ref[0])}
        m_new, acc_new = [], []
        for h in range(8):
            if h + 1 < 8:
                st[h + 1] = _dot(k_ref[(h + 1) // 4], q_ref[h + 1])
            m_old = m_all[h:h + 1, :]
            mn = jnp.maximum(m_old, jnp.max(st[h], axis=0, keepdims=True))
            pt = jnp.exp2(st[h] - mn).astype(BF16)
            acc_new.append(jnp.exp2(m_old - mn) * acc_s[h] + _dot(v_ref[h // 4], pt))
            m_new.append(mn)
            del st[h]
        for h in range(8):
            acc_s[h] = acc_new[h]
            m_s[h:h + 1, :] = m_new[h]

        @pl.when(j == nj - 1)
        def _():
            for h in range(8):
                ot = acc_s[h, 0:DH, :] / acc_s[h, DH:DH + 1, :]
                ot_ref[h] = ot
                o_ref[:, DH * h:DH * h + DH] = ot.T
                lse_ref[h // 4, h % 4:h % 4 + 1, :] = m_s[h:h + 1, :] + jnp.log2(acc_s[h, DH:DH + 1, :])

    return pl.pallas_call(
        body, name="attn_fwd",
        out_shape=[jax.ShapeDtypeStruct((S, 512), F32), jax.ShapeDtypeStruct((8, DH, S), F32), jax.ShapeDtypeStruct((2, 4, S), F32)],
        grid=(S // tq, nj),
        in_specs=[pl.BlockSpec((8, DH, tq), lambda i, j: (0, 0, i)), pl.BlockSpec((2, tk, DH), lambda i, j: (0, j, 0)),
                  pl.BlockSpec((2, DHA, tk), lambda i, j: (0, 0, j))],
        out_specs=[pl.BlockSpec((tq, 512), lambda i, j: (i, 0)), pl.BlockSpec((8, DH, tq), lambda i, j: (0, 0, i)),
                   pl.BlockSpec((2, 4, tq), lambda i, j: (0, 0, i))],
        scratch_shapes=[pltpu.VMEM((8, tq), F32), pltpu.VMEM((8, DHA, tq), F32)],
        compiler_params=_cp(("parallel", "arbitrary"), VMEM_BIG),
    )(qt, kh, vta)


def _ret_tables(wf, wb):
    C = CH

    def body(wf_ref, wb_ref, dc_ref, qdf_ref, qdb_ref, kdf_ref, kdb_ref, a_ref):
        def logsig(w):
            z = jnp.exp(-jnp.abs(w))
            u = 1.0 + z
            l1p = jnp.where(u == 1.0, z, jnp.log(u) * (z / jnp.where(u == 1.0, 1.0, u - 1.0)))
            return jnp.minimum(w, 0.0) - l1p

        lgf, lgb = logsig(wf_ref[...]), logsig(wb_ref[...])
        lane4 = lax.broadcasted_iota(jnp.int32, (1, 4), 1)

        def pick(lg, h):
            return jnp.sum(jnp.where(lane4 == h, lg, 0.0), axis=-1, keepdims=True)

        ii = lax.broadcasted_iota(jnp.int32, (C, C), 0).astype(F32)
        jj = lax.broadcasted_iota(jnp.int32, (C, C), 1).astype(F32)
        dif = ii - jj
        hd = lax.broadcasted_iota(jnp.int32, (C, 256), 1) // DH
        lf_l = jnp.zeros((C, 256), F32)
        lb_l = jnp.zeros((C, 256), F32)
        for h in range(HR):
            lf, lb = pick(lgf, h), pick(lgb, h)
            dc_ref[h] = jnp.where(dif >= 0, jnp.exp(lf * jnp.maximum(dif, 0.0)), jnp.exp(lb * jnp.maximum(-dif, 0.0)))
            lf_l = jnp.where(hd == h, lf, lf_l)
            lb_l = jnp.where(hd == h, lb, lb_l)
            a_ref[h:h + 1, :] = jnp.broadcast_to(jnp.exp(lf * C), (1, 128))
            a_ref[HR + h:HR + h + 1, :] = jnp.broadcast_to(jnp.exp(lb * C), (1, 128))
        ri = lax.broadcasted_iota(jnp.int32, (C, 256), 0).astype(F32)
        qdf_ref[...] = jnp.exp(lf_l * (ri + 1.0))
        qdb_ref[...] = jnp.exp(lb_l * (C - ri))
        kdf_ref[...] = jnp.exp(lf_l * (C - 1.0 - ri))
        kdb_ref[...] = jnp.exp(lb_l * ri)

    t = jax.ShapeDtypeStruct((C, 256), F32)
    return pl.pallas_call(body, name="ret_tables",
                          out_shape=[jax.ShapeDtypeStruct((HR, C, C), F32), t, t, t, t, jax.ShapeDtypeStruct((8, 128), F32)])(wf, wb)


def _ret_states(kr2, p, kdf, kdb, adec):
    S = kr2.shape[0]
    C, N = CH, S // CH

    def body(kf_ref, vf_ref, kb_ref, vb_ref, kdf_ref, kdb_ref, a_ref, rf_ref, rb_ref, sf, sb):
        @pl.when(pl.program_id(0) == 0)
        def _():
            sf[...] = jnp.zeros_like(sf)
            sb[...] = jnp.zeros_like(sb)

        rf_ref[0] = sf[...]
        rb_ref[0] = sb[...]
        kdfw = (kf_ref[...] * kdf_ref[...]).astype(BF16)
        kdbw = (kb_ref[...] * kdb_ref[...]).astype(BF16)
        vf, vb = vf_ref[...].astype(BF16), vb_ref[...].astype(BF16)
        kvf = [_dot(kdfw[:, _ks(h)], vf[:, _vs(h)], TN) for h in range(HR)]
        kvb = [_dot(kdbw[:, _ks(h)], vb[:, _vs(h)], TN) for h in range(HR)]
        for h in range(HR):
            sf[h] = a_ref[h:h + 1, :] * sf[h] + kvf[h]
            sb[h] = a_ref[HR + h:HR + h + 1, :] * sb[h] + kvb[h]

    st = jax.ShapeDtypeStruct((N, HR, DH, DV), F32)
    return pl.pallas_call(
        body, name="ret_states", out_shape=[st, st], grid=(N,),
        in_specs=[pl.BlockSpec((C, 256), lambda t: (t, 0)), pl.BlockSpec((C, 512), lambda t: (t, O_VR // 512)),
                  pl.BlockSpec((C, 256), lambda t: (N - 1 - t, 0)), pl.BlockSpec((C, 512), lambda t: (N - 1 - t, O_VR // 512)),
                  _full((C, 256)), _full((C, 256)), _full((8, 128))],
        out_specs=[pl.BlockSpec((1, HR, DH, DV), lambda t: (t, 0, 0, 0)), pl.BlockSpec((1, HR, DH, DV), lambda t: (N - 1 - t, 0, 0, 0))],
        scratch_shapes=[pltpu.VMEM((HR, DH, DV), F32), pltpu.VMEM((HR, DH, DV), F32)],
        compiler_params=_cp(("arbitrary",)),
    )(kr2, p, kr2, p, kdf, kdb, adec)


def _ks(h):
    return slice(DH * h, DH * h + DH)


def _vs(h):
    return slice(DV * h, DV * h + DV)


def _ret_heads_fwd(qb, kb, vb, qfw, qbw, dc_ref, rf_ref, rb_ref):
    hs = range(HR)
    s = [_dot(qb[:, _ks(h)], kb[:, _ks(h)], NT) for h in hs]
    inter = [_dot(qfw[:, _ks(h)], rf_ref[0, h].astype(BF16)) + _dot(qbw[:, _ks(h)], rb_ref[0, h].astype(BF16)) for h in hs]
    sd = [s[h] * dc_ref[h] for h in hs]
    o = [_dot(sd[h].astype(BF16), vb[:, _vs(h)]) + inter[h] for h in hs]
    return sd, o


def _ret_out(qr2, kr2, p, rf, rb, dc, qdf, qdb, gn):
    S = qr2.shape[0]
    C, N = CH, S // CH

    def body(q_ref, k_ref, v_ref, z_ref, rf_ref, rb_ref, dc_ref, qdf_ref, qdb_ref, gn_ref, yr_ref):
        qv = q_ref[...]
        qb, kb, vb = qv.astype(BF16), k_ref[...].astype(BF16), v_ref[...].astype(BF16)
        qfw, qbw = (qv * qdf_ref[...]).astype(BF16), (qv * qdb_ref[...]).astype(BF16)
        _, o = _ret_heads_fwd(qb, kb, vb, qfw, qbw, dc_ref, rf_ref, rb_ref)
        for h in range(HR):
            vs = _vs(h)
            mu = jnp.mean(o[h], axis=-1, keepdims=True)
            var = jnp.mean(jnp.square(o[h] - mu), axis=-1, keepdims=True)
            on = (o[h] - mu) * lax.rsqrt(var + EPS)
            z = z_ref[:, vs].astype(F32)
            yr_ref[:, vs] = ((on * gn_ref[:, vs]) * (z * _sigmoid(z))).astype(BF16)

    return pl.pallas_call(
        body, name="ret_out", out_shape=jax.ShapeDtypeStruct((S, 512), BF16), grid=(N,),
        in_specs=[pl.BlockSpec((C, 256), lambda t: (t, 0)), pl.BlockSpec((C, 256), lambda t: (t, 0)),
                  pl.BlockSpec((C, 512), lambda t: (t, O_VR // 512)), pl.BlockSpec((C, 512), lambda t: (t, O_ZR // 512)),
                  pl.BlockSpec((1, HR, DH, DV), lambda t: (t, 0, 0, 0)), pl.BlockSpec((1, HR, DH, DV), lambda t: (t, 0, 0, 0)),
                  _full((HR, C, C)), _full((C, 256)), _full((C, 256)), _full((1, 512))],
        out_specs=pl.BlockSpec((C, 512), lambda t: (t, 0)),
        compiler_params=_cp(("parallel",)),
    )(qr2, kr2, p, p, rf, rb, dc, qdf, qdb, gn)


def _mid(x, tgt, mod, g_post, o_att, p, yr, w_pa, w_pr, w_out):
    S = x.shape[0]
    tm = min(256, S)

    def body(x_ref, t_ref, mod_ref, gp_ref, o_ref, za_ref, gl_ref, yr_ref, wpa_ref, wpr_ref, wout_ref,
             dout_ref, do_ref, dpm_ref, dyr_ref, mb_ref, dub_ref, yab_ref, dab_ref, drb_ref, sums_ref):
        @pl.when(pl.program_id(0) == 0)
        def _():
            sums_ref[...] = jnp.zeros_like(sums_ref)

        za = za_ref[...].astype(F32)
        sa = _sigmoid(za)
        sil = za * sa
        ov = o_ref[...]
        ya_b = (ov * sil).astype(BF16)
        yr_b = yr_ref[...]
        av = _dot(ya_b, wpa_ref[...])
        rv = _dot(yr_b, wpr_ref[...])
        ga = _sigmoid(gl_ref[:, :D].astype(F32))
        gr = _sigmoid(gl_ref[:, D:].astype(F32))
        mb = (ga * av + gr * rv).astype(BF16)
        u = _dot(mb, wout_ref[...])
        r2 = lax.rsqrt(jnp.mean(u * u, axis=-1, keepdims=True) + EPS)
        un = u * r2
        gp = gp_ref[...]
        yv = un * gp
        gate = mod_ref[2:3, :]
        err = (x_ref[...] + gate * yv) - t_ref[...]
        dout = err * (1.0 / D)
        dout_ref[...] = dout
        dy = dout * gate
        sums_ref[0:1, :] += jnp.sum(dout * yv, axis=0, keepdims=True)
        sums_ref[1:2, :] += jnp.sum(dy * un, axis=0, keepdims=True)
        sums_ref[2:3, :] += jnp.sum(err * err, axis=0, keepdims=True)
        dyg = dy * gp
        du_b = (r2 * (dyg - un * jnp.mean(dyg * un, axis=-1, keepdims=True))).astype(BF16)
        dm = _dot(du_b, wout_ref[...], NT)
        da_b = (dm * ga).astype(BF16)
        dr_b = (dm * gr).astype(BF16)
        dpm_ref[:, :D] = (dm * av * (ga * (1.0 - ga))).astype(BF16)
        dpm_ref[:, D:2 * D] = (dm * rv * (gr * (1.0 - gr))).astype(BF16)
        dya = _dot(da_b, wpa_ref[...], NT)
        dyr_ref[...] = _dot(dr_b, wpr_ref[...], NT)
        dov = dya * sil
        for g in range(4):
            dt = dov[:, 128 * g:128 * g + 128].T
            do_ref[2 * g] = dt[:DH].astype(BF16)
            do_ref[2 * g + 1] = dt[DH:].astype(BF16)
        dpm_ref[:, 2 * D:] = (dya * ov * (sa * (1.0 + za * (1.0 - sa)))).astype(BF16)
        mb_ref[...] = mb
        dub_ref[...] = du_b
        yab_ref[...] = ya_b
        dab_ref[...] = da_b
        drb_ref[...] = dr_b

    row = lambda w: pl.BlockSpec((tm, w), lambda i: (i, 0))
    sd = lambda w, dt: jax.ShapeDtypeStruct((S, w), dt)
    return pl.pallas_call(
        body, name="mid",
        out_shape=[sd(D, F32), jax.ShapeDtypeStruct((8, DH, S), BF16), sd(2560, BF16), sd(512, F32), sd(D, BF16), sd(D, BF16), sd(512, BF16),
                   sd(D, BF16), sd(D, BF16), jax.ShapeDtypeStruct((8, D), F32)],
        grid=(S // tm,),
        in_specs=[row(D), row(D), _full((3, D)), _full((1, D)), row(512), pl.BlockSpec((tm, 512), lambda i: (i, O_ZA // 512)),
                  pl.BlockSpec((tm, 2048), lambda i: (i, 0)), row(512), _full((512, D)), _full((512, D)), _full((D, D))],
        out_specs=[row(D), pl.BlockSpec((8, DH, tm), lambda i: (0, 0, i)), row(2560), row(512), row(D), row(D), row(512), row(D), row(D),
                   _full((8, D))],
        compiler_params=_cp(("arbitrary",), VMEM_BIG),
    )(x, tgt, mod, g_post, o_att, p, p, yr, w_pa, w_pr, w_out)


def _attn_bwd(qt, kh, kt, vh, dot_, ot, lse, xs):
    S = qt.shape[2]
    tq, tk = min(512, S), min(512, S)

    def body(q_ref, k_ref, kt_ref, v_ref, do_ref, o_ref, lse_ref, dq_ref, dk_ref, dv_ref):
        j, i = pl.program_id(0), pl.program_id(1)
        cols = pl.ds(pl.multiple_of(i * tq, tq), tq)
        st = {0: _dot(k_ref[0], q_ref[0])}
        dpt = {0: _dot(v_ref[0], do_ref[0])}
        dk_acc, dv_acc, dqs = [None, None], [None, None], []
        for h in range(8):
            g = h // 4
            if h + 1 < 8:
                st[h + 1] = _dot(k_ref[(h + 1) // 4], q_ref[h + 1])
                dpt[h + 1] = _dot(v_ref[(h + 1) // 4], do_ref[h + 1])
            qt_h, dot_h = q_ref[h], do_ref[h]
            delta = jnp.sum(dot_h.astype(F32) * o_ref[h], axis=0, keepdims=True)
            pt = jnp.exp2(st[h] - lse_ref[g, h % 4:h % 4 + 1, :])
            dst = (pt * (dpt[h] - delta)).astype(BF16)
            dv_h = _dot(dot_h, pt.astype(BF16), NT)
            dk_h = _dot(qt_h, dst, NT)
            dqs.append(_dot(kt_ref[g], dst))
            dv_acc[g] = dv_h if dv_acc[g] is None else dv_acc[g] + dv_h
            dk_acc[g] = dk_h if dk_acc[g] is None else dk_acc[g] + dk_h
            del st[h], dpt[h]

        @pl.when(i == 0)
        def _():
            for g in range(2):
                dk_ref[g] = dk_acc[g]
                dv_ref[g] = dv_acc[g]

        @pl.when(i > 0)
        def _():
            for g in range(2):
                dk_ref[g] += dk_acc[g]
                dv_ref[g] += dv_acc[g]

        @pl.when(j == 0)
        def _():
            for h in range(8):
                dq_ref[h, :, cols] = dqs[h]

        @pl.when(j > 0)
        def _():
            for h in range(8):
                dq_ref[h, :, cols] += dqs[h]

    return _host_call(
        body, xs, name="attn_bwd",
        out_shape=[jax.ShapeDtypeStruct((8, DH, S), F32), jax.ShapeDtypeStruct((2, DH, S), F32), jax.ShapeDtypeStruct((2, DH, S), F32)],
        grid=(S // tk, S // tq),
        in_specs=[pl.BlockSpec((8, DH, tq), lambda j, i: (0, 0, i)), pl.BlockSpec((2, tk, DH), lambda j, i: (0, j, 0)),
                  pl.BlockSpec((2, DH, tk), lambda j, i: (0, 0, j)), pl.BlockSpec((2, tk, DH), lambda j, i: (0, j, 0)),
                  pl.BlockSpec((8, DH, tq), lambda j, i: (0, 0, i)), pl.BlockSpec((8, DH, tq), lambda j, i: (0, 0, i)),
                  pl.BlockSpec((2, 4, tq), lambda j, i: (0, 0, i))],
        out_specs=[pl.BlockSpec((8, DH, S), lambda j, i: (0, 0, 0)), pl.BlockSpec((2, DH, tk), lambda j, i: (0, 0, j)),
                   pl.BlockSpec((2, DH, tk), lambda j, i: (0, 0, j))],
        scratch_shapes=[], operands=(qt, kh, kt, vh, dot_, ot, lse),
        compiler_params=_cp(("arbitrary", "arbitrary"), VMEM_BIG),
    )


def _attn_prep_bwd(dqt, dkt, dvt, p, cos, sin, qg, kg):
    S = dqt.shape[2]
    tm = min(512, S)

    def body(dq_ref, dk_ref, dv_ref, qa_ref, ka_ref, cos_ref, sin_ref, qg_ref, kg_ref, dp_ref, gs_ref):
        @pl.when(pl.program_id(0) == 0)
        def _():
            gs_ref[...] = jnp.zeros_like(gs_ref)

        cos_v, sin_v = cos_ref[...], sin_ref[...]

        def pair(ref, a):
            return jnp.concatenate([ref[a], ref[a + 1]], axis=0).T

        def norm_bwd(dyv, xv, gv, row):
            r = lax.rsqrt(_head_mean(xv * xv) + EPS)
            xn = xv * r
            dxh = _rope_t(dyv, cos_v, sin_v)
            gs_ref[row:row + 1, :] += jnp.sum(dxh * xn, axis=0, keepdims=True)
            dg = dxh * gv
            return r * (dg - xn * _head_mean(dg * xn))

        for g in range(4):
            sl = slice(128 * g, 128 * g + 128)
            dp_ref[:, sl] = norm_bwd(pair(dq_ref, 2 * g) * 0.125, qa_ref[:, sl].astype(F32), qg_ref[...], 0).astype(BF16)
        dp_ref[:, 512:640] = norm_bwd(pair(dk_ref, 0) * LN2, ka_ref[...].astype(F32), kg_ref[...], 1).astype(BF16)
        dp_ref[:, 640:768] = pair(dv_ref, 0).astype(BF16)

    ht = lambda n: pl.BlockSpec((n, DH, tm), lambda i: (0, 0, i))
    return pl.pallas_call(
        body, name="attn_prep_bwd", out_shape=[jax.ShapeDtypeStruct((S, 768), BF16), jax.ShapeDtypeStruct((8, 128), F32)],
        grid=(S // tm,),
        in_specs=[ht(8), ht(2), ht(2),
                  pl.BlockSpec((tm, 512), lambda i: (i, O_QA // 512)), pl.BlockSpec((tm, 128), lambda i: (i, O_KA // 128)),
                  pl.BlockSpec((tm, 128), lambda i: (i, 0)), pl.BlockSpec((tm, 128), lambda i: (i, 0)), _full((1, 128)), _full((1, 128))],
        out_specs=[pl.BlockSpec((tm, 768), lambda i: (i, 0)), _full((8, 128))],
        compiler_params=_cp(("arbitrary",)),
    )(dqt, dkt, dvt, p, p, cos, sin, qg, kg)


def _ret_bwd_chunk(qr2, kr2, p, rf, rb, dc, qdf, qdb, gn, dyr, cos, sin, xs):
    S = qr2.shape[0]
    C, N = CH, S // CH

    def body(q_ref, k_ref, v_ref, z_ref, rf_ref, rb_ref, dc_ref, qdf_ref, qdb_ref, gn_ref, dyr_ref, cos_ref, sin_ref,
             dpa_ref, dk_ref, dv_ref, drf_ref, drb_ref, dgn_ref, dlg_ref, dqs):
        @pl.when(pl.program_id(0) == 0)
        def _():
            dgn_ref[...] = jnp.zeros_like(dgn_ref)
            dlg_ref[...] = jnp.zeros_like(dlg_ref)

        qv = q_ref[...]
        qb, kb, vb = qv.astype(BF16), k_ref[...].astype(BF16), v_ref[...].astype(BF16)
        qf32, qb32 = qv * qdf_ref[...], qv * qdb_ref[...]
        qfw, qbw = qf32.astype(BF16), qb32.astype(BF16)
        ii = lax.broadcasted_iota(jnp.int32, (C, C), 0).astype(F32)
        jj = lax.broadcasted_iota(jnp.int32, (C, C), 1).astype(F32)
        dif = ii - jj
        ri = lax.broadcasted_iota(jnp.int32, (C, 1), 0).astype(F32)
        hs = range(HR)
        sd, o = _ret_heads_fwd(qb, kb, vb, qfw, qbw, dc_ref, rf_ref, rb_ref)
        do_b = []
        for h in hs:
            vs = _vs(h)
            mu = jnp.mean(o[h], axis=-1, keepdims=True)
            rstd = lax.rsqrt(jnp.mean(jnp.square(o[h] - mu), axis=-1, keepdims=True) + EPS)
            on = (o[h] - mu) * rstd
            z = z_ref[:, vs].astype(F32)
            sz = _sigmoid(z)
            dy = dyr_ref[:, vs]
            gnv = gn_ref[:, vs]
            dpa_ref[:, 256 + DV * h:256 + DV * h + DV] = (dy * (on * gnv) * (sz * (1.0 + z * (1.0 - sz)))).astype(BF16)
            dys = dy * (z * sz)
            dgn_ref[:, vs] += jnp.sum(dys * on, axis=0, keepdims=True)
            don = dys * gnv
            do = rstd * (don - jnp.mean(don, axis=-1, keepdims=True) - on * jnp.mean(don * on, axis=-1, keepdims=True))
            do_b.append(do.astype(BF16))
        dpm = [_dot(do_b[h], vb[:, _vs(h)], NT) for h in hs]
        dqf = [_dot(do_b[h], rf_ref[0, h].astype(BF16), NT) for h in hs]
        dqb = [_dot(do_b[h], rb_ref[0, h].astype(BF16), NT) for h in hs]
        for h in hs:
            dv_ref[:, _vs(h)] = _dot(sd[h].astype(BF16), do_b[h], TN)
            drf_ref[0, h] = _dot(qfw[:, _ks(h)], do_b[h], TN)
            drb_ref[0, h] = _dot(qbw[:, _ks(h)], do_b[h], TN)
        dsd = [(dpm[h] * dc_ref[h]).astype(BF16) for h in hs]
        for h in hs:
            ks = _ks(h)
            dqs[:, ks] = _dot(dsd[h], kb[:, ks]) + dqf[h] * qdf_ref[:, ks] + dqb[h] * qdb_ref[:, ks]
            dk_ref[:, ks] = _dot(dsd[h], qb[:, ks], TN)
        for h in hs:
            ks = _ks(h)
            e = dpm[h] * sd[h]
            lf = _sum11(e * jnp.maximum(dif, 0.0)) + _sum11(jnp.sum(qf32[:, ks] * dqf[h], axis=-1, keepdims=True) * (ri + 1.0))
            lb = _sum11(e * jnp.maximum(-dif, 0.0)) + _sum11(jnp.sum(qb32[:, ks] * dqb[h], axis=-1, keepdims=True) * (C - ri))
            dlg_ref[h:h + 1, :] += jnp.broadcast_to(lf, (1, 128))
            dlg_ref[HR + h:HR + h + 1, :] += jnp.broadcast_to(lb, (1, 128))
        cos_v, sin_v = cos_ref[...], sin_ref[...]
        for g in range(2):
            sl = slice(128 * g, 128 * g + 128)
            dpa_ref[:, sl] = _rope_t(dqs[:, sl], cos_v, sin_v).astype(BF16)

    st = jax.ShapeDtypeStruct((N, HR, DH, DV), F32)
    stb = lambda: pl.BlockSpec((1, HR, DH, DV), lambda t: (t, 0, 0, 0))
    return _host_call(
        body, xs, name="ret_bwd_chunk",
        out_shape=[jax.ShapeDtypeStruct((S, 768), BF16), jax.ShapeDtypeStruct((S, 256), F32), jax.ShapeDtypeStruct((S, 512), F32), st, st,
                   jax.ShapeDtypeStruct((1, 512), F32), jax.ShapeDtypeStruct((8, 128), F32)],
        grid=(N,),
        in_specs=[pl.BlockSpec((C, 256), lambda t: (t, 0)), pl.BlockSpec((C, 256), lambda t: (t, 0)),
                  pl.BlockSpec((C, 512), lambda t: (t, O_VR // 512)), pl.BlockSpec((C, 512), lambda t: (t, O_ZR // 512)),
                  stb(), stb(), _full((HR, C, C)), _full((C, 256)), _full((C, 256)), _full((1, 512)),
                  pl.BlockSpec((C, 512), lambda t: (t, 0)), pl.BlockSpec((C, 128), lambda t: (t, 0)), pl.BlockSpec((C, 128), lambda t: (t, 0))],
        out_specs=[pl.BlockSpec((C, 768), lambda t: (t, 0)), pl.BlockSpec((C, 256), lambda t: (t, 0)), pl.BlockSpec((C, 512), lambda t: (t, 0)),
                   stb(), stb(), _full((1, 512)), _full((8, 128))],
        scratch_shapes=[pltpu.VMEM((C, 256), F32)], operands=(qr2, kr2, p, p, rf, rb, dc, qdf, qdb, gn, dyr, cos, sin),
        compiler_params=_cp(("arbitrary",)),
    )


def _ret_bwd_scan(kr2, p, rf, rb, drf, drb, kdf, kdb, adec):
    S = kr2.shape[0]
    C, N = CH, S // CH

    def body(kf_ref, vf_ref, kb_ref, vb_ref, rf_ref, rb_ref, drf_ref, drb_ref, kdf_ref, kdb_ref, a_ref,
             dkf_ref, dkb_ref, dvf_ref, dvb_ref, dlg_ref, gf, gb):
        @pl.when(pl.program_id(0) == 0)
        def _():
            gf[...] = jnp.zeros_like(gf)
            gb[...] = jnp.zeros_like(gb)
            dlg_ref[...] = jnp.zeros_like(dlg_ref)

        ri = lax.broadcasted_iota(jnp.int32, (C, 1), 0).astype(F32)

        def one(k_ref, v_ref, r_ref, dr_ref, kd_ref, g_s, dk_ref, dv_ref, row0, wexp):
            kd32 = k_ref[...] * kd_ref[...]
            kdw = kd32.astype(BF16)
            vb = v_ref[...].astype(BF16)
            for h in range(HR):
                ks, vs = _ks(h), _vs(h)
                gst = g_s[h]
                g_b = gst.astype(BF16)
                dkd = _dot(vb[:, vs], g_b, NT)
                dk_ref[:, ks] = dkd * kd_ref[:, ks]
                dv_ref[:, vs] = _dot(kdw[:, ks], g_b)
                av = a_ref[row0 + h:row0 + h + 1, :]
                lg = (_sum11(jnp.sum(kd32[:, ks] * dkd, axis=-1, keepdims=True) * wexp)
                      + C * av[:, 0:1] * _sum11(r_ref[0, h] * gst))
                dlg_ref[row0 + h:row0 + h + 1, :] += jnp.broadcast_to(lg, (1, 128))
                g_s[h] = dr_ref[0, h] + av * gst

        one(kf_ref, vf_ref, rf_ref, drf_ref, kdf_ref, gf, dkf_ref, dvf_ref, 0, C - 1.0 - ri)
        one(kb_ref, vb_ref, rb_ref, drb_ref, kdb_ref, gb, dkb_ref, dvb_ref, HR, ri)

    fwd = lambda w, off=0: pl.BlockSpec((C, w), lambda t: (N - 1 - t, off))
    bwd = lambda w, off=0: pl.BlockSpec((C, w), lambda t: (t, off))
    stf = lambda: pl.BlockSpec((1, HR, DH, DV), lambda t: (N - 1 - t, 0, 0, 0))
    stb = lambda: pl.BlockSpec((1, HR, DH, DV), lambda t: (t, 0, 0, 0))
    return pl.pallas_call(
        body, name="ret_bwd_scan",
        out_shape=[jax.ShapeDtypeStruct((S, 256), F32), jax.ShapeDtypeStruct((S, 256), F32), jax.ShapeDtypeStruct((S, 512), F32),
                   jax.ShapeDtypeStruct((S, 512), F32), jax.ShapeDtypeStruct((8, 128), F32)],
        grid=(N,),
        in_specs=[fwd(256), fwd(512, O_VR // 512), bwd(256), bwd(512, O_VR // 512), stf(), stb(), stf(), stb(),
                  _full((C, 256)), _full((C, 256)), _full((8, 128))],
        out_specs=[fwd(256), bwd(256), fwd(512), bwd(512), _full((8, 128))],
        scratch_shapes=[pltpu.VMEM((HR, DH, DV), F32), pltpu.VMEM((HR, DH, DV), F32)],
        compiler_params=_cp(("arbitrary",)),
    )(kr2, p, kr2, p, rf, rb, drf, drb, kdf, kdb, adec)


def _ret_bwd_final(dk_i, dkf, dkb, dv_i, dvf, dvb, cos, sin):
    S = dk_i.shape[0]
    tm = min(512, S)

    def body(a_ref, b_ref, c_ref, d_ref, e_ref, f_ref, cos_ref, sin_ref, o_ref):
        o_ref[:, :512] = (d_ref[...] + e_ref[...] + f_ref[...]).astype(BF16)
        cos_v, sin_v = cos_ref[...], sin_ref[...]
        for g in range(2):
            sl = slice(128 * g, 128 * g + 128)
            dk = a_ref[:, sl] + b_ref[:, sl] + c_ref[:, sl]
            o_ref[:, 512 + 128 * g:512 + 128 * g + 128] = (_rope_t(dk, cos_v, sin_v) * 0.125).astype(BF16)

    row = lambda w: pl.BlockSpec((tm, w), lambda i: (i, 0))
    return pl.pallas_call(
        body, name="ret_bwd_final", out_shape=jax.ShapeDtypeStruct((S, 768), BF16), grid=(S // tm,),
        in_specs=[row(256), row(256), row(256), row(512), row(512), row(512), row(128), row(128)], out_specs=row(768),
        compiler_params=_cp(("parallel",)),
    )(dk_i, dkf, dkb, dv_i, dvf, dvb, cos, sin)


def _bwd_in(dpm, dpa, dpra, dprb, w_p, x, dout, mod, g_pre, xs):
    S = x.shape[0]
    tm = min(256, S)

    def body(a_ref, b_ref, c_ref, d_ref, w_ref, x_ref, dout_ref, mod_ref, g_ref, gx_ref, sums_ref):
        @pl.when(pl.program_id(0) == 0)
        def _():
            sums_ref[...] = jnp.zeros_like(sums_ref)

        dh = (_dot(a_ref[...], w_ref[:, :O_QA], NT) + _dot(b_ref[...], w_ref[:, O_QA:O_QR], NT)
              + _dot(c_ref[...], w_ref[:, O_QR:O_VR], NT) + _dot(d_ref[...], w_ref[:, O_VR:], NT))
        xv = x_ref[...]
        r = lax.rsqrt(jnp.mean(xv * xv, axis=-1, keepdims=True) + EPS)
        xn = xv * r
        gv = g_ref[...]
        sc1 = 1.0 + mod_ref[1:2, :]
        sums_ref[0:1, :] += jnp.sum(dh, axis=0, keepdims=True)
        sums_ref[1:2, :] += jnp.sum(dh * (xn * gv), axis=0, keepdims=True)
        sums_ref[2:3, :] += jnp.sum(dh * xn, axis=0, keepdims=True) * sc1
        dxn = dh * (gv * sc1)
        gx_ref[...] = dout_ref[...] + r * (dxn - xn * jnp.mean(dxn * xn, axis=-1, keepdims=True))

    row = lambda w: pl.BlockSpec((tm, w), lambda i: (i, 0))
    return _host_call(
        body, xs, name="bwd_in", out_shape=[jax.ShapeDtypeStruct((S, D), F32), jax.ShapeDtypeStruct((8, D), F32)], grid=(S // tm,),
        in_specs=[row(2560), row(768), row(768), row(768), _full((D, P_W)), row(D), row(D), _full((3, D)), _full((1, D))],
        out_specs=[row(D), _full((8, D))], scratch_shapes=[], operands=(dpm, dpa, dpra, dprb, w_p, x, dout, mod, g_pre),
        compiler_params=_cp(("arbitrary",), VMEM_BIG),
    )


SMALL = ("b_ada", "g_pre", "qn_g", "kn_g", "w_dec_f", "w_dec_b", "gn_g", "g_post")


def _small_update(gathered, wmv):
    ns = len(SMALL)

    def body(*refs):
        gin_ref, gmid_ref, ggn_ref, gatt_ref, gl1_ref, gl2_ref = refs[:6]
        wmv_refs = refs[6:6 + 3 * ns]
        loss_ref = refs[6 + 3 * ns]
        out_refs = refs[7 + 3 * ns:]

        def dsum(ref, r=None):
            rows = slice(None) if r is None else slice(r, r + 1)
            acc = ref[0, rows, :]
            for d in range(1, NDEV):
                acc = acc + ref[d, rows, :]
            return acc

        s_lg = dsum(gl1_ref) + dsum(gl2_ref)
        loss_ref[...] = (0.5 / D) * jnp.sum(dsum(gmid_ref, 2), axis=-1, keepdims=True)
        eye = lax.broadcasted_iota(jnp.int32, (8, 128), 0) == lax.broadcasted_iota(jnp.int32, (8, 128), 1)
        dlg = jnp.sum(jnp.where(eye, s_lg, 0.0), axis=0, keepdims=True)
        w_f, w_b = wmv_refs[3 * SMALL.index("w_dec_f")][...], wmv_refs[3 * SMALL.index("w_dec_b")][...]
        s_q, s_k = dsum(gatt_ref, 0), dsum(gatt_ref, 1)
        grads = dict(
            b_ada=jnp.concatenate([dsum(gin_ref, 0), dsum(gin_ref, 1), dsum(gmid_ref, 0)], axis=1),
            g_pre=dsum(gin_ref, 2), g_post=dsum(gmid_ref, 1), gn_g=dsum(ggn_ref),
            qn_g=s_q[:, :DH] + s_q[:, DH:], kn_g=s_k[:, :DH] + s_k[:, DH:],
            w_dec_f=dlg[:, 0:HR] * _sigmoid(-w_f), w_dec_b=dlg[:, HR:2 * HR] * _sigmoid(-w_b))
        for i, nme in enumerate(SMALL):
            g = grads[nme]
            w_ref, m_ref, v_ref = wmv_refs[3 * i:3 * i + 3]
            g_ref, d_ref, nm_ref, nv_ref = out_refs[4 * i:4 * i + 4]
            g_ref[...] = g
            m2 = ADAM_B1 * m_ref[...] + (1.0 - ADAM_B1) * g
            v2 = ADAM_B2 * v_ref[...] + (1.0 - ADAM_B2) * jnp.square(g)
            m_hat = m2 / (1.0 - ADAM_B1 ** ADAM_STEP)
            v_hat = v2 / (1.0 - ADAM_B2 ** ADAM_STEP)
            d_ref[...] = -ADAM_LR * (m_hat / (jnp.sqrt(v_hat) + ADAM_EPS) + ADAM_WD * w_ref[...])
            nm_ref[...] = m2
            nv_ref[...] = v2

    out_shape = [jax.ShapeDtypeStruct((1, 1), F32)]
    for i in range(ns):
        out_shape += [jax.ShapeDtypeStruct(wmv[3 * i].shape, F32)] * 4
    return pl.pallas_call(body, name="small_update", out_shape=out_shape)(*gathered, *wmv)


def _adamw(parts, w, m, v, name):
    n, R, L = parts.shape
    tr = 256 if (R % 256 == 0 and R > 256) else R

    def body(p_ref, w_ref, m_ref, v_ref, g_ref, d_ref, nm_ref, nv_ref):
        g = p_ref[0].astype(F32)
        for k in range(1, n):
            g = g + p_ref[k].astype(F32)
        g_ref[...] = g
        m2 = ADAM_B1 * m_ref[...] + (1.0 - ADAM_B1) * g
        v2 = ADAM_B2 * v_ref[...] + (1.0 - ADAM_B2) * jnp.square(g)
        m_hat = m2 / (1.0 - ADAM_B1 ** ADAM_STEP)
        v_hat = v2 / (1.0 - ADAM_B2 ** ADAM_STEP)
        d_ref[...] = -ADAM_LR * (m_hat / (jnp.sqrt(v_hat) + ADAM_EPS) + ADAM_WD * w_ref[...])
        nm_ref[...] = m2
        nv_ref[...] = v2

    blk = pl.BlockSpec((tr, L), lambda i: (i, 0))
    o = jax.ShapeDtypeStruct((R, L), F32)
    return pl.pallas_call(
        body, name=name, out_shape=[o, o, o, o], grid=(R // tr,),
        in_specs=[pl.BlockSpec((n, tr, L), lambda i: (0, i, 0)), blk, blk, blk], out_specs=[blk, blk, blk, blk],
        compiler_params=_cp(("parallel",), VMEM_BIG),
    )(parts, w, m, v)


def _rope_tables(S):
    f = np.float32
    t = np.arange(S)
    row, col = (t // 64).astype(f), (t % 64).astype(f)
    half = DH // 2
    inv = np.power(f(ROPE_THETA), -np.arange(0, half, 2, dtype=f) / f(half)).astype(f)
    ar, ac = (row[:, None] * inv[None, :]).astype(f), (col[:, None] * inv[None, :]).astype(f)
    cos64 = np.concatenate([np.cos(ar), np.cos(ar), np.cos(ac), np.cos(ac)], axis=1).astype(f)
    sin64 = np.concatenate([-np.sin(ar), np.sin(ar), -np.sin(ac), np.sin(ac)], axis=1).astype(f)
    return jnp.asarray(np.tile(cos64, (1, 2))), jnp.asarray(np.tile(sin64, (1, 2)))


def _to_p_order(w_orig):
    return jnp.concatenate([w_orig[:, ORIG[n][0]:ORIG[n][1]] for n in P_ORDER], axis=1)


def _pad_lanes(v, n):
    return jnp.pad(v, ((0, 0), (0, n - v.shape[1])))


def kernel(x, c, w_ada, b_ada, g_pre, w_in, qn_g, kn_g, w_dec_f, w_dec_b, gn_g, w_pa, w_pr, w_out, g_post, loss_target, m_w_ada, m_b_ada, m_g_pre, m_w_in, m_qn_g, m_kn_g, m_w_dec_f, m_w_dec_b, m_gn_g, m_w_pa, m_w_pr, m_w_out, m_g_post, v_w_ada, v_b_ada, v_g_pre, v_w_in, v_qn_g, v_kn_g, v_w_dec_f, v_w_dec_b, v_gn_g, v_w_pa, v_w_pr, v_w_out, v_g_post):
    S = x.shape[1]
    me = 4 * lax.axis_index("x") + 2 * lax.axis_index("y") + lax.axis_index("c")
    xs, tgt = x[0], loss_target[0]
    ncol_ada = w_ada.shape[2]
    ncol_in = w_in.shape[2]

    b_ada_s = lax.dynamic_slice(b_ada, (0, me * ncol_ada), (1, ncol_ada))
    mod_all, c_act, (wg_in, wg_pa, wg_pr, wg_out) = _prologue(
        jnp.pad(c, ((0, 7), (0, 0))), w_ada[0], b_ada_s,
        [w_in[0].astype(BF16), w_pa[0].astype(BF16), w_pr[0].astype(BF16), w_out[0].astype(BF16)])
    mod = lax.dynamic_index_in_dim(mod_all, me, axis=1, keepdims=False).reshape(3, D)
    w_p = _to_p_order(wg_in.transpose(1, 0, 2).reshape(D, NDEV * ncol_in))
    w_pa_f = wg_pa.transpose(1, 0, 2).reshape(512, D)
    w_pr_f = wg_pr.transpose(1, 0, 2).reshape(512, D)
    w_out_f = wg_out.reshape(D, D)

    cos, sin = _rope_tables(S)
    qg, kg = jnp.tile(qn_g, (1, 2)), jnp.tile(kn_g, (1, 2))

    p, h = _fwd_in(xs, mod, g_pre, w_p)
    qt, kh, kt, vh, vta, qr2, kr2 = _prep(p, cos, sin, qg, kg)
    o_att, o_t, lse = _attn_fwd(qt, kh, vta)
    dc, qdf, qdb, kdf, kdb, adec = _ret_tables(w_dec_f, w_dec_b)
    rf, rb = _ret_states(kr2, p, kdf, kdb, adec)
    yr = _ret_out(qr2, kr2, p, rf, rb, dc, qdf, qdb, gn_g)

    dout, do, dpm, dyr, mb, dub, yab, dab, drb_, sums_mid = _mid(xs, tgt, mod, g_post, o_att, p, yr, w_pa_f, w_pr_f, w_out_f)
    gw_out = _mm_tn(mb, dub, "gw_out")
    gw_pa = _mm_tn(yab, dab, "gw_pa")
    gw_pr = _mm_tn(yr, drb_, "gw_pr")
    gi_m = _mm_tn(h, dpm, "gw_in_mid")

    def shards(cols, nd):
        return cols.astype(BF16).reshape(D, nd, ncol_in).transpose(1, 0, 2)

    all_dev = tuple(range(NDEV))
    (dqt, dkt, dvt), (rs_out, rs_pa, rs_pr, rs_in) = _attn_bwd(qt, kh, kt, vh, do, o_t, lse, [
        (gw_out.astype(BF16).reshape(NDEV, 128, D), all_dev, None),
        (gw_pa.astype(BF16).reshape(512, NDEV, 128).transpose(1, 0, 2), all_dev, None),
        (gw_pr.astype(BF16).reshape(512, NDEV, 128).transpose(1, 0, 2), all_dev, None),
        (shards(gi_m[:, 224:2048], 3), (5, 6, 7), None)])
    dpa, gs_att = _attn_prep_bwd(dqt, dkt, dvt, p, cos, sin, qg, kg)
    gi_a = _mm_tn(h, dpa, "gw_in_att")
    (dpra, dk_i, dv_i, drf, drb, dgn, dlg1), _ = _ret_bwd_chunk(qr2, kr2, p, rf, rb, dc, qdf, qdb, gn_g, dyr, cos, sin, [])
    dkf, dkb, dvf, dvb, dlg2 = _ret_bwd_scan(kr2, p, rf, rb, drf, drb, kdf, kdb, adec)
    dprb = _ret_bwd_final(dk_i, dkf, dkb, dv_i, dvf, dvb, cos, sin)
    gi_ra = _mm_tn(h, dpra, "gw_in_reta")
    gi_rb = _mm_tn(h, dprb, "gw_in_retb")
    (grad_x, sums_in), (rs_in,) = _bwd_in(dpm, dpa, dpra, dprb, w_p, xs, dout, mod, g_pre, [
        (shards(jnp.concatenate([gi_a, gi_m[:, 2048:2560], gi_ra[:, :256], gi_rb[:, 512:768], gi_rb[:, :512], gi_ra[:, 256:768],
                                 gi_m[:, :224]], axis=1), 5), (0, 1, 2, 3, 4), rs_in)])

    gathered = _small_allgather([sums_in, sums_mid, dgn, gs_att, dlg1, dlg2], "ag_small")
    given = dict(b_ada=(b_ada, m_b_ada, v_b_ada), g_pre=(g_pre, m_g_pre, v_g_pre), qn_g=(qn_g, m_qn_g, v_qn_g), kn_g=(kn_g, m_kn_g, v_kn_g),
                 w_dec_f=(w_dec_f, m_w_dec_f, v_w_dec_f), w_dec_b=(w_dec_b, m_w_dec_b, v_w_dec_b), gn_g=(gn_g, m_gn_g, v_gn_g),
                 g_post=(g_post, m_g_post, v_g_post))
    small = _small_update(gathered, [a for nme in SMALL for a in given[nme]])
    loss = small[0][0, 0]

    g_in_all, g_mid_all = gathered[0], gathered[1]
    dmod_all = lax.dynamic_slice(jnp.concatenate([g_in_all[:, 0, :], g_in_all[:, 1, :], g_mid_all[:, 0, :]], axis=1),
                                 (0, me * ncol_ada), (NDEV, ncol_ada))
    g_ada = _mm_tn(c_act, jnp.pad(dmod_all, ((0, 8), (0, 0))).astype(BF16), "gw_ada")

    res = dict(
        w_ada=_adamw(g_ada[None], w_ada[0], m_w_ada[0], v_w_ada[0], "adamw_ada"),
        w_in=_adamw(rs_in, w_in[0], m_w_in[0], v_w_in[0], "adamw_in"),
        w_pa=_adamw(rs_pa, w_pa[0], m_w_pa[0], v_w_pa[0], "adamw_pa"),
        w_pr=_adamw(rs_pr, w_pr[0], m_w_pr[0], v_w_pr[0], "adamw_pr"),
        w_out=_adamw(rs_out, w_out[0], m_w_out[0], v_w_out[0], "adamw_out"),
    )
    names = ["w_ada", "b_ada", "g_pre", "w_in", "qn_g", "kn_g", "w_dec_f", "w_dec_b", "gn_g", "w_pa", "w_pr", "w_out", "g_post"]
    outs = [[], [], [], []]
    for nme in names:
        for q in range(4):
            if nme in res:
                outs[q].append(res[nme][q][None])
            else:
                outs[q].append(small[1 + 4 * SMALL.index(nme) + q])
    return (loss, grad_x[None], *outs[0], *outs[1], *outs[2], *outs[3])
```

```python
import jax
import jax.numpy as jnp
import numpy as np
from jax import lax
from jax.experimental import pallas as pl
from jax.experimental.pallas import tpu as pltpu

F32, BF16 = jnp.float32, jnp.bfloat16
D = 1024
DH = 64
DHA = 80
DV = 128
LOG2E = 1.4426950408889634
LN2 = 0.6931471805599453
HR = 4
CH = 128
EPS = 1e-6
ROPE_THETA = 10000.0
NDEV = 8
O_GL, O_ZA, O_QA, O_KA, O_VA, O_QR, O_ZR, O_VR, O_KR, P_W = 0, 2048, 2560, 3072, 3200, 3328, 3584, 4096, 4608, 4864
ORIG = dict(qa=(0, 512), ka=(512, 640), va=(640, 768), za=(768, 1280), qr=(1280, 1536), kr=(1536, 1792),
            vr=(1792, 2304), zr=(2304, 2816), gl=(2816, 4864))
P_ORDER = ("gl", "za", "qa", "ka", "va", "qr", "zr", "vr", "kr")
ADAM_LR, ADAM_B1, ADAM_B2, ADAM_EPS, ADAM_WD, ADAM_STEP = 0.001, 0.9, 0.999, 1e-08, 0.01, 10
VMEM_BIG = 56 * 1024 * 1024
MESH = pl.DeviceIdType.MESH

NT = (((1,), (1,)), ((), ()))
TN = (((0,), (0,)), ((), ()))


def _dot(a, b, dims=None):
    if dims is None:
        return jnp.dot(a, b, preferred_element_type=F32)
    return lax.dot_general(a, b, dims, preferred_element_type=F32)


def _cp(sem=None, vmem=None):
    kw = {}
    if sem is not None:
        kw["dimension_semantics"] = sem
    if vmem is not None:
        kw["vmem_limit_bytes"] = vmem
    return pltpu.CompilerParams(**kw)


def _sigmoid(z):
    return 1.0 / (1.0 + jnp.exp(-z))


def _sum11(m):
    return jnp.sum(jnp.sum(m, axis=-1, keepdims=True), axis=0, keepdims=True)


def _full(shape):
    n = len(shape)
    return pl.BlockSpec(shape, lambda *_: (0,) * n)


def _my_pos():
    return lax.axis_index("x"), lax.axis_index("y"), lax.axis_index("c")


def _peer(k, x, y, c):
    return ((1 - x) if k & 4 else x, (1 - y) if k & 2 else y, (1 - c) if k & 1 else c)


def _small_allgather(vs, name):
    n = len(vs)

    def body(*refs):
        v_refs, out_refs = refs[:n], refs[n:2 * n]
        send_sems, recv_sems = refs[2 * n:]
        x, y, c = _my_pos()
        me = 4 * x + 2 * y + c
        cps = []
        for a in range(n):
            out_refs[a][me] = v_refs[a][...]
            for k in range(1, NDEV):
                cp = pltpu.make_async_remote_copy(src_ref=v_refs[a], dst_ref=out_refs[a].at[me], send_sem=send_sems.at[a, k - 1],
                                                  recv_sem=recv_sems.at[a, k - 1], device_id=_peer(k, x, y, c), device_id_type=MESH)
                cp.start()
                cps.append(cp)
        for cp in cps:
            cp.wait()

    vm = pl.BlockSpec(memory_space=pltpu.VMEM)
    return pl.pallas_call(
        body, name=name, out_shape=[jax.ShapeDtypeStruct((NDEV,) + v.shape, v.dtype) for v in vs],
        in_specs=[vm] * n, out_specs=[vm] * n,
        scratch_shapes=[pltpu.SemaphoreType.DMA((n, NDEV - 1)), pltpu.SemaphoreType.DMA((n, NDEV - 1))],
    )(*vs)


def _prologue(c8, w_ada_s, b_ada_s, arrs):
    n = len(arrs)
    ncol = w_ada_s.shape[1]

    def body(*refs):
        c_ref, wa_ref, ba_ref = refs[:3]
        ins = refs[3:3 + n]
        mod_ref, cact_ref = refs[3 + n:5 + n]
        outs = refs[5 + n:5 + 2 * n]
        call_ref, send_sems, recv_sems, local_sems, s_send, s_recv = refs[5 + 2 * n:]
        x, y, c = _my_pos()
        me, sibling = (x, y, c), (x, y, 1 - c)
        chips = [(1 - x, y), (x, 1 - y), (1 - x, 1 - y)]
        me_i = 4 * x + 2 * y + c

        def small_gather(src_ref, dst_ref, row):
            cps = []
            for k in range(1, NDEV):
                cp = pltpu.make_async_remote_copy(src_ref=src_ref, dst_ref=dst_ref.at[me_i], send_sem=s_send.at[row, k - 1],
                                                  recv_sem=s_recv.at[row, k - 1], device_id=_peer(k, x, y, c), device_id_type=MESH)
                cp.start()
                cps.append(cp)
            return cps

        def blk(a, px, py, pc):
            return outs[a].at[4 * px + 2 * py + pc]

        def copy(a, k, block, to, src=None):
            return pltpu.make_async_remote_copy(src_ref=blk(a, *block) if src is None else src, dst_ref=blk(a, *block),
                                                send_sem=send_sems.at[a, k], recv_sem=recv_sems.at[a, k], device_id=to, device_id_type=MESH)

        call_ref[me_i] = c_ref[...]
        for cp in small_gather(c_ref, call_ref, 0):
            cp.wait()

        local, sent = [], []
        for a in range(n):
            mine = pltpu.make_async_copy(ins[a], blk(a, *me), local_sems.at[a])
            mine.start()
            local.append(mine)
            first = [copy(a, 0, me, sibling, src=ins[a])] + [copy(a, 1 + j, me, (*chip, c), src=ins[a]) for j, chip in enumerate(chips)]
            for cp in first:
                cp.start()
            sent += first

        cv = call_ref[:, 0, :]
        ca = jnp.concatenate([cv * _sigmoid(cv), jnp.zeros_like(cv)], axis=0).astype(BF16)
        cact_ref[...] = ca
        mod_ref[me_i] = (_dot(ca, wa_ref[...].astype(BF16)) + ba_ref[...])[:8]
        mod_copies = small_gather(mod_ref.at[me_i], mod_ref, 1)

        for j, chip in enumerate(chips):
            for a in range(n):
                copy(a, 1 + j, (*chip, c), me).wait_recv()
                cp = copy(a, 4 + j, (*chip, c), sibling)
                cp.start()
                sent.append(cp)
        for a in range(n):
            copy(a, 0, sibling, me).wait_recv()
            for j, chip in enumerate(chips):
                copy(a, 4 + j, (*chip, 1 - c), me).wait_recv()
        for cp in sent:
            cp.wait_send()
        for cp in local + mod_copies:
            cp.wait()

    vm, hbm = pl.BlockSpec(memory_space=pltpu.VMEM), pl.BlockSpec(memory_space=pl.ANY)
    res = pl.pallas_call(
        body, name="prologue",
        out_shape=[jax.ShapeDtypeStruct((NDEV, 8, ncol), F32), jax.ShapeDtypeStruct((16, D), BF16)]
        + [jax.ShapeDtypeStruct((NDEV,) + a.shape, a.dtype) for a in arrs],
        in_specs=[vm, vm, vm] + [hbm] * n, out_specs=[vm, vm] + [hbm] * n,
        scratch_shapes=[pltpu.VMEM((NDEV, 8, D), F32), pltpu.SemaphoreType.DMA((n, NDEV - 1)), pltpu.SemaphoreType.DMA((n, NDEV - 1)),
                        pltpu.SemaphoreType.DMA((n,)), pltpu.SemaphoreType.DMA((2, NDEV - 1)), pltpu.SemaphoreType.DMA((2, NDEV - 1))],
    )(c8, w_ada_s, b_ada_s, *arrs)
    return res[0], res[1], res[2:]


def _in_set(idx, dests):
    p = idx == dests[0]
    for d in dests[1:]:
        p = jnp.logical_or(p, idx == d)
    return p


def _host_call(body, xs, *, name, grid, in_specs, out_specs, out_shape, scratch_shapes, operands, compiler_params):
    nx, nin, nout, nscr = len(xs), len(operands), len(out_shape), len(scratch_shapes)
    if nx == 0:
        res = pl.pallas_call(body, name=name, grid=grid, in_specs=in_specs, out_specs=out_specs, out_shape=out_shape,
                             scratch_shapes=scratch_shapes, compiler_params=compiler_params)(*operands)
        return res, []
    ops, specs, aliases = list(operands), list(in_specs), {}
    oshape, ospecs = list(out_shape), list(out_specs)
    any_spec = pl.BlockSpec(memory_space=pl.ANY)
    for a, (send, dests, recv) in enumerate(xs):
        ops.append(send)
        specs.append(any_spec)
        if recv is not None:
            aliases[len(ops)] = nout + a
            ops.append(recv)
            specs.append(any_spec)
            oshape.append(jax.ShapeDtypeStruct(recv.shape, recv.dtype))
        else:
            oshape.append(jax.ShapeDtypeStruct((NDEV,) + send.shape[1:], send.dtype))
        ospecs.append(any_spec)
    ntot_in = len(ops)

    def wrapped(*refs):
        host_in = refs[:nin]
        sends, pos = [], nin
        for (_, _, recv) in xs:
            sends.append(refs[pos])
            pos += 1 if recv is None else 2
        host_out = refs[ntot_in:ntot_in + nout]
        recvs = refs[ntot_in + nout:ntot_in + nout + nx]
        host_scr = refs[ntot_in + nout + nx:ntot_in + nout + nx + nscr]
        send_sems, recv_sems, local_sems = refs[ntot_in + nout + nx + nscr:]
        first = pl.program_id(0) == 0
        last = pl.program_id(0) == grid[0] - 1
        for ax in range(1, len(grid)):
            first = jnp.logical_and(first, pl.program_id(ax) == 0)
            last = jnp.logical_and(last, pl.program_id(ax) == grid[ax] - 1)
        x, y, c = _my_pos()
        me = 4 * x + 2 * y + c

        def each(fn_remote, fn_local):
            for a, (_, dests, _) in enumerate(xs):
                lo, nd = dests[0], len(dests)
                for k in range(1, NDEV):
                    px, py, pc = _peer(k, x, y, c)
                    pidx = 4 * px + 2 * py + pc
                    cp = pltpu.make_async_remote_copy(src_ref=sends[a].at[jnp.clip(pidx - lo, 0, nd - 1)], dst_ref=recvs[a].at[me],
                                                      send_sem=send_sems.at[a, k - 1], recv_sem=recv_sems.at[a, k - 1],
                                                      device_id=(px, py, pc), device_id_type=MESH)
                    fn_remote(cp, _in_set(pidx, dests), _in_set(me, dests))
                lc = pltpu.make_async_copy(sends[a].at[jnp.clip(me - lo, 0, nd - 1)], recvs[a].at[me], local_sems.at[a])
                fn_local(lc, _in_set(me, dests))

        def start_remote(cp, to_dest, _):
            pl.when(jnp.logical_and(first, to_dest))(cp.start)

        def start_local(lc, i_am_dest):
            pl.when(jnp.logical_and(first, i_am_dest))(lc.start)

        def wait_remote(cp, to_dest, i_am_dest):
            pl.when(jnp.logical_and(last, to_dest))(cp.wait_send)
            pl.when(jnp.logical_and(last, i_am_dest))(cp.wait_recv)

        def wait_local(lc, i_am_dest):
            pl.when(jnp.logical_and(last, i_am_dest))(lc.wait)

        each(start_remote, start_local)
        body(*host_in, *host_out, *host_scr)
        each(wait_remote, wait_local)

    res = pl.pallas_call(
        wrapped, name=name, grid=grid, in_specs=specs, out_specs=ospecs, out_shape=oshape, input_output_aliases=aliases,
        scratch_shapes=list(scratch_shapes) + [pltpu.SemaphoreType.DMA((nx, NDEV - 1)), pltpu.SemaphoreType.DMA((nx, NDEV - 1)),
                                               pltpu.SemaphoreType.DMA((nx,))],
        compiler_params=compiler_params,
    )(*ops)
    return res[:nout], res[nout:]


def _mm_tn(a, b, name):
    S, M = a.shape
    N = b.shape[1]
    tk = min(2048, S)
    tn = N if N <= 768 else (640 if N % 640 == 0 else 512)
    nk = S // tk

    def body(a_ref, b_ref, o_ref):
        @pl.when(pl.program_id(1) == 0)
        def _():
            o_ref[...] = jnp.zeros_like(o_ref)
        o_ref[...] += _dot(a_ref[...], b_ref[...], TN)

    return pl.pallas_call(
        body, name=name, out_shape=jax.ShapeDtypeStruct((M, N), F32), grid=(N // tn, nk),
        in_specs=[pl.BlockSpec((tk, M), lambda j, k: (k, 0)), pl.BlockSpec((tk, tn), lambda j, k: (k, j))],
        out_specs=pl.BlockSpec((M, tn), lambda j, k: (0, j)),
        compiler_params=_cp(("parallel", "arbitrary"), VMEM_BIG),
    )(a, b)


def _fwd_in(x, mod, g_pre, w_p):
    S = x.shape[0]
    tm = min(512, S)

    def body(x_ref, mod_ref, g_ref, w_ref, p_ref, h_ref):
        xv = x_ref[...]
        r = lax.rsqrt(jnp.mean(xv * xv, axis=-1, keepdims=True) + EPS)
        h = (((xv * r) * g_ref[...]) * (1.0 + mod_ref[1:2, :]) + mod_ref[0:1, :]).astype(BF16)
        h_ref[...] = h
        p_ref[...] = _dot(h, w_ref[...]).astype(BF16)

    return pl.pallas_call(
        body, name="fwd_in", out_shape=[jax.ShapeDtypeStruct((S, P_W), BF16), jax.ShapeDtypeStruct((S, D), BF16)],
        grid=(S // tm,),
        in_specs=[pl.BlockSpec((tm, D), lambda i: (i, 0)), _full((3, D)), _full((1, D)), _full((D, P_W))],
        out_specs=[pl.BlockSpec((tm, P_W), lambda i: (i, 0)), pl.BlockSpec((tm, D), lambda i: (i, 0))],
        compiler_params=_cp(("parallel",), VMEM_BIG),
    )(x, mod, g_pre, w_p)


def _swap16(v):
    lane = lax.broadcasted_iota(jnp.int32, v.shape, 1)
    return jnp.where((lane % 32) < 16, pltpu.roll(v, 112, 1), pltpu.roll(v, 16, 1))


def _rope(v, cos, sin):
    return v * cos + _swap16(v) * sin


def _rope_t(v, cos, sin):
    return v * cos - _swap16(v) * sin


def _head_mean(v):
    lo = lax.broadcasted_iota(jnp.int32, v.shape, 1) < 64
    m0 = jnp.sum(jnp.where(lo, v, 0.0), axis=-1, keepdims=True)
    m1 = jnp.sum(jnp.where(lo, 0.0, v), axis=-1, keepdims=True)
    return jnp.where(lo, m0, m1) * (1.0 / 64.0)


def _prep(p, cos, sin, qg, kg):
    S = p.shape[0]
    tm = min(512, S)

    def body(qa_ref, kv_ref, qr_ref, kr_ref, cos_ref, sin_ref, qg_ref, kg_ref, qt_ref, kh_ref, kt_ref, vh_ref, vta_ref, qr2_ref, kr2_ref):
        cos_v, sin_v = cos_ref[...], sin_ref[...]
        for g in range(4):
            xv = qa_ref[:, 128 * g:128 * g + 128].astype(F32)
            r = lax.rsqrt(_head_mean(xv * xv) + EPS)
            yt = (_rope((xv * r) * qg_ref[...], cos_v, sin_v) * (0.125 * LOG2E)).T
            qt_ref[2 * g] = yt[:DH].astype(BF16)
            qt_ref[2 * g + 1] = yt[DH:].astype(BF16)
        xv = kv_ref[:, :128].astype(F32)
        r = lax.rsqrt(_head_mean(xv * xv) + EPS)
        yv = _rope((xv * r) * kg_ref[...], cos_v, sin_v)
        kh_ref[0] = yv[:, :64].astype(BF16)
        kh_ref[1] = yv[:, 64:].astype(BF16)
        yt = yv.T
        kt_ref[0] = yt[:DH].astype(BF16)
        kt_ref[1] = yt[DH:].astype(BF16)
        vv = kv_ref[:, 128:].astype(F32)
        vh_ref[0] = vv[:, :64].astype(BF16)
        vh_ref[1] = vv[:, 64:].astype(BF16)
        vt = vv.T
        tail = (lax.broadcasted_iota(jnp.int32, (DHA - DH, tm), 0) == 0).astype(BF16)
        for kvh in range(2):
            vta_ref[kvh, 0:DH, :] = vt[DH * kvh:DH * kvh + DH].astype(BF16)
            vta_ref[kvh, DH:DHA, :] = tail
        for g in range(2):
            sl = slice(128 * g, 128 * g + 128)
            qr2_ref[:, sl] = _rope(qr_ref[:, sl].astype(F32), cos_v, sin_v)
            kr2_ref[:, sl] = _rope(kr_ref[:, sl].astype(F32), cos_v, sin_v) * 0.125

    hm = lambda n: pl.BlockSpec((n, tm, DH), lambda i: (0, i, 0))
    ht = lambda n, r: pl.BlockSpec((n, r, tm), lambda i: (0, 0, i))
    return pl.pallas_call(
        body, name="prep",
        out_shape=[jax.ShapeDtypeStruct((8, DH, S), BF16), jax.ShapeDtypeStruct((2, S, DH), BF16), jax.ShapeDtypeStruct((2, DH, S), BF16),
                   jax.ShapeDtypeStruct((2, S, DH), BF16), jax.ShapeDtypeStruct((2, DHA, S), BF16),
                   jax.ShapeDtypeStruct((S, 256), F32), jax.ShapeDtypeStruct((S, 256), F32)],
        grid=(S // tm,),
        in_specs=[pl.BlockSpec((tm, 512), lambda i: (i, O_QA // 512)), pl.BlockSpec((tm, 256), lambda i: (i, O_KA // 256)),
                  pl.BlockSpec((tm, 256), lambda i: (i, O_QR // 256)), pl.BlockSpec((tm, 256), lambda i: (i, O_KR // 256)),
                  pl.BlockSpec((tm, 128), lambda i: (i, 0)), pl.BlockSpec((tm, 128), lambda i: (i, 0)), _full((1, 128)), _full((1, 128))],
        out_specs=[ht(8, DH), hm(2), ht(2, DH), hm(2), ht(2, DHA), pl.BlockSpec((tm, 256), lambda i: (i, 0)), pl.BlockSpec((tm, 256), lambda i: (i, 0))],
        compiler_params=_cp(("parallel",)),
    )(p, p, p, p, cos, sin, qg, kg)


def _attn_fwd(qt, kh, vta):
    S = qt.shape[2]
    tq, tk = min(1024, S), min(512, S)
    nj = S // tk

    def body(q_ref, k_ref, v_ref, o_ref, ot_ref, lse_ref, m_s, acc_s):
        j = pl.program_id(1)

        @pl.when(j == 0)
        def _():
            m_s[...] = jnp.full_like(m_s, -jnp.inf)
            acc_s[...] = jnp.zeros_like(acc_s)

        m_all = m_s[...]
        st = {0: _dot(k_ref[0], q_ref[0])}
        m_new, acc_new = [], []
        for h in range(8):
            if h + 1 < 8:
                st[h + 1] = _dot(k_ref[(h + 1) // 4], q_ref[h + 1])
            m_old = m_all[h:h + 1, :]
            mn = jnp.maximum(m_old, jnp.max(st[h], axis=0, keepdims=True))
            pt = jnp.exp2(st[h] - mn).astype(BF16)
            acc_new.append(jnp.exp2(m_old - mn) * acc_s[h] + _dot(v_ref[h // 4], pt))
            m_new.append(mn)
            del st[h]
        for h in range(8):
            acc_s[h] = acc_new[h]
            m_s[h:h + 1, :] = m_new[h]

        @pl.when(j == nj - 1)
        def _():
            for h in range(8):
                ot = acc_s[h, 0:DH, :] / acc_s[h, DH:DH + 1, :]
                ot_ref[h] = ot
                o_ref[:, DH * h:DH * h + DH] = ot.T
                lse_ref[h // 4, h % 4:h % 4 + 1, :] = m_s[h:h + 1, :] + jnp.log2(acc_s[h, DH:DH + 1, :])

    return pl.pallas_call(
        body, name="attn_fwd",
        out_shape=[jax.ShapeDtypeStruct((S, 512), F32), jax.ShapeDtypeStruct((8, DH, S), F32), jax.ShapeDtypeStruct((2, 4, S), F32)],
        grid=(S // tq, nj),
        in_specs=[pl.BlockSpec((8, DH, tq), lambda i, j: (0, 0, i)), pl.BlockSpec((2, tk, DH), lambda i, j: (0, j, 0)),
                  pl.BlockSpec((2, DHA, tk), lambda i, j: (0, 0, j))],
        out_specs=[pl.BlockSpec((tq, 512), lambda i, j: (i, 0)), pl.BlockSpec((8, DH, tq), lambda i, j: (0, 0, i)),
                   pl.BlockSpec((2, 4, tq), lambda i, j: (0, 0, i))],
        scratch_shapes=[pltpu.VMEM((8, tq), F32), pltpu.VMEM((8, DHA, tq), F32)],
        compiler_params=_cp(("parallel", "arbitrary"), VMEM_BIG),
    )(qt, kh, vta)


def _ret_tables(wf, wb):
    C = CH

    def body(wf_ref, wb_ref, dc_ref, qdf_ref, qdb_ref, kdf_ref, kdb_ref, a_ref):
        def logsig(w):
            z = jnp.exp(-jnp.abs(w))
            u = 1.0 + z
            l1p = jnp.where(u == 1.0, z, jnp.log(u) * (z / jnp.where(u == 1.0, 1.0, u - 1.0)))
            return jnp.minimum(w, 0.0) - l1p

        lgf, lgb = logsig(wf_ref[...]), logsig(wb_ref[...])
        lane4 = lax.broadcasted_iota(jnp.int32, (1, 4), 1)

        def pick(lg, h):
            return jnp.sum(jnp.where(lane4 == h, lg, 0.0), axis=-1, keepdims=True)

        ii = lax.broadcasted_iota(jnp.int32, (C, C), 0).astype(F32)
        jj = lax.broadcasted_iota(jnp.int32, (C, C), 1).astype(F32)
        dif = ii - jj
        hd = lax.broadcasted_iota(jnp.int32, (C, 256), 1) // DH
        lf_l = jnp.zeros((C, 256), F32)
        lb_l = jnp.zeros((C, 256), F32)
        for h in range(HR):
            lf, lb = pick(lgf, h), pick(lgb, h)
            dc_ref[h] = jnp.where(dif >= 0, jnp.exp(lf * jnp.maximum(dif, 0.0)), jnp.exp(lb * jnp.maximum(-dif, 0.0)))
            lf_l = jnp.where(hd == h, lf, lf_l)
            lb_l = jnp.where(hd == h, lb, lb_l)
            a_ref[h:h + 1, :] = jnp.broadcast_to(jnp.exp(lf * C), (1, 128))
            a_ref[HR + h:HR + h + 1, :] = jnp.broadcast_to(jnp.exp(lb * C), (1, 128))
        ri = lax.broadcasted_iota(jnp.int32, (C, 256), 0).astype(F32)
        qdf_ref[...] = jnp.exp(lf_l * (ri + 1.0))
        qdb_ref[...] = jnp.exp(lb_l * (C - ri))
        kdf_ref[...] = jnp.exp(lf_l * (C - 1.0 - ri))
        kdb_ref[...] = jnp.exp(lb_l * ri)

    t = jax.ShapeDtypeStruct((C, 256), F32)
    return pl.pallas_call(body, name="ret_tables",
                          out_shape=[jax.ShapeDtypeStruct((HR, C, C), F32), t, t, t, t, jax.ShapeDtypeStruct((8, 128), F32)])(wf, wb)


def _ret_states(kr2, p, kdf, kdb, adec):
    S = kr2.shape[0]
    C, N = CH, S // CH

    def body(kf_ref, vf_ref, kb_ref, vb_ref, kdf_ref, kdb_ref, a_ref, rf_ref, rb_ref, sf, sb):
        @pl.when(pl.program_id(0) == 0)
        def _():
            sf[...] = jnp.zeros_like(sf)
            sb[...] = jnp.zeros_like(sb)

        rf_ref[0] = sf[...]
        rb_ref[0] = sb[...]
        kdfw = (kf_ref[...] * kdf_ref[...]).astype(BF16)
        kdbw = (kb_ref[...] * kdb_ref[...]).astype(BF16)
        vf, vb = vf_ref[...].astype(BF16), vb_ref[...].astype(BF16)
        kvf = [_dot(kdfw[:, _ks(h)], vf[:, _vs(h)], TN) for h in range(HR)]
        kvb = [_dot(kdbw[:, _ks(h)], vb[:, _vs(h)], TN) for h in range(HR)]
        for h in range(HR):
            sf[h] = a_ref[h:h + 1, :] * sf[h] + kvf[h]
            sb[h] = a_ref[HR + h:HR + h + 1, :] * sb[h] + kvb[h]

    st = jax.ShapeDtypeStruct((N, HR, DH, DV), F32)
    return pl.pallas_call(
        body, name="ret_states", out_shape=[st, st], grid=(N,),
        in_specs=[pl.BlockSpec((C, 256), lambda t: (t, 0)), pl.BlockSpec((C, 512), lambda t: (t, O_VR // 512)),
                  pl.BlockSpec((C, 256), lambda t: (N - 1 - t, 0)), pl.BlockSpec((C, 512), lambda t: (N - 1 - t, O_VR // 512)),
                  _full((C, 256)), _full((C, 256)), _full((8, 128))],
        out_specs=[pl.BlockSpec((1, HR, DH, DV), lambda t: (t, 0, 0, 0)), pl.BlockSpec((1, HR, DH, DV), lambda t: (N - 1 - t, 0, 0, 0))],
        scratch_shapes=[pltpu.VMEM((HR, DH, DV), F32), pltpu.VMEM((HR, DH, DV), F32)],
        compiler_params=_cp(("arbitrary",)),
    )(kr2, p, kr2, p, kdf, kdb, adec)


def _ks(h):
    return slice(DH * h, DH * h + DH)


def _vs(h):
    return slice(DV * h, DV * h + DV)


def _ret_heads_fwd(qb, kb, vb, qfw, qbw, dc_ref, rf_ref, rb_ref):
    hs = range(HR)
    s = [_dot(qb[:, _ks(h)], kb[:, _ks(h)], NT) for h in hs]
    inter = [_dot(qfw[:, _ks(h)], rf_ref[0, h].astype(BF16)) + _dot(qbw[:, _ks(h)], rb_ref[0, h].astype(BF16)) for h in hs]
    sd = [s[h] * dc_ref[h] for h in hs]
    o = [_dot(sd[h].astype(BF16), vb[:, _vs(h)]) + inter[h] for h in hs]
    return sd, o


def _ret_out(qr2, kr2, p, rf, rb, dc, qdf, qdb, gn):
    S = qr2.shape[0]
    C, N = CH, S // CH

    def body(q_ref, k_ref, v_ref, z_ref, rf_ref, rb_ref, dc_ref, qdf_ref, qdb_ref, gn_ref, yr_ref):
        qv = q_ref[...]
        qb, kb, vb = qv.astype(BF16), k_ref[...].astype(BF16), v_ref[...].astype(BF16)
        qfw, qbw = (qv * qdf_ref[...]).astype(BF16), (qv * qdb_ref[...]).astype(BF16)
        _, o = _ret_heads_fwd(qb, kb, vb, qfw, qbw, dc_ref, rf_ref, rb_ref)
        for h in range(HR):
            vs = _vs(h)
            mu = jnp.mean(o[h], axis=-1, keepdims=True)
            var = jnp.mean(jnp.square(o[h] - mu), axis=-1, keepdims=True)
            on = (o[h] - mu) * lax.rsqrt(var + EPS)
            z = z_ref[:, vs].astype(F32)
            yr_ref[:, vs] = ((on * gn_ref[:, vs]) * (z * _sigmoid(z))).astype(BF16)

    return pl.pallas_call(
        body, name="ret_out", out_shape=jax.ShapeDtypeStruct((S, 512), BF16), grid=(N,),
        in_specs=[pl.BlockSpec((C, 256), lambda t: (t, 0)), pl.BlockSpec((C, 256), lambda t: (t, 0)),
                  pl.BlockSpec((C, 512), lambda t: (t, O_VR // 512)), pl.BlockSpec((C, 512), lambda t: (t, O_ZR // 512)),
                  pl.BlockSpec((1, HR, DH, DV), lambda t: (t, 0, 0, 0)), pl.BlockSpec((1, HR, DH, DV), lambda t: (t, 0, 0, 0)),
                  _full((HR, C, C)), _full((C, 256)), _full((C, 256)), _full((1, 512))],
        out_specs=pl.BlockSpec((C, 512), lambda t: (t, 0)),
        compiler_params=_cp(("parallel",)),
    )(qr2, kr2, p, p, rf, rb, dc, qdf, qdb, gn)


def _mid(x, tgt, mod, g_post, o_att, p, yr, w_pa, w_pr, w_out):
    S = x.shape[0]
    tm = min(256, S)

    def body(x_ref, t_ref, mod_ref, gp_ref, o_ref, za_ref, gl_ref, yr_ref, wpa_ref, wpr_ref, wout_ref,
             dout_ref, do_ref, dpm_ref, dyr_ref, mb_ref, dub_ref, yab_ref, dab_ref, drb_ref, sums_ref):
        @pl.when(pl.program_id(0) == 0)
        def _():
            sums_ref[...] = jnp.zeros_like(sums_ref)

        za = za_ref[...].astype(F32)
        sa = _sigmoid(za)
        sil = za * sa
        ov = o_ref[...]
        ya_b = (ov * sil).astype(BF16)
        yr_b = yr_ref[...]
        av = _dot(ya_b, wpa_ref[...])
        rv = _dot(yr_b, wpr_ref[...])
        ga = _sigmoid(gl_ref[:, :D].astype(F32))
        gr = _sigmoid(gl_ref[:, D:].astype(F32))
        mb = (ga * av + gr * rv).astype(BF16)
        u = _dot(mb, wout_ref[...])
        r2 = lax.rsqrt(jnp.mean(u * u, axis=-1, keepdims=True) + EPS)
        un = u * r2
        gp = gp_ref[...]
        yv = un * gp
        gate = mod_ref[2:3, :]
        err = (x_ref[...] + gate * yv) - t_ref[...]
        dout = err * (1.0 / D)
        dout_ref[...] = dout
        dy = dout * gate
        sums_ref[0:1, :] += jnp.sum(dout * yv, axis=0, keepdims=True)
        sums_ref[1:2, :] += jnp.sum(dy * un, axis=0, keepdims=True)
        sums_ref[2:3, :] += jnp.sum(err * err, axis=0, keepdims=True)
        dyg = dy * gp
        du_b = (r2 * (dyg - un * jnp.mean(dyg * un, axis=-1, keepdims=True))).astype(BF16)
        dm = _dot(du_b, wout_ref[...], NT)
        da_b = (dm * ga).astype(BF16)
        dr_b = (dm * gr).astype(BF16)
        dpm_ref[:, :D] = (dm * av * (ga * (1.0 - ga))).astype(BF16)
        dpm_ref[:, D:2 * D] = (dm * rv * (gr * (1.0 - gr))).astype(BF16)
        dya = _dot(da_b, wpa_ref[...], NT)
        dyr_ref[...] = _dot(dr_b, wpr_ref[...], NT)
        dov = dya * sil
        for g in range(4):
            dt = dov[:, 128 * g:128 * g + 128].T
            do_ref[2 * g] = dt[:DH].astype(BF16)
            do_ref[2 * g + 1] = dt[DH:].astype(BF16)
        dpm_ref[:, 2 * D:] = (dya * ov * (sa * (1.0 + za * (1.0 - sa)))).astype(BF16)
        mb_ref[...] = mb
        dub_ref[...] = du_b
        yab_ref[...] = ya_b
        dab_ref[...] = da_b
        drb_ref[...] = dr_b

    row = lambda w: pl.BlockSpec((tm, w), lambda i: (i, 0))
    sd = lambda w, dt: jax.ShapeDtypeStruct((S, w), dt)
    return pl.pallas_call(
        body, name="mid",
        out_shape=[sd(D, F32), jax.ShapeDtypeStruct((8, DH, S), BF16), sd(2560, BF16), sd(512, F32), sd(D, BF16), sd(D, BF16), sd(512, BF16),
                   sd(D, BF16), sd(D, BF16), jax.ShapeDtypeStruct((8, D), F32)],
        grid=(S // tm,),
        in_specs=[row(D), row(D), _full((3, D)), _full((1, D)), row(512), pl.BlockSpec((tm, 512), lambda i: (i, O_ZA // 512)),
                  pl.BlockSpec((tm, 2048), lambda i: (i, 0)), row(512), _full((512, D)), _full((512, D)), _full((D, D))],
        out_specs=[row(D), pl.BlockSpec((8, DH, tm), lambda i: (0, 0, i)), row(2560), row(512), row(D), row(D), row(512), row(D), row(D),
                   _full((8, D))],
        compiler_params=_cp(("arbitrary",), VMEM_BIG),
    )(x, tgt, mod, g_post, o_att, p, p, yr, w_pa, w_pr, w_out)


def _attn_bwd(qt, kh, kt, vh, dot_, ot, lse, xs):
    S = qt.shape[2]
    tq, tk = min(512, S), min(1024, S)

    def body(q_ref, k_ref, kt_ref, v_ref, do_ref, o_ref, lse_ref, dq_ref, dk_ref, dv_ref):
        j, i = pl.program_id(0), pl.program_id(1)
        cols = pl.ds(pl.multiple_of(i * tq, tq), tq)
        st = {0: _dot(k_ref[0], q_ref[0])}
        dpt = {0: _dot(v_ref[0], do_ref[0])}
        dk_acc, dv_acc, dqs = [None, None], [None, None], []
        for h in range(8):
            g = h // 4
            if h + 1 < 8:
                st[h + 1] = _dot(k_ref[(h + 1) // 4], q_ref[h + 1])
                dpt[h + 1] = _dot(v_ref[(h + 1) // 4], do_ref[h + 1])
            qt_h, dot_h = q_ref[h], do_ref[h]
            delta = jnp.sum(dot_h.astype(F32) * o_ref[h], axis=0, keepdims=True)
            pt = jnp.exp2(st[h] - lse_ref[g, h % 4:h % 4 + 1, :])
            dst = (pt * (dpt[h] - delta)).astype(BF16)
            dv_h = _dot(dot_h, pt.astype(BF16), NT)
            dk_h = _dot(qt_h, dst, NT)
            dqs.append(_dot(kt_ref[g], dst))
            dv_acc[g] = dv_h if dv_acc[g] is None else dv_acc[g] + dv_h
            dk_acc[g] = dk_h if dk_acc[g] is None else dk_acc[g] + dk_h
            del st[h], dpt[h]

        @pl.when(i == 0)
        def _():
            for g in range(2):
                dk_ref[g] = dk_acc[g]
                dv_ref[g] = dv_acc[g]

        @pl.when(i > 0)
        def _():
            for g in range(2):
                dk_ref[g] += dk_acc[g]
                dv_ref[g] += dv_acc[g]

        @pl.when(j == 0)
        def _():
            for h in range(8):
                dq_ref[h, :, cols] = dqs[h]

        @pl.when(j > 0)
        def _():
            for h in range(8):
                dq_ref[h, :, cols] += dqs[h]

    return _host_call(
        body, xs, name="attn_bwd",
        out_shape=[jax.ShapeDtypeStruct((8, DH, S), F32), jax.ShapeDtypeStruct((2, DH, S), F32), jax.ShapeDtypeStruct((2, DH, S), F32)],
        grid=(S // tk, S // tq),
        in_specs=[pl.BlockSpec((8, DH, tq), lambda j, i: (0, 0, i)), pl.BlockSpec((2, tk, DH), lambda j, i: (0, j, 0)),
                  pl.BlockSpec((2, DH, tk), lambda j, i: (0, 0, j)), pl.BlockSpec((2, tk, DH), lambda j, i: (0, j, 0)),
                  pl.BlockSpec((8, DH, tq), lambda j, i: (0, 0, i)), pl.BlockSpec((8, DH, tq), lambda j, i: (0, 0, i)),
                  pl.BlockSpec((2, 4, tq), lambda j, i: (0, 0, i))],
        out_specs=[pl.BlockSpec((8, DH, S), lambda j, i: (0, 0, 0)), pl.BlockSpec((2, DH, tk), lambda j, i: (0, 0, j)),
                   pl.BlockSpec((2, DH, tk), lambda j, i: (0, 0, j))],
        scratch_shapes=[], operands=(qt, kh, kt, vh, dot_, ot, lse),
        compiler_params=_cp(("arbitrary", "arbitrary"), VMEM_BIG),
    )


def _attn_prep_bwd(dqt, dkt, dvt, p, cos, sin, qg, kg):
    S = dqt.shape[2]
    tm = min(512, S)

    def body(dq_ref, dk_ref, dv_ref, qa_ref, ka_ref, cos_ref, sin_ref, qg_ref, kg_ref, dp_ref, gs_ref):
        @pl.when(pl.program_id(0) == 0)
        def _():
            gs_ref[...] = jnp.zeros_like(gs_ref)

        cos_v, sin_v = cos_ref[...], sin_ref[...]

        def pair(ref, a):
            return jnp.concatenate([ref[a], ref[a + 1]], axis=0).T

        def norm_bwd(dyv, xv, gv, row):
            r = lax.rsqrt(_head_mean(xv * xv) + EPS)
            xn = xv * r
            dxh = _rope_t(dyv, cos_v, sin_v)
            gs_ref[row:row + 1, :] += jnp.sum(dxh * xn, axis=0, keepdims=True)
            dg = dxh * gv
            return r * (dg - xn * _head_mean(dg * xn))

        for g in range(4):
            sl = slice(128 * g, 128 * g + 128)
            dp_ref[:, sl] = norm_bwd(pair(dq_ref, 2 * g) * 0.125, qa_ref[:, sl].astype(F32), qg_ref[...], 0).astype(BF16)
        dp_ref[:, 512:640] = norm_bwd(pair(dk_ref, 0) * LN2, ka_ref[...].astype(F32), kg_ref[...], 1).astype(BF16)
        dp_ref[:, 640:768] = pair(dv_ref, 0).astype(BF16)

    ht = lambda n: pl.BlockSpec((n, DH, tm), lambda i: (0, 0, i))
    return pl.pallas_call(
        body, name="attn_prep_bwd", out_shape=[jax.ShapeDtypeStruct((S, 768), BF16), jax.ShapeDtypeStruct((8, 128), F32)],
        grid=(S // tm,),
        in_specs=[ht(8), ht(2), ht(2),
                  pl.BlockSpec((tm, 512), lambda i: (i, O_QA // 512)), pl.BlockSpec((tm, 128), lambda i: (i, O_KA // 128)),
                  pl.BlockSpec((tm, 128), lambda i: (i, 0)), pl.BlockSpec((tm, 128), lambda i: (i, 0)), _full((1, 128)), _full((1, 128))],
        out_specs=[pl.BlockSpec((tm, 768), lambda i: (i, 0)), _full((8, 128))],
        compiler_params=_cp(("arbitrary",)),
    )(dqt, dkt, dvt, p, p, cos, sin, qg, kg)


def _ret_bwd_chunk(qr2, kr2, p, rf, rb, dc, qdf, qdb, gn, dyr, cos, sin, xs):
    S = qr2.shape[0]
    C, N = CH, S // CH

    def body(q_ref, k_ref, v_ref, z_ref, rf_ref, rb_ref, dc_ref, qdf_ref, qdb_ref, gn_ref, dyr_ref, cos_ref, sin_ref,
             dpa_ref, dk_ref, dv_ref, drf_ref, drb_ref, dgn_ref, dlg_ref, dqs):
        @pl.when(pl.program_id(0) == 0)
        def _():
            dgn_ref[...] = jnp.zeros_like(dgn_ref)
            dlg_ref[...] = jnp.zeros_like(dlg_ref)

        qv = q_ref[...]
        qb, kb, vb = qv.astype(BF16), k_ref[...].astype(BF16), v_ref[...].astype(BF16)
        qf32, qb32 = qv * qdf_ref[...], qv * qdb_ref[...]
        qfw, qbw = qf32.astype(BF16), qb32.astype(BF16)
        ii = lax.broadcasted_iota(jnp.int32, (C, C), 0).astype(F32)
        jj = lax.broadcasted_iota(jnp.int32, (C, C), 1).astype(F32)
        dif = ii - jj
        ri = lax.broadcasted_iota(jnp.int32, (C, 1), 0).astype(F32)
        hs = range(HR)
        sd, o = _ret_heads_fwd(qb, kb, vb, qfw, qbw, dc_ref, rf_ref, rb_ref)
        do_b = []
        for h in hs:
            vs = _vs(h)
            mu = jnp.mean(o[h], axis=-1, keepdims=True)
            rstd = lax.rsqrt(jnp.mean(jnp.square(o[h] - mu), axis=-1, keepdims=True) + EPS)
            on = (o[h] - mu) * rstd
            z = z_ref[:, vs].astype(F32)
            sz = _sigmoid(z)
            dy = dyr_ref[:, vs]
            gnv = gn_ref[:, vs]
            dpa_ref[:, 256 + DV * h:256 + DV * h + DV] = (dy * (on * gnv) * (sz * (1.0 + z * (1.0 - sz)))).astype(BF16)
            dys = dy * (z * sz)
            dgn_ref[:, vs] += jnp.sum(dys * on, axis=0, keepdims=True)
            don = dys * gnv
            do = rstd * (don - jnp.mean(don, axis=-1, keepdims=True) - on * jnp.mean(don * on, axis=-1, keepdims=True))
            do_b.append(do.astype(BF16))
        dpm = [_dot(do_b[h], vb[:, _vs(h)], NT) for h in hs]
        dqf = [_dot(do_b[h], rf_ref[0, h].astype(BF16), NT) for h in hs]
        dqb = [_dot(do_b[h], rb_ref[0, h].astype(BF16), NT) for h in hs]
        for h in hs:
            dv_ref[:, _vs(h)] = _dot(sd[h].astype(BF16), do_b[h], TN)
            drf_ref[0, h] = _dot(qfw[:, _ks(h)], do_b[h], TN)
            drb_ref[0, h] = _dot(qbw[:, _ks(h)], do_b[h], TN)
        dsd = [(dpm[h] * dc_ref[h]).astype(BF16) for h in hs]
        for h in hs:
            ks = _ks(h)
            dqs[:, ks] = _dot(dsd[h], kb[:, ks]) + dqf[h] * qdf_ref[:, ks] + dqb[h] * qdb_ref[:, ks]
            dk_ref[:, ks] = _dot(dsd[h], qb[:, ks], TN)
        for h in hs:
            ks = _ks(h)
            e = dpm[h] * sd[h]
            lf = _sum11(e * jnp.maximum(dif, 0.0)) + _sum11(jnp.sum(qf32[:, ks] * dqf[h], axis=-1, keepdims=True) * (ri + 1.0))
            lb = _sum11(e * jnp.maximum(-dif, 0.0)) + _sum11(jnp.sum(qb32[:, ks] * dqb[h], axis=-1, keepdims=True) * (C - ri))
            dlg_ref[h:h + 1, :] += jnp.broadcast_to(lf, (1, 128))
            dlg_ref[HR + h:HR + h + 1, :] += jnp.broadcast_to(lb, (1, 128))
        cos_v, sin_v = cos_ref[...], sin_ref[...]
        for g in range(2):
            sl = slice(128 * g, 128 * g + 128)
            dpa_ref[:, sl] = _rope_t(dqs[:, sl], cos_v, sin_v).astype(BF16)

    st = jax.ShapeDtypeStruct((N, HR, DH, DV), F32)
    stb = lambda: pl.BlockSpec((1, HR, DH, DV), lambda t: (t, 0, 0, 0))
    return _host_call(
        body, xs, name="ret_bwd_chunk",
        out_shape=[jax.ShapeDtypeStruct((S, 768), BF16), jax.ShapeDtypeStruct((S, 256), F32), jax.ShapeDtypeStruct((S, 512), F32), st, st,
                   jax.ShapeDtypeStruct((1, 512), F32), jax.ShapeDtypeStruct((8, 128), F32)],
        grid=(N,),
        in_specs=[pl.BlockSpec((C, 256), lambda t: (t, 0)), pl.BlockSpec((C, 256), lambda t: (t, 0)),
                  pl.BlockSpec((C, 512), lambda t: (t, O_VR // 512)), pl.BlockSpec((C, 512), lambda t: (t, O_ZR // 512)),
                  stb(), stb(), _full((HR, C, C)), _full((C, 256)), _full((C, 256)), _full((1, 512)),
                  pl.BlockSpec((C, 512), lambda t: (t, 0)), pl.BlockSpec((C, 128), lambda t: (t, 0)), pl.BlockSpec((C, 128), lambda t: (t, 0))],
        out_specs=[pl.BlockSpec((C, 768), lambda t: (t, 0)), pl.BlockSpec((C, 256), lambda t: (t, 0)), pl.BlockSpec((C, 512), lambda t: (t, 0)),
                   stb(), stb(), _full((1, 512)), _full((8, 128))],
        scratch_shapes=[pltpu.VMEM((C, 256), F32)], operands=(qr2, kr2, p, p, rf, rb, dc, qdf, qdb, gn, dyr, cos, sin),
        compiler_params=_cp(("arbitrary",)),
    )


def _ret_bwd_scan(kr2, p, rf, rb, drf, drb, kdf, kdb, adec):
    S = kr2.shape[0]
    C, N = CH, S // CH

    def body(kf_ref, vf_ref, kb_ref, vb_ref, rf_ref, rb_ref, drf_ref, drb_ref, kdf_ref, kdb_ref, a_ref,
             dkf_ref, dkb_ref, dvf_ref, dvb_ref, dlg_ref, gf, gb):
        @pl.when(pl.program_id(0) == 0)
        def _():
            gf[...] = jnp.zeros_like(gf)
            gb[...] = jnp.zeros_like(gb)
            dlg_ref[...] = jnp.zeros_like(dlg_ref)

        ri = lax.broadcasted_iota(jnp.int32, (C, 1), 0).astype(F32)

        def one(k_ref, v_ref, r_ref, dr_ref, kd_ref, g_s, dk_ref, dv_ref, row0, wexp):
            kd32 = k_ref[...] * kd_ref[...]
            kdw = kd32.astype(BF16)
            vb = v_ref[...].astype(BF16)
            for h in range(HR):
                ks, vs = _ks(h), _vs(h)
                gst = g_s[h]
                g_b = gst.astype(BF16)
                dkd = _dot(vb[:, vs], g_b, NT)
                dk_ref[:, ks] = dkd * kd_ref[:, ks]
                dv_ref[:, vs] = _dot(kdw[:, ks], g_b)
                av = a_ref[row0 + h:row0 + h + 1, :]
                lg = (_sum11(jnp.sum(kd32[:, ks] * dkd, axis=-1, keepdims=True) * wexp)
                      + C * av[:, 0:1] * _sum11(r_ref[0, h] * gst))
                dlg_ref[row0 + h:row0 + h + 1, :] += jnp.broadcast_to(lg, (1, 128))
                g_s[h] = dr_ref[0, h] + av * gst

        one(kf_ref, vf_ref, rf_ref, drf_ref, kdf_ref, gf, dkf_ref, dvf_ref, 0, C - 1.0 - ri)
        one(kb_ref, vb_ref, rb_ref, drb_ref, kdb_ref, gb, dkb_ref, dvb_ref, HR, ri)

    fwd = lambda w, off=0: pl.BlockSpec((C, w), lambda t: (N - 1 - t, off))
    bwd = lambda w, off=0: pl.BlockSpec((C, w), lambda t: (t, off))
    stf = lambda: pl.BlockSpec((1, HR, DH, DV), lambda t: (N - 1 - t, 0, 0, 0))
    stb = lambda: pl.BlockSpec((1, HR, DH, DV), lambda t: (t, 0, 0, 0))
    return pl.pallas_call(
        body, name="ret_bwd_scan",
        out_shape=[jax.ShapeDtypeStruct((S, 256), F32), jax.ShapeDtypeStruct((S, 256), F32), jax.ShapeDtypeStruct((S, 512), F32),
                   jax.ShapeDtypeStruct((S, 512), F32), jax.ShapeDtypeStruct((8, 128), F32)],
        grid=(N,),
        in_specs=[fwd(256), fwd(512, O_VR // 512), bwd(256), bwd(512, O_VR // 512), stf(), stb(), stf(), stb(),
                  _full((C, 256)), _full((C, 256)), _full((8, 128))],
        out_specs=[fwd(256), bwd(256), fwd(512), bwd(512), _full((8, 128))],
        scratch_shapes=[pltpu.VMEM((HR, DH, DV), F32), pltpu.VMEM((HR, DH, DV), F32)],
        compiler_params=_cp(("arbitrary",)),
    )(kr2, p, kr2, p, rf, rb, drf, drb, kdf, kdb, adec)


def _ret_bwd_final(dk_i, dkf, dkb, dv_i, dvf, dvb, cos, sin):
    S = dk_i.shape[0]
    tm = min(512, S)

    def body(a_ref, b_ref, c_ref, d_ref, e_ref, f_ref, cos_ref, sin_ref, o_ref):
        o_ref[:, :512] = (d_ref[...] + e_ref[...] + f_ref[...]).astype(BF16)
        cos_v, sin_v = cos_ref[...], sin_ref[...]
        for g in range(2):
            sl = slice(128 * g, 128 * g + 128)
            dk = a_ref[:, sl] + b_ref[:, sl] + c_ref[:, sl]
            o_ref[:, 512 + 128 * g:512 + 128 * g + 128] = (_rope_t(dk, cos_v, sin_v) * 0.125).astype(BF16)

    row = lambda w: pl.BlockSpec((tm, w), lambda i: (i, 0))
    return pl.pallas_call(
        body, name="ret_bwd_final", out_shape=jax.ShapeDtypeStruct((S, 768), BF16), grid=(S // tm,),
        in_specs=[row(256), row(256), row(256), row(512), row(512), row(512), row(128), row(128)], out_specs=row(768),
        compiler_params=_cp(("parallel",)),
    )(dk_i, dkf, dkb, dv_i, dvf, dvb, cos, sin)


def _bwd_in(dpm, dpa, dpra, dprb, w_p, x, dout, mod, g_pre, xs):
    S = x.shape[0]
    tm = min(256, S)

    def body(a_ref, b_ref, c_ref, d_ref, w_ref, x_ref, dout_ref, mod_ref, g_ref, gx_ref, sums_ref):
        @pl.when(pl.program_id(0) == 0)
        def _():
            sums_ref[...] = jnp.zeros_like(sums_ref)

        dh = (_dot(a_ref[...], w_ref[:, :O_QA], NT) + _dot(b_ref[...], w_ref[:, O_QA:O_QR], NT)
              + _dot(c_ref[...], w_ref[:, O_QR:O_VR], NT) + _dot(d_ref[...], w_ref[:, O_VR:], NT))
        xv = x_ref[...]
        r = lax.rsqrt(jnp.mean(xv * xv, axis=-1, keepdims=True) + EPS)
        xn = xv * r
        gv = g_ref[...]
        sc1 = 1.0 + mod_ref[1:2, :]
        sums_ref[0:1, :] += jnp.sum(dh, axis=0, keepdims=True)
        sums_ref[1:2, :] += jnp.sum(dh * (xn * gv), axis=0, keepdims=True)
        sums_ref[2:3, :] += jnp.sum(dh * xn, axis=0, keepdims=True) * sc1
        dxn = dh * (gv * sc1)
        gx_ref[...] = dout_ref[...] + r * (dxn - xn * jnp.mean(dxn * xn, axis=-1, keepdims=True))

    row = lambda w: pl.BlockSpec((tm, w), lambda i: (i, 0))
    return _host_call(
        body, xs, name="bwd_in", out_shape=[jax.ShapeDtypeStruct((S, D), F32), jax.ShapeDtypeStruct((8, D), F32)], grid=(S // tm,),
        in_specs=[row(2560), row(768), row(768), row(768), _full((D, P_W)), row(D), row(D), _full((3, D)), _full((1, D))],
        out_specs=[row(D), _full((8, D))], scratch_shapes=[], operands=(dpm, dpa, dpra, dprb, w_p, x, dout, mod, g_pre),
        compiler_params=_cp(("arbitrary",), VMEM_BIG),
    )


SMALL = ("b_ada", "g_pre", "qn_g", "kn_g", "w_dec_f", "w_dec_b", "gn_g", "g_post")


def _small_update(gathered, wmv):
    ns = len(SMALL)

    def body(*refs):
        gin_ref, gmid_ref, ggn_ref, gatt_ref, gl1_ref, gl2_ref = refs[:6]
        wmv_refs = refs[6:6 + 3 * ns]
        loss_ref = refs[6 + 3 * ns]
        out_refs = refs[7 + 3 * ns:]

        def dsum(ref, r=None):
            rows = slice(None) if r is None else slice(r, r + 1)
            acc = ref[0, rows, :]
            for d in range(1, NDEV):
                acc = acc + ref[d, rows, :]
            return acc

        s_lg = dsum(gl1_ref) + dsum(gl2_ref)
        loss_ref[...] = (0.5 / D) * jnp.sum(dsum(gmid_ref, 2), axis=-1, keepdims=True)
        eye = lax.broadcasted_iota(jnp.int32, (8, 128), 0) == lax.broadcasted_iota(jnp.int32, (8, 128), 1)
        dlg = jnp.sum(jnp.where(eye, s_lg, 0.0), axis=0, keepdims=True)
        w_f, w_b = wmv_refs[3 * SMALL.index("w_dec_f")][...], wmv_refs[3 * SMALL.index("w_dec_b")][...]
        s_q, s_k = dsum(gatt_ref, 0), dsum(gatt_ref, 1)
        grads = dict(
            b_ada=jnp.concatenate([dsum(gin_ref, 0), dsum(gin_ref, 1), dsum(gmid_ref, 0)], axis=1),
            g_pre=dsum(gin_ref, 2), g_post=dsum(gmid_ref, 1), gn_g=dsum(ggn_ref),
            qn_g=s_q[:, :DH] + s_q[:, DH:], kn_g=s_k[:, :DH] + s_k[:, DH:],
            w_dec_f=dlg[:, 0:HR] * _sigmoid(-w_f), w_dec_b=dlg[:, HR:2 * HR] * _sigmoid(-w_b))
        for i, nme in enumerate(SMALL):
            g = grads[nme]
            w_ref, m_ref, v_ref = wmv_refs[3 * i:3 * i + 3]
            g_ref, d_ref, nm_ref, nv_ref = out_refs[4 * i:4 * i + 4]
            g_ref[...] = g
            m2 = ADAM_B1 * m_ref[...] + (1.0 - ADAM_B1) * g
            v2 = ADAM_B2 * v_ref[...] + (1.0 - ADAM_B2) * jnp.square(g)
            m_hat = m2 / (1.0 - ADAM_B1 ** ADAM_STEP)
            v_hat = v2 / (1.0 - ADAM_B2 ** ADAM_STEP)
            d_ref[...] = -ADAM_LR * (m_hat / (jnp.sqrt(v_hat) + ADAM_EPS) + ADAM_WD * w_ref[...])
            nm_ref[...] = m2
            nv_ref[...] = v2

    out_shape = [jax.ShapeDtypeStruct((1, 1), F32)]
    for i in range(ns):
        out_shape += [jax.ShapeDtypeStruct(wmv[3 * i].shape, F32)] * 4
    return pl.pallas_call(body, name="small_update", out_shape=out_shape)(*gathered, *wmv)


def _adamw(parts, w, m, v, name):
    n, R, L = parts.shape
    tr = 256 if (R % 256 == 0 and R > 256) else R

    def body(p_ref, w_ref, m_ref, v_ref, g_ref, d_ref, nm_ref, nv_ref):
        g = p_ref[0].astype(F32)
        for k in range(1, n):
            g = g + p_ref[k].astype(F32)
        g_ref[...] = g
        m2 = ADAM_B1 * m_ref[...] + (1.0 - ADAM_B1) * g
        v2 = ADAM_B2 * v_ref[...] + (1.0 - ADAM_B2) * jnp.square(g)
        m_hat = m2 / (1.0 - ADAM_B1 ** ADAM_STEP)
        v_hat = v2 / (1.0 - ADAM_B2 ** ADAM_STEP)
        d_ref[...] = -ADAM_LR * (m_hat / (jnp.sqrt(v_hat) + ADAM_EPS) + ADAM_WD * w_ref[...])
        nm_ref[...] = m2
        nv_ref[...] = v2

    blk = pl.BlockSpec((tr, L), lambda i: (i, 0))
    o = jax.ShapeDtypeStruct((R, L), F32)
    return pl.pallas_call(
        body, name=name, out_shape=[o, o, o, o], grid=(R // tr,),
        in_specs=[pl.BlockSpec((n, tr, L), lambda i: (0, i, 0)), blk, blk, blk], out_specs=[blk, blk, blk, blk],
        compiler_params=_cp(("parallel",), VMEM_BIG),
    )(parts, w, m, v)


def _rope_tables(S):
    f = np.float32
    t = np.arange(S)
    row, col = (t // 64).astype(f), (t % 64).astype(f)
    half = DH // 2
    inv = np.power(f(ROPE_THETA), -np.arange(0, half, 2, dtype=f) / f(half)).astype(f)
    ar, ac = (row[:, None] * inv[None, :]).astype(f), (col[:, None] * inv[None, :]).astype(f)
    cos64 = np.concatenate([np.cos(ar), np.cos(ar), np.cos(ac), np.cos(ac)], axis=1).astype(f)
    sin64 = np.concatenate([-np.sin(ar), np.sin(ar), -np.sin(ac), np.sin(ac)], axis=1).astype(f)
    return jnp.asarray(np.tile(cos64, (1, 2))), jnp.asarray(np.tile(sin64, (1, 2)))


def _to_p_order(w_orig):
    return jnp.concatenate([w_orig[:, ORIG[n][0]:ORIG[n][1]] for n in P_ORDER], axis=1)


def _pad_lanes(v, n):
    return jnp.pad(v, ((0, 0), (0, n - v.shape[1])))


def kernel(x, c, w_ada, b_ada, g_pre, w_in, qn_g, kn_g, w_dec_f, w_dec_b, gn_g, w_pa, w_pr, w_out, g_post, loss_target, m_w_ada, m_b_ada, m_g_pre, m_w_in, m_qn_g, m_kn_g, m_w_dec_f, m_w_dec_b, m_gn_g, m_w_pa, m_w_pr, m_w_out, m_g_post, v_w_ada, v_b_ada, v_g_pre, v_w_in, v_qn_g, v_kn_g, v_w_dec_f, v_w_dec_b, v_gn_g, v_w_pa, v_w_pr, v_w_out, v_g_post):
    S = x.shape[1]
    me = 4 * lax.axis_index("x") + 2 * lax.axis_index("y") + lax.axis_index("c")
    xs, tgt = x[0], loss_target[0]
    ncol_ada = w_ada.shape[2]
    ncol_in = w_in.shape[2]

    b_ada_s = lax.dynamic_slice(b_ada, (0, me * ncol_ada), (1, ncol_ada))
    mod_all, c_act, (wg_in, wg_pa, wg_pr, wg_out) = _prologue(
        jnp.pad(c, ((0, 7), (0, 0))), w_ada[0], b_ada_s,
        [w_in[0].astype(BF16), w_pa[0].astype(BF16), w_pr[0].astype(BF16), w_out[0].astype(BF16)])
    mod = lax.dynamic_index_in_dim(mod_all, me, axis=1, keepdims=False).reshape(3, D)
    w_p = _to_p_order(wg_in.transpose(1, 0, 2).reshape(D, NDEV * ncol_in))
    w_pa_f = wg_pa.transpose(1, 0, 2).reshape(512, D)
    w_pr_f = wg_pr.transpose(1, 0, 2).reshape(512, D)
    w_out_f = wg_out.reshape(D, D)

    cos, sin = _rope_tables(S)
    qg, kg = jnp.tile(qn_g, (1, 2)), jnp.tile(kn_g, (1, 2))

    p, h = _fwd_in(xs, mod, g_pre, w_p)
    qt, kh, kt, vh, vta, qr2, kr2 = _prep(p, cos, sin, qg, kg)
    o_att, o_t, lse = _attn_fwd(qt, kh, vta)
    dc, qdf, qdb, kdf, kdb, adec = _ret_tables(w_dec_f, w_dec_b)
    rf, rb = _ret_states(kr2, p, kdf, kdb, adec)
    yr = _ret_out(qr2, kr2, p, rf, rb, dc, qdf, qdb, gn_g)

    dout, do, dpm, dyr, mb, dub, yab, dab, drb_, sums_mid = _mid(xs, tgt, mod, g_post, o_att, p, yr, w_pa_f, w_pr_f, w_out_f)
    gw_out = _mm_tn(mb, dub, "gw_out")
    gw_pa = _mm_tn(yab, dab, "gw_pa")
    gw_pr = _mm_tn(yr, drb_, "gw_pr")
    gi_m = _mm_tn(h, dpm, "gw_in_mid")

    def shards(cols, nd):
        return cols.astype(BF16).reshape(D, nd, ncol_in).transpose(1, 0, 2)

    all_dev = tuple(range(NDEV))
    (dqt, dkt, dvt), (rs_out, rs_pa, rs_pr, rs_in) = _attn_bwd(qt, kh, kt, vh, do, o_t, lse, [
        (gw_out.astype(BF16).reshape(NDEV, 128, D), all_dev, None),
        (gw_pa.astype(BF16).reshape(512, NDEV, 128).transpose(1, 0, 2), all_dev, None),
        (gw_pr.astype(BF16).reshape(512, NDEV, 128).transpose(1, 0, 2), all_dev, None),
        (shards(gi_m[:, 224:2048], 3), (5, 6, 7), None)])
    dpa, gs_att = _attn_prep_bwd(dqt, dkt, dvt, p, cos, sin, qg, kg)
    gi_a = _mm_tn(h, dpa, "gw_in_att")
    (dpra, dk_i, dv_i, drf, drb, dgn, dlg1), _ = _ret_bwd_chunk(qr2, kr2, p, rf, rb, dc, qdf, qdb, gn_g, dyr, cos, sin, [])
    dkf, dkb, dvf, dvb, dlg2 = _ret_bwd_scan(kr2, p, rf, rb, drf, drb, kdf, kdb, adec)
    dprb = _ret_bwd_final(dk_i, dkf, dkb, dv_i, dvf, dvb, cos, sin)
    gi_ra = _mm_tn(h, dpra, "gw_in_reta")
    gi_rb = _mm_tn(h, dprb, "gw_in_retb")
    (grad_x, sums_in), (rs_in,) = _bwd_in(dpm, dpa, dpra, dprb, w_p, xs, dout, mod, g_pre, [
        (shards(jnp.concatenate([gi_a, gi_m[:, 2048:2560], gi_ra[:, :256], gi_rb[:, 512:768], gi_rb[:, :512], gi_ra[:, 256:768],
                                 gi_m[:, :224]], axis=1), 5), (0, 1, 2, 3, 4), rs_in)])

    gathered = _small_allgather([sums_in, sums_mid, dgn, gs_att, dlg1, dlg2], "ag_small")
    given = dict(b_ada=(b_ada, m_b_ada, v_b_ada), g_pre=(g_pre, m_g_pre, v_g_pre), qn_g=(qn_g, m_qn_g, v_qn_g), kn_g=(kn_g, m_kn_g, v_kn_g),
                 w_dec_f=(w_dec_f, m_w_dec_f, v_w_dec_f), w_dec_b=(w_dec_b, m_w_dec_b, v_w_dec_b), gn_g=(gn_g, m_gn_g, v_gn_g),
                 g_post=(g_post, m_g_post, v_g_post))
    small = _small_update(gathered, [a for nme in SMALL for a in given[nme]])
    loss = small[0][0, 0]

    g_in_all, g_mid_all = gathered[0], gathered[1]
    dmod_all = lax.dynamic_slice(jnp.concatenate([g_in_all[:, 0, :], g_in_all[:, 1, :], g_mid_all[:, 0, :]], axis=1),
                                 (0, me * ncol_ada), (NDEV, ncol_ada))
    g_ada = _mm_tn(c_act, jnp.pad(dmod_all, ((0, 8), (0, 0))).astype(BF16), "gw_ada")

    res = dict(
        w_ada=_adamw(g_ada[None], w_ada[0], m_w_ada[0], v_w_ada[0], "adamw_ada"),
        w_in=_adamw(rs_in, w_in[0], m_w_in[0], v_w_in[0], "adamw_in"),
        w_pa=_adamw(rs_pa, w_pa[0], m_w_pa[0], v_w_pa[0], "adamw_pa"),
        w_pr=_adamw(rs_pr, w_pr[0], m_w_pr[0], v_w_pr[0], "adamw_pr"),
        w_out=_adamw(rs_out, w_out[0], m_w_out[0], v_w_out[0], "adamw_out"),
    )
    names = ["w_ada", "b_ada", "g_pre", "w_in", "qn_g", "kn_g", "w_dec_f", "w_dec_b", "gn_g", "w_pa", "w_pr", "w_out", "g_post"]
    outs = [[], [], [], []]
    for nme in names:
        for q in range(4):
            if nme in res:
                outs[q].append(res[nme][q][None])
            else:
                outs[q].append(small[1 + 4 * SMALL.index(nme) + q])
    return (loss, grad_x[None], *outs[0], *outs[1], *outs[2], *outs[3])
```

```python
import jax
import jax.numpy as jnp
import numpy as np
from jax import lax
from jax.experimental import pallas as pl
from jax.experimental.pallas import tpu as pltpu

F32, BF16 = jnp.float32, jnp.bfloat16
D = 1024
DH = 64
DHA = 80
DV = 128
LOG2E = 1.4426950408889634
LN2 = 0.6931471805599453
HR = 4
CH = 128
EPS = 1e-6
ROPE_THETA = 10000.0
NDEV = 8
O_GL, O_ZA, O_QA, O_KA, O_VA, O_QR, O_ZR, O_VR, O_KR, P_W = 0, 2048, 2560, 3072, 3200, 3328, 3584, 4096, 4608, 4864
ORIG = dict(qa=(0, 512), ka=(512, 640), va=(640, 768), za=(768, 1280), qr=(1280, 1536), kr=(1536, 1792),
            vr=(1792, 2304), zr=(2304, 2816), gl=(2816, 4864))
P_ORDER = ("gl", "za", "qa", "ka", "va", "qr", "zr", "vr", "kr")
ADAM_LR, ADAM_B1, ADAM_B2, ADAM_EPS, ADAM_WD, ADAM_STEP = 0.001, 0.9, 0.999, 1e-08, 0.01, 10
VMEM_BIG = 56 * 1024 * 1024
MESH = pl.DeviceIdType.MESH

NT = (((1,), (1,)), ((), ()))
TN = (((0,), (0,)), ((), ()))


def _dot(a, b, dims=None):
    if dims is None:
        return jnp.dot(a, b, preferred_element_type=F32)
    return lax.dot_general(a, b, dims, preferred_element_type=F32)


def _cp(sem=None, vmem=None):
    kw = {}
    if sem is not None:
        kw["dimension_semantics"] = sem
    if vmem is not None:
        kw["vmem_limit_bytes"] = vmem
    return pltpu.CompilerParams(**kw)


def _sigmoid(z):
    return 1.0 / (1.0 + jnp.exp(-z))


def _sum11(m):
    return jnp.sum(jnp.sum(m, axis=-1, keepdims=True), axis=0, keepdims=True)


def _full(shape):
    n = len(shape)
    return pl.BlockSpec(shape, lambda *_: (0,) * n)


def _my_pos():
    return lax.axis_index("x"), lax.axis_index("y"), lax.axis_index("c")


def _peer(k, x, y, c):
    return ((1 - x) if k & 4 else x, (1 - y) if k & 2 else y, (1 - c) if k & 1 else c)


def _small_allgather(vs, name):
    n = len(vs)

    def body(*refs):
        v_refs, out_refs = refs[:n], refs[n:2 * n]
        send_sems, recv_sems = refs[2 * n:]
        x, y, c = _my_pos()
        me = 4 * x + 2 * y + c
        cps = []
        for a in range(n):
            out_refs[a][me] = v_refs[a][...]
            for k in range(1, NDEV):
                cp = pltpu.make_async_remote_copy(src_ref=v_refs[a], dst_ref=out_refs[a].at[me], send_sem=send_sems.at[a, k - 1],
                                                  recv_sem=recv_sems.at[a, k - 1], device_id=_peer(k, x, y, c), device_id_type=MESH)
                cp.start()
                cps.append(cp)
        for cp in cps:
            cp.wait()

    vm = pl.BlockSpec(memory_space=pltpu.VMEM)
    return pl.pallas_call(
        body, name=name, out_shape=[jax.ShapeDtypeStruct((NDEV,) + v.shape, v.dtype) for v in vs],
        in_specs=[vm] * n, out_specs=[vm] * n,
        scratch_shapes=[pltpu.SemaphoreType.DMA((n, NDEV - 1)), pltpu.SemaphoreType.DMA((n, NDEV - 1))],
    )(*vs)


def _prologue(c8, w_ada_s, b_ada_s, arrs):
    n = len(arrs)
    ncol = w_ada_s.shape[1]

    def body(*refs):
        c_ref, wa_ref, ba_ref = refs[:3]
        ins = refs[3:3 + n]
        mod_ref, cact_ref = refs[3 + n:5 + n]
        outs = refs[5 + n:5 + 2 * n]
        call_ref, send_sems, recv_sems, local_sems, s_send, s_recv = refs[5 + 2 * n:]
        x, y, c = _my_pos()
        me, sibling = (x, y, c), (x, y, 1 - c)
        chips = [(1 - x, y), (x, 1 - y), (1 - x, 1 - y)]
        me_i = 4 * x + 2 * y + c

        def small_gather(src_ref, dst_ref, row):
            cps = []
            for k in range(1, NDEV):
                cp = pltpu.make_async_remote_copy(src_ref=src_ref, dst_ref=dst_ref.at[me_i], send_sem=s_send.at[row, k - 1],
                                                  recv_sem=s_recv.at[row, k - 1], device_id=_peer(k, x, y, c), device_id_type=MESH)
                cp.start()
                cps.append(cp)
            return cps

        def blk(a, px, py, pc):
            return outs[a].at[4 * px + 2 * py + pc]

        def copy(a, k, block, to, src=None):
            return pltpu.make_async_remote_copy(src_ref=blk(a, *block) if src is None else src, dst_ref=blk(a, *block),
                                                send_sem=send_sems.at[a, k], recv_sem=recv_sems.at[a, k], device_id=to, device_id_type=MESH)

        call_ref[me_i] = c_ref[...]
        for cp in small_gather(c_ref, call_ref, 0):
            cp.wait()

        local, sent = [], []
        for a in range(n):
            mine = pltpu.make_async_copy(ins[a], blk(a, *me), local_sems.at[a])
            mine.start()
            local.append(mine)
            first = [copy(a, 0, me, sibling, src=ins[a])] + [copy(a, 1 + j, me, (*chip, c), src=ins[a]) for j, chip in enumerate(chips)]
            for cp in first:
                cp.start()
            sent += first

        cv = call_ref[:, 0, :]
        ca = jnp.concatenate([cv * _sigmoid(cv), jnp.zeros_like(cv)], axis=0).astype(BF16)
        cact_ref[...] = ca
        mod_ref[me_i] = (_dot(ca, wa_ref[...].astype(BF16)) + ba_ref[...])[:8]
        mod_copies = small_gather(mod_ref.at[me_i], mod_ref, 1)

        for j, chip in enumerate(chips):
            for a in range(n):
                copy(a, 1 + j, (*chip, c), me).wait_recv()
                cp = copy(a, 4 + j, (*chip, c), sibling)
                cp.start()
                sent.append(cp)
        for a in range(n):
            copy(a, 0, sibling, me).wait_recv()
            for j, chip in enumerate(chips):
                copy(a, 4 + j, (*chip, 1 - c), me).wait_recv()
        for cp in sent:
            cp.wait_send()
        for cp in local + mod_copies:
            cp.wait()

    vm, hbm = pl.BlockSpec(memory_space=pltpu.VMEM), pl.BlockSpec(memory_space=pl.ANY)
    res = pl.pallas_call(
        body, name="prologue",
        out_shape=[jax.ShapeDtypeStruct((NDEV, 8, ncol), F32), jax.ShapeDtypeStruct((16, D), BF16)]
        + [jax.ShapeDtypeStruct((NDEV,) + a.shape, a.dtype) for a in arrs],
        in_specs=[vm, vm, vm] + [hbm] * n, out_specs=[vm, vm] + [hbm] * n,
        scratch_shapes=[pltpu.VMEM((NDEV, 8, D), F32), pltpu.SemaphoreType.DMA((n, NDEV - 1)), pltpu.SemaphoreType.DMA((n, NDEV - 1)),
                        pltpu.SemaphoreType.DMA((n,)), pltpu.SemaphoreType.DMA((2, NDEV - 1)), pltpu.SemaphoreType.DMA((2, NDEV - 1))],
    )(c8, w_ada_s, b_ada_s, *arrs)
    return res[0], res[1], res[2:]


def _in_set(idx, dests):
    p = idx == dests[0]
    for d in dests[1:]:
        p = jnp.logical_or(p, idx == d)
    return p


def _host_call(body, xs, *, name, grid, in_specs, out_specs, out_shape, scratch_shapes, operands, compiler_params):
    nx, nin, nout, nscr = len(xs), len(operands), len(out_shape), len(scratch_shapes)
    if nx == 0:
        res = pl.pallas_call(body, name=name, grid=grid, in_specs=in_specs, out_specs=out_specs, out_shape=out_shape,
                             scratch_shapes=scratch_shapes, compiler_params=compiler_params)(*operands)
        return res, []
    ops, specs, aliases = list(operands), list(in_specs), {}
    oshape, ospecs = list(out_shape), list(out_specs)
    any_spec = pl.BlockSpec(memory_space=pl.ANY)
    for a, (send, dests, recv) in enumerate(xs):
        ops.append(send)
        specs.append(any_spec)
        if recv is not None:
            aliases[len(ops)] = nout + a
            ops.append(recv)
            specs.append(any_spec)
            oshape.append(jax.ShapeDtypeStruct(recv.shape, recv.dtype))
        else:
            oshape.append(jax.ShapeDtypeStruct((NDEV,) + send.shape[1:], send.dtype))
        ospecs.append(any_spec)
    ntot_in = len(ops)

    def wrapped(*refs):
        host_in = refs[:nin]
        sends, pos = [], nin
        for (_, _, recv) in xs:
            sends.append(refs[pos])
            pos += 1 if recv is None else 2
        host_out = refs[ntot_in:ntot_in + nout]
        recvs = refs[ntot_in + nout:ntot_in + nout + nx]
        host_scr = refs[ntot_in + nout + nx:ntot_in + nout + nx + nscr]
        send_sems, recv_sems, local_sems = refs[ntot_in + nout + nx + nscr:]
        first = pl.program_id(0) == 0
        last = pl.program_id(0) == grid[0] - 1
        for ax in range(1, len(grid)):
            first = jnp.logical_and(first, pl.program_id(ax) == 0)
            last = jnp.logical_and(last, pl.program_id(ax) == grid[ax] - 1)
        x, y, c = _my_pos()
        me = 4 * x + 2 * y + c

        def each(fn_remote, fn_local):
            for a, (_, dests, _) in enumerate(xs):
                lo, nd = dests[0], len(dests)
                for k in range(1, NDEV):
                    px, py, pc = _peer(k, x, y, c)
                    pidx = 4 * px + 2 * py + pc
                    cp = pltpu.make_async_remote_copy(src_ref=sends[a].at[jnp.clip(pidx - lo, 0, nd - 1)], dst_ref=recvs[a].at[me],
                                                      send_sem=send_sems.at[a, k - 1], recv_sem=recv_sems.at[a, k - 1],
                                                      device_id=(px, py, pc), device_id_type=MESH)
                    fn_remote(cp, _in_set(pidx, dests), _in_set(me, dests))
                lc = pltpu.make_async_copy(sends[a].at[jnp.clip(me - lo, 0, nd - 1)], recvs[a].at[me], local_sems.at[a])
                fn_local(lc, _in_set(me, dests))

        def start_remote(cp, to_dest, _):
            pl.when(jnp.logical_and(first, to_dest))(cp.start)

        def start_local(lc, i_am_dest):
            pl.when(jnp.logical_and(first, i_am_dest))(lc.start)

        def wait_remote(cp, to_dest, i_am_dest):
            pl.when(jnp.logical_and(last, to_dest))(cp.wait_send)
            pl.when(jnp.logical_and(last, i_am_dest))(cp.wait_recv)

        def wait_local(lc, i_am_dest):
            pl.when(jnp.logical_and(last, i_am_dest))(lc.wait)

        each(start_remote, start_local)
        body(*host_in, *host_out, *host_scr)
        each(wait_remote, wait_local)

    res = pl.pallas_call(
        wrapped, name=name, grid=grid, in_specs=specs, out_specs=ospecs, out_shape=oshape, input_output_aliases=aliases,
        scratch_shapes=list(scratch_shapes) + [pltpu.SemaphoreType.DMA((nx, NDEV - 1)), pltpu.SemaphoreType.DMA((nx, NDEV - 1)),
                                               pltpu.SemaphoreType.DMA((nx,))],
        compiler_params=compiler_params,
    )(*ops)
    return res[:nout], res[nout:]


_HBM = pl.BlockSpec(memory_space=pltpu.HBM)
_SEM = pl.BlockSpec(memory_space=pltpu.SEMAPHORE)


def _xchg_copies(xs_dests, sends, lands, ssem, rsem, lsem, row0):
    x, y, c = _my_pos()
    me = 4 * x + 2 * y + c
    remote, local = [], []
    for a, dests in enumerate(xs_dests):
        lo, nd = dests[0], len(dests)
        for k in range(1, NDEV):
            px, py, pc = _peer(k, x, y, c)
            pidx = 4 * px + 2 * py + pc
            cp = pltpu.make_async_remote_copy(src_ref=sends[a].at[jnp.clip(pidx - lo, 0, nd - 1)], dst_ref=lands[a].at[me],
                                              send_sem=ssem.at[(row0 + a) * (NDEV - 1) + k - 1], recv_sem=rsem.at[(row0 + a) * (NDEV - 1) + k - 1],
                                              device_id=(px, py, pc), device_id_type=MESH)
            remote.append((cp, _in_set(pidx, dests), _in_set(me, dests)))
        lc = pltpu.make_async_copy(sends[a].at[jnp.clip(me - lo, 0, nd - 1)], lands[a].at[me], lsem.at[row0 + a])
        local.append((lc, _in_set(me, dests)))
    return remote, local


def _xchg_start(xs, name):
    n = len(xs)
    dests = [d for _, d in xs]
    sends = [pltpu.with_memory_space_constraint(s, pltpu.HBM) for s, _ in xs]
    lands = [pltpu.with_memory_space_constraint(lax.empty((NDEV,) + s.shape[1:], s.dtype), pltpu.HBM) for s, _ in xs]

    def body(*refs):
        send_refs, land_refs = refs[:n], refs[n:2 * n]
        ssem, rsem, lsem = refs[2 * n:2 * n + 3]
        token = refs[-1]
        remote, local = _xchg_copies(dests, send_refs, land_refs, ssem, rsem, lsem, 0)
        for cp, to_dest, _ in remote:
            pl.when(to_dest)(cp.start)
        for lc, i_am_dest in local:
            pl.when(i_am_dest)(lc.start)
        token[...] = jnp.zeros_like(token)

    res = pl.pallas_call(
        body, name=name,
        out_shape=[pltpu.SemaphoreType.DMA((n * (NDEV - 1),)), pltpu.SemaphoreType.DMA((n * (NDEV - 1),)), pltpu.SemaphoreType.DMA((n,))]
        + [pltpu.HBM(a.shape, a.dtype) for a in sends + lands] + [jax.ShapeDtypeStruct((8, 128), F32)],
        in_specs=[_HBM] * (2 * n), out_specs=[_SEM, _SEM, _SEM] + [_HBM] * (2 * n) + [pl.BlockSpec(memory_space=pltpu.VMEM)],
        input_output_aliases={i: 3 + i for i in range(2 * n)},
        compiler_params=pltpu.CompilerParams(has_side_effects=pltpu.SideEffectType.DATAFLOW_SIDE_EFFECTING),
    )(*sends, *lands)
    return (res[0], res[1], res[2], res[3:3 + n], res[3 + n:3 + 2 * n], dests), res[-1]


def _xchg_wait(state, after, name):
    ssem, rsem, lsem, sends, lands, dests = state
    n = len(dests)

    def body(*refs):
        send_refs, land_refs = refs[:n], refs[n:2 * n]
        ssem_r, rsem_r, lsem_r = refs[2 * n:2 * n + 3]
        remote, local = _xchg_copies(dests, send_refs, land_refs, ssem_r, rsem_r, lsem_r, 0)
        for cp, to_dest, i_am_dest in remote:
            pl.when(to_dest)(cp.wait_send)
            pl.when(i_am_dest)(cp.wait_recv)
        for lc, i_am_dest in local:
            pl.when(i_am_dest)(lc.wait)

    res = pl.pallas_call(
        body, name=name, out_shape=[pltpu.HBM(a.shape, a.dtype) for a in list(sends) + list(lands)],
        in_specs=[_HBM] * (2 * n) + [_SEM, _SEM, _SEM, pl.BlockSpec(memory_space=pl.ANY)], out_specs=[_HBM] * (2 * n),
        input_output_aliases={i: i for i in range(2 * n)},
        compiler_params=pltpu.CompilerParams(has_side_effects=pltpu.SideEffectType.DATAFLOW_SIDE_EFFECTING),
    )(*sends, *lands, ssem, rsem, lsem, after)
    return res[n:]


def _mm_tn(a, b, name):
    S, M = a.shape
    N = b.shape[1]
    tk = min(2048, S)
    tn = N if N <= 768 else (640 if N % 640 == 0 else 512)
    nk = S // tk

    def body(a_ref, b_ref, o_ref):
        @pl.when(pl.program_id(1) == 0)
        def _():
            o_ref[...] = jnp.zeros_like(o_ref)
        o_ref[...] += _dot(a_ref[...], b_ref[...], TN)

    return pl.pallas_call(
        body, name=name, out_shape=jax.ShapeDtypeStruct((M, N), F32), grid=(N // tn, nk),
        in_specs=[pl.BlockSpec((tk, M), lambda j, k: (k, 0)), pl.BlockSpec((tk, tn), lambda j, k: (k, j))],
        out_specs=pl.BlockSpec((M, tn), lambda j, k: (0, j)),
        compiler_params=_cp(("parallel", "arbitrary"), VMEM_BIG),
    )(a, b)


def _fwd_in(x, mod, g_pre, w_p):
    S = x.shape[0]
    tm = min(512, S)

    def body(x_ref, mod_ref, g_ref, w_ref, p_ref, h_ref):
        xv = x_ref[...]
        r = lax.rsqrt(jnp.mean(xv * xv, axis=-1, keepdims=True) + EPS)
        h = (((xv * r) * g_ref[...]) * (1.0 + mod_ref[1:2, :]) + mod_ref[0:1, :]).astype(BF16)
        h_ref[...] = h
        p_ref[...] = _dot(h, w_ref[...]).astype(BF16)

    return pl.pallas_call(
        body, name="fwd_in", out_shape=[jax.ShapeDtypeStruct((S, P_W), BF16), jax.ShapeDtypeStruct((S, D), BF16)],
        grid=(S // tm,),
        in_specs=[pl.BlockSpec((tm, D), lambda i: (i, 0)), _full((3, D)), _full((1, D)), _full((D, P_W))],
        out_specs=[pl.BlockSpec((tm, P_W), lambda i: (i, 0)), pl.BlockSpec((tm, D), lambda i: (i, 0))],
        compiler_params=_cp(("parallel",), VMEM_BIG),
    )(x, mod, g_pre, w_p)


def _swap16(v):
    lane = lax.broadcasted_iota(jnp.int32, v.shape, 1)
    return jnp.where((lane % 32) < 16, pltpu.roll(v, 112, 1), pltpu.roll(v, 16, 1))


def _rope(v, cos, sin):
    return v * cos + _swap16(v) * sin


def _rope_t(v, cos, sin):
    return v * cos - _swap16(v) * sin


def _head_mean(v):
    lo = lax.broadcasted_iota(jnp.int32, v.shape, 1) < 64
    m0 = jnp.sum(jnp.where(lo, v, 0.0), axis=-1, keepdims=True)
    m1 = jnp.sum(jnp.where(lo, 0.0, v), axis=-1, keepdims=True)
    return jnp.where(lo, m0, m1) * (1.0 / 64.0)


def _prep(p, cos, sin, qg, kg):
    S = p.shape[0]
    tm = min(512, S)

    def body(qa_ref, kv_ref, qr_ref, kr_ref, cos_ref, sin_ref, qg_ref, kg_ref, qt_ref, kh_ref, kt_ref, vh_ref, vta_ref, qr2_ref, kr2_ref):
        cos_v, sin_v = cos_ref[...], sin_ref[...]
        for g in range(4):
            xv = qa_ref[:, 128 * g:128 * g + 128].astype(F32)
            r = lax.rsqrt(_head_mean(xv * xv) + EPS)
            yt = (_rope((xv * r) * qg_ref[...], cos_v, sin_v) * (0.125 * LOG2E)).T
            qt_ref[2 * g] = yt[:DH].astype(BF16)
            qt_ref[2 * g + 1] = yt[DH:].astype(BF16)
        xv = kv_ref[:, :128].astype(F32)
        r = lax.rsqrt(_head_mean(xv * xv) + EPS)
        yv = _rope((xv * r) * kg_ref[...], cos_v, sin_v)
        kh_ref[0] = yv[:, :64].astype(BF16)
        kh_ref[1] = yv[:, 64:].astype(BF16)
        yt = yv.T
        kt_ref[0] = yt[:DH].astype(BF16)
        kt_ref[1] = yt[DH:].astype(BF16)
        vv = kv_ref[:, 128:].astype(F32)
        vh_ref[0] = vv[:, :64].astype(BF16)
        vh_ref[1] = vv[:, 64:].astype(BF16)
        vt = vv.T
        tail = (lax.broadcasted_iota(jnp.int32, (DHA - DH, tm), 0) == 0).astype(BF16)
        for kvh in range(2):
            vta_ref[kvh, 0:DH, :] = vt[DH * kvh:DH * kvh + DH].astype(BF16)
            vta_ref[kvh, DH:DHA, :] = tail
        for g in range(2):
            sl = slice(128 * g, 128 * g + 128)
            qr2_ref[:, sl] = _rope(qr_ref[:, sl].astype(F32), cos_v, sin_v)
            kr2_ref[:, sl] = _rope(kr_ref[:, sl].astype(F32), cos_v, sin_v) * 0.125

    hm = lambda n: pl.BlockSpec((n, tm, DH), lambda i: (0, i, 0))
    ht = lambda n, r: pl.BlockSpec((n, r, tm), lambda i: (0, 0, i))
    return pl.pallas_call(
        body, name="prep",
        out_shape=[jax.ShapeDtypeStruct((8, DH, S), BF16), jax.ShapeDtypeStruct((2, S, DH), BF16), jax.ShapeDtypeStruct((2, DH, S), BF16),
                   jax.ShapeDtypeStruct((2, S, DH), BF16), jax.ShapeDtypeStruct((2, DHA, S), BF16),
                   jax.ShapeDtypeStruct((S, 256), F32), jax.ShapeDtypeStruct((S, 256), F32)],
        grid=(S // tm,),
        in_specs=[pl.BlockSpec((tm, 512), lambda i: (i, O_QA // 512)), pl.BlockSpec((tm, 256), lambda i: (i, O_KA // 256)),
                  pl.BlockSpec((tm, 256), lambda i: (i, O_QR // 256)), pl.BlockSpec((tm, 256), lambda i: (i, O_KR // 256)),
                  pl.BlockSpec((tm, 128), lambda i: (i, 0)), pl.BlockSpec((tm, 128), lambda i: (i, 0)), _full((1, 128)), _full((1, 128))],
        out_specs=[ht(8, DH), hm(2), ht(2, DH), hm(2), ht(2, DHA), pl.BlockSpec((tm, 256), lambda i: (i, 0)), pl.BlockSpec((tm, 256), lambda i: (i, 0))],
        compiler_params=_cp(("parallel",)),
    )(p, p, p, p, cos, sin, qg, kg)


def _attn_fwd(qt, kh, vta):
    S = qt.shape[2]
    tq, tk = min(1024, S), min(512, S)
    nj = S // tk

    def body(q_ref, k_ref, v_ref, o_ref, ot_ref, lse_ref, m_s, acc_s):
        j = pl.program_id(1)

        @pl.when(j == 0)
        def _():
            m_s[...] = jnp.full_like(m_s, -jnp.inf)
            acc_s[...] = jnp.zeros_like(acc_s)

        m_all = m_s[...]
        st = {0: _dot(k_ref[0], q_ref[0])}
        m_new, acc_new = [], []
        for h in range(8):
            if h + 1 < 8:
                st[h + 1] = _dot(k_ref[(h + 1) // 4], q_ref[h + 1])
            m_old = m_all[h:h + 1, :]
            mn = jnp.maximum(m_old, jnp.max(st[h], axis=0, keepdims=True))
            pt = jnp.exp2(st[h] - mn).astype(BF16)
            acc_new.append(jnp.exp2(m_old - mn) * acc_s[h] + _dot(v_ref[h // 4], pt))
            m_new.append(mn)
            del st[h]
        for h in range(8):
            acc_s[h] = acc_new[h]
            m_s[h:h + 1, :] = m_new[h]

        @pl.when(j == nj - 1)
        def _():
            for h in range(8):
                ot = acc_s[h, 0:DH, :] / acc_s[h, DH:DH + 1, :]
                ot_ref[h] = ot
                o_ref[:, DH * h:DH * h + DH] = ot.T
                lse_ref[h // 4, h % 4:h % 4 + 1, :] = m_s[h:h + 1, :] + jnp.log2(acc_s[h, DH:DH + 1, :])

    return pl.pallas_call(
        body, name="attn_fwd",
        out_shape=[jax.ShapeDtypeStruct((S, 512), F32), jax.ShapeDtypeStruct((8, DH, S), F32), jax.ShapeDtypeStruct((2, 4, S), F32)],
        grid=(S // tq, nj),
        in_specs=[pl.BlockSpec((8, DH, tq), lambda i, j: (0, 0, i)), pl.BlockSpec((2, tk, DH), lambda i, j: (0, j, 0)),
                  pl.BlockSpec((2, DHA, tk), lambda i, j: (0, 0, j))],
        out_specs=[pl.BlockSpec((tq, 512), lambda i, j: (i, 0)), pl.BlockSpec((8, DH, tq), lambda i, j: (0, 0, i)),
                   pl.BlockSpec((2, 4, tq), lambda i, j: (0, 0, i))],
        scratch_shapes=[pltpu.VMEM((8, tq), F32), pltpu.VMEM((8, DHA, tq), F32)],
        compiler_params=_cp(("parallel", "arbitrary"), VMEM_BIG),
    )(qt, kh, vta)


def _ret_tables(wf, wb):
    C = CH

    def body(wf_ref, wb_ref, dc_ref, qdf_ref, qdb_ref, kdf_ref, kdb_ref, a_ref):
        def logsig(w):
            z = jnp.exp(-jnp.abs(w))
            u = 1.0 + z
            l1p = jnp.where(u == 1.0, z, jnp.log(u) * (z / jnp.where(u == 1.0, 1.0, u - 1.0)))
            return jnp.minimum(w, 0.0) - l1p

        lgf, lgb = logsig(wf_ref[...]), logsig(wb_ref[...])
        lane4 = lax.broadcasted_iota(jnp.int32, (1, 4), 1)

        def pick(lg, h):
            return jnp.sum(jnp.where(lane4 == h, lg, 0.0), axis=-1, keepdims=True)

        ii = lax.broadcasted_iota(jnp.int32, (C, C), 0).astype(F32)
        jj = lax.broadcasted_iota(jnp.int32, (C, C), 1).astype(F32)
        dif = ii - jj
        hd = lax.broadcasted_iota(jnp.int32, (C, 256), 1) // DH
        lf_l = jnp.zeros((C, 256), F32)
        lb_l = jnp.zeros((C, 256), F32)
        for h in range(HR):
            lf, lb = pick(lgf, h), pick(lgb, h)
            dc_ref[h] = jnp.where(dif >= 0, jnp.exp(lf * jnp.maximum(dif, 0.0)), jnp.exp(lb * jnp.maximum(-dif, 0.0)))
            lf_l = jnp.where(hd == h, lf, lf_l)
            lb_l = jnp.where(hd == h, lb, lb_l)
            a_ref[h:h + 1, :] = jnp.broadcast_to(jnp.exp(lf * C), (1, 128))
            a_ref[HR + h:HR + h + 1, :] = jnp.broadcast_to(jnp.exp(lb * C), (1, 128))
        ri = lax.broadcasted_iota(jnp.int32, (C, 256), 0).astype(F32)
        qdf_ref[...] = jnp.exp(lf_l * (ri + 1.0))
        qdb_ref[...] = jnp.exp(lb_l * (C - ri))
        kdf_ref[...] = jnp.exp(lf_l * (C - 1.0 - ri))
        kdb_ref[...] = jnp.exp(lb_l * ri)

    t = jax.ShapeDtypeStruct((C, 256), F32)
    return pl.pallas_call(body, name="ret_tables",
                          out_shape=[jax.ShapeDtypeStruct((HR, C, C), F32), t, t, t, t, jax.ShapeDtypeStruct((8, 128), F32)])(wf, wb)


def _ret_states(kr2, p, kdf, kdb, adec):
    S = kr2.shape[0]
    C, N = CH, S // CH

    def body(kf_ref, vf_ref, kb_ref, vb_ref, kdf_ref, kdb_ref, a_ref, rf_ref, rb_ref, sf, sb):
        @pl.when(pl.program_id(0) == 0)
        def _():
            sf[...] = jnp.zeros_like(sf)
            sb[...] = jnp.zeros_like(sb)

        rf_ref[0] = sf[...]
        rb_ref[0] = sb[...]
        kdfw = (kf_ref[...] * kdf_ref[...]).astype(BF16)
        kdbw = (kb_ref[...] * kdb_ref[...]).astype(BF16)
        vf, vb = vf_ref[...].astype(BF16), vb_ref[...].astype(BF16)
        kvf = [_dot(kdfw[:, _ks(h)], vf[:, _vs(h)], TN) for h in range(HR)]
        kvb = [_dot(kdbw[:, _ks(h)], vb[:, _vs(h)], TN) for h in range(HR)]
        for h in range(HR):
            sf[h] = a_ref[h:h + 1, :] * sf[h] + kvf[h]
            sb[h] = a_ref[HR + h:HR + h + 1, :] * sb[h] + kvb[h]

    st = jax.ShapeDtypeStruct((N, HR, DH, DV), F32)
    return pl.pallas_call(
        body, name="ret_states", out_shape=[st, st], grid=(N,),
        in_specs=[pl.BlockSpec((C, 256), lambda t: (t, 0)), pl.BlockSpec((C, 512), lambda t: (t, O_VR // 512)),
                  pl.BlockSpec((C, 256), lambda t: (N - 1 - t, 0)), pl.BlockSpec((C, 512), lambda t: (N - 1 - t, O_VR // 512)),
                  _full((C, 256)), _full((C, 256)), _full((8, 128))],
        out_specs=[pl.BlockSpec((1, HR, DH, DV), lambda t: (t, 0, 0, 0)), pl.BlockSpec((1, HR, DH, DV), lambda t: (N - 1 - t, 0, 0, 0))],
        scratch_shapes=[pltpu.VMEM((HR, DH, DV), F32), pltpu.VMEM((HR, DH, DV), F32)],
        compiler_params=_cp(("arbitrary",)),
    )(kr2, p, kr2, p, kdf, kdb, adec)


def _ks(h):
    return slice(DH * h, DH * h + DH)


def _vs(h):
    return slice(DV * h, DV * h + DV)


def _ret_heads_fwd(qb, kb, vb, qfw, qbw, dc_ref, rf_ref, rb_ref):
    hs = range(HR)
    s = [_dot(qb[:, _ks(h)], kb[:, _ks(h)], NT) for h in hs]
    inter = [_dot(qfw[:, _ks(h)], rf_ref[0, h].astype(BF16)) + _dot(qbw[:, _ks(h)], rb_ref[0, h].astype(BF16)) for h in hs]
    sd = [s[h] * dc_ref[h] for h in hs]
    o = [_dot(sd[h].astype(BF16), vb[:, _vs(h)]) + inter[h] for h in hs]
    return sd, o


def _ret_out(qr2, kr2, p, rf, rb, dc, qdf, qdb, gn):
    S = qr2.shape[0]
    C, N = CH, S // CH

    def body(q_ref, k_ref, v_ref, z_ref, rf_ref, rb_ref, dc_ref, qdf_ref, qdb_ref, gn_ref, yr_ref):
        qv = q_ref[...]
        qb, kb, vb = qv.astype(BF16), k_ref[...].astype(BF16), v_ref[...].astype(BF16)
        qfw, qbw = (qv * qdf_ref[...]).astype(BF16), (qv * qdb_ref[...]).astype(BF16)
        _, o = _ret_heads_fwd(qb, kb, vb, qfw, qbw, dc_ref, rf_ref, rb_ref)
        for h in range(HR):
            vs = _vs(h)
            mu = jnp.mean(o[h], axis=-1, keepdims=True)
            var = jnp.mean(jnp.square(o[h] - mu), axis=-1, keepdims=True)
            on = (o[h] - mu) * lax.rsqrt(var + EPS)
            z = z_ref[:, vs].astype(F32)
            yr_ref[:, vs] = ((on * gn_ref[:, vs]) * (z * _sigmoid(z))).astype(BF16)

    return pl.pallas_call(
        body, name="ret_out", out_shape=jax.ShapeDtypeStruct((S, 512), BF16), grid=(N,),
        in_specs=[pl.BlockSpec((C, 256), lambda t: (t, 0)), pl.BlockSpec((C, 256), lambda t: (t, 0)),
                  pl.BlockSpec((C, 512), lambda t: (t, O_VR // 512)), pl.BlockSpec((C, 512), lambda t: (t, O_ZR // 512)),
                  pl.BlockSpec((1, HR, DH, DV), lambda t: (t, 0, 0, 0)), pl.BlockSpec((1, HR, DH, DV), lambda t: (t, 0, 0, 0)),
                  _full((HR, C, C)), _full((C, 256)), _full((C, 256)), _full((1, 512))],
        out_specs=pl.BlockSpec((C, 512), lambda t: (t, 0)),
        compiler_params=_cp(("parallel",)),
    )(qr2, kr2, p, p, rf, rb, dc, qdf, qdb, gn)


def _mid(x, tgt, mod, g_post, o_att, p, yr, w_pa, w_pr, w_out):
    S = x.shape[0]
    tm = min(256, S)

    def body(x_ref, t_ref, mod_ref, gp_ref, o_ref, za_ref, gl_ref, yr_ref, wpa_ref, wpr_ref, wout_ref,
             dout_ref, do_ref, dpm_ref, dyr_ref, mb_ref, dub_ref, yab_ref, dab_ref, drb_ref, sums_ref):
        @pl.when(pl.program_id(0) == 0)
        def _():
            sums_ref[...] = jnp.zeros_like(sums_ref)

        za = za_ref[...].astype(F32)
        sa = _sigmoid(za)
        sil = za * sa
        ov = o_ref[...]
        ya_b = (ov * sil).astype(BF16)
        yr_b = yr_ref[...]
        av = _dot(ya_b, wpa_ref[...])
        rv = _dot(yr_b, wpr_ref[...])
        ga = _sigmoid(gl_ref[:, :D].astype(F32))
        gr = _sigmoid(gl_ref[:, D:].astype(F32))
        mb = (ga * av + gr * rv).astype(BF16)
        u = _dot(mb, wout_ref[...])
        r2 = lax.rsqrt(jnp.mean(u * u, axis=-1, keepdims=True) + EPS)
        un = u * r2
        gp = gp_ref[...]
        yv = un * gp
        gate = mod_ref[2:3, :]
        err = (x_ref[...] + gate * yv) - t_ref[...]
        dout = err * (1.0 / D)
        dout_ref[...] = dout
        dy = dout * gate
        sums_ref[0:1, :] += jnp.sum(dout * yv, axis=0, keepdims=True)
        sums_ref[1:2, :] += jnp.sum(dy * un, axis=0, keepdims=True)
        sums_ref[2:3, :] += jnp.sum(err * err, axis=0, keepdims=True)
        dyg = dy * gp
        du_b = (r2 * (dyg - un * jnp.mean(dyg * un, axis=-1, keepdims=True))).astype(BF16)
        dm = _dot(du_b, wout_ref[...], NT)
        da_b = (dm * ga).astype(BF16)
        dr_b = (dm * gr).astype(BF16)
        dpm_ref[:, :D] = (dm * av * (ga * (1.0 - ga))).astype(BF16)
        dpm_ref[:, D:2 * D] = (dm * rv * (gr * (1.0 - gr))).astype(BF16)
        dya = _dot(da_b, wpa_ref[...], NT)
        dyr_ref[...] = _dot(dr_b, wpr_ref[...], NT)
        dov = dya * sil
        for g in range(4):
            dt = dov[:, 128 * g:128 * g + 128].T
            do_ref[2 * g] = dt[:DH].astype(BF16)
            do_ref[2 * g + 1] = dt[DH:].astype(BF16)
        dpm_ref[:, 2 * D:] = (dya * ov * (sa * (1.0 + za * (1.0 - sa)))).astype(BF16)
        mb_ref[...] = mb
        dub_ref[...] = du_b
        yab_ref[...] = ya_b
        dab_ref[...] = da_b
        drb_ref[...] = dr_b

    row = lambda w: pl.BlockSpec((tm, w), lambda i: (i, 0))
    sd = lambda w, dt: jax.ShapeDtypeStruct((S, w), dt)
    return pl.pallas_call(
        body, name="mid",
        out_shape=[sd(D, F32), jax.ShapeDtypeStruct((8, DH, S), BF16), sd(2560, BF16), sd(512, F32), sd(D, BF16), sd(D, BF16), sd(512, BF16),
                   sd(D, BF16), sd(D, BF16), jax.ShapeDtypeStruct((8, D), F32)],
        grid=(S // tm,),
        in_specs=[row(D), row(D), _full((3, D)), _full((1, D)), row(512), pl.BlockSpec((tm, 512), lambda i: (i, O_ZA // 512)),
                  pl.BlockSpec((tm, 2048), lambda i: (i, 0)), row(512), _full((512, D)), _full((512, D)), _full((D, D))],
        out_specs=[row(D), pl.BlockSpec((8, DH, tm), lambda i: (0, 0, i)), row(2560), row(512), row(D), row(D), row(512), row(D), row(D),
                   _full((8, D))],
        compiler_params=_cp(("arbitrary",), VMEM_BIG),
    )(x, tgt, mod, g_post, o_att, p, p, yr, w_pa, w_pr, w_out)


def _attn_bwd(qt, kh, kt, vh, dot_, ot, lse, xs):
    S = qt.shape[2]
    tq, tk = min(512, S), min(1024, S)

    def body(q_ref, k_ref, kt_ref, v_ref, do_ref, o_ref, lse_ref, dq_ref, dk_ref, dv_ref):
        j, i = pl.program_id(0), pl.program_id(1)
        cols = pl.ds(pl.multiple_of(i * tq, tq), tq)
        st = {0: _dot(k_ref[0], q_ref[0])}
        dpt = {0: _dot(v_ref[0], do_ref[0])}
        dk_acc, dv_acc, dqs = [None, None], [None, None], []
        for h in range(8):
            g = h // 4
            if h + 1 < 8:
                st[h + 1] = _dot(k_ref[(h + 1) // 4], q_ref[h + 1])
                dpt[h + 1] = _dot(v_ref[(h + 1) // 4], do_ref[h + 1])
            qt_h, dot_h = q_ref[h], do_ref[h]
            delta = jnp.sum(dot_h.astype(F32) * o_ref[h], axis=0, keepdims=True)
            pt = jnp.exp2(st[h] - lse_ref[g, h % 4:h % 4 + 1, :])
            dst = (pt * (dpt[h] - delta)).astype(BF16)
            dv_h = _dot(dot_h, pt.astype(BF16), NT)
            dk_h = _dot(qt_h, dst, NT)
            dqs.append(_dot(kt_ref[g], dst))
            dv_acc[g] = dv_h if dv_acc[g] is None else dv_acc[g] + dv_h
            dk_acc[g] = dk_h if dk_acc[g] is None else dk_acc[g] + dk_h
            del st[h], dpt[h]

        @pl.when(i == 0)
        def _():
            for g in range(2):
                dk_ref[g] = dk_acc[g]
                dv_ref[g] = dv_acc[g]

        @pl.when(i > 0)
        def _():
            for g in range(2):
                dk_ref[g] += dk_acc[g]
                dv_ref[g] += dv_acc[g]

        @pl.when(j == 0)
        def _():
            for h in range(8):
                dq_ref[h, :, cols] = dqs[h]

        @pl.when(j > 0)
        def _():
            for h in range(8):
                dq_ref[h, :, cols] += dqs[h]

    return _host_call(
        body, xs, name="attn_bwd",
        out_shape=[jax.ShapeDtypeStruct((8, DH, S), F32), jax.ShapeDtypeStruct((2, DH, S), F32), jax.ShapeDtypeStruct((2, DH, S), F32)],
        grid=(S // tk, S // tq),
        in_specs=[pl.BlockSpec((8, DH, tq), lambda j, i: (0, 0, i)), pl.BlockSpec((2, tk, DH), lambda j, i: (0, j, 0)),
                  pl.BlockSpec((2, DH, tk), lambda j, i: (0, 0, j)), pl.BlockSpec((2, tk, DH), lambda j, i: (0, j, 0)),
                  pl.BlockSpec((8, DH, tq), lambda j, i: (0, 0, i)), pl.BlockSpec((8, DH, tq), lambda j, i: (0, 0, i)),
                  pl.BlockSpec((2, 4, tq), lambda j, i: (0, 0, i))],
        out_specs=[pl.BlockSpec((8, DH, S), lambda j, i: (0, 0, 0)), pl.BlockSpec((2, DH, tk), lambda j, i: (0, 0, j)),
                   pl.BlockSpec((2, DH, tk), lambda j, i: (0, 0, j))],
        scratch_shapes=[], operands=(qt, kh, kt, vh, dot_, ot, lse),
        compiler_params=_cp(("arbitrary", "arbitrary"), VMEM_BIG),
    )


def _attn_prep_bwd(dqt, dkt, dvt, p, cos, sin, qg, kg):
    S = dqt.shape[2]
    tm = min(512, S)

    def body(dq_ref, dk_ref, dv_ref, qa_ref, ka_ref, cos_ref, sin_ref, qg_ref, kg_ref, dp_ref, gs_ref):
        @pl.when(pl.program_id(0) == 0)
        def _():
            gs_ref[...] = jnp.zeros_like(gs_ref)

        cos_v, sin_v = cos_ref[...], sin_ref[...]

        def pair(ref, a):
            return jnp.concatenate([ref[a], ref[a + 1]], axis=0).T

        def norm_bwd(dyv, xv, gv, row):
            r = lax.rsqrt(_head_mean(xv * xv) + EPS)
            xn = xv * r
            dxh = _rope_t(dyv, cos_v, sin_v)
            gs_ref[row:row + 1, :] += jnp.sum(dxh * xn, axis=0, keepdims=True)
            dg = dxh * gv
            return r * (dg - xn * _head_mean(dg * xn))

        for g in range(4):
            sl = slice(128 * g, 128 * g + 128)
            dp_ref[:, sl] = norm_bwd(pair(dq_ref, 2 * g) * 0.125, qa_ref[:, sl].astype(F32), qg_ref[...], 0).astype(BF16)
        dp_ref[:, 512:640] = norm_bwd(pair(dk_ref, 0) * LN2, ka_ref[...].astype(F32), kg_ref[...], 1).astype(BF16)
        dp_ref[:, 640:768] = pair(dv_ref, 0).astype(BF16)

    ht = lambda n: pl.BlockSpec((n, DH, tm), lambda i: (0, 0, i))
    return pl.pallas_call(
        body, name="attn_prep_bwd", out_shape=[jax.ShapeDtypeStruct((S, 768), BF16), jax.ShapeDtypeStruct((8, 128), F32)],
        grid=(S // tm,),
        in_specs=[ht(8), ht(2), ht(2),
                  pl.BlockSpec((tm, 512), lambda i: (i, O_QA // 512)), pl.BlockSpec((tm, 128), lambda i: (i, O_KA // 128)),
                  pl.BlockSpec((tm, 128), lambda i: (i, 0)), pl.BlockSpec((tm, 128), lambda i: (i, 0)), _full((1, 128)), _full((1, 128))],
        out_specs=[pl.BlockSpec((tm, 768), lambda i: (i, 0)), _full((8, 128))],
        compiler_params=_cp(("arbitrary",)),
    )(dqt, dkt, dvt, p, p, cos, sin, qg, kg)


def _ret_bwd_chunk(qr2, kr2, p, rf, rb, dc, qdf, qdb, gn, dyr, cos, sin, xs):
    S = qr2.shape[0]
    C, N = CH, S // CH

    def body(q_ref, k_ref, v_ref, z_ref, rf_ref, rb_ref, dc_ref, qdf_ref, qdb_ref, gn_ref, dyr_ref, cos_ref, sin_ref,
             dpa_ref, dk_ref, dv_ref, drf_ref, drb_ref, dgn_ref, dlg_ref, dqs):
        @pl.when(pl.program_id(0) == 0)
        def _():
            dgn_ref[...] = jnp.zeros_like(dgn_ref)
            dlg_ref[...] = jnp.zeros_like(dlg_ref)

        qv = q_ref[...]
        qb, kb, vb = qv.astype(BF16), k_ref[...].astype(BF16), v_ref[...].astype(BF16)
        qf32, qb32 = qv * qdf_ref[...], qv * qdb_ref[...]
        qfw, qbw = qf32.astype(BF16), qb32.astype(BF16)
        ii = lax.broadcasted_iota(jnp.int32, (C, C), 0).astype(F32)
        jj = lax.broadcasted_iota(jnp.int32, (C, C), 1).astype(F32)
        dif = ii - jj
        ri = lax.broadcasted_iota(jnp.int32, (C, 1), 0).astype(F32)
        hs = range(HR)
        sd, o = _ret_heads_fwd(qb, kb, vb, qfw, qbw, dc_ref, rf_ref, rb_ref)
        do_b = []
        for h in hs:
            vs = _vs(h)
            mu = jnp.mean(o[h], axis=-1, keepdims=True)
            rstd = lax.rsqrt(jnp.mean(jnp.square(o[h] - mu), axis=-1, keepdims=True) + EPS)
            on = (o[h] - mu) * rstd
            z = z_ref[:, vs].astype(F32)
            sz = _sigmoid(z)
            dy = dyr_ref[:, vs]
            gnv = gn_ref[:, vs]
            dpa_ref[:, 256 + DV * h:256 + DV * h + DV] = (dy * (on * gnv) * (sz * (1.0 + z * (1.0 - sz)))).astype(BF16)
            dys = dy * (z * sz)
            dgn_ref[:, vs] += jnp.sum(dys * on, axis=0, keepdims=True)
            don = dys * gnv
            do = rstd * (don - jnp.mean(don, axis=-1, keepdims=True) - on * jnp.mean(don * on, axis=-1, keepdims=True))
            do_b.append(do.astype(BF16))
        dpm = [_dot(do_b[h], vb[:, _vs(h)], NT) for h in hs]
        dqf = [_dot(do_b[h], rf_ref[0, h].astype(BF16), NT) for h in hs]
        dqb = [_dot(do_b[h], rb_ref[0, h].astype(BF16), NT) for h in hs]
        for h in hs:
            dv_ref[:, _vs(h)] = _dot(sd[h].astype(BF16), do_b[h], TN)
            drf_ref[0, h] = _dot(qfw[:, _ks(h)], do_b[h], TN)
            drb_ref[0, h] = _dot(qbw[:, _ks(h)], do_b[h], TN)
        dsd = [(dpm[h] * dc_ref[h]).astype(BF16) for h in hs]
        for h in hs:
            ks = _ks(h)
            dqs[:, ks] = _dot(dsd[h], kb[:, ks]) + dqf[h] * qdf_ref[:, ks] + dqb[h] * qdb_ref[:, ks]
            dk_ref[:, ks] = _dot(dsd[h], qb[:, ks], TN)
        for h in hs:
            ks = _ks(h)
            e = dpm[h] * sd[h]
            lf = _sum11(e * jnp.maximum(dif, 0.0)) + _sum11(jnp.sum(qf32[:, ks] * dqf[h], axis=-1, keepdims=True) * (ri + 1.0))
            lb = _sum11(e * jnp.maximum(-dif, 0.0)) + _sum11(jnp.sum(qb32[:, ks] * dqb[h], axis=-1, keepdims=True) * (C - ri))
            dlg_ref[h:h + 1, :] += jnp.broadcast_to(lf, (1, 128))
            dlg_ref[HR + h:HR + h + 1, :] += jnp.broadcast_to(lb, (1, 128))
        cos_v, sin_v = cos_ref[...], sin_ref[...]
        for g in range(2):
            sl = slice(128 * g, 128 * g + 128)
            dpa_ref[:, sl] = _rope_t(dqs[:, sl], cos_v, sin_v).astype(BF16)

    st = jax.ShapeDtypeStruct((N, HR, DH, DV), F32)
    stb = lambda: pl.BlockSpec((1, HR, DH, DV), lambda t: (t, 0, 0, 0))
    return _host_call(
        body, xs, name="ret_bwd_chunk",
        out_shape=[jax.ShapeDtypeStruct((S, 768), BF16), jax.ShapeDtypeStruct((S, 256), F32), jax.ShapeDtypeStruct((S, 512), F32), st, st,
                   jax.ShapeDtypeStruct((1, 512), F32), jax.ShapeDtypeStruct((8, 128), F32)],
        grid=(N,),
        in_specs=[pl.BlockSpec((C, 256), lambda t: (t, 0)), pl.BlockSpec((C, 256), lambda t: (t, 0)),
                  pl.BlockSpec((C, 512), lambda t: (t, O_VR // 512)), pl.BlockSpec((C, 512), lambda t: (t, O_ZR // 512)),
                  stb(), stb(), _full((HR, C, C)), _full((C, 256)), _full((C, 256)), _full((1, 512)),
                  pl.BlockSpec((C, 512), lambda t: (t, 0)), pl.BlockSpec((C, 128), lambda t: (t, 0)), pl.BlockSpec((C, 128), lambda t: (t, 0))],
        out_specs=[pl.BlockSpec((C, 768), lambda t: (t, 0)), pl.BlockSpec((C, 256), lambda t: (t, 0)), pl.BlockSpec((C, 512), lambda t: (t, 0)),
                   stb(), stb(), _full((1, 512)), _full((8, 128))],
        scratch_shapes=[pltpu.VMEM((C, 256), F32)], operands=(qr2, kr2, p, p, rf, rb, dc, qdf, qdb, gn, dyr, cos, sin),
        compiler_params=_cp(("arbitrary",)),
    )


def _ret_bwd_scan(kr2, p, rf, rb, drf, drb, kdf, kdb, adec):
    S = kr2.shape[0]
    C, N = CH, S // CH

    def body(kf_ref, vf_ref, kb_ref, vb_ref, rf_ref, rb_ref, drf_ref, drb_ref, kdf_ref, kdb_ref, a_ref,
             dkf_ref, dkb_ref, dvf_ref, dvb_ref, dlg_ref, gf, gb):
        @pl.when(pl.program_id(0) == 0)
        def _():
            gf[...] = jnp.zeros_like(gf)
            gb[...] = jnp.zeros_like(gb)
            dlg_ref[...] = jnp.zeros_like(dlg_ref)

        ri = lax.broadcasted_iota(jnp.int32, (C, 1), 0).astype(F32)

        def one(k_ref, v_ref, r_ref, dr_ref, kd_ref, g_s, dk_ref, dv_ref, row0, wexp):
            kd32 = k_ref[...] * kd_ref[...]
            kdw = kd32.astype(BF16)
            vb = v_ref[...].astype(BF16)
            for h in range(HR):
                ks, vs = _ks(h), _vs(h)
                gst = g_s[h]
                g_b = gst.astype(BF16)
                dkd = _dot(vb[:, vs], g_b, NT)
                dk_ref[:, ks] = dkd * kd_ref[:, ks]
                dv_ref[:, vs] = _dot(kdw[:, ks], g_b)
                av = a_ref[row0 + h:row0 + h + 1, :]
                lg = (_sum11(jnp.sum(kd32[:, ks] * dkd, axis=-1, keepdims=True) * wexp)
                      + C * av[:, 0:1] * _sum11(r_ref[0, h] * gst))
                dlg_ref[row0 + h:row0 + h + 1, :] += jnp.broadcast_to(lg, (1, 128))
                g_s[h] = dr_ref[0, h] + av * gst

        one(kf_ref, vf_ref, rf_ref, drf_ref, kdf_ref, gf, dkf_ref, dvf_ref, 0, C - 1.0 - ri)
        one(kb_ref, vb_ref, rb_ref, drb_ref, kdb_ref, gb, dkb_ref, dvb_ref, HR, ri)

    fwd = lambda w, off=0: pl.BlockSpec((C, w), lambda t: (N - 1 - t, off))
    bwd = lambda w, off=0: pl.BlockSpec((C, w), lambda t: (t, off))
    stf = lambda: pl.BlockSpec((1, HR, DH, DV), lambda t: (N - 1 - t, 0, 0, 0))
    stb = lambda: pl.BlockSpec((1, HR, DH, DV), lambda t: (t, 0, 0, 0))
    return pl.pallas_call(
        body, name="ret_bwd_scan",
        out_shape=[jax.ShapeDtypeStruct((S, 256), F32), jax.ShapeDtypeStruct((S, 256), F32), jax.ShapeDtypeStruct((S, 512), F32),
                   jax.ShapeDtypeStruct((S, 512), F32), jax.ShapeDtypeStruct((8, 128), F32)],
        grid=(N,),
        in_specs=[fwd(256), fwd(512, O_VR // 512), bwd(256), bwd(512, O_VR // 512), stf(), stb(), stf(), stb(),
                  _full((C, 256)), _full((C, 256)), _full((8, 128))],
        out_specs=[fwd(256), bwd(256), fwd(512), bwd(512), _full((8, 128))],
        scratch_shapes=[pltpu.VMEM((HR, DH, DV), F32), pltpu.VMEM((HR, DH, DV), F32)],
        compiler_params=_cp(("arbitrary",)),
    )(kr2, p, kr2, p, rf, rb, drf, drb, kdf, kdb, adec)


def _ret_bwd_final(dk_i, dkf, dkb, dv_i, dvf, dvb, cos, sin):
    S = dk_i.shape[0]
    tm = min(512, S)

    def body(a_ref, b_ref, c_ref, d_ref, e_ref, f_ref, cos_ref, sin_ref, o_ref):
        o_ref[:, :512] = (d_ref[...] + e_ref[...] + f_ref[...]).astype(BF16)
        cos_v, sin_v = cos_ref[...], sin_ref[...]
        for g in range(2):
            sl = slice(128 * g, 128 * g + 128)
            dk = a_ref[:, sl] + b_ref[:, sl] + c_ref[:, sl]
            o_ref[:, 512 + 128 * g:512 + 128 * g + 128] = (_rope_t(dk, cos_v, sin_v) * 0.125).astype(BF16)

    row = lambda w: pl.BlockSpec((tm, w), lambda i: (i, 0))
    return pl.pallas_call(
        body, name="ret_bwd_final", out_shape=jax.ShapeDtypeStruct((S, 768), BF16), grid=(S // tm,),
        in_specs=[row(256), row(256), row(256), row(512), row(512), row(512), row(128), row(128)], out_specs=row(768),
        compiler_params=_cp(("parallel",)),
    )(dk_i, dkf, dkb, dv_i, dvf, dvb, cos, sin)


def _bwd_in(dpm, dpa, dpra, dprb, w_p, x, dout, mod, g_pre, xs):
    S = x.shape[0]
    tm = min(256, S)

    def body(a_ref, b_ref, c_ref, d_ref, w_ref, x_ref, dout_ref, mod_ref, g_ref, gx_ref, sums_ref):
        @pl.when(pl.program_id(0) == 0)
        def _():
            sums_ref[...] = jnp.zeros_like(sums_ref)

        dh = (_dot(a_ref[...], w_ref[:, :O_QA], NT) + _dot(b_ref[...], w_ref[:, O_QA:O_QR], NT)
              + _dot(c_ref[...], w_ref[:, O_QR:O_VR], NT) + _dot(d_ref[...], w_ref[:, O_VR:], NT))
        xv = x_ref[...]
        r = lax.rsqrt(jnp.mean(xv * xv, axis=-1, keepdims=True) + EPS)
        xn = xv * r
        gv = g_ref[...]
        sc1 = 1.0 + mod_ref[1:2, :]
        sums_ref[0:1, :] += jnp.sum(dh, axis=0, keepdims=True)
        sums_ref[1:2, :] += jnp.sum(dh * (xn * gv), axis=0, keepdims=True)
        sums_ref[2:3, :] += jnp.sum(dh * xn, axis=0, keepdims=True) * sc1
        dxn = dh * (gv * sc1)
        gx_ref[...] = dout_ref[...] + r * (dxn - xn * jnp.mean(dxn * xn, axis=-1, keepdims=True))

    row = lambda w: pl.BlockSpec((tm, w), lambda i: (i, 0))
    return _host_call(
        body, xs, name="bwd_in", out_shape=[jax.ShapeDtypeStruct((S, D), F32), jax.ShapeDtypeStruct((8, D), F32)], grid=(S // tm,),
        in_specs=[row(2560), row(768), row(768), row(768), _full((D, P_W)), row(D), row(D), _full((3, D)), _full((1, D))],
        out_specs=[row(D), _full((8, D))], scratch_shapes=[], operands=(dpm, dpa, dpra, dprb, w_p, x, dout, mod, g_pre),
        compiler_params=_cp(("arbitrary",), VMEM_BIG),
    )


SMALL = ("b_ada", "g_pre", "qn_g", "kn_g", "w_dec_f", "w_dec_b", "gn_g", "g_post")


def _small_update(gathered, wmv):
    ns = len(SMALL)

    def body(*refs):
        gin_ref, gmid_ref, ggn_ref, gatt_ref, gl1_ref, gl2_ref = refs[:6]
        wmv_refs = refs[6:6 + 3 * ns]
        loss_ref = refs[6 + 3 * ns]
        out_refs = refs[7 + 3 * ns:]

        def dsum(ref, r=None):
            rows = slice(None) if r is None else slice(r, r + 1)
            acc = ref[0, rows, :]
            for d in range(1, NDEV):
                acc = acc + ref[d, rows, :]
            return acc

        s_lg = dsum(gl1_ref) + dsum(gl2_ref)
        loss_ref[...] = (0.5 / D) * jnp.sum(dsum(gmid_ref, 2), axis=-1, keepdims=True)
        eye = lax.broadcasted_iota(jnp.int32, (8, 128), 0) == lax.broadcasted_iota(jnp.int32, (8, 128), 1)
        dlg = jnp.sum(jnp.where(eye, s_lg, 0.0), axis=0, keepdims=True)
        w_f, w_b = wmv_refs[3 * SMALL.index("w_dec_f")][...], wmv_refs[3 * SMALL.index("w_dec_b")][...]
        s_q, s_k = dsum(gatt_ref, 0), dsum(gatt_ref, 1)
        grads = dict(
            b_ada=jnp.concatenate([dsum(gin_ref, 0), dsum(gin_ref, 1), dsum(gmid_ref, 0)], axis=1),
            g_pre=dsum(gin_ref, 2), g_post=dsum(gmid_ref, 1), gn_g=dsum(ggn_ref),
            qn_g=s_q[:, :DH] + s_q[:, DH:], kn_g=s_k[:, :DH] + s_k[:, DH:],
            w_dec_f=dlg[:, 0:HR] * _sigmoid(-w_f), w_dec_b=dlg[:, HR:2 * HR] * _sigmoid(-w_b))
        for i, nme in enumerate(SMALL):
            g = grads[nme]
            w_ref, m_ref, v_ref = wmv_refs[3 * i:3 * i + 3]
            g_ref, d_ref, nm_ref, nv_ref = out_refs[4 * i:4 * i + 4]
            g_ref[...] = g
            m2 = ADAM_B1 * m_ref[...] + (1.0 - ADAM_B1) * g
            v2 = ADAM_B2 * v_ref[...] + (1.0 - ADAM_B2) * jnp.square(g)
            m_hat = m2 / (1.0 - ADAM_B1 ** ADAM_STEP)
            v_hat = v2 / (1.0 - ADAM_B2 ** ADAM_STEP)
            d_ref[...] = -ADAM_LR * (m_hat / (jnp.sqrt(v_hat) + ADAM_EPS) + ADAM_WD * w_ref[...])
            nm_ref[...] = m2
            nv_ref[...] = v2

    out_shape = [jax.ShapeDtypeStruct((1, 1), F32)]
    for i in range(ns):
        out_shape += [jax.ShapeDtypeStruct(wmv[3 * i].shape, F32)] * 4
    return pl.pallas_call(body, name="small_update", out_shape=out_shape)(*gathered, *wmv)


def _adamw(parts, w, m, v, name):
    n, R, L = parts.shape
    tr = 256 if (R % 256 == 0 and R > 256) else R

    def body(p_ref, w_ref, m_ref, v_ref, g_ref, d_ref, nm_ref, nv_ref):
        g = p_ref[0].astype(F32)
        for k in range(1, n):
            g = g + p_ref[k].astype(F32)
        g_ref[...] = g
        m2 = ADAM_B1 * m_ref[...] + (1.0 - ADAM_B1) * g
        v2 = ADAM_B2 * v_ref[...] + (1.0 - ADAM_B2) * jnp.square(g)
        m_hat = m2 / (1.0 - ADAM_B1 ** ADAM_STEP)
        v_hat = v2 / (1.0 - ADAM_B2 ** ADAM_STEP)
        d_ref[...] = -ADAM_LR * (m_hat / (jnp.sqrt(v_hat) + ADAM_EPS) + ADAM_WD * w_ref[...])
        nm_ref[...] = m2
        nv_ref[...] = v2

    blk = pl.BlockSpec((tr, L), lambda i: (i, 0))
    o = jax.ShapeDtypeStruct((R, L), F32)
    return pl.pallas_call(
        body, name=name, out_shape=[o, o, o, o], grid=(R // tr,),
        in_specs=[pl.BlockSpec((n, tr, L), lambda i: (0, i, 0)), blk, blk, blk], out_specs=[blk, blk, blk, blk],
        compiler_params=_cp(("parallel",), VMEM_BIG),
    )(parts, w, m, v)


def _rope_tables(S):
    f = np.float32
    t = np.arange(S)
    row, col = (t // 64).astype(f), (t % 64).astype(f)
    half = DH // 2
    inv = np.power(f(ROPE_THETA), -np.arange(0, half, 2, dtype=f) / f(half)).astype(f)
    ar, ac = (row[:, None] * inv[None, :]).astype(f), (col[:, None] * inv[None, :]).astype(f)
    cos64 = np.concatenate([np.cos(ar), np.cos(ar), np.cos(ac), np.cos(ac)], axis=1).astype(f)
    sin64 = np.concatenate([-np.sin(ar), np.sin(ar), -np.sin(ac), np.sin(ac)], axis=1).astype(f)
    return jnp.asarray(np.tile(cos64, (1, 2))), jnp.asarray(np.tile(sin64, (1, 2)))


def _to_p_order(w_orig):
    return jnp.concatenate([w_orig[:, ORIG[n][0]:ORIG[n][1]] for n in P_ORDER], axis=1)


def _pad_lanes(v, n):
    return jnp.pad(v, ((0, 0), (0, n - v.shape[1])))


def kernel(x, c, w_ada, b_ada, g_pre, w_in, qn_g, kn_g, w_dec_f, w_dec_b, gn_g, w_pa, w_pr, w_out, g_post, loss_target, m_w_ada, m_b_ada, m_g_pre, m_w_in, m_qn_g, m_kn_g, m_w_dec_f, m_w_dec_b, m_gn_g, m_w_pa, m_w_pr, m_w_out, m_g_post, v_w_ada, v_b_ada, v_g_pre, v_w_in, v_qn_g, v_kn_g, v_w_dec_f, v_w_dec_b, v_gn_g, v_w_pa, v_w_pr, v_w_out, v_g_post):
    S = x.shape[1]
    me = 4 * lax.axis_index("x") + 2 * lax.axis_index("y") + lax.axis_index("c")
    xs, tgt = x[0], loss_target[0]
    ncol_ada = w_ada.shape[2]
    ncol_in = w_in.shape[2]

    b_ada_s = lax.dynamic_slice(b_ada, (0, me * ncol_ada), (1, ncol_ada))
    mod_all, c_act, (wg_in, wg_pa, wg_pr, wg_out) = _prologue(
        jnp.pad(c, ((0, 7), (0, 0))), w_ada[0], b_ada_s,
        [w_in[0].astype(BF16), w_pa[0].astype(BF16), w_pr[0].astype(BF16), w_out[0].astype(BF16)])
    mod = lax.dynamic_index_in_dim(mod_all, me, axis=1, keepdims=False).reshape(3, D)
    w_p = _to_p_order(wg_in.transpose(1, 0, 2).reshape(D, NDEV * ncol_in))
    w_pa_f = wg_pa.transpose(1, 0, 2).reshape(512, D)
    w_pr_f = wg_pr.transpose(1, 0, 2).reshape(512, D)
    w_out_f = wg_out.reshape(D, D)

    cos, sin = _rope_tables(S)
    qg, kg = jnp.tile(qn_g, (1, 2)), jnp.tile(kn_g, (1, 2))

    p, h = _fwd_in(xs, mod, g_pre, w_p)
    qt, kh, kt, vh, vta, qr2, kr2 = _prep(p, cos, sin, qg, kg)
    o_att, o_t, lse = _attn_fwd(qt, kh, vta)
    dc, qdf, qdb, kdf, kdb, adec = _ret_tables(w_dec_f, w_dec_b)
    rf, rb = _ret_states(kr2, p, kdf, kdb, adec)
    yr = _ret_out(qr2, kr2, p, rf, rb, dc, qdf, qdb, gn_g)

    dout, do, dpm, dyr, mb, dub, yab, dab, drb_, sums_mid = _mid(xs, tgt, mod, g_post, o_att, p, yr, w_pa_f, w_pr_f, w_out_f)
    gw_out = _mm_tn(mb, dub, "gw_out")
    gw_pa = _mm_tn(yab, dab, "gw_pa")
    gw_pr = _mm_tn(yr, drb_, "gw_pr")
    gi_m = _mm_tn(h, dpm, "gw_in_mid")

    def shards(cols, nd):
        return cols.astype(BF16).reshape(D, nd, ncol_in).transpose(1, 0, 2)

    all_dev = tuple(range(NDEV))
    st_a, tok_a = _xchg_start([
        (gw_out.astype(BF16).reshape(NDEV, 128, D), all_dev),
        (gw_pa.astype(BF16).reshape(512, NDEV, 128).transpose(1, 0, 2), all_dev),
        (gw_pr.astype(BF16).reshape(512, NDEV, 128).transpose(1, 0, 2), all_dev),
        (shards(gi_m[:, 224:2048], 3), (5, 6, 7))], "xchg_start_a")
    (dqt, dkt, dvt), _ = _attn_bwd(qt, kh, kt, vh, do, o_t, lse + tok_a[0, 0], [])
    dpa, gs_att = _attn_prep_bwd(dqt, dkt, dvt, p, cos, sin, qg, kg)
    gi_a = _mm_tn(h, dpa, "gw_in_att")
    st_b, tok_b = _xchg_start([(shards(jnp.concatenate([gi_a, gi_m[:, 2048:2496]], axis=1), 2), (0, 1))], "xchg_start_b")
    (dpra, dk_i, dv_i, drf, drb, dgn, dlg1), _ = _ret_bwd_chunk(qr2, kr2, p, rf, rb, dc, qdf, qdb, gn_g + tok_b[0:1, 0:1], dyr, cos, sin, [])
    dkf, dkb, dvf, dvb, dlg2 = _ret_bwd_scan(kr2, p, rf, rb, drf, drb, kdf, kdb, adec)
    dprb = _ret_bwd_final(dk_i, dkf, dkb, dv_i, dvf, dvb, cos, sin)
    gi_ra = _mm_tn(h, dpra, "gw_in_reta")
    gi_rb = _mm_tn(h, dprb, "gw_in_retb")
    st_c, tok_c = _xchg_start([(shards(jnp.concatenate([gi_m[:, 2496:2560], gi_ra[:, :256], gi_rb[:, 512:768], gi_rb[:, :512],
                                                        gi_ra[:, 256:768], gi_m[:, :224]], axis=1), 3), (2, 3, 4))], "xchg_start_c")
    (grad_x, sums_in), _ = _bwd_in(dpm, dpa, dpra, dprb, w_p, xs, dout, mod, g_pre + tok_c[0:1, 0:1], [])

    gathered = _small_allgather([sums_in, sums_mid, dgn, gs_att, dlg1, dlg2], "ag_small")
    given = dict(b_ada=(b_ada, m_b_ada, v_b_ada), g_pre=(g_pre, m_g_pre, v_g_pre), qn_g=(qn_g, m_qn_g, v_qn_g), kn_g=(kn_g, m_kn_g, v_kn_g),
                 w_dec_f=(w_dec_f, m_w_dec_f, v_w_dec_f), w_dec_b=(w_dec_b, m_w_dec_b, v_w_dec_b), gn_g=(gn_g, m_gn_g, v_gn_g),
                 g_post=(g_post, m_g_post, v_g_post))
    small = _small_update(gathered, [a for nme in SMALL for a in given[nme]])
    loss = small[0][0, 0]

    g_in_all, g_mid_all = gathered[0], gathered[1]
    dmod_all = lax.dynamic_slice(jnp.concatenate([g_in_all[:, 0, :], g_in_all[:, 1, :], g_mid_all[:, 0, :]], axis=1),
                                 (0, me * ncol_ada), (NDEV, ncol_ada))
    g_ada = _mm_tn(c_act, jnp.pad(dmod_all, ((0, 8), (0, 0))).astype(BF16), "gw_ada")

    ada = _adamw(g_ada[None], w_ada[0], m_w_ada[0], v_w_ada[0], "adamw_ada")
    rs_out, rs_pa, rs_pr, rs_in_a = _xchg_wait(st_a, ada[1], "xchg_wait_a")
    (rs_in_b,) = _xchg_wait(st_b, rs_in_a, "xchg_wait_b")
    (rs_in_c,) = _xchg_wait(st_c, rs_in_b, "xchg_wait_c")
    rs_in = lax.switch((me >= 2).astype(jnp.int32) + (me >= 5).astype(jnp.int32),
                       [lambda: rs_in_b, lambda: rs_in_c, lambda: rs_in_a])
    res = dict(
        w_ada=ada,
        w_in=_adamw(rs_in, w_in[0], m_w_in[0], v_w_in[0], "adamw_in"),
        w_pa=_adamw(rs_pa, w_pa[0], m_w_pa[0], v_w_pa[0], "adamw_pa"),
        w_pr=_adamw(rs_pr, w_pr[0], m_w_pr[0], v_w_pr[0], "adamw_pr"),
        w_out=_adamw(rs_out, w_out[0], m_w_out[0], v_w_out[0], "adamw_out"),
    )
    names = ["w_ada", "b_ada", "g_pre", "w_in", "qn_g", "kn_g", "w_dec_f", "w_dec_b", "gn_g", "w_pa", "w_pr", "w_out", "g_post"]
    outs = [[], [], [], []]
    for nme in names:
        for q in range(4):
            if nme in res:
                outs[q].append(res[nme][q][None])
            else:
                outs[q].append(small[1 + 4 * SMALL.index(nme) + q])
    return (loss, grad_x[None], *outs[0], *outs[1], *outs[2], *outs[3])
```

```python
import jax
import jax.numpy as jnp
import numpy as np
from jax import lax
from jax.experimental import pallas as pl
from jax.experimental.pallas import tpu as pltpu

F32, BF16 = jnp.float32, jnp.bfloat16
D = 1024
DH = 64
DHA = 80
DV = 128
LOG2E = 1.4426950408889634
LN2 = 0.6931471805599453
HR = 4
CH = 128
EPS = 1e-6
ROPE_THETA = 10000.0
NDEV = 8
O_GL, O_ZA, O_QA, O_KA, O_VA, O_QR, O_ZR, O_VR, O_KR, P_W = 0, 2048, 2560, 3072, 3200, 3328, 3584, 4096, 4608, 4864
ORIG = dict(qa=(0, 512), ka=(512, 640), va=(640, 768), za=(768, 1280), qr=(1280, 1536), kr=(1536, 1792),
            vr=(1792, 2304), zr=(2304, 2816), gl=(2816, 4864))
P_ORDER = ("gl", "za", "qa", "ka", "va", "qr", "zr", "vr", "kr")
ADAM_LR, ADAM_B1, ADAM_B2, ADAM_EPS, ADAM_WD, ADAM_STEP = 0.001, 0.9, 0.999, 1e-08, 0.01, 10
VMEM_BIG = 56 * 1024 * 1024
MESH = pl.DeviceIdType.MESH

NT = (((1,), (1,)), ((), ()))
TN = (((0,), (0,)), ((), ()))


def _dot(a, b, dims=None):
    if dims is None:
        return jnp.dot(a, b, preferred_element_type=F32)
    return lax.dot_general(a, b, dims, preferred_element_type=F32)


def _cp(sem=None, vmem=None):
    kw = {}
    if sem is not None:
        kw["dimension_semantics"] = sem
    if vmem is not None:
        kw["vmem_limit_bytes"] = vmem
    return pltpu.CompilerParams(**kw)


def _sigmoid(z):
    return 1.0 / (1.0 + jnp.exp(-z))


def _sum11(m):
    return jnp.sum(jnp.sum(m, axis=-1, keepdims=True), axis=0, keepdims=True)


def _full(shape):
    n = len(shape)
    return pl.BlockSpec(shape, lambda *_: (0,) * n)


def _my_pos():
    return lax.axis_index("x"), lax.axis_index("y"), lax.axis_index("c")


def _peer(k, x, y, c):
    return ((1 - x) if k & 4 else x, (1 - y) if k & 2 else y, (1 - c) if k & 1 else c)


def _small_allgather(vs, name):
    n = len(vs)

    def body(*refs):
        v_refs, out_refs = refs[:n], refs[n:2 * n]
        send_sems, recv_sems = refs[2 * n:]
        x, y, c = _my_pos()
        me = 4 * x + 2 * y + c
        cps = []
        for a in range(n):
            out_refs[a][me] = v_refs[a][...]
            for k in range(1, NDEV):
                cp = pltpu.make_async_remote_copy(src_ref=v_refs[a], dst_ref=out_refs[a].at[me], send_sem=send_sems.at[a, k - 1],
                                                  recv_sem=recv_sems.at[a, k - 1], device_id=_peer(k, x, y, c), device_id_type=MESH)
                cp.start()
                cps.append(cp)
        for cp in cps:
            cp.wait()

    vm = pl.BlockSpec(memory_space=pltpu.VMEM)
    return pl.pallas_call(
        body, name=name, out_shape=[jax.ShapeDtypeStruct((NDEV,) + v.shape, v.dtype) for v in vs],
        in_specs=[vm] * n, out_specs=[vm] * n,
        scratch_shapes=[pltpu.SemaphoreType.DMA((n, NDEV - 1)), pltpu.SemaphoreType.DMA((n, NDEV - 1))],
    )(*vs)


def _prologue(c8, w_ada_s, b_ada_s, arrs):
    n = len(arrs)
    ncol = w_ada_s.shape[1]

    def body(*refs):
        c_ref, wa_ref, ba_ref = refs[:3]
        ins = refs[3:3 + n]
        mod_ref, cact_ref = refs[3 + n:5 + n]
        outs = refs[5 + n:5 + 2 * n]
        call_ref, send_sems, recv_sems, local_sems, s_send, s_recv = refs[5 + 2 * n:]
        x, y, c = _my_pos()
        me, sibling = (x, y, c), (x, y, 1 - c)
        chips = [(1 - x, y), (x, 1 - y), (1 - x, 1 - y)]
        me_i = 4 * x + 2 * y + c

        def small_gather(src_ref, dst_ref, row):
            cps = []
            for k in range(1, NDEV):
                cp = pltpu.make_async_remote_copy(src_ref=src_ref, dst_ref=dst_ref.at[me_i], send_sem=s_send.at[row, k - 1],
                                                  recv_sem=s_recv.at[row, k - 1], device_id=_peer(k, x, y, c), device_id_type=MESH)
                cp.start()
                cps.append(cp)
            return cps

        def blk(a, px, py, pc):
            return outs[a].at[4 * px + 2 * py + pc]

        def copy(a, k, block, to, src=None):
            return pltpu.make_async_remote_copy(src_ref=blk(a, *block) if src is None else src, dst_ref=blk(a, *block),
                                                send_sem=send_sems.at[a, k], recv_sem=recv_sems.at[a, k], device_id=to, device_id_type=MESH)

        call_ref[me_i] = c_ref[...]
        for cp in small_gather(c_ref, call_ref, 0):
            cp.wait()

        local, sent = [], []
        for a in range(n):
            mine = pltpu.make_async_copy(ins[a], blk(a, *me), local_sems.at[a])
            mine.start()
            local.append(mine)
            first = [copy(a, 0, me, sibling, src=ins[a])] + [copy(a, 1 + j, me, (*chip, c), src=ins[a]) for j, chip in enumerate(chips)]
            for cp in first:
                cp.start()
            sent += first

        cv = call_ref[:, 0, :]
        ca = jnp.concatenate([cv * _sigmoid(cv), jnp.zeros_like(cv)], axis=0).astype(BF16)
        cact_ref[...] = ca
        mod_ref[me_i] = (_dot(ca, wa_ref[...].astype(BF16)) + ba_ref[...])[:8]
        mod_copies = small_gather(mod_ref.at[me_i], mod_ref, 1)

        for j, chip in enumerate(chips):
            for a in range(n):
                copy(a, 1 + j, (*chip, c), me).wait_recv()
                cp = copy(a, 4 + j, (*chip, c), sibling)
                cp.start()
                sent.append(cp)
        for a in range(n):
            copy(a, 0, sibling, me).wait_recv()
            for j, chip in enumerate(chips):
                copy(a, 4 + j, (*chip, 1 - c), me).wait_recv()
        for cp in sent:
            cp.wait_send()
        for cp in local + mod_copies:
            cp.wait()

    vm, hbm = pl.BlockSpec(memory_space=pltpu.VMEM), pl.BlockSpec(memory_space=pl.ANY)
    res = pl.pallas_call(
        body, name="prologue",
        out_shape=[jax.ShapeDtypeStruct((NDEV, 8, ncol), F32), jax.ShapeDtypeStruct((16, D), BF16)]
        + [jax.ShapeDtypeStruct((NDEV,) + a.shape, a.dtype) for a in arrs],
        in_specs=[vm, vm, vm] + [hbm] * n, out_specs=[vm, vm] + [hbm] * n,
        scratch_shapes=[pltpu.VMEM((NDEV, 8, D), F32), pltpu.SemaphoreType.DMA((n, NDEV - 1)), pltpu.SemaphoreType.DMA((n, NDEV - 1)),
                        pltpu.SemaphoreType.DMA((n,)), pltpu.SemaphoreType.DMA((2, NDEV - 1)), pltpu.SemaphoreType.DMA((2, NDEV - 1))],
    )(c8, w_ada_s, b_ada_s, *arrs)
    return res[0], res[1], res[2:]


def _in_set(idx, dests):
    p = idx == dests[0]
    for d in dests[1:]:
        p = jnp.logical_or(p, idx == d)
    return p


def _host_call(body, xs, *, name, grid, in_specs, out_specs, out_shape, scratch_shapes, operands, compiler_params):
    nx, nin, nout, nscr = len(xs), len(operands), len(out_shape), len(scratch_shapes)
    if nx == 0:
        res = pl.pallas_call(body, name=name, grid=grid, in_specs=in_specs, out_specs=out_specs, out_shape=out_shape,
                             scratch_shapes=scratch_shapes, compiler_params=compiler_params)(*operands)
        return res, []
    ops, specs, aliases = list(operands), list(in_specs), {}
    oshape, ospecs = list(out_shape), list(out_specs)
    any_spec = pl.BlockSpec(memory_space=pl.ANY)
    for a, (send, dests, recv) in enumerate(xs):
        ops.append(send)
        specs.append(any_spec)
        if recv is not None:
            aliases[len(ops)] = nout + a
            ops.append(recv)
            specs.append(any_spec)
            oshape.append(jax.ShapeDtypeStruct(recv.shape, recv.dtype))
        else:
            oshape.append(jax.ShapeDtypeStruct((NDEV,) + send.shape[1:], send.dtype))
        ospecs.append(any_spec)
    ntot_in = len(ops)

    def wrapped(*refs):
        host_in = refs[:nin]
        sends, pos = [], nin
        for (_, _, recv) in xs:
            sends.append(refs[pos])
            pos += 1 if recv is None else 2
        host_out = refs[ntot_in:ntot_in + nout]
        recvs = refs[ntot_in + nout:ntot_in + nout + nx]
        host_scr = refs[ntot_in + nout + nx:ntot_in + nout + nx + nscr]
        send_sems, recv_sems, local_sems = refs[ntot_in + nout + nx + nscr:]
        first = pl.program_id(0) == 0
        last = pl.program_id(0) == grid[0] - 1
        for ax in range(1, len(grid)):
            first = jnp.logical_and(first, pl.program_id(ax) == 0)
            last = jnp.logical_and(last, pl.program_id(ax) == grid[ax] - 1)
        x, y, c = _my_pos()
        me = 4 * x + 2 * y + c

        def each(fn_remote, fn_local):
            for a, (_, dests, _) in enumerate(xs):
                lo, nd = dests[0], len(dests)
                for k in range(1, NDEV):
                    px, py, pc = _peer(k, x, y, c)
                    pidx = 4 * px + 2 * py + pc
                    cp = pltpu.make_async_remote_copy(src_ref=sends[a].at[jnp.clip(pidx - lo, 0, nd - 1)], dst_ref=recvs[a].at[me],
                                                      send_sem=send_sems.at[a, k - 1], recv_sem=recv_sems.at[a, k - 1],
                                                      device_id=(px, py, pc), device_id_type=MESH)
                    fn_remote(cp, _in_set(pidx, dests), _in_set(me, dests))
                lc = pltpu.make_async_copy(sends[a].at[jnp.clip(me - lo, 0, nd - 1)], recvs[a].at[me], local_sems.at[a])
                fn_local(lc, _in_set(me, dests))

        def start_remote(cp, to_dest, _):
            pl.when(jnp.logical_and(first, to_dest))(cp.start)

        def start_local(lc, i_am_dest):
            pl.when(jnp.logical_and(first, i_am_dest))(lc.start)

        def wait_remote(cp, to_dest, i_am_dest):
            pl.when(jnp.logical_and(last, to_dest))(cp.wait_send)
            pl.when(jnp.logical_and(last, i_am_dest))(cp.wait_recv)

        def wait_local(lc, i_am_dest):
            pl.when(jnp.logical_and(last, i_am_dest))(lc.wait)

        each(start_remote, start_local)
        body(*host_in, *host_out, *host_scr)
        each(wait_remote, wait_local)

    res = pl.pallas_call(
        wrapped, name=name, grid=grid, in_specs=specs, out_specs=ospecs, out_shape=oshape, input_output_aliases=aliases,
        scratch_shapes=list(scratch_shapes) + [pltpu.SemaphoreType.DMA((nx, NDEV - 1)), pltpu.SemaphoreType.DMA((nx, NDEV - 1)),
                                               pltpu.SemaphoreType.DMA((nx,))],
        compiler_params=compiler_params,
    )(*ops)
    return res[:nout], res[nout:]


_HBM = pl.BlockSpec(memory_space=pltpu.HBM)
_SEM = pl.BlockSpec(memory_space=pltpu.SEMAPHORE)


def _xchg_copies(xs_dests, sends, lands, ssem, rsem, lsem):
    x, y, c = _my_pos()
    me = 4 * x + 2 * y + c
    remote, local = [], []
    for a, dests in enumerate(xs_dests):
        lo, nd = dests[0], sends[a].shape[0]
        for k in range(1, NDEV):
            px, py, pc = _peer(k, x, y, c)
            pidx = 4 * px + 2 * py + pc
            cp = pltpu.make_async_remote_copy(src_ref=sends[a].at[jnp.clip(pidx - lo, 0, nd - 1)], dst_ref=lands[a].at[me],
                                              send_sem=ssem.at[a * (NDEV - 1) + k - 1], recv_sem=rsem.at[a * (NDEV - 1) + k - 1],
                                              device_id=(px, py, pc), device_id_type=MESH)
            remote.append((cp, _in_set(pidx, dests), _in_set(me, dests)))
        lc = pltpu.make_async_copy(sends[a].at[jnp.clip(me - lo, 0, nd - 1)], lands[a].at[me], lsem.at[a])
        local.append((lc, _in_set(me, dests)))
    return remote, local


def _xchg_start(xs, name, lands=None):
    n = len(xs)
    dests = [d for _, d in xs]
    sends = [pltpu.with_memory_space_constraint(s, pltpu.HBM) for s, _ in xs]
    if lands is None:
        lands = [pltpu.with_memory_space_constraint(lax.empty((NDEV,) + s.shape[1:], s.dtype), pltpu.HBM) for s, _ in xs]

    def body(*refs):
        send_refs, land_refs = refs[:n], refs[n:2 * n]
        ssem, rsem, lsem = refs[2 * n:2 * n + 3]
        token = refs[-1]
        remote, local = _xchg_copies(dests, send_refs, land_refs, ssem, rsem, lsem)
        for cp, to_dest, _ in remote:
            pl.when(to_dest)(cp.start)
        for lc, i_am_dest in local:
            pl.when(i_am_dest)(lc.start)
        token[...] = jnp.zeros_like(token)

    res = pl.pallas_call(
        body, name=name,
        out_shape=[pltpu.SemaphoreType.DMA((n * (NDEV - 1),)), pltpu.SemaphoreType.DMA((n * (NDEV - 1),)), pltpu.SemaphoreType.DMA((n,))]
        + [pltpu.HBM(a.shape, a.dtype) for a in list(sends) + list(lands)] + [jax.ShapeDtypeStruct((8, 128), F32)],
        in_specs=[_HBM] * (2 * n), out_specs=[_SEM, _SEM, _SEM] + [_HBM] * (2 * n) + [pl.BlockSpec(memory_space=pltpu.VMEM)],
        input_output_aliases={i: 3 + i for i in range(2 * n)},
        compiler_params=pltpu.CompilerParams(has_side_effects=pltpu.SideEffectType.DATAFLOW_SIDE_EFFECTING),
    )(*sends, *lands)
    return dict(sems=res[0:3], sends=res[3:3 + n], lands=res[3 + n:3 + 2 * n], dests=dests), res[-1]


def _xchg_wait(states, lands, land_of, after, name):
    flat = []
    for st in states:
        flat += list(st["sends"]) + list(st["sems"])
    nl = len(lands)

    def body(*refs):
        land_refs = refs[:nl]
        pos = nl
        for s, st in enumerate(states):
            n = len(st["dests"])
            send_refs = refs[pos:pos + n]
            ssem, rsem, lsem = refs[pos + n:pos + n + 3]
            pos += n + 3
            remote, local = _xchg_copies(st["dests"], send_refs, [land_refs[i] for i in land_of[s]], ssem, rsem, lsem)
            for cp, to_dest, i_am_dest in remote:
                pl.when(to_dest)(cp.wait_send)
                pl.when(i_am_dest)(cp.wait_recv)
            for lc, i_am_dest in local:
                pl.when(i_am_dest)(lc.wait)

    in_specs = [_HBM] * nl
    for st in states:
        in_specs += [_HBM] * len(st["dests"]) + [_SEM, _SEM, _SEM]
    return pl.pallas_call(
        body, name=name, out_shape=[pltpu.HBM(a.shape, a.dtype) for a in lands],
        in_specs=in_specs + [pl.BlockSpec(memory_space=pl.ANY)], out_specs=[_HBM] * nl,
        input_output_aliases={i: i for i in range(nl)},
        compiler_params=pltpu.CompilerParams(has_side_effects=pltpu.SideEffectType.DATAFLOW_SIDE_EFFECTING),
    )(*lands, *flat, after)


def _mm_tn(a, b, name):
    S, M = a.shape
    N = b.shape[1]
    tk = min(2048, S)
    tn = N if N <= 768 else (640 if N % 640 == 0 else 512)
    nk = S // tk

    def body(a_ref, b_ref, o_ref):
        @pl.when(pl.program_id(1) == 0)
        def _():
            o_ref[...] = jnp.zeros_like(o_ref)
        o_ref[...] += _dot(a_ref[...], b_ref[...], TN)

    return pl.pallas_call(
        body, name=name, out_shape=jax.ShapeDtypeStruct((M, N), F32), grid=(N // tn, nk),
        in_specs=[pl.BlockSpec((tk, M), lambda j, k: (k, 0)), pl.BlockSpec((tk, tn), lambda j, k: (k, j))],
        out_specs=pl.BlockSpec((M, tn), lambda j, k: (0, j)),
        compiler_params=_cp(("parallel", "arbitrary"), VMEM_BIG),
    )(a, b)


def _fwd_in(x, mod, g_pre, w_p):
    S = x.shape[0]
    tm = min(512, S)

    def body(x_ref, mod_ref, g_ref, w_ref, p_ref, h_ref):
        xv = x_ref[...]
        r = lax.rsqrt(jnp.mean(xv * xv, axis=-1, keepdims=True) + EPS)
        h = (((xv * r) * g_ref[...]) * (1.0 + mod_ref[1:2, :]) + mod_ref[0:1, :]).astype(BF16)
        h_ref[...] = h
        p_ref[...] = _dot(h, w_ref[...]).astype(BF16)

    return pl.pallas_call(
        body, name="fwd_in", out_shape=[jax.ShapeDtypeStruct((S, P_W), BF16), jax.ShapeDtypeStruct((S, D), BF16)],
        grid=(S // tm,),
        in_specs=[pl.BlockSpec((tm, D), lambda i: (i, 0)), _full((3, D)), _full((1, D)), _full((D, P_W))],
        out_specs=[pl.BlockSpec((tm, P_W), lambda i: (i, 0)), pl.BlockSpec((tm, D), lambda i: (i, 0))],
        compiler_params=_cp(("parallel",), VMEM_BIG),
    )(x, mod, g_pre, w_p)


def _swap16(v):
    lane = lax.broadcasted_iota(jnp.int32, v.shape, 1)
    return jnp.where((lane % 32) < 16, pltpu.roll(v, 112, 1), pltpu.roll(v, 16, 1))


def _rope(v, cos, sin):
    return v * cos + _swap16(v) * sin


def _rope_t(v, cos, sin):
    return v * cos - _swap16(v) * sin


def _head_mean(v):
    lo = lax.broadcasted_iota(jnp.int32, v.shape, 1) < 64
    m0 = jnp.sum(jnp.where(lo, v, 0.0), axis=-1, keepdims=True)
    m1 = jnp.sum(jnp.where(lo, 0.0, v), axis=-1, keepdims=True)
    return jnp.where(lo, m0, m1) * (1.0 / 64.0)


def _prep(p, cos, sin, qg, kg):
    S = p.shape[0]
    tm = min(512, S)

    def body(qa_ref, kv_ref, qr_ref, kr_ref, cos_ref, sin_ref, qg_ref, kg_ref, qt_ref, kh_ref, kt_ref, vh_ref, vta_ref, qr2_ref, kr2_ref):
        cos_v, sin_v = cos_ref[...], sin_ref[...]
        for g in range(4):
            xv = qa_ref[:, 128 * g:128 * g + 128].astype(F32)
            r = lax.rsqrt(_head_mean(xv * xv) + EPS)
            yt = (_rope((xv * r) * qg_ref[...], cos_v, sin_v) * (0.125 * LOG2E)).T
            qt_ref[2 * g] = yt[:DH].astype(BF16)
            qt_ref[2 * g + 1] = yt[DH:].astype(BF16)
        xv = kv_ref[:, :128].astype(F32)
        r = lax.rsqrt(_head_mean(xv * xv) + EPS)
        yv = _rope((xv * r) * kg_ref[...], cos_v, sin_v)
        kh_ref[0] = yv[:, :64].astype(BF16)
        kh_ref[1] = yv[:, 64:].astype(BF16)
        yt = yv.T
        kt_ref[0] = yt[:DH].astype(BF16)
        kt_ref[1] = yt[DH:].astype(BF16)
        vv = kv_ref[:, 128:].astype(F32)
        vh_ref[0] = vv[:, :64].astype(BF16)
        vh_ref[1] = vv[:, 64:].astype(BF16)
        vt = vv.T
        tail = (lax.broadcasted_iota(jnp.int32, (DHA - DH, tm), 0) == 0).astype(BF16)
        for kvh in range(2):
            vta_ref[kvh, 0:DH, :] = vt[DH * kvh:DH * kvh + DH].astype(BF16)
            vta_ref[kvh, DH:DHA, :] = tail
        for g in range(2):
            sl = slice(128 * g, 128 * g + 128)
            qr2_ref[:, sl] = _rope(qr_ref[:, sl].astype(F32), cos_v, sin_v)
            kr2_ref[:, sl] = _rope(kr_ref[:, sl].astype(F32), cos_v, sin_v) * 0.125

    hm = lambda n: pl.BlockSpec((n, tm, DH), lambda i: (0, i, 0))
    ht = lambda n, r: pl.BlockSpec((n, r, tm), lambda i: (0, 0, i))
    return pl.pallas_call(
        body, name="prep",
        out_shape=[jax.ShapeDtypeStruct((8, DH, S), BF16), jax.ShapeDtypeStruct((2, S, DH), BF16), jax.ShapeDtypeStruct((2, DH, S), BF16),
                   jax.ShapeDtypeStruct((2, S, DH), BF16), jax.ShapeDtypeStruct((2, DHA, S), BF16),
                   jax.ShapeDtypeStruct((S, 256), F32), jax.ShapeDtypeStruct((S, 256), F32)],
        grid=(S // tm,),
        in_specs=[pl.BlockSpec((tm, 512), lambda i: (i, O_QA // 512)), pl.BlockSpec((tm, 256), lambda i: (i, O_KA // 256)),
                  pl.BlockSpec((tm, 256), lambda i: (i, O_QR // 256)), pl.BlockSpec((tm, 256), lambda i: (i, O_KR // 256)),
                  pl.BlockSpec((tm, 128), lambda i: (i, 0)), pl.BlockSpec((tm, 128), lambda i: (i, 0)), _full((1, 128)), _full((1, 128))],
        out_specs=[ht(8, DH), hm(2), ht(2, DH), hm(2), ht(2, DHA), pl.BlockSpec((tm, 256), lambda i: (i, 0)), pl.BlockSpec((tm, 256), lambda i: (i, 0))],
        compiler_params=_cp(("parallel",)),
    )(p, p, p, p, cos, sin, qg, kg)


def _attn_fwd(qt, kh, vta):
    S = qt.shape[2]
    tq, tk = min(1024, S), min(512, S)
    nj = S // tk

    def body(q_ref, k_ref, v_ref, o_ref, ot_ref, lse_ref, m_s, acc_s):
        j = pl.program_id(1)

        @pl.when(j == 0)
        def _():
            m_s[...] = jnp.full_like(m_s, -jnp.inf)
            acc_s[...] = jnp.zeros_like(acc_s)

        m_all = m_s[...]
        st = {0: _dot(k_ref[0], q_ref[0])}
        m_new, acc_new = [], []
        for h in range(8):
            if h + 1 < 8:
                st[h + 1] = _dot(k_ref[(h + 1) // 4], q_ref[h + 1])
            m_old = m_all[h:h + 1, :]
            mn = jnp.maximum(m_old, jnp.max(st[h], axis=0, keepdims=True))
            pt = jnp.exp2(st[h] - mn).astype(BF16)
            acc_new.append(jnp.exp2(m_old - mn) * acc_s[h] + _dot(v_ref[h // 4], pt))
            m_new.append(mn)
            del st[h]
        for h in range(8):
            acc_s[h] = acc_new[h]
            m_s[h:h + 1, :] = m_new[h]

        @pl.when(j == nj - 1)
        def _():
            for h in range(8):
                ot = acc_s[h, 0:DH, :] / acc_s[h, DH:DH + 1, :]
                ot_ref[h] = ot
                o_ref[:, DH * h:DH * h + DH] = ot.T
                lse_ref[h // 4, h % 4:h % 4 + 1, :] = m_s[h:h + 1, :] + jnp.log2(acc_s[h, DH:DH + 1, :])

    return pl.pallas_call(
        body, name="attn_fwd",
        out_shape=[jax.ShapeDtypeStruct((S, 512), F32), jax.ShapeDtypeStruct((8, DH, S), F32), jax.ShapeDtypeStruct((2, 4, S), F32)],
        grid=(S // tq, nj),
        in_specs=[pl.BlockSpec((8, DH, tq), lambda i, j: (0, 0, i)), pl.BlockSpec((2, tk, DH), lambda i, j: (0, j, 0)),
                  pl.BlockSpec((2, DHA, tk), lambda i, j: (0, 0, j))],
        out_specs=[pl.BlockSpec((tq, 512), lambda i, j: (i, 0)), pl.BlockSpec((8, DH, tq), lambda i, j: (0, 0, i)),
                   pl.BlockSpec((2, 4, tq), lambda i, j: (0, 0, i))],
        scratch_shapes=[pltpu.VMEM((8, tq), F32), pltpu.VMEM((8, DHA, tq), F32)],
        compiler_params=_cp(("parallel", "arbitrary"), VMEM_BIG),
    )(qt, kh, vta)


def _ret_tables(wf, wb):
    C = CH

    def body(wf_ref, wb_ref, dc_ref, qdf_ref, qdb_ref, kdf_ref, kdb_ref, a_ref):
        def logsig(w):
            z = jnp.exp(-jnp.abs(w))
            u = 1.0 + z
            l1p = jnp.where(u == 1.0, z, jnp.log(u) * (z / jnp.where(u == 1.0, 1.0, u - 1.0)))
            return jnp.minimum(w, 0.0) - l1p

        lgf, lgb = logsig(wf_ref[...]), logsig(wb_ref[...])
        lane4 = lax.broadcasted_iota(jnp.int32, (1, 4), 1)

        def pick(lg, h):
            return jnp.sum(jnp.where(lane4 == h, lg, 0.0), axis=-1, keepdims=True)

        ii = lax.broadcasted_iota(jnp.int32, (C, C), 0).astype(F32)
        jj = lax.broadcasted_iota(jnp.int32, (C, C), 1).astype(F32)
        dif = ii - jj
        hd = lax.broadcasted_iota(jnp.int32, (C, 256), 1) // DH
        lf_l = jnp.zeros((C, 256), F32)
        lb_l = jnp.zeros((C, 256), F32)
        for h in range(HR):
            lf, lb = pick(lgf, h), pick(lgb, h)
            dc_ref[h] = jnp.where(dif >= 0, jnp.exp(lf * jnp.maximum(dif, 0.0)), jnp.exp(lb * jnp.maximum(-dif, 0.0)))
            lf_l = jnp.where(hd == h, lf, lf_l)
            lb_l = jnp.where(hd == h, lb, lb_l)
            a_ref[h:h + 1, :] = jnp.broadcast_to(jnp.exp(lf * C), (1, 128))
            a_ref[HR + h:HR + h + 1, :] = jnp.broadcast_to(jnp.exp(lb * C), (1, 128))
        ri = lax.broadcasted_iota(jnp.int32, (C, 256), 0).astype(F32)
        qdf_ref[...] = jnp.exp(lf_l * (ri + 1.0))
        qdb_ref[...] = jnp.exp(lb_l * (C - ri))
        kdf_ref[...] = jnp.exp(lf_l * (C - 1.0 - ri))
        kdb_ref[...] = jnp.exp(lb_l * ri)

    t = jax.ShapeDtypeStruct((C, 256), F32)
    return pl.pallas_call(body, name="ret_tables",
                          out_shape=[jax.ShapeDtypeStruct((HR, C, C), F32), t, t, t, t, jax.ShapeDtypeStruct((8, 128), F32)])(wf, wb)


def _ret_states(kr2, p, kdf, kdb, adec):
    S = kr2.shape[0]
    C, N = CH, S // CH

    def body(kf_ref, vf_ref, kb_ref, vb_ref, kdf_ref, kdb_ref, a_ref, rf_ref, rb_ref, sf, sb):
        @pl.when(pl.program_id(0) == 0)
        def _():
            sf[...] = jnp.zeros_like(sf)
            sb[...] = jnp.zeros_like(sb)

        rf_ref[0] = sf[...]
        rb_ref[0] = sb[...]
        kdfw = (kf_ref[...] * kdf_ref[...]).astype(BF16)
        kdbw = (kb_ref[...] * kdb_ref[...]).astype(BF16)
        vf, vb = vf_ref[...].astype(BF16), vb_ref[...].astype(BF16)
        kvf = [_dot(kdfw[:, _ks(h)], vf[:, _vs(h)], TN) for h in range(HR)]
        kvb = [_dot(kdbw[:, _ks(h)], vb[:, _vs(h)], TN) for h in range(HR)]
        for h in range(HR):
            sf[h] = a_ref[h:h + 1, :] * sf[h] + kvf[h]
            sb[h] = a_ref[HR + h:HR + h + 1, :] * sb[h] + kvb[h]

    st = jax.ShapeDtypeStruct((N, HR, DH, DV), F32)
    return pl.pallas_call(
        body, name="ret_states", out_shape=[st, st], grid=(N,),
        in_specs=[pl.BlockSpec((C, 256), lambda t: (t, 0)), pl.BlockSpec((C, 512), lambda t: (t, O_VR // 512)),
                  pl.BlockSpec((C, 256), lambda t: (N - 1 - t, 0)), pl.BlockSpec((C, 512), lambda t: (N - 1 - t, O_VR // 512)),
                  _full((C, 256)), _full((C, 256)), _full((8, 128))],
        out_specs=[pl.BlockSpec((1, HR, DH, DV), lambda t: (t, 0, 0, 0)), pl.BlockSpec((1, HR, DH, DV), lambda t: (N - 1 - t, 0, 0, 0))],
        scratch_shapes=[pltpu.VMEM((HR, DH, DV), F32), pltpu.VMEM((HR, DH, DV), F32)],
        compiler_params=_cp(("arbitrary",)),
    )(kr2, p, kr2, p, kdf, kdb, adec)


def _ks(h):
    return slice(DH * h, DH * h + DH)


def _vs(h):
    return slice(DV * h, DV * h + DV)


def _ret_heads_fwd(qb, kb, vb, qfw, qbw, dc_ref, rf_ref, rb_ref):
    hs = range(HR)
    s = [_dot(qb[:, _ks(h)], kb[:, _ks(h)], NT) for h in hs]
    inter = [_dot(qfw[:, _ks(h)], rf_ref[0, h].astype(BF16)) + _dot(qbw[:, _ks(h)], rb_ref[0, h].astype(BF16)) for h in hs]
    sd = [s[h] * dc_ref[h] for h in hs]
    o = [_dot(sd[h].astype(BF16), vb[:, _vs(h)]) + inter[h] for h in hs]
    return sd, o


def _ret_out(qr2, kr2, p, rf, rb, dc, qdf, qdb, gn):
    S = qr2.shape[0]
    C, N = CH, S // CH

    def body(q_ref, k_ref, v_ref, z_ref, rf_ref, rb_ref, dc_ref, qdf_ref, qdb_ref, gn_ref, yr_ref):
        qv = q_ref[...]
        qb, kb, vb = qv.astype(BF16), k_ref[...].astype(BF16), v_ref[...].astype(BF16)
        qfw, qbw = (qv * qdf_ref[...]).astype(BF16), (qv * qdb_ref[...]).astype(BF16)
        _, o = _ret_heads_fwd(qb, kb, vb, qfw, qbw, dc_ref, rf_ref, rb_ref)
        for h in range(HR):
            vs = _vs(h)
            mu = jnp.mean(o[h], axis=-1, keepdims=True)
            var = jnp.mean(jnp.square(o[h] - mu), axis=-1, keepdims=True)
            on = (o[h] - mu) * lax.rsqrt(var + EPS)
            z = z_ref[:, vs].astype(F32)
            yr_ref[:, vs] = ((on * gn_ref[:, vs]) * (z * _sigmoid(z))).astype(BF16)

    return pl.pallas_call(
        body, name="ret_out", out_shape=jax.ShapeDtypeStruct((S, 512), BF16), grid=(N,),
        in_specs=[pl.BlockSpec((C, 256), lambda t: (t, 0)), pl.BlockSpec((C, 256), lambda t: (t, 0)),
                  pl.BlockSpec((C, 512), lambda t: (t, O_VR // 512)), pl.BlockSpec((C, 512), lambda t: (t, O_ZR // 512)),
                  pl.BlockSpec((1, HR, DH, DV), lambda t: (t, 0, 0, 0)), pl.BlockSpec((1, HR, DH, DV), lambda t: (t, 0, 0, 0)),
                  _full((HR, C, C)), _full((C, 256)), _full((C, 256)), _full((1, 512))],
        out_specs=pl.BlockSpec((C, 512), lambda t: (t, 0)),
        compiler_params=_cp(("parallel",)),
    )(qr2, kr2, p, p, rf, rb, dc, qdf, qdb, gn)


def _mid(x, tgt, mod, g_post, o_att, p, yr, w_pa, w_pr, w_out):
    S = x.shape[0]
    tm = min(256, S)

    def body(x_ref, t_ref, mod_ref, gp_ref, o_ref, za_ref, gl_ref, yr_ref, wpa_ref, wpr_ref, wout_ref,
             dout_ref, do_ref, dpm_ref, dyr_ref, mb_ref, dub_ref, yab_ref, dab_ref, drb_ref, sums_ref):
        @pl.when(pl.program_id(0) == 0)
        def _():
            sums_ref[...] = jnp.zeros_like(sums_ref)

        za = za_ref[...].astype(F32)
        sa = _sigmoid(za)
        sil = za * sa
        ov = o_ref[...]
        ya_b = (ov * sil).astype(BF16)
        yr_b = yr_ref[...]
        av = _dot(ya_b, wpa_ref[...])
        rv = _dot(yr_b, wpr_ref[...])
        ga = _sigmoid(gl_ref[:, :D].astype(F32))
        gr = _sigmoid(gl_ref[:, D:].astype(F32))
        mb = (ga * av + gr * rv).astype(BF16)
        u = _dot(mb, wout_ref[...])
        r2 = lax.rsqrt(jnp.mean(u * u, axis=-1, keepdims=True) + EPS)
        un = u * r2
        gp = gp_ref[...]
        yv = un * gp
        gate = mod_ref[2:3, :]
        err = (x_ref[...] + gate * yv) - t_ref[...]
        dout = err * (1.0 / D)
        dout_ref[...] = dout
        dy = dout * gate
        sums_ref[0:1, :] += jnp.sum(dout * yv, axis=0, keepdims=True)
        sums_ref[1:2, :] += jnp.sum(dy * un, axis=0, keepdims=True)
        sums_ref[2:3, :] += jnp.sum(err * err, axis=0, keepdims=True)
        dyg = dy * gp
        du_b = (r2 * (dyg - un * jnp.mean(dyg * un, axis=-1, keepdims=True))).astype(BF16)
        dm = _dot(du_b, wout_ref[...], NT)
        da_b = (dm * ga).astype(BF16)
        dr_b = (dm * gr).astype(BF16)
        dpm_ref[:, :D] = (dm * av * (ga * (1.0 - ga))).astype(BF16)
        dpm_ref[:, D:2 * D] = (dm * rv * (gr * (1.0 - gr))).astype(BF16)
        dya = _dot(da_b, wpa_ref[...], NT)
        dyr_ref[...] = _dot(dr_b, wpr_ref[...], NT)
        dov = dya * sil
        for g in range(4):
            dt = dov[:, 128 * g:128 * g + 128].T
            do_ref[2 * g] = dt[:DH].astype(BF16)
            do_ref[2 * g + 1] = dt[DH:].astype(BF16)
        dpm_ref[:, 2 * D:] = (dya * ov * (sa * (1.0 + za * (1.0 - sa)))).astype(BF16)
        mb_ref[...] = mb
        dub_ref[...] = du_b
        yab_ref[...] = ya_b
        dab_ref[...] = da_b
        drb_ref[...] = dr_b

    row = lambda w: pl.BlockSpec((tm, w), lambda i: (i, 0))
    sd = lambda w, dt: jax.ShapeDtypeStruct((S, w), dt)
    return pl.pallas_call(
        body, name="mid",
        out_shape=[sd(D, F32), jax.ShapeDtypeStruct((8, DH, S), BF16), sd(2560, BF16), sd(512, F32), sd(D, BF16), sd(D, BF16), sd(512, BF16),
                   sd(D, BF16), sd(D, BF16), jax.ShapeDtypeStruct((8, D), F32)],
        grid=(S // tm,),
        in_specs=[row(D), row(D), _full((3, D)), _full((1, D)), row(512), pl.BlockSpec((tm, 512), lambda i: (i, O_ZA // 512)),
                  pl.BlockSpec((tm, 2048), lambda i: (i, 0)), row(512), _full((512, D)), _full((512, D)), _full((D, D))],
        out_specs=[row(D), pl.BlockSpec((8, DH, tm), lambda i: (0, 0, i)), row(2560), row(512), row(D), row(D), row(512), row(D), row(D),
                   _full((8, D))],
        compiler_params=_cp(("arbitrary",), VMEM_BIG),
    )(x, tgt, mod, g_post, o_att, p, p, yr, w_pa, w_pr, w_out)


def _attn_bwd(qt, kh, kt, vh, dot_, ot, lse, xs):
    S = qt.shape[2]
    tq, tk = min(512, S), min(1024, S)

    def body(q_ref, k_ref, kt_ref, v_ref, do_ref, o_ref, lse_ref, dq_ref, dk_ref, dv_ref):
        j, i = pl.program_id(0), pl.program_id(1)
        cols = pl.ds(pl.multiple_of(i * tq, tq), tq)
        st = {0: _dot(k_ref[0], q_ref[0])}
        dpt = {0: _dot(v_ref[0], do_ref[0])}
        dk_acc, dv_acc, dqs = [None, None], [None, None], []
        for h in range(8):
            g = h // 4
            if h + 1 < 8:
                st[h + 1] = _dot(k_ref[(h + 1) // 4], q_ref[h + 1])
                dpt[h + 1] = _dot(v_ref[(h + 1) // 4], do_ref[h + 1])
            qt_h, dot_h = q_ref[h], do_ref[h]
            delta = jnp.sum(dot_h.astype(F32) * o_ref[h], axis=0, keepdims=True)
            pt = jnp.exp2(st[h] - lse_ref[g, h % 4:h % 4 + 1, :])
            dst = (pt * (dpt[h] - delta)).astype(BF16)
            dv_h = _dot(dot_h, pt.astype(BF16), NT)
            dk_h = _dot(qt_h, dst, NT)
            dqs.append(_dot(kt_ref[g], dst))
            dv_acc[g] = dv_h if dv_acc[g] is None else dv_acc[g] + dv_h
            dk_acc[g] = dk_h if dk_acc[g] is None else dk_acc[g] + dk_h
            del st[h], dpt[h]

        @pl.when(i == 0)
        def _():
            for g in range(2):
                dk_ref[g] = dk_acc[g]
                dv_ref[g] = dv_acc[g]

        @pl.when(i > 0)
        def _():
            for g in range(2):
                dk_ref[g] += dk_acc[g]
                dv_ref[g] += dv_acc[g]

        @pl.when(j == 0)
        def _():
            for h in range(8):
                dq_ref[h, :, cols] = dqs[h]

        @pl.when(j > 0)
        def _():
            for h in range(8):
                dq_ref[h, :, cols] += dqs[h]

    return _host_call(
        body, xs, name="attn_bwd",
        out_shape=[jax.ShapeDtypeStruct((8, DH, S), F32), jax.ShapeDtypeStruct((2, DH, S), F32), jax.ShapeDtypeStruct((2, DH, S), F32)],
        grid=(S // tk, S // tq),
        in_specs=[pl.BlockSpec((8, DH, tq), lambda j, i: (0, 0, i)), pl.BlockSpec((2, tk, DH), lambda j, i: (0, j, 0)),
                  pl.BlockSpec((2, DH, tk), lambda j, i: (0, 0, j)), pl.BlockSpec((2, tk, DH), lambda j, i: (0, j, 0)),
                  pl.BlockSpec((8, DH, tq), lambda j, i: (0, 0, i)), pl.BlockSpec((8, DH, tq), lambda j, i: (0, 0, i)),
                  pl.BlockSpec((2, 4, tq), lambda j, i: (0, 0, i))],
        out_specs=[pl.BlockSpec((8, DH, S), lambda j, i: (0, 0, 0)), pl.BlockSpec((2, DH, tk), lambda j, i: (0, 0, j)),
                   pl.BlockSpec((2, DH, tk), lambda j, i: (0, 0, j))],
        scratch_shapes=[], operands=(qt, kh, kt, vh, dot_, ot, lse),
        compiler_params=_cp(("arbitrary", "arbitrary"), VMEM_BIG),
    )


def _attn_prep_bwd(dqt, dkt, dvt, p, cos, sin, qg, kg):
    S = dqt.shape[2]
    tm = min(512, S)

    def body(dq_ref, dk_ref, dv_ref, qa_ref, ka_ref, cos_ref, sin_ref, qg_ref, kg_ref, dp_ref, gs_ref):
        @pl.when(pl.program_id(0) == 0)
        def _():
            gs_ref[...] = jnp.zeros_like(gs_ref)

        cos_v, sin_v = cos_ref[...], sin_ref[...]

        def pair(ref, a):
            return jnp.concatenate([ref[a], ref[a + 1]], axis=0).T

        def norm_bwd(dyv, xv, gv, row):
            r = lax.rsqrt(_head_mean(xv * xv) + EPS)
            xn = xv * r
            dxh = _rope_t(dyv, cos_v, sin_v)
            gs_ref[row:row + 1, :] += jnp.sum(dxh * xn, axis=0, keepdims=True)
            dg = dxh * gv
            return r * (dg - xn * _head_mean(dg * xn))

        for g in range(4):
            sl = slice(128 * g, 128 * g + 128)
            dp_ref[:, sl] = norm_bwd(pair(dq_ref, 2 * g) * 0.125, qa_ref[:, sl].astype(F32), qg_ref[...], 0).astype(BF16)
        dp_ref[:, 512:640] = norm_bwd(pair(dk_ref, 0) * LN2, ka_ref[...].astype(F32), kg_ref[...], 1).astype(BF16)
        dp_ref[:, 640:768] = pair(dv_ref, 0).astype(BF16)

    ht = lambda n: pl.BlockSpec((n, DH, tm), lambda i: (0, 0, i))
    return pl.pallas_call(
        body, name="attn_prep_bwd", out_shape=[jax.ShapeDtypeStruct((S, 768), BF16), jax.ShapeDtypeStruct((8, 128), F32)],
        grid=(S // tm,),
        in_specs=[ht(8), ht(2), ht(2),
                  pl.BlockSpec((tm, 512), lambda i: (i, O_QA // 512)), pl.BlockSpec((tm, 128), lambda i: (i, O_KA // 128)),
                  pl.BlockSpec((tm, 128), lambda i: (i, 0)), pl.BlockSpec((tm, 128), lambda i: (i, 0)), _full((1, 128)), _full((1, 128))],
        out_specs=[pl.BlockSpec((tm, 768), lambda i: (i, 0)), _full((8, 128))],
        compiler_params=_cp(("arbitrary",)),
    )(dqt, dkt, dvt, p, p, cos, sin, qg, kg)


def _ret_bwd_chunk(qr2, kr2, p, rf, rb, dc, qdf, qdb, gn, dyr, cos, sin, xs):
    S = qr2.shape[0]
    C, N = CH, S // CH

    def body(q_ref, k_ref, v_ref, z_ref, rf_ref, rb_ref, dc_ref, qdf_ref, qdb_ref, gn_ref, dyr_ref, cos_ref, sin_ref,
             dpa_ref, dk_ref, dv_ref, drf_ref, drb_ref, dgn_ref, dlg_ref, dqs):
        @pl.when(pl.program_id(0) == 0)
        def _():
            dgn_ref[...] = jnp.zeros_like(dgn_ref)
            dlg_ref[...] = jnp.zeros_like(dlg_ref)

        qv = q_ref[...]
        qb, kb, vb = qv.astype(BF16), k_ref[...].astype(BF16), v_ref[...].astype(BF16)
        qf32, qb32 = qv * qdf_ref[...], qv * qdb_ref[...]
        qfw, qbw = qf32.astype(BF16), qb32.astype(BF16)
        ii = lax.broadcasted_iota(jnp.int32, (C, C), 0).astype(F32)
        jj = lax.broadcasted_iota(jnp.int32, (C, C), 1).astype(F32)
        dif = ii - jj
        ri = lax.broadcasted_iota(jnp.int32, (C, 1), 0).astype(F32)
        hs = range(HR)
        sd, o = _ret_heads_fwd(qb, kb, vb, qfw, qbw, dc_ref, rf_ref, rb_ref)
        do_b = []
        for h in hs:
            vs = _vs(h)
            mu = jnp.mean(o[h], axis=-1, keepdims=True)
            rstd = lax.rsqrt(jnp.mean(jnp.square(o[h] - mu), axis=-1, keepdims=True) + EPS)
            on = (o[h] - mu) * rstd
            z = z_ref[:, vs].astype(F32)
            sz = _sigmoid(z)
            dy = dyr_ref[:, vs]
            gnv = gn_ref[:, vs]
            dpa_ref[:, 256 + DV * h:256 + DV * h + DV] = (dy * (on * gnv) * (sz * (1.0 + z * (1.0 - sz)))).astype(BF16)
            dys = dy * (z * sz)
            dgn_ref[:, vs] += jnp.sum(dys * on, axis=0, keepdims=True)
            don = dys * gnv
            do = rstd * (don - jnp.mean(don, axis=-1, keepdims=True) - on * jnp.mean(don * on, axis=-1, keepdims=True))
            do_b.append(do.astype(BF16))
        dpm = [_dot(do_b[h], vb[:, _vs(h)], NT) for h in hs]
        dqf = [_dot(do_b[h], rf_ref[0, h].astype(BF16), NT) for h in hs]
        dqb = [_dot(do_b[h], rb_ref[0, h].astype(BF16), NT) for h in hs]
        for h in hs:
            dv_ref[:, _vs(h)] = _dot(sd[h].astype(BF16), do_b[h], TN)
            drf_ref[0, h] = _dot(qfw[:, _ks(h)], do_b[h], TN)
            drb_ref[0, h] = _dot(qbw[:, _ks(h)], do_b[h], TN)
        dsd = [(dpm[h] * dc_ref[h]).astype(BF16) for h in hs]
        for h in hs:
            ks = _ks(h)
            dqs[:, ks] = _dot(dsd[h], kb[:, ks]) + dqf[h] * qdf_ref[:, ks] + dqb[h] * qdb_ref[:, ks]
            dk_ref[:, ks] = _dot(dsd[h], qb[:, ks], TN)
        for h in hs:
            ks = _ks(h)
            e = dpm[h] * sd[h]
            lf = _sum11(e * jnp.maximum(dif, 0.0)) + _sum11(jnp.sum(qf32[:, ks] * dqf[h], axis=-1, keepdims=True) * (ri + 1.0))
            lb = _sum11(e * jnp.maximum(-dif, 0.0)) + _sum11(jnp.sum(qb32[:, ks] * dqb[h], axis=-1, keepdims=True) * (C - ri))
            dlg_ref[h:h + 1, :] += jnp.broadcast_to(lf, (1, 128))
            dlg_ref[HR + h:HR + h + 1, :] += jnp.broadcast_to(lb, (1, 128))
        cos_v, sin_v = cos_ref[...], sin_ref[...]
        for g in range(2):
            sl = slice(128 * g, 128 * g + 128)
            dpa_ref[:, sl] = _rope_t(dqs[:, sl], cos_v, sin_v).astype(BF16)

    st = jax.ShapeDtypeStruct((N, HR, DH, DV), F32)
    stb = lambda: pl.BlockSpec((1, HR, DH, DV), lambda t: (t, 0, 0, 0))
    return _host_call(
        body, xs, name="ret_bwd_chunk",
        out_shape=[jax.ShapeDtypeStruct((S, 768), BF16), jax.ShapeDtypeStruct((S, 256), F32), jax.ShapeDtypeStruct((S, 512), F32), st, st,
                   jax.ShapeDtypeStruct((1, 512), F32), jax.ShapeDtypeStruct((8, 128), F32)],
        grid=(N,),
        in_specs=[pl.BlockSpec((C, 256), lambda t: (t, 0)), pl.BlockSpec((C, 256), lambda t: (t, 0)),
                  pl.BlockSpec((C, 512), lambda t: (t, O_VR // 512)), pl.BlockSpec((C, 512), lambda t: (t, O_ZR // 512)),
                  stb(), stb(), _full((HR, C, C)), _full((C, 256)), _full((C, 256)), _full((1, 512)),
                  pl.BlockSpec((C, 512), lambda t: (t, 0)), pl.BlockSpec((C, 128), lambda t: (t, 0)), pl.BlockSpec((C, 128), lambda t: (t, 0))],
        out_specs=[pl.BlockSpec((C, 768), lambda t: (t, 0)), pl.BlockSpec((C, 256), lambda t: (t, 0)), pl.BlockSpec((C, 512), lambda t: (t, 0)),
                   stb(), stb(), _full((1, 512)), _full((8, 128))],
        scratch_shapes=[pltpu.VMEM((C, 256), F32)], operands=(qr2, kr2, p, p, rf, rb, dc, qdf, qdb, gn, dyr, cos, sin),
        compiler_params=_cp(("arbitrary",)),
    )


def _ret_bwd_scan(kr2, p, rf, rb, drf, drb, kdf, kdb, adec):
    S = kr2.shape[0]
    C, N = CH, S // CH

    def body(kf_ref, vf_ref, kb_ref, vb_ref, rf_ref, rb_ref, drf_ref, drb_ref, kdf_ref, kdb_ref, a_ref,
             dkf_ref, dkb_ref, dvf_ref, dvb_ref, dlg_ref, gf, gb):
        @pl.when(pl.program_id(0) == 0)
        def _():
            gf[...] = jnp.zeros_like(gf)
            gb[...] = jnp.zeros_like(gb)
            dlg_ref[...] = jnp.zeros_like(dlg_ref)

        ri = lax.broadcasted_iota(jnp.int32, (C, 1), 0).astype(F32)

        def one(k_ref, v_ref, r_ref, dr_ref, kd_ref, g_s, dk_ref, dv_ref, row0, wexp):
            kd32 = k_ref[...] * kd_ref[...]
            kdw = kd32.astype(BF16)
            vb = v_ref[...].astype(BF16)
            for h in range(HR):
                ks, vs = _ks(h), _vs(h)
                gst = g_s[h]
                g_b = gst.astype(BF16)
                dkd = _dot(vb[:, vs], g_b, NT)
                dk_ref[:, ks] = dkd * kd_ref[:, ks]
                dv_ref[:, vs] = _dot(kdw[:, ks], g_b)
                av = a_ref[row0 + h:row0 + h + 1, :]
                lg = (_sum11(jnp.sum(kd32[:, ks] * dkd, axis=-1, keepdims=True) * wexp)
                      + C * av[:, 0:1] * _sum11(r_ref[0, h] * gst))
                dlg_ref[row0 + h:row0 + h + 1, :] += jnp.broadcast_to(lg, (1, 128))
                g_s[h] = dr_ref[0, h] + av * gst

        one(kf_ref, vf_ref, rf_ref, drf_ref, kdf_ref, gf, dkf_ref, dvf_ref, 0, C - 1.0 - ri)
        one(kb_ref, vb_ref, rb_ref, drb_ref, kdb_ref, gb, dkb_ref, dvb_ref, HR, ri)

    fwd = lambda w, off=0: pl.BlockSpec((C, w), lambda t: (N - 1 - t, off))
    bwd = lambda w, off=0: pl.BlockSpec((C, w), lambda t: (t, off))
    stf = lambda: pl.BlockSpec((1, HR, DH, DV), lambda t: (N - 1 - t, 0, 0, 0))
    stb = lambda: pl.BlockSpec((1, HR, DH, DV), lambda t: (t, 0, 0, 0))
    return pl.pallas_call(
        body, name="ret_bwd_scan",
        out_shape=[jax.ShapeDtypeStruct((S, 256), F32), jax.ShapeDtypeStruct((S, 256), F32), jax.ShapeDtypeStruct((S, 512), F32),
                   jax.ShapeDtypeStruct((S, 512), F32), jax.ShapeDtypeStruct((8, 128), F32)],
        grid=(N,),
        in_specs=[fwd(256), fwd(512, O_VR // 512), bwd(256), bwd(512, O_VR // 512), stf(), stb(), stf(), stb(),
                  _full((C, 256)), _full((C, 256)), _full((8, 128))],
        out_specs=[fwd(256), bwd(256), fwd(512), bwd(512), _full((8, 128))],
        scratch_shapes=[pltpu.VMEM((HR, DH, DV), F32), pltpu.VMEM((HR, DH, DV), F32)],
        compiler_params=_cp(("arbitrary",)),
    )(kr2, p, kr2, p, rf, rb, drf, drb, kdf, kdb, adec)


def _ret_bwd_final(dk_i, dkf, dkb, dv_i, dvf, dvb, cos, sin):
    S = dk_i.shape[0]
    tm = min(512, S)

    def body(a_ref, b_ref, c_ref, d_ref, e_ref, f_ref, cos_ref, sin_ref, o_ref):
        o_ref[:, :512] = (d_ref[...] + e_ref[...] + f_ref[...]).astype(BF16)
        cos_v, sin_v = cos_ref[...], sin_ref[...]
        for g in range(2):
            sl = slice(128 * g, 128 * g + 128)
            dk = a_ref[:, sl] + b_ref[:, sl] + c_ref[:, sl]
            o_ref[:, 512 + 128 * g:512 + 128 * g + 128] = (_rope_t(dk, cos_v, sin_v) * 0.125).astype(BF16)

    row = lambda w: pl.BlockSpec((tm, w), lambda i: (i, 0))
    return pl.pallas_call(
        body, name="ret_bwd_final", out_shape=jax.ShapeDtypeStruct((S, 768), BF16), grid=(S // tm,),
        in_specs=[row(256), row(256), row(256), row(512), row(512), row(512), row(128), row(128)], out_specs=row(768),
        compiler_params=_cp(("parallel",)),
    )(dk_i, dkf, dkb, dv_i, dvf, dvb, cos, sin)


def _bwd_in(dpm, dpa, dpra, dprb, w_p, x, dout, mod, g_pre, xs):
    S = x.shape[0]
    tm = min(256, S)

    def body(a_ref, b_ref, c_ref, d_ref, w_ref, x_ref, dout_ref, mod_ref, g_ref, gx_ref, sums_ref):
        @pl.when(pl.program_id(0) == 0)
        def _():
            sums_ref[...] = jnp.zeros_like(sums_ref)

        dh = (_dot(a_ref[...], w_ref[:, :O_QA], NT) + _dot(b_ref[...], w_ref[:, O_QA:O_QR], NT)
              + _dot(c_ref[...], w_ref[:, O_QR:O_VR], NT) + _dot(d_ref[...], w_ref[:, O_VR:], NT))
        xv = x_ref[...]
        r = lax.rsqrt(jnp.mean(xv * xv, axis=-1, keepdims=True) + EPS)
        xn = xv * r
        gv = g_ref[...]
        sc1 = 1.0 + mod_ref[1:2, :]
        sums_ref[0:1, :] += jnp.sum(dh, axis=0, keepdims=True)
        sums_ref[1:2, :] += jnp.sum(dh * (xn * gv), axis=0, keepdims=True)
        sums_ref[2:3, :] += jnp.sum(dh * xn, axis=0, keepdims=True) * sc1
        dxn = dh * (gv * sc1)
        gx_ref[...] = dout_ref[...] + r * (dxn - xn * jnp.mean(dxn * xn, axis=-1, keepdims=True))

    row = lambda w: pl.BlockSpec((tm, w), lambda i: (i, 0))
    return _host_call(
        body, xs, name="bwd_in", out_shape=[jax.ShapeDtypeStruct((S, D), F32), jax.ShapeDtypeStruct((8, D), F32)], grid=(S // tm,),
        in_specs=[row(2560), row(768), row(768), row(768), _full((D, P_W)), row(D), row(D), _full((3, D)), _full((1, D))],
        out_specs=[row(D), _full((8, D))], scratch_shapes=[], operands=(dpm, dpa, dpra, dprb, w_p, x, dout, mod, g_pre),
        compiler_params=_cp(("arbitrary",), VMEM_BIG),
    )


SMALL = ("b_ada", "g_pre", "qn_g", "kn_g", "w_dec_f", "w_dec_b", "gn_g", "g_post")


def _small_update(gathered, wmv):
    ns = len(SMALL)

    def body(*refs):
        gin_ref, gmid_ref, ggn_ref, gatt_ref, gl1_ref, gl2_ref = refs[:6]
        wmv_refs = refs[6:6 + 3 * ns]
        loss_ref = refs[6 + 3 * ns]
        out_refs = refs[7 + 3 * ns:]

        def dsum(ref, r=None):
            rows = slice(None) if r is None else slice(r, r + 1)
            acc = ref[0, rows, :]
            for d in range(1, NDEV):
                acc = acc + ref[d, rows, :]
            return acc

        s_lg = dsum(gl1_ref) + dsum(gl2_ref)
        loss_ref[...] = (0.5 / D) * jnp.sum(dsum(gmid_ref, 2), axis=-1, keepdims=True)
        eye = lax.broadcasted_iota(jnp.int32, (8, 128), 0) == lax.broadcasted_iota(jnp.int32, (8, 128), 1)
        dlg = jnp.sum(jnp.where(eye, s_lg, 0.0), axis=0, keepdims=True)
        w_f, w_b = wmv_refs[3 * SMALL.index("w_dec_f")][...], wmv_refs[3 * SMALL.index("w_dec_b")][...]
        s_q, s_k = dsum(gatt_ref, 0), dsum(gatt_ref, 1)
        grads = dict(
            b_ada=jnp.concatenate([dsum(gin_ref, 0), dsum(gin_ref, 1), dsum(gmid_ref, 0)], axis=1),
            g_pre=dsum(gin_ref, 2), g_post=dsum(gmid_ref, 1), gn_g=dsum(ggn_ref),
            qn_g=s_q[:, :DH] + s_q[:, DH:], kn_g=s_k[:, :DH] + s_k[:, DH:],
            w_dec_f=dlg[:, 0:HR] * _sigmoid(-w_f), w_dec_b=dlg[:, HR:2 * HR] * _sigmoid(-w_b))
        for i, nme in enumerate(SMALL):
            g = grads[nme]
            w_ref, m_ref, v_ref = wmv_refs[3 * i:3 * i + 3]
            g_ref, d_ref, nm_ref, nv_ref = out_refs[4 * i:4 * i + 4]
            g_ref[...] = g
            m2 = ADAM_B1 * m_ref[...] + (1.0 - ADAM_B1) * g
            v2 = ADAM_B2 * v_ref[...] + (1.0 - ADAM_B2) * jnp.square(g)
            m_hat = m2 / (1.0 - ADAM_B1 ** ADAM_STEP)
            v_hat = v2 / (1.0 - ADAM_B2 ** ADAM_STEP)
            d_ref[...] = -ADAM_LR * (m_hat / (jnp.sqrt(v_hat) + ADAM_EPS) + ADAM_WD * w_ref[...])
            nm_ref[...] = m2
            nv_ref[...] = v2

    out_shape = [jax.ShapeDtypeStruct((1, 1), F32)]
    for i in range(ns):
        out_shape += [jax.ShapeDtypeStruct(wmv[3 * i].shape, F32)] * 4
    return pl.pallas_call(body, name="small_update", out_shape=out_shape)(*gathered, *wmv)


def _adamw(parts, w, m, v, name):
    n, R, L = parts.shape
    tr = 256 if (R % 256 == 0 and R > 256) else R

    def body(p_ref, w_ref, m_ref, v_ref, g_ref, d_ref, nm_ref, nv_ref):
        g = p_ref[0].astype(F32)
        for k in range(1, n):
            g = g + p_ref[k].astype(F32)
        g_ref[...] = g
        m2 = ADAM_B1 * m_ref[...] + (1.0 - ADAM_B1) * g
        v2 = ADAM_B2 * v_ref[...] + (1.0 - ADAM_B2) * jnp.square(g)
        m_hat = m2 / (1.0 - ADAM_B1 ** ADAM_STEP)
        v_hat = v2 / (1.0 - ADAM_B2 ** ADAM_STEP)
        d_ref[...] = -ADAM_LR * (m_hat / (jnp.sqrt(v_hat) + ADAM_EPS) + ADAM_WD * w_ref[...])
        nm_ref[...] = m2
        nv_ref[...] = v2

    blk = pl.BlockSpec((tr, L), lambda i: (i, 0))
    o = jax.ShapeDtypeStruct((R, L), F32)
    return pl.pallas_call(
        body, name=name, out_shape=[o, o, o, o], grid=(R // tr,),
        in_specs=[pl.BlockSpec((n, tr, L), lambda i: (0, i, 0)), blk, blk, blk], out_specs=[blk, blk, blk, blk],
        compiler_params=_cp(("parallel",), VMEM_BIG),
    )(parts, w, m, v)


def _rope_tables(S):
    f = np.float32
    t = np.arange(S)
    row, col = (t // 64).astype(f), (t % 64).astype(f)
    half = DH // 2
    inv = np.power(f(ROPE_THETA), -np.arange(0, half, 2, dtype=f) / f(half)).astype(f)
    ar, ac = (row[:, None] * inv[None, :]).astype(f), (col[:, None] * inv[None, :]).astype(f)
    cos64 = np.concatenate([np.cos(ar), np.cos(ar), np.cos(ac), np.cos(ac)], axis=1).astype(f)
    sin64 = np.concatenate([-np.sin(ar), np.sin(ar), -np.sin(ac), np.sin(ac)], axis=1).astype(f)
    return jnp.asarray(np.tile(cos64, (1, 2))), jnp.asarray(np.tile(sin64, (1, 2)))


def _to_p_order(w_orig):
    return jnp.concatenate([w_orig[:, ORIG[n][0]:ORIG[n][1]] for n in P_ORDER], axis=1)


def _pad_lanes(v, n):
    return jnp.pad(v, ((0, 0), (0, n - v.shape[1])))


def kernel(x, c, w_ada, b_ada, g_pre, w_in, qn_g, kn_g, w_dec_f, w_dec_b, gn_g, w_pa, w_pr, w_out, g_post, loss_target, m_w_ada, m_b_ada, m_g_pre, m_w_in, m_qn_g, m_kn_g, m_w_dec_f, m_w_dec_b, m_gn_g, m_w_pa, m_w_pr, m_w_out, m_g_post, v_w_ada, v_b_ada, v_g_pre, v_w_in, v_qn_g, v_kn_g, v_w_dec_f, v_w_dec_b, v_gn_g, v_w_pa, v_w_pr, v_w_out, v_g_post):
    S = x.shape[1]
    me = 4 * lax.axis_index("x") + 2 * lax.axis_index("y") + lax.axis_index("c")
    xs, tgt = x[0], loss_target[0]
    ncol_ada = w_ada.shape[2]
    ncol_in = w_in.shape[2]

    b_ada_s = lax.dynamic_slice(b_ada, (0, me * ncol_ada), (1, ncol_ada))
    mod_all, c_act, (wg_in,) = _prologue(jnp.pad(c, ((0, 7), (0, 0))), w_ada[0], b_ada_s, [w_in[0].astype(BF16)])
    mod = lax.dynamic_index_in_dim(mod_all, me, axis=1, keepdims=False).reshape(3, D)
    w_p = _to_p_order(wg_in.transpose(1, 0, 2).reshape(D, NDEV * ncol_in))
    all_dev = tuple(range(NDEV))
    st_w, tok_w = _xchg_start([(w_pa[0].astype(BF16)[None], all_dev), (w_pr[0].astype(BF16)[None], all_dev),
                               (w_out[0].astype(BF16)[None], all_dev)], "wgather_start")

    cos, sin = _rope_tables(S)
    qg, kg = jnp.tile(qn_g, (1, 2)), jnp.tile(kn_g, (1, 2))

    p, h = _fwd_in(xs, mod, g_pre + tok_w[0:1, 0:1], w_p)
    qt, kh, kt, vh, vta, qr2, kr2 = _prep(p, cos, sin, qg, kg)
    o_att, o_t, lse = _attn_fwd(qt, kh, vta)
    dc, qdf, qdb, kdf, kdb, adec = _ret_tables(w_dec_f, w_dec_b)
    rf, rb = _ret_states(kr2, p, kdf, kdb, adec)
    yr = _ret_out(qr2, kr2, p, rf, rb, dc, qdf, qdb, gn_g)
    wg_pa, wg_pr, wg_out = _xchg_wait([st_w], st_w["lands"], [[0, 1, 2]], yr, "wgather_wait")
    w_pa_f = wg_pa.transpose(1, 0, 2).reshape(512, D)
    w_pr_f = wg_pr.transpose(1, 0, 2).reshape(512, D)
    w_out_f = wg_out.reshape(D, D)

    dout, do, dpm, dyr, mb, dub, yab, dab, drb_, sums_mid = _mid(xs, tgt, mod, g_post, o_att, p, yr, w_pa_f, w_pr_f, w_out_f)
    gw_out = _mm_tn(mb, dub, "gw_out")
    gw_pa = _mm_tn(yab, dab, "gw_pa")
    gw_pr = _mm_tn(yr, drb_, "gw_pr")
    gi_m = _mm_tn(h, dpm, "gw_in_mid")

    def shards(cols, nd):
        return cols.astype(BF16).reshape(D, nd, ncol_in).transpose(1, 0, 2)

    st_a, tok_a = _xchg_start([
        (gw_out.astype(BF16).reshape(NDEV, 128, D), all_dev),
        (gw_pa.astype(BF16).reshape(512, NDEV, 128).transpose(1, 0, 2), all_dev),
        (gw_pr.astype(BF16).reshape(512, NDEV, 128).transpose(1, 0, 2), all_dev),
        (shards(gi_m[:, 224:2048], 3), (5, 6, 7))], "xchg_start_a")
    (dqt, dkt, dvt), _ = _attn_bwd(qt, kh, kt, vh, do, o_t, lse + tok_a[0, 0], [])
    dpa, gs_att = _attn_prep_bwd(dqt, dkt, dvt, p, cos, sin, qg, kg)
    gi_a = _mm_tn(h, dpa, "gw_in_att")
    st_b, tok_b = _xchg_start([(shards(jnp.concatenate([gi_a, gi_m[:, 2048:2496]], axis=1), 2), (0, 1))], "xchg_start_b",
                              lands=[st_a["lands"][3]])
    (dpra, dk_i, dv_i, drf, drb, dgn, dlg1), _ = _ret_bwd_chunk(qr2, kr2, p, rf, rb, dc, qdf, qdb, gn_g + tok_b[0:1, 0:1], dyr, cos, sin, [])
    dkf, dkb, dvf, dvb, dlg2 = _ret_bwd_scan(kr2, p, rf, rb, drf, drb, kdf, kdb, adec)
    dprb = _ret_bwd_final(dk_i, dkf, dkb, dv_i, dvf, dvb, cos, sin)
    gi_ra = _mm_tn(h, dpra, "gw_in_reta")
    gi_rb = _mm_tn(h, dprb, "gw_in_retb")
    st_c, tok_c = _xchg_start([(shards(jnp.concatenate([gi_m[:, 2496:2560], gi_ra[:, :256], gi_rb[:, 512:768], gi_rb[:, :512],
                                                        gi_ra[:, 256:768], gi_m[:, :224]], axis=1), 3), (2, 3, 4))], "xchg_start_c",
                              lands=[st_b["lands"][0]])
    (grad_x, sums_in), _ = _bwd_in(dpm, dpa, dpra, dprb, w_p, xs, dout, mod, g_pre + tok_c[0:1, 0:1], [])

    gathered = _small_allgather([sums_in, sums_mid, dgn, gs_att, dlg1, dlg2], "ag_small")
    given = dict(b_ada=(b_ada, m_b_ada, v_b_ada), g_pre=(g_pre, m_g_pre, v_g_pre), qn_g=(qn_g, m_qn_g, v_qn_g), kn_g=(kn_g, m_kn_g, v_kn_g),
                 w_dec_f=(w_dec_f, m_w_dec_f, v_w_dec_f), w_dec_b=(w_dec_b, m_w_dec_b, v_w_dec_b), gn_g=(gn_g, m_gn_g, v_gn_g),
                 g_post=(g_post, m_g_post, v_g_post))
    small = _small_update(gathered, [a for nme in SMALL for a in given[nme]])
    loss = small[0][0, 0]

    g_in_all, g_mid_all = gathered[0], gathered[1]
    dmod_all = lax.dynamic_slice(jnp.concatenate([g_in_all[:, 0, :], g_in_all[:, 1, :], g_mid_all[:, 0, :]], axis=1),
                                 (0, me * ncol_ada), (NDEV, ncol_ada))
    g_ada = _mm_tn(c_act, jnp.pad(dmod_all, ((0, 8), (0, 0))).astype(BF16), "gw_ada")

    ada = _adamw(g_ada[None], w_ada[0], m_w_ada[0], v_w_ada[0], "adamw_ada")
    rs_out, rs_pa, rs_pr, rs_in = _xchg_wait([st_a, st_b, st_c], list(st_a["lands"][:3]) + [st_c["lands"][0]],
                                             [[0, 1, 2, 3], [3], [3]], ada[1], "xchg_wait")
    res = dict(
        w_ada=ada,
        w_in=_adamw(rs_in, w_in[0], m_w_in[0], v_w_in[0], "adamw_in"),
        w_pa=_adamw(rs_pa, w_pa[0], m_w_pa[0], v_w_pa[0], "adamw_pa"),
        w_pr=_adamw(rs_pr, w_pr[0], m_w_pr[0], v_w_pr[0], "adamw_pr"),
        w_out=_adamw(rs_out, w_out[0], m_w_out[0], v_w_out[0], "adamw_out"),
    )
    names = ["w_ada", "b_ada", "g_pre", "w_in", "qn_g", "kn_g", "w_dec_f", "w_dec_b", "gn_g", "w_pa", "w_pr", "w_out", "g_post"]
    outs = [[], [], [], []]
    for nme in names:
        for q in range(4):
            if nme in res:
                outs[q].append(res[nme][q][None])
            else:
                outs[q].append(small[1 + 4 * SMALL.index(nme) + q])
    return (loss, grad_x[None], *outs[0], *outs[1], *outs[2], *outs[3])
```

```python
import jax
import jax.numpy as jnp
import numpy as np
from jax import lax
from jax.experimental import pallas as pl
from jax.experimental.pallas import tpu as pltpu

F32, BF16 = jnp.float32, jnp.bfloat16
D = 1024
DH = 64
DHA = 80
DV = 128
LOG2E = 1.4426950408889634
LN2 = 0.6931471805599453
HR = 4
CH = 128
EPS = 1e-6
ROPE_THETA = 10000.0
NDEV = 8
O_GL, O_ZA, O_QA, O_KA, O_VA, O_QR, O_ZR, O_VR, O_KR, P_W = 0, 2048, 2560, 3072, 3200, 3328, 3584, 4096, 4608, 4864
ORIG = dict(qa=(0, 512), ka=(512, 640), va=(640, 768), za=(768, 1280), qr=(1280, 1536), kr=(1536, 1792),
            vr=(1792, 2304), zr=(2304, 2816), gl=(2816, 4864))
P_ORDER = ("gl", "za", "qa", "ka", "va", "qr", "zr", "vr", "kr")
ADAM_LR, ADAM_B1, ADAM_B2, ADAM_EPS, ADAM_WD, ADAM_STEP = 0.001, 0.9, 0.999, 1e-08, 0.01, 10
VMEM_BIG = 56 * 1024 * 1024
MESH = pl.DeviceIdType.MESH

NT = (((1,), (1,)), ((), ()))
TN = (((0,), (0,)), ((), ()))


def _dot(a, b, dims=None):
    if dims is None:
        return jnp.dot(a, b, preferred_element_type=F32)
    return lax.dot_general(a, b, dims, preferred_element_type=F32)


def _cp(sem=None, vmem=None):
    kw = {}
    if sem is not None:
        kw["dimension_semantics"] = sem
    if vmem is not None:
        kw["vmem_limit_bytes"] = vmem
    return pltpu.CompilerParams(**kw)


def _sigmoid(z):
    return 1.0 / (1.0 + jnp.exp(-z))


def _sum11(m):
    return jnp.sum(jnp.sum(m, axis=-1, keepdims=True), axis=0, keepdims=True)


def _full(shape):
    n = len(shape)
    return pl.BlockSpec(shape, lambda *_: (0,) * n)


def _my_pos():
    return lax.axis_index("x"), lax.axis_index("y"), lax.axis_index("c")


def _peer(k, x, y, c):
    return ((1 - x) if k & 4 else x, (1 - y) if k & 2 else y, (1 - c) if k & 1 else c)


def _small_allgather(vs, name):
    n = len(vs)

    def body(*refs):
        v_refs, out_refs = refs[:n], refs[n:2 * n]
        send_sems, recv_sems = refs[2 * n:]
        x, y, c = _my_pos()
        me = 4 * x + 2 * y + c
        cps = []
        for a in range(n):
            out_refs[a][me] = v_refs[a][...]
            for k in range(1, NDEV):
                cp = pltpu.make_async_remote_copy(src_ref=v_refs[a], dst_ref=out_refs[a].at[me], send_sem=send_sems.at[a, k - 1],
                                                  recv_sem=recv_sems.at[a, k - 1], device_id=_peer(k, x, y, c), device_id_type=MESH)
                cp.start()
                cps.append(cp)
        for cp in cps:
            cp.wait()

    vm = pl.BlockSpec(memory_space=pltpu.VMEM)
    return pl.pallas_call(
        body, name=name, out_shape=[jax.ShapeDtypeStruct((NDEV,) + v.shape, v.dtype) for v in vs],
        in_specs=[vm] * n, out_specs=[vm] * n,
        scratch_shapes=[pltpu.SemaphoreType.DMA((n, NDEV - 1)), pltpu.SemaphoreType.DMA((n, NDEV - 1))],
    )(*vs)


def _prologue(c8, w_ada_s, b_ada_s, arrs):
    n = len(arrs)
    ncol = w_ada_s.shape[1]

    def body(*refs):
        c_ref, wa_ref, ba_ref = refs[:3]
        ins = refs[3:3 + n]
        mod_ref, cact_ref = refs[3 + n:5 + n]
        outs = refs[5 + n:5 + 2 * n]
        call_ref, send_sems, recv_sems, local_sems, s_send, s_recv = refs[5 + 2 * n:]
        x, y, c = _my_pos()
        me, sibling = (x, y, c), (x, y, 1 - c)
        chips = [(1 - x, y), (x, 1 - y), (1 - x, 1 - y)]
        me_i = 4 * x + 2 * y + c

        def small_gather(src_ref, dst_ref, row):
            cps = []
            for k in range(1, NDEV):
                cp = pltpu.make_async_remote_copy(src_ref=src_ref, dst_ref=dst_ref.at[me_i], send_sem=s_send.at[row, k - 1],
                                                  recv_sem=s_recv.at[row, k - 1], device_id=_peer(k, x, y, c), device_id_type=MESH)
                cp.start()
                cps.append(cp)
            return cps

        def blk(a, px, py, pc):
            return outs[a].at[4 * px + 2 * py + pc]

        def copy(a, k, block, to, src=None):
            return pltpu.make_async_remote_copy(src_ref=blk(a, *block) if src is None else src, dst_ref=blk(a, *block),
                                                send_sem=send_sems.at[a, k], recv_sem=recv_sems.at[a, k], device_id=to, device_id_type=MESH)

        call_ref[me_i] = c_ref[...]
        for cp in small_gather(c_ref, call_ref, 0):
            cp.wait()

        local, sent = [], []
        for a in range(n):
            mine = pltpu.make_async_copy(ins[a], blk(a, *me), local_sems.at[a])
            mine.start()
            local.append(mine)
            first = [copy(a, 0, me, sibling, src=ins[a])] + [copy(a, 1 + j, me, (*chip, c), src=ins[a]) for j, chip in enumerate(chips)]
            for cp in first:
                cp.start()
            sent += first

        cv = call_ref[:, 0, :]
        ca = jnp.concatenate([cv * _sigmoid(cv), jnp.zeros_like(cv)], axis=0).astype(BF16)
        cact_ref[...] = ca
        mod_ref[me_i] = (_dot(ca, wa_ref[...].astype(BF16)) + ba_ref[...])[:8]
        mod_copies = small_gather(mod_ref.at[me_i], mod_ref, 1)

        for j, chip in enumerate(chips):
            for a in range(n):
                copy(a, 1 + j, (*chip, c), me).wait_recv()
                cp = copy(a, 4 + j, (*chip, c), sibling)
                cp.start()
                sent.append(cp)
        for a in range(n):
            copy(a, 0, sibling, me).wait_recv()
            for j, chip in enumerate(chips):
                copy(a, 4 + j, (*chip, 1 - c), me).wait_recv()
        for cp in sent:
            cp.wait_send()
        for cp in local + mod_copies:
            cp.wait()

    vm, hbm = pl.BlockSpec(memory_space=pltpu.VMEM), pl.BlockSpec(memory_space=pl.ANY)
    res = pl.pallas_call(
        body, name="prologue",
        out_shape=[jax.ShapeDtypeStruct((NDEV, 8, ncol), F32), jax.ShapeDtypeStruct((16, D), BF16)]
        + [jax.ShapeDtypeStruct((NDEV,) + a.shape, a.dtype) for a in arrs],
        in_specs=[vm, vm, vm] + [hbm] * n, out_specs=[vm, vm] + [hbm] * n,
        scratch_shapes=[pltpu.VMEM((NDEV, 8, D), F32), pltpu.SemaphoreType.DMA((n, NDEV - 1)), pltpu.SemaphoreType.DMA((n, NDEV - 1)),
                        pltpu.SemaphoreType.DMA((n,)), pltpu.SemaphoreType.DMA((2, NDEV - 1)), pltpu.SemaphoreType.DMA((2, NDEV - 1))],
    )(c8, w_ada_s, b_ada_s, *arrs)
    return res[0], res[1], res[2:]


def _in_set(idx, dests):
    p = idx == dests[0]
    for d in dests[1:]:
        p = jnp.logical_or(p, idx == d)
    return p


def _host_call(body, xs, *, name, grid, in_specs, out_specs, out_shape, scratch_shapes, operands, compiler_params):
    nx, nin, nout, nscr = len(xs), len(operands), len(out_shape), len(scratch_shapes)
    if nx == 0:
        res = pl.pallas_call(body, name=name, grid=grid, in_specs=in_specs, out_specs=out_specs, out_shape=out_shape,
                             scratch_shapes=scratch_shapes, compiler_params=compiler_params)(*operands)
        return res, []
    ops, specs, aliases = list(operands), list(in_specs), {}
    oshape, ospecs = list(out_shape), list(out_specs)
    any_spec = pl.BlockSpec(memory_space=pl.ANY)
    for a, (send, dests, recv) in enumerate(xs):
        ops.append(send)
        specs.append(any_spec)
        if recv is not None:
            aliases[len(ops)] = nout + a
            ops.append(recv)
            specs.append(any_spec)
            oshape.append(jax.ShapeDtypeStruct(recv.shape, recv.dtype))
        else:
            oshape.append(jax.ShapeDtypeStruct((NDEV,) + send.shape[1:], send.dtype))
        ospecs.append(any_spec)
    ntot_in = len(ops)

    def wrapped(*refs):
        host_in = refs[:nin]
        sends, pos = [], nin
        for (_, _, recv) in xs:
            sends.append(refs[pos])
            pos += 1 if recv is None else 2
        host_out = refs[ntot_in:ntot_in + nout]
        recvs = refs[ntot_in + nout:ntot_in + nout + nx]
        host_scr = refs[ntot_in + nout + nx:ntot_in + nout + nx + nscr]
        send_sems, recv_sems, local_sems = refs[ntot_in + nout + nx + nscr:]
        first = pl.program_id(0) == 0
        last = pl.program_id(0) == grid[0] - 1
        for ax in range(1, len(grid)):
            first = jnp.logical_and(first, pl.program_id(ax) == 0)
            last = jnp.logical_and(last, pl.program_id(ax) == grid[ax] - 1)
        x, y, c = _my_pos()
        me = 4 * x + 2 * y + c

        def each(fn_remote, fn_local):
            for a, (_, dests, _) in enumerate(xs):
                lo, nd = dests[0], len(dests)
                for k in range(1, NDEV):
                    px, py, pc = _peer(k, x, y, c)
                    pidx = 4 * px + 2 * py + pc
                    cp = pltpu.make_async_remote_copy(src_ref=sends[a].at[jnp.clip(pidx - lo, 0, nd - 1)], dst_ref=recvs[a].at[me],
                                                      send_sem=send_sems.at[a, k - 1], recv_sem=recv_sems.at[a, k - 1],
                                                      device_id=(px, py, pc), device_id_type=MESH)
                    fn_remote(cp, _in_set(pidx, dests), _in_set(me, dests))
                lc = pltpu.make_async_copy(sends[a].at[jnp.clip(me - lo, 0, nd - 1)], recvs[a].at[me], local_sems.at[a])
                fn_local(lc, _in_set(me, dests))

        def start_remote(cp, to_dest, _):
            pl.when(jnp.logical_and(first, to_dest))(cp.start)

        def start_local(lc, i_am_dest):
            pl.when(jnp.logical_and(first, i_am_dest))(lc.start)

        def wait_remote(cp, to_dest, i_am_dest):
            pl.when(jnp.logical_and(last, to_dest))(cp.wait_send)
            pl.when(jnp.logical_and(last, i_am_dest))(cp.wait_recv)

        def wait_local(lc, i_am_dest):
            pl.when(jnp.logical_and(last, i_am_dest))(lc.wait)

        each(start_remote, start_local)
        body(*host_in, *host_out, *host_scr)
        each(wait_remote, wait_local)

    res = pl.pallas_call(
        wrapped, name=name, grid=grid, in_specs=specs, out_specs=ospecs, out_shape=oshape, input_output_aliases=aliases,
        scratch_shapes=list(scratch_shapes) + [pltpu.SemaphoreType.DMA((nx, NDEV - 1)), pltpu.SemaphoreType.DMA((nx, NDEV - 1)),
                                               pltpu.SemaphoreType.DMA((nx,))],
        compiler_params=compiler_params,
    )(*ops)
    return res[:nout], res[nout:]


_HBM = pl.BlockSpec(memory_space=pltpu.HBM)
_SEM = pl.BlockSpec(memory_space=pltpu.SEMAPHORE)


def _pair_reduce(send, dests, name):
    nd = send.shape[0]

    def body(s_ref, o_ref, land, ssem, rsem):
        x, y, c = _my_pos()
        cps = []
        for i in range(nd):
            cp = pltpu.make_async_remote_copy(src_ref=s_ref.at[i], dst_ref=land.at[i], send_sem=ssem.at[i], recv_sem=rsem.at[i],
                                              device_id=(x, y, 1 - c), device_id_type=MESH)
            pl.when(c != (dests[i] & 1))(cp.start)
            cps.append(cp)
        for i in range(nd):
            mine = c == (dests[i] & 1)

            @pl.when(mine)
            def _():
                cps[i].wait_recv()
                o_ref[i] = (s_ref[i].astype(F32) + land[i].astype(F32)).astype(BF16)

            pl.when(jnp.logical_not(mine))(cps[i].wait_send)

    vm = pl.BlockSpec(memory_space=pltpu.VMEM)
    return pl.pallas_call(
        body, name=name, out_shape=jax.ShapeDtypeStruct(send.shape, send.dtype), in_specs=[vm], out_specs=vm,
        scratch_shapes=[pltpu.VMEM(send.shape, send.dtype), pltpu.SemaphoreType.DMA((nd,)), pltpu.SemaphoreType.DMA((nd,))],
        compiler_params=_cp(None, VMEM_BIG),
    )(send)


def _xchg_copies(xs_dests, sends, lands, ssem, rsem, lsem):
    x, y, c = _my_pos()
    me = 4 * x + 2 * y + c
    remote, local = [], []
    for a, dests in enumerate(xs_dests):
        same_core = dests[-1] == "same core"
        dests = dests[:-1] if same_core else dests
        lo, nd = dests[0], sends[a].shape[0]
        for k in range(1, NDEV):
            if same_core and k & 1:
                continue
            px, py, pc = _peer(k, x, y, c)
            pidx = 4 * px + 2 * py + pc
            cp = pltpu.make_async_remote_copy(src_ref=sends[a].at[jnp.clip(pidx - lo, 0, nd - 1)], dst_ref=lands[a].at[me],
                                              send_sem=ssem.at[a * (NDEV - 1) + k - 1], recv_sem=rsem.at[a * (NDEV - 1) + k - 1],
                                              device_id=(px, py, pc), device_id_type=MESH)
            remote.append((cp, _in_set(pidx, dests), _in_set(me, dests)))
        lc = pltpu.make_async_copy(sends[a].at[jnp.clip(me - lo, 0, nd - 1)], lands[a].at[me], lsem.at[a])
        local.append((lc, _in_set(me, dests)))
    return remote, local


def _xchg_start(xs, name, lands=None):
    n = len(xs)
    dests = [d for _, d in xs]
    sends = [pltpu.with_memory_space_constraint(s, pltpu.HBM) for s, _ in xs]
    lands = [None] * n if lands is None else lands
    lands = [pltpu.with_memory_space_constraint(lax.empty((NDEV,) + s.shape[1:], s.dtype) if l is None else l, pltpu.HBM)
             for (s, _), l in zip(xs, lands)]

    def body(*refs):
        send_refs, land_refs = refs[:n], refs[n:2 * n]
        ssem, rsem, lsem = refs[2 * n:2 * n + 3]
        token = refs[-1]
        remote, local = _xchg_copies(dests, send_refs, land_refs, ssem, rsem, lsem)
        for cp, to_dest, _ in remote:
            pl.when(to_dest)(cp.start)
        for lc, i_am_dest in local:
            pl.when(i_am_dest)(lc.start)
        token[...] = jnp.zeros_like(token)

    res = pl.pallas_call(
        body, name=name,
        out_shape=[pltpu.SemaphoreType.DMA((n * (NDEV - 1),)), pltpu.SemaphoreType.DMA((n * (NDEV - 1),)), pltpu.SemaphoreType.DMA((n,))]
        + [pltpu.HBM(a.shape, a.dtype) for a in list(sends) + list(lands)] + [jax.ShapeDtypeStruct((8, 128), F32)],
        in_specs=[_HBM] * (2 * n), out_specs=[_SEM, _SEM, _SEM] + [_HBM] * (2 * n) + [pl.BlockSpec(memory_space=pltpu.VMEM)],
        input_output_aliases={i: 3 + i for i in range(2 * n)},
        compiler_params=pltpu.CompilerParams(has_side_effects=pltpu.SideEffectType.DATAFLOW_SIDE_EFFECTING),
    )(*sends, *lands)
    return dict(sems=res[0:3], sends=res[3:3 + n], lands=res[3 + n:3 + 2 * n], dests=dests), res[-1]


def _xchg_wait(states, lands, land_of, after, name):
    flat = []
    for st in states:
        flat += list(st["sends"]) + list(st["sems"])
    nl = len(lands)

    def body(*refs):
        land_refs = refs[:nl]
        pos = nl
        for s, st in enumerate(states):
            n = len(st["dests"])
            send_refs = refs[pos:pos + n]
            ssem, rsem, lsem = refs[pos + n:pos + n + 3]
            pos += n + 3
            remote, local = _xchg_copies(st["dests"], send_refs, [land_refs[i] for i in land_of[s]], ssem, rsem, lsem)
            for cp, to_dest, i_am_dest in remote:
                pl.when(to_dest)(cp.wait_send)
                pl.when(i_am_dest)(cp.wait_recv)
            for lc, i_am_dest in local:
                pl.when(i_am_dest)(lc.wait)

    in_specs = [_HBM] * nl
    for st in states:
        in_specs += [_HBM] * len(st["dests"]) + [_SEM, _SEM, _SEM]
    return pl.pallas_call(
        body, name=name, out_shape=[pltpu.HBM(a.shape, a.dtype) for a in lands],
        in_specs=in_specs + [pl.BlockSpec(memory_space=pl.ANY)], out_specs=[_HBM] * nl,
        input_output_aliases={i: i for i in range(nl)},
        compiler_params=pltpu.CompilerParams(has_side_effects=pltpu.SideEffectType.DATAFLOW_SIDE_EFFECTING),
    )(*lands, *flat, after)


def _mm_tn(a, b, name):
    S, M = a.shape
    N = b.shape[1]
    tk = min(2048, S)
    tn = N if N <= 768 else (640 if N % 640 == 0 else 512)
    nk = S // tk

    def body(a_ref, b_ref, o_ref):
        @pl.when(pl.program_id(1) == 0)
        def _():
            o_ref[...] = jnp.zeros_like(o_ref)
        o_ref[...] += _dot(a_ref[...], b_ref[...], TN)

    return pl.pallas_call(
        body, name=name, out_shape=jax.ShapeDtypeStruct((M, N), F32), grid=(N // tn, nk),
        in_specs=[pl.BlockSpec((tk, M), lambda j, k: (k, 0)), pl.BlockSpec((tk, tn), lambda j, k: (k, j))],
        out_specs=pl.BlockSpec((M, tn), lambda j, k: (0, j)),
        compiler_params=_cp(("parallel", "arbitrary"), VMEM_BIG),
    )(a, b)


def _fwd_in(x, mod, g_pre, w_p):
    S = x.shape[0]
    tm = min(512, S)

    def body(x_ref, mod_ref, g_ref, w_ref, p_ref, h_ref):
        xv = x_ref[...]
        r = lax.rsqrt(jnp.mean(xv * xv, axis=-1, keepdims=True) + EPS)
        h = (((xv * r) * g_ref[...]) * (1.0 + mod_ref[1:2, :]) + mod_ref[0:1, :]).astype(BF16)
        h_ref[...] = h
        p_ref[...] = _dot(h, w_ref[...]).astype(BF16)

    return pl.pallas_call(
        body, name="fwd_in", out_shape=[jax.ShapeDtypeStruct((S, P_W), BF16), jax.ShapeDtypeStruct((S, D), BF16)],
        grid=(S // tm,),
        in_specs=[pl.BlockSpec((tm, D), lambda i: (i, 0)), _full((3, D)), _full((1, D)), _full((D, P_W))],
        out_specs=[pl.BlockSpec((tm, P_W), lambda i: (i, 0)), pl.BlockSpec((tm, D), lambda i: (i, 0))],
        compiler_params=_cp(("parallel",), VMEM_BIG),
    )(x, mod, g_pre, w_p)


def _swap16(v):
    lane = lax.broadcasted_iota(jnp.int32, v.shape, 1)
    return jnp.where((lane % 32) < 16, pltpu.roll(v, 112, 1), pltpu.roll(v, 16, 1))


def _rope(v, cos, sin):
    return v * cos + _swap16(v) * sin


def _rope_t(v, cos, sin):
    return v * cos - _swap16(v) * sin


def _head_mean(v):
    lo = lax.broadcasted_iota(jnp.int32, v.shape, 1) < 64
    m0 = jnp.sum(jnp.where(lo, v, 0.0), axis=-1, keepdims=True)
    m1 = jnp.sum(jnp.where(lo, 0.0, v), axis=-1, keepdims=True)
    return jnp.where(lo, m0, m1) * (1.0 / 64.0)


def _prep(p, cos, sin, qg, kg):
    S = p.shape[0]
    tm = min(512, S)

    def body(qa_ref, kv_ref, qr_ref, kr_ref, cos_ref, sin_ref, qg_ref, kg_ref, qt_ref, kh_ref, kt_ref, vh_ref, vta_ref, qr2_ref, kr2_ref):
        cos_v, sin_v = cos_ref[...], sin_ref[...]
        for g in range(4):
            xv = qa_ref[:, 128 * g:128 * g + 128].astype(F32)
            r = lax.rsqrt(_head_mean(xv * xv) + EPS)
            yt = (_rope((xv * r) * qg_ref[...], cos_v, sin_v) * (0.125 * LOG2E)).T
            qt_ref[2 * g] = yt[:DH].astype(BF16)
            qt_ref[2 * g + 1] = yt[DH:].astype(BF16)
        xv = kv_ref[:, :128].astype(F32)
        r = lax.rsqrt(_head_mean(xv * xv) + EPS)
        yv = _rope((xv * r) * kg_ref[...], cos_v, sin_v)
        kh_ref[0] = yv[:, :64].astype(BF16)
        kh_ref[1] = yv[:, 64:].astype(BF16)
        yt = yv.T
        kt_ref[0] = yt[:DH].astype(BF16)
        kt_ref[1] = yt[DH:].astype(BF16)
        vv = kv_ref[:, 128:].astype(F32)
        vh_ref[0] = vv[:, :64].astype(BF16)
        vh_ref[1] = vv[:, 64:].astype(BF16)
        vt = vv.T
        tail = (lax.broadcasted_iota(jnp.int32, (DHA - DH, tm), 0) == 0).astype(BF16)
        for kvh in range(2):
            vta_ref[kvh, 0:DH, :] = vt[DH * kvh:DH * kvh + DH].astype(BF16)
            vta_ref[kvh, DH:DHA, :] = tail
        for g in range(2):
            sl = slice(128 * g, 128 * g + 128)
            qr2_ref[:, sl] = _rope(qr_ref[:, sl].astype(F32), cos_v, sin_v)
            kr2_ref[:, sl] = _rope(kr_ref[:, sl].astype(F32), cos_v, sin_v) * 0.125

    hm = lambda n: pl.BlockSpec((n, tm, DH), lambda i: (0, i, 0))
    ht = lambda n, r: pl.BlockSpec((n, r, tm), lambda i: (0, 0, i))
    return pl.pallas_call(
        body, name="prep",
        out_shape=[jax.ShapeDtypeStruct((8, DH, S), BF16), jax.ShapeDtypeStruct((2, S, DH), BF16), jax.ShapeDtypeStruct((2, DH, S), BF16),
                   jax.ShapeDtypeStruct((2, S, DH), BF16), jax.ShapeDtypeStruct((2, DHA, S), BF16),
                   jax.ShapeDtypeStruct((S, 256), F32), jax.ShapeDtypeStruct((S, 256), F32)],
        grid=(S // tm,),
        in_specs=[pl.BlockSpec((tm, 512), lambda i: (i, O_QA // 512)), pl.BlockSpec((tm, 256), lambda i: (i, O_KA // 256)),
                  pl.BlockSpec((tm, 256), lambda i: (i, O_QR // 256)), pl.BlockSpec((tm, 256), lambda i: (i, O_KR // 256)),
                  pl.BlockSpec((tm, 128), lambda i: (i, 0)), pl.BlockSpec((tm, 128), lambda i: (i, 0)), _full((1, 128)), _full((1, 128))],
        out_specs=[ht(8, DH), hm(2), ht(2, DH), hm(2), ht(2, DHA), pl.BlockSpec((tm, 256), lambda i: (i, 0)), pl.BlockSpec((tm, 256), lambda i: (i, 0))],
        compiler_params=_cp(("parallel",)),
    )(p, p, p, p, cos, sin, qg, kg)


def _attn_fwd(qt, kh, vta):
    S = qt.shape[2]
    tq, tk = min(1024, S), min(512, S)
    nj = S // tk

    def body(q_ref, k_ref, v_ref, o_ref, ot_ref, lse_ref, m_s, acc_s):
        j = pl.program_id(1)

        @pl.when(j == 0)
        def _():
            m_s[...] = jnp.full_like(m_s, -jnp.inf)
            acc_s[...] = jnp.zeros_like(acc_s)

        m_all = m_s[...]
        st = {0: _dot(k_ref[0], q_ref[0])}
        m_new, acc_new = [], []
        for h in range(8):
            if h + 1 < 8:
                st[h + 1] = _dot(k_ref[(h + 1) // 4], q_ref[h + 1])
            m_old = m_all[h:h + 1, :]
            mn = jnp.maximum(m_old, jnp.max(st[h], axis=0, keepdims=True))
            pt = jnp.exp2(st[h] - mn).astype(BF16)
            acc_new.append(jnp.exp2(m_old - mn) * acc_s[h] + _dot(v_ref[h // 4], pt))
            m_new.append(mn)
            del st[h]
        for h in range(8):
            acc_s[h] = acc_new[h]
            m_s[h:h + 1, :] = m_new[h]

        @pl.when(j == nj - 1)
        def _():
            for h in range(8):
                ot = acc_s[h, 0:DH, :] / acc_s[h, DH:DH + 1, :]
                ot_ref[h] = ot
                o_ref[:, DH * h:DH * h + DH] = ot.T
                lse_ref[h // 4, h % 4:h % 4 + 1, :] = m_s[h:h + 1, :] + jnp.log2(acc_s[h, DH:DH + 1, :])

    return pl.pallas_call(
        body, name="attn_fwd",
        out_shape=[jax.ShapeDtypeStruct((S, 512), F32), jax.ShapeDtypeStruct((8, DH, S), F32), jax.ShapeDtypeStruct((2, 4, S), F32)],
        grid=(S // tq, nj),
        in_specs=[pl.BlockSpec((8, DH, tq), lambda i, j: (0, 0, i)), pl.BlockSpec((2, tk, DH), lambda i, j: (0, j, 0)),
                  pl.BlockSpec((2, DHA, tk), lambda i, j: (0, 0, j))],
        out_specs=[pl.BlockSpec((tq, 512), lambda i, j: (i, 0)), pl.BlockSpec((8, DH, tq), lambda i, j: (0, 0, i)),
                   pl.BlockSpec((2, 4, tq), lambda i, j: (0, 0, i))],
        scratch_shapes=[pltpu.VMEM((8, tq), F32), pltpu.VMEM((8, DHA, tq), F32)],
        compiler_params=_cp(("parallel", "arbitrary"), VMEM_BIG),
    )(qt, kh, vta)


def _ret_tables(wf, wb):
    C = CH

    def body(wf_ref, wb_ref, dc_ref, qdf_ref, qdb_ref, kdf_ref, kdb_ref, a_ref):
        def logsig(w):
            z = jnp.exp(-jnp.abs(w))
            u = 1.0 + z
            l1p = jnp.where(u == 1.0, z, jnp.log(u) * (z / jnp.where(u == 1.0, 1.0, u - 1.0)))
            return jnp.minimum(w, 0.0) - l1p

        lgf, lgb = logsig(wf_ref[...]), logsig(wb_ref[...])
        lane4 = lax.broadcasted_iota(jnp.int32, (1, 4), 1)

        def pick(lg, h):
            return jnp.sum(jnp.where(lane4 == h, lg, 0.0), axis=-1, keepdims=True)

        ii = lax.broadcasted_iota(jnp.int32, (C, C), 0).astype(F32)
        jj = lax.broadcasted_iota(jnp.int32, (C, C), 1).astype(F32)
        dif = ii - jj
        hd = lax.broadcasted_iota(jnp.int32, (C, 256), 1) // DH
        lf_l = jnp.zeros((C, 256), F32)
        lb_l = jnp.zeros((C, 256), F32)
        for h in range(HR):
            lf, lb = pick(lgf, h), pick(lgb, h)
            dc_ref[h] = jnp.where(dif >= 0, jnp.exp(lf * jnp.maximum(dif, 0.0)), jnp.exp(lb * jnp.maximum(-dif, 0.0)))
            lf_l = jnp.where(hd == h, lf, lf_l)
            lb_l = jnp.where(hd == h, lb, lb_l)
            a_ref[h:h + 1, :] = jnp.broadcast_to(jnp.exp(lf * C), (1, 128))
            a_ref[HR + h:HR + h + 1, :] = jnp.broadcast_to(jnp.exp(lb * C), (1, 128))
        ri = lax.broadcasted_iota(jnp.int32, (C, 256), 0).astype(F32)
        qdf_ref[...] = jnp.exp(lf_l * (ri + 1.0))
        qdb_ref[...] = jnp.exp(lb_l * (C - ri))
        kdf_ref[...] = jnp.exp(lf_l * (C - 1.0 - ri))
        kdb_ref[...] = jnp.exp(lb_l * ri)

    t = jax.ShapeDtypeStruct((C, 256), F32)
    return pl.pallas_call(body, name="ret_tables",
                          out_shape=[jax.ShapeDtypeStruct((HR, C, C), F32), t, t, t, t, jax.ShapeDtypeStruct((8, 128), F32)])(wf, wb)


def _ret_states(kr2, p, kdf, kdb, adec):
    S = kr2.shape[0]
    C, N = CH, S // CH

    def body(kf_ref, vf_ref, kb_ref, vb_ref, kdf_ref, kdb_ref, a_ref, rf_ref, rb_ref, sf, sb):
        @pl.when(pl.program_id(0) == 0)
        def _():
            sf[...] = jnp.zeros_like(sf)
            sb[...] = jnp.zeros_like(sb)

        rf_ref[0] = sf[...]
        rb_ref[0] = sb[...]
        kdfw = (kf_ref[...] * kdf_ref[...]).astype(BF16)
        kdbw = (kb_ref[...] * kdb_ref[...]).astype(BF16)
        vf, vb = vf_ref[...].astype(BF16), vb_ref[...].astype(BF16)
        kvf = [_dot(kdfw[:, _ks(h)], vf[:, _vs(h)], TN) for h in range(HR)]
        kvb = [_dot(kdbw[:, _ks(h)], vb[:, _vs(h)], TN) for h in range(HR)]
        for h in range(HR):
            sf[h] = a_ref[h:h + 1, :] * sf[h] + kvf[h]
            sb[h] = a_ref[HR + h:HR + h + 1, :] * sb[h] + kvb[h]

    st = jax.ShapeDtypeStruct((N, HR, DH, DV), F32)
    return pl.pallas_call(
        body, name="ret_states", out_shape=[st, st], grid=(N,),
        in_specs=[pl.BlockSpec((C, 256), lambda t: (t, 0)), pl.BlockSpec((C, 512), lambda t: (t, O_VR // 512)),
                  pl.BlockSpec((C, 256), lambda t: (N - 1 - t, 0)), pl.BlockSpec((C, 512), lambda t: (N - 1 - t, O_VR // 512)),
                  _full((C, 256)), _full((C, 256)), _full((8, 128))],
        out_specs=[pl.BlockSpec((1, HR, DH, DV), lambda t: (t, 0, 0, 0)), pl.BlockSpec((1, HR, DH, DV), lambda t: (N - 1 - t, 0, 0, 0))],
        scratch_shapes=[pltpu.VMEM((HR, DH, DV), F32), pltpu.VMEM((HR, DH, DV), F32)],
        compiler_params=_cp(("arbitrary",)),
    )(kr2, p, kr2, p, kdf, kdb, adec)


def _ks(h):
    return slice(DH * h, DH * h + DH)


def _vs(h):
    return slice(DV * h, DV * h + DV)


def _ret_heads_fwd(qb, kb, vb, qfw, qbw, dc_ref, rf_ref, rb_ref):
    hs = range(HR)
    s = [_dot(qb[:, _ks(h)], kb[:, _ks(h)], NT) for h in hs]
    inter = [_dot(qfw[:, _ks(h)], rf_ref[0, h].astype(BF16)) + _dot(qbw[:, _ks(h)], rb_ref[0, h].astype(BF16)) for h in hs]
    sd = [s[h] * dc_ref[h] for h in hs]
    o = [_dot(sd[h].astype(BF16), vb[:, _vs(h)]) + inter[h] for h in hs]
    return sd, o


def _ret_out(qr2, kr2, p, rf, rb, dc, qdf, qdb, gn):
    S = qr2.shape[0]
    C, N = CH, S // CH

    def body(q_ref, k_ref, v_ref, z_ref, rf_ref, rb_ref, dc_ref, qdf_ref, qdb_ref, gn_ref, yr_ref):
        qv = q_ref[...]
        qb, kb, vb = qv.astype(BF16), k_ref[...].astype(BF16), v_ref[...].astype(BF16)
        qfw, qbw = (qv * qdf_ref[...]).astype(BF16), (qv * qdb_ref[...]).astype(BF16)
        _, o = _ret_heads_fwd(qb, kb, vb, qfw, qbw, dc_ref, rf_ref, rb_ref)
        for h in range(HR):
            vs = _vs(h)
            mu = jnp.mean(o[h], axis=-1, keepdims=True)
            var = jnp.mean(jnp.square(o[h] - mu), axis=-1, keepdims=True)
            on = (o[h] - mu) * lax.rsqrt(var + EPS)
            z = z_ref[:, vs].astype(F32)
            yr_ref[:, vs] = ((on * gn_ref[:, vs]) * (z * _sigmoid(z))).astype(BF16)

    return pl.pallas_call(
        body, name="ret_out", out_shape=jax.ShapeDtypeStruct((S, 512), BF16), grid=(N,),
        in_specs=[pl.BlockSpec((C, 256), lambda t: (t, 0)), pl.BlockSpec((C, 256), lambda t: (t, 0)),
                  pl.BlockSpec((C, 512), lambda t: (t, O_VR // 512)), pl.BlockSpec((C, 512), lambda t: (t, O_ZR // 512)),
                  pl.BlockSpec((1, HR, DH, DV), lambda t: (t, 0, 0, 0)), pl.BlockSpec((1, HR, DH, DV), lambda t: (t, 0, 0, 0)),
                  _full((HR, C, C)), _full((C, 256)), _full((C, 256)), _full((1, 512))],
        out_specs=pl.BlockSpec((C, 512), lambda t: (t, 0)),
        compiler_params=_cp(("parallel",)),
    )(qr2, kr2, p, p, rf, rb, dc, qdf, qdb, gn)


def _mid(x, tgt, mod, g_post, o_att, p, yr, w_pa, w_pr, w_out):
    S = x.shape[0]
    tm = min(256, S)

    def body(x_ref, t_ref, mod_ref, gp_ref, o_ref, za_ref, gl_ref, yr_ref, wpa_ref, wpr_ref, wout_ref,
             dout_ref, do_ref, dpm_ref, dyr_ref, mb_ref, dub_ref, yab_ref, dab_ref, drb_ref, sums_ref):
        @pl.when(pl.program_id(0) == 0)
        def _():
            sums_ref[...] = jnp.zeros_like(sums_ref)

        za = za_ref[...].astype(F32)
        sa = _sigmoid(za)
        sil = za * sa
        ov = o_ref[...]
        ya_b = (ov * sil).astype(BF16)
        yr_b = yr_ref[...]
        av = _dot(ya_b, wpa_ref[...])
        rv = _dot(yr_b, wpr_ref[...])
        ga = _sigmoid(gl_ref[:, :D].astype(F32))
        gr = _sigmoid(gl_ref[:, D:].astype(F32))
        mb = (ga * av + gr * rv).astype(BF16)
        u = _dot(mb, wout_ref[...])
        r2 = lax.rsqrt(jnp.mean(u * u, axis=-1, keepdims=True) + EPS)
        un = u * r2
        gp = gp_ref[...]
        yv = un * gp
        gate = mod_ref[2:3, :]
        err = (x_ref[...] + gate * yv) - t_ref[...]
        dout = err * (1.0 / D)
        dout_ref[...] = dout
        dy = dout * gate
        sums_ref[0:1, :] += jnp.sum(dout * yv, axis=0, keepdims=True)
        sums_ref[1:2, :] += jnp.sum(dy * un, axis=0, keepdims=True)
        sums_ref[2:3, :] += jnp.sum(err * err, axis=0, keepdims=True)
        dyg = dy * gp
        du_b = (r2 * (dyg - un * jnp.mean(dyg * un, axis=-1, keepdims=True))).astype(BF16)
        dm = _dot(du_b, wout_ref[...], NT)
        da_b = (dm * ga).astype(BF16)
        dr_b = (dm * gr).astype(BF16)
        dpm_ref[:, :D] = (dm * av * (ga * (1.0 - ga))).astype(BF16)
        dpm_ref[:, D:2 * D] = (dm * rv * (gr * (1.0 - gr))).astype(BF16)
        dya = _dot(da_b, wpa_ref[...], NT)
        dyr_ref[...] = _dot(dr_b, wpr_ref[...], NT)
        dov = dya * sil
        for g in range(4):
            dt = dov[:, 128 * g:128 * g + 128].T
            do_ref[2 * g] = dt[:DH].astype(BF16)
            do_ref[2 * g + 1] = dt[DH:].astype(BF16)
        dpm_ref[:, 2 * D:] = (dya * ov * (sa * (1.0 + za * (1.0 - sa)))).astype(BF16)
        mb_ref[...] = mb
        dub_ref[...] = du_b
        yab_ref[...] = ya_b
        dab_ref[...] = da_b
        drb_ref[...] = dr_b

    row = lambda w: pl.BlockSpec((tm, w), lambda i: (i, 0))
    sd = lambda w, dt: jax.ShapeDtypeStruct((S, w), dt)
    return pl.pallas_call(
        body, name="mid",
        out_shape=[sd(D, F32), jax.ShapeDtypeStruct((8, DH, S), BF16), sd(2560, BF16), sd(512, F32), sd(D, BF16), sd(D, BF16), sd(512, BF16),
                   sd(D, BF16), sd(D, BF16), jax.ShapeDtypeStruct((8, D), F32)],
        grid=(S // tm,),
        in_specs=[row(D), row(D), _full((3, D)), _full((1, D)), row(512), pl.BlockSpec((tm, 512), lambda i: (i, O_ZA // 512)),
                  pl.BlockSpec((tm, 2048), lambda i: (i, 0)), row(512), _full((512, D)), _full((512, D)), _full((D, D))],
        out_specs=[row(D), pl.BlockSpec((8, DH, tm), lambda i: (0, 0, i)), row(2560), row(512), row(D), row(D), row(512), row(D), row(D),
                   _full((8, D))],
        compiler_params=_cp(("arbitrary",), VMEM_BIG),
    )(x, tgt, mod, g_post, o_att, p, p, yr, w_pa, w_pr, w_out)


def _attn_bwd(qt, kh, kt, vh, dot_, ot, lse, xs):
    S = qt.shape[2]
    tq, tk = min(512, S), min(1024, S)

    def body(q_ref, k_ref, kt_ref, v_ref, do_ref, o_ref, lse_ref, dq_ref, dk_ref, dv_ref):
        j, i = pl.program_id(0), pl.program_id(1)
        cols = pl.ds(pl.multiple_of(i * tq, tq), tq)
        st = {0: _dot(k_ref[0], q_ref[0])}
        dpt = {0: _dot(v_ref[0], do_ref[0])}
        dk_acc, dv_acc, dqs = [None, None], [None, None], []
        for h in range(8):
            g = h // 4
            if h + 1 < 8:
                st[h + 1] = _dot(k_ref[(h + 1) // 4], q_ref[h + 1])
                dpt[h + 1] = _dot(v_ref[(h + 1) // 4], do_ref[h + 1])
            qt_h, dot_h = q_ref[h], do_ref[h]
            delta = jnp.sum(dot_h.astype(F32) * o_ref[h], axis=0, keepdims=True)
            pt = jnp.exp2(st[h] - lse_ref[g, h % 4:h % 4 + 1, :])
            dst = (pt * (dpt[h] - delta)).astype(BF16)
            dv_h = _dot(dot_h, pt.astype(BF16), NT)
            dk_h = _dot(qt_h, dst, NT)
            dqs.append(_dot(kt_ref[g], dst))
            dv_acc[g] = dv_h if dv_acc[g] is None else dv_acc[g] + dv_h
            dk_acc[g] = dk_h if dk_acc[g] is None else dk_acc[g] + dk_h
            del st[h], dpt[h]

        @pl.when(i == 0)
        def _():
            for g in range(2):
                dk_ref[g] = dk_acc[g]
                dv_ref[g] = dv_acc[g]

        @pl.when(i > 0)
        def _():
            for g in range(2):
                dk_ref[g] += dk_acc[g]
                dv_ref[g] += dv_acc[g]

        @pl.when(j == 0)
        def _():
            for h in range(8):
                dq_ref[h, :, cols] = dqs[h]

        @pl.when(j > 0)
        def _():
            for h in range(8):
                dq_ref[h, :, cols] += dqs[h]

    return _host_call(
        body, xs, name="attn_bwd",
        out_shape=[jax.ShapeDtypeStruct((8, DH, S), F32), jax.ShapeDtypeStruct((2, DH, S), F32), jax.ShapeDtypeStruct((2, DH, S), F32)],
        grid=(S // tk, S // tq),
        in_specs=[pl.BlockSpec((8, DH, tq), lambda j, i: (0, 0, i)), pl.BlockSpec((2, tk, DH), lambda j, i: (0, j, 0)),
                  pl.BlockSpec((2, DH, tk), lambda j, i: (0, 0, j)), pl.BlockSpec((2, tk, DH), lambda j, i: (0, j, 0)),
                  pl.BlockSpec((8, DH, tq), lambda j, i: (0, 0, i)), pl.BlockSpec((8, DH, tq), lambda j, i: (0, 0, i)),
                  pl.BlockSpec((2, 4, tq), lambda j, i: (0, 0, i))],
        out_specs=[pl.BlockSpec((8, DH, S), lambda j, i: (0, 0, 0)), pl.BlockSpec((2, DH, tk), lambda j, i: (0, 0, j)),
                   pl.BlockSpec((2, DH, tk), lambda j, i: (0, 0, j))],
        scratch_shapes=[], operands=(qt, kh, kt, vh, dot_, ot, lse),
        compiler_params=_cp(("arbitrary", "arbitrary"), VMEM_BIG),
    )


def _attn_prep_bwd(dqt, dkt, dvt, p, cos, sin, qg, kg):
    S = dqt.shape[2]
    tm = min(512, S)

    def body(dq_ref, dk_ref, dv_ref, qa_ref, ka_ref, cos_ref, sin_ref, qg_ref, kg_ref, dp_ref, gs_ref):
        @pl.when(pl.program_id(0) == 0)
        def _():
            gs_ref[...] = jnp.zeros_like(gs_ref)

        cos_v, sin_v = cos_ref[...], sin_ref[...]

        def pair(ref, a):
            return jnp.concatenate([ref[a], ref[a + 1]], axis=0).T

        def norm_bwd(dyv, xv, gv, row):
            r = lax.rsqrt(_head_mean(xv * xv) + EPS)
            xn = xv * r
            dxh = _rope_t(dyv, cos_v, sin_v)
            gs_ref[row:row + 1, :] += jnp.sum(dxh * xn, axis=0, keepdims=True)
            dg = dxh * gv
            return r * (dg - xn * _head_mean(dg * xn))

        for g in range(4):
            sl = slice(128 * g, 128 * g + 128)
            dp_ref[:, sl] = norm_bwd(pair(dq_ref, 2 * g) * 0.125, qa_ref[:, sl].astype(F32), qg_ref[...], 0).astype(BF16)
        dp_ref[:, 512:640] = norm_bwd(pair(dk_ref, 0) * LN2, ka_ref[...].astype(F32), kg_ref[...], 1).astype(BF16)
        dp_ref[:, 640:768] = pair(dv_ref, 0).astype(BF16)

    ht = lambda n: pl.BlockSpec((n, DH, tm), lambda i: (0, 0, i))
    return pl.pallas_call(
        body, name="attn_prep_bwd", out_shape=[jax.ShapeDtypeStruct((S, 768), BF16), jax.ShapeDtypeStruct((8, 128), F32)],
        grid=(S // tm,),
        in_specs=[ht(8), ht(2), ht(2),
                  pl.BlockSpec((tm, 512), lambda i: (i, O_QA // 512)), pl.BlockSpec((tm, 128), lambda i: (i, O_KA // 128)),
                  pl.BlockSpec((tm, 128), lambda i: (i, 0)), pl.BlockSpec((tm, 128), lambda i: (i, 0)), _full((1, 128)), _full((1, 128))],
        out_specs=[pl.BlockSpec((tm, 768), lambda i: (i, 0)), _full((8, 128))],
        compiler_params=_cp(("arbitrary",)),
    )(dqt, dkt, dvt, p, p, cos, sin, qg, kg)


def _ret_bwd_chunk(qr2, kr2, p, rf, rb, dc, qdf, qdb, gn, dyr, cos, sin, xs):
    S = qr2.shape[0]
    C, N = CH, S // CH

    def body(q_ref, k_ref, v_ref, z_ref, rf_ref, rb_ref, dc_ref, qdf_ref, qdb_ref, gn_ref, dyr_ref, cos_ref, sin_ref,
             dpa_ref, dk_ref, dv_ref, drf_ref, drb_ref, dgn_ref, dlg_ref, dqs):
        @pl.when(pl.program_id(0) == 0)
        def _():
            dgn_ref[...] = jnp.zeros_like(dgn_ref)
            dlg_ref[...] = jnp.zeros_like(dlg_ref)

        qv = q_ref[...]
        qb, kb, vb = qv.astype(BF16), k_ref[...].astype(BF16), v_ref[...].astype(BF16)
        qf32, qb32 = qv * qdf_ref[...], qv * qdb_ref[...]
        qfw, qbw = qf32.astype(BF16), qb32.astype(BF16)
        ii = lax.broadcasted_iota(jnp.int32, (C, C), 0).astype(F32)
        jj = lax.broadcasted_iota(jnp.int32, (C, C), 1).astype(F32)
        dif = ii - jj
        ri = lax.broadcasted_iota(jnp.int32, (C, 1), 0).astype(F32)
        hs = range(HR)
        sd, o = _ret_heads_fwd(qb, kb, vb, qfw, qbw, dc_ref, rf_ref, rb_ref)
        do_b = []
        for h in hs:
            vs = _vs(h)
            mu = jnp.mean(o[h], axis=-1, keepdims=True)
            rstd = lax.rsqrt(jnp.mean(jnp.square(o[h] - mu), axis=-1, keepdims=True) + EPS)
            on = (o[h] - mu) * rstd
            z = z_ref[:, vs].astype(F32)
            sz = _sigmoid(z)
            dy = dyr_ref[:, vs]
            gnv = gn_ref[:, vs]
            dpa_ref[:, 256 + DV * h:256 + DV * h + DV] = (dy * (on * gnv) * (sz * (1.0 + z * (1.0 - sz)))).astype(BF16)
            dys = dy * (z * sz)
            dgn_ref[:, vs] += jnp.sum(dys * on, axis=0, keepdims=True)
            don = dys * gnv
            do = rstd * (don - jnp.mean(don, axis=-1, keepdims=True) - on * jnp.mean(don * on, axis=-1, keepdims=True))
            do_b.append(do.astype(BF16))
        dpm = [_dot(do_b[h], vb[:, _vs(h)], NT) for h in hs]
        dqf = [_dot(do_b[h], rf_ref[0, h].astype(BF16), NT) for h in hs]
        dqb = [_dot(do_b[h], rb_ref[0, h].astype(BF16), NT) for h in hs]
        for h in hs:
            dv_ref[:, _vs(h)] = _dot(sd[h].astype(BF16), do_b[h], TN)
            drf_ref[0, h] = _dot(qfw[:, _ks(h)], do_b[h], TN)
            drb_ref[0, h] = _dot(qbw[:, _ks(h)], do_b[h], TN)
        dsd = [(dpm[h] * dc_ref[h]).astype(BF16) for h in hs]
        for h in hs:
            ks = _ks(h)
            dqs[:, ks] = _dot(dsd[h], kb[:, ks]) + dqf[h] * qdf_ref[:, ks] + dqb[h] * qdb_ref[:, ks]
            dk_ref[:, ks] = _dot(dsd[h], qb[:, ks], TN)
        for h in hs:
            ks = _ks(h)
            e = dpm[h] * sd[h]
            lf = _sum11(e * jnp.maximum(dif, 0.0)) + _sum11(jnp.sum(qf32[:, ks] * dqf[h], axis=-1, keepdims=True) * (ri + 1.0))
            lb = _sum11(e * jnp.maximum(-dif, 0.0)) + _sum11(jnp.sum(qb32[:, ks] * dqb[h], axis=-1, keepdims=True) * (C - ri))
            dlg_ref[h:h + 1, :] += jnp.broadcast_to(lf, (1, 128))
            dlg_ref[HR + h:HR + h + 1, :] += jnp.broadcast_to(lb, (1, 128))
        cos_v, sin_v = cos_ref[...], sin_ref[...]
        for g in range(2):
            sl = slice(128 * g, 128 * g + 128)
            dpa_ref[:, sl] = _rope_t(dqs[:, sl], cos_v, sin_v).astype(BF16)

    st = jax.ShapeDtypeStruct((N, HR, DH, DV), F32)
    stb = lambda: pl.BlockSpec((1, HR, DH, DV), lambda t: (t, 0, 0, 0))
    return _host_call(
        body, xs, name="ret_bwd_chunk",
        out_shape=[jax.ShapeDtypeStruct((S, 768), BF16), jax.ShapeDtypeStruct((S, 256), F32), jax.ShapeDtypeStruct((S, 512), F32), st, st,
                   jax.ShapeDtypeStruct((1, 512), F32), jax.ShapeDtypeStruct((8, 128), F32)],
        grid=(N,),
        in_specs=[pl.BlockSpec((C, 256), lambda t: (t, 0)), pl.BlockSpec((C, 256), lambda t: (t, 0)),
                  pl.BlockSpec((C, 512), lambda t: (t, O_VR // 512)), pl.BlockSpec((C, 512), lambda t: (t, O_ZR // 512)),
                  stb(), stb(), _full((HR, C, C)), _full((C, 256)), _full((C, 256)), _full((1, 512)),
                  pl.BlockSpec((C, 512), lambda t: (t, 0)), pl.BlockSpec((C, 128), lambda t: (t, 0)), pl.BlockSpec((C, 128), lambda t: (t, 0))],
        out_specs=[pl.BlockSpec((C, 768), lambda t: (t, 0)), pl.BlockSpec((C, 256), lambda t: (t, 0)), pl.BlockSpec((C, 512), lambda t: (t, 0)),
                   stb(), stb(), _full((1, 512)), _full((8, 128))],
        scratch_shapes=[pltpu.VMEM((C, 256), F32)], operands=(qr2, kr2, p, p, rf, rb, dc, qdf, qdb, gn, dyr, cos, sin),
        compiler_params=_cp(("arbitrary",)),
    )


def _ret_bwd_scan(kr2, p, rf, rb, drf, drb, kdf, kdb, adec):
    S = kr2.shape[0]
    C, N = CH, S // CH

    def body(kf_ref, vf_ref, kb_ref, vb_ref, rf_ref, rb_ref, drf_ref, drb_ref, kdf_ref, kdb_ref, a_ref,
             dkf_ref, dkb_ref, dvf_ref, dvb_ref, dlg_ref, gf, gb):
        @pl.when(pl.program_id(0) == 0)
        def _():
            gf[...] = jnp.zeros_like(gf)
            gb[...] = jnp.zeros_like(gb)
            dlg_ref[...] = jnp.zeros_like(dlg_ref)

        ri = lax.broadcasted_iota(jnp.int32, (C, 1), 0).astype(F32)

        def one(k_ref, v_ref, r_ref, dr_ref, kd_ref, g_s, dk_ref, dv_ref, row0, wexp):
            kd32 = k_ref[...] * kd_ref[...]
            kdw = kd32.astype(BF16)
            vb = v_ref[...].astype(BF16)
            for h in range(HR):
                ks, vs = _ks(h), _vs(h)
                gst = g_s[h]
                g_b = gst.astype(BF16)
                dkd = _dot(vb[:, vs], g_b, NT)
                dk_ref[:, ks] = dkd * kd_ref[:, ks]
                dv_ref[:, vs] = _dot(kdw[:, ks], g_b)
                av = a_ref[row0 + h:row0 + h + 1, :]
                lg = (_sum11(jnp.sum(kd32[:, ks] * dkd, axis=-1, keepdims=True) * wexp)
                      + C * av[:, 0:1] * _sum11(r_ref[0, h] * gst))
                dlg_ref[row0 + h:row0 + h + 1, :] += jnp.broadcast_to(lg, (1, 128))
                g_s[h] = dr_ref[0, h] + av * gst

        one(kf_ref, vf_ref, rf_ref, drf_ref, kdf_ref, gf, dkf_ref, dvf_ref, 0, C - 1.0 - ri)
        one(kb_ref, vb_ref, rb_ref, drb_ref, kdb_ref, gb, dkb_ref, dvb_ref, HR, ri)

    fwd = lambda w, off=0: pl.BlockSpec((C, w), lambda t: (N - 1 - t, off))
    bwd = lambda w, off=0: pl.BlockSpec((C, w), lambda t: (t, off))
    stf = lambda: pl.BlockSpec((1, HR, DH, DV), lambda t: (N - 1 - t, 0, 0, 0))
    stb = lambda: pl.BlockSpec((1, HR, DH, DV), lambda t: (t, 0, 0, 0))
    return pl.pallas_call(
        body, name="ret_bwd_scan",
        out_shape=[jax.ShapeDtypeStruct((S, 256), F32), jax.ShapeDtypeStruct((S, 256), F32), jax.ShapeDtypeStruct((S, 512), F32),
                   jax.ShapeDtypeStruct((S, 512), F32), jax.ShapeDtypeStruct((8, 128), F32)],
        grid=(N,),
        in_specs=[fwd(256), fwd(512, O_VR // 512), bwd(256), bwd(512, O_VR // 512), stf(), stb(), stf(), stb(),
                  _full((C, 256)), _full((C, 256)), _full((8, 128))],
        out_specs=[fwd(256), bwd(256), fwd(512), bwd(512), _full((8, 128))],
        scratch_shapes=[pltpu.VMEM((HR, DH, DV), F32), pltpu.VMEM((HR, DH, DV), F32)],
        compiler_params=_cp(("arbitrary",)),
    )(kr2, p, kr2, p, rf, rb, drf, drb, kdf, kdb, adec)


def _ret_bwd_final(dk_i, dkf, dkb, dv_i, dvf, dvb, cos, sin):
    S = dk_i.shape[0]
    tm = min(512, S)

    def body(a_ref, b_ref, c_ref, d_ref, e_ref, f_ref, cos_ref, sin_ref, o_ref):
        o_ref[:, :512] = (d_ref[...] + e_ref[...] + f_ref[...]).astype(BF16)
        cos_v, sin_v = cos_ref[...], sin_ref[...]
        for g in range(2):
            sl = slice(128 * g, 128 * g + 128)
            dk = a_ref[:, sl] + b_ref[:, sl] + c_ref[:, sl]
            o_ref[:, 512 + 128 * g:512 + 128 * g + 128] = (_rope_t(dk, cos_v, sin_v) * 0.125).astype(BF16)

    row = lambda w: pl.BlockSpec((tm, w), lambda i: (i, 0))
    return pl.pallas_call(
        body, name="ret_bwd_final", out_shape=jax.ShapeDtypeStruct((S, 768), BF16), grid=(S // tm,),
        in_specs=[row(256), row(256), row(256), row(512), row(512), row(512), row(128), row(128)], out_specs=row(768),
        compiler_params=_cp(("parallel",)),
    )(dk_i, dkf, dkb, dv_i, dvf, dvb, cos, sin)


def _bwd_in(dpm, dpa, dpra, dprb, w_p, x, dout, mod, g_pre, xs):
    S = x.shape[0]
    tm = min(256, S)

    def body(a_ref, b_ref, c_ref, d_ref, w_ref, x_ref, dout_ref, mod_ref, g_ref, gx_ref, sums_ref):
        @pl.when(pl.program_id(0) == 0)
        def _():
            sums_ref[...] = jnp.zeros_like(sums_ref)

        dh = (_dot(a_ref[...], w_ref[:, :O_QA], NT) + _dot(b_ref[...], w_ref[:, O_QA:O_QR], NT)
              + _dot(c_ref[...], w_ref[:, O_QR:O_VR], NT) + _dot(d_ref[...], w_ref[:, O_VR:], NT))
        xv = x_ref[...]
        r = lax.rsqrt(jnp.mean(xv * xv, axis=-1, keepdims=True) + EPS)
        xn = xv * r
        gv = g_ref[...]
        sc1 = 1.0 + mod_ref[1:2, :]
        sums_ref[0:1, :] += jnp.sum(dh, axis=0, keepdims=True)
        sums_ref[1:2, :] += jnp.sum(dh * (xn * gv), axis=0, keepdims=True)
        sums_ref[2:3, :] += jnp.sum(dh * xn, axis=0, keepdims=True) * sc1
        dxn = dh * (gv * sc1)
        gx_ref[...] = dout_ref[...] + r * (dxn - xn * jnp.mean(dxn * xn, axis=-1, keepdims=True))

    row = lambda w: pl.BlockSpec((tm, w), lambda i: (i, 0))
    return _host_call(
        body, xs, name="bwd_in", out_shape=[jax.ShapeDtypeStruct((S, D), F32), jax.ShapeDtypeStruct((8, D), F32)], grid=(S // tm,),
        in_specs=[row(2560), row(768), row(768), row(768), _full((D, P_W)), row(D), row(D), _full((3, D)), _full((1, D))],
        out_specs=[row(D), _full((8, D))], scratch_shapes=[], operands=(dpm, dpa, dpra, dprb, w_p, x, dout, mod, g_pre),
        compiler_params=_cp(("arbitrary",), VMEM_BIG),
    )


SMALL = ("b_ada", "g_pre", "qn_g", "kn_g", "w_dec_f", "w_dec_b", "gn_g", "g_post")


def _small_update(gathered, wmv):
    ns = len(SMALL)

    def body(*refs):
        gin_ref, gmid_ref, ggn_ref, gatt_ref, gl1_ref, gl2_ref = refs[:6]
        wmv_refs = refs[6:6 + 3 * ns]
        loss_ref = refs[6 + 3 * ns]
        out_refs = refs[7 + 3 * ns:]

        def dsum(ref, r=None):
            rows = slice(None) if r is None else slice(r, r + 1)
            acc = ref[0, rows, :]
            for d in range(1, NDEV):
                acc = acc + ref[d, rows, :]
            return acc

        s_lg = dsum(gl1_ref) + dsum(gl2_ref)
        loss_ref[...] = (0.5 / D) * jnp.sum(dsum(gmid_ref, 2), axis=-1, keepdims=True)
        eye = lax.broadcasted_iota(jnp.int32, (8, 128), 0) == lax.broadcasted_iota(jnp.int32, (8, 128), 1)
        dlg = jnp.sum(jnp.where(eye, s_lg, 0.0), axis=0, keepdims=True)
        w_f, w_b = wmv_refs[3 * SMALL.index("w_dec_f")][...], wmv_refs[3 * SMALL.index("w_dec_b")][...]
        s_q, s_k = dsum(gatt_ref, 0), dsum(gatt_ref, 1)
        grads = dict(
            b_ada=jnp.concatenate([dsum(gin_ref, 0), dsum(gin_ref, 1), dsum(gmid_ref, 0)], axis=1),
            g_pre=dsum(gin_ref, 2), g_post=dsum(gmid_ref, 1), gn_g=dsum(ggn_ref),
            qn_g=s_q[:, :DH] + s_q[:, DH:], kn_g=s_k[:, :DH] + s_k[:, DH:],
            w_dec_f=dlg[:, 0:HR] * _sigmoid(-w_f), w_dec_b=dlg[:, HR:2 * HR] * _sigmoid(-w_b))
        for i, nme in enumerate(SMALL):
            g = grads[nme]
            w_ref, m_ref, v_ref = wmv_refs[3 * i:3 * i + 3]
            g_ref, d_ref, nm_ref, nv_ref = out_refs[4 * i:4 * i + 4]
            g_ref[...] = g
            m2 = ADAM_B1 * m_ref[...] + (1.0 - ADAM_B1) * g
            v2 = ADAM_B2 * v_ref[...] + (1.0 - ADAM_B2) * jnp.square(g)
            m_hat = m2 / (1.0 - ADAM_B1 ** ADAM_STEP)
            v_hat = v2 / (1.0 - ADAM_B2 ** ADAM_STEP)
            d_ref[...] = -ADAM_LR * (m_hat / (jnp.sqrt(v_hat) + ADAM_EPS) + ADAM_WD * w_ref[...])
            nm_ref[...] = m2
            nv_ref[...] = v2

    out_shape = [jax.ShapeDtypeStruct((1, 1), F32)]
    for i in range(ns):
        out_shape += [jax.ShapeDtypeStruct(wmv[3 * i].shape, F32)] * 4
    return pl.pallas_call(body, name="small_update", out_shape=out_shape)(*gathered, *wmv)


def _adamw(parts, w, m, v, name):
    n, R, L = parts.shape
    tr = 256 if (R % 256 == 0 and R > 256) else R

    def body(p_ref, w_ref, m_ref, v_ref, g_ref, d_ref, nm_ref, nv_ref):
        g = p_ref[0].astype(F32)
        for k in range(1, n):
            g = g + p_ref[k].astype(F32)
        g_ref[...] = g
        m2 = ADAM_B1 * m_ref[...] + (1.0 - ADAM_B1) * g
        v2 = ADAM_B2 * v_ref[...] + (1.0 - ADAM_B2) * jnp.square(g)
        m_hat = m2 / (1.0 - ADAM_B1 ** ADAM_STEP)
        v_hat = v2 / (1.0 - ADAM_B2 ** ADAM_STEP)
        d_ref[...] = -ADAM_LR * (m_hat / (jnp.sqrt(v_hat) + ADAM_EPS) + ADAM_WD * w_ref[...])
        nm_ref[...] = m2
        nv_ref[...] = v2

    blk = pl.BlockSpec((tr, L), lambda i: (i, 0))
    o = jax.ShapeDtypeStruct((R, L), F32)
    return pl.pallas_call(
        body, name=name, out_shape=[o, o, o, o], grid=(R // tr,),
        in_specs=[pl.BlockSpec((n, tr, L), lambda i: (0, i, 0)), blk, blk, blk], out_specs=[blk, blk, blk, blk],
        compiler_params=_cp(("parallel",), VMEM_BIG),
    )(parts, w, m, v)


def _rope_tables(S):
    f = np.float32
    t = np.arange(S)
    row, col = (t // 64).astype(f), (t % 64).astype(f)
    half = DH // 2
    inv = np.power(f(ROPE_THETA), -np.arange(0, half, 2, dtype=f) / f(half)).astype(f)
    ar, ac = (row[:, None] * inv[None, :]).astype(f), (col[:, None] * inv[None, :]).astype(f)
    cos64 = np.concatenate([np.cos(ar), np.cos(ar), np.cos(ac), np.cos(ac)], axis=1).astype(f)
    sin64 = np.concatenate([-np.sin(ar), np.sin(ar), -np.sin(ac), np.sin(ac)], axis=1).astype(f)
    return jnp.asarray(np.tile(cos64, (1, 2))), jnp.asarray(np.tile(sin64, (1, 2)))


def _to_p_order(w_orig):
    return jnp.concatenate([w_orig[:, ORIG[n][0]:ORIG[n][1]] for n in P_ORDER], axis=1)


def _pad_lanes(v, n):
    return jnp.pad(v, ((0, 0), (0, n - v.shape[1])))


def kernel(x, c, w_ada, b_ada, g_pre, w_in, qn_g, kn_g, w_dec_f, w_dec_b, gn_g, w_pa, w_pr, w_out, g_post, loss_target, m_w_ada, m_b_ada, m_g_pre, m_w_in, m_qn_g, m_kn_g, m_w_dec_f, m_w_dec_b, m_gn_g, m_w_pa, m_w_pr, m_w_out, m_g_post, v_w_ada, v_b_ada, v_g_pre, v_w_in, v_qn_g, v_kn_g, v_w_dec_f, v_w_dec_b, v_gn_g, v_w_pa, v_w_pr, v_w_out, v_g_post):
    S = x.shape[1]
    me = 4 * lax.axis_index("x") + 2 * lax.axis_index("y") + lax.axis_index("c")
    xs, tgt = x[0], loss_target[0]
    ncol_ada = w_ada.shape[2]
    ncol_in = w_in.shape[2]

    b_ada_s = lax.dynamic_slice(b_ada, (0, me * ncol_ada), (1, ncol_ada))
    mod_all, c_act, (wg_in,) = _prologue(jnp.pad(c, ((0, 7), (0, 0))), w_ada[0], b_ada_s, [w_in[0].astype(BF16)])
    mod = lax.dynamic_index_in_dim(mod_all, me, axis=1, keepdims=False).reshape(3, D)
    w_p = _to_p_order(wg_in.transpose(1, 0, 2).reshape(D, NDEV * ncol_in))
    all_dev = tuple(range(NDEV))
    st_w, tok_w = _xchg_start([(w_pa[0].astype(BF16)[None], all_dev), (w_pr[0].astype(BF16)[None], all_dev),
                               (w_out[0].astype(BF16)[None], all_dev)], "wgather_start")

    cos, sin = _rope_tables(S)
    qg, kg = jnp.tile(qn_g, (1, 2)), jnp.tile(kn_g, (1, 2))

    p, h = _fwd_in(xs, mod, g_pre + tok_w[0:1, 0:1], w_p)
    qt, kh, kt, vh, vta, qr2, kr2 = _prep(p, cos, sin, qg, kg)
    o_att, o_t, lse = _attn_fwd(qt, kh, vta)
    dc, qdf, qdb, kdf, kdb, adec = _ret_tables(w_dec_f, w_dec_b)
    rf, rb = _ret_states(kr2, p, kdf, kdb, adec)
    yr = _ret_out(qr2, kr2, p, rf, rb, dc, qdf, qdb, gn_g)
    wg_pa, wg_pr, wg_out = _xchg_wait([st_w], st_w["lands"], [[0, 1, 2]], yr, "wgather_wait")
    w_pa_f = wg_pa.transpose(1, 0, 2).reshape(512, D)
    w_pr_f = wg_pr.transpose(1, 0, 2).reshape(512, D)
    w_out_f = wg_out.reshape(D, D)

    dout, do, dpm, dyr, mb, dub, yab, dab, drb_, sums_mid = _mid(xs, tgt, mod, g_post, o_att, p, yr, w_pa_f, w_pr_f, w_out_f)
    gw_out = _mm_tn(mb, dub, "gw_out")
    gw_pa = _mm_tn(yab, dab, "gw_pa")
    gw_pr = _mm_tn(yr, drb_, "gw_pr")
    gi_m = _mm_tn(h, dpm, "gw_in_mid")

    def shards(cols, nd):
        return cols.astype(BF16).reshape(D, nd, ncol_in).transpose(1, 0, 2)

    st_a, tok_a = _xchg_start([
        (gw_out.astype(BF16).reshape(NDEV, 128, D), all_dev),
        (gw_pa.astype(BF16).reshape(512, NDEV, 128).transpose(1, 0, 2), all_dev),
        (gw_pr.astype(BF16).reshape(512, NDEV, 128).transpose(1, 0, 2), all_dev),
        (shards(gi_m[:, 224:2048], 3), (5, 6, 7))], "xchg_start_a",
        lands=[None, None, None, jnp.zeros((NDEV, D, ncol_in), BF16)])
    (dqt, dkt, dvt), _ = _attn_bwd(qt, kh, kt, vh, do, o_t, lse + tok_a[0, 0], [])
    dpa, gs_att = _attn_prep_bwd(dqt, dkt, dvt, p, cos, sin, qg, kg)
    gi_a = _mm_tn(h, dpa, "gw_in_att")
    st_b, tok_b = _xchg_start([(shards(jnp.concatenate([gi_a, gi_m[:, 2048:2496]], axis=1), 2), (0, 1))], "xchg_start_b",
                              lands=[st_a["lands"][3]])
    (dpra, dk_i, dv_i, drf, drb, dgn, dlg1), _ = _ret_bwd_chunk(qr2, kr2, p, rf, rb, dc, qdf, qdb, gn_g + tok_b[0:1, 0:1], dyr, cos, sin, [])
    dkf, dkb, dvf, dvb, dlg2 = _ret_bwd_scan(kr2, p, rf, rb, drf, drb, kdf, kdb, adec)
    dprb = _ret_bwd_final(dk_i, dkf, dkb, dv_i, dvf, dvb, cos, sin)
    gi_ra = _mm_tn(h, dpra, "gw_in_reta")
    gi_rb = _mm_tn(h, dprb, "gw_in_retb")
    chip_c = _pair_reduce(shards(jnp.concatenate([gi_m[:, 2496:2560], gi_ra[:, :256], gi_rb[:, 512:768], gi_rb[:, :512],
                                                  gi_ra[:, 256:768], gi_m[:, :224]], axis=1), 3), (2, 3, 4), "pair_reduce_c")
    st_c, tok_c = _xchg_start([(chip_c, (2, 3, 4, "same core"))], "xchg_start_c", lands=[st_b["lands"][0]])
    (grad_x, sums_in), _ = _bwd_in(dpm, dpa, dpra, dprb, w_p, xs, dout, mod, g_pre + tok_c[0:1, 0:1], [])

    gathered = _small_allgather([sums_in, sums_mid, dgn, gs_att, dlg1, dlg2], "ag_small")
    given = dict(b_ada=(b_ada, m_b_ada, v_b_ada), g_pre=(g_pre, m_g_pre, v_g_pre), qn_g=(qn_g, m_qn_g, v_qn_g), kn_g=(kn_g, m_kn_g, v_kn_g),
                 w_dec_f=(w_dec_f, m_w_dec_f, v_w_dec_f), w_dec_b=(w_dec_b, m_w_dec_b, v_w_dec_b), gn_g=(gn_g, m_gn_g, v_gn_g),
                 g_post=(g_post, m_g_post, v_g_post))
    small = _small_update(gathered, [a for nme in SMALL for a in given[nme]])
    loss = small[0][0, 0]

    g_in_all, g_mid_all = gathered[0], gathered[1]
    dmod_all = lax.dynamic_slice(jnp.concatenate([g_in_all[:, 0, :], g_in_all[:, 1, :], g_mid_all[:, 0, :]], axis=1),
                                 (0, me * ncol_ada), (NDEV, ncol_ada))
    g_ada = _mm_tn(c_act, jnp.pad(dmod_all, ((0, 8), (0, 0))).astype(BF16), "gw_ada")

    ada = _adamw(g_ada[None], w_ada[0], m_w_ada[0], v_w_ada[0], "adamw_ada")
    rs_out, rs_pa, rs_pr, rs_in = _xchg_wait([st_a, st_b, st_c], list(st_a["lands"][:3]) + [st_c["lands"][0]],
                                             [[0, 1, 2, 3], [3], [3]], ada[1], "xchg_wait")
    res = dict(
        w_ada=ada,
        w_in=_adamw(rs_in, w_in[0], m_w_in[0], v_w_in[0], "adamw_in"),
        w_pa=_adamw(rs_pa, w_pa[0], m_w_pa[0], v_w_pa[0], "adamw_pa"),
        w_pr=_adamw(rs_pr, w_pr[0], m_w_pr[0], v_w_pr[0], "adamw_pr"),
        w_out=_adamw(rs_out, w_out[0], m_w_out[0], v_w_out[0], "adamw_out"),
    )
    names = ["w_ada", "b_ada", "g_pre", "w_in", "qn_g", "kn_g", "w_dec_f", "w_dec_b", "gn_g", "w_pa", "w_pr", "w_out", "g_post"]
    outs = [[], [], [], []]
    for nme in names:
        for q in range(4):
            if nme in res:
                outs[q].append(res[nme][q][None])
            else:
                outs[q].append(small[1 + 4 * SMALL.index(nme) + q])
    return (loss, grad_x[None], *outs[0], *outs[1], *outs[2], *outs[3])
```

```python
import jax
import jax.numpy as jnp
import numpy as np
from jax import lax
from jax.experimental import pallas as pl
from jax.experimental.pallas import tpu as pltpu

F32, BF16 = jnp.float32, jnp.bfloat16
D = 1024
DH = 64
DHA = 80
DV = 128
LOG2E = 1.4426950408889634
LN2 = 0.6931471805599453
HR = 4
CH = 128
EPS = 1e-6
ROPE_THETA = 10000.0
NDEV = 8
O_GL, O_ZA, O_QA, O_KA, O_VA, O_QR, O_ZR, O_VR, O_KR, P_W = 0, 2048, 2560, 3072, 3200, 3328, 3584, 4096, 4608, 4864
ORIG = dict(qa=(0, 512), ka=(512, 640), va=(640, 768), za=(768, 1280), qr=(1280, 1536), kr=(1536, 1792),
            vr=(1792, 2304), zr=(2304, 2816), gl=(2816, 4864))
P_ORDER = ("gl", "za", "qa", "ka", "va", "qr", "zr", "vr", "kr")
ADAM_LR, ADAM_B1, ADAM_B2, ADAM_EPS, ADAM_WD, ADAM_STEP = 0.001, 0.9, 0.999, 1e-08, 0.01, 10
VMEM_BIG = 56 * 1024 * 1024
MESH = pl.DeviceIdType.MESH

NT = (((1,), (1,)), ((), ()))
TN = (((0,), (0,)), ((), ()))


def _dot(a, b, dims=None):
    if dims is None:
        return jnp.dot(a, b, preferred_element_type=F32)
    return lax.dot_general(a, b, dims, preferred_element_type=F32)


def _cp(sem=None, vmem=None):
    kw = {}
    if sem is not None:
        kw["dimension_semantics"] = sem
    if vmem is not None:
        kw["vmem_limit_bytes"] = vmem
    return pltpu.CompilerParams(**kw)


def _sigmoid(z):
    return 1.0 / (1.0 + jnp.exp(-z))


def _sum11(m):
    return jnp.sum(jnp.sum(m, axis=-1, keepdims=True), axis=0, keepdims=True)


def _full(shape):
    n = len(shape)
    return pl.BlockSpec(shape, lambda *_: (0,) * n)


def _my_pos():
    return lax.axis_index("x"), lax.axis_index("y"), lax.axis_index("c")


def _peer(k, x, y, c):
    return ((1 - x) if k & 4 else x, (1 - y) if k & 2 else y, (1 - c) if k & 1 else c)


def _small_allgather(vs, name):
    n = len(vs)

    def body(*refs):
        v_refs, out_refs = refs[:n], refs[n:2 * n]
        send_sems, recv_sems = refs[2 * n:]
        x, y, c = _my_pos()
        me = 4 * x + 2 * y + c
        cps = []
        for a in range(n):
            out_refs[a][me] = v_refs[a][...]
            for k in range(1, NDEV):
                cp = pltpu.make_async_remote_copy(src_ref=v_refs[a], dst_ref=out_refs[a].at[me], send_sem=send_sems.at[a, k - 1],
                                                  recv_sem=recv_sems.at[a, k - 1], device_id=_peer(k, x, y, c), device_id_type=MESH)
                cp.start()
                cps.append(cp)
        for cp in cps:
            cp.wait()

    vm = pl.BlockSpec(memory_space=pltpu.VMEM)
    return pl.pallas_call(
        body, name=name, out_shape=[jax.ShapeDtypeStruct((NDEV,) + v.shape, v.dtype) for v in vs],
        in_specs=[vm] * n, out_specs=[vm] * n,
        scratch_shapes=[pltpu.SemaphoreType.DMA((n, NDEV - 1)), pltpu.SemaphoreType.DMA((n, NDEV - 1))],
    )(*vs)


def _prologue(c8, w_ada_s, b_ada_s, arrs):
    n = len(arrs)
    ncol = w_ada_s.shape[1]

    def body(*refs):
        c_ref, wa_ref, ba_ref = refs[:3]
        ins = refs[3:3 + n]
        mod_ref, cact_ref = refs[3 + n:5 + n]
        outs = refs[5 + n:5 + 2 * n]
        call_ref, send_sems, recv_sems, local_sems, s_send, s_recv = refs[5 + 2 * n:]
        x, y, c = _my_pos()
        me, sibling = (x, y, c), (x, y, 1 - c)
        chips = [(1 - x, y), (x, 1 - y), (1 - x, 1 - y)]
        me_i = 4 * x + 2 * y + c

        def small_gather(src_ref, dst_ref, row):
            cps = []
            for k in range(1, NDEV):
                cp = pltpu.make_async_remote_copy(src_ref=src_ref, dst_ref=dst_ref.at[me_i], send_sem=s_send.at[row, k - 1],
                                                  recv_sem=s_recv.at[row, k - 1], device_id=_peer(k, x, y, c), device_id_type=MESH)
                cp.start()
                cps.append(cp)
            return cps

        def blk(a, px, py, pc):
            return outs[a].at[4 * px + 2 * py + pc]

        def copy(a, k, block, to, src=None):
            return pltpu.make_async_remote_copy(src_ref=blk(a, *block) if src is None else src, dst_ref=blk(a, *block),
                                                send_sem=send_sems.at[a, k], recv_sem=recv_sems.at[a, k], device_id=to, device_id_type=MESH)

        call_ref[me_i] = c_ref[...]
        for cp in small_gather(c_ref, call_ref, 0):
            cp.wait()

        local, sent = [], []
        for a in range(n):
            mine = pltpu.make_async_copy(ins[a], blk(a, *me), local_sems.at[a])
            mine.start()
            local.append(mine)
            first = [copy(a, 0, me, sibling, src=ins[a])] + [copy(a, 1 + j, me, (*chip, c), src=ins[a]) for j, chip in enumerate(chips)]
            for cp in first:
                cp.start()
            sent += first

        cv = call_ref[:, 0, :]
        ca = jnp.concatenate([cv * _sigmoid(cv), jnp.zeros_like(cv)], axis=0).astype(BF16)
        cact_ref[...] = ca
        mod_ref[me_i] = (_dot(ca, wa_ref[...].astype(BF16)) + ba_ref[...])[:8]
        mod_copies = small_gather(mod_ref.at[me_i], mod_ref, 1)

        for j, chip in enumerate(chips):
            for a in range(n):
                copy(a, 1 + j, (*chip, c), me).wait_recv()
                cp = copy(a, 4 + j, (*chip, c), sibling)
                cp.start()
                sent.append(cp)
        for a in range(n):
            copy(a, 0, sibling, me).wait_recv()
            for j, chip in enumerate(chips):
                copy(a, 4 + j, (*chip, 1 - c), me).wait_recv()
        for cp in sent:
            cp.wait_send()
        for cp in local + mod_copies:
            cp.wait()

    vm, hbm = pl.BlockSpec(memory_space=pltpu.VMEM), pl.BlockSpec(memory_space=pl.ANY)
    res = pl.pallas_call(
        body, name="prologue",
        out_shape=[jax.ShapeDtypeStruct((NDEV, 8, ncol), F32), jax.ShapeDtypeStruct((16, D), BF16)]
        + [jax.ShapeDtypeStruct((NDEV,) + a.shape, a.dtype) for a in arrs],
        in_specs=[vm, vm, vm] + [hbm] * n, out_specs=[vm, vm] + [hbm] * n,
        scratch_shapes=[pltpu.VMEM((NDEV, 8, D), F32), pltpu.SemaphoreType.DMA((n, NDEV - 1)), pltpu.SemaphoreType.DMA((n, NDEV - 1)),
                        pltpu.SemaphoreType.DMA((n,)), pltpu.SemaphoreType.DMA((2, NDEV - 1)), pltpu.SemaphoreType.DMA((2, NDEV - 1))],
    )(c8, w_ada_s, b_ada_s, *arrs)
    return res[0], res[1], res[2:]


def _in_set(idx, dests):
    p = idx == dests[0]
    for d in dests[1:]:
        p = jnp.logical_or(p, idx == d)
    return p


def _host_call(body, xs, *, name, grid, in_specs, out_specs, out_shape, scratch_shapes, operands, compiler_params):
    nx, nin, nout, nscr = len(xs), len(operands), len(out_shape), len(scratch_shapes)
    if nx == 0:
        res = pl.pallas_call(body, name=name, grid=grid, in_specs=in_specs, out_specs=out_specs, out_shape=out_shape,
                             scratch_shapes=scratch_shapes, compiler_params=compiler_params)(*operands)
        return res, []
    ops, specs, aliases = list(operands), list(in_specs), {}
    oshape, ospecs = list(out_shape), list(out_specs)
    any_spec = pl.BlockSpec(memory_space=pl.ANY)
    for a, (send, dests, recv) in enumerate(xs):
        ops.append(send)
        specs.append(any_spec)
        if recv is not None:
            aliases[len(ops)] = nout + a
            ops.append(recv)
            specs.append(any_spec)
            oshape.append(jax.ShapeDtypeStruct(recv.shape, recv.dtype))
        else:
            oshape.append(jax.ShapeDtypeStruct((NDEV,) + send.shape[1:], send.dtype))
        ospecs.append(any_spec)
    ntot_in = len(ops)

    def wrapped(*refs):
        host_in = refs[:nin]
        sends, pos = [], nin
        for (_, _, recv) in xs:
            sends.append(refs[pos])
            pos += 1 if recv is None else 2
        host_out = refs[ntot_in:ntot_in + nout]
        recvs = refs[ntot_in + nout:ntot_in + nout + nx]
        host_scr = refs[ntot_in + nout + nx:ntot_in + nout + nx + nscr]
        send_sems, recv_sems, local_sems = refs[ntot_in + nout + nx + nscr:]
        first = pl.program_id(0) == 0
        last = pl.program_id(0) == grid[0] - 1
        for ax in range(1, len(grid)):
            first = jnp.logical_and(first, pl.program_id(ax) == 0)
            last = jnp.logical_and(last, pl.program_id(ax) == grid[ax] - 1)
        x, y, c = _my_pos()
        me = 4 * x + 2 * y + c

        def each(fn_remote, fn_local):
            for a, (_, dests, _) in enumerate(xs):
                lo, nd = dests[0], len(dests)
                for k in range(1, NDEV):
                    px, py, pc = _peer(k, x, y, c)
                    pidx = 4 * px + 2 * py + pc
                    cp = pltpu.make_async_remote_copy(src_ref=sends[a].at[jnp.clip(pidx - lo, 0, nd - 1)], dst_ref=recvs[a].at[me],
                                                      send_sem=send_sems.at[a, k - 1], recv_sem=recv_sems.at[a, k - 1],
                                                      device_id=(px, py, pc), device_id_type=MESH)
                    fn_remote(cp, _in_set(pidx, dests), _in_set(me, dests))
                lc = pltpu.make_async_copy(sends[a].at[jnp.clip(me - lo, 0, nd - 1)], recvs[a].at[me], local_sems.at[a])
                fn_local(lc, _in_set(me, dests))

        def start_remote(cp, to_dest, _):
            pl.when(jnp.logical_and(first, to_dest))(cp.start)

        def start_local(lc, i_am_dest):
            pl.when(jnp.logical_and(first, i_am_dest))(lc.start)

        def wait_remote(cp, to_dest, i_am_dest):
            pl.when(jnp.logical_and(last, to_dest))(cp.wait_send)
            pl.when(jnp.logical_and(last, i_am_dest))(cp.wait_recv)

        def wait_local(lc, i_am_dest):
            pl.when(jnp.logical_and(last, i_am_dest))(lc.wait)

        each(start_remote, start_local)
        body(*host_in, *host_out, *host_scr)
        each(wait_remote, wait_local)

    res = pl.pallas_call(
        wrapped, name=name, grid=grid, in_specs=specs, out_specs=ospecs, out_shape=oshape, input_output_aliases=aliases,
        scratch_shapes=list(scratch_shapes) + [pltpu.SemaphoreType.DMA((nx, NDEV - 1)), pltpu.SemaphoreType.DMA((nx, NDEV - 1)),
                                               pltpu.SemaphoreType.DMA((nx,))],
        compiler_params=compiler_params,
    )(*ops)
    return res[:nout], res[nout:]


_HBM = pl.BlockSpec(memory_space=pltpu.HBM)
_SEM = pl.BlockSpec(memory_space=pltpu.SEMAPHORE)


def _pair_reduce(send, dests, name):
    nd = send.shape[0]

    def body(s_ref, o_ref, land, ssem, rsem):
        x, y, c = _my_pos()
        cps = []
        for i in range(nd):
            cp = pltpu.make_async_remote_copy(src_ref=s_ref.at[i], dst_ref=land.at[i], send_sem=ssem.at[i], recv_sem=rsem.at[i],
                                              device_id=(x, y, 1 - c), device_id_type=MESH)
            pl.when(c != (dests[i] & 1))(cp.start)
            cps.append(cp)
        for i in range(nd):
            mine = c == (dests[i] & 1)

            @pl.when(mine)
            def _():
                cps[i].wait_recv()
                o_ref[i] = (s_ref[i].astype(F32) + land[i].astype(F32)).astype(BF16)

            pl.when(jnp.logical_not(mine))(cps[i].wait_send)

    vm = pl.BlockSpec(memory_space=pltpu.VMEM)
    return pl.pallas_call(
        body, name=name, out_shape=jax.ShapeDtypeStruct(send.shape, send.dtype), in_specs=[vm], out_specs=vm,
        scratch_shapes=[pltpu.VMEM(send.shape, send.dtype), pltpu.SemaphoreType.DMA((nd,)), pltpu.SemaphoreType.DMA((nd,))],
        compiler_params=_cp(None, VMEM_BIG),
    )(send)


def _xchg_copies(xs_dests, sends, lands, ssem, rsem, lsem):
    x, y, c = _my_pos()
    me = 4 * x + 2 * y + c
    remote, local = [], []
    for a, dests in enumerate(xs_dests):
        same_core = dests[-1] == "same core"
        dests = dests[:-1] if same_core else dests
        lo, nd = dests[0], sends[a].shape[0]
        for k in range(1, NDEV):
            if same_core and k & 1:
                continue
            px, py, pc = _peer(k, x, y, c)
            pidx = 4 * px + 2 * py + pc
            cp = pltpu.make_async_remote_copy(src_ref=sends[a].at[jnp.clip(pidx - lo, 0, nd - 1)], dst_ref=lands[a].at[me],
                                              send_sem=ssem.at[a * (NDEV - 1) + k - 1], recv_sem=rsem.at[a * (NDEV - 1) + k - 1],
                                              device_id=(px, py, pc), device_id_type=MESH)
            remote.append((cp, _in_set(pidx, dests), _in_set(me, dests)))
        lc = pltpu.make_async_copy(sends[a].at[jnp.clip(me - lo, 0, nd - 1)], lands[a].at[me], lsem.at[a])
        local.append((lc, _in_set(me, dests)))
    return remote, local


def _xchg_start(xs, name, lands=None):
    n = len(xs)
    dests = [d for _, d in xs]
    sends = [pltpu.with_memory_space_constraint(s, pltpu.HBM) for s, _ in xs]
    lands = [None] * n if lands is None else lands
    lands = [pltpu.with_memory_space_constraint(lax.empty((NDEV,) + s.shape[1:], s.dtype) if l is None else l, pltpu.HBM)
             for (s, _), l in zip(xs, lands)]

    def body(*refs):
        send_refs, land_refs = refs[:n], refs[n:2 * n]
        ssem, rsem, lsem = refs[2 * n:2 * n + 3]
        token = refs[-1]
        remote, local = _xchg_copies(dests, send_refs, land_refs, ssem, rsem, lsem)
        for cp, to_dest, _ in remote:
            pl.when(to_dest)(cp.start)
        for lc, i_am_dest in local:
            pl.when(i_am_dest)(lc.start)
        token[...] = jnp.zeros_like(token)

    res = pl.pallas_call(
        body, name=name,
        out_shape=[pltpu.SemaphoreType.DMA((n * (NDEV - 1),)), pltpu.SemaphoreType.DMA((n * (NDEV - 1),)), pltpu.SemaphoreType.DMA((n,))]
        + [pltpu.HBM(a.shape, a.dtype) for a in list(sends) + list(lands)] + [jax.ShapeDtypeStruct((8, 128), F32)],
        in_specs=[_HBM] * (2 * n), out_specs=[_SEM, _SEM, _SEM] + [_HBM] * (2 * n) + [pl.BlockSpec(memory_space=pltpu.VMEM)],
        input_output_aliases={i: 3 + i for i in range(2 * n)},
        compiler_params=pltpu.CompilerParams(has_side_effects=pltpu.SideEffectType.DATAFLOW_SIDE_EFFECTING),
    )(*sends, *lands)
    return dict(sems=res[0:3], sends=res[3:3 + n], lands=res[3 + n:3 + 2 * n], dests=dests), res[-1]


def _xchg_wait(states, lands, land_of, after, name):
    flat = []
    for st in states:
        flat += list(st["sends"]) + list(st["sems"])
    nl = len(lands)

    def body(*refs):
        land_refs = refs[:nl]
        pos = nl
        for s, st in enumerate(states):
            n = len(st["dests"])
            send_refs = refs[pos:pos + n]
            ssem, rsem, lsem = refs[pos + n:pos + n + 3]
            pos += n + 3
            remote, local = _xchg_copies(st["dests"], send_refs, [land_refs[i] for i in land_of[s]], ssem, rsem, lsem)
            for cp, to_dest, i_am_dest in remote:
                pl.when(to_dest)(cp.wait_send)
                pl.when(i_am_dest)(cp.wait_recv)
            for lc, i_am_dest in local:
                pl.when(i_am_dest)(lc.wait)

    in_specs = [_HBM] * nl
    for st in states:
        in_specs += [_HBM] * len(st["dests"]) + [_SEM, _SEM, _SEM]
    return pl.pallas_call(
        body, name=name, out_shape=[pltpu.HBM(a.shape, a.dtype) for a in lands],
        in_specs=in_specs + [pl.BlockSpec(memory_space=pl.ANY)], out_specs=[_HBM] * nl,
        input_output_aliases={i: i for i in range(nl)},
        compiler_params=pltpu.CompilerParams(has_side_effects=pltpu.SideEffectType.DATAFLOW_SIDE_EFFECTING),
    )(*lands, *flat, after)


def _mm_tn(a, b, name):
    S, M = a.shape
    N = b.shape[1]
    tk = min(2048, S)
    tn = N if N <= 768 else (640 if N % 640 == 0 else 512)
    nk = S // tk

    def body(a_ref, b_ref, o_ref):
        @pl.when(pl.program_id(1) == 0)
        def _():
            o_ref[...] = jnp.zeros_like(o_ref)
        o_ref[...] += _dot(a_ref[...], b_ref[...], TN)

    return pl.pallas_call(
        body, name=name, out_shape=jax.ShapeDtypeStruct((M, N), F32), grid=(N // tn, nk),
        in_specs=[pl.BlockSpec((tk, M), lambda j, k: (k, 0)), pl.BlockSpec((tk, tn), lambda j, k: (k, j))],
        out_specs=pl.BlockSpec((M, tn), lambda j, k: (0, j)),
        compiler_params=_cp(("parallel", "arbitrary"), VMEM_BIG),
    )(a, b)


def _fwd_in(x, mod, g_pre, w_p):
    S = x.shape[0]
    tm = min(512, S)

    def body(x_ref, mod_ref, g_ref, w_ref, p_ref, h_ref):
        xv = x_ref[...]
        r = lax.rsqrt(jnp.mean(xv * xv, axis=-1, keepdims=True) + EPS)
        h = (((xv * r) * g_ref[...]) * (1.0 + mod_ref[1:2, :]) + mod_ref[0:1, :]).astype(BF16)
        h_ref[...] = h
        p_ref[...] = _dot(h, w_ref[...]).astype(BF16)

    return pl.pallas_call(
        body, name="fwd_in", out_shape=[jax.ShapeDtypeStruct((S, P_W), BF16), jax.ShapeDtypeStruct((S, D), BF16)],
        grid=(S // tm,),
        in_specs=[pl.BlockSpec((tm, D), lambda i: (i, 0)), _full((3, D)), _full((1, D)), _full((D, P_W))],
        out_specs=[pl.BlockSpec((tm, P_W), lambda i: (i, 0)), pl.BlockSpec((tm, D), lambda i: (i, 0))],
        compiler_params=_cp(("parallel",), VMEM_BIG),
    )(x, mod, g_pre, w_p)


def _swap16(v):
    lane = lax.broadcasted_iota(jnp.int32, v.shape, 1)
    return jnp.where((lane % 32) < 16, pltpu.roll(v, 112, 1), pltpu.roll(v, 16, 1))


def _rope(v, cos, sin):
    return v * cos + _swap16(v) * sin


def _rope_t(v, cos, sin):
    return v * cos - _swap16(v) * sin


def _head_mean(v):
    lo = lax.broadcasted_iota(jnp.int32, v.shape, 1) < 64
    m0 = jnp.sum(jnp.where(lo, v, 0.0), axis=-1, keepdims=True)
    m1 = jnp.sum(jnp.where(lo, 0.0, v), axis=-1, keepdims=True)
    return jnp.where(lo, m0, m1) * (1.0 / 64.0)


def _prep(p, cos, sin, qg, kg):
    S = p.shape[0]
    tm = min(512, S)

    def body(qa_ref, kv_ref, qr_ref, kr_ref, cos_ref, sin_ref, qg_ref, kg_ref, qt_ref, kh_ref, kt_ref, vh_ref, vta_ref, qr2_ref, kr2_ref):
        cos_v, sin_v = cos_ref[...], sin_ref[...]
        for g in range(4):
            xv = qa_ref[:, 128 * g:128 * g + 128].astype(F32)
            r = lax.rsqrt(_head_mean(xv * xv) + EPS)
            yt = (_rope((xv * r) * qg_ref[...], cos_v, sin_v) * (0.125 * LOG2E)).T
            qt_ref[2 * g] = yt[:DH].astype(BF16)
            qt_ref[2 * g + 1] = yt[DH:].astype(BF16)
        xv = kv_ref[:, :128].astype(F32)
        r = lax.rsqrt(_head_mean(xv * xv) + EPS)
        yv = _rope((xv * r) * kg_ref[...], cos_v, sin_v)
        kh_ref[0] = yv[:, :64].astype(BF16)
        kh_ref[1] = yv[:, 64:].astype(BF16)
        yt = yv.T
        kt_ref[0] = yt[:DH].astype(BF16)
        kt_ref[1] = yt[DH:].astype(BF16)
        vv = kv_ref[:, 128:].astype(F32)
        vh_ref[0] = vv[:, :64].astype(BF16)
        vh_ref[1] = vv[:, 64:].astype(BF16)
        vt = vv.T
        tail = (lax.broadcasted_iota(jnp.int32, (DHA - DH, tm), 0) == 0).astype(BF16)
        for kvh in range(2):
            vta_ref[kvh, 0:DH, :] = vt[DH * kvh:DH * kvh + DH].astype(BF16)
            vta_ref[kvh, DH:DHA, :] = tail
        for g in range(2):
            sl = slice(128 * g, 128 * g + 128)
            qr2_ref[:, sl] = _rope(qr_ref[:, sl].astype(F32), cos_v, sin_v)
            kr2_ref[:, sl] = _rope(kr_ref[:, sl].astype(F32), cos_v, sin_v) * 0.125

    hm = lambda n: pl.BlockSpec((n, tm, DH), lambda i: (0, i, 0))
    ht = lambda n, r: pl.BlockSpec((n, r, tm), lambda i: (0, 0, i))
    return pl.pallas_call(
        body, name="prep",
        out_shape=[jax.ShapeDtypeStruct((8, DH, S), BF16), jax.ShapeDtypeStruct((2, S, DH), BF16), jax.ShapeDtypeStruct((2, DH, S), BF16),
                   jax.ShapeDtypeStruct((2, S, DH), BF16), jax.ShapeDtypeStruct((2, DHA, S), BF16),
                   jax.ShapeDtypeStruct((S, 256), F32), jax.ShapeDtypeStruct((S, 256), F32)],
        grid=(S // tm,),
        in_specs=[pl.BlockSpec((tm, 512), lambda i: (i, O_QA // 512)), pl.BlockSpec((tm, 256), lambda i: (i, O_KA // 256)),
                  pl.BlockSpec((tm, 256), lambda i: (i, O_QR // 256)), pl.BlockSpec((tm, 256), lambda i: (i, O_KR // 256)),
                  pl.BlockSpec((tm, 128), lambda i: (i, 0)), pl.BlockSpec((tm, 128), lambda i: (i, 0)), _full((1, 128)), _full((1, 128))],
        out_specs=[ht(8, DH), hm(2), ht(2, DH), hm(2), ht(2, DHA), pl.BlockSpec((tm, 256), lambda i: (i, 0)), pl.BlockSpec((tm, 256), lambda i: (i, 0))],
        compiler_params=_cp(("parallel",)),
    )(p, p, p, p, cos, sin, qg, kg)


def _attn_fwd(qt, kh, vta):
    S = qt.shape[2]
    tq, tk = min(1024, S), min(512, S)
    nj = S // tk

    def body(q_ref, k_ref, v_ref, o_ref, ot_ref, lse_ref, m_s, acc_s):
        j = pl.program_id(1)

        @pl.when(j == 0)
        def _():
            m_s[...] = jnp.full_like(m_s, -jnp.inf)
            acc_s[...] = jnp.zeros_like(acc_s)

        m_all = m_s[...]
        st = {0: _dot(k_ref[0], q_ref[0])}
        m_new, acc_new = [], []
        for h in range(8):
            if h + 1 < 8:
                st[h + 1] = _dot(k_ref[(h + 1) // 4], q_ref[h + 1])
            m_old = m_all[h:h + 1, :]
            mn = jnp.maximum(m_old, jnp.max(st[h], axis=0, keepdims=True))
            pt = jnp.exp2(st[h] - mn).astype(BF16)
            acc_new.append(jnp.exp2(m_old - mn) * acc_s[h] + _dot(v_ref[h // 4], pt))
            m_new.append(mn)
            del st[h]
        for h in range(8):
            acc_s[h] = acc_new[h]
            m_s[h:h + 1, :] = m_new[h]

        @pl.when(j == nj - 1)
        def _():
            for h in range(8):
                ot = acc_s[h, 0:DH, :] / acc_s[h, DH:DH + 1, :]
                ot_ref[h] = ot
                o_ref[:, DH * h:DH * h + DH] = ot.T
                lse_ref[h // 4, h % 4:h % 4 + 1, :] = m_s[h:h + 1, :] + jnp.log2(acc_s[h, DH:DH + 1, :])

    return pl.pallas_call(
        body, name="attn_fwd",
        out_shape=[jax.ShapeDtypeStruct((S, 512), F32), jax.ShapeDtypeStruct((8, DH, S), F32), jax.ShapeDtypeStruct((2, 4, S), F32)],
        grid=(S // tq, nj),
        in_specs=[pl.BlockSpec((8, DH, tq), lambda i, j: (0, 0, i)), pl.BlockSpec((2, tk, DH), lambda i, j: (0, j, 0)),
                  pl.BlockSpec((2, DHA, tk), lambda i, j: (0, 0, j))],
        out_specs=[pl.BlockSpec((tq, 512), lambda i, j: (i, 0)), pl.BlockSpec((8, DH, tq), lambda i, j: (0, 0, i)),
                   pl.BlockSpec((2, 4, tq), lambda i, j: (0, 0, i))],
        scratch_shapes=[pltpu.VMEM((8, tq), F32), pltpu.VMEM((8, DHA, tq), F32)],
        compiler_params=_cp(("parallel", "arbitrary"), VMEM_BIG),
    )(qt, kh, vta)


def _ret_tables(wf, wb):
    C = CH

    def body(wf_ref, wb_ref, dc_ref, qdf_ref, qdb_ref, kdf_ref, kdb_ref, a_ref):
        def logsig(w):
            z = jnp.exp(-jnp.abs(w))
            u = 1.0 + z
            l1p = jnp.where(u == 1.0, z, jnp.log(u) * (z / jnp.where(u == 1.0, 1.0, u - 1.0)))
            return jnp.minimum(w, 0.0) - l1p

        lgf, lgb = logsig(wf_ref[...]), logsig(wb_ref[...])
        lane4 = lax.broadcasted_iota(jnp.int32, (1, 4), 1)

        def pick(lg, h):
            return jnp.sum(jnp.where(lane4 == h, lg, 0.0), axis=-1, keepdims=True)

        ii = lax.broadcasted_iota(jnp.int32, (C, C), 0).astype(F32)
        jj = lax.broadcasted_iota(jnp.int32, (C, C), 1).astype(F32)
        dif = ii - jj
        hd = lax.broadcasted_iota(jnp.int32, (C, 256), 1) // DH
        lf_l = jnp.zeros((C, 256), F32)
        lb_l = jnp.zeros((C, 256), F32)
        for h in range(HR):
            lf, lb = pick(lgf, h), pick(lgb, h)
            dc_ref[h] = jnp.where(dif >= 0, jnp.exp(lf * jnp.maximum(dif, 0.0)), jnp.exp(lb * jnp.maximum(-dif, 0.0)))
            lf_l = jnp.where(hd == h, lf, lf_l)
            lb_l = jnp.where(hd == h, lb, lb_l)
            a_ref[h:h + 1, :] = jnp.broadcast_to(jnp.exp(lf * C), (1, 128))
            a_ref[HR + h:HR + h + 1, :] = jnp.broadcast_to(jnp.exp(lb * C), (1, 128))
        ri = lax.broadcasted_iota(jnp.int32, (C, 256), 0).astype(F32)
        qdf_ref[...] = jnp.exp(lf_l * (ri + 1.0))
        qdb_ref[...] = jnp.exp(lb_l * (C - ri))
        kdf_ref[...] = jnp.exp(lf_l * (C - 1.0 - ri))
        kdb_ref[...] = jnp.exp(lb_l * ri)

    t = jax.ShapeDtypeStruct((C, 256), F32)
    return pl.pallas_call(body, name="ret_tables",
                          out_shape=[jax.ShapeDtypeStruct((HR, C, C), F32), t, t, t, t, jax.ShapeDtypeStruct((8, 128), F32)])(wf, wb)


def _ret_states(kr2, p, kdf, kdb, adec):
    S = kr2.shape[0]
    C, N = CH, S // CH
    G = _scan_group(N)
    NG = N // G

    def body(kf_ref, vf_ref, kb_ref, vb_ref, kdf_ref, kdb_ref, a_ref, rf_ref, rb_ref, sf, sb):
        @pl.when(pl.program_id(0) == 0)
        def _():
            sf[...] = jnp.zeros_like(sf)
            sb[...] = jnp.zeros_like(sb)

        kvf, kvb = [], []
        for u in range(G):
            rows = slice(C * u, C * u + C)
            kdfw = (kf_ref[rows, :] * kdf_ref[...]).astype(BF16)
            kdbw = (kb_ref[rows, :] * kdb_ref[...]).astype(BF16)
            vf, vb = vf_ref[rows, :].astype(BF16), vb_ref[rows, :].astype(BF16)
            kvf.append([_dot(kdfw[:, _ks(h)], vf[:, _vs(h)], TN) for h in range(HR)])
            kvb.append([_dot(kdbw[:, _ks(h)], vb[:, _vs(h)], TN) for h in range(HR)])
        for u in range(G):
            rf_ref[u] = sf[...]
            for h in range(HR):
                sf[h] = a_ref[h:h + 1, :] * sf[h] + kvf[u][h]
        for u in reversed(range(G)):
            rb_ref[u] = sb[...]
            for h in range(HR):
                sb[h] = a_ref[HR + h:HR + h + 1, :] * sb[h] + kvb[u][h]

    st = jax.ShapeDtypeStruct((N, HR, DH, DV), F32)
    return pl.pallas_call(
        body, name="ret_states", out_shape=[st, st], grid=(NG,),
        in_specs=[pl.BlockSpec((G * C, 256), lambda t: (t, 0)), pl.BlockSpec((G * C, 512), lambda t: (t, O_VR // 512)),
                  pl.BlockSpec((G * C, 256), lambda t: (NG - 1 - t, 0)), pl.BlockSpec((G * C, 512), lambda t: (NG - 1 - t, O_VR // 512)),
                  _full((C, 256)), _full((C, 256)), _full((8, 128))],
        out_specs=[pl.BlockSpec((G, HR, DH, DV), lambda t: (t, 0, 0, 0)), pl.BlockSpec((G, HR, DH, DV), lambda t: (NG - 1 - t, 0, 0, 0))],
        scratch_shapes=[pltpu.VMEM((HR, DH, DV), F32), pltpu.VMEM((HR, DH, DV), F32)],
        compiler_params=_cp(("arbitrary",)),
    )(kr2, p, kr2, p, kdf, kdb, adec)


def _scan_group(n):
    return 4 if n % 4 == 0 else (2 if n % 2 == 0 else 1)


def _ks(h):
    return slice(DH * h, DH * h + DH)


def _vs(h):
    return slice(DV * h, DV * h + DV)


def _ret_heads_fwd(qb, kb, vb, qfw, qbw, dc_ref, rf_ref, rb_ref):
    hs = range(HR)
    s = [_dot(qb[:, _ks(h)], kb[:, _ks(h)], NT) for h in hs]
    inter = [_dot(qfw[:, _ks(h)], rf_ref[0, h].astype(BF16)) + _dot(qbw[:, _ks(h)], rb_ref[0, h].astype(BF16)) for h in hs]
    sd = [s[h] * dc_ref[h] for h in hs]
    o = [_dot(sd[h].astype(BF16), vb[:, _vs(h)]) + inter[h] for h in hs]
    return sd, o


def _ret_out(qr2, kr2, p, rf, rb, dc, qdf, qdb, gn):
    S = qr2.shape[0]
    C, N = CH, S // CH

    def body(q_ref, k_ref, v_ref, z_ref, rf_ref, rb_ref, dc_ref, qdf_ref, qdb_ref, gn_ref, yr_ref):
        qv = q_ref[...]
        qb, kb, vb = qv.astype(BF16), k_ref[...].astype(BF16), v_ref[...].astype(BF16)
        qfw, qbw = (qv * qdf_ref[...]).astype(BF16), (qv * qdb_ref[...]).astype(BF16)
        _, o = _ret_heads_fwd(qb, kb, vb, qfw, qbw, dc_ref, rf_ref, rb_ref)
        for h in range(HR):
            vs = _vs(h)
            mu = jnp.mean(o[h], axis=-1, keepdims=True)
            var = jnp.mean(jnp.square(o[h] - mu), axis=-1, keepdims=True)
            on = (o[h] - mu) * lax.rsqrt(var + EPS)
            z = z_ref[:, vs].astype(F32)
            yr_ref[:, vs] = ((on * gn_ref[:, vs]) * (z * _sigmoid(z))).astype(BF16)

    return pl.pallas_call(
        body, name="ret_out", out_shape=jax.ShapeDtypeStruct((S, 512), BF16), grid=(N,),
        in_specs=[pl.BlockSpec((C, 256), lambda t: (t, 0)), pl.BlockSpec((C, 256), lambda t: (t, 0)),
                  pl.BlockSpec((C, 512), lambda t: (t, O_VR // 512)), pl.BlockSpec((C, 512), lambda t: (t, O_ZR // 512)),
                  pl.BlockSpec((1, HR, DH, DV), lambda t: (t, 0, 0, 0)), pl.BlockSpec((1, HR, DH, DV), lambda t: (t, 0, 0, 0)),
                  _full((HR, C, C)), _full((C, 256)), _full((C, 256)), _full((1, 512))],
        out_specs=pl.BlockSpec((C, 512), lambda t: (t, 0)),
        compiler_params=_cp(("parallel",)),
    )(qr2, kr2, p, p, rf, rb, dc, qdf, qdb, gn)


def _mid(x, tgt, mod, g_post, o_att, p, yr, w_pa, w_pr, w_out):
    S = x.shape[0]
    tm = min(256, S)

    def body(x_ref, t_ref, mod_ref, gp_ref, o_ref, za_ref, gl_ref, yr_ref, wpa_ref, wpr_ref, wout_ref,
             dout_ref, do_ref, dpm_ref, dyr_ref, mb_ref, dub_ref, yab_ref, dab_ref, drb_ref, sums_ref):
        @pl.when(pl.program_id(0) == 0)
        def _():
            sums_ref[...] = jnp.zeros_like(sums_ref)

        za = za_ref[...].astype(F32)
        sa = _sigmoid(za)
        sil = za * sa
        ov = o_ref[...]
        ya_b = (ov * sil).astype(BF16)
        yr_b = yr_ref[...]
        av = _dot(ya_b, wpa_ref[...])
        rv = _dot(yr_b, wpr_ref[...])
        ga = _sigmoid(gl_ref[:, :D].astype(F32))
        gr = _sigmoid(gl_ref[:, D:].astype(F32))
        mb = (ga * av + gr * rv).astype(BF16)
        u = _dot(mb, wout_ref[...])
        r2 = lax.rsqrt(jnp.mean(u * u, axis=-1, keepdims=True) + EPS)
        un = u * r2
        gp = gp_ref[...]
        yv = un * gp
        gate = mod_ref[2:3, :]
        err = (x_ref[...] + gate * yv) - t_ref[...]
        dout = err * (1.0 / D)
        dout_ref[...] = dout
        dy = dout * gate
        sums_ref[0:1, :] += jnp.sum(dout * yv, axis=0, keepdims=True)
        sums_ref[1:2, :] += jnp.sum(dy * un, axis=0, keepdims=True)
        sums_ref[2:3, :] += jnp.sum(err * err, axis=0, keepdims=True)
        dyg = dy * gp
        du_b = (r2 * (dyg - un * jnp.mean(dyg * un, axis=-1, keepdims=True))).astype(BF16)
        dm = _dot(du_b, wout_ref[...], NT)
        da_b = (dm * ga).astype(BF16)
        dr_b = (dm * gr).astype(BF16)
        dpm_ref[:, :D] = (dm * av * (ga * (1.0 - ga))).astype(BF16)
        dpm_ref[:, D:2 * D] = (dm * rv * (gr * (1.0 - gr))).astype(BF16)
        dya = _dot(da_b, wpa_ref[...], NT)
        dyr_ref[...] = _dot(dr_b, wpr_ref[...], NT)
        dov = dya * sil
        for g in range(4):
            dt = dov[:, 128 * g:128 * g + 128].T
            do_ref[2 * g] = dt[:DH].astype(BF16)
            do_ref[2 * g + 1] = dt[DH:].astype(BF16)
        dpm_ref[:, 2 * D:] = (dya * ov * (sa * (1.0 + za * (1.0 - sa)))).astype(BF16)
        mb_ref[...] = mb
        dub_ref[...] = du_b
        yab_ref[...] = ya_b
        dab_ref[...] = da_b
        drb_ref[...] = dr_b

    row = lambda w: pl.BlockSpec((tm, w), lambda i: (i, 0))
    sd = lambda w, dt: jax.ShapeDtypeStruct((S, w), dt)
    return pl.pallas_call(
        body, name="mid",
        out_shape=[sd(D, F32), jax.ShapeDtypeStruct((8, DH, S), BF16), sd(2560, BF16), sd(512, F32), sd(D, BF16), sd(D, BF16), sd(512, BF16),
                   sd(D, BF16), sd(D, BF16), jax.ShapeDtypeStruct((8, D), F32)],
        grid=(S // tm,),
        in_specs=[row(D), row(D), _full((3, D)), _full((1, D)), row(512), pl.BlockSpec((tm, 512), lambda i: (i, O_ZA // 512)),
                  pl.BlockSpec((tm, 2048), lambda i: (i, 0)), row(512), _full((512, D)), _full((512, D)), _full((D, D))],
        out_specs=[row(D), pl.BlockSpec((8, DH, tm), lambda i: (0, 0, i)), row(2560), row(512), row(D), row(D), row(512), row(D), row(D),
                   _full((8, D))],
        compiler_params=_cp(("arbitrary",), VMEM_BIG),
    )(x, tgt, mod, g_post, o_att, p, p, yr, w_pa, w_pr, w_out)


def _attn_bwd(qt, kh, kt, vh, dot_, ot, lse, xs):
    S = qt.shape[2]
    tq, tk = min(512, S), min(1024, S)

    def body(q_ref, k_ref, kt_ref, v_ref, do_ref, o_ref, lse_ref, dq_ref, dk_ref, dv_ref):
        j, i = pl.program_id(0), pl.program_id(1)
        cols = pl.ds(pl.multiple_of(i * tq, tq), tq)
        st = {0: _dot(k_ref[0], q_ref[0])}
        dpt = {0: _dot(v_ref[0], do_ref[0])}
        dk_acc, dv_acc, dqs = [None, None], [None, None], []
        for h in range(8):
            g = h // 4
            if h + 1 < 8:
                st[h + 1] = _dot(k_ref[(h + 1) // 4], q_ref[h + 1])
                dpt[h + 1] = _dot(v_ref[(h + 1) // 4], do_ref[h + 1])
            qt_h, dot_h = q_ref[h], do_ref[h]
            delta = jnp.sum(dot_h.astype(F32) * o_ref[h], axis=0, keepdims=True)
            pt = jnp.exp2(st[h] - lse_ref[g, h % 4:h % 4 + 1, :])
            dst = (pt * (dpt[h] - delta)).astype(BF16)
            dv_h = _dot(dot_h, pt.astype(BF16), NT)
            dk_h = _dot(qt_h, dst, NT)
            dqs.append(_dot(kt_ref[g], dst))
            dv_acc[g] = dv_h if dv_acc[g] is None else dv_acc[g] + dv_h
            dk_acc[g] = dk_h if dk_acc[g] is None else dk_acc[g] + dk_h
            del st[h], dpt[h]

        @pl.when(i == 0)
        def _():
            for g in range(2):
                dk_ref[g] = dk_acc[g]
                dv_ref[g] = dv_acc[g]

        @pl.when(i > 0)
        def _():
            for g in range(2):
                dk_ref[g] += dk_acc[g]
                dv_ref[g] += dv_acc[g]

        @pl.when(j == 0)
        def _():
            for h in range(8):
                dq_ref[h, :, cols] = dqs[h]

        @pl.when(j > 0)
        def _():
            for h in range(8):
                dq_ref[h, :, cols] += dqs[h]

    return _host_call(
        body, xs, name="attn_bwd",
        out_shape=[jax.ShapeDtypeStruct((8, DH, S), F32), jax.ShapeDtypeStruct((2, DH, S), F32), jax.ShapeDtypeStruct((2, DH, S), F32)],
        grid=(S // tk, S // tq),
        in_specs=[pl.BlockSpec((8, DH, tq), lambda j, i: (0, 0, i)), pl.BlockSpec((2, tk, DH), lambda j, i: (0, j, 0)),
                  pl.BlockSpec((2, DH, tk), lambda j, i: (0, 0, j)), pl.BlockSpec((2, tk, DH), lambda j, i: (0, j, 0)),
                  pl.BlockSpec((8, DH, tq), lambda j, i: (0, 0, i)), pl.BlockSpec((8, DH, tq), lambda j, i: (0, 0, i)),
                  pl.BlockSpec((2, 4, tq), lambda j, i: (0, 0, i))],
        out_specs=[pl.BlockSpec((8, DH, S), lambda j, i: (0, 0, 0)), pl.BlockSpec((2, DH, tk), lambda j, i: (0, 0, j)),
                   pl.BlockSpec((2, DH, tk), lambda j, i: (0, 0, j))],
        scratch_shapes=[], operands=(qt, kh, kt, vh, dot_, ot, lse),
        compiler_params=_cp(("arbitrary", "arbitrary"), VMEM_BIG),
    )


def _attn_prep_bwd(dqt, dkt, dvt, p, cos, sin, qg, kg):
    S = dqt.shape[2]
    tm = min(512, S)

    def body(dq_ref, dk_ref, dv_ref, qa_ref, ka_ref, cos_ref, sin_ref, qg_ref, kg_ref, dp_ref, gs_ref):
        @pl.when(pl.program_id(0) == 0)
        def _():
            gs_ref[...] = jnp.zeros_like(gs_ref)

        cos_v, sin_v = cos_ref[...], sin_ref[...]

        def pair(ref, a):
            return jnp.concatenate([ref[a], ref[a + 1]], axis=0).T

        def norm_bwd(dyv, xv, gv, row):
            r = lax.rsqrt(_head_mean(xv * xv) + EPS)
            xn = xv * r
            dxh = _rope_t(dyv, cos_v, sin_v)
            gs_ref[row:row + 1, :] += jnp.sum(dxh * xn, axis=0, keepdims=True)
            dg = dxh * gv
            return r * (dg - xn * _head_mean(dg * xn))

        for g in range(4):
            sl = slice(128 * g, 128 * g + 128)
            dp_ref[:, sl] = norm_bwd(pair(dq_ref, 2 * g) * 0.125, qa_ref[:, sl].astype(F32), qg_ref[...], 0).astype(BF16)
        dp_ref[:, 512:640] = norm_bwd(pair(dk_ref, 0) * LN2, ka_ref[...].astype(F32), kg_ref[...], 1).astype(BF16)
        dp_ref[:, 640:768] = pair(dv_ref, 0).astype(BF16)

    ht = lambda n: pl.BlockSpec((n, DH, tm), lambda i: (0, 0, i))
    return pl.pallas_call(
        body, name="attn_prep_bwd", out_shape=[jax.ShapeDtypeStruct((S, 768), BF16), jax.ShapeDtypeStruct((8, 128), F32)],
        grid=(S // tm,),
        in_specs=[ht(8), ht(2), ht(2),
                  pl.BlockSpec((tm, 512), lambda i: (i, O_QA // 512)), pl.BlockSpec((tm, 128), lambda i: (i, O_KA // 128)),
                  pl.BlockSpec((tm, 128), lambda i: (i, 0)), pl.BlockSpec((tm, 128), lambda i: (i, 0)), _full((1, 128)), _full((1, 128))],
        out_specs=[pl.BlockSpec((tm, 768), lambda i: (i, 0)), _full((8, 128))],
        compiler_params=_cp(("arbitrary",)),
    )(dqt, dkt, dvt, p, p, cos, sin, qg, kg)


def _ret_bwd_chunk(qr2, kr2, p, rf, rb, dc, qdf, qdb, gn, dyr, cos, sin, xs):
    S = qr2.shape[0]
    C, N = CH, S // CH

    def body(q_ref, k_ref, v_ref, z_ref, rf_ref, rb_ref, dc_ref, qdf_ref, qdb_ref, gn_ref, dyr_ref, cos_ref, sin_ref,
             dpa_ref, dk_ref, dv_ref, drf_ref, drb_ref, dgn_ref, dlg_ref, dqs):
        @pl.when(pl.program_id(0) == 0)
        def _():
            dgn_ref[...] = jnp.zeros_like(dgn_ref)
            dlg_ref[...] = jnp.zeros_like(dlg_ref)

        qv = q_ref[...]
        qb, kb, vb = qv.astype(BF16), k_ref[...].astype(BF16), v_ref[...].astype(BF16)
        qf32, qb32 = qv * qdf_ref[...], qv * qdb_ref[...]
        qfw, qbw = qf32.astype(BF16), qb32.astype(BF16)
        ii = lax.broadcasted_iota(jnp.int32, (C, C), 0).astype(F32)
        jj = lax.broadcasted_iota(jnp.int32, (C, C), 1).astype(F32)
        dif = ii - jj
        ri = lax.broadcasted_iota(jnp.int32, (C, 1), 0).astype(F32)
        hs = range(HR)
        sd, o = _ret_heads_fwd(qb, kb, vb, qfw, qbw, dc_ref, rf_ref, rb_ref)
        do_b = []
        for h in hs:
            vs = _vs(h)
            mu = jnp.mean(o[h], axis=-1, keepdims=True)
            rstd = lax.rsqrt(jnp.mean(jnp.square(o[h] - mu), axis=-1, keepdims=True) + EPS)
            on = (o[h] - mu) * rstd
            z = z_ref[:, vs].astype(F32)
            sz = _sigmoid(z)
            dy = dyr_ref[:, vs]
            gnv = gn_ref[:, vs]
            dpa_ref[:, 256 + DV * h:256 + DV * h + DV] = (dy * (on * gnv) * (sz * (1.0 + z * (1.0 - sz)))).astype(BF16)
            dys = dy * (z * sz)
            dgn_ref[:, vs] += jnp.sum(dys * on, axis=0, keepdims=True)
            don = dys * gnv
            do = rstd * (don - jnp.mean(don, axis=-1, keepdims=True) - on * jnp.mean(don * on, axis=-1, keepdims=True))
            do_b.append(do.astype(BF16))
        dpm = [_dot(do_b[h], vb[:, _vs(h)], NT) for h in hs]
        dqf = [_dot(do_b[h], rf_ref[0, h].astype(BF16), NT) for h in hs]
        dqb = [_dot(do_b[h], rb_ref[0, h].astype(BF16), NT) for h in hs]
        for h in hs:
            dv_ref[:, _vs(h)] = _dot(sd[h].astype(BF16), do_b[h], TN)
            drf_ref[0, h] = _dot(qfw[:, _ks(h)], do_b[h], TN)
            drb_ref[0, h] = _dot(qbw[:, _ks(h)], do_b[h], TN)
        dsd = [(dpm[h] * dc_ref[h]).astype(BF16) for h in hs]
        for h in hs:
            ks = _ks(h)
            dqs[:, ks] = _dot(dsd[h], kb[:, ks]) + dqf[h] * qdf_ref[:, ks] + dqb[h] * qdb_ref[:, ks]
            dk_ref[:, ks] = _dot(dsd[h], qb[:, ks], TN)
        for h in hs:
            ks = _ks(h)
            e = dpm[h] * sd[h]
            lf = _sum11(e * jnp.maximum(dif, 0.0)) + _sum11(jnp.sum(qf32[:, ks] * dqf[h], axis=-1, keepdims=True) * (ri + 1.0))
            lb = _sum11(e * jnp.maximum(-dif, 0.0)) + _sum11(jnp.sum(qb32[:, ks] * dqb[h], axis=-1, keepdims=True) * (C - ri))
            dlg_ref[h:h + 1, :] += jnp.broadcast_to(lf, (1, 128))
            dlg_ref[HR + h:HR + h + 1, :] += jnp.broadcast_to(lb, (1, 128))
        cos_v, sin_v = cos_ref[...], sin_ref[...]
        for g in range(2):
            sl = slice(128 * g, 128 * g + 128)
            dpa_ref[:, sl] = _rope_t(dqs[:, sl], cos_v, sin_v).astype(BF16)

    st = jax.ShapeDtypeStruct((N, HR, DH, DV), F32)
    stb = lambda: pl.BlockSpec((1, HR, DH, DV), lambda t: (t, 0, 0, 0))
    return _host_call(
        body, xs, name="ret_bwd_chunk",
        out_shape=[jax.ShapeDtypeStruct((S, 768), BF16), jax.ShapeDtypeStruct((S, 256), F32), jax.ShapeDtypeStruct((S, 512), F32), st, st,
                   jax.ShapeDtypeStruct((1, 512), F32), jax.ShapeDtypeStruct((8, 128), F32)],
        grid=(N,),
        in_specs=[pl.BlockSpec((C, 256), lambda t: (t, 0)), pl.BlockSpec((C, 256), lambda t: (t, 0)),
                  pl.BlockSpec((C, 512), lambda t: (t, O_VR // 512)), pl.BlockSpec((C, 512), lambda t: (t, O_ZR // 512)),
                  stb(), stb(), _full((HR, C, C)), _full((C, 256)), _full((C, 256)), _full((1, 512)),
                  pl.BlockSpec((C, 512), lambda t: (t, 0)), pl.BlockSpec((C, 128), lambda t: (t, 0)), pl.BlockSpec((C, 128), lambda t: (t, 0))],
        out_specs=[pl.BlockSpec((C, 768), lambda t: (t, 0)), pl.BlockSpec((C, 256), lambda t: (t, 0)), pl.BlockSpec((C, 512), lambda t: (t, 0)),
                   stb(), stb(), _full((1, 512)), _full((8, 128))],
        scratch_shapes=[pltpu.VMEM((C, 256), F32)], operands=(qr2, kr2, p, p, rf, rb, dc, qdf, qdb, gn, dyr, cos, sin),
        compiler_params=_cp(("arbitrary",)),
    )


def _ret_bwd_scan(kr2, p, rf, rb, drf, drb, kdf, kdb, adec):
    S = kr2.shape[0]
    C, N = CH, S // CH
    G = _scan_group(N)
    NG = N // G

    def body(kf_ref, vf_ref, kb_ref, vb_ref, rf_ref, rb_ref, drf_ref, drb_ref, kdf_ref, kdb_ref, a_ref,
             dkf_ref, dkb_ref, dvf_ref, dvb_ref, dlg_ref, gf, gb):
        @pl.when(pl.program_id(0) == 0)
        def _():
            gf[...] = jnp.zeros_like(gf)
            gb[...] = jnp.zeros_like(gb)
            dlg_ref[...] = jnp.zeros_like(dlg_ref)

        ri = lax.broadcasted_iota(jnp.int32, (C, 1), 0).astype(F32)

        def one(k_ref, v_ref, r_ref, dr_ref, kd_ref, g_s, dk_ref, dv_ref, row0, wexp, order):
            g = [g_s[h] for h in range(HR)]
            lgs = [jnp.zeros((1, 1), F32) for _ in range(HR)]
            for u in order:
                rows = slice(C * u, C * u + C)
                kd32 = k_ref[rows, :] * kd_ref[...]
                kdw = kd32.astype(BF16)
                vb = v_ref[rows, :].astype(BF16)
                for h in range(HR):
                    ks, vs = _ks(h), _vs(h)
                    g_b = g[h].astype(BF16)
                    dkd = _dot(vb[:, vs], g_b, NT)
                    dk_ref[rows, ks] = dkd * kd_ref[:, ks]
                    dv_ref[rows, vs] = _dot(kdw[:, ks], g_b)
                    av = a_ref[row0 + h:row0 + h + 1, :]
                    lgs[h] = lgs[h] + (_sum11(jnp.sum(kd32[:, ks] * dkd, axis=-1, keepdims=True) * wexp)
                                       + C * av[:, 0:1] * _sum11(r_ref[u, h] * g[h]))
                    g[h] = dr_ref[u, h] + av * g[h]
            for h in range(HR):
                g_s[h] = g[h]
                dlg_ref[row0 + h:row0 + h + 1, :] += jnp.broadcast_to(lgs[h], (1, 128))

        one(kf_ref, vf_ref, rf_ref, drf_ref, kdf_ref, gf, dkf_ref, dvf_ref, 0, C - 1.0 - ri, list(reversed(range(G))))
        one(kb_ref, vb_ref, rb_ref, drb_ref, kdb_ref, gb, dkb_ref, dvb_ref, HR, ri, list(range(G)))

    fwd = lambda w, off=0: pl.BlockSpec((G * C, w), lambda t: (NG - 1 - t, off))
    bwd = lambda w, off=0: pl.BlockSpec((G * C, w), lambda t: (t, off))
    stf = lambda: pl.BlockSpec((G, HR, DH, DV), lambda t: (NG - 1 - t, 0, 0, 0))
    stb = lambda: pl.BlockSpec((G, HR, DH, DV), lambda t: (t, 0, 0, 0))
    return pl.pallas_call(
        body, name="ret_bwd_scan",
        out_shape=[jax.ShapeDtypeStruct((S, 256), F32), jax.ShapeDtypeStruct((S, 256), F32), jax.ShapeDtypeStruct((S, 512), F32),
                   jax.ShapeDtypeStruct((S, 512), F32), jax.ShapeDtypeStruct((8, 128), F32)],
        grid=(NG,),
        in_specs=[fwd(256), fwd(512, O_VR // 512), bwd(256), bwd(512, O_VR // 512), stf(), stb(), stf(), stb(),
                  _full((C, 256)), _full((C, 256)), _full((8, 128))],
        out_specs=[fwd(256), bwd(256), fwd(512), bwd(512), _full((8, 128))],
        scratch_shapes=[pltpu.VMEM((HR, DH, DV), F32), pltpu.VMEM((HR, DH, DV), F32)],
        compiler_params=_cp(("arbitrary",)),
    )(kr2, p, kr2, p, rf, rb, drf, drb, kdf, kdb, adec)


def _ret_bwd_final(dk_i, dkf, dkb, dv_i, dvf, dvb, cos, sin):
    S = dk_i.shape[0]
    tm = min(512, S)

    def body(a_ref, b_ref, c_ref, d_ref, e_ref, f_ref, cos_ref, sin_ref, o_ref):
        o_ref[:, :512] = (d_ref[...] + e_ref[...] + f_ref[...]).astype(BF16)
        cos_v, sin_v = cos_ref[...], sin_ref[...]
        for g in range(2):
            sl = slice(128 * g, 128 * g + 128)
            dk = a_ref[:, sl] + b_ref[:, sl] + c_ref[:, sl]
            o_ref[:, 512 + 128 * g:512 + 128 * g + 128] = (_rope_t(dk, cos_v, sin_v) * 0.125).astype(BF16)

    row = lambda w: pl.BlockSpec((tm, w), lambda i: (i, 0))
    return pl.pallas_call(
        body, name="ret_bwd_final", out_shape=jax.ShapeDtypeStruct((S, 768), BF16), grid=(S // tm,),
        in_specs=[row(256), row(256), row(256), row(512), row(512), row(512), row(128), row(128)], out_specs=row(768),
        compiler_params=_cp(("parallel",)),
    )(dk_i, dkf, dkb, dv_i, dvf, dvb, cos, sin)


def _bwd_in(dpm, dpa, dpra, dprb, w_p, x, dout, mod, g_pre, xs):
    S = x.shape[0]
    tm = min(256, S)

    def body(a_ref, b_ref, c_ref, d_ref, w_ref, x_ref, dout_ref, mod_ref, g_ref, gx_ref, sums_ref):
        @pl.when(pl.program_id(0) == 0)
        def _():
            sums_ref[...] = jnp.zeros_like(sums_ref)

        dh = (_dot(a_ref[...], w_ref[:, :O_QA], NT) + _dot(b_ref[...], w_ref[:, O_QA:O_QR], NT)
              + _dot(c_ref[...], w_ref[:, O_QR:O_VR], NT) + _dot(d_ref[...], w_ref[:, O_VR:], NT))
        xv = x_ref[...]
        r = lax.rsqrt(jnp.mean(xv * xv, axis=-1, keepdims=True) + EPS)
        xn = xv * r
        gv = g_ref[...]
        sc1 = 1.0 + mod_ref[1:2, :]
        sums_ref[0:1, :] += jnp.sum(dh, axis=0, keepdims=True)
        sums_ref[1:2, :] += jnp.sum(dh * (xn * gv), axis=0, keepdims=True)
        sums_ref[2:3, :] += jnp.sum(dh * xn, axis=0, keepdims=True) * sc1
        dxn = dh * (gv * sc1)
        gx_ref[...] = dout_ref[...] + r * (dxn - xn * jnp.mean(dxn * xn, axis=-1, keepdims=True))

    row = lambda w: pl.BlockSpec((tm, w), lambda i: (i, 0))
    return _host_call(
        body, xs, name="bwd_in", out_shape=[jax.ShapeDtypeStruct((S, D), F32), jax.ShapeDtypeStruct((8, D), F32)], grid=(S // tm,),
        in_specs=[row(2560), row(768), row(768), row(768), _full((D, P_W)), row(D), row(D), _full((3, D)), _full((1, D))],
        out_specs=[row(D), _full((8, D))], scratch_shapes=[], operands=(dpm, dpa, dpra, dprb, w_p, x, dout, mod, g_pre),
        compiler_params=_cp(("arbitrary",), VMEM_BIG),
    )


SMALL = ("b_ada", "g_pre", "qn_g", "kn_g", "w_dec_f", "w_dec_b", "gn_g", "g_post")


def _small_update(gathered, wmv):
    ns = len(SMALL)

    def body(*refs):
        gin_ref, gmid_ref, ggn_ref, gatt_ref, gl1_ref, gl2_ref = refs[:6]
        wmv_refs = refs[6:6 + 3 * ns]
        loss_ref = refs[6 + 3 * ns]
        out_refs = refs[7 + 3 * ns:]

        def dsum(ref, r=None):
            rows = slice(None) if r is None else slice(r, r + 1)
            acc = ref[0, rows, :]
            for d in range(1, NDEV):
                acc = acc + ref[d, rows, :]
            return acc

        s_lg = dsum(gl1_ref) + dsum(gl2_ref)
        loss_ref[...] = (0.5 / D) * jnp.sum(dsum(gmid_ref, 2), axis=-1, keepdims=True)
        eye = lax.broadcasted_iota(jnp.int32, (8, 128), 0) == lax.broadcasted_iota(jnp.int32, (8, 128), 1)
        dlg = jnp.sum(jnp.where(eye, s_lg, 0.0), axis=0, keepdims=True)
        w_f, w_b = wmv_refs[3 * SMALL.index("w_dec_f")][...], wmv_refs[3 * SMALL.index("w_dec_b")][...]
        s_q, s_k = dsum(gatt_ref, 0), dsum(gatt_ref, 1)
        grads = dict(
            b_ada=jnp.concatenate([dsum(gin_ref, 0), dsum(gin_ref, 1), dsum(gmid_ref, 0)], axis=1),
            g_pre=dsum(gin_ref, 2), g_post=dsum(gmid_ref, 1), gn_g=dsum(ggn_ref),
            qn_g=s_q[:, :DH] + s_q[:, DH:], kn_g=s_k[:, :DH] + s_k[:, DH:],
            w_dec_f=dlg[:, 0:HR] * _sigmoid(-w_f), w_dec_b=dlg[:, HR:2 * HR] * _sigmoid(-w_b))
        for i, nme in enumerate(SMALL):
            g = grads[nme]
            w_ref, m_ref, v_ref = wmv_refs[3 * i:3 * i + 3]
            g_ref, d_ref, nm_ref, nv_ref = out_refs[4 * i:4 * i + 4]
            g_ref[...] = g
            m2 = ADAM_B1 * m_ref[...] + (1.0 - ADAM_B1) * g
            v2 = ADAM_B2 * v_ref[...] + (1.0 - ADAM_B2) * jnp.square(g)
            m_hat = m2 / (1.0 - ADAM_B1 ** ADAM_STEP)
            v_hat = v2 / (1.0 - ADAM_B2 ** ADAM_STEP)
            d_ref[...] = -ADAM_LR * (m_hat / (jnp.sqrt(v_hat) + ADAM_EPS) + ADAM_WD * w_ref[...])
            nm_ref[...] = m2
            nv_ref[...] = v2

    out_shape = [jax.ShapeDtypeStruct((1, 1), F32)]
    for i in range(ns):
        out_shape += [jax.ShapeDtypeStruct(wmv[3 * i].shape, F32)] * 4
    return pl.pallas_call(body, name="small_update", out_shape=out_shape)(*gathered, *wmv)


def _adamw(parts, w, m, v, name):
    n, R, L = parts.shape
    tr = 256 if (R % 256 == 0 and R > 256) else R

    def body(p_ref, w_ref, m_ref, v_ref, g_ref, d_ref, nm_ref, nv_ref):
        g = p_ref[0].astype(F32)
        for k in range(1, n):
            g = g + p_ref[k].astype(F32)
        g_ref[...] = g
        m2 = ADAM_B1 * m_ref[...] + (1.0 - ADAM_B1) * g
        v2 = ADAM_B2 * v_ref[...] + (1.0 - ADAM_B2) * jnp.square(g)
        m_hat = m2 / (1.0 - ADAM_B1 ** ADAM_STEP)
        v_hat = v2 / (1.0 - ADAM_B2 ** ADAM_STEP)
        d_ref[...] = -ADAM_LR * (m_hat / (jnp.sqrt(v_hat) + ADAM_EPS) + ADAM_WD * w_ref[...])
        nm_ref[...] = m2
        nv_ref[...] = v2

    blk = pl.BlockSpec((tr, L), lambda i: (i, 0))
    o = jax.ShapeDtypeStruct((R, L), F32)
    return pl.pallas_call(
        body, name=name, out_shape=[o, o, o, o], grid=(R // tr,),
        in_specs=[pl.BlockSpec((n, tr, L), lambda i: (0, i, 0)), blk, blk, blk], out_specs=[blk, blk, blk, blk],
        compiler_params=_cp(("parallel",), VMEM_BIG),
    )(parts, w, m, v)


def _rope_tables(S):
    f = np.float32
    t = np.arange(S)
    row, col = (t // 64).astype(f), (t % 64).astype(f)
    half = DH // 2
    inv = np.power(f(ROPE_THETA), -np.arange(0, half, 2, dtype=f) / f(half)).astype(f)
    ar, ac = (row[:, None] * inv[None, :]).astype(f), (col[:, None] * inv[None, :]).astype(f)
    cos64 = np.concatenate([np.cos(ar), np.cos(ar), np.cos(ac), np.cos(ac)], axis=1).astype(f)
    sin64 = np.concatenate([-np.sin(ar), np.sin(ar), -np.sin(ac), np.sin(ac)], axis=1).astype(f)
    return jnp.asarray(np.tile(cos64, (1, 2))), jnp.asarray(np.tile(sin64, (1, 2)))


def _to_p_order(w_orig):
    return jnp.concatenate([w_orig[:, ORIG[n][0]:ORIG[n][1]] for n in P_ORDER], axis=1)


def _pad_lanes(v, n):
    return jnp.pad(v, ((0, 0), (0, n - v.shape[1])))


def kernel(x, c, w_ada, b_ada, g_pre, w_in, qn_g, kn_g, w_dec_f, w_dec_b, gn_g, w_pa, w_pr, w_out, g_post, loss_target, m_w_ada, m_b_ada, m_g_pre, m_w_in, m_qn_g, m_kn_g, m_w_dec_f, m_w_dec_b, m_gn_g, m_w_pa, m_w_pr, m_w_out, m_g_post, v_w_ada, v_b_ada, v_g_pre, v_w_in, v_qn_g, v_kn_g, v_w_dec_f, v_w_dec_b, v_gn_g, v_w_pa, v_w_pr, v_w_out, v_g_post):
    S = x.shape[1]
    me = 4 * lax.axis_index("x") + 2 * lax.axis_index("y") + lax.axis_index("c")
    xs, tgt = x[0], loss_target[0]
    ncol_ada = w_ada.shape[2]
    ncol_in = w_in.shape[2]

    b_ada_s = lax.dynamic_slice(b_ada, (0, me * ncol_ada), (1, ncol_ada))
    mod_all, c_act, (wg_in,) = _prologue(jnp.pad(c, ((0, 7), (0, 0))), w_ada[0], b_ada_s, [w_in[0].astype(BF16)])
    mod = lax.dynamic_index_in_dim(mod_all, me, axis=1, keepdims=False).reshape(3, D)
    w_p = _to_p_order(wg_in.transpose(1, 0, 2).reshape(D, NDEV * ncol_in))
    all_dev = tuple(range(NDEV))
    st_w, tok_w = _xchg_start([(w_pa[0].astype(BF16)[None], all_dev), (w_pr[0].astype(BF16)[None], all_dev),
                               (w_out[0].astype(BF16)[None], all_dev)], "wgather_start")

    cos, sin = _rope_tables(S)
    qg, kg = jnp.tile(qn_g, (1, 2)), jnp.tile(kn_g, (1, 2))

    p, h = _fwd_in(xs, mod, g_pre + tok_w[0:1, 0:1], w_p)
    qt, kh, kt, vh, vta, qr2, kr2 = _prep(p, cos, sin, qg, kg)
    o_att, o_t, lse = _attn_fwd(qt, kh, vta)
    dc, qdf, qdb, kdf, kdb, adec = _ret_tables(w_dec_f, w_dec_b)
    rf, rb = _ret_states(kr2, p, kdf, kdb, adec)
    yr = _ret_out(qr2, kr2, p, rf, rb, dc, qdf, qdb, gn_g)
    wg_pa, wg_pr, wg_out = _xchg_wait([st_w], st_w["lands"], [[0, 1, 2]], yr, "wgather_wait")
    w_pa_f = wg_pa.transpose(1, 0, 2).reshape(512, D)
    w_pr_f = wg_pr.transpose(1, 0, 2).reshape(512, D)
    w_out_f = wg_out.reshape(D, D)

    dout, do, dpm, dyr, mb, dub, yab, dab, drb_, sums_mid = _mid(xs, tgt, mod, g_post, o_att, p, yr, w_pa_f, w_pr_f, w_out_f)
    gw_out = _mm_tn(mb, dub, "gw_out")
    gw_pa = _mm_tn(yab, dab, "gw_pa")
    gw_pr = _mm_tn(yr, drb_, "gw_pr")
    gi_m = _mm_tn(h, dpm, "gw_in_mid")

    def shards(cols, nd):
        return cols.astype(BF16).reshape(D, nd, ncol_in).transpose(1, 0, 2)

    st_a, tok_a = _xchg_start([
        (gw_out.astype(BF16).reshape(NDEV, 128, D), all_dev),
        (gw_pa.astype(BF16).reshape(512, NDEV, 128).transpose(1, 0, 2), all_dev),
        (gw_pr.astype(BF16).reshape(512, NDEV, 128).transpose(1, 0, 2), all_dev),
        (shards(gi_m[:, 224:2048], 3), (5, 6, 7))], "xchg_start_a",
        lands=[None, None, None, jnp.zeros((NDEV, D, ncol_in), BF16)])
    (dqt, dkt, dvt), _ = _attn_bwd(qt, kh, kt, vh, do, o_t, lse + tok_a[0, 0], [])
    dpa, gs_att = _attn_prep_bwd(dqt, dkt, dvt, p, cos, sin, qg, kg)
    gi_a = _mm_tn(h, dpa, "gw_in_att")
    st_b, tok_b = _xchg_start([(shards(jnp.concatenate([gi_a, gi_m[:, 2048:2496]], axis=1), 2), (0, 1))], "xchg_start_b",
                              lands=[st_a["lands"][3]])
    (dpra, dk_i, dv_i, drf, drb, dgn, dlg1), _ = _ret_bwd_chunk(qr2, kr2, p, rf, rb, dc, qdf, qdb, gn_g + tok_b[0:1, 0:1], dyr, cos, sin, [])
    dkf, dkb, dvf, dvb, dlg2 = _ret_bwd_scan(kr2, p, rf, rb, drf, drb, kdf, kdb, adec)
    dprb = _ret_bwd_final(dk_i, dkf, dkb, dv_i, dvf, dvb, cos, sin)
    gi_ra = _mm_tn(h, dpra, "gw_in_reta")
    gi_rb = _mm_tn(h, dprb, "gw_in_retb")
    chip_c = _pair_reduce(shards(jnp.concatenate([gi_m[:, 2496:2560], gi_ra[:, :256], gi_rb[:, 512:768], gi_rb[:, :512],
                                                  gi_ra[:, 256:768], gi_m[:, :224]], axis=1), 3), (2, 3, 4), "pair_reduce_c")
    st_c, tok_c = _xchg_start([(chip_c, (2, 3, 4, "same core"))], "xchg_start_c", lands=[st_b["lands"][0]])
    (grad_x, sums_in), _ = _bwd_in(dpm, dpa, dpra, dprb, w_p, xs, dout, mod, g_pre + tok_c[0:1, 0:1], [])

    gathered = _small_allgather([sums_in, sums_mid, dgn, gs_att, dlg1, dlg2], "ag_small")
    given = dict(b_ada=(b_ada, m_b_ada, v_b_ada), g_pre=(g_pre, m_g_pre, v_g_pre), qn_g=(qn_g, m_qn_g, v_qn_g), kn_g=(kn_g, m_kn_g, v_kn_g),
                 w_dec_f=(w_dec_f, m_w_dec_f, v_w_dec_f), w_dec_b=(w_dec_b, m_w_dec_b, v_w_dec_b), gn_g=(gn_g, m_gn_g, v_gn_g),
                 g_post=(g_post, m_g_post, v_g_post))
    small = _small_update(gathered, [a for nme in SMALL for a in given[nme]])
    loss = small[0][0, 0]

    g_in_all, g_mid_all = gathered[0], gathered[1]
    dmod_all = lax.dynamic_slice(jnp.concatenate([g_in_all[:, 0, :], g_in_all[:, 1, :], g_mid_all[:, 0, :]], axis=1),
                                 (0, me * ncol_ada), (NDEV, ncol_ada))
    g_ada = _mm_tn(c_act, jnp.pad(dmod_all, ((0, 8), (0, 0))).astype(BF16), "gw_ada")

    ada = _adamw(g_ada[None], w_ada[0], m_w_ada[0], v_w_ada[0], "adamw_ada")
    rs_out, rs_pa, rs_pr, rs_in = _xchg_wait([st_a, st_b, st_c], list(st_a["lands"][:3]) + [st_c["lands"][0]],
                                             [[0, 1, 2, 3], [3], [3]], ada[1], "xchg_wait")
    res = dict(
        w_ada=ada,
        w_in=_adamw(rs_in, w_in[0], m_w_in[0], v_w_in[0], "adamw_in"),
        w_pa=_adamw(rs_pa, w_pa[0], m_w_pa[0], v_w_pa[0], "adamw_pa"),
        w_pr=_adamw(rs_pr, w_pr[0], m_w_pr[0], v_w_pr[0], "adamw_pr"),
        w_out=_adamw(rs_out, w_out[0], m_w_out[0], v_w_out[0], "adamw_out"),
    )
    names = ["w_ada", "b_ada", "g_pre", "w_in", "qn_g", "kn_g", "w_dec_f", "w_dec_b", "gn_g", "w_pa", "w_pr", "w_out", "g_post"]
    outs = [[], [], [], []]
    for nme in names:
        for q in range(4):
            if nme in res:
                outs[q].append(res[nme][q][None])
            else:
                outs[q].append(small[1 + 4 * SMALL.index(nme) + q])
    return (loss, grad_x[None], *outs[0], *outs[1], *outs[2], *outs[3])
```

```python
import jax
import jax.numpy as jnp
import numpy as np
from jax import lax
from jax.experimental import pallas as pl
from jax.experimental.pallas import tpu as pltpu

F32, BF16 = jnp.float32, jnp.bfloat16
D = 1024
DH = 64
DHA = 80
DV = 128
LOG2E = 1.4426950408889634
LN2 = 0.6931471805599453
HR = 4
CH = 128
EPS = 1e-6
ROPE_THETA = 10000.0
NDEV = 8
O_GL, O_ZA, O_QA, O_KA, O_VA, O_QR, O_ZR, O_VR, O_KR, P_W = 0, 2048, 2560, 3072, 3200, 3328, 3584, 4096, 4608, 4864
ORIG = dict(qa=(0, 512), ka=(512, 640), va=(640, 768), za=(768, 1280), qr=(1280, 1536), kr=(1536, 1792),
            vr=(1792, 2304), zr=(2304, 2816), gl=(2816, 4864))
P_ORDER = ("gl", "za", "qa", "ka", "va", "qr", "zr", "vr", "kr")
ADAM_LR, ADAM_B1, ADAM_B2, ADAM_EPS, ADAM_WD, ADAM_STEP = 0.001, 0.9, 0.999, 1e-08, 0.01, 10
VMEM_BIG = 56 * 1024 * 1024
MESH = pl.DeviceIdType.MESH

NT = (((1,), (1,)), ((), ()))
TN = (((0,), (0,)), ((), ()))


def _dot(a, b, dims=None):
    if dims is None:
        return jnp.dot(a, b, preferred_element_type=F32)
    return lax.dot_general(a, b, dims, preferred_element_type=F32)


def _cp(sem=None, vmem=None):
    kw = {}
    if sem is not None:
        kw["dimension_semantics"] = sem
    if vmem is not None:
        kw["vmem_limit_bytes"] = vmem
    return pltpu.CompilerParams(**kw)


def _sigmoid(z):
    return 1.0 / (1.0 + jnp.exp(-z))


def _sum11(m):
    return jnp.sum(jnp.sum(m, axis=-1, keepdims=True), axis=0, keepdims=True)


def _full(shape):
    n = len(shape)
    return pl.BlockSpec(shape, lambda *_: (0,) * n)


def _my_pos():
    return lax.axis_index("x"), lax.axis_index("y"), lax.axis_index("c")


def _peer(k, x, y, c):
    return ((1 - x) if k & 4 else x, (1 - y) if k & 2 else y, (1 - c) if k & 1 else c)


def _small_allgather(vs, name):
    n = len(vs)

    def body(*refs):
        v_refs, out_refs = refs[:n], refs[n:2 * n]
        send_sems, recv_sems = refs[2 * n:]
        x, y, c = _my_pos()
        me = 4 * x + 2 * y + c
        cps = []
        for a in range(n):
            out_refs[a][me] = v_refs[a][...]
            for k in range(1, NDEV):
                cp = pltpu.make_async_remote_copy(src_ref=v_refs[a], dst_ref=out_refs[a].at[me], send_sem=send_sems.at[a, k - 1],
                                                  recv_sem=recv_sems.at[a, k - 1], device_id=_peer(k, x, y, c), device_id_type=MESH)
                cp.start()
                cps.append(cp)
        for cp in cps:
            cp.wait()

    vm = pl.BlockSpec(memory_space=pltpu.VMEM)
    return pl.pallas_call(
        body, name=name, out_shape=[jax.ShapeDtypeStruct((NDEV,) + v.shape, v.dtype) for v in vs],
        in_specs=[vm] * n, out_specs=[vm] * n,
        scratch_shapes=[pltpu.SemaphoreType.DMA((n, NDEV - 1)), pltpu.SemaphoreType.DMA((n, NDEV - 1))],
    )(*vs)


def _prologue(c8, w_ada_s, b_ada_s, arrs):
    n = len(arrs)
    ncol = w_ada_s.shape[1]

    def body(*refs):
        c_ref, wa_ref, ba_ref = refs[:3]
        ins = refs[3:3 + n]
        mod_ref, cact_ref = refs[3 + n:5 + n]
        outs = refs[5 + n:5 + 2 * n]
        call_ref, send_sems, recv_sems, local_sems, s_send, s_recv = refs[5 + 2 * n:]
        x, y, c = _my_pos()
        me, sibling = (x, y, c), (x, y, 1 - c)
        chips = [(1 - x, y), (x, 1 - y), (1 - x, 1 - y)]
        me_i = 4 * x + 2 * y + c

        def small_gather(src_ref, dst_ref, row):
            cps = []
            for k in range(1, NDEV):
                cp = pltpu.make_async_remote_copy(src_ref=src_ref, dst_ref=dst_ref.at[me_i], send_sem=s_send.at[row, k - 1],
                                                  recv_sem=s_recv.at[row, k - 1], device_id=_peer(k, x, y, c), device_id_type=MESH)
                cp.start()
                cps.append(cp)
            return cps

        def blk(a, px, py, pc):
            return outs[a].at[4 * px + 2 * py + pc]

        def copy(a, k, block, to, src=None):
            return pltpu.make_async_remote_copy(src_ref=blk(a, *block) if src is None else src, dst_ref=blk(a, *block),
                                                send_sem=send_sems.at[a, k], recv_sem=recv_sems.at[a, k], device_id=to, device_id_type=MESH)

        call_ref[me_i] = c_ref[...]
        for cp in small_gather(c_ref, call_ref, 0):
            cp.wait()

        local, sent = [], []
        for a in range(n):
            mine = pltpu.make_async_copy(ins[a], blk(a, *me), local_sems.at[a])
            mine.start()
            local.append(mine)
            first = [copy(a, 0, me, sibling, src=ins[a])] + [copy(a, 1 + j, me, (*chip, c), src=ins[a]) for j, chip in enumerate(chips)]
            for cp in first:
                cp.start()
            sent += first

        cv = call_ref[:, 0, :]
        ca = jnp.concatenate([cv * _sigmoid(cv), jnp.zeros_like(cv)], axis=0).astype(BF16)
        cact_ref[...] = ca
        mod_ref[me_i] = (_dot(ca, wa_ref[...].astype(BF16)) + ba_ref[...])[:8]
        mod_copies = small_gather(mod_ref.at[me_i], mod_ref, 1)

        for j, chip in enumerate(chips):
            for a in range(n):
                copy(a, 1 + j, (*chip, c), me).wait_recv()
                cp = copy(a, 4 + j, (*chip, c), sibling)
                cp.start()
                sent.append(cp)
        for a in range(n):
            copy(a, 0, sibling, me).wait_recv()
            for j, chip in enumerate(chips):
                copy(a, 4 + j, (*chip, 1 - c), me).wait_recv()
        for cp in sent:
            cp.wait_send()
        for cp in local + mod_copies:
            cp.wait()

    vm, hbm = pl.BlockSpec(memory_space=pltpu.VMEM), pl.BlockSpec(memory_space=pl.ANY)
    res = pl.pallas_call(
        body, name="prologue",
        out_shape=[jax.ShapeDtypeStruct((NDEV, 8, ncol), F32), jax.ShapeDtypeStruct((16, D), BF16)]
        + [jax.ShapeDtypeStruct((NDEV,) + a.shape, a.dtype) for a in arrs],
        in_specs=[vm, vm, vm] + [hbm] * n, out_specs=[vm, vm] + [hbm] * n,
        scratch_shapes=[pltpu.VMEM((NDEV, 8, D), F32), pltpu.SemaphoreType.DMA((n, NDEV - 1)), pltpu.SemaphoreType.DMA((n, NDEV - 1)),
                        pltpu.SemaphoreType.DMA((n,)), pltpu.SemaphoreType.DMA((2, NDEV - 1)), pltpu.SemaphoreType.DMA((2, NDEV - 1))],
    )(c8, w_ada_s, b_ada_s, *arrs)
    return res[0], res[1], res[2:]


def _in_set(idx, dests):
    p = idx == dests[0]
    for d in dests[1:]:
        p = jnp.logical_or(p, idx == d)
    return p


def _host_call(body, xs, *, name, grid, in_specs, out_specs, out_shape, scratch_shapes, operands, compiler_params):
    nx, nin, nout, nscr = len(xs), len(operands), len(out_shape), len(scratch_shapes)
    if nx == 0:
        res = pl.pallas_call(body, name=name, grid=grid, in_specs=in_specs, out_specs=out_specs, out_shape=out_shape,
                             scratch_shapes=scratch_shapes, compiler_params=compiler_params)(*operands)
        return res, []
    ops, specs, aliases = list(operands), list(in_specs), {}
    oshape, ospecs = list(out_shape), list(out_specs)
    any_spec = pl.BlockSpec(memory_space=pl.ANY)
    for a, (send, dests, recv) in enumerate(xs):
        ops.append(send)
        specs.append(any_spec)
        if recv is not None:
            aliases[len(ops)] = nout + a
            ops.append(recv)
            specs.append(any_spec)
            oshape.append(jax.ShapeDtypeStruct(recv.shape, recv.dtype))
        else:
            oshape.append(jax.ShapeDtypeStruct((NDEV,) + send.shape[1:], send.dtype))
        ospecs.append(any_spec)
    ntot_in = len(ops)

    def wrapped(*refs):
        host_in = refs[:nin]
        sends, pos = [], nin
        for (_, _, recv) in xs:
            sends.append(refs[pos])
            pos += 1 if recv is None else 2
        host_out = refs[ntot_in:ntot_in + nout]
        recvs = refs[ntot_in + nout:ntot_in + nout + nx]
        host_scr = refs[ntot_in + nout + nx:ntot_in + nout + nx + nscr]
        send_sems, recv_sems, local_sems = refs[ntot_in + nout + nx + nscr:]
        first = pl.program_id(0) == 0
        last = pl.program_id(0) == grid[0] - 1
        for ax in range(1, len(grid)):
            first = jnp.logical_and(first, pl.program_id(ax) == 0)
            last = jnp.logical_and(last, pl.program_id(ax) == grid[ax] - 1)
        x, y, c = _my_pos()
        me = 4 * x + 2 * y + c

        def each(fn_remote, fn_local):
            for a, (_, dests, _) in enumerate(xs):
                lo, nd = dests[0], len(dests)
                for k in range(1, NDEV):
                    px, py, pc = _peer(k, x, y, c)
                    pidx = 4 * px + 2 * py + pc
                    cp = pltpu.make_async_remote_copy(src_ref=sends[a].at[jnp.clip(pidx - lo, 0, nd - 1)], dst_ref=recvs[a].at[me],
                                                      send_sem=send_sems.at[a, k - 1], recv_sem=recv_sems.at[a, k - 1],
                                                      device_id=(px, py, pc), device_id_type=MESH)
                    fn_remote(cp, _in_set(pidx, dests), _in_set(me, dests))
                lc = pltpu.make_async_copy(sends[a].at[jnp.clip(me - lo, 0, nd - 1)], recvs[a].at[me], local_sems.at[a])
                fn_local(lc, _in_set(me, dests))

        def start_remote(cp, to_dest, _):
            pl.when(jnp.logical_and(first, to_dest))(cp.start)

        def start_local(lc, i_am_dest):
            pl.when(jnp.logical_and(first, i_am_dest))(lc.start)

        def wait_remote(cp, to_dest, i_am_dest):
            pl.when(jnp.logical_and(last, to_dest))(cp.wait_send)
            pl.when(jnp.logical_and(last, i_am_dest))(cp.wait_recv)

        def wait_local(lc, i_am_dest):
            pl.when(jnp.logical_and(last, i_am_dest))(lc.wait)

        each(start_remote, start_local)
        body(*host_in, *host_out, *host_scr)
        each(wait_remote, wait_local)

    res = pl.pallas_call(
        wrapped, name=name, grid=grid, in_specs=specs, out_specs=ospecs, out_shape=oshape, input_output_aliases=aliases,
        scratch_shapes=list(scratch_shapes) + [pltpu.SemaphoreType.DMA((nx, NDEV - 1)), pltpu.SemaphoreType.DMA((nx, NDEV - 1)),
                                               pltpu.SemaphoreType.DMA((nx,))],
        compiler_params=compiler_params,
    )(*ops)
    return res[:nout], res[nout:]


_HBM = pl.BlockSpec(memory_space=pltpu.HBM)
_SEM = pl.BlockSpec(memory_space=pltpu.SEMAPHORE)


def _pair_reduce(send, dests, name):
    nd = send.shape[0]

    def body(s_ref, o_ref, land, ssem, rsem):
        x, y, c = _my_pos()
        cps = []
        for i in range(nd):
            cp = pltpu.make_async_remote_copy(src_ref=s_ref.at[i], dst_ref=land.at[i], send_sem=ssem.at[i], recv_sem=rsem.at[i],
                                              device_id=(x, y, 1 - c), device_id_type=MESH)
            pl.when(c != (dests[i] & 1))(cp.start)
            cps.append(cp)
        for i in range(nd):
            mine = c == (dests[i] & 1)

            @pl.when(mine)
            def _():
                cps[i].wait_recv()
                o_ref[i] = (s_ref[i].astype(F32) + land[i].astype(F32)).astype(BF16)

            pl.when(jnp.logical_not(mine))(cps[i].wait_send)

    vm = pl.BlockSpec(memory_space=pltpu.VMEM)
    return pl.pallas_call(
        body, name=name, out_shape=jax.ShapeDtypeStruct(send.shape, send.dtype), in_specs=[vm], out_specs=vm,
        scratch_shapes=[pltpu.VMEM(send.shape, send.dtype), pltpu.SemaphoreType.DMA((nd,)), pltpu.SemaphoreType.DMA((nd,))],
        compiler_params=_cp(None, VMEM_BIG),
    )(send)


def _xchg_copies(xs_dests, sends, lands, ssem, rsem, lsem):
    x, y, c = _my_pos()
    me = 4 * x + 2 * y + c
    remote, local = [], []
    for a, dests in enumerate(xs_dests):
        same_core = dests[-1] == "same core"
        dests = dests[:-1] if same_core else dests
        lo, nd = dests[0], sends[a].shape[0]
        for k in range(1, NDEV):
            if same_core and k & 1:
                continue
            px, py, pc = _peer(k, x, y, c)
            pidx = 4 * px + 2 * py + pc
            cp = pltpu.make_async_remote_copy(src_ref=sends[a].at[jnp.clip(pidx - lo, 0, nd - 1)], dst_ref=lands[a].at[me],
                                              send_sem=ssem.at[a * (NDEV - 1) + k - 1], recv_sem=rsem.at[a * (NDEV - 1) + k - 1],
                                              device_id=(px, py, pc), device_id_type=MESH)
            remote.append((cp, _in_set(pidx, dests), _in_set(me, dests)))
        lc = pltpu.make_async_copy(sends[a].at[jnp.clip(me - lo, 0, nd - 1)], lands[a].at[me], lsem.at[a])
        local.append((lc, _in_set(me, dests)))
    return remote, local


def _xchg_start(xs, name, lands=None):
    n = len(xs)
    dests = [d for _, d in xs]
    sends = [pltpu.with_memory_space_constraint(s, pltpu.HBM) for s, _ in xs]
    lands = [None] * n if lands is None else lands
    lands = [pltpu.with_memory_space_constraint(lax.empty((NDEV,) + s.shape[1:], s.dtype) if l is None else l, pltpu.HBM)
             for (s, _), l in zip(xs, lands)]

    def body(*refs):
        send_refs, land_refs = refs[:n], refs[n:2 * n]
        ssem, rsem, lsem = refs[2 * n:2 * n + 3]
        token = refs[-1]
        remote, local = _xchg_copies(dests, send_refs, land_refs, ssem, rsem, lsem)
        for cp, to_dest, _ in remote:
            pl.when(to_dest)(cp.start)
        for lc, i_am_dest in local:
            pl.when(i_am_dest)(lc.start)
        token[...] = jnp.zeros_like(token)

    res = pl.pallas_call(
        body, name=name,
        out_shape=[pltpu.SemaphoreType.DMA((n * (NDEV - 1),)), pltpu.SemaphoreType.DMA((n * (NDEV - 1),)), pltpu.SemaphoreType.DMA((n,))]
        + [pltpu.HBM(a.shape, a.dtype) for a in list(sends) + list(lands)] + [jax.ShapeDtypeStruct((8, 128), F32)],
        in_specs=[_HBM] * (2 * n), out_specs=[_SEM, _SEM, _SEM] + [_HBM] * (2 * n) + [pl.BlockSpec(memory_space=pltpu.VMEM)],
        input_output_aliases={i: 3 + i for i in range(2 * n)},
        compiler_params=pltpu.CompilerParams(has_side_effects=pltpu.SideEffectType.DATAFLOW_SIDE_EFFECTING),
    )(*sends, *lands)
    return dict(sems=res[0:3], sends=res[3:3 + n], lands=res[3 + n:3 + 2 * n], dests=dests), res[-1]


def _xchg_wait(states, lands, land_of, after, name):
    flat = []
    for st in states:
        flat += list(st["sends"]) + list(st["sems"])
    nl = len(lands)

    def body(*refs):
        land_refs = refs[:nl]
        pos = nl
        for s, st in enumerate(states):
            n = len(st["dests"])
            send_refs = refs[pos:pos + n]
            ssem, rsem, lsem = refs[pos + n:pos + n + 3]
            pos += n + 3
            remote, local = _xchg_copies(st["dests"], send_refs, [land_refs[i] for i in land_of[s]], ssem, rsem, lsem)
            for cp, to_dest, i_am_dest in remote:
                pl.when(to_dest)(cp.wait_send)
                pl.when(i_am_dest)(cp.wait_recv)
            for lc, i_am_dest in local:
                pl.when(i_am_dest)(lc.wait)

    in_specs = [_HBM] * nl
    for st in states:
        in_specs += [_HBM] * len(st["dests"]) + [_SEM, _SEM, _SEM]
    return pl.pallas_call(
        body, name=name, out_shape=[pltpu.HBM(a.shape, a.dtype) for a in lands],
        in_specs=in_specs + [pl.BlockSpec(memory_space=pl.ANY)], out_specs=[_HBM] * nl,
        input_output_aliases={i: i for i in range(nl)},
        compiler_params=pltpu.CompilerParams(has_side_effects=pltpu.SideEffectType.DATAFLOW_SIDE_EFFECTING),
    )(*lands, *flat, after)


def _mm_tn(a, b, name, out_dtype=F32):
    S, M = a.shape
    N = b.shape[1]
    tk = min(2048, S)
    tn = N if N <= 768 else (640 if N % 640 == 0 else 512)
    nk = S // tk

    def body(a_ref, b_ref, o_ref, acc):
        k = pl.program_id(1)

        @pl.when(k == 0)
        def _():
            acc[...] = _dot(a_ref[...], b_ref[...], TN)

        @pl.when(k > 0)
        def _():
            acc[...] += _dot(a_ref[...], b_ref[...], TN)

        @pl.when(k == nk - 1)
        def _():
            o_ref[...] = acc[...].astype(out_dtype)

    return pl.pallas_call(
        body, name=name, out_shape=jax.ShapeDtypeStruct((M, N), out_dtype), grid=(N // tn, nk),
        in_specs=[pl.BlockSpec((tk, M), lambda j, k: (k, 0)), pl.BlockSpec((tk, tn), lambda j, k: (k, j))],
        out_specs=pl.BlockSpec((M, tn), lambda j, k: (0, j)), scratch_shapes=[pltpu.VMEM((M, tn), F32)],
        compiler_params=_cp(("parallel", "arbitrary"), VMEM_BIG),
    )(a, b)


def _fwd_in(x, mod, g_pre, w_p):
    S = x.shape[0]
    tm = min(512, S)

    def body(x_ref, mod_ref, g_ref, w_ref, p_ref, h_ref):
        xv = x_ref[...]
        r = lax.rsqrt(jnp.mean(xv * xv, axis=-1, keepdims=True) + EPS)
        h = (((xv * r) * g_ref[...]) * (1.0 + mod_ref[1:2, :]) + mod_ref[0:1, :]).astype(BF16)
        h_ref[...] = h
        p_ref[...] = _dot(h, w_ref[...]).astype(BF16)

    return pl.pallas_call(
        body, name="fwd_in", out_shape=[jax.ShapeDtypeStruct((S, P_W), BF16), jax.ShapeDtypeStruct((S, D), BF16)],
        grid=(S // tm,),
        in_specs=[pl.BlockSpec((tm, D), lambda i: (i, 0)), _full((3, D)), _full((1, D)), _full((D, P_W))],
        out_specs=[pl.BlockSpec((tm, P_W), lambda i: (i, 0)), pl.BlockSpec((tm, D), lambda i: (i, 0))],
        compiler_params=_cp(("parallel",), VMEM_BIG),
    )(x, mod, g_pre, w_p)


def _swap16(v):
    lane = lax.broadcasted_iota(jnp.int32, v.shape, 1)
    return jnp.where((lane % 32) < 16, pltpu.roll(v, 112, 1), pltpu.roll(v, 16, 1))


def _rope(v, cos, sin):
    return v * cos + _swap16(v) * sin


def _rope_t(v, cos, sin):
    return v * cos - _swap16(v) * sin


def _head_mean(v):
    lo = lax.broadcasted_iota(jnp.int32, v.shape, 1) < 64
    m0 = jnp.sum(jnp.where(lo, v, 0.0), axis=-1, keepdims=True)
    m1 = jnp.sum(jnp.where(lo, 0.0, v), axis=-1, keepdims=True)
    return jnp.where(lo, m0, m1) * (1.0 / 64.0)


def _prep(p, cos, sin, qg, kg):
    S = p.shape[0]
    tm = min(512, S)

    def body(qa_ref, kv_ref, qr_ref, kr_ref, cos_ref, sin_ref, qg_ref, kg_ref, qt_ref, kh_ref, kt_ref, vh_ref, vta_ref, qr2_ref, kr2_ref):
        cos_v, sin_v = cos_ref[...], sin_ref[...]
        for g in range(4):
            xv = qa_ref[:, 128 * g:128 * g + 128].astype(F32)
            r = lax.rsqrt(_head_mean(xv * xv) + EPS)
            yt = (_rope((xv * r) * qg_ref[...], cos_v, sin_v) * (0.125 * LOG2E)).T
            qt_ref[2 * g] = yt[:DH].astype(BF16)
            qt_ref[2 * g + 1] = yt[DH:].astype(BF16)
        xv = kv_ref[:, :128].astype(F32)
        r = lax.rsqrt(_head_mean(xv * xv) + EPS)
        yv = _rope((xv * r) * kg_ref[...], cos_v, sin_v)
        kh_ref[0] = yv[:, :64].astype(BF16)
        kh_ref[1] = yv[:, 64:].astype(BF16)
        yt = yv.T
        kt_ref[0] = yt[:DH].astype(BF16)
        kt_ref[1] = yt[DH:].astype(BF16)
        vv = kv_ref[:, 128:].astype(F32)
        vh_ref[0] = vv[:, :64].astype(BF16)
        vh_ref[1] = vv[:, 64:].astype(BF16)
        vt = vv.T
        tail = (lax.broadcasted_iota(jnp.int32, (DHA - DH, tm), 0) == 0).astype(BF16)
        for kvh in range(2):
            vta_ref[kvh, 0:DH, :] = vt[DH * kvh:DH * kvh + DH].astype(BF16)
            vta_ref[kvh, DH:DHA, :] = tail
        for g in range(2):
            sl = slice(128 * g, 128 * g + 128)
            qr2_ref[:, sl] = _rope(qr_ref[:, sl].astype(F32), cos_v, sin_v)
            kr2_ref[:, sl] = _rope(kr_ref[:, sl].astype(F32), cos_v, sin_v) * 0.125

    hm = lambda n: pl.BlockSpec((n, tm, DH), lambda i: (0, i, 0))
    ht = lambda n, r: pl.BlockSpec((n, r, tm), lambda i: (0, 0, i))
    return pl.pallas_call(
        body, name="prep",
        out_shape=[jax.ShapeDtypeStruct((8, DH, S), BF16), jax.ShapeDtypeStruct((2, S, DH), BF16), jax.ShapeDtypeStruct((2, DH, S), BF16),
                   jax.ShapeDtypeStruct((2, S, DH), BF16), jax.ShapeDtypeStruct((2, DHA, S), BF16),
                   jax.ShapeDtypeStruct((S, 256), F32), jax.ShapeDtypeStruct((S, 256), F32)],
        grid=(S // tm,),
        in_specs=[pl.BlockSpec((tm, 512), lambda i: (i, O_QA // 512)), pl.BlockSpec((tm, 256), lambda i: (i, O_KA // 256)),
                  pl.BlockSpec((tm, 256), lambda i: (i, O_QR // 256)), pl.BlockSpec((tm, 256), lambda i: (i, O_KR // 256)),
                  pl.BlockSpec((tm, 128), lambda i: (i, 0)), pl.BlockSpec((tm, 128), lambda i: (i, 0)), _full((1, 128)), _full((1, 128))],
        out_specs=[ht(8, DH), hm(2), ht(2, DH), hm(2), ht(2, DHA), pl.BlockSpec((tm, 256), lambda i: (i, 0)), pl.BlockSpec((tm, 256), lambda i: (i, 0))],
        compiler_params=_cp(("parallel",)),
    )(p, p, p, p, cos, sin, qg, kg)


def _attn_fwd(qt, kh, vta):
    S = qt.shape[2]
    tq, tk = min(1024, S), min(512, S)
    nj = S // tk

    def body(q_ref, k_ref, v_ref, o_ref, ot_ref, lse_ref, m_s, acc_s):
        j = pl.program_id(1)

        @pl.when(j == 0)
        def _():
            m_s[...] = jnp.full_like(m_s, -jnp.inf)
            acc_s[...] = jnp.zeros_like(acc_s)

        m_all = m_s[...]
        st = {0: _dot(k_ref[0], q_ref[0])}
        m_new, acc_new = [], []
        for h in range(8):
            if h + 1 < 8:
                st[h + 1] = _dot(k_ref[(h + 1) // 4], q_ref[h + 1])
            m_old = m_all[h:h + 1, :]
            mn = jnp.maximum(m_old, jnp.max(st[h], axis=0, keepdims=True))
            pt = jnp.exp2(st[h] - mn).astype(BF16)
            acc_new.append(jnp.exp2(m_old - mn) * acc_s[h] + _dot(v_ref[h // 4], pt))
            m_new.append(mn)
            del st[h]
        for h in range(8):
            acc_s[h] = acc_new[h]
            m_s[h:h + 1, :] = m_new[h]

        @pl.when(j == nj - 1)
        def _():
            for h in range(8):
                ot = acc_s[h, 0:DH, :] / acc_s[h, DH:DH + 1, :]
                ot_ref[h] = ot
                o_ref[:, DH * h:DH * h + DH] = ot.T
                lse_ref[h // 4, h % 4:h % 4 + 1, :] = m_s[h:h + 1, :] + jnp.log2(acc_s[h, DH:DH + 1, :])

    return pl.pallas_call(
        body, name="attn_fwd",
        out_shape=[jax.ShapeDtypeStruct((S, 512), F32), jax.ShapeDtypeStruct((8, DH, S), F32), jax.ShapeDtypeStruct((2, 4, S), F32)],
        grid=(S // tq, nj),
        in_specs=[pl.BlockSpec((8, DH, tq), lambda i, j: (0, 0, i)), pl.BlockSpec((2, tk, DH), lambda i, j: (0, j, 0)),
                  pl.BlockSpec((2, DHA, tk), lambda i, j: (0, 0, j))],
        out_specs=[pl.BlockSpec((tq, 512), lambda i, j: (i, 0)), pl.BlockSpec((8, DH, tq), lambda i, j: (0, 0, i)),
                   pl.BlockSpec((2, 4, tq), lambda i, j: (0, 0, i))],
        scratch_shapes=[pltpu.VMEM((8, tq), F32), pltpu.VMEM((8, DHA, tq), F32)],
        compiler_params=_cp(("parallel", "arbitrary"), VMEM_BIG),
    )(qt, kh, vta)


def _ret_tables(wf, wb):
    C = CH

    def body(wf_ref, wb_ref, dc_ref, qdf_ref, qdb_ref, kdf_ref, kdb_ref, a_ref):
        def logsig(w):
            z = jnp.exp(-jnp.abs(w))
            u = 1.0 + z
            l1p = jnp.where(u == 1.0, z, jnp.log(u) * (z / jnp.where(u == 1.0, 1.0, u - 1.0)))
            return jnp.minimum(w, 0.0) - l1p

        lgf, lgb = logsig(wf_ref[...]), logsig(wb_ref[...])
        lane4 = lax.broadcasted_iota(jnp.int32, (1, 4), 1)

        def pick(lg, h):
            return jnp.sum(jnp.where(lane4 == h, lg, 0.0), axis=-1, keepdims=True)

        ii = lax.broadcasted_iota(jnp.int32, (C, C), 0).astype(F32)
        jj = lax.broadcasted_iota(jnp.int32, (C, C), 1).astype(F32)
        dif = ii - jj
        hd = lax.broadcasted_iota(jnp.int32, (C, 256), 1) // DH
        lf_l = jnp.zeros((C, 256), F32)
        lb_l = jnp.zeros((C, 256), F32)
        for h in range(HR):
            lf, lb = pick(lgf, h), pick(lgb, h)
            dc_ref[h] = jnp.where(dif >= 0, jnp.exp(lf * jnp.maximum(dif, 0.0)), jnp.exp(lb * jnp.maximum(-dif, 0.0)))
            lf_l = jnp.where(hd == h, lf, lf_l)
            lb_l = jnp.where(hd == h, lb, lb_l)
            a_ref[h:h + 1, :] = jnp.broadcast_to(jnp.exp(lf * C), (1, 128))
            a_ref[HR + h:HR + h + 1, :] = jnp.broadcast_to(jnp.exp(lb * C), (1, 128))
        ri = lax.broadcasted_iota(jnp.int32, (C, 256), 0).astype(F32)
        qdf_ref[...] = jnp.exp(lf_l * (ri + 1.0))
        qdb_ref[...] = jnp.exp(lb_l * (C - ri))
        kdf_ref[...] = jnp.exp(lf_l * (C - 1.0 - ri))
        kdb_ref[...] = jnp.exp(lb_l * ri)

    t = jax.ShapeDtypeStruct((C, 256), F32)
    return pl.pallas_call(body, name="ret_tables",
                          out_shape=[jax.ShapeDtypeStruct((HR, C, C), F32), t, t, t, t, jax.ShapeDtypeStruct((8, 128), F32)])(wf, wb)


def _ret_states(kr2, p, kdf, kdb, adec):
    S = kr2.shape[0]
    C, N = CH, S // CH
    G = _scan_group(N)
    NG = N // G

    def body(kf_ref, vf_ref, kb_ref, vb_ref, kdf_ref, kdb_ref, a_ref, rf_ref, rb_ref, sf, sb):
        @pl.when(pl.program_id(0) == 0)
        def _():
            sf[...] = jnp.zeros_like(sf)
            sb[...] = jnp.zeros_like(sb)

        kvf, kvb = [], []
        for u in range(G):
            rows = slice(C * u, C * u + C)
            kdfw = (kf_ref[rows, :] * kdf_ref[...]).astype(BF16)
            kdbw = (kb_ref[rows, :] * kdb_ref[...]).astype(BF16)
            vf, vb = vf_ref[rows, :].astype(BF16), vb_ref[rows, :].astype(BF16)
            kvf.append([_dot(kdfw[:, _ks(h)], vf[:, _vs(h)], TN) for h in range(HR)])
            kvb.append([_dot(kdbw[:, _ks(h)], vb[:, _vs(h)], TN) for h in range(HR)])
        for u in range(G):
            rf_ref[u] = sf[...]
            for h in range(HR):
                sf[h] = a_ref[h:h + 1, :] * sf[h] + kvf[u][h]
        for u in reversed(range(G)):
            rb_ref[u] = sb[...]
            for h in range(HR):
                sb[h] = a_ref[HR + h:HR + h + 1, :] * sb[h] + kvb[u][h]

    st = jax.ShapeDtypeStruct((N, HR, DH, DV), F32)
    return pl.pallas_call(
        body, name="ret_states", out_shape=[st, st], grid=(NG,),
        in_specs=[pl.BlockSpec((G * C, 256), lambda t: (t, 0)), pl.BlockSpec((G * C, 512), lambda t: (t, O_VR // 512)),
                  pl.BlockSpec((G * C, 256), lambda t: (NG - 1 - t, 0)), pl.BlockSpec((G * C, 512), lambda t: (NG - 1 - t, O_VR // 512)),
                  _full((C, 256)), _full((C, 256)), _full((8, 128))],
        out_specs=[pl.BlockSpec((G, HR, DH, DV), lambda t: (t, 0, 0, 0)), pl.BlockSpec((G, HR, DH, DV), lambda t: (NG - 1 - t, 0, 0, 0))],
        scratch_shapes=[pltpu.VMEM((HR, DH, DV), F32), pltpu.VMEM((HR, DH, DV), F32)],
        compiler_params=_cp(("arbitrary",)),
    )(kr2, p, kr2, p, kdf, kdb, adec)


def _scan_group(n):
    return 4 if n % 4 == 0 else (2 if n % 2 == 0 else 1)


def _ks(h):
    return slice(DH * h, DH * h + DH)


def _vs(h):
    return slice(DV * h, DV * h + DV)


def _ret_heads_fwd(qb, kb, vb, qfw, qbw, dc_ref, rf_ref, rb_ref):
    hs = range(HR)
    s = [_dot(qb[:, _ks(h)], kb[:, _ks(h)], NT) for h in hs]
    inter = [_dot(qfw[:, _ks(h)], rf_ref[0, h].astype(BF16)) + _dot(qbw[:, _ks(h)], rb_ref[0, h].astype(BF16)) for h in hs]
    sd = [s[h] * dc_ref[h] for h in hs]
    o = [_dot(sd[h].astype(BF16), vb[:, _vs(h)]) + inter[h] for h in hs]
    return sd, o


def _ret_out(qr2, kr2, p, rf, rb, dc, qdf, qdb, gn):
    S = qr2.shape[0]
    C, N = CH, S // CH

    def body(q_ref, k_ref, v_ref, z_ref, rf_ref, rb_ref, dc_ref, qdf_ref, qdb_ref, gn_ref, yr_ref):
        qv = q_ref[...]
        qb, kb, vb = qv.astype(BF16), k_ref[...].astype(BF16), v_ref[...].astype(BF16)
        qfw, qbw = (qv * qdf_ref[...]).astype(BF16), (qv * qdb_ref[...]).astype(BF16)
        _, o = _ret_heads_fwd(qb, kb, vb, qfw, qbw, dc_ref, rf_ref, rb_ref)
        for h in range(HR):
            vs = _vs(h)
            mu = jnp.mean(o[h], axis=-1, keepdims=True)
            var = jnp.mean(jnp.square(o[h] - mu), axis=-1, keepdims=True)
            on = (o[h] - mu) * lax.rsqrt(var + EPS)
            z = z_ref[:, vs].astype(F32)
            yr_ref[:, vs] = ((on * gn_ref[:, vs]) * (z * _sigmoid(z))).astype(BF16)

    return pl.pallas_call(
        body, name="ret_out", out_shape=jax.ShapeDtypeStruct((S, 512), BF16), grid=(N,),
        in_specs=[pl.BlockSpec((C, 256), lambda t: (t, 0)), pl.BlockSpec((C, 256), lambda t: (t, 0)),
                  pl.BlockSpec((C, 512), lambda t: (t, O_VR // 512)), pl.BlockSpec((C, 512), lambda t: (t, O_ZR // 512)),
                  pl.BlockSpec((1, HR, DH, DV), lambda t: (t, 0, 0, 0)), pl.BlockSpec((1, HR, DH, DV), lambda t: (t, 0, 0, 0)),
                  _full((HR, C, C)), _full((C, 256)), _full((C, 256)), _full((1, 512))],
        out_specs=pl.BlockSpec((C, 512), lambda t: (t, 0)),
        compiler_params=_cp(("parallel",)),
    )(qr2, kr2, p, p, rf, rb, dc, qdf, qdb, gn)


def _mid(x, tgt, mod, g_post, o_att, p, yr, w_pa, w_pr, w_out):
    S = x.shape[0]
    tm = min(256, S)

    def body(x_ref, t_ref, mod_ref, gp_ref, o_ref, za_ref, gl_ref, yr_ref, wpa_ref, wpr_ref, wout_ref,
             dout_ref, do_ref, dpm_ref, dyr_ref, mb_ref, dub_ref, yab_ref, dab_ref, drb_ref, sums_ref):
        @pl.when(pl.program_id(0) == 0)
        def _():
            sums_ref[...] = jnp.zeros_like(sums_ref)

        za = za_ref[...].astype(F32)
        sa = _sigmoid(za)
        sil = za * sa
        ov = o_ref[...]
        ya_b = (ov * sil).astype(BF16)
        yr_b = yr_ref[...]
        av = _dot(ya_b, wpa_ref[...])
        rv = _dot(yr_b, wpr_ref[...])
        ga = _sigmoid(gl_ref[:, :D].astype(F32))
        gr = _sigmoid(gl_ref[:, D:].astype(F32))
        mb = (ga * av + gr * rv).astype(BF16)
        u = _dot(mb, wout_ref[...])
        r2 = lax.rsqrt(jnp.mean(u * u, axis=-1, keepdims=True) + EPS)
        un = u * r2
        gp = gp_ref[...]
        yv = un * gp
        gate = mod_ref[2:3, :]
        err = (x_ref[...] + gate * yv) - t_ref[...]
        dout = err * (1.0 / D)
        dout_ref[...] = dout
        dy = dout * gate
        sums_ref[0:1, :] += jnp.sum(dout * yv, axis=0, keepdims=True)
        sums_ref[1:2, :] += jnp.sum(dy * un, axis=0, keepdims=True)
        sums_ref[2:3, :] += jnp.sum(err * err, axis=0, keepdims=True)
        dyg = dy * gp
        du_b = (r2 * (dyg - un * jnp.mean(dyg * un, axis=-1, keepdims=True))).astype(BF16)
        dm = _dot(du_b, wout_ref[...], NT)
        da_b = (dm * ga).astype(BF16)
        dr_b = (dm * gr).astype(BF16)
        dpm_ref[:, :D] = (dm * av * (ga * (1.0 - ga))).astype(BF16)
        dpm_ref[:, D:2 * D] = (dm * rv * (gr * (1.0 - gr))).astype(BF16)
        dya = _dot(da_b, wpa_ref[...], NT)
        dyr_ref[...] = _dot(dr_b, wpr_ref[...], NT)
        dov = dya * sil
        for g in range(4):
            dt = dov[:, 128 * g:128 * g + 128].T
            do_ref[2 * g] = dt[:DH].astype(BF16)
            do_ref[2 * g + 1] = dt[DH:].astype(BF16)
        dpm_ref[:, 2 * D:] = (dya * ov * (sa * (1.0 + za * (1.0 - sa)))).astype(BF16)
        mb_ref[...] = mb
        dub_ref[...] = du_b
        yab_ref[...] = ya_b
        dab_ref[...] = da_b
        drb_ref[...] = dr_b

    row = lambda w: pl.BlockSpec((tm, w), lambda i: (i, 0))
    sd = lambda w, dt: jax.ShapeDtypeStruct((S, w), dt)
    return pl.pallas_call(
        body, name="mid",
        out_shape=[sd(D, F32), jax.ShapeDtypeStruct((8, DH, S), BF16), sd(2560, BF16), sd(512, F32), sd(D, BF16), sd(D, BF16), sd(512, BF16),
                   sd(D, BF16), sd(D, BF16), jax.ShapeDtypeStruct((8, D), F32)],
        grid=(S // tm,),
        in_specs=[row(D), row(D), _full((3, D)), _full((1, D)), row(512), pl.BlockSpec((tm, 512), lambda i: (i, O_ZA // 512)),
                  pl.BlockSpec((tm, 2048), lambda i: (i, 0)), row(512), _full((512, D)), _full((512, D)), _full((D, D))],
        out_specs=[row(D), pl.BlockSpec((8, DH, tm), lambda i: (0, 0, i)), row(2560), row(512), row(D), row(D), row(512), row(D), row(D),
                   _full((8, D))],
        compiler_params=_cp(("arbitrary",), VMEM_BIG),
    )(x, tgt, mod, g_post, o_att, p, p, yr, w_pa, w_pr, w_out)


def _attn_bwd(qt, kh, kt, vh, dot_, ot, lse, xs):
    S = qt.shape[2]
    tq, tk = min(512, S), min(1024, S)

    def body(q_ref, k_ref, kt_ref, v_ref, do_ref, o_ref, lse_ref, dq_ref, dk_ref, dv_ref):
        j, i = pl.program_id(0), pl.program_id(1)
        cols = pl.ds(pl.multiple_of(i * tq, tq), tq)
        st = {0: _dot(k_ref[0], q_ref[0])}
        dpt = {0: _dot(v_ref[0], do_ref[0])}
        dk_acc, dv_acc, dqs = [None, None], [None, None], []
        for h in range(8):
            g = h // 4
            if h + 1 < 8:
                st[h + 1] = _dot(k_ref[(h + 1) // 4], q_ref[h + 1])
                dpt[h + 1] = _dot(v_ref[(h + 1) // 4], do_ref[h + 1])
            qt_h, dot_h = q_ref[h], do_ref[h]
            delta = jnp.sum(dot_h.astype(F32) * o_ref[h], axis=0, keepdims=True)
            pt = jnp.exp2(st[h] - lse_ref[g, h % 4:h % 4 + 1, :])
            dst = (pt * (dpt[h] - delta)).astype(BF16)
            dv_h = _dot(dot_h, pt.astype(BF16), NT)
            dk_h = _dot(qt_h, dst, NT)
            dqs.append(_dot(kt_ref[g], dst))
            dv_acc[g] = dv_h if dv_acc[g] is None else dv_acc[g] + dv_h
            dk_acc[g] = dk_h if dk_acc[g] is None else dk_acc[g] + dk_h
            del st[h], dpt[h]

        @pl.when(i == 0)
        def _():
            for g in range(2):
                dk_ref[g] = dk_acc[g]
                dv_ref[g] = dv_acc[g]

        @pl.when(i > 0)
        def _():
            for g in range(2):
                dk_ref[g] += dk_acc[g]
                dv_ref[g] += dv_acc[g]

        @pl.when(j == 0)
        def _():
            for h in range(8):
                dq_ref[h, :, cols] = dqs[h]

        @pl.when(j > 0)
        def _():
            for h in range(8):
                dq_ref[h, :, cols] += dqs[h]

    return _host_call(
        body, xs, name="attn_bwd",
        out_shape=[jax.ShapeDtypeStruct((8, DH, S), F32), jax.ShapeDtypeStruct((2, DH, S), F32), jax.ShapeDtypeStruct((2, DH, S), F32)],
        grid=(S // tk, S // tq),
        in_specs=[pl.BlockSpec((8, DH, tq), lambda j, i: (0, 0, i)), pl.BlockSpec((2, tk, DH), lambda j, i: (0, j, 0)),
                  pl.BlockSpec((2, DH, tk), lambda j, i: (0, 0, j)), pl.BlockSpec((2, tk, DH), lambda j, i: (0, j, 0)),
                  pl.BlockSpec((8, DH, tq), lambda j, i: (0, 0, i)), pl.BlockSpec((8, DH, tq), lambda j, i: (0, 0, i)),
                  pl.BlockSpec((2, 4, tq), lambda j, i: (0, 0, i))],
        out_specs=[pl.BlockSpec((8, DH, S), lambda j, i: (0, 0, 0)), pl.BlockSpec((2, DH, tk), lambda j, i: (0, 0, j)),
                   pl.BlockSpec((2, DH, tk), lambda j, i: (0, 0, j))],
        scratch_shapes=[], operands=(qt, kh, kt, vh, dot_, ot, lse),
        compiler_params=_cp(("arbitrary", "arbitrary"), VMEM_BIG),
    )


def _attn_prep_bwd(dqt, dkt, dvt, p, cos, sin, qg, kg):
    S = dqt.shape[2]
    tm = min(512, S)

    def body(dq_ref, dk_ref, dv_ref, qa_ref, ka_ref, cos_ref, sin_ref, qg_ref, kg_ref, dp_ref, gs_ref):
        @pl.when(pl.program_id(0) == 0)
        def _():
            gs_ref[...] = jnp.zeros_like(gs_ref)

        cos_v, sin_v = cos_ref[...], sin_ref[...]

        def pair(ref, a):
            return jnp.concatenate([ref[a], ref[a + 1]], axis=0).T

        def norm_bwd(dyv, xv, gv, row):
            r = lax.rsqrt(_head_mean(xv * xv) + EPS)
            xn = xv * r
            dxh = _rope_t(dyv, cos_v, sin_v)
            gs_ref[row:row + 1, :] += jnp.sum(dxh * xn, axis=0, keepdims=True)
            dg = dxh * gv
            return r * (dg - xn * _head_mean(dg * xn))

        for g in range(4):
            sl = slice(128 * g, 128 * g + 128)
            dp_ref[:, sl] = norm_bwd(pair(dq_ref, 2 * g) * 0.125, qa_ref[:, sl].astype(F32), qg_ref[...], 0).astype(BF16)
        dp_ref[:, 512:640] = norm_bwd(pair(dk_ref, 0) * LN2, ka_ref[...].astype(F32), kg_ref[...], 1).astype(BF16)
        dp_ref[:, 640:768] = pair(dv_ref, 0).astype(BF16)

    ht = lambda n: pl.BlockSpec((n, DH, tm), lambda i: (0, 0, i))
    return pl.pallas_call(
        body, name="attn_prep_bwd", out_shape=[jax.ShapeDtypeStruct((S, 768), BF16), jax.ShapeDtypeStruct((8, 128), F32)],
        grid=(S // tm,),
        in_specs=[ht(8), ht(2), ht(2),
                  pl.BlockSpec((tm, 512), lambda i: (i, O_QA // 512)), pl.BlockSpec((tm, 128), lambda i: (i, O_KA // 128)),
                  pl.BlockSpec((tm, 128), lambda i: (i, 0)), pl.BlockSpec((tm, 128), lambda i: (i, 0)), _full((1, 128)), _full((1, 128))],
        out_specs=[pl.BlockSpec((tm, 768), lambda i: (i, 0)), _full((8, 128))],
        compiler_params=_cp(("arbitrary",)),
    )(dqt, dkt, dvt, p, p, cos, sin, qg, kg)


def _ret_bwd_chunk(qr2, kr2, p, rf, rb, dc, qdf, qdb, gn, dyr, cos, sin, xs):
    S = qr2.shape[0]
    C, N = CH, S // CH

    def body(q_ref, k_ref, v_ref, z_ref, rf_ref, rb_ref, dc_ref, qdf_ref, qdb_ref, gn_ref, dyr_ref, cos_ref, sin_ref,
             dpa_ref, dk_ref, dv_ref, drf_ref, drb_ref, dgn_ref, dlg_ref, dqs):
        @pl.when(pl.program_id(0) == 0)
        def _():
            dgn_ref[...] = jnp.zeros_like(dgn_ref)
            dlg_ref[...] = jnp.zeros_like(dlg_ref)

        qv = q_ref[...]
        qb, kb, vb = qv.astype(BF16), k_ref[...].astype(BF16), v_ref[...].astype(BF16)
        qf32, qb32 = qv * qdf_ref[...], qv * qdb_ref[...]
        qfw, qbw = qf32.astype(BF16), qb32.astype(BF16)
        ii = lax.broadcasted_iota(jnp.int32, (C, C), 0).astype(F32)
        jj = lax.broadcasted_iota(jnp.int32, (C, C), 1).astype(F32)
        dif = ii - jj
        ri = lax.broadcasted_iota(jnp.int32, (C, 1), 0).astype(F32)
        hs = range(HR)
        sd, o = _ret_heads_fwd(qb, kb, vb, qfw, qbw, dc_ref, rf_ref, rb_ref)
        do_b = []
        for h in hs:
            vs = _vs(h)
            mu = jnp.mean(o[h], axis=-1, keepdims=True)
            rstd = lax.rsqrt(jnp.mean(jnp.square(o[h] - mu), axis=-1, keepdims=True) + EPS)
            on = (o[h] - mu) * rstd
            z = z_ref[:, vs].astype(F32)
            sz = _sigmoid(z)
            dy = dyr_ref[:, vs]
            gnv = gn_ref[:, vs]
            dpa_ref[:, 256 + DV * h:256 + DV * h + DV] = (dy * (on * gnv) * (sz * (1.0 + z * (1.0 - sz)))).astype(BF16)
            dys = dy * (z * sz)
            dgn_ref[:, vs] += jnp.sum(dys * on, axis=0, keepdims=True)
            don = dys * gnv
            do = rstd * (don - jnp.mean(don, axis=-1, keepdims=True) - on * jnp.mean(don * on, axis=-1, keepdims=True))
            do_b.append(do.astype(BF16))
        dpm = [_dot(do_b[h], vb[:, _vs(h)], NT) for h in hs]
        dqf = [_dot(do_b[h], rf_ref[0, h].astype(BF16), NT) for h in hs]
        dqb = [_dot(do_b[h], rb_ref[0, h].astype(BF16), NT) for h in hs]
        for h in hs:
            dv_ref[:, _vs(h)] = _dot(sd[h].astype(BF16), do_b[h], TN)
            drf_ref[0, h] = _dot(qfw[:, _ks(h)], do_b[h], TN)
            drb_ref[0, h] = _dot(qbw[:, _ks(h)], do_b[h], TN)
        dsd = [(dpm[h] * dc_ref[h]).astype(BF16) for h in hs]
        for h in hs:
            ks = _ks(h)
            dqs[:, ks] = _dot(dsd[h], kb[:, ks]) + dqf[h] * qdf_ref[:, ks] + dqb[h] * qdb_ref[:, ks]
            dk_ref[:, ks] = _dot(dsd[h], qb[:, ks], TN)
        for h in hs:
            ks = _ks(h)
            e = dpm[h] * sd[h]
            lf = _sum11(e * jnp.maximum(dif, 0.0)) + _sum11(jnp.sum(qf32[:, ks] * dqf[h], axis=-1, keepdims=True) * (ri + 1.0))
            lb = _sum11(e * jnp.maximum(-dif, 0.0)) + _sum11(jnp.sum(qb32[:, ks] * dqb[h], axis=-1, keepdims=True) * (C - ri))
            dlg_ref[h:h + 1, :] += jnp.broadcast_to(lf, (1, 128))
            dlg_ref[HR + h:HR + h + 1, :] += jnp.broadcast_to(lb, (1, 128))
        cos_v, sin_v = cos_ref[...], sin_ref[...]
        for g in range(2):
            sl = slice(128 * g, 128 * g + 128)
            dpa_ref[:, sl] = _rope_t(dqs[:, sl], cos_v, sin_v).astype(BF16)

    st = jax.ShapeDtypeStruct((N, HR, DH, DV), F32)
    stb = lambda: pl.BlockSpec((1, HR, DH, DV), lambda t: (t, 0, 0, 0))
    return _host_call(
        body, xs, name="ret_bwd_chunk",
        out_shape=[jax.ShapeDtypeStruct((S, 768), BF16), jax.ShapeDtypeStruct((S, 256), F32), jax.ShapeDtypeStruct((S, 512), F32), st, st,
                   jax.ShapeDtypeStruct((1, 512), F32), jax.ShapeDtypeStruct((8, 128), F32)],
        grid=(N,),
        in_specs=[pl.BlockSpec((C, 256), lambda t: (t, 0)), pl.BlockSpec((C, 256), lambda t: (t, 0)),
                  pl.BlockSpec((C, 512), lambda t: (t, O_VR // 512)), pl.BlockSpec((C, 512), lambda t: (t, O_ZR // 512)),
                  stb(), stb(), _full((HR, C, C)), _full((C, 256)), _full((C, 256)), _full((1, 512)),
                  pl.BlockSpec((C, 512), lambda t: (t, 0)), pl.BlockSpec((C, 128), lambda t: (t, 0)), pl.BlockSpec((C, 128), lambda t: (t, 0))],
        out_specs=[pl.BlockSpec((C, 768), lambda t: (t, 0)), pl.BlockSpec((C, 256), lambda t: (t, 0)), pl.BlockSpec((C, 512), lambda t: (t, 0)),
                   stb(), stb(), _full((1, 512)), _full((8, 128))],
        scratch_shapes=[pltpu.VMEM((C, 256), F32)], operands=(qr2, kr2, p, p, rf, rb, dc, qdf, qdb, gn, dyr, cos, sin),
        compiler_params=_cp(("arbitrary",)),
    )


def _ret_bwd_scan(kr2, p, rf, rb, drf, drb, kdf, kdb, adec):
    S = kr2.shape[0]
    C, N = CH, S // CH
    G = _scan_group(N)
    NG = N // G

    def body(kf_ref, vf_ref, kb_ref, vb_ref, rf_ref, rb_ref, drf_ref, drb_ref, kdf_ref, kdb_ref, a_ref,
             dkf_ref, dkb_ref, dvf_ref, dvb_ref, dlg_ref, gf, gb):
        @pl.when(pl.program_id(0) == 0)
        def _():
            gf[...] = jnp.zeros_like(gf)
            gb[...] = jnp.zeros_like(gb)
            dlg_ref[...] = jnp.zeros_like(dlg_ref)

        ri = lax.broadcasted_iota(jnp.int32, (C, 1), 0).astype(F32)

        def one(k_ref, v_ref, r_ref, dr_ref, kd_ref, g_s, dk_ref, dv_ref, row0, wexp, order):
            g = [g_s[h] for h in range(HR)]
            lgs = [jnp.zeros((1, 1), F32) for _ in range(HR)]
            for u in order:
                rows = slice(C * u, C * u + C)
                kd32 = k_ref[rows, :] * kd_ref[...]
                kdw = kd32.astype(BF16)
                vb = v_ref[rows, :].astype(BF16)
                for h in range(HR):
                    ks, vs = _ks(h), _vs(h)
                    g_b = g[h].astype(BF16)
                    dkd = _dot(vb[:, vs], g_b, NT)
                    dk_ref[rows, ks] = dkd * kd_ref[:, ks]
                    dv_ref[rows, vs] = _dot(kdw[:, ks], g_b)
                    av = a_ref[row0 + h:row0 + h + 1, :]
                    lgs[h] = lgs[h] + (_sum11(jnp.sum(kd32[:, ks] * dkd, axis=-1, keepdims=True) * wexp)
                                       + C * av[:, 0:1] * _sum11(r_ref[u, h] * g[h]))
                    g[h] = dr_ref[u, h] + av * g[h]
            for h in range(HR):
                g_s[h] = g[h]
                dlg_ref[row0 + h:row0 + h + 1, :] += jnp.broadcast_to(lgs[h], (1, 128))

        one(kf_ref, vf_ref, rf_ref, drf_ref, kdf_ref, gf, dkf_ref, dvf_ref, 0, C - 1.0 - ri, list(reversed(range(G))))
        one(kb_ref, vb_ref, rb_ref, drb_ref, kdb_ref, gb, dkb_ref, dvb_ref, HR, ri, list(range(G)))

    fwd = lambda w, off=0: pl.BlockSpec((G * C, w), lambda t: (NG - 1 - t, off))
    bwd = lambda w, off=0: pl.BlockSpec((G * C, w), lambda t: (t, off))
    stf = lambda: pl.BlockSpec((G, HR, DH, DV), lambda t: (NG - 1 - t, 0, 0, 0))
    stb = lambda: pl.BlockSpec((G, HR, DH, DV), lambda t: (t, 0, 0, 0))
    return pl.pallas_call(
        body, name="ret_bwd_scan",
        out_shape=[jax.ShapeDtypeStruct((S, 256), F32), jax.ShapeDtypeStruct((S, 256), F32), jax.ShapeDtypeStruct((S, 512), F32),
                   jax.ShapeDtypeStruct((S, 512), F32), jax.ShapeDtypeStruct((8, 128), F32)],
        grid=(NG,),
        in_specs=[fwd(256), fwd(512, O_VR // 512), bwd(256), bwd(512, O_VR // 512), stf(), stb(), stf(), stb(),
                  _full((C, 256)), _full((C, 256)), _full((8, 128))],
        out_specs=[fwd(256), bwd(256), fwd(512), bwd(512), _full((8, 128))],
        scratch_shapes=[pltpu.VMEM((HR, DH, DV), F32), pltpu.VMEM((HR, DH, DV), F32)],
        compiler_params=_cp(("arbitrary",)),
    )(kr2, p, kr2, p, rf, rb, drf, drb, kdf, kdb, adec)


def _ret_bwd_final(dk_i, dkf, dkb, dv_i, dvf, dvb, cos, sin):
    S = dk_i.shape[0]
    tm = min(512, S)

    def body(a_ref, b_ref, c_ref, d_ref, e_ref, f_ref, cos_ref, sin_ref, o_ref):
        o_ref[:, :512] = (d_ref[...] + e_ref[...] + f_ref[...]).astype(BF16)
        cos_v, sin_v = cos_ref[...], sin_ref[...]
        for g in range(2):
            sl = slice(128 * g, 128 * g + 128)
            dk = a_ref[:, sl] + b_ref[:, sl] + c_ref[:, sl]
            o_ref[:, 512 + 128 * g:512 + 128 * g + 128] = (_rope_t(dk, cos_v, sin_v) * 0.125).astype(BF16)

    row = lambda w: pl.BlockSpec((tm, w), lambda i: (i, 0))
    return pl.pallas_call(
        body, name="ret_bwd_final", out_shape=jax.ShapeDtypeStruct((S, 768), BF16), grid=(S // tm,),
        in_specs=[row(256), row(256), row(256), row(512), row(512), row(512), row(128), row(128)], out_specs=row(768),
        compiler_params=_cp(("parallel",)),
    )(dk_i, dkf, dkb, dv_i, dvf, dvb, cos, sin)


def _bwd_in(dpm, dpa, dpra, dprb, w_p, x, dout, mod, g_pre, xs):
    S = x.shape[0]
    tm = min(256, S)

    def body(a_ref, b_ref, c_ref, d_ref, w_ref, x_ref, dout_ref, mod_ref, g_ref, gx_ref, sums_ref):
        @pl.when(pl.program_id(0) == 0)
        def _():
            sums_ref[...] = jnp.zeros_like(sums_ref)

        dh = (_dot(a_ref[...], w_ref[:, :O_QA], NT) + _dot(b_ref[...], w_ref[:, O_QA:O_QR], NT)
              + _dot(c_ref[...], w_ref[:, O_QR:O_VR], NT) + _dot(d_ref[...], w_ref[:, O_VR:], NT))
        xv = x_ref[...]
        r = lax.rsqrt(jnp.mean(xv * xv, axis=-1, keepdims=True) + EPS)
        xn = xv * r
        gv = g_ref[...]
        sc1 = 1.0 + mod_ref[1:2, :]
        sums_ref[0:1, :] += jnp.sum(dh, axis=0, keepdims=True)
        sums_ref[1:2, :] += jnp.sum(dh * (xn * gv), axis=0, keepdims=True)
        sums_ref[2:3, :] += jnp.sum(dh * xn, axis=0, keepdims=True) * sc1
        dxn = dh * (gv * sc1)
        gx_ref[...] = dout_ref[...] + r * (dxn - xn * jnp.mean(dxn * xn, axis=-1, keepdims=True))

    row = lambda w: pl.BlockSpec((tm, w), lambda i: (i, 0))
    return _host_call(
        body, xs, name="bwd_in", out_shape=[jax.ShapeDtypeStruct((S, D), F32), jax.ShapeDtypeStruct((8, D), F32)], grid=(S // tm,),
        in_specs=[row(2560), row(768), row(768), row(768), _full((D, P_W)), row(D), row(D), _full((3, D)), _full((1, D))],
        out_specs=[row(D), _full((8, D))], scratch_shapes=[], operands=(dpm, dpa, dpra, dprb, w_p, x, dout, mod, g_pre),
        compiler_params=_cp(("arbitrary",), VMEM_BIG),
    )


SMALL = ("b_ada", "g_pre", "qn_g", "kn_g", "w_dec_f", "w_dec_b", "gn_g", "g_post")


def _small_update(gathered, wmv):
    ns = len(SMALL)

    def body(*refs):
        gin_ref, gmid_ref, ggn_ref, gatt_ref, gl1_ref, gl2_ref = refs[:6]
        wmv_refs = refs[6:6 + 3 * ns]
        loss_ref = refs[6 + 3 * ns]
        out_refs = refs[7 + 3 * ns:]

        def dsum(ref, r=None):
            rows = slice(None) if r is None else slice(r, r + 1)
            acc = ref[0, rows, :]
            for d in range(1, NDEV):
                acc = acc + ref[d, rows, :]
            return acc

        s_lg = dsum(gl1_ref) + dsum(gl2_ref)
        loss_ref[...] = (0.5 / D) * jnp.sum(dsum(gmid_ref, 2), axis=-1, keepdims=True)
        eye = lax.broadcasted_iota(jnp.int32, (8, 128), 0) == lax.broadcasted_iota(jnp.int32, (8, 128), 1)
        dlg = jnp.sum(jnp.where(eye, s_lg, 0.0), axis=0, keepdims=True)
        w_f, w_b = wmv_refs[3 * SMALL.index("w_dec_f")][...], wmv_refs[3 * SMALL.index("w_dec_b")][...]
        s_q, s_k = dsum(gatt_ref, 0), dsum(gatt_ref, 1)
        grads = dict(
            b_ada=jnp.concatenate([dsum(gin_ref, 0), dsum(gin_ref, 1), dsum(gmid_ref, 0)], axis=1),
            g_pre=dsum(gin_ref, 2), g_post=dsum(gmid_ref, 1), gn_g=dsum(ggn_ref),
            qn_g=s_q[:, :DH] + s_q[:, DH:], kn_g=s_k[:, :DH] + s_k[:, DH:],
            w_dec_f=dlg[:, 0:HR] * _sigmoid(-w_f), w_dec_b=dlg[:, HR:2 * HR] * _sigmoid(-w_b))
        for i, nme in enumerate(SMALL):
            g = grads[nme]
            w_ref, m_ref, v_ref = wmv_refs[3 * i:3 * i + 3]
            g_ref, d_ref, nm_ref, nv_ref = out_refs[4 * i:4 * i + 4]
            g_ref[...] = g
            m2 = ADAM_B1 * m_ref[...] + (1.0 - ADAM_B1) * g
            v2 = ADAM_B2 * v_ref[...] + (1.0 - ADAM_B2) * jnp.square(g)
            m_hat = m2 / (1.0 - ADAM_B1 ** ADAM_STEP)
            v_hat = v2 / (1.0 - ADAM_B2 ** ADAM_STEP)
            d_ref[...] = -ADAM_LR * (m_hat / (jnp.sqrt(v_hat) + ADAM_EPS) + ADAM_WD * w_ref[...])
            nm_ref[...] = m2
            nv_ref[...] = v2

    out_shape = [jax.ShapeDtypeStruct((1, 1), F32)]
    for i in range(ns):
        out_shape += [jax.ShapeDtypeStruct(wmv[3 * i].shape, F32)] * 4
    return pl.pallas_call(body, name="small_update", out_shape=out_shape)(*gathered, *wmv)


def _adamw(parts, w, m, v, name):
    n, R, L = parts.shape
    tr = 256 if (R % 256 == 0 and R > 256) else R

    def body(p_ref, w_ref, m_ref, v_ref, g_ref, d_ref, nm_ref, nv_ref):
        g = p_ref[0].astype(F32)
        for k in range(1, n):
            g = g + p_ref[k].astype(F32)
        g_ref[...] = g
        m2 = ADAM_B1 * m_ref[...] + (1.0 - ADAM_B1) * g
        v2 = ADAM_B2 * v_ref[...] + (1.0 - ADAM_B2) * jnp.square(g)
        m_hat = m2 / (1.0 - ADAM_B1 ** ADAM_STEP)
        v_hat = v2 / (1.0 - ADAM_B2 ** ADAM_STEP)
        d_ref[...] = -ADAM_LR * (m_hat / (jnp.sqrt(v_hat) + ADAM_EPS) + ADAM_WD * w_ref[...])
        nm_ref[...] = m2
        nv_ref[...] = v2

    blk = pl.BlockSpec((tr, L), lambda i: (i, 0))
    o = jax.ShapeDtypeStruct((R, L), F32)
    return pl.pallas_call(
        body, name=name, out_shape=[o, o, o, o], grid=(R // tr,),
        in_specs=[pl.BlockSpec((n, tr, L), lambda i: (0, i, 0)), blk, blk, blk], out_specs=[blk, blk, blk, blk],
        compiler_params=_cp(("parallel",), VMEM_BIG),
    )(parts, w, m, v)


def _rope_tables(S):
    f = np.float32
    t = np.arange(S)
    row, col = (t // 64).astype(f), (t % 64).astype(f)
    half = DH // 2
    inv = np.power(f(ROPE_THETA), -np.arange(0, half, 2, dtype=f) / f(half)).astype(f)
    ar, ac = (row[:, None] * inv[None, :]).astype(f), (col[:, None] * inv[None, :]).astype(f)
    cos64 = np.concatenate([np.cos(ar), np.cos(ar), np.cos(ac), np.cos(ac)], axis=1).astype(f)
    sin64 = np.concatenate([-np.sin(ar), np.sin(ar), -np.sin(ac), np.sin(ac)], axis=1).astype(f)
    return jnp.asarray(np.tile(cos64, (1, 2))), jnp.asarray(np.tile(sin64, (1, 2)))


def _to_p_order(w_orig):
    return jnp.concatenate([w_orig[:, ORIG[n][0]:ORIG[n][1]] for n in P_ORDER], axis=1)


def _pad_lanes(v, n):
    return jnp.pad(v, ((0, 0), (0, n - v.shape[1])))


def kernel(x, c, w_ada, b_ada, g_pre, w_in, qn_g, kn_g, w_dec_f, w_dec_b, gn_g, w_pa, w_pr, w_out, g_post, loss_target, m_w_ada, m_b_ada, m_g_pre, m_w_in, m_qn_g, m_kn_g, m_w_dec_f, m_w_dec_b, m_gn_g, m_w_pa, m_w_pr, m_w_out, m_g_post, v_w_ada, v_b_ada, v_g_pre, v_w_in, v_qn_g, v_kn_g, v_w_dec_f, v_w_dec_b, v_gn_g, v_w_pa, v_w_pr, v_w_out, v_g_post):
    S = x.shape[1]
    me = 4 * lax.axis_index("x") + 2 * lax.axis_index("y") + lax.axis_index("c")
    xs, tgt = x[0], loss_target[0]
    ncol_ada = w_ada.shape[2]
    ncol_in = w_in.shape[2]

    b_ada_s = lax.dynamic_slice(b_ada, (0, me * ncol_ada), (1, ncol_ada))
    mod_all, c_act, (wg_in,) = _prologue(jnp.pad(c, ((0, 7), (0, 0))), w_ada[0], b_ada_s, [w_in[0].astype(BF16)])
    mod = lax.dynamic_index_in_dim(mod_all, me, axis=1, keepdims=False).reshape(3, D)
    w_p = _to_p_order(wg_in.transpose(1, 0, 2).reshape(D, NDEV * ncol_in))
    all_dev = tuple(range(NDEV))
    st_w, tok_w = _xchg_start([(w_pa[0].astype(BF16)[None], all_dev), (w_pr[0].astype(BF16)[None], all_dev),
                               (w_out[0].astype(BF16)[None], all_dev)], "wgather_start")

    cos, sin = _rope_tables(S)
    qg, kg = jnp.tile(qn_g, (1, 2)), jnp.tile(kn_g, (1, 2))

    p, h = _fwd_in(xs, mod, g_pre + tok_w[0:1, 0:1], w_p)
    qt, kh, kt, vh, vta, qr2, kr2 = _prep(p, cos, sin, qg, kg)
    o_att, o_t, lse = _attn_fwd(qt, kh, vta)
    dc, qdf, qdb, kdf, kdb, adec = _ret_tables(w_dec_f, w_dec_b)
    rf, rb = _ret_states(kr2, p, kdf, kdb, adec)
    yr = _ret_out(qr2, kr2, p, rf, rb, dc, qdf, qdb, gn_g)
    wg_pa, wg_pr, wg_out = _xchg_wait([st_w], st_w["lands"], [[0, 1, 2]], yr, "wgather_wait")
    w_pa_f = wg_pa.transpose(1, 0, 2).reshape(512, D)
    w_pr_f = wg_pr.transpose(1, 0, 2).reshape(512, D)
    w_out_f = wg_out.reshape(D, D)

    dout, do, dpm, dyr, mb, dub, yab, dab, drb_, sums_mid = _mid(xs, tgt, mod, g_post, o_att, p, yr, w_pa_f, w_pr_f, w_out_f)
    gw_out = _mm_tn(mb, dub, "gw_out", BF16)
    gw_pa = _mm_tn(yab, dab, "gw_pa", BF16)
    gw_pr = _mm_tn(yr, drb_, "gw_pr", BF16)
    gi_m = _mm_tn(h, dpm, "gw_in_mid", BF16)

    def shards(cols, nd):
        return cols.reshape(D, nd, ncol_in).transpose(1, 0, 2)

    st_a, tok_a = _xchg_start([
        (gw_out.reshape(NDEV, 128, D), all_dev),
        (gw_pa.reshape(512, NDEV, 128).transpose(1, 0, 2), all_dev),
        (gw_pr.reshape(512, NDEV, 128).transpose(1, 0, 2), all_dev),
        (shards(gi_m[:, 224:2048], 3), (5, 6, 7))], "xchg_start_a",
        lands=[None, None, None, jnp.zeros((NDEV, D, ncol_in), BF16)])
    (dqt, dkt, dvt), _ = _attn_bwd(qt, kh, kt, vh, do, o_t, lse + tok_a[0, 0], [])
    dpa, gs_att = _attn_prep_bwd(dqt, dkt, dvt, p, cos, sin, qg, kg)
    gi_a = _mm_tn(h, dpa, "gw_in_att", BF16)
    st_b, tok_b = _xchg_start([(shards(jnp.concatenate([gi_a, gi_m[:, 2048:2496]], axis=1), 2), (0, 1))], "xchg_start_b",
                              lands=[st_a["lands"][3]])
    (dpra, dk_i, dv_i, drf, drb, dgn, dlg1), _ = _ret_bwd_chunk(qr2, kr2, p, rf, rb, dc, qdf, qdb, gn_g + tok_b[0:1, 0:1], dyr, cos, sin, [])
    dkf, dkb, dvf, dvb, dlg2 = _ret_bwd_scan(kr2, p, rf, rb, drf, drb, kdf, kdb, adec)
    dprb = _ret_bwd_final(dk_i, dkf, dkb, dv_i, dvf, dvb, cos, sin)
    gi_ra = _mm_tn(h, dpra, "gw_in_reta", BF16)
    gi_rb = _mm_tn(h, dprb, "gw_in_retb", BF16)
    chip_c = _pair_reduce(shards(jnp.concatenate([gi_m[:, 2496:2560], gi_ra[:, :256], gi_rb[:, 512:768], gi_rb[:, :512],
                                                  gi_ra[:, 256:768], gi_m[:, :224]], axis=1), 3), (2, 3, 4), "pair_reduce_c")
    st_c, tok_c = _xchg_start([(chip_c, (2, 3, 4, "same core"))], "xchg_start_c", lands=[st_b["lands"][0]])
    (grad_x, sums_in), _ = _bwd_in(dpm, dpa, dpra, dprb, w_p, xs, dout, mod, g_pre + tok_c[0:1, 0:1], [])

    gathered = _small_allgather([sums_in, sums_mid, dgn, gs_att, dlg1, dlg2], "ag_small")
    given = dict(b_ada=(b_ada, m_b_ada, v_b_ada), g_pre=(g_pre, m_g_pre, v_g_pre), qn_g=(qn_g, m_qn_g, v_qn_g), kn_g=(kn_g, m_kn_g, v_kn_g),
                 w_dec_f=(w_dec_f, m_w_dec_f, v_w_dec_f), w_dec_b=(w_dec_b, m_w_dec_b, v_w_dec_b), gn_g=(gn_g, m_gn_g, v_gn_g),
                 g_post=(g_post, m_g_post, v_g_post))
    small = _small_update(gathered, [a for nme in SMALL for a in given[nme]])
    loss = small[0][0, 0]

    g_in_all, g_mid_all = gathered[0], gathered[1]
    dmod_all = lax.dynamic_slice(jnp.concatenate([g_in_all[:, 0, :], g_in_all[:, 1, :], g_mid_all[:, 0, :]], axis=1),
                                 (0, me * ncol_ada), (NDEV, ncol_ada))
    g_ada = _mm_tn(c_act, jnp.pad(dmod_all, ((0, 8), (0, 0))).astype(BF16), "gw_ada")

    ada = _adamw(g_ada[None], w_ada[0], m_w_ada[0], v_w_ada[0], "adamw_ada")
    rs_out, rs_pa, rs_pr, rs_in = _xchg_wait([st_a, st_b, st_c], list(st_a["lands"][:3]) + [st_c["lands"][0]],
                                             [[0, 1, 2, 3], [3], [3]], ada[1], "xchg_wait")
    res = dict(
        w_ada=ada,
        w_in=_adamw(rs_in, w_in[0], m_w_in[0], v_w_in[0], "adamw_in"),
        w_pa=_adamw(rs_pa, w_pa[0], m_w_pa[0], v_w_pa[0], "adamw_pa"),
        w_pr=_adamw(rs_pr, w_pr[0], m_w_pr[0], v_w_pr[0], "adamw_pr"),
        w_out=_adamw(rs_out, w_out[0], m_w_out[0], v_w_out[0], "adamw_out"),
    )
    names = ["w_ada", "b_ada", "g_pre", "w_in", "qn_g", "kn_g", "w_dec_f", "w_dec_b", "gn_g", "w_pa", "w_pr", "w_out", "g_post"]
    outs = [[], [], [], []]
    for nme in names:
        for q in range(4):
            if nme in res:
                outs[q].append(res[nme][q][None])
            else:
                outs[q].append(small[1 + 4 * SMALL.index(nme) + q])
    return (loss, grad_x[None], *outs[0], *outs[1], *outs[2], *outs[3])
```

```python
import jax
import jax.numpy as jnp
import numpy as np
from jax import lax
from jax.experimental import pallas as pl
from jax.experimental.pallas import tpu as pltpu

F32, BF16 = jnp.float32, jnp.bfloat16
D = 1024
DH = 64
DHA = 80
DV = 128
LOG2E = 1.4426950408889634
LN2 = 0.6931471805599453
HR = 4
CH = 128
EPS = 1e-6
ROPE_THETA = 10000.0
NDEV = 8
O_GL, O_ZA, O_QA, O_KA, O_VA, O_QR, O_ZR, O_VR, O_KR, P_W = 0, 2048, 2560, 3072, 3200, 3328, 3584, 4096, 4608, 4864
ORIG = dict(qa=(0, 512), ka=(512, 640), va=(640, 768), za=(768, 1280), qr=(1280, 1536), kr=(1536, 1792),
            vr=(1792, 2304), zr=(2304, 2816), gl=(2816, 4864))
P_ORDER = ("gl", "za", "qa", "ka", "va", "qr", "zr", "vr", "kr")
ADAM_LR, ADAM_B1, ADAM_B2, ADAM_EPS, ADAM_WD, ADAM_STEP = 0.001, 0.9, 0.999, 1e-08, 0.01, 10
VMEM_BIG = 56 * 1024 * 1024
MESH = pl.DeviceIdType.MESH

NT = (((1,), (1,)), ((), ()))
TN = (((0,), (0,)), ((), ()))


def _dot(a, b, dims=None):
    if dims is None:
        return jnp.dot(a, b, preferred_element_type=F32)
    return lax.dot_general(a, b, dims, preferred_element_type=F32)


def _cp(sem=None, vmem=None):
    kw = {}
    if sem is not None:
        kw["dimension_semantics"] = sem
    if vmem is not None:
        kw["vmem_limit_bytes"] = vmem
    return pltpu.CompilerParams(**kw)


def _sigmoid(z):
    return 1.0 / (1.0 + jnp.exp(-z))


def _sum11(m):
    return jnp.sum(jnp.sum(m, axis=-1, keepdims=True), axis=0, keepdims=True)


def _full(shape):
    n = len(shape)
    return pl.BlockSpec(shape, lambda *_: (0,) * n)


def _my_pos():
    return lax.axis_index("x"), lax.axis_index("y"), lax.axis_index("c")


def _peer(k, x, y, c):
    return ((1 - x) if k & 4 else x, (1 - y) if k & 2 else y, (1 - c) if k & 1 else c)


def _small_allgather(vs, name):
    n = len(vs)

    def body(*refs):
        v_refs, out_refs = refs[:n], refs[n:2 * n]
        send_sems, recv_sems = refs[2 * n:]
        x, y, c = _my_pos()
        me = 4 * x + 2 * y + c
        cps = []
        for a in range(n):
            out_refs[a][me] = v_refs[a][...]
            for k in range(1, NDEV):
                cp = pltpu.make_async_remote_copy(src_ref=v_refs[a], dst_ref=out_refs[a].at[me], send_sem=send_sems.at[a, k - 1],
                                                  recv_sem=recv_sems.at[a, k - 1], device_id=_peer(k, x, y, c), device_id_type=MESH)
                cp.start()
                cps.append(cp)
        for cp in cps:
            cp.wait()

    vm = pl.BlockSpec(memory_space=pltpu.VMEM)
    return pl.pallas_call(
        body, name=name, out_shape=[jax.ShapeDtypeStruct((NDEV,) + v.shape, v.dtype) for v in vs],
        in_specs=[vm] * n, out_specs=[vm] * n,
        scratch_shapes=[pltpu.SemaphoreType.DMA((n, NDEV - 1)), pltpu.SemaphoreType.DMA((n, NDEV - 1))],
    )(*vs)


def _prologue(c8, w_ada_s, b_ada_s, arrs):
    n = len(arrs)
    ncol = w_ada_s.shape[1]

    def body(*refs):
        c_ref, wa_ref, ba_ref = refs[:3]
        ins = refs[3:3 + n]
        mod_ref, cact_ref = refs[3 + n:5 + n]
        outs = refs[5 + n:5 + 2 * n]
        call_ref, send_sems, recv_sems, local_sems, s_send, s_recv = refs[5 + 2 * n:]
        x, y, c = _my_pos()
        me, sibling = (x, y, c), (x, y, 1 - c)
        chips = [(1 - x, y), (x, 1 - y), (1 - x, 1 - y)]
        me_i = 4 * x + 2 * y + c

        def small_gather(src_ref, dst_ref, row):
            cps = []
            for k in range(1, NDEV):
                cp = pltpu.make_async_remote_copy(src_ref=src_ref, dst_ref=dst_ref.at[me_i], send_sem=s_send.at[row, k - 1],
                                                  recv_sem=s_recv.at[row, k - 1], device_id=_peer(k, x, y, c), device_id_type=MESH)
                cp.start()
                cps.append(cp)
            return cps

        def blk(a, px, py, pc):
            return outs[a].at[4 * px + 2 * py + pc]

        def copy(a, k, block, to, src=None):
            return pltpu.make_async_remote_copy(src_ref=blk(a, *block) if src is None else src, dst_ref=blk(a, *block),
                                                send_sem=send_sems.at[a, k], recv_sem=recv_sems.at[a, k], device_id=to, device_id_type=MESH)

        call_ref[me_i] = c_ref[...]
        for cp in small_gather(c_ref, call_ref, 0):
            cp.wait()

        local, sent = [], []
        for a in range(n):
            mine = pltpu.make_async_copy(ins[a], blk(a, *me), local_sems.at[a])
            mine.start()
            local.append(mine)
            first = [copy(a, 0, me, sibling, src=ins[a])] + [copy(a, 1 + j, me, (*chip, c), src=ins[a]) for j, chip in enumerate(chips)]
            for cp in first:
                cp.start()
            sent += first

        cv = call_ref[:, 0, :]
        ca = jnp.concatenate([cv * _sigmoid(cv), jnp.zeros_like(cv)], axis=0).astype(BF16)
        cact_ref[...] = ca
        mod_ref[me_i] = (_dot(ca, wa_ref[...].astype(BF16)) + ba_ref[...])[:8]
        mod_copies = small_gather(mod_ref.at[me_i], mod_ref, 1)

        for j, chip in enumerate(chips):
            for a in range(n):
                copy(a, 1 + j, (*chip, c), me).wait_recv()
                cp = copy(a, 4 + j, (*chip, c), sibling)
                cp.start()
                sent.append(cp)
        for a in range(n):
            copy(a, 0, sibling, me).wait_recv()
            for j, chip in enumerate(chips):
                copy(a, 4 + j, (*chip, 1 - c), me).wait_recv()
        for cp in sent:
            cp.wait_send()
        for cp in local + mod_copies:
            cp.wait()

    vm, hbm = pl.BlockSpec(memory_space=pltpu.VMEM), pl.BlockSpec(memory_space=pl.ANY)
    res = pl.pallas_call(
        body, name="prologue",
        out_shape=[jax.ShapeDtypeStruct((NDEV, 8, ncol), F32), jax.ShapeDtypeStruct((16, D), BF16)]
        + [jax.ShapeDtypeStruct((NDEV,) + a.shape, a.dtype) for a in arrs],
        in_specs=[vm, vm, vm] + [hbm] * n, out_specs=[vm, vm] + [hbm] * n,
        scratch_shapes=[pltpu.VMEM((NDEV, 8, D), F32), pltpu.SemaphoreType.DMA((n, NDEV - 1)), pltpu.SemaphoreType.DMA((n, NDEV - 1)),
                        pltpu.SemaphoreType.DMA((n,)), pltpu.SemaphoreType.DMA((2, NDEV - 1)), pltpu.SemaphoreType.DMA((2, NDEV - 1))],
    )(c8, w_ada_s, b_ada_s, *arrs)
    return res[0], res[1], res[2:]


def _in_set(idx, dests):
    p = idx == dests[0]
    for d in dests[1:]:
        p = jnp.logical_or(p, idx == d)
    return p


_HBM = pl.BlockSpec(memory_space=pltpu.HBM)
_SEM = pl.BlockSpec(memory_space=pltpu.SEMAPHORE)


def _pair_reduce(send, dests, name):
    nd = send.shape[0]

    def body(s_ref, o_ref, land, ssem, rsem):
        x, y, c = _my_pos()
        cps = []
        for i in range(nd):
            cp = pltpu.make_async_remote_copy(src_ref=s_ref.at[i], dst_ref=land.at[i], send_sem=ssem.at[i], recv_sem=rsem.at[i],
                                              device_id=(x, y, 1 - c), device_id_type=MESH)
            pl.when(c != (dests[i] & 1))(cp.start)
            cps.append(cp)
        for i in range(nd):
            mine = c == (dests[i] & 1)

            @pl.when(mine)
            def _():
                cps[i].wait_recv()
                o_ref[i] = (s_ref[i].astype(F32) + land[i].astype(F32)).astype(BF16)

            pl.when(jnp.logical_not(mine))(cps[i].wait_send)

    vm = pl.BlockSpec(memory_space=pltpu.VMEM)
    return pl.pallas_call(
        body, name=name, out_shape=jax.ShapeDtypeStruct(send.shape, send.dtype), in_specs=[vm], out_specs=vm,
        scratch_shapes=[pltpu.VMEM(send.shape, send.dtype), pltpu.SemaphoreType.DMA((nd,)), pltpu.SemaphoreType.DMA((nd,))],
        compiler_params=_cp(None, VMEM_BIG),
    )(send)


def _xchg_copies(xs_dests, sends, lands, ssem, rsem, lsem):
    x, y, c = _my_pos()
    me = 4 * x + 2 * y + c
    remote, local = [], []
    for a, dests in enumerate(xs_dests):
        same_core = dests[-1] == "same core"
        dests = dests[:-1] if same_core else dests
        lo, nd = dests[0], sends[a].shape[0]
        for k in range(1, NDEV):
            if same_core and k & 1:
                continue
            px, py, pc = _peer(k, x, y, c)
            pidx = 4 * px + 2 * py + pc
            cp = pltpu.make_async_remote_copy(src_ref=sends[a].at[jnp.clip(pidx - lo, 0, nd - 1)], dst_ref=lands[a].at[me],
                                              send_sem=ssem.at[a * (NDEV - 1) + k - 1], recv_sem=rsem.at[a * (NDEV - 1) + k - 1],
                                              device_id=(px, py, pc), device_id_type=MESH)
            remote.append((cp, _in_set(pidx, dests), _in_set(me, dests)))
        lc = pltpu.make_async_copy(sends[a].at[jnp.clip(me - lo, 0, nd - 1)], lands[a].at[me], lsem.at[a])
        local.append((lc, _in_set(me, dests)))
    return remote, local


def _xchg_start(xs, name, lands=None):
    n = len(xs)
    dests = [d for _, d in xs]
    sends = [pltpu.with_memory_space_constraint(s, pltpu.HBM) for s, _ in xs]
    lands = [None] * n if lands is None else lands
    lands = [pltpu.with_memory_space_constraint(lax.empty((NDEV,) + s.shape[1:], s.dtype) if l is None else l, pltpu.HBM)
             for (s, _), l in zip(xs, lands)]

    def body(*refs):
        send_refs, land_refs = refs[:n], refs[n:2 * n]
        ssem, rsem, lsem = refs[2 * n:2 * n + 3]
        token = refs[-1]
        remote, local = _xchg_copies(dests, send_refs, land_refs, ssem, rsem, lsem)
        for cp, to_dest, _ in remote:
            pl.when(to_dest)(cp.start)
        for lc, i_am_dest in local:
            pl.when(i_am_dest)(lc.start)
        token[...] = jnp.zeros_like(token)

    res = pl.pallas_call(
        body, name=name,
        out_shape=[pltpu.SemaphoreType.DMA((n * (NDEV - 1),)), pltpu.SemaphoreType.DMA((n * (NDEV - 1),)), pltpu.SemaphoreType.DMA((n,))]
        + [pltpu.HBM(a.shape, a.dtype) for a in list(sends) + list(lands)] + [jax.ShapeDtypeStruct((8, 128), F32)],
        in_specs=[_HBM] * (2 * n), out_specs=[_SEM, _SEM, _SEM] + [_HBM] * (2 * n) + [pl.BlockSpec(memory_space=pltpu.VMEM)],
        input_output_aliases={i: 3 + i for i in range(2 * n)},
        compiler_params=pltpu.CompilerParams(has_side_effects=pltpu.SideEffectType.DATAFLOW_SIDE_EFFECTING),
    )(*sends, *lands)
    return dict(sems=res[0:3], sends=res[3:3 + n], lands=res[3 + n:3 + 2 * n], dests=dests), res[-1]


def _xchg_wait(states, lands, land_of, after, name):
    flat = []
    for st in states:
        flat += list(st["sends"]) + list(st["sems"])
    nl = len(lands)

    def body(*refs):
        land_refs = refs[:nl]
        pos = nl
        for s, st in enumerate(states):
            n = len(st["dests"])
            send_refs = refs[pos:pos + n]
            ssem, rsem, lsem = refs[pos + n:pos + n + 3]
            pos += n + 3
            remote, local = _xchg_copies(st["dests"], send_refs, [land_refs[i] for i in land_of[s]], ssem, rsem, lsem)
            for cp, to_dest, i_am_dest in remote:
                pl.when(to_dest)(cp.wait_send)
                pl.when(i_am_dest)(cp.wait_recv)
            for lc, i_am_dest in local:
                pl.when(i_am_dest)(lc.wait)

    in_specs = [_HBM] * nl
    for st in states:
        in_specs += [_HBM] * len(st["dests"]) + [_SEM, _SEM, _SEM]
    return pl.pallas_call(
        body, name=name, out_shape=[pltpu.HBM(a.shape, a.dtype) for a in lands],
        in_specs=in_specs + [pl.BlockSpec(memory_space=pl.ANY)], out_specs=[_HBM] * nl,
        input_output_aliases={i: i for i in range(nl)},
        compiler_params=pltpu.CompilerParams(has_side_effects=pltpu.SideEffectType.DATAFLOW_SIDE_EFFECTING),
    )(*lands, *flat, after)


def _mm_tn(a, b, name, out_dtype=F32):
    S, M = a.shape
    N = b.shape[1]
    tk = min(2048, S)
    tn = N if N <= 768 else (640 if N % 640 == 0 else 512)
    nk = S // tk

    def body(a_ref, b_ref, o_ref, acc):
        k = pl.program_id(1)

        @pl.when(k == 0)
        def _():
            acc[...] = _dot(a_ref[...], b_ref[...], TN)

        @pl.when(k > 0)
        def _():
            acc[...] += _dot(a_ref[...], b_ref[...], TN)

        @pl.when(k == nk - 1)
        def _():
            o_ref[...] = acc[...].astype(out_dtype)

    return pl.pallas_call(
        body, name=name, out_shape=jax.ShapeDtypeStruct((M, N), out_dtype), grid=(N // tn, nk),
        in_specs=[pl.BlockSpec((tk, M), lambda j, k: (k, 0)), pl.BlockSpec((tk, tn), lambda j, k: (k, j))],
        out_specs=pl.BlockSpec((M, tn), lambda j, k: (0, j)), scratch_shapes=[pltpu.VMEM((M, tn), F32)],
        compiler_params=_cp(("parallel", "arbitrary"), VMEM_BIG),
    )(a, b)


def _fwd_in(x, mod, g_pre, w_p):
    S = x.shape[0]
    tm = min(512, S)

    def body(x_ref, mod_ref, g_ref, w_ref, p_ref, h_ref):
        xv = x_ref[...]
        r = lax.rsqrt(jnp.mean(xv * xv, axis=-1, keepdims=True) + EPS)
        h = (((xv * r) * g_ref[...]) * (1.0 + mod_ref[1:2, :]) + mod_ref[0:1, :]).astype(BF16)
        h_ref[...] = h
        p_ref[...] = _dot(h, w_ref[...]).astype(BF16)

    return pl.pallas_call(
        body, name="fwd_in", out_shape=[jax.ShapeDtypeStruct((S, P_W), BF16), jax.ShapeDtypeStruct((S, D), BF16)],
        grid=(S // tm,),
        in_specs=[pl.BlockSpec((tm, D), lambda i: (i, 0)), _full((3, D)), _full((1, D)), _full((D, P_W))],
        out_specs=[pl.BlockSpec((tm, P_W), lambda i: (i, 0)), pl.BlockSpec((tm, D), lambda i: (i, 0))],
        compiler_params=_cp(("parallel",), VMEM_BIG),
    )(x, mod, g_pre, w_p)


def _swap16(v):
    lane = lax.broadcasted_iota(jnp.int32, v.shape, 1)
    return jnp.where((lane % 32) < 16, pltpu.roll(v, 112, 1), pltpu.roll(v, 16, 1))


def _rope(v, cos, sin):
    return v * cos + _swap16(v) * sin


def _rope_t(v, cos, sin):
    return v * cos - _swap16(v) * sin


def _head_mean(v):
    lo = lax.broadcasted_iota(jnp.int32, v.shape, 1) < 64
    m0 = jnp.sum(jnp.where(lo, v, 0.0), axis=-1, keepdims=True)
    m1 = jnp.sum(jnp.where(lo, 0.0, v), axis=-1, keepdims=True)
    return jnp.where(lo, m0, m1) * (1.0 / 64.0)


def _prep(p, cos, sin, qg, kg):
    S = p.shape[0]
    tm = min(512, S)

    def body(qa_ref, kv_ref, qr_ref, kr_ref, cos_ref, sin_ref, qg_ref, kg_ref, qt_ref, kh_ref, kt_ref, vh_ref, vta_ref, qr2_ref, kr2_ref):
        cos_v, sin_v = cos_ref[...], sin_ref[...]
        for g in range(4):
            xv = qa_ref[:, 128 * g:128 * g + 128].astype(F32)
            r = lax.rsqrt(_head_mean(xv * xv) + EPS)
            yt = (_rope((xv * r) * qg_ref[...], cos_v, sin_v) * (0.125 * LOG2E)).T
            qt_ref[2 * g] = yt[:DH].astype(BF16)
            qt_ref[2 * g + 1] = yt[DH:].astype(BF16)
        xv = kv_ref[:, :128].astype(F32)
        r = lax.rsqrt(_head_mean(xv * xv) + EPS)
        yv = _rope((xv * r) * kg_ref[...], cos_v, sin_v)
        kh_ref[0] = yv[:, :64].astype(BF16)
        kh_ref[1] = yv[:, 64:].astype(BF16)
        yt = yv.T
        kt_ref[0] = yt[:DH].astype(BF16)
        kt_ref[1] = yt[DH:].astype(BF16)
        vv = kv_ref[:, 128:].astype(F32)
        vh_ref[0] = vv[:, :64].astype(BF16)
        vh_ref[1] = vv[:, 64:].astype(BF16)
        vt = vv.T
        tail = (lax.broadcasted_iota(jnp.int32, (DHA - DH, tm), 0) == 0).astype(BF16)
        for kvh in range(2):
            vta_ref[kvh, 0:DH, :] = vt[DH * kvh:DH * kvh + DH].astype(BF16)
            vta_ref[kvh, DH:DHA, :] = tail
        for g in range(2):
            sl = slice(128 * g, 128 * g + 128)
            qr2_ref[:, sl] = _rope(qr_ref[:, sl].astype(F32), cos_v, sin_v)
            kr2_ref[:, sl] = _rope(kr_ref[:, sl].astype(F32), cos_v, sin_v) * 0.125

    hm = lambda n: pl.BlockSpec((n, tm, DH), lambda i: (0, i, 0))
    ht = lambda n, r: pl.BlockSpec((n, r, tm), lambda i: (0, 0, i))
    return pl.pallas_call(
        body, name="prep",
        out_shape=[jax.ShapeDtypeStruct((8, DH, S), BF16), jax.ShapeDtypeStruct((2, S, DH), BF16), jax.ShapeDtypeStruct((2, DH, S), BF16),
                   jax.ShapeDtypeStruct((2, S, DH), BF16), jax.ShapeDtypeStruct((2, DHA, S), BF16),
                   jax.ShapeDtypeStruct((S, 256), F32), jax.ShapeDtypeStruct((S, 256), F32)],
        grid=(S // tm,),
        in_specs=[pl.BlockSpec((tm, 512), lambda i: (i, O_QA // 512)), pl.BlockSpec((tm, 256), lambda i: (i, O_KA // 256)),
                  pl.BlockSpec((tm, 256), lambda i: (i, O_QR // 256)), pl.BlockSpec((tm, 256), lambda i: (i, O_KR // 256)),
                  pl.BlockSpec((tm, 128), lambda i: (i, 0)), pl.BlockSpec((tm, 128), lambda i: (i, 0)), _full((1, 128)), _full((1, 128))],
        out_specs=[ht(8, DH), hm(2), ht(2, DH), hm(2), ht(2, DHA), pl.BlockSpec((tm, 256), lambda i: (i, 0)), pl.BlockSpec((tm, 256), lambda i: (i, 0))],
        compiler_params=_cp(("parallel",)),
    )(p, p, p, p, cos, sin, qg, kg)


def _attn_fwd(qt, kh, vta):
    S = qt.shape[2]
    tq, tk = min(1024, S), min(512, S)
    nj = S // tk

    def body(q_ref, k_ref, v_ref, o_ref, ot_ref, lse_ref, m_s, acc_s):
        j = pl.program_id(1)

        @pl.when(j == 0)
        def _():
            m_s[...] = jnp.full_like(m_s, -jnp.inf)
            acc_s[...] = jnp.zeros_like(acc_s)

        m_all = m_s[...]
        st = {0: _dot(k_ref[0], q_ref[0])}
        m_new, acc_new = [], []
        for h in range(8):
            if h + 1 < 8:
                st[h + 1] = _dot(k_ref[(h + 1) // 4], q_ref[h + 1])
            m_old = m_all[h:h + 1, :]
            mn = jnp.maximum(m_old, jnp.max(st[h], axis=0, keepdims=True))
            pt = jnp.exp2(st[h] - mn).astype(BF16)
            acc_new.append(jnp.exp2(m_old - mn) * acc_s[h] + _dot(v_ref[h // 4], pt))
            m_new.append(mn)
            del st[h]
        for h in range(8):
            acc_s[h] = acc_new[h]
            m_s[h:h + 1, :] = m_new[h]

        @pl.when(j == nj - 1)
        def _():
            for h in range(8):
                ot = acc_s[h, 0:DH, :] / acc_s[h, DH:DH + 1, :]
                ot_ref[h] = ot
                o_ref[:, DH * h:DH * h + DH] = ot.T
                lse_ref[h // 4, h % 4:h % 4 + 1, :] = m_s[h:h + 1, :] + jnp.log2(acc_s[h, DH:DH + 1, :])

    return pl.pallas_call(
        body, name="attn_fwd",
        out_shape=[jax.ShapeDtypeStruct((S, 512), F32), jax.ShapeDtypeStruct((8, DH, S), F32), jax.ShapeDtypeStruct((2, 4, S), F32)],
        grid=(S // tq, nj),
        in_specs=[pl.BlockSpec((8, DH, tq), lambda i, j: (0, 0, i)), pl.BlockSpec((2, tk, DH), lambda i, j: (0, j, 0)),
                  pl.BlockSpec((2, DHA, tk), lambda i, j: (0, 0, j))],
        out_specs=[pl.BlockSpec((tq, 512), lambda i, j: (i, 0)), pl.BlockSpec((8, DH, tq), lambda i, j: (0, 0, i)),
                   pl.BlockSpec((2, 4, tq), lambda i, j: (0, 0, i))],
        scratch_shapes=[pltpu.VMEM((8, tq), F32), pltpu.VMEM((8, DHA, tq), F32)],
        compiler_params=_cp(("parallel", "arbitrary"), VMEM_BIG),
    )(qt, kh, vta)


def _ret_tables(wf, wb):
    C = CH

    def body(wf_ref, wb_ref, dc_ref, qdf_ref, qdb_ref, kdf_ref, kdb_ref, a_ref):
        def logsig(w):
            z = jnp.exp(-jnp.abs(w))
            u = 1.0 + z
            l1p = jnp.where(u == 1.0, z, jnp.log(u) * (z / jnp.where(u == 1.0, 1.0, u - 1.0)))
            return jnp.minimum(w, 0.0) - l1p

        lgf, lgb = logsig(wf_ref[...]), logsig(wb_ref[...])
        lane4 = lax.broadcasted_iota(jnp.int32, (1, 4), 1)

        def pick(lg, h):
            return jnp.sum(jnp.where(lane4 == h, lg, 0.0), axis=-1, keepdims=True)

        ii = lax.broadcasted_iota(jnp.int32, (C, C), 0).astype(F32)
        jj = lax.broadcasted_iota(jnp.int32, (C, C), 1).astype(F32)
        dif = ii - jj
        hd = lax.broadcasted_iota(jnp.int32, (C, 256), 1) // DH
        lf_l = jnp.zeros((C, 256), F32)
        lb_l = jnp.zeros((C, 256), F32)
        for h in range(HR):
            lf, lb = pick(lgf, h), pick(lgb, h)
            dc_ref[h] = jnp.where(dif >= 0, jnp.exp(lf * jnp.maximum(dif, 0.0)), jnp.exp(lb * jnp.maximum(-dif, 0.0)))
            lf_l = jnp.where(hd == h, lf, lf_l)
            lb_l = jnp.where(hd == h, lb, lb_l)
            a_ref[h:h + 1, :] = jnp.broadcast_to(jnp.exp(lf * C), (1, 128))
            a_ref[HR + h:HR + h + 1, :] = jnp.broadcast_to(jnp.exp(lb * C), (1, 128))
        ri = lax.broadcasted_iota(jnp.int32, (C, 256), 0).astype(F32)
        qdf_ref[...] = jnp.exp(lf_l * (ri + 1.0))
        qdb_ref[...] = jnp.exp(lb_l * (C - ri))
        kdf_ref[...] = jnp.exp(lf_l * (C - 1.0 - ri))
        kdb_ref[...] = jnp.exp(lb_l * ri)

    t = jax.ShapeDtypeStruct((C, 256), F32)
    return pl.pallas_call(body, name="ret_tables",
                          out_shape=[jax.ShapeDtypeStruct((HR, C, C), F32), t, t, t, t, jax.ShapeDtypeStruct((8, 128), F32)])(wf, wb)


def _ret_states(kr2, p, kdf, kdb, adec):
    S = kr2.shape[0]
    C, N = CH, S // CH
    G = _scan_group(N)
    NG = N // G

    def body(kf_ref, vf_ref, kb_ref, vb_ref, kdf_ref, kdb_ref, a_ref, rf_ref, rb_ref, sf, sb):
        @pl.when(pl.program_id(0) == 0)
        def _():
            sf[...] = jnp.zeros_like(sf)
            sb[...] = jnp.zeros_like(sb)

        kvf, kvb = [], []
        for u in range(G):
            rows = slice(C * u, C * u + C)
            kdfw = (kf_ref[rows, :] * kdf_ref[...]).astype(BF16)
            kdbw = (kb_ref[rows, :] * kdb_ref[...]).astype(BF16)
            vf, vb = vf_ref[rows, :].astype(BF16), vb_ref[rows, :].astype(BF16)
            kvf.append([_dot(kdfw[:, _ks(h)], vf[:, _vs(h)], TN) for h in range(HR)])
            kvb.append([_dot(kdbw[:, _ks(h)], vb[:, _vs(h)], TN) for h in range(HR)])
        for u in range(G):
            rf_ref[u] = sf[...]
            for h in range(HR):
                sf[h] = a_ref[h:h + 1, :] * sf[h] + kvf[u][h]
        for u in reversed(range(G)):
            rb_ref[u] = sb[...]
            for h in range(HR):
                sb[h] = a_ref[HR + h:HR + h + 1, :] * sb[h] + kvb[u][h]

    st = jax.ShapeDtypeStruct((N, HR, DH, DV), F32)
    return pl.pallas_call(
        body, name="ret_states", out_shape=[st, st], grid=(NG,),
        in_specs=[pl.BlockSpec((G * C, 256), lambda t: (t, 0)), pl.BlockSpec((G * C, 512), lambda t: (t, O_VR // 512)),
                  pl.BlockSpec((G * C, 256), lambda t: (NG - 1 - t, 0)), pl.BlockSpec((G * C, 512), lambda t: (NG - 1 - t, O_VR // 512)),
                  _full((C, 256)), _full((C, 256)), _full((8, 128))],
        out_specs=[pl.BlockSpec((G, HR, DH, DV), lambda t: (t, 0, 0, 0)), pl.BlockSpec((G, HR, DH, DV), lambda t: (NG - 1 - t, 0, 0, 0))],
        scratch_shapes=[pltpu.VMEM((HR, DH, DV), F32), pltpu.VMEM((HR, DH, DV), F32)],
        compiler_params=_cp(("arbitrary",)),
    )(kr2, p, kr2, p, kdf, kdb, adec)


def _scan_group(n):
    return 4 if n % 4 == 0 else (2 if n % 2 == 0 else 1)


def _ks(h):
    return slice(DH * h, DH * h + DH)


def _vs(h):
    return slice(DV * h, DV * h + DV)


def _ret_heads_fwd(qb, kb, vb, qfw, qbw, dc_ref, rf_ref, rb_ref, u=0):
    hs = range(HR)
    s = [_dot(qb[:, _ks(h)], kb[:, _ks(h)], NT) for h in hs]
    inter = [_dot(qfw[:, _ks(h)], rf_ref[u, h].astype(BF16)) + _dot(qbw[:, _ks(h)], rb_ref[u, h].astype(BF16)) for h in hs]
    sd = [s[h] * dc_ref[h] for h in hs]
    o = [_dot(sd[h].astype(BF16), vb[:, _vs(h)]) + inter[h] for h in hs]
    return sd, o


def _ret_out(qr2, kr2, p, rf, rb, dc, qdf, qdb, gn):
    S = qr2.shape[0]
    C, N = CH, S // CH

    def body(q_ref, k_ref, v_ref, z_ref, rf_ref, rb_ref, dc_ref, qdf_ref, qdb_ref, gn_ref, yr_ref):
        qv = q_ref[...]
        qb, kb, vb = qv.astype(BF16), k_ref[...].astype(BF16), v_ref[...].astype(BF16)
        qfw, qbw = (qv * qdf_ref[...]).astype(BF16), (qv * qdb_ref[...]).astype(BF16)
        _, o = _ret_heads_fwd(qb, kb, vb, qfw, qbw, dc_ref, rf_ref, rb_ref)
        for h in range(HR):
            vs = _vs(h)
            mu = jnp.mean(o[h], axis=-1, keepdims=True)
            var = jnp.mean(jnp.square(o[h] - mu), axis=-1, keepdims=True)
            on = (o[h] - mu) * lax.rsqrt(var + EPS)
            z = z_ref[:, vs].astype(F32)
            yr_ref[:, vs] = ((on * gn_ref[:, vs]) * (z * _sigmoid(z))).astype(BF16)

    return pl.pallas_call(
        body, name="ret_out", out_shape=jax.ShapeDtypeStruct((S, 512), BF16), grid=(N,),
        in_specs=[pl.BlockSpec((C, 256), lambda t: (t, 0)), pl.BlockSpec((C, 256), lambda t: (t, 0)),
                  pl.BlockSpec((C, 512), lambda t: (t, O_VR // 512)), pl.BlockSpec((C, 512), lambda t: (t, O_ZR // 512)),
                  pl.BlockSpec((1, HR, DH, DV), lambda t: (t, 0, 0, 0)), pl.BlockSpec((1, HR, DH, DV), lambda t: (t, 0, 0, 0)),
                  _full((HR, C, C)), _full((C, 256)), _full((C, 256)), _full((1, 512))],
        out_specs=pl.BlockSpec((C, 512), lambda t: (t, 0)),
        compiler_params=_cp(("parallel",)),
    )(qr2, kr2, p, p, rf, rb, dc, qdf, qdb, gn)


def _mid(x, tgt, mod, g_post, o_att, p, yr, w_pa, w_pr, w_out):
    S = x.shape[0]
    tm = min(256, S)

    def body(x_ref, t_ref, mod_ref, gp_ref, o_ref, za_ref, gl_ref, yr_ref, wpa_ref, wpr_ref, wout_ref,
             dout_ref, do_ref, dpm_ref, dyr_ref, mb_ref, dub_ref, yab_ref, dab_ref, drb_ref, sums_ref):
        @pl.when(pl.program_id(0) == 0)
        def _():
            sums_ref[...] = jnp.zeros_like(sums_ref)

        za = za_ref[...].astype(F32)
        sa = _sigmoid(za)
        sil = za * sa
        ov = o_ref[...]
        ya_b = (ov * sil).astype(BF16)
        yr_b = yr_ref[...]
        av = _dot(ya_b, wpa_ref[...])
        rv = _dot(yr_b, wpr_ref[...])
        ga = _sigmoid(gl_ref[:, :D].astype(F32))
        gr = _sigmoid(gl_ref[:, D:].astype(F32))
        mb = (ga * av + gr * rv).astype(BF16)
        u = _dot(mb, wout_ref[...])
        r2 = lax.rsqrt(jnp.mean(u * u, axis=-1, keepdims=True) + EPS)
        un = u * r2
        gp = gp_ref[...]
        yv = un * gp
        gate = mod_ref[2:3, :]
        err = (x_ref[...] + gate * yv) - t_ref[...]
        dout = err * (1.0 / D)
        dout_ref[...] = dout
        dy = dout * gate
        sums_ref[0:1, :] += jnp.sum(dout * yv, axis=0, keepdims=True)
        sums_ref[1:2, :] += jnp.sum(dy * un, axis=0, keepdims=True)
        sums_ref[2:3, :] += jnp.sum(err * err, axis=0, keepdims=True)
        dyg = dy * gp
        du_b = (r2 * (dyg - un * jnp.mean(dyg * un, axis=-1, keepdims=True))).astype(BF16)
        dm = _dot(du_b, wout_ref[...], NT)
        da_b = (dm * ga).astype(BF16)
        dr_b = (dm * gr).astype(BF16)
        dpm_ref[:, :D] = (dm * av * (ga * (1.0 - ga))).astype(BF16)
        dpm_ref[:, D:2 * D] = (dm * rv * (gr * (1.0 - gr))).astype(BF16)
        dya = _dot(da_b, wpa_ref[...], NT)
        dyr_ref[...] = _dot(dr_b, wpr_ref[...], NT)
        dov = dya * sil
        for g in range(4):
            dt = dov[:, 128 * g:128 * g + 128].T
            do_ref[2 * g] = dt[:DH].astype(BF16)
            do_ref[2 * g + 1] = dt[DH:].astype(BF16)
        dpm_ref[:, 2 * D:] = (dya * ov * (sa * (1.0 + za * (1.0 - sa)))).astype(BF16)
        mb_ref[...] = mb
        dub_ref[...] = du_b
        yab_ref[...] = ya_b
        dab_ref[...] = da_b
        drb_ref[...] = dr_b

    row = lambda w: pl.BlockSpec((tm, w), lambda i: (i, 0))
    sd = lambda w, dt: jax.ShapeDtypeStruct((S, w), dt)
    return pl.pallas_call(
        body, name="mid",
        out_shape=[sd(D, F32), jax.ShapeDtypeStruct((8, DH, S), BF16), sd(2560, BF16), sd(512, F32), sd(D, BF16), sd(D, BF16), sd(512, BF16),
                   sd(D, BF16), sd(D, BF16), jax.ShapeDtypeStruct((8, D), F32)],
        grid=(S // tm,),
        in_specs=[row(D), row(D), _full((3, D)), _full((1, D)), row(512), pl.BlockSpec((tm, 512), lambda i: (i, O_ZA // 512)),
                  pl.BlockSpec((tm, 2048), lambda i: (i, 0)), row(512), _full((512, D)), _full((512, D)), _full((D, D))],
        out_specs=[row(D), pl.BlockSpec((8, DH, tm), lambda i: (0, 0, i)), row(2560), row(512), row(D), row(D), row(512), row(D), row(D),
                   _full((8, D))],
        compiler_params=_cp(("arbitrary",), VMEM_BIG),
    )(x, tgt, mod, g_post, o_att, p, p, yr, w_pa, w_pr, w_out)


def _attn_bwd(qt, kh, kt, vh, dot_, ot, lse):
    S = qt.shape[2]
    tq, tk = min(1024, S), min(1024, S)

    def body(q_ref, k_ref, kt_ref, v_ref, do_ref, o_ref, lse_ref, dq_ref, dk_ref, dv_ref):
        j, i = pl.program_id(0), pl.program_id(1)
        cols = pl.ds(pl.multiple_of(i * tq, tq), tq)
        st = {0: _dot(k_ref[0], q_ref[0])}
        dpt = {0: _dot(v_ref[0], do_ref[0])}
        dk_acc, dv_acc, dqs = [None, None], [None, None], []
        for h in range(8):
            g = h // 4
            if h + 1 < 8:
                st[h + 1] = _dot(k_ref[(h + 1) // 4], q_ref[h + 1])
                dpt[h + 1] = _dot(v_ref[(h + 1) // 4], do_ref[h + 1])
            qt_h, dot_h = q_ref[h], do_ref[h]
            delta = jnp.sum(dot_h.astype(F32) * o_ref[h], axis=0, keepdims=True)
            pt = jnp.exp2(st[h] - lse_ref[g, h % 4:h % 4 + 1, :])
            dst = (pt * (dpt[h] - delta)).astype(BF16)
            dv_h = _dot(dot_h, pt.astype(BF16), NT)
            dk_h = _dot(qt_h, dst, NT)
            dqs.append(_dot(kt_ref[g], dst))
            dv_acc[g] = dv_h if dv_acc[g] is None else dv_acc[g] + dv_h
            dk_acc[g] = dk_h if dk_acc[g] is None else dk_acc[g] + dk_h
            del st[h], dpt[h]

        @pl.when(i == 0)
        def _():
            for g in range(2):
                dk_ref[g] = dk_acc[g]
                dv_ref[g] = dv_acc[g]

        @pl.when(i > 0)
        def _():
            for g in range(2):
                dk_ref[g] += dk_acc[g]
                dv_ref[g] += dv_acc[g]

        @pl.when(j == 0)
        def _():
            for h in range(8):
                dq_ref[h, :, cols] = dqs[h]

        @pl.when(j > 0)
        def _():
            for h in range(8):
                dq_ref[h, :, cols] += dqs[h]

    return pl.pallas_call(
        body, name="attn_bwd",
        out_shape=[jax.ShapeDtypeStruct((8, DH, S), F32), jax.ShapeDtypeStruct((2, DH, S), F32), jax.ShapeDtypeStruct((2, DH, S), F32)],
        grid=(S // tk, S // tq),
        in_specs=[pl.BlockSpec((8, DH, tq), lambda j, i: (0, 0, i)), pl.BlockSpec((2, tk, DH), lambda j, i: (0, j, 0)),
                  pl.BlockSpec((2, DH, tk), lambda j, i: (0, 0, j)), pl.BlockSpec((2, tk, DH), lambda j, i: (0, j, 0)),
                  pl.BlockSpec((8, DH, tq), lambda j, i: (0, 0, i)), pl.BlockSpec((8, DH, tq), lambda j, i: (0, 0, i)),
                  pl.BlockSpec((2, 4, tq), lambda j, i: (0, 0, i))],
        out_specs=[pl.BlockSpec((8, DH, S), lambda j, i: (0, 0, 0)), pl.BlockSpec((2, DH, tk), lambda j, i: (0, 0, j)),
                   pl.BlockSpec((2, DH, tk), lambda j, i: (0, 0, j))],
        compiler_params=_cp(("arbitrary", "arbitrary"), VMEM_BIG),
    )(qt, kh, kt, vh, dot_, ot, lse)


def _attn_prep_bwd(dqt, dkt, dvt, p, cos, sin, qg, kg):
    S = dqt.shape[2]
    tm = min(512, S)

    def body(dq_ref, dk_ref, dv_ref, qa_ref, ka_ref, cos_ref, sin_ref, qg_ref, kg_ref, dp_ref, gs_ref):
        @pl.when(pl.program_id(0) == 0)
        def _():
            gs_ref[...] = jnp.zeros_like(gs_ref)

        cos_v, sin_v = cos_ref[...], sin_ref[...]

        def pair(ref, a):
            return jnp.concatenate([ref[a], ref[a + 1]], axis=0).T

        def norm_bwd(dyv, xv, gv, row):
            r = lax.rsqrt(_head_mean(xv * xv) + EPS)
            xn = xv * r
            dxh = _rope_t(dyv, cos_v, sin_v)
            gs_ref[row:row + 1, :] += jnp.sum(dxh * xn, axis=0, keepdims=True)
            dg = dxh * gv
            return r * (dg - xn * _head_mean(dg * xn))

        for g in range(4):
            sl = slice(128 * g, 128 * g + 128)
            dp_ref[:, sl] = norm_bwd(pair(dq_ref, 2 * g) * 0.125, qa_ref[:, sl].astype(F32), qg_ref[...], 0).astype(BF16)
        dp_ref[:, 512:640] = norm_bwd(pair(dk_ref, 0) * LN2, ka_ref[...].astype(F32), kg_ref[...], 1).astype(BF16)
        dp_ref[:, 640:768] = pair(dv_ref, 0).astype(BF16)

    ht = lambda n: pl.BlockSpec((n, DH, tm), lambda i: (0, 0, i))
    return pl.pallas_call(
        body, name="attn_prep_bwd", out_shape=[jax.ShapeDtypeStruct((S, 768), BF16), jax.ShapeDtypeStruct((8, 128), F32)],
        grid=(S // tm,),
        in_specs=[ht(8), ht(2), ht(2),
                  pl.BlockSpec((tm, 512), lambda i: (i, O_QA // 512)), pl.BlockSpec((tm, 128), lambda i: (i, O_KA // 128)),
                  pl.BlockSpec((tm, 128), lambda i: (i, 0)), pl.BlockSpec((tm, 128), lambda i: (i, 0)), _full((1, 128)), _full((1, 128))],
        out_specs=[pl.BlockSpec((tm, 768), lambda i: (i, 0)), _full((8, 128))],
        compiler_params=_cp(("arbitrary",)),
    )(dqt, dkt, dvt, p, p, cos, sin, qg, kg)


def _ret_bwd_chunk(qr2, kr2, p, rf, rb, dc, qdf, qdb, gn, dyr, cos, sin):
    S = qr2.shape[0]
    C, N = CH, S // CH
    G = 1

    def body(q_ref, k_ref, v_ref, z_ref, rf_ref, rb_ref, dc_ref, qdf_ref, qdb_ref, gn_ref, dyr_ref, cos_ref, sin_ref,
             dpa_ref, dk_ref, dv_ref, drf_ref, drb_ref, dgn_ref, dlg_ref, dqs):
        @pl.when(pl.program_id(0) == 0)
        def _():
            dgn_ref[...] = jnp.zeros_like(dgn_ref)
            dlg_ref[...] = jnp.zeros_like(dlg_ref)

        ii = lax.broadcasted_iota(jnp.int32, (C, C), 0).astype(F32)
        jj = lax.broadcasted_iota(jnp.int32, (C, C), 1).astype(F32)
        dif = ii - jj
        ri = lax.broadcasted_iota(jnp.int32, (C, 1), 0).astype(F32)
        hs = range(HR)
        for u in range(G):
            rows = slice(C * u, C * u + C)
            qv = q_ref[rows, :]
            qb, kb, vb = qv.astype(BF16), k_ref[rows, :].astype(BF16), v_ref[rows, :].astype(BF16)
            qf32, qb32 = qv * qdf_ref[...], qv * qdb_ref[...]
            qfw, qbw = qf32.astype(BF16), qb32.astype(BF16)
            sd, o = _ret_heads_fwd(qb, kb, vb, qfw, qbw, dc_ref, rf_ref, rb_ref, u)
            do_b = []
            for h in hs:
                vs = _vs(h)
                mu = jnp.mean(o[h], axis=-1, keepdims=True)
                rstd = lax.rsqrt(jnp.mean(jnp.square(o[h] - mu), axis=-1, keepdims=True) + EPS)
                on = (o[h] - mu) * rstd
                z = z_ref[rows, vs].astype(F32)
                sz = _sigmoid(z)
                dy = dyr_ref[rows, vs]
                gnv = gn_ref[:, vs]
                dpa_ref[rows, 256 + DV * h:256 + DV * h + DV] = (dy * (on * gnv) * (sz * (1.0 + z * (1.0 - sz)))).astype(BF16)
                dys = dy * (z * sz)
                dgn_ref[:, vs] += jnp.sum(dys * on, axis=0, keepdims=True)
                don = dys * gnv
                do = rstd * (don - jnp.mean(don, axis=-1, keepdims=True) - on * jnp.mean(don * on, axis=-1, keepdims=True))
                do_b.append(do.astype(BF16))
            dpm = [_dot(do_b[h], vb[:, _vs(h)], NT) for h in hs]
            dqf = [_dot(do_b[h], rf_ref[u, h].astype(BF16), NT) for h in hs]
            dqb = [_dot(do_b[h], rb_ref[u, h].astype(BF16), NT) for h in hs]
            for h in hs:
                dv_ref[rows, _vs(h)] = _dot(sd[h].astype(BF16), do_b[h], TN)
                drf_ref[u, h] = _dot(qfw[:, _ks(h)], do_b[h], TN)
                drb_ref[u, h] = _dot(qbw[:, _ks(h)], do_b[h], TN)
            dsd = [(dpm[h] * dc_ref[h]).astype(BF16) for h in hs]
            for h in hs:
                ks = _ks(h)
                dqs[rows, ks] = _dot(dsd[h], kb[:, ks]) + dqf[h] * qdf_ref[:, ks] + dqb[h] * qdb_ref[:, ks]
                dk_ref[rows, ks] = _dot(dsd[h], qb[:, ks], TN)
            for h in hs:
                ks = _ks(h)
                e = dpm[h] * sd[h]
                lf = _sum11(e * jnp.maximum(dif, 0.0)) + _sum11(jnp.sum(qf32[:, ks] * dqf[h], axis=-1, keepdims=True) * (ri + 1.0))
                lb = _sum11(e * jnp.maximum(-dif, 0.0)) + _sum11(jnp.sum(qb32[:, ks] * dqb[h], axis=-1, keepdims=True) * (C - ri))
                dlg_ref[h:h + 1, :] += jnp.broadcast_to(lf, (1, 128))
                dlg_ref[HR + h:HR + h + 1, :] += jnp.broadcast_to(lb, (1, 128))
            for g in range(2):
                sl = slice(128 * g, 128 * g + 128)
                dpa_ref[rows, sl] = _rope_t(dqs[rows, sl], cos_ref[rows, :], sin_ref[rows, :]).astype(BF16)

    st = jax.ShapeDtypeStruct((N, HR, DH, DV), F32)
    stb = lambda: pl.BlockSpec((G, HR, DH, DV), lambda t: (t, 0, 0, 0))
    row = lambda w, off=0: pl.BlockSpec((G * C, w), lambda t: (t, off))
    return pl.pallas_call(
        body, name="ret_bwd_chunk",
        out_shape=[jax.ShapeDtypeStruct((S, 768), BF16), jax.ShapeDtypeStruct((S, 256), F32), jax.ShapeDtypeStruct((S, 512), F32), st, st,
                   jax.ShapeDtypeStruct((1, 512), F32), jax.ShapeDtypeStruct((8, 128), F32)],
        grid=(N // G,),
        in_specs=[row(256), row(256), row(512, O_VR // 512), row(512, O_ZR // 512),
                  stb(), stb(), _full((HR, C, C)), _full((C, 256)), _full((C, 256)), _full((1, 512)), row(512), row(128), row(128)],
        out_specs=[row(768), row(256), row(512), stb(), stb(), _full((1, 512)), _full((8, 128))],
        scratch_shapes=[pltpu.VMEM((G * C, 256), F32)],
        compiler_params=_cp(("arbitrary",)),
    )(qr2, kr2, p, p, rf, rb, dc, qdf, qdb, gn, dyr, cos, sin)


def _ret_bwd_scan(kr2, p, rf, rb, drf, drb, kdf, kdb, adec):
    S = kr2.shape[0]
    C, N = CH, S // CH
    G = _scan_group(N)
    NG = N // G

    def body(kf_ref, vf_ref, kb_ref, vb_ref, rf_ref, rb_ref, drf_ref, drb_ref, kdf_ref, kdb_ref, a_ref,
             dkf_ref, dkb_ref, dvf_ref, dvb_ref, dlg_ref, gf, gb):
        @pl.when(pl.program_id(0) == 0)
        def _():
            gf[...] = jnp.zeros_like(gf)
            gb[...] = jnp.zeros_like(gb)
            dlg_ref[...] = jnp.zeros_like(dlg_ref)

        ri = lax.broadcasted_iota(jnp.int32, (C, 1), 0).astype(F32)

        def one(k_ref, v_ref, r_ref, dr_ref, kd_ref, g_s, dk_ref, dv_ref, row0, wexp, order):
            g = [g_s[h] for h in range(HR)]
            lgs = [jnp.zeros((1, 1), F32) for _ in range(HR)]
            for u in order:
                rows = slice(C * u, C * u + C)
                kd32 = k_ref[rows, :] * kd_ref[...]
                kdw = kd32.astype(BF16)
                vb = v_ref[rows, :].astype(BF16)
                for h in range(HR):
                    ks, vs = _ks(h), _vs(h)
                    g_b = g[h].astype(BF16)
                    dkd = _dot(vb[:, vs], g_b, NT)
                    dk_ref[rows, ks] = dkd * kd_ref[:, ks]
                    dv_ref[rows, vs] = _dot(kdw[:, ks], g_b)
                    av = a_ref[row0 + h:row0 + h + 1, :]
                    lgs[h] = lgs[h] + (_sum11(jnp.sum(kd32[:, ks] * dkd, axis=-1, keepdims=True) * wexp)
                                       + C * av[:, 0:1] * _sum11(r_ref[u, h] * g[h]))
                    g[h] = dr_ref[u, h] + av * g[h]
            for h in range(HR):
                g_s[h] = g[h]
                dlg_ref[row0 + h:row0 + h + 1, :] += jnp.broadcast_to(lgs[h], (1, 128))

        one(kf_ref, vf_ref, rf_ref, drf_ref, kdf_ref, gf, dkf_ref, dvf_ref, 0, C - 1.0 - ri, list(reversed(range(G))))
        one(kb_ref, vb_ref, rb_ref, drb_ref, kdb_ref, gb, dkb_ref, dvb_ref, HR, ri, list(range(G)))

    fwd = lambda w, off=0: pl.BlockSpec((G * C, w), lambda t: (NG - 1 - t, off))
    bwd = lambda w, off=0: pl.BlockSpec((G * C, w), lambda t: (t, off))
    stf = lambda: pl.BlockSpec((G, HR, DH, DV), lambda t: (NG - 1 - t, 0, 0, 0))
    stb = lambda: pl.BlockSpec((G, HR, DH, DV), lambda t: (t, 0, 0, 0))
    return pl.pallas_call(
        body, name="ret_bwd_scan",
        out_shape=[jax.ShapeDtypeStruct((S, 256), F32), jax.ShapeDtypeStruct((S, 256), F32), jax.ShapeDtypeStruct((S, 512), F32),
                   jax.ShapeDtypeStruct((S, 512), F32), jax.ShapeDtypeStruct((8, 128), F32)],
        grid=(NG,),
        in_specs=[fwd(256), fwd(512, O_VR // 512), bwd(256), bwd(512, O_VR // 512), stf(), stb(), stf(), stb(),
                  _full((C, 256)), _full((C, 256)), _full((8, 128))],
        out_specs=[fwd(256), bwd(256), fwd(512), bwd(512), _full((8, 128))],
        scratch_shapes=[pltpu.VMEM((HR, DH, DV), F32), pltpu.VMEM((HR, DH, DV), F32)],
        compiler_params=_cp(("arbitrary",)),
    )(kr2, p, kr2, p, rf, rb, drf, drb, kdf, kdb, adec)


def _ret_bwd_final(dk_i, dkf, dkb, dv_i, dvf, dvb, cos, sin):
    S = dk_i.shape[0]
    tm = min(512, S)

    def body(a_ref, b_ref, c_ref, d_ref, e_ref, f_ref, cos_ref, sin_ref, o_ref):
        o_ref[:, :512] = (d_ref[...] + e_ref[...] + f_ref[...]).astype(BF16)
        cos_v, sin_v = cos_ref[...], sin_ref[...]
        for g in range(2):
            sl = slice(128 * g, 128 * g + 128)
            dk = a_ref[:, sl] + b_ref[:, sl] + c_ref[:, sl]
            o_ref[:, 512 + 128 * g:512 + 128 * g + 128] = (_rope_t(dk, cos_v, sin_v) * 0.125).astype(BF16)

    row = lambda w: pl.BlockSpec((tm, w), lambda i: (i, 0))
    return pl.pallas_call(
        body, name="ret_bwd_final", out_shape=jax.ShapeDtypeStruct((S, 768), BF16), grid=(S // tm,),
        in_specs=[row(256), row(256), row(256), row(512), row(512), row(512), row(128), row(128)], out_specs=row(768),
        compiler_params=_cp(("parallel",)),
    )(dk_i, dkf, dkb, dv_i, dvf, dvb, cos, sin)


def _bwd_in(dpm, dpa, dpra, dprb, w_p, x, dout, mod, g_pre):
    S = x.shape[0]
    tm = min(256, S)

    def body(a_ref, b_ref, c_ref, d_ref, w_ref, x_ref, dout_ref, mod_ref, g_ref, gx_ref, sums_ref):
        @pl.when(pl.program_id(0) == 0)
        def _():
            sums_ref[...] = jnp.zeros_like(sums_ref)

        dh = (_dot(a_ref[...], w_ref[:, :O_QA], NT) + _dot(b_ref[...], w_ref[:, O_QA:O_QR], NT)
              + _dot(c_ref[...], w_ref[:, O_QR:O_VR], NT) + _dot(d_ref[...], w_ref[:, O_VR:], NT))
        xv = x_ref[...]
        r = lax.rsqrt(jnp.mean(xv * xv, axis=-1, keepdims=True) + EPS)
        xn = xv * r
        gv = g_ref[...]
        sc1 = 1.0 + mod_ref[1:2, :]
        sums_ref[0:1, :] += jnp.sum(dh, axis=0, keepdims=True)
        sums_ref[1:2, :] += jnp.sum(dh * (xn * gv), axis=0, keepdims=True)
        sums_ref[2:3, :] += jnp.sum(dh * xn, axis=0, keepdims=True) * sc1
        dxn = dh * (gv * sc1)
        gx_ref[...] = dout_ref[...] + r * (dxn - xn * jnp.mean(dxn * xn, axis=-1, keepdims=True))

    row = lambda w: pl.BlockSpec((tm, w), lambda i: (i, 0))
    return pl.pallas_call(
        body, name="bwd_in", out_shape=[jax.ShapeDtypeStruct((S, D), F32), jax.ShapeDtypeStruct((8, D), F32)], grid=(S // tm,),
        in_specs=[row(2560), row(768), row(768), row(768), _full((D, P_W)), row(D), row(D), _full((3, D)), _full((1, D))],
        out_specs=[row(D), _full((8, D))],
        compiler_params=_cp(("arbitrary",), VMEM_BIG),
    )(dpm, dpa, dpra, dprb, w_p, x, dout, mod, g_pre)


SMALL = ("b_ada", "g_pre", "qn_g", "kn_g", "w_dec_f", "w_dec_b", "gn_g", "g_post")


def _small_update(gathered, wmv):
    ns = len(SMALL)

    def body(*refs):
        gin_ref, gmid_ref, ggn_ref, gatt_ref, gl1_ref, gl2_ref = refs[:6]
        wmv_refs = refs[6:6 + 3 * ns]
        loss_ref = refs[6 + 3 * ns]
        out_refs = refs[7 + 3 * ns:]

        def dsum(ref, r=None):
            rows = slice(None) if r is None else slice(r, r + 1)
            acc = ref[0, rows, :]
            for d in range(1, NDEV):
                acc = acc + ref[d, rows, :]
            return acc

        s_lg = dsum(gl1_ref) + dsum(gl2_ref)
        loss_ref[...] = (0.5 / D) * jnp.sum(dsum(gmid_ref, 2), axis=-1, keepdims=True)
        eye = lax.broadcasted_iota(jnp.int32, (8, 128), 0) == lax.broadcasted_iota(jnp.int32, (8, 128), 1)
        dlg = jnp.sum(jnp.where(eye, s_lg, 0.0), axis=0, keepdims=True)
        w_f, w_b = wmv_refs[3 * SMALL.index("w_dec_f")][...], wmv_refs[3 * SMALL.index("w_dec_b")][...]
        s_q, s_k = dsum(gatt_ref, 0), dsum(gatt_ref, 1)
        grads = dict(
            b_ada=jnp.concatenate([dsum(gin_ref, 0), dsum(gin_ref, 1), dsum(gmid_ref, 0)], axis=1),
            g_pre=dsum(gin_ref, 2), g_post=dsum(gmid_ref, 1), gn_g=dsum(ggn_ref),
            qn_g=s_q[:, :DH] + s_q[:, DH:], kn_g=s_k[:, :DH] + s_k[:, DH:],
            w_dec_f=dlg[:, 0:HR] * _sigmoid(-w_f), w_dec_b=dlg[:, HR:2 * HR] * _sigmoid(-w_b))
        for i, nme in enumerate(SMALL):
            g = grads[nme]
            w_ref, m_ref, v_ref = wmv_refs[3 * i:3 * i + 3]
            g_ref, d_ref, nm_ref, nv_ref = out_refs[4 * i:4 * i + 4]
            g_ref[...] = g
            m2 = ADAM_B1 * m_ref[...] + (1.0 - ADAM_B1) * g
            v2 = ADAM_B2 * v_ref[...] + (1.0 - ADAM_B2) * jnp.square(g)
            m_hat = m2 / (1.0 - ADAM_B1 ** ADAM_STEP)
            v_hat = v2 / (1.0 - ADAM_B2 ** ADAM_STEP)
            d_ref[...] = -ADAM_LR * (m_hat / (jnp.sqrt(v_hat) + ADAM_EPS) + ADAM_WD * w_ref[...])
            nm_ref[...] = m2
            nv_ref[...] = v2

    out_shape = [jax.ShapeDtypeStruct((1, 1), F32)]
    for i in range(ns):
        out_shape += [jax.ShapeDtypeStruct(wmv[3 * i].shape, F32)] * 4
    return pl.pallas_call(body, name="small_update", out_shape=out_shape)(*gathered, *wmv)


def _adamw(parts, w, m, v, name):
    n, R, L = parts.shape
    tr = 256 if (R % 256 == 0 and R > 256) else R

    def body(p_ref, w_ref, m_ref, v_ref, g_ref, d_ref, nm_ref, nv_ref):
        g = p_ref[0].astype(F32)
        for k in range(1, n):
            g = g + p_ref[k].astype(F32)
        g_ref[...] = g
        m2 = ADAM_B1 * m_ref[...] + (1.0 - ADAM_B1) * g
        v2 = ADAM_B2 * v_ref[...] + (1.0 - ADAM_B2) * jnp.square(g)
        m_hat = m2 / (1.0 - ADAM_B1 ** ADAM_STEP)
        v_hat = v2 / (1.0 - ADAM_B2 ** ADAM_STEP)
        d_ref[...] = -ADAM_LR * (m_hat / (jnp.sqrt(v_hat) + ADAM_EPS) + ADAM_WD * w_ref[...])
        nm_ref[...] = m2
        nv_ref[...] = v2

    blk = pl.BlockSpec((tr, L), lambda i: (i, 0))
    o = jax.ShapeDtypeStruct((R, L), F32)
    return pl.pallas_call(
        body, name=name, out_shape=[o, o, o, o], grid=(R // tr,),
        in_specs=[pl.BlockSpec((n, tr, L), lambda i: (0, i, 0)), blk, blk, blk], out_specs=[blk, blk, blk, blk],
        compiler_params=_cp(("parallel",), VMEM_BIG),
    )(parts, w, m, v)


def _rope_tables(S):
    f = np.float32
    t = np.arange(S)
    row, col = (t // 64).astype(f), (t % 64).astype(f)
    half = DH // 2
    inv = np.power(f(ROPE_THETA), -np.arange(0, half, 2, dtype=f) / f(half)).astype(f)
    ar, ac = (row[:, None] * inv[None, :]).astype(f), (col[:, None] * inv[None, :]).astype(f)
    cos64 = np.concatenate([np.cos(ar), np.cos(ar), np.cos(ac), np.cos(ac)], axis=1).astype(f)
    sin64 = np.concatenate([-np.sin(ar), np.sin(ar), -np.sin(ac), np.sin(ac)], axis=1).astype(f)
    return jnp.asarray(np.tile(cos64, (1, 2))), jnp.asarray(np.tile(sin64, (1, 2)))


def _to_p_order(w_orig):
    return jnp.concatenate([w_orig[:, ORIG[n][0]:ORIG[n][1]] for n in P_ORDER], axis=1)


def _pad_lanes(v, n):
    return jnp.pad(v, ((0, 0), (0, n - v.shape[1])))


def kernel(x, c, w_ada, b_ada, g_pre, w_in, qn_g, kn_g, w_dec_f, w_dec_b, gn_g, w_pa, w_pr, w_out, g_post, loss_target, m_w_ada, m_b_ada, m_g_pre, m_w_in, m_qn_g, m_kn_g, m_w_dec_f, m_w_dec_b, m_gn_g, m_w_pa, m_w_pr, m_w_out, m_g_post, v_w_ada, v_b_ada, v_g_pre, v_w_in, v_qn_g, v_kn_g, v_w_dec_f, v_w_dec_b, v_gn_g, v_w_pa, v_w_pr, v_w_out, v_g_post):
    S = x.shape[1]
    me = 4 * lax.axis_index("x") + 2 * lax.axis_index("y") + lax.axis_index("c")
    xs, tgt = x[0], loss_target[0]
    ncol_ada = w_ada.shape[2]
    ncol_in = w_in.shape[2]

    b_ada_s = lax.dynamic_slice(b_ada, (0, me * ncol_ada), (1, ncol_ada))
    mod_all, c_act, (wg_in,) = _prologue(jnp.pad(c, ((0, 7), (0, 0))), w_ada[0], b_ada_s, [w_in[0].astype(BF16)])
    mod = lax.dynamic_index_in_dim(mod_all, me, axis=1, keepdims=False).reshape(3, D)
    w_p = _to_p_order(wg_in.transpose(1, 0, 2).reshape(D, NDEV * ncol_in))
    all_dev = tuple(range(NDEV))
    st_w, tok_w = _xchg_start([(w_pa[0].astype(BF16)[None], all_dev), (w_pr[0].astype(BF16)[None], all_dev),
                               (w_out[0].astype(BF16)[None], all_dev)], "wgather_start")

    cos, sin = _rope_tables(S)
    qg, kg = jnp.tile(qn_g, (1, 2)), jnp.tile(kn_g, (1, 2))

    p, h = _fwd_in(xs, mod, g_pre + tok_w[0:1, 0:1], w_p)
    qt, kh, kt, vh, vta, qr2, kr2 = _prep(p, cos, sin, qg, kg)
    o_att, o_t, lse = _attn_fwd(qt, kh, vta)
    dc, qdf, qdb, kdf, kdb, adec = _ret_tables(w_dec_f, w_dec_b)
    rf, rb = _ret_states(kr2, p, kdf, kdb, adec)
    yr = _ret_out(qr2, kr2, p, rf, rb, dc, qdf, qdb, gn_g)
    wg_pa, wg_pr, wg_out = _xchg_wait([st_w], st_w["lands"], [[0, 1, 2]], yr, "wgather_wait")
    w_pa_f = wg_pa.transpose(1, 0, 2).reshape(512, D)
    w_pr_f = wg_pr.transpose(1, 0, 2).reshape(512, D)
    w_out_f = wg_out.reshape(D, D)

    dout, do, dpm, dyr, mb, dub, yab, dab, drb_, sums_mid = _mid(xs, tgt, mod, g_post, o_att, p, yr, w_pa_f, w_pr_f, w_out_f)
    gw_out = _mm_tn(mb, dub, "gw_out", BF16)
    gw_pa = _mm_tn(yab, dab, "gw_pa", BF16)
    gw_pr = _mm_tn(yr, drb_, "gw_pr", BF16)
    gi_m = _mm_tn(h, dpm, "gw_in_mid", BF16)

    def shards(cols, nd):
        return cols.reshape(D, nd, ncol_in).transpose(1, 0, 2)

    st_a, tok_a = _xchg_start([
        (gw_out.reshape(NDEV, 128, D), all_dev),
        (gw_pa.reshape(512, NDEV, 128).transpose(1, 0, 2), all_dev),
        (gw_pr.reshape(512, NDEV, 128).transpose(1, 0, 2), all_dev),
        (shards(gi_m[:, 224:2048], 3), (5, 6, 7))], "xchg_start_a",
        lands=[None, None, None, jnp.zeros((NDEV, D, ncol_in), BF16)])
    dqt, dkt, dvt = _attn_bwd(qt, kh, kt, vh, do, o_t, lse + tok_a[0, 0])
    dpa, gs_att = _attn_prep_bwd(dqt, dkt, dvt, p, cos, sin, qg, kg)
    gi_a = _mm_tn(h, dpa, "gw_in_att", BF16)
    st_b, tok_b = _xchg_start([(shards(jnp.concatenate([gi_a, gi_m[:, 2048:2496]], axis=1), 2), (0, 1))], "xchg_start_b",
                              lands=[st_a["lands"][3]])
    dpra, dk_i, dv_i, drf, drb, dgn, dlg1 = _ret_bwd_chunk(qr2, kr2, p, rf, rb, dc, qdf, qdb, gn_g + tok_b[0:1, 0:1], dyr, cos, sin)
    dkf, dkb, dvf, dvb, dlg2 = _ret_bwd_scan(kr2, p, rf, rb, drf, drb, kdf, kdb, adec)
    dprb = _ret_bwd_final(dk_i, dkf, dkb, dv_i, dvf, dvb, cos, sin)
    gi_ra = _mm_tn(h, dpra, "gw_in_reta", BF16)
    gi_rb = _mm_tn(h, dprb, "gw_in_retb", BF16)
    chip_c = _pair_reduce(shards(jnp.concatenate([gi_m[:, 2496:2560], gi_ra[:, :256], gi_rb[:, 512:768], gi_rb[:, :512],
                                                  gi_ra[:, 256:768], gi_m[:, :224]], axis=1), 3), (2, 3, 4), "pair_reduce_c")
    st_c, tok_c = _xchg_start([(chip_c, (2, 3, 4, "same core"))], "xchg_start_c", lands=[st_b["lands"][0]])
    grad_x, sums_in = _bwd_in(dpm, dpa, dpra, dprb, w_p, xs, dout, mod, g_pre + tok_c[0:1, 0:1])

    gathered = _small_allgather([sums_in, sums_mid, dgn, gs_att, dlg1, dlg2], "ag_small")
    given = dict(b_ada=(b_ada, m_b_ada, v_b_ada), g_pre=(g_pre, m_g_pre, v_g_pre), qn_g=(qn_g, m_qn_g, v_qn_g), kn_g=(kn_g, m_kn_g, v_kn_g),
                 w_dec_f=(w_dec_f, m_w_dec_f, v_w_dec_f), w_dec_b=(w_dec_b, m_w_dec_b, v_w_dec_b), gn_g=(gn_g, m_gn_g, v_gn_g),
                 g_post=(g_post, m_g_post, v_g_post))
    small = _small_update(gathered, [a for nme in SMALL for a in given[nme]])
    loss = small[0][0, 0]

    g_in_all, g_mid_all = gathered[0], gathered[1]
    dmod_all = lax.dynamic_slice(jnp.concatenate([g_in_all[:, 0, :], g_in_all[:, 1, :], g_mid_all[:, 0, :]], axis=1),
                                 (0, me * ncol_ada), (NDEV, ncol_ada))
    g_ada = _mm_tn(c_act, jnp.pad(dmod_all, ((0, 8), (0, 0))).astype(BF16), "gw_ada")

    ada = _adamw(g_ada[None], w_ada[0], m_w_ada[0], v_w_ada[0], "adamw_ada")
    rs_out, rs_pa, rs_pr, rs_in = _xchg_wait([st_a, st_b, st_c], list(st_a["lands"][:3]) + [st_c["lands"][0]],
                                             [[0, 1, 2, 3], [3], [3]], ada[1], "xchg_wait")
    res = dict(
        w_ada=ada,
        w_in=_adamw(rs_in, w_in[0], m_w_in[0], v_w_in[0], "adamw_in"),
        w_pa=_adamw(rs_pa, w_pa[0], m_w_pa[0], v_w_pa[0], "adamw_pa"),
        w_pr=_adamw(rs_pr, w_pr[0], m_w_pr[0], v_w_pr[0], "adamw_pr"),
        w_out=_adamw(rs_out, w_out[0], m_w_out[0], v_w_out[0], "adamw_out"),
    )
    names = ["w_ada", "b_ada", "g_pre", "w_in", "qn_g", "kn_g", "w_dec_f", "w_dec_b", "gn_g", "w_pa", "w_pr", "w_out", "g_post"]
    outs = [[], [], [], []]
    for nme in names:
        for q in range(4):
            if nme in res:
                outs[q].append(res[nme][q][None])
            else:
                outs[q].append(small[1 + 4 * SMALL.index(nme) + q])
    return (loss, grad_x[None], *outs[0], *outs[1], *outs[2], *outs[3])
```

```python
import jax
import jax.numpy as jnp
import numpy as np
from jax import lax
from jax.experimental import pallas as pl
from jax.experimental.pallas import tpu as pltpu

F32, BF16 = jnp.float32, jnp.bfloat16
D = 1024
DH = 64
DHA = 80
DV = 128
LOG2E = 1.4426950408889634
LN2 = 0.6931471805599453
HR = 4
CH = 128
EPS = 1e-6
ROPE_THETA = 10000.0
NDEV = 8
O_GL, O_ZA, O_QA, O_KA, O_VA, O_QR, O_ZR, O_VR, O_KR, P_W = 0, 2048, 2560, 3072, 3200, 3328, 3584, 4096, 4608, 4864
ORIG = dict(qa=(0, 512), ka=(512, 640), va=(640, 768), za=(768, 1280), qr=(1280, 1536), kr=(1536, 1792),
            vr=(1792, 2304), zr=(2304, 2816), gl=(2816, 4864))
P_ORDER = ("gl", "za", "qa", "ka", "va", "qr", "zr", "vr", "kr")
ADAM_LR, ADAM_B1, ADAM_B2, ADAM_EPS, ADAM_WD, ADAM_STEP = 0.001, 0.9, 0.999, 1e-08, 0.01, 10
VMEM_BIG = 56 * 1024 * 1024
MESH = pl.DeviceIdType.MESH

NT = (((1,), (1,)), ((), ()))
TN = (((0,), (0,)), ((), ()))


def _dot(a, b, dims=None):
    if dims is None:
        return jnp.dot(a, b, preferred_element_type=F32)
    return lax.dot_general(a, b, dims, preferred_element_type=F32)


def _cp(sem=None, vmem=None):
    kw = {}
    if sem is not None:
        kw["dimension_semantics"] = sem
    if vmem is not None:
        kw["vmem_limit_bytes"] = vmem
    return pltpu.CompilerParams(**kw)


def _sigmoid(z):
    return 1.0 / (1.0 + jnp.exp(-z))


def _sum11(m):
    return jnp.sum(jnp.sum(m, axis=-1, keepdims=True), axis=0, keepdims=True)


def _full(shape):
    n = len(shape)
    return pl.BlockSpec(shape, lambda *_: (0,) * n)


def _my_pos():
    return lax.axis_index("x"), lax.axis_index("y"), lax.axis_index("c")


def _peer(k, x, y, c):
    return ((1 - x) if k & 4 else x, (1 - y) if k & 2 else y, (1 - c) if k & 1 else c)


def _small_allgather(vs, name):
    n = len(vs)

    def body(*refs):
        v_refs, out_refs = refs[:n], refs[n:2 * n]
        send_sems, recv_sems = refs[2 * n:]
        x, y, c = _my_pos()
        me = 4 * x + 2 * y + c
        cps = []
        for a in range(n):
            out_refs[a][me] = v_refs[a][...]
            for k in range(1, NDEV):
                cp = pltpu.make_async_remote_copy(src_ref=v_refs[a], dst_ref=out_refs[a].at[me], send_sem=send_sems.at[a, k - 1],
                                                  recv_sem=recv_sems.at[a, k - 1], device_id=_peer(k, x, y, c), device_id_type=MESH)
                cp.start()
                cps.append(cp)
        for cp in cps:
            cp.wait()

    vm = pl.BlockSpec(memory_space=pltpu.VMEM)
    return pl.pallas_call(
        body, name=name, out_shape=[jax.ShapeDtypeStruct((NDEV,) + v.shape, v.dtype) for v in vs],
        in_specs=[vm] * n, out_specs=[vm] * n,
        scratch_shapes=[pltpu.SemaphoreType.DMA((n, NDEV - 1)), pltpu.SemaphoreType.DMA((n, NDEV - 1))],
    )(*vs)


def _prologue(c8, w_ada_s, b_ada_s, arrs):
    n = len(arrs)
    ncol = w_ada_s.shape[1]

    def body(*refs):
        c_ref, wa_ref, ba_ref = refs[:3]
        ins = refs[3:3 + n]
        mod_ref, cact_ref = refs[3 + n:5 + n]
        outs = refs[5 + n:5 + 2 * n]
        call_ref, send_sems, recv_sems, local_sems, s_send, s_recv = refs[5 + 2 * n:]
        x, y, c = _my_pos()
        me, sibling = (x, y, c), (x, y, 1 - c)
        chips = [(1 - x, y), (x, 1 - y), (1 - x, 1 - y)]
        me_i = 4 * x + 2 * y + c

        def small_gather(src_ref, dst_ref, row):
            cps = []
            for k in range(1, NDEV):
                cp = pltpu.make_async_remote_copy(src_ref=src_ref, dst_ref=dst_ref.at[me_i], send_sem=s_send.at[row, k - 1],
                                                  recv_sem=s_recv.at[row, k - 1], device_id=_peer(k, x, y, c), device_id_type=MESH)
                cp.start()
                cps.append(cp)
            return cps

        def blk(a, px, py, pc):
            return outs[a].at[4 * px + 2 * py + pc]

        def copy(a, k, block, to, src=None):
            return pltpu.make_async_remote_copy(src_ref=blk(a, *block) if src is None else src, dst_ref=blk(a, *block),
                                                send_sem=send_sems.at[a, k], recv_sem=recv_sems.at[a, k], device_id=to, device_id_type=MESH)

        call_ref[me_i] = c_ref[...]
        for cp in small_gather(c_ref, call_ref, 0):
            cp.wait()

        local, sent = [], []
        for a in range(n):
            mine = pltpu.make_async_copy(ins[a], blk(a, *me), local_sems.at[a])
            mine.start()
            local.append(mine)
            first = [copy(a, 0, me, sibling, src=ins[a])] + [copy(a, 1 + j, me, (*chip, c), src=ins[a]) for j, chip in enumerate(chips)]
            for cp in first:
                cp.start()
            sent += first

        cv = call_ref[:, 0, :]
        ca = jnp.concatenate([cv * _sigmoid(cv), jnp.zeros_like(cv)], axis=0).astype(BF16)
        cact_ref[...] = ca
        mod_ref[me_i] = (_dot(ca, wa_ref[...].astype(BF16)) + ba_ref[...])[:8]
        mod_copies = small_gather(mod_ref.at[me_i], mod_ref, 1)

        for j, chip in enumerate(chips):
            for a in range(n):
                copy(a, 1 + j, (*chip, c), me).wait_recv()
                cp = copy(a, 4 + j, (*chip, c), sibling)
                cp.start()
                sent.append(cp)
        for a in range(n):
            copy(a, 0, sibling, me).wait_recv()
            for j, chip in enumerate(chips):
                copy(a, 4 + j, (*chip, 1 - c), me).wait_recv()
        for cp in sent:
            cp.wait_send()
        for cp in local + mod_copies:
            cp.wait()

    vm, hbm = pl.BlockSpec(memory_space=pltpu.VMEM), pl.BlockSpec(memory_space=pl.ANY)
    res = pl.pallas_call(
        body, name="prologue",
        out_shape=[jax.ShapeDtypeStruct((NDEV, 8, ncol), F32), jax.ShapeDtypeStruct((16, D), BF16)]
        + [jax.ShapeDtypeStruct((NDEV,) + a.shape, a.dtype) for a in arrs],
        in_specs=[vm, vm, vm] + [hbm] * n, out_specs=[vm, vm] + [hbm] * n,
        scratch_shapes=[pltpu.VMEM((NDEV, 8, D), F32), pltpu.SemaphoreType.DMA((n, NDEV - 1)), pltpu.SemaphoreType.DMA((n, NDEV - 1)),
                        pltpu.SemaphoreType.DMA((n,)), pltpu.SemaphoreType.DMA((2, NDEV - 1)), pltpu.SemaphoreType.DMA((2, NDEV - 1))],
    )(c8, w_ada_s, b_ada_s, *arrs)
    return res[0], res[1], res[2:]


def _in_set(idx, dests):
    p = idx == dests[0]
    for d in dests[1:]:
        p = jnp.logical_or(p, idx == d)
    return p


_HBM = pl.BlockSpec(memory_space=pltpu.HBM)
_SEM = pl.BlockSpec(memory_space=pltpu.SEMAPHORE)


def _pair_reduce(send, dests, name):
    nd = send.shape[0]

    def body(s_ref, o_ref, land, ssem, rsem):
        x, y, c = _my_pos()
        cps = []
        for i in range(nd):
            cp = pltpu.make_async_remote_copy(src_ref=s_ref.at[i], dst_ref=land.at[i], send_sem=ssem.at[i], recv_sem=rsem.at[i],
                                              device_id=(x, y, 1 - c), device_id_type=MESH)
            pl.when(c != (dests[i] & 1))(cp.start)
            cps.append(cp)
        for i in range(nd):
            mine = c == (dests[i] & 1)

            @pl.when(mine)
            def _():
                cps[i].wait_recv()
                o_ref[i] = (s_ref[i].astype(F32) + land[i].astype(F32)).astype(BF16)

            pl.when(jnp.logical_not(mine))(cps[i].wait_send)

    vm = pl.BlockSpec(memory_space=pltpu.VMEM)
    return pl.pallas_call(
        body, name=name, out_shape=jax.ShapeDtypeStruct(send.shape, send.dtype), in_specs=[vm], out_specs=vm,
        scratch_shapes=[pltpu.VMEM(send.shape, send.dtype), pltpu.SemaphoreType.DMA((nd,)), pltpu.SemaphoreType.DMA((nd,))],
        compiler_params=_cp(None, VMEM_BIG),
    )(send)


def _xchg_copies(xs_dests, sends, lands, ssem, rsem, lsem):
    x, y, c = _my_pos()
    me = 4 * x + 2 * y + c
    remote, local = [], []
    for a, dests in enumerate(xs_dests):
        same_core = dests[-1] == "same core"
        dests = dests[:-1] if same_core else dests
        lo, nd = dests[0], sends[a].shape[0]
        for k in range(1, NDEV):
            if same_core and k & 1:
                continue
            px, py, pc = _peer(k, x, y, c)
            pidx = 4 * px + 2 * py + pc
            cp = pltpu.make_async_remote_copy(src_ref=sends[a].at[jnp.clip(pidx - lo, 0, nd - 1)], dst_ref=lands[a].at[me],
                                              send_sem=ssem.at[a * (NDEV - 1) + k - 1], recv_sem=rsem.at[a * (NDEV - 1) + k - 1],
                                              device_id=(px, py, pc), device_id_type=MESH)
            remote.append((cp, _in_set(pidx, dests), _in_set(me, dests)))
        lc = pltpu.make_async_copy(sends[a].at[jnp.clip(me - lo, 0, nd - 1)], lands[a].at[me], lsem.at[a])
        local.append((lc, _in_set(me, dests)))
    return remote, local


def _xchg_start(xs, name, lands=None):
    n = len(xs)
    dests = [d for _, d in xs]
    sends = [pltpu.with_memory_space_constraint(s, pltpu.HBM) for s, _ in xs]
    lands = [None] * n if lands is None else lands
    lands = [pltpu.with_memory_space_constraint(lax.empty((NDEV,) + s.shape[1:], s.dtype) if l is None else l, pltpu.HBM)
             for (s, _), l in zip(xs, lands)]

    def body(*refs):
        send_refs, land_refs = refs[:n], refs[n:2 * n]
        ssem, rsem, lsem = refs[2 * n:2 * n + 3]
        token = refs[-1]
        remote, local = _xchg_copies(dests, send_refs, land_refs, ssem, rsem, lsem)
        for cp, to_dest, _ in remote:
            pl.when(to_dest)(cp.start)
        for lc, i_am_dest in local:
            pl.when(i_am_dest)(lc.start)
        token[...] = jnp.zeros_like(token)

    res = pl.pallas_call(
        body, name=name,
        out_shape=[pltpu.SemaphoreType.DMA((n * (NDEV - 1),)), pltpu.SemaphoreType.DMA((n * (NDEV - 1),)), pltpu.SemaphoreType.DMA((n,))]
        + [pltpu.HBM(a.shape, a.dtype) for a in list(sends) + list(lands)] + [jax.ShapeDtypeStruct((8, 128), F32)],
        in_specs=[_HBM] * (2 * n), out_specs=[_SEM, _SEM, _SEM] + [_HBM] * (2 * n) + [pl.BlockSpec(memory_space=pltpu.VMEM)],
        input_output_aliases={i: 3 + i for i in range(2 * n)},
        compiler_params=pltpu.CompilerParams(has_side_effects=pltpu.SideEffectType.DATAFLOW_SIDE_EFFECTING),
    )(*sends, *lands)
    return dict(sems=res[0:3], sends=res[3:3 + n], lands=res[3 + n:3 + 2 * n], dests=dests), res[-1]


def _xchg_wait(states, lands, land_of, after, name):
    flat = []
    for st in states:
        flat += list(st["sends"]) + list(st["sems"])
    nl = len(lands)

    def body(*refs):
        land_refs = refs[:nl]
        pos = nl
        for s, st in enumerate(states):
            n = len(st["dests"])
            send_refs = refs[pos:pos + n]
            ssem, rsem, lsem = refs[pos + n:pos + n + 3]
            pos += n + 3
            remote, local = _xchg_copies(st["dests"], send_refs, [land_refs[i] for i in land_of[s]], ssem, rsem, lsem)
            for cp, to_dest, i_am_dest in remote:
                pl.when(to_dest)(cp.wait_send)
                pl.when(i_am_dest)(cp.wait_recv)
            for lc, i_am_dest in local:
                pl.when(i_am_dest)(lc.wait)

    in_specs = [_HBM] * nl
    for st in states:
        in_specs += [_HBM] * len(st["dests"]) + [_SEM, _SEM, _SEM]
    return pl.pallas_call(
        body, name=name, out_shape=[pltpu.HBM(a.shape, a.dtype) for a in lands],
        in_specs=in_specs + [pl.BlockSpec(memory_space=pl.ANY)], out_specs=[_HBM] * nl,
        input_output_aliases={i: i for i in range(nl)},
        compiler_params=pltpu.CompilerParams(has_side_effects=pltpu.SideEffectType.DATAFLOW_SIDE_EFFECTING),
    )(*lands, *flat, after)


def _mm_tn(a, b, name, out_dtype=F32):
    S, M = a.shape
    N = b.shape[1]
    tk = min(2048, S)
    tn = N if N <= 768 else (640 if N % 640 == 0 else 512)
    nk = S // tk

    def body(a_ref, b_ref, o_ref, acc):
        k = pl.program_id(1)

        @pl.when(k == 0)
        def _():
            acc[...] = _dot(a_ref[...], b_ref[...], TN)

        @pl.when(k > 0)
        def _():
            acc[...] += _dot(a_ref[...], b_ref[...], TN)

        @pl.when(k == nk - 1)
        def _():
            o_ref[...] = acc[...].astype(out_dtype)

    return pl.pallas_call(
        body, name=name, out_shape=jax.ShapeDtypeStruct((M, N), out_dtype), grid=(N // tn, nk),
        in_specs=[pl.BlockSpec((tk, M), lambda j, k: (k, 0)), pl.BlockSpec((tk, tn), lambda j, k: (k, j))],
        out_specs=pl.BlockSpec((M, tn), lambda j, k: (0, j)), scratch_shapes=[pltpu.VMEM((M, tn), F32)],
        compiler_params=_cp(("parallel", "arbitrary"), VMEM_BIG),
    )(a, b)


def _fwd_in(x, mod, g_pre, w_p):
    S = x.shape[0]
    tm = min(512, S)

    def body(x_ref, mod_ref, g_ref, w_ref, p_ref, h_ref):
        xv = x_ref[...]
        r = lax.rsqrt(jnp.mean(xv * xv, axis=-1, keepdims=True) + EPS)
        h = (((xv * r) * g_ref[...]) * (1.0 + mod_ref[1:2, :]) + mod_ref[0:1, :]).astype(BF16)
        h_ref[...] = h
        p_ref[...] = _dot(h, w_ref[...]).astype(BF16)

    return pl.pallas_call(
        body, name="fwd_in", out_shape=[jax.ShapeDtypeStruct((S, P_W), BF16), jax.ShapeDtypeStruct((S, D), BF16)],
        grid=(S // tm,),
        in_specs=[pl.BlockSpec((tm, D), lambda i: (i, 0)), _full((3, D)), _full((1, D)), _full((D, P_W))],
        out_specs=[pl.BlockSpec((tm, P_W), lambda i: (i, 0)), pl.BlockSpec((tm, D), lambda i: (i, 0))],
        compiler_params=_cp(("parallel",), VMEM_BIG),
    )(x, mod, g_pre, w_p)


def _swap16(v):
    lane = lax.broadcasted_iota(jnp.int32, v.shape, 1)
    return jnp.where((lane % 32) < 16, pltpu.roll(v, 112, 1), pltpu.roll(v, 16, 1))


def _rope(v, cos, sin):
    return v * cos + _swap16(v) * sin


def _rope_t(v, cos, sin):
    return v * cos - _swap16(v) * sin


def _head_mean(v):
    lo = lax.broadcasted_iota(jnp.int32, v.shape, 1) < 64
    m0 = jnp.sum(jnp.where(lo, v, 0.0), axis=-1, keepdims=True)
    m1 = jnp.sum(jnp.where(lo, 0.0, v), axis=-1, keepdims=True)
    return jnp.where(lo, m0, m1) * (1.0 / 64.0)


def _prep(p, cos, sin, qg, kg):
    S = p.shape[0]
    tm = min(512, S)

    def body(qa_ref, kv_ref, qr_ref, kr_ref, cos_ref, sin_ref, qg_ref, kg_ref, qt_ref, kh_ref, kt_ref, vh_ref, vta_ref, qr2_ref, kr2_ref):
        cos_v, sin_v = cos_ref[...], sin_ref[...]
        for g in range(4):
            xv = qa_ref[:, 128 * g:128 * g + 128].astype(F32)
            r = lax.rsqrt(_head_mean(xv * xv) + EPS)
            yt = (_rope((xv * r) * qg_ref[...], cos_v, sin_v) * (0.125 * LOG2E)).T
            qt_ref[2 * g] = yt[:DH].astype(BF16)
            qt_ref[2 * g + 1] = yt[DH:].astype(BF16)
        xv = kv_ref[:, :128].astype(F32)
        r = lax.rsqrt(_head_mean(xv * xv) + EPS)
        yv = _rope((xv * r) * kg_ref[...], cos_v, sin_v)
        kh_ref[0] = yv[:, :64].astype(BF16)
        kh_ref[1] = yv[:, 64:].astype(BF16)
        yt = yv.T
        kt_ref[0] = yt[:DH].astype(BF16)
        kt_ref[1] = yt[DH:].astype(BF16)
        vv = kv_ref[:, 128:].astype(F32)
        vh_ref[0] = vv[:, :64].astype(BF16)
        vh_ref[1] = vv[:, 64:].astype(BF16)
        vt = vv.T
        tail = (lax.broadcasted_iota(jnp.int32, (DHA - DH, tm), 0) == 0).astype(BF16)
        for kvh in range(2):
            vta_ref[kvh, 0:DH, :] = vt[DH * kvh:DH * kvh + DH].astype(BF16)
            vta_ref[kvh, DH:DHA, :] = tail
        for g in range(2):
            sl = slice(128 * g, 128 * g + 128)
            qr2_ref[:, sl] = _rope(qr_ref[:, sl].astype(F32), cos_v, sin_v)
            kr2_ref[:, sl] = _rope(kr_ref[:, sl].astype(F32), cos_v, sin_v) * 0.125

    hm = lambda n: pl.BlockSpec((n, tm, DH), lambda i: (0, i, 0))
    ht = lambda n, r: pl.BlockSpec((n, r, tm), lambda i: (0, 0, i))
    return pl.pallas_call(
        body, name="prep",
        out_shape=[jax.ShapeDtypeStruct((8, DH, S), BF16), jax.ShapeDtypeStruct((2, S, DH), BF16), jax.ShapeDtypeStruct((2, DH, S), BF16),
                   jax.ShapeDtypeStruct((2, S, DH), BF16), jax.ShapeDtypeStruct((2, DHA, S), BF16),
                   jax.ShapeDtypeStruct((S, 256), F32), jax.ShapeDtypeStruct((S, 256), F32)],
        grid=(S // tm,),
        in_specs=[pl.BlockSpec((tm, 512), lambda i: (i, O_QA // 512)), pl.BlockSpec((tm, 256), lambda i: (i, O_KA // 256)),
                  pl.BlockSpec((tm, 256), lambda i: (i, O_QR // 256)), pl.BlockSpec((tm, 256), lambda i: (i, O_KR // 256)),
                  pl.BlockSpec((tm, 128), lambda i: (i, 0)), pl.BlockSpec((tm, 128), lambda i: (i, 0)), _full((1, 128)), _full((1, 128))],
        out_specs=[ht(8, DH), hm(2), ht(2, DH), hm(2), ht(2, DHA), pl.BlockSpec((tm, 256), lambda i: (i, 0)), pl.BlockSpec((tm, 256), lambda i: (i, 0))],
        compiler_params=_cp(("parallel",)),
    )(p, p, p, p, cos, sin, qg, kg)


def _attn_fwd(qt, kh, vta):
    S = qt.shape[2]
    tq, tk = min(1024, S), min(512, S)
    nj = S // tk

    def body(q_ref, k_ref, v_ref, o_ref, ot_ref, lse_ref, m_s, acc_s):
        j = pl.program_id(1)

        @pl.when(j == 0)
        def _():
            m_s[...] = jnp.full_like(m_s, -jnp.inf)
            acc_s[...] = jnp.zeros_like(acc_s)

        m_all = m_s[...]
        st = {0: _dot(k_ref[0], q_ref[0])}
        m_new, acc_new = [], []
        for h in range(8):
            if h + 1 < 8:
                st[h + 1] = _dot(k_ref[(h + 1) // 4], q_ref[h + 1])
            m_old = m_all[h:h + 1, :]
            mn = jnp.maximum(m_old, jnp.max(st[h], axis=0, keepdims=True))
            pt = jnp.exp2((st[h] - mn).astype(BF16))
            acc_new.append(jnp.exp2(m_old - mn) * acc_s[h] + _dot(v_ref[h // 4], pt))
            m_new.append(mn)
            del st[h]
        for h in range(8):
            acc_s[h] = acc_new[h]
            m_s[h:h + 1, :] = m_new[h]

        @pl.when(j == nj - 1)
        def _():
            for h in range(8):
                ot = acc_s[h, 0:DH, :] / acc_s[h, DH:DH + 1, :]
                ot_ref[h] = ot
                o_ref[:, DH * h:DH * h + DH] = ot.T
                lse_ref[h // 4, h % 4:h % 4 + 1, :] = m_s[h:h + 1, :] + jnp.log2(acc_s[h, DH:DH + 1, :])

    return pl.pallas_call(
        body, name="attn_fwd",
        out_shape=[jax.ShapeDtypeStruct((S, 512), F32), jax.ShapeDtypeStruct((8, DH, S), F32), jax.ShapeDtypeStruct((2, 4, S), F32)],
        grid=(S // tq, nj),
        in_specs=[pl.BlockSpec((8, DH, tq), lambda i, j: (0, 0, i)), pl.BlockSpec((2, tk, DH), lambda i, j: (0, j, 0)),
                  pl.BlockSpec((2, DHA, tk), lambda i, j: (0, 0, j))],
        out_specs=[pl.BlockSpec((tq, 512), lambda i, j: (i, 0)), pl.BlockSpec((8, DH, tq), lambda i, j: (0, 0, i)),
                   pl.BlockSpec((2, 4, tq), lambda i, j: (0, 0, i))],
        scratch_shapes=[pltpu.VMEM((8, tq), F32), pltpu.VMEM((8, DHA, tq), F32)],
        compiler_params=_cp(("parallel", "arbitrary"), VMEM_BIG),
    )(qt, kh, vta)


def _ret_tables(wf, wb):
    C = CH

    def body(wf_ref, wb_ref, dc_ref, qdf_ref, qdb_ref, kdf_ref, kdb_ref, a_ref):
        def logsig(w):
            z = jnp.exp(-jnp.abs(w))
            u = 1.0 + z
            l1p = jnp.where(u == 1.0, z, jnp.log(u) * (z / jnp.where(u == 1.0, 1.0, u - 1.0)))
            return jnp.minimum(w, 0.0) - l1p

        lgf, lgb = logsig(wf_ref[...]), logsig(wb_ref[...])
        lane4 = lax.broadcasted_iota(jnp.int32, (1, 4), 1)

        def pick(lg, h):
            return jnp.sum(jnp.where(lane4 == h, lg, 0.0), axis=-1, keepdims=True)

        ii = lax.broadcasted_iota(jnp.int32, (C, C), 0).astype(F32)
        jj = lax.broadcasted_iota(jnp.int32, (C, C), 1).astype(F32)
        dif = ii - jj
        hd = lax.broadcasted_iota(jnp.int32, (C, 256), 1) // DH
        lf_l = jnp.zeros((C, 256), F32)
        lb_l = jnp.zeros((C, 256), F32)
        for h in range(HR):
            lf, lb = pick(lgf, h), pick(lgb, h)
            dc_ref[h] = jnp.where(dif >= 0, jnp.exp(lf * jnp.maximum(dif, 0.0)), jnp.exp(lb * jnp.maximum(-dif, 0.0)))
            lf_l = jnp.where(hd == h, lf, lf_l)
            lb_l = jnp.where(hd == h, lb, lb_l)
            a_ref[h:h + 1, :] = jnp.broadcast_to(jnp.exp(lf * C), (1, 128))
            a_ref[HR + h:HR + h + 1, :] = jnp.broadcast_to(jnp.exp(lb * C), (1, 128))
        ri = lax.broadcasted_iota(jnp.int32, (C, 256), 0).astype(F32)
        qdf_ref[...] = jnp.exp(lf_l * (ri + 1.0))
        qdb_ref[...] = jnp.exp(lb_l * (C - ri))
        kdf_ref[...] = jnp.exp(lf_l * (C - 1.0 - ri))
        kdb_ref[...] = jnp.exp(lb_l * ri)

    t = jax.ShapeDtypeStruct((C, 256), F32)
    return pl.pallas_call(body, name="ret_tables",
                          out_shape=[jax.ShapeDtypeStruct((HR, C, C), F32), t, t, t, t, jax.ShapeDtypeStruct((8, 128), F32)])(wf, wb)


def _ret_states(kr2, p, kdf, kdb, adec):
    S = kr2.shape[0]
    C, N = CH, S // CH
    G = _scan_group(N)
    NG = N // G

    def body(kf_ref, vf_ref, kb_ref, vb_ref, kdf_ref, kdb_ref, a_ref, rf_ref, rb_ref, sf, sb):
        @pl.when(pl.program_id(0) == 0)
        def _():
            sf[...] = jnp.zeros_like(sf)
            sb[...] = jnp.zeros_like(sb)

        kvf, kvb = [], []
        for u in range(G):
            rows = slice(C * u, C * u + C)
            kdfw = (kf_ref[rows, :] * kdf_ref[...]).astype(BF16)
            kdbw = (kb_ref[rows, :] * kdb_ref[...]).astype(BF16)
            vf, vb = vf_ref[rows, :].astype(BF16), vb_ref[rows, :].astype(BF16)
            kvf.append([_dot(kdfw[:, _ks(h)], vf[:, _vs(h)], TN) for h in range(HR)])
            kvb.append([_dot(kdbw[:, _ks(h)], vb[:, _vs(h)], TN) for h in range(HR)])
        for u in range(G):
            rf_ref[u] = sf[...]
            for h in range(HR):
                sf[h] = a_ref[h:h + 1, :] * sf[h] + kvf[u][h]
        for u in reversed(range(G)):
            rb_ref[u] = sb[...]
            for h in range(HR):
                sb[h] = a_ref[HR + h:HR + h + 1, :] * sb[h] + kvb[u][h]

    st = jax.ShapeDtypeStruct((N, HR, DH, DV), F32)
    return pl.pallas_call(
        body, name="ret_states", out_shape=[st, st], grid=(NG,),
        in_specs=[pl.BlockSpec((G * C, 256), lambda t: (t, 0)), pl.BlockSpec((G * C, 512), lambda t: (t, O_VR // 512)),
                  pl.BlockSpec((G * C, 256), lambda t: (NG - 1 - t, 0)), pl.BlockSpec((G * C, 512), lambda t: (NG - 1 - t, O_VR // 512)),
                  _full((C, 256)), _full((C, 256)), _full((8, 128))],
        out_specs=[pl.BlockSpec((G, HR, DH, DV), lambda t: (t, 0, 0, 0)), pl.BlockSpec((G, HR, DH, DV), lambda t: (NG - 1 - t, 0, 0, 0))],
        scratch_shapes=[pltpu.VMEM((HR, DH, DV), F32), pltpu.VMEM((HR, DH, DV), F32)],
        compiler_params=_cp(("arbitrary",)),
    )(kr2, p, kr2, p, kdf, kdb, adec)


def _scan_group(n):
    return 4 if n % 4 == 0 else (2 if n % 2 == 0 else 1)


def _ks(h):
    return slice(DH * h, DH * h + DH)


def _vs(h):
    return slice(DV * h, DV * h + DV)


def _ret_heads_fwd(qb, kb, vb, qfw, qbw, dc_ref, rf_ref, rb_ref, u=0):
    hs = range(HR)
    s = [_dot(qb[:, _ks(h)], kb[:, _ks(h)], NT) for h in hs]
    inter = [_dot(qfw[:, _ks(h)], rf_ref[u, h].astype(BF16)) + _dot(qbw[:, _ks(h)], rb_ref[u, h].astype(BF16)) for h in hs]
    sd = [s[h] * dc_ref[h] for h in hs]
    o = [_dot(sd[h].astype(BF16), vb[:, _vs(h)]) + inter[h] for h in hs]
    return sd, o


def _ret_out(qr2, kr2, p, rf, rb, dc, qdf, qdb, gn):
    S = qr2.shape[0]
    C, N = CH, S // CH

    def body(q_ref, k_ref, v_ref, z_ref, rf_ref, rb_ref, dc_ref, qdf_ref, qdb_ref, gn_ref, yr_ref):
        qv = q_ref[...]
        qb, kb, vb = qv.astype(BF16), k_ref[...].astype(BF16), v_ref[...].astype(BF16)
        qfw, qbw = (qv * qdf_ref[...]).astype(BF16), (qv * qdb_ref[...]).astype(BF16)
        _, o = _ret_heads_fwd(qb, kb, vb, qfw, qbw, dc_ref, rf_ref, rb_ref)
        for h in range(HR):
            vs = _vs(h)
            mu = jnp.mean(o[h], axis=-1, keepdims=True)
            var = jnp.mean(jnp.square(o[h] - mu), axis=-1, keepdims=True)
            on = (o[h] - mu) * lax.rsqrt(var + EPS)
            z = z_ref[:, vs].astype(F32)
            yr_ref[:, vs] = ((on * gn_ref[:, vs]) * (z * _sigmoid(z))).astype(BF16)

    return pl.pallas_call(
        body, name="ret_out", out_shape=jax.ShapeDtypeStruct((S, 512), BF16), grid=(N,),
        in_specs=[pl.BlockSpec((C, 256), lambda t: (t, 0)), pl.BlockSpec((C, 256), lambda t: (t, 0)),
                  pl.BlockSpec((C, 512), lambda t: (t, O_VR // 512)), pl.BlockSpec((C, 512), lambda t: (t, O_ZR // 512)),
                  pl.BlockSpec((1, HR, DH, DV), lambda t: (t, 0, 0, 0)), pl.BlockSpec((1, HR, DH, DV), lambda t: (t, 0, 0, 0)),
                  _full((HR, C, C)), _full((C, 256)), _full((C, 256)), _full((1, 512))],
        out_specs=pl.BlockSpec((C, 512), lambda t: (t, 0)),
        compiler_params=_cp(("parallel",)),
    )(qr2, kr2, p, p, rf, rb, dc, qdf, qdb, gn)


def _mid(x, tgt, mod, g_post, o_att, p, yr, w_pa, w_pr, w_out):
    S = x.shape[0]
    tm = min(256, S)

    def body(x_ref, t_ref, mod_ref, gp_ref, o_ref, za_ref, gl_ref, yr_ref, wpa_ref, wpr_ref, wout_ref,
             dout_ref, do_ref, dpm_ref, dyr_ref, mb_ref, dub_ref, yab_ref, dab_ref, drb_ref, sums_ref):
        @pl.when(pl.program_id(0) == 0)
        def _():
            sums_ref[...] = jnp.zeros_like(sums_ref)

        za = za_ref[...].astype(F32)
        sa = _sigmoid(za)
        sil = za * sa
        ov = o_ref[...]
        ya_b = (ov * sil).astype(BF16)
        yr_b = yr_ref[...]
        av = _dot(ya_b, wpa_ref[...])
        rv = _dot(yr_b, wpr_ref[...])
        ga = _sigmoid(gl_ref[:, :D].astype(F32))
        gr = _sigmoid(gl_ref[:, D:].astype(F32))
        mb = (ga * av + gr * rv).astype(BF16)
        u = _dot(mb, wout_ref[...])
        r2 = lax.rsqrt(jnp.mean(u * u, axis=-1, keepdims=True) + EPS)
        un = u * r2
        gp = gp_ref[...]
        yv = un * gp
        gate = mod_ref[2:3, :]
        err = (x_ref[...] + gate * yv) - t_ref[...]
        dout = err * (1.0 / D)
        dout_ref[...] = dout
        dy = dout * gate
        sums_ref[0:1, :] += jnp.sum(dout * yv, axis=0, keepdims=True)
        sums_ref[1:2, :] += jnp.sum(dy * un, axis=0, keepdims=True)
        sums_ref[2:3, :] += jnp.sum(err * err, axis=0, keepdims=True)
        dyg = dy * gp
        du_b = (r2 * (dyg - un * jnp.mean(dyg * un, axis=-1, keepdims=True))).astype(BF16)
        dm = _dot(du_b, wout_ref[...], NT)
        da_b = (dm * ga).astype(BF16)
        dr_b = (dm * gr).astype(BF16)
        dpm_ref[:, :D] = (dm * av * (ga * (1.0 - ga))).astype(BF16)
        dpm_ref[:, D:2 * D] = (dm * rv * (gr * (1.0 - gr))).astype(BF16)
        dya = _dot(da_b, wpa_ref[...], NT)
        dyr_ref[...] = _dot(dr_b, wpr_ref[...], NT)
        dov = dya * sil
        for g in range(4):
            dt = dov[:, 128 * g:128 * g + 128].T
            do_ref[2 * g] = dt[:DH].astype(BF16)
            do_ref[2 * g + 1] = dt[DH:].astype(BF16)
        dpm_ref[:, 2 * D:] = (dya * ov * (sa * (1.0 + za * (1.0 - sa)))).astype(BF16)
        mb_ref[...] = mb
        dub_ref[...] = du_b
        yab_ref[...] = ya_b
        dab_ref[...] = da_b
        drb_ref[...] = dr_b

    row = lambda w: pl.BlockSpec((tm, w), lambda i: (i, 0))
    sd = lambda w, dt: jax.ShapeDtypeStruct((S, w), dt)
    return pl.pallas_call(
        body, name="mid",
        out_shape=[sd(D, F32), jax.ShapeDtypeStruct((8, DH, S), BF16), sd(2560, BF16), sd(512, F32), sd(D, BF16), sd(D, BF16), sd(512, BF16),
                   sd(D, BF16), sd(D, BF16), jax.ShapeDtypeStruct((8, D), F32)],
        grid=(S // tm,),
        in_specs=[row(D), row(D), _full((3, D)), _full((1, D)), row(512), pl.BlockSpec((tm, 512), lambda i: (i, O_ZA // 512)),
                  pl.BlockSpec((tm, 2048), lambda i: (i, 0)), row(512), _full((512, D)), _full((512, D)), _full((D, D))],
        out_specs=[row(D), pl.BlockSpec((8, DH, tm), lambda i: (0, 0, i)), row(2560), row(512), row(D), row(D), row(512), row(D), row(D),
                   _full((8, D))],
        compiler_params=_cp(("arbitrary",), VMEM_BIG),
    )(x, tgt, mod, g_post, o_att, p, p, yr, w_pa, w_pr, w_out)


def _attn_bwd(qt, kh, kt, vh, dot_, ot, lse):
    S = qt.shape[2]
    tq, tk = min(1024, S), min(1024, S)

    def body(q_ref, k_ref, kt_ref, v_ref, do_ref, o_ref, lse_ref, dq_ref, dk_ref, dv_ref):
        j, i = pl.program_id(0), pl.program_id(1)
        cols = pl.ds(pl.multiple_of(i * tq, tq), tq)
        st = {0: _dot(k_ref[0], q_ref[0])}
        dpt = {0: _dot(v_ref[0], do_ref[0])}
        dk_acc, dv_acc, dqs = [None, None], [None, None], []
        for h in range(8):
            g = h // 4
            if h + 1 < 8:
                st[h + 1] = _dot(k_ref[(h + 1) // 4], q_ref[h + 1])
                dpt[h + 1] = _dot(v_ref[(h + 1) // 4], do_ref[h + 1])
            qt_h, dot_h = q_ref[h], do_ref[h]
            delta = jnp.sum(dot_h.astype(F32) * o_ref[h], axis=0, keepdims=True)
            pt = jnp.exp2((st[h] - lse_ref[g, h % 4:h % 4 + 1, :]).astype(BF16))
            dst = (pt.astype(F32) * (dpt[h] - delta)).astype(BF16)
            dv_h = _dot(dot_h, pt, NT)
            dk_h = _dot(qt_h, dst, NT)
            dqs.append(_dot(kt_ref[g], dst))
            dv_acc[g] = dv_h if dv_acc[g] is None else dv_acc[g] + dv_h
            dk_acc[g] = dk_h if dk_acc[g] is None else dk_acc[g] + dk_h
            del st[h], dpt[h]

        @pl.when(i == 0)
        def _():
            for g in range(2):
                dk_ref[g] = dk_acc[g]
                dv_ref[g] = dv_acc[g]

        @pl.when(i > 0)
        def _():
            for g in range(2):
                dk_ref[g] += dk_acc[g]
                dv_ref[g] += dv_acc[g]

        @pl.when(j == 0)
        def _():
            for h in range(8):
                dq_ref[h, :, cols] = dqs[h]

        @pl.when(j > 0)
        def _():
            for h in range(8):
                dq_ref[h, :, cols] += dqs[h]

    return pl.pallas_call(
        body, name="attn_bwd",
        out_shape=[jax.ShapeDtypeStruct((8, DH, S), F32), jax.ShapeDtypeStruct((2, DH, S), F32), jax.ShapeDtypeStruct((2, DH, S), F32)],
        grid=(S // tk, S // tq),
        in_specs=[pl.BlockSpec((8, DH, tq), lambda j, i: (0, 0, i)), pl.BlockSpec((2, tk, DH), lambda j, i: (0, j, 0)),
                  pl.BlockSpec((2, DH, tk), lambda j, i: (0, 0, j)), pl.BlockSpec((2, tk, DH), lambda j, i: (0, j, 0)),
                  pl.BlockSpec((8, DH, tq), lambda j, i: (0, 0, i)), pl.BlockSpec((8, DH, tq), lambda j, i: (0, 0, i)),
                  pl.BlockSpec((2, 4, tq), lambda j, i: (0, 0, i))],
        out_specs=[pl.BlockSpec((8, DH, S), lambda j, i: (0, 0, 0)), pl.BlockSpec((2, DH, tk), lambda j, i: (0, 0, j)),
                   pl.BlockSpec((2, DH, tk), lambda j, i: (0, 0, j))],
        compiler_params=_cp(("arbitrary", "arbitrary"), VMEM_BIG),
    )(qt, kh, kt, vh, dot_, ot, lse)


def _attn_prep_bwd(dqt, dkt, dvt, p, cos, sin, qg, kg):
    S = dqt.shape[2]
    tm = min(512, S)

    def body(dq_ref, dk_ref, dv_ref, qa_ref, ka_ref, cos_ref, sin_ref, qg_ref, kg_ref, dp_ref, gs_ref):
        @pl.when(pl.program_id(0) == 0)
        def _():
            gs_ref[...] = jnp.zeros_like(gs_ref)

        cos_v, sin_v = cos_ref[...], sin_ref[...]

        def pair(ref, a):
            return jnp.concatenate([ref[a], ref[a + 1]], axis=0).T

        def norm_bwd(dyv, xv, gv, row):
            r = lax.rsqrt(_head_mean(xv * xv) + EPS)
            xn = xv * r
            dxh = _rope_t(dyv, cos_v, sin_v)
            gs_ref[row:row + 1, :] += jnp.sum(dxh * xn, axis=0, keepdims=True)
            dg = dxh * gv
            return r * (dg - xn * _head_mean(dg * xn))

        for g in range(4):
            sl = slice(128 * g, 128 * g + 128)
            dp_ref[:, sl] = norm_bwd(pair(dq_ref, 2 * g) * 0.125, qa_ref[:, sl].astype(F32), qg_ref[...], 0).astype(BF16)
        dp_ref[:, 512:640] = norm_bwd(pair(dk_ref, 0) * LN2, ka_ref[...].astype(F32), kg_ref[...], 1).astype(BF16)
        dp_ref[:, 640:768] = pair(dv_ref, 0).astype(BF16)

    ht = lambda n: pl.BlockSpec((n, DH, tm), lambda i: (0, 0, i))
    return pl.pallas_call(
        body, name="attn_prep_bwd", out_shape=[jax.ShapeDtypeStruct((S, 768), BF16), jax.ShapeDtypeStruct((8, 128), F32)],
        grid=(S // tm,),
        in_specs=[ht(8), ht(2), ht(2),
                  pl.BlockSpec((tm, 512), lambda i: (i, O_QA // 512)), pl.BlockSpec((tm, 128), lambda i: (i, O_KA // 128)),
                  pl.BlockSpec((tm, 128), lambda i: (i, 0)), pl.BlockSpec((tm, 128), lambda i: (i, 0)), _full((1, 128)), _full((1, 128))],
        out_specs=[pl.BlockSpec((tm, 768), lambda i: (i, 0)), _full((8, 128))],
        compiler_params=_cp(("arbitrary",)),
    )(dqt, dkt, dvt, p, p, cos, sin, qg, kg)


def _ret_bwd_chunk(qr2, kr2, p, rf, rb, dc, qdf, qdb, gn, dyr, cos, sin):
    S = qr2.shape[0]
    C, N = CH, S // CH
    G = 1

    def body(q_ref, k_ref, v_ref, z_ref, rf_ref, rb_ref, dc_ref, qdf_ref, qdb_ref, gn_ref, dyr_ref, cos_ref, sin_ref,
             dpa_ref, dk_ref, dv_ref, drf_ref, drb_ref, dgn_ref, dlg_ref, dqs):
        @pl.when(pl.program_id(0) == 0)
        def _():
            dgn_ref[...] = jnp.zeros_like(dgn_ref)
            dlg_ref[...] = jnp.zeros_like(dlg_ref)

        ii = lax.broadcasted_iota(jnp.int32, (C, C), 0).astype(F32)
        jj = lax.broadcasted_iota(jnp.int32, (C, C), 1).astype(F32)
        dif = ii - jj
        ri = lax.broadcasted_iota(jnp.int32, (C, 1), 0).astype(F32)
        hs = range(HR)
        for u in range(G):
            rows = slice(C * u, C * u + C)
            qv = q_ref[rows, :]
            qb, kb, vb = qv.astype(BF16), k_ref[rows, :].astype(BF16), v_ref[rows, :].astype(BF16)
            qf32, qb32 = qv * qdf_ref[...], qv * qdb_ref[...]
            qfw, qbw = qf32.astype(BF16), qb32.astype(BF16)
            sd, o = _ret_heads_fwd(qb, kb, vb, qfw, qbw, dc_ref, rf_ref, rb_ref, u)
            do_b = []
            for h in hs:
                vs = _vs(h)
                mu = jnp.mean(o[h], axis=-1, keepdims=True)
                rstd = lax.rsqrt(jnp.mean(jnp.square(o[h] - mu), axis=-1, keepdims=True) + EPS)
                on = (o[h] - mu) * rstd
                z = z_ref[rows, vs].astype(F32)
                sz = _sigmoid(z)
                dy = dyr_ref[rows, vs]
                gnv = gn_ref[:, vs]
                dpa_ref[rows, 256 + DV * h:256 + DV * h + DV] = (dy * (on * gnv) * (sz * (1.0 + z * (1.0 - sz)))).astype(BF16)
                dys = dy * (z * sz)
                dgn_ref[:, vs] += jnp.sum(dys * on, axis=0, keepdims=True)
                don = dys * gnv
                do = rstd * (don - jnp.mean(don, axis=-1, keepdims=True) - on * jnp.mean(don * on, axis=-1, keepdims=True))
                do_b.append(do.astype(BF16))
            dpm = [_dot(do_b[h], vb[:, _vs(h)], NT) for h in hs]
            dqf = [_dot(do_b[h], rf_ref[u, h].astype(BF16), NT) for h in hs]
            dqb = [_dot(do_b[h], rb_ref[u, h].astype(BF16), NT) for h in hs]
            for h in hs:
                dv_ref[rows, _vs(h)] = _dot(sd[h].astype(BF16), do_b[h], TN)
                drf_ref[u, h] = _dot(qfw[:, _ks(h)], do_b[h], TN)
                drb_ref[u, h] = _dot(qbw[:, _ks(h)], do_b[h], TN)
            dsd = [(dpm[h] * dc_ref[h]).astype(BF16) for h in hs]
            for h in hs:
                ks = _ks(h)
                dqs[rows, ks] = _dot(dsd[h], kb[:, ks]) + dqf[h] * qdf_ref[:, ks] + dqb[h] * qdb_ref[:, ks]
                dk_ref[rows, ks] = _dot(dsd[h], qb[:, ks], TN)
            for h in hs:
                ks = _ks(h)
                e = dpm[h] * sd[h]
                lf = _sum11(e * jnp.maximum(dif, 0.0)) + _sum11(jnp.sum(qf32[:, ks] * dqf[h], axis=-1, keepdims=True) * (ri + 1.0))
                lb = _sum11(e * jnp.maximum(-dif, 0.0)) + _sum11(jnp.sum(qb32[:, ks] * dqb[h], axis=-1, keepdims=True) * (C - ri))
                dlg_ref[h:h + 1, :] += jnp.broadcast_to(lf, (1, 128))
                dlg_ref[HR + h:HR + h + 1, :] += jnp.broadcast_to(lb, (1, 128))
            for g in range(2):
                sl = slice(128 * g, 128 * g + 128)
                dpa_ref[rows, sl] = _rope_t(dqs[rows, sl], cos_ref[rows, :], sin_ref[rows, :]).astype(BF16)

    st = jax.ShapeDtypeStruct((N, HR, DH, DV), F32)
    stb = lambda: pl.BlockSpec((G, HR, DH, DV), lambda t: (t, 0, 0, 0))
    row = lambda w, off=0: pl.BlockSpec((G * C, w), lambda t: (t, off))
    return pl.pallas_call(
        body, name="ret_bwd_chunk",
        out_shape=[jax.ShapeDtypeStruct((S, 768), BF16), jax.ShapeDtypeStruct((S, 256), F32), jax.ShapeDtypeStruct((S, 512), F32), st, st,
                   jax.ShapeDtypeStruct((1, 512), F32), jax.ShapeDtypeStruct((8, 128), F32)],
        grid=(N // G,),
        in_specs=[row(256), row(256), row(512, O_VR // 512), row(512, O_ZR // 512),
                  stb(), stb(), _full((HR, C, C)), _full((C, 256)), _full((C, 256)), _full((1, 512)), row(512), row(128), row(128)],
        out_specs=[row(768), row(256), row(512), stb(), stb(), _full((1, 512)), _full((8, 128))],
        scratch_shapes=[pltpu.VMEM((G * C, 256), F32)],
        compiler_params=_cp(("arbitrary",)),
    )(qr2, kr2, p, p, rf, rb, dc, qdf, qdb, gn, dyr, cos, sin)


def _ret_bwd_scan(kr2, p, rf, rb, drf, drb, kdf, kdb, adec):
    S = kr2.shape[0]
    C, N = CH, S // CH
    G = _scan_group(N)
    NG = N // G

    def body(kf_ref, vf_ref, kb_ref, vb_ref, rf_ref, rb_ref, drf_ref, drb_ref, kdf_ref, kdb_ref, a_ref,
             dkf_ref, dkb_ref, dvf_ref, dvb_ref, dlg_ref, gf, gb):
        @pl.when(pl.program_id(0) == 0)
        def _():
            gf[...] = jnp.zeros_like(gf)
            gb[...] = jnp.zeros_like(gb)
            dlg_ref[...] = jnp.zeros_like(dlg_ref)

        ri = lax.broadcasted_iota(jnp.int32, (C, 1), 0).astype(F32)

        def one(k_ref, v_ref, r_ref, dr_ref, kd_ref, g_s, dk_ref, dv_ref, row0, wexp, order):
            g = [g_s[h] for h in range(HR)]
            lgs = [jnp.zeros((1, 1), F32) for _ in range(HR)]
            for u in order:
                rows = slice(C * u, C * u + C)
                kd32 = k_ref[rows, :] * kd_ref[...]
                kdw = kd32.astype(BF16)
                vb = v_ref[rows, :].astype(BF16)
                for h in range(HR):
                    ks, vs = _ks(h), _vs(h)
                    g_b = g[h].astype(BF16)
                    dkd = _dot(vb[:, vs], g_b, NT)
                    dk_ref[rows, ks] = dkd * kd_ref[:, ks]
                    dv_ref[rows, vs] = _dot(kdw[:, ks], g_b)
                    av = a_ref[row0 + h:row0 + h + 1, :]
                    lgs[h] = lgs[h] + (_sum11(jnp.sum(kd32[:, ks] * dkd, axis=-1, keepdims=True) * wexp)
                                       + C * av[:, 0:1] * _sum11(r_ref[u, h] * g[h]))
                    g[h] = dr_ref[u, h] + av * g[h]
            for h in range(HR):
                g_s[h] = g[h]
                dlg_ref[row0 + h:row0 + h + 1, :] += jnp.broadcast_to(lgs[h], (1, 128))

        one(kf_ref, vf_ref, rf_ref, drf_ref, kdf_ref, gf, dkf_ref, dvf_ref, 0, C - 1.0 - ri, list(reversed(range(G))))
        one(kb_ref, vb_ref, rb_ref, drb_ref, kdb_ref, gb, dkb_ref, dvb_ref, HR, ri, list(range(G)))

    fwd = lambda w, off=0: pl.BlockSpec((G * C, w), lambda t: (NG - 1 - t, off))
    bwd = lambda w, off=0: pl.BlockSpec((G * C, w), lambda t: (t, off))
    stf = lambda: pl.BlockSpec((G, HR, DH, DV), lambda t: (NG - 1 - t, 0, 0, 0))
    stb = lambda: pl.BlockSpec((G, HR, DH, DV), lambda t: (t, 0, 0, 0))
    return pl.pallas_call(
        body, name="ret_bwd_scan",
        out_shape=[jax.ShapeDtypeStruct((S, 256), F32), jax.ShapeDtypeStruct((S, 256), F32), jax.ShapeDtypeStruct((S, 512), F32),
                   jax.ShapeDtypeStruct((S, 512), F32), jax.ShapeDtypeStruct((8, 128), F32)],
        grid=(NG,),
        in_specs=[fwd(256), fwd(512, O_VR // 512), bwd(256), bwd(512, O_VR // 512), stf(), stb(), stf(), stb(),
                  _full((C, 256)), _full((C, 256)), _full((8, 128))],
        out_specs=[fwd(256), bwd(256), fwd(512), bwd(512), _full((8, 128))],
        scratch_shapes=[pltpu.VMEM((HR, DH, DV), F32), pltpu.VMEM((HR, DH, DV), F32)],
        compiler_params=_cp(("arbitrary",)),
    )(kr2, p, kr2, p, rf, rb, drf, drb, kdf, kdb, adec)


def _ret_bwd_final(dk_i, dkf, dkb, dv_i, dvf, dvb, cos, sin):
    S = dk_i.shape[0]
    tm = min(512, S)

    def body(a_ref, b_ref, c_ref, d_ref, e_ref, f_ref, cos_ref, sin_ref, o_ref):
        o_ref[:, :512] = (d_ref[...] + e_ref[...] + f_ref[...]).astype(BF16)
        cos_v, sin_v = cos_ref[...], sin_ref[...]
        for g in range(2):
            sl = slice(128 * g, 128 * g + 128)
            dk = a_ref[:, sl] + b_ref[:, sl] + c_ref[:, sl]
            o_ref[:, 512 + 128 * g:512 + 128 * g + 128] = (_rope_t(dk, cos_v, sin_v) * 0.125).astype(BF16)

    row = lambda w: pl.BlockSpec((tm, w), lambda i: (i, 0))
    return pl.pallas_call(
        body, name="ret_bwd_final", out_shape=jax.ShapeDtypeStruct((S, 768), BF16), grid=(S // tm,),
        in_specs=[row(256), row(256), row(256), row(512), row(512), row(512), row(128), row(128)], out_specs=row(768),
        compiler_params=_cp(("parallel",)),
    )(dk_i, dkf, dkb, dv_i, dvf, dvb, cos, sin)


def _bwd_in(dpm, dpa, dpra, dprb, w_p, x, dout, mod, g_pre):
    S = x.shape[0]
    tm = min(256, S)

    def body(a_ref, b_ref, c_ref, d_ref, w_ref, x_ref, dout_ref, mod_ref, g_ref, gx_ref, sums_ref):
        @pl.when(pl.program_id(0) == 0)
        def _():
            sums_ref[...] = jnp.zeros_like(sums_ref)

        dh = (_dot(a_ref[...], w_ref[:, :O_QA], NT) + _dot(b_ref[...], w_ref[:, O_QA:O_QR], NT)
              + _dot(c_ref[...], w_ref[:, O_QR:O_VR], NT) + _dot(d_ref[...], w_ref[:, O_VR:], NT))
        xv = x_ref[...]
        r = lax.rsqrt(jnp.mean(xv * xv, axis=-1, keepdims=True) + EPS)
        xn = xv * r
        gv = g_ref[...]
        sc1 = 1.0 + mod_ref[1:2, :]
        sums_ref[0:1, :] += jnp.sum(dh, axis=0, keepdims=True)
        sums_ref[1:2, :] += jnp.sum(dh * (xn * gv), axis=0, keepdims=True)
        sums_ref[2:3, :] += jnp.sum(dh * xn, axis=0, keepdims=True) * sc1
        dxn = dh * (gv * sc1)
        gx_ref[...] = dout_ref[...] + r * (dxn - xn * jnp.mean(dxn * xn, axis=-1, keepdims=True))

    row = lambda w: pl.BlockSpec((tm, w), lambda i: (i, 0))
    return pl.pallas_call(
        body, name="bwd_in", out_shape=[jax.ShapeDtypeStruct((S, D), F32), jax.ShapeDtypeStruct((8, D), F32)], grid=(S // tm,),
        in_specs=[row(2560), row(768), row(768), row(768), _full((D, P_W)), row(D), row(D), _full((3, D)), _full((1, D))],
        out_specs=[row(D), _full((8, D))],
        compiler_params=_cp(("arbitrary",), VMEM_BIG),
    )(dpm, dpa, dpra, dprb, w_p, x, dout, mod, g_pre)


SMALL = ("b_ada", "g_pre", "qn_g", "kn_g", "w_dec_f", "w_dec_b", "gn_g", "g_post")


def _small_update(gathered, wmv):
    ns = len(SMALL)

    def body(*refs):
        gin_ref, gmid_ref, ggn_ref, gatt_ref, gl1_ref, gl2_ref = refs[:6]
        wmv_refs = refs[6:6 + 3 * ns]
        loss_ref = refs[6 + 3 * ns]
        out_refs = refs[7 + 3 * ns:]

        def dsum(ref, r=None):
            rows = slice(None) if r is None else slice(r, r + 1)
            acc = ref[0, rows, :]
            for d in range(1, NDEV):
                acc = acc + ref[d, rows, :]
            return acc

        s_lg = dsum(gl1_ref) + dsum(gl2_ref)
        loss_ref[...] = (0.5 / D) * jnp.sum(dsum(gmid_ref, 2), axis=-1, keepdims=True)
        eye = lax.broadcasted_iota(jnp.int32, (8, 128), 0) == lax.broadcasted_iota(jnp.int32, (8, 128), 1)
        dlg = jnp.sum(jnp.where(eye, s_lg, 0.0), axis=0, keepdims=True)
        w_f, w_b = wmv_refs[3 * SMALL.index("w_dec_f")][...], wmv_refs[3 * SMALL.index("w_dec_b")][...]
        s_q, s_k = dsum(gatt_ref, 0), dsum(gatt_ref, 1)
        grads = dict(
            b_ada=jnp.concatenate([dsum(gin_ref, 0), dsum(gin_ref, 1), dsum(gmid_ref, 0)], axis=1),
            g_pre=dsum(gin_ref, 2), g_post=dsum(gmid_ref, 1), gn_g=dsum(ggn_ref),
            qn_g=s_q[:, :DH] + s_q[:, DH:], kn_g=s_k[:, :DH] + s_k[:, DH:],
            w_dec_f=dlg[:, 0:HR] * _sigmoid(-w_f), w_dec_b=dlg[:, HR:2 * HR] * _sigmoid(-w_b))
        for i, nme in enumerate(SMALL):
            g = grads[nme]
            w_ref, m_ref, v_ref = wmv_refs[3 * i:3 * i + 3]
            g_ref, d_ref, nm_ref, nv_ref = out_refs[4 * i:4 * i + 4]
            g_ref[...] = g
            m2 = ADAM_B1 * m_ref[...] + (1.0 - ADAM_B1) * g
            v2 = ADAM_B2 * v_ref[...] + (1.0 - ADAM_B2) * jnp.square(g)
            m_hat = m2 / (1.0 - ADAM_B1 ** ADAM_STEP)
            v_hat = v2 / (1.0 - ADAM_B2 ** ADAM_STEP)
            d_ref[...] = -ADAM_LR * (m_hat / (jnp.sqrt(v_hat) + ADAM_EPS) + ADAM_WD * w_ref[...])
            nm_ref[...] = m2
            nv_ref[...] = v2

    out_shape = [jax.ShapeDtypeStruct((1, 1), F32)]
    for i in range(ns):
        out_shape += [jax.ShapeDtypeStruct(wmv[3 * i].shape, F32)] * 4
    return pl.pallas_call(body, name="small_update", out_shape=out_shape)(*gathered, *wmv)


def _adamw(parts, w, m, v, name):
    n, R, L = parts.shape
    tr = 256 if (R % 256 == 0 and R > 256) else R

    def body(p_ref, w_ref, m_ref, v_ref, g_ref, d_ref, nm_ref, nv_ref):
        g = p_ref[0].astype(F32)
        for k in range(1, n):
            g = g + p_ref[k].astype(F32)
        g_ref[...] = g
        m2 = ADAM_B1 * m_ref[...] + (1.0 - ADAM_B1) * g
        v2 = ADAM_B2 * v_ref[...] + (1.0 - ADAM_B2) * jnp.square(g)
        m_hat = m2 / (1.0 - ADAM_B1 ** ADAM_STEP)
        v_hat = v2 / (1.0 - ADAM_B2 ** ADAM_STEP)
        d_ref[...] = -ADAM_LR * (m_hat / (jnp.sqrt(v_hat) + ADAM_EPS) + ADAM_WD * w_ref[...])
        nm_ref[...] = m2
        nv_ref[...] = v2

    blk = pl.BlockSpec((tr, L), lambda i: (i, 0))
    o = jax.ShapeDtypeStruct((R, L), F32)
    return pl.pallas_call(
        body, name=name, out_shape=[o, o, o, o], grid=(R // tr,),
        in_specs=[pl.BlockSpec((n, tr, L), lambda i: (0, i, 0)), blk, blk, blk], out_specs=[blk, blk, blk, blk],
        compiler_params=_cp(("parallel",), VMEM_BIG),
    )(parts, w, m, v)


def _rope_tables(S):
    f = np.float32
    t = np.arange(S)
    row, col = (t // 64).astype(f), (t % 64).astype(f)
    half = DH // 2
    inv = np.power(f(ROPE_THETA), -np.arange(0, half, 2, dtype=f) / f(half)).astype(f)
    ar, ac = (row[:, None] * inv[None, :]).astype(f), (col[:, None] * inv[None, :]).astype(f)
    cos64 = np.concatenate([np.cos(ar), np.cos(ar), np.cos(ac), np.cos(ac)], axis=1).astype(f)
    sin64 = np.concatenate([-np.sin(ar), np.sin(ar), -np.sin(ac), np.sin(ac)], axis=1).astype(f)
    return jnp.asarray(np.tile(cos64, (1, 2))), jnp.asarray(np.tile(sin64, (1, 2)))


def _to_p_order(w_orig):
    return jnp.concatenate([w_orig[:, ORIG[n][0]:ORIG[n][1]] for n in P_ORDER], axis=1)


def _pad_lanes(v, n):
    return jnp.pad(v, ((0, 0), (0, n - v.shape[1])))


def kernel(x, c, w_ada, b_ada, g_pre, w_in, qn_g, kn_g, w_dec_f, w_dec_b, gn_g, w_pa, w_pr, w_out, g_post, loss_target, m_w_ada, m_b_ada, m_g_pre, m_w_in, m_qn_g, m_kn_g, m_w_dec_f, m_w_dec_b, m_gn_g, m_w_pa, m_w_pr, m_w_out, m_g_post, v_w_ada, v_b_ada, v_g_pre, v_w_in, v_qn_g, v_kn_g, v_w_dec_f, v_w_dec_b, v_gn_g, v_w_pa, v_w_pr, v_w_out, v_g_post):
    S = x.shape[1]
    me = 4 * lax.axis_index("x") + 2 * lax.axis_index("y") + lax.axis_index("c")
    xs, tgt = x[0], loss_target[0]
    ncol_ada = w_ada.shape[2]
    ncol_in = w_in.shape[2]

    b_ada_s = lax.dynamic_slice(b_ada, (0, me * ncol_ada), (1, ncol_ada))
    mod_all, c_act, (wg_in,) = _prologue(jnp.pad(c, ((0, 7), (0, 0))), w_ada[0], b_ada_s, [w_in[0].astype(BF16)])
    mod = lax.dynamic_index_in_dim(mod_all, me, axis=1, keepdims=False).reshape(3, D)
    w_p = _to_p_order(wg_in.transpose(1, 0, 2).reshape(D, NDEV * ncol_in))
    all_dev = tuple(range(NDEV))
    st_w, tok_w = _xchg_start([(w_pa[0].astype(BF16)[None], all_dev), (w_pr[0].astype(BF16)[None], all_dev),
                               (w_out[0].astype(BF16)[None], all_dev)], "wgather_start")

    cos, sin = _rope_tables(S)
    qg, kg = jnp.tile(qn_g, (1, 2)), jnp.tile(kn_g, (1, 2))

    p, h = _fwd_in(xs, mod, g_pre + tok_w[0:1, 0:1], w_p)
    qt, kh, kt, vh, vta, qr2, kr2 = _prep(p, cos, sin, qg, kg)
    o_att, o_t, lse = _attn_fwd(qt, kh, vta)
    dc, qdf, qdb, kdf, kdb, adec = _ret_tables(w_dec_f, w_dec_b)
    rf, rb = _ret_states(kr2, p, kdf, kdb, adec)
    yr = _ret_out(qr2, kr2, p, rf, rb, dc, qdf, qdb, gn_g)
    wg_pa, wg_pr, wg_out = _xchg_wait([st_w], st_w["lands"], [[0, 1, 2]], yr, "wgather_wait")
    w_pa_f = wg_pa.transpose(1, 0, 2).reshape(512, D)
    w_pr_f = wg_pr.transpose(1, 0, 2).reshape(512, D)
    w_out_f = wg_out.reshape(D, D)

    dout, do, dpm, dyr, mb, dub, yab, dab, drb_, sums_mid = _mid(xs, tgt, mod, g_post, o_att, p, yr, w_pa_f, w_pr_f, w_out_f)
    gw_out = _mm_tn(mb, dub, "gw_out", BF16)
    gw_pa = _mm_tn(yab, dab, "gw_pa", BF16)
    gw_pr = _mm_tn(yr, drb_, "gw_pr", BF16)
    gi_m = _mm_tn(h, dpm, "gw_in_mid", BF16)

    def shards(cols, nd):
        return cols.reshape(D, nd, ncol_in).transpose(1, 0, 2)

    st_a, tok_a = _xchg_start([
        (gw_out.reshape(NDEV, 128, D), all_dev),
        (gw_pa.reshape(512, NDEV, 128).transpose(1, 0, 2), all_dev),
        (gw_pr.reshape(512, NDEV, 128).transpose(1, 0, 2), all_dev),
        (shards(gi_m[:, 224:2048], 3), (5, 6, 7))], "xchg_start_a",
        lands=[None, None, None, jnp.zeros((NDEV, D, ncol_in), BF16)])
    dqt, dkt, dvt = _attn_bwd(qt, kh, kt, vh, do, o_t, lse + tok_a[0, 0])
    dpa, gs_att = _attn_prep_bwd(dqt, dkt, dvt, p, cos, sin, qg, kg)
    gi_a = _mm_tn(h, dpa, "gw_in_att", BF16)
    st_b, tok_b = _xchg_start([(shards(jnp.concatenate([gi_a, gi_m[:, 2048:2496]], axis=1), 2), (0, 1))], "xchg_start_b",
                              lands=[st_a["lands"][3]])
    dpra, dk_i, dv_i, drf, drb, dgn, dlg1 = _ret_bwd_chunk(qr2, kr2, p, rf, rb, dc, qdf, qdb, gn_g + tok_b[0:1, 0:1], dyr, cos, sin)
    dkf, dkb, dvf, dvb, dlg2 = _ret_bwd_scan(kr2, p, rf, rb, drf, drb, kdf, kdb, adec)
    dprb = _ret_bwd_final(dk_i, dkf, dkb, dv_i, dvf, dvb, cos, sin)
    gi_ra = _mm_tn(h, dpra, "gw_in_reta", BF16)
    gi_rb = _mm_tn(h, dprb, "gw_in_retb", BF16)
    chip_c = _pair_reduce(shards(jnp.concatenate([gi_m[:, 2496:2560], gi_ra[:, :256], gi_rb[:, 512:768], gi_rb[:, :512],
                                                  gi_ra[:, 256:768], gi_m[:, :224]], axis=1), 3), (2, 3, 4), "pair_reduce_c")
    st_c, tok_c = _xchg_start([(chip_c, (2, 3, 4, "same core"))], "xchg_start_c", lands=[st_b["lands"][0]])
    grad_x, sums_in = _bwd_in(dpm, dpa, dpra, dprb, w_p, xs, dout, mod, g_pre + tok_c[0:1, 0:1])

    gathered = _small_allgather([sums_in, sums_mid, dgn, gs_att, dlg1, dlg2], "ag_small")
    given = dict(b_ada=(b_ada, m_b_ada, v_b_ada), g_pre=(g_pre, m_g_pre, v_g_pre), qn_g=(qn_g, m_qn_g, v_qn_g), kn_g=(kn_g, m_kn_g, v_kn_g),
                 w_dec_f=(w_dec_f, m_w_dec_f, v_w_dec_f), w_dec_b=(w_dec_b, m_w_dec_b, v_w_dec_b), gn_g=(gn_g, m_gn_g, v_gn_g),
                 g_post=(g_post, m_g_post, v_g_post))
    small = _small_update(gathered, [a for nme in SMALL for a in given[nme]])
    loss = small[0][0, 0]

    g_in_all, g_mid_all = gathered[0], gathered[1]
    dmod_all = lax.dynamic_slice(jnp.concatenate([g_in_all[:, 0, :], g_in_all[:, 1, :], g_mid_all[:, 0, :]], axis=1),
                                 (0, me * ncol_ada), (NDEV, ncol_ada))
    g_ada = _mm_tn(c_act, jnp.pad(dmod_all, ((0, 8), (0, 0))).astype(BF16), "gw_ada")

    ada = _adamw(g_ada[None], w_ada[0], m_w_ada[0], v_w_ada[0], "adamw_ada")
    rs_out, rs_pa, rs_pr, rs_in = _xchg_wait([st_a, st_b, st_c], list(st_a["lands"][:3]) + [st_c["lands"][0]],
                                             [[0, 1, 2, 3], [3], [3]], ada[1], "xchg_wait")
    res = dict(
        w_ada=ada,
        w_in=_adamw(rs_in, w_in[0], m_w_in[0], v_w_in[0], "adamw_in"),
        w_pa=_adamw(rs_pa, w_pa[0], m_w_pa[0], v_w_pa[0], "adamw_pa"),
        w_pr=_adamw(rs_pr, w_pr[0], m_w_pr[0], v_w_pr[0], "adamw_pr"),
        w_out=_adamw(rs_out, w_out[0], m_w_out[0], v_w_out[0], "adamw_out"),
    )
    names = ["w_ada", "b_ada", "g_pre", "w_in", "qn_g", "kn_g", "w_dec_f", "w_dec_b", "gn_g", "w_pa", "w_pr", "w_out", "g_post"]
    outs = [[], [], [], []]
    for nme in names:
        for q in range(4):
            if nme in res:
                outs[q].append(res[nme][q][None])
            else:
                outs[q].append(small[1 + 4 * SMALL.index(nme) + q])
    return (loss, grad_x[None], *outs[0], *outs[1], *outs[2], *outs[3])
```

```python
import jax
import jax.numpy as jnp
import numpy as np
from jax import lax
from jax.experimental import pallas as pl
from jax.experimental.pallas import tpu as pltpu

F32, BF16 = jnp.float32, jnp.bfloat16
D = 1024
DH = 64
DHA = 80
DV = 128
LOG2E = 1.4426950408889634
LN2 = 0.6931471805599453
HR = 4
CH = 128
EPS = 1e-6
ROPE_THETA = 10000.0
NDEV = 8
O_GL, O_ZA, O_QA, O_KA, O_VA, O_QR, O_ZR, O_VR, O_KR, P_W = 0, 2048, 2560, 3072, 3200, 3328, 3584, 4096, 4608, 4864
ORIG = dict(qa=(0, 512), ka=(512, 640), va=(640, 768), za=(768, 1280), qr=(1280, 1536), kr=(1536, 1792),
            vr=(1792, 2304), zr=(2304, 2816), gl=(2816, 4864))
P_ORDER = ("gl", "za", "qa", "ka", "va", "qr", "zr", "vr", "kr")
ADAM_LR, ADAM_B1, ADAM_B2, ADAM_EPS, ADAM_WD, ADAM_STEP = 0.001, 0.9, 0.999, 1e-08, 0.01, 10
VMEM_BIG = 56 * 1024 * 1024
MESH = pl.DeviceIdType.MESH

NT = (((1,), (1,)), ((), ()))
TN = (((0,), (0,)), ((), ()))


def _dot(a, b, dims=None):
    if dims is None:
        return jnp.dot(a, b, preferred_element_type=F32)
    return lax.dot_general(a, b, dims, preferred_element_type=F32)


def _cp(sem=None, vmem=None):
    kw = {}
    if sem is not None:
        kw["dimension_semantics"] = sem
    if vmem is not None:
        kw["vmem_limit_bytes"] = vmem
    return pltpu.CompilerParams(**kw)


def _sigmoid(z):
    return 1.0 / (1.0 + jnp.exp(-z))


def _sum11(m):
    return jnp.sum(jnp.sum(m, axis=-1, keepdims=True), axis=0, keepdims=True)


def _full(shape):
    n = len(shape)
    return pl.BlockSpec(shape, lambda *_: (0,) * n)


def _my_pos():
    return lax.axis_index("x"), lax.axis_index("y"), lax.axis_index("c")


def _peer(k, x, y, c):
    return ((1 - x) if k & 4 else x, (1 - y) if k & 2 else y, (1 - c) if k & 1 else c)


def _small_allgather(vs, name):
    n = len(vs)

    def body(*refs):
        v_refs, out_refs = refs[:n], refs[n:2 * n]
        send_sems, recv_sems = refs[2 * n:]
        x, y, c = _my_pos()
        me = 4 * x + 2 * y + c
        cps = []
        for a in range(n):
            out_refs[a][me] = v_refs[a][...]
            for k in range(1, NDEV):
                cp = pltpu.make_async_remote_copy(src_ref=v_refs[a], dst_ref=out_refs[a].at[me], send_sem=send_sems.at[a, k - 1],
                                                  recv_sem=recv_sems.at[a, k - 1], device_id=_peer(k, x, y, c), device_id_type=MESH)
                cp.start()
                cps.append(cp)
        for cp in cps:
            cp.wait()

    vm = pl.BlockSpec(memory_space=pltpu.VMEM)
    return pl.pallas_call(
        body, name=name, out_shape=[jax.ShapeDtypeStruct((NDEV,) + v.shape, v.dtype) for v in vs],
        in_specs=[vm] * n, out_specs=[vm] * n,
        scratch_shapes=[pltpu.SemaphoreType.DMA((n, NDEV - 1)), pltpu.SemaphoreType.DMA((n, NDEV - 1))],
    )(*vs)


def _prologue(c8, w_ada_s, b_ada_s, arrs):
    n = len(arrs)
    ncol = w_ada_s.shape[1]

    def body(*refs):
        c_ref, wa_ref, ba_ref = refs[:3]
        ins = refs[3:3 + n]
        mod_ref, cact_ref = refs[3 + n:5 + n]
        outs = refs[5 + n:5 + 2 * n]
        call_ref, send_sems, recv_sems, local_sems, s_send, s_recv = refs[5 + 2 * n:]
        x, y, c = _my_pos()
        me, sibling = (x, y, c), (x, y, 1 - c)
        chips = [(1 - x, y), (x, 1 - y), (1 - x, 1 - y)]
        me_i = 4 * x + 2 * y + c

        def small_gather(src_ref, dst_ref, row):
            cps = []
            for k in range(1, NDEV):
                cp = pltpu.make_async_remote_copy(src_ref=src_ref, dst_ref=dst_ref.at[me_i], send_sem=s_send.at[row, k - 1],
                                                  recv_sem=s_recv.at[row, k - 1], device_id=_peer(k, x, y, c), device_id_type=MESH)
                cp.start()
                cps.append(cp)
            return cps

        def blk(a, px, py, pc):
            return outs[a].at[4 * px + 2 * py + pc]

        def copy(a, k, block, to, src=None):
            return pltpu.make_async_remote_copy(src_ref=blk(a, *block) if src is None else src, dst_ref=blk(a, *block),
                                                send_sem=send_sems.at[a, k], recv_sem=recv_sems.at[a, k], device_id=to, device_id_type=MESH)

        call_ref[me_i] = c_ref[...]
        for cp in small_gather(c_ref, call_ref, 0):
            cp.wait()

        local, sent = [], []
        for a in range(n):
            mine = pltpu.make_async_copy(ins[a], blk(a, *me), local_sems.at[a])
            mine.start()
            local.append(mine)
            first = [copy(a, 0, me, sibling, src=ins[a])] + [copy(a, 1 + j, me, (*chip, c), src=ins[a]) for j, chip in enumerate(chips)]
            for cp in first:
                cp.start()
            sent += first

        cv = call_ref[:, 0, :]
        ca = jnp.concatenate([cv * _sigmoid(cv), jnp.zeros_like(cv)], axis=0).astype(BF16)
        cact_ref[...] = ca
        mod_ref[me_i] = (_dot(ca, wa_ref[...].astype(BF16)) + ba_ref[...])[:8]
        mod_copies = small_gather(mod_ref.at[me_i], mod_ref, 1)

        for j, chip in enumerate(chips):
            for a in range(n):
                copy(a, 1 + j, (*chip, c), me).wait_recv()
                cp = copy(a, 4 + j, (*chip, c), sibling)
                cp.start()
                sent.append(cp)
        for a in range(n):
            copy(a, 0, sibling, me).wait_recv()
            for j, chip in enumerate(chips):
                copy(a, 4 + j, (*chip, 1 - c), me).wait_recv()
        for cp in sent:
            cp.wait_send()
        for cp in local + mod_copies:
            cp.wait()

    vm, hbm = pl.BlockSpec(memory_space=pltpu.VMEM), pl.BlockSpec(memory_space=pl.ANY)
    res = pl.pallas_call(
        body, name="prologue",
        out_shape=[jax.ShapeDtypeStruct((NDEV, 8, ncol), F32), jax.ShapeDtypeStruct((16, D), BF16)]
        + [jax.ShapeDtypeStruct((NDEV,) + a.shape, a.dtype) for a in arrs],
        in_specs=[vm, vm, vm] + [hbm] * n, out_specs=[vm, vm] + [hbm] * n,
        scratch_shapes=[pltpu.VMEM((NDEV, 8, D), F32), pltpu.SemaphoreType.DMA((n, NDEV - 1)), pltpu.SemaphoreType.DMA((n, NDEV - 1)),
                        pltpu.SemaphoreType.DMA((n,)), pltpu.SemaphoreType.DMA((2, NDEV - 1)), pltpu.SemaphoreType.DMA((2, NDEV - 1))],
    )(c8, w_ada_s, b_ada_s, *arrs)
    return res[0], res[1], res[2:]


def _in_set(idx, dests):
    p = idx == dests[0]
    for d in dests[1:]:
        p = jnp.logical_or(p, idx == d)
    return p


_HBM = pl.BlockSpec(memory_space=pltpu.HBM)
_SEM = pl.BlockSpec(memory_space=pltpu.SEMAPHORE)


def _pair_reduce(send, dests, name):
    nd = send.shape[0]

    def body(s_ref, o_ref, land, ssem, rsem):
        x, y, c = _my_pos()
        cps = []
        for i in range(nd):
            cp = pltpu.make_async_remote_copy(src_ref=s_ref.at[i], dst_ref=land.at[i], send_sem=ssem.at[i], recv_sem=rsem.at[i],
                                              device_id=(x, y, 1 - c), device_id_type=MESH)
            pl.when(c != (dests[i] & 1))(cp.start)
            cps.append(cp)
        for i in range(nd):
            mine = c == (dests[i] & 1)

            @pl.when(mine)
            def _():
                cps[i].wait_recv()
                o_ref[i] = (s_ref[i].astype(F32) + land[i].astype(F32)).astype(BF16)

            pl.when(jnp.logical_not(mine))(cps[i].wait_send)

    vm = pl.BlockSpec(memory_space=pltpu.VMEM)
    return pl.pallas_call(
        body, name=name, out_shape=jax.ShapeDtypeStruct(send.shape, send.dtype), in_specs=[vm], out_specs=vm,
        scratch_shapes=[pltpu.VMEM(send.shape, send.dtype), pltpu.SemaphoreType.DMA((nd,)), pltpu.SemaphoreType.DMA((nd,))],
        compiler_params=_cp(None, VMEM_BIG),
    )(send)


def _xchg_copies(xs_dests, sends, lands, ssem, rsem, lsem):
    x, y, c = _my_pos()
    me = 4 * x + 2 * y + c
    remote, local = [], []
    for a, dests in enumerate(xs_dests):
        same_core = dests[-1] == "same core"
        dests = dests[:-1] if same_core else dests
        lo, nd = dests[0], sends[a].shape[0]
        for k in range(1, NDEV):
            if same_core and k & 1:
                continue
            px, py, pc = _peer(k, x, y, c)
            pidx = 4 * px + 2 * py + pc
            cp = pltpu.make_async_remote_copy(src_ref=sends[a].at[jnp.clip(pidx - lo, 0, nd - 1)], dst_ref=lands[a].at[me],
                                              send_sem=ssem.at[a * (NDEV - 1) + k - 1], recv_sem=rsem.at[a * (NDEV - 1) + k - 1],
                                              device_id=(px, py, pc), device_id_type=MESH)
            remote.append((cp, _in_set(pidx, dests), _in_set(me, dests)))
        lc = pltpu.make_async_copy(sends[a].at[jnp.clip(me - lo, 0, nd - 1)], lands[a].at[me], lsem.at[a])
        local.append((lc, _in_set(me, dests)))
    return remote, local


def _xchg_start(xs, name, lands=None):
    n = len(xs)
    dests = [d for _, d in xs]
    sends = [pltpu.with_memory_space_constraint(s, pltpu.HBM) for s, _ in xs]
    lands = [None] * n if lands is None else lands
    lands = [pltpu.with_memory_space_constraint(lax.empty((NDEV,) + s.shape[1:], s.dtype) if l is None else l, pltpu.HBM)
             for (s, _), l in zip(xs, lands)]

    def body(*refs):
        send_refs, land_refs = refs[:n], refs[n:2 * n]
        ssem, rsem, lsem = refs[2 * n:2 * n + 3]
        token = refs[-1]
        remote, local = _xchg_copies(dests, send_refs, land_refs, ssem, rsem, lsem)
        for cp, to_dest, _ in remote:
            pl.when(to_dest)(cp.start)
        for lc, i_am_dest in local:
            pl.when(i_am_dest)(lc.start)
        token[...] = jnp.zeros_like(token)

    res = pl.pallas_call(
        body, name=name,
        out_shape=[pltpu.SemaphoreType.DMA((n * (NDEV - 1),)), pltpu.SemaphoreType.DMA((n * (NDEV - 1),)), pltpu.SemaphoreType.DMA((n,))]
        + [pltpu.HBM(a.shape, a.dtype) for a in list(sends) + list(lands)] + [jax.ShapeDtypeStruct((8, 128), F32)],
        in_specs=[_HBM] * (2 * n), out_specs=[_SEM, _SEM, _SEM] + [_HBM] * (2 * n) + [pl.BlockSpec(memory_space=pltpu.VMEM)],
        input_output_aliases={i: 3 + i for i in range(2 * n)},
        compiler_params=pltpu.CompilerParams(has_side_effects=pltpu.SideEffectType.DATAFLOW_SIDE_EFFECTING),
    )(*sends, *lands)
    return dict(sems=res[0:3], sends=res[3:3 + n], lands=res[3 + n:3 + 2 * n], dests=dests), res[-1]


def _xchg_wait(states, lands, land_of, after, name):
    flat = []
    for st in states:
        flat += list(st["sends"]) + list(st["sems"])
    nl = len(lands)

    def body(*refs):
        land_refs = refs[:nl]
        pos = nl
        for s, st in enumerate(states):
            n = len(st["dests"])
            send_refs = refs[pos:pos + n]
            ssem, rsem, lsem = refs[pos + n:pos + n + 3]
            pos += n + 3
            remote, local = _xchg_copies(st["dests"], send_refs, [land_refs[i] for i in land_of[s]], ssem, rsem, lsem)
            for cp, to_dest, i_am_dest in remote:
                pl.when(to_dest)(cp.wait_send)
                pl.when(i_am_dest)(cp.wait_recv)
            for lc, i_am_dest in local:
                pl.when(i_am_dest)(lc.wait)

    in_specs = [_HBM] * nl
    for st in states:
        in_specs += [_HBM] * len(st["dests"]) + [_SEM, _SEM, _SEM]
    return pl.pallas_call(
        body, name=name, out_shape=[pltpu.HBM(a.shape, a.dtype) for a in lands],
        in_specs=in_specs + [pl.BlockSpec(memory_space=pl.ANY)], out_specs=[_HBM] * nl,
        input_output_aliases={i: i for i in range(nl)},
        compiler_params=pltpu.CompilerParams(has_side_effects=pltpu.SideEffectType.DATAFLOW_SIDE_EFFECTING),
    )(*lands, *flat, after)


def _mm_tn(a, b, name, out_dtype=F32):
    S, M = a.shape
    N = b.shape[1]
    tk = min(2048, S)
    tn = N if N <= 768 else (640 if N % 640 == 0 else 512)
    nk = S // tk

    def body(a_ref, b_ref, o_ref, acc):
        k = pl.program_id(1)

        @pl.when(k == 0)
        def _():
            acc[...] = _dot(a_ref[...], b_ref[...], TN)

        @pl.when(k > 0)
        def _():
            acc[...] += _dot(a_ref[...], b_ref[...], TN)

        @pl.when(k == nk - 1)
        def _():
            o_ref[...] = acc[...].astype(out_dtype)

    return pl.pallas_call(
        body, name=name, out_shape=jax.ShapeDtypeStruct((M, N), out_dtype), grid=(N // tn, nk),
        in_specs=[pl.BlockSpec((tk, M), lambda j, k: (k, 0)), pl.BlockSpec((tk, tn), lambda j, k: (k, j))],
        out_specs=pl.BlockSpec((M, tn), lambda j, k: (0, j)), scratch_shapes=[pltpu.VMEM((M, tn), F32)],
        compiler_params=_cp(("parallel", "arbitrary"), VMEM_BIG),
    )(a, b)


def _fwd_in(x, mod, g_pre, w_p):
    S = x.shape[0]
    tm = min(512, S)

    def body(x_ref, mod_ref, g_ref, w_ref, p_ref, h_ref):
        xv = x_ref[...]
        r = lax.rsqrt(jnp.mean(xv * xv, axis=-1, keepdims=True) + EPS)
        h = (((xv * r) * g_ref[...]) * (1.0 + mod_ref[1:2, :]) + mod_ref[0:1, :]).astype(BF16)
        h_ref[...] = h
        p_ref[...] = _dot(h, w_ref[...]).astype(BF16)

    return pl.pallas_call(
        body, name="fwd_in", out_shape=[jax.ShapeDtypeStruct((S, P_W), BF16), jax.ShapeDtypeStruct((S, D), BF16)],
        grid=(S // tm,),
        in_specs=[pl.BlockSpec((tm, D), lambda i: (i, 0)), _full((3, D)), _full((1, D)), _full((D, P_W))],
        out_specs=[pl.BlockSpec((tm, P_W), lambda i: (i, 0)), pl.BlockSpec((tm, D), lambda i: (i, 0))],
        compiler_params=_cp(("parallel",), VMEM_BIG),
    )(x, mod, g_pre, w_p)


def _swap16(v):
    lane = lax.broadcasted_iota(jnp.int32, v.shape, 1)
    return jnp.where((lane % 32) < 16, pltpu.roll(v, 112, 1), pltpu.roll(v, 16, 1))


def _rope(v, cos, sin):
    return v * cos + _swap16(v) * sin


def _rope_t(v, cos, sin):
    return v * cos - _swap16(v) * sin


def _head_mean(v):
    lo = lax.broadcasted_iota(jnp.int32, v.shape, 1) < 64
    m0 = jnp.sum(jnp.where(lo, v, 0.0), axis=-1, keepdims=True)
    m1 = jnp.sum(jnp.where(lo, 0.0, v), axis=-1, keepdims=True)
    return jnp.where(lo, m0, m1) * (1.0 / 64.0)


def _prep(p, cos, sin, qg, kg):
    S = p.shape[0]
    tm = min(512, S)

    def body(qa_ref, kv_ref, qr_ref, kr_ref, cos_ref, sin_ref, qg_ref, kg_ref, qt_ref, kh_ref, kt_ref, vh_ref, vta_ref, qr2_ref, kr2_ref):
        cos_v, sin_v = cos_ref[...], sin_ref[...]
        for g in range(4):
            xv = qa_ref[:, 128 * g:128 * g + 128].astype(F32)
            r = lax.rsqrt(_head_mean(xv * xv) + EPS)
            yt = (_rope((xv * r) * qg_ref[...], cos_v, sin_v) * (0.125 * LOG2E)).T
            qt_ref[2 * g] = yt[:DH].astype(BF16)
            qt_ref[2 * g + 1] = yt[DH:].astype(BF16)
        xv = kv_ref[:, :128].astype(F32)
        r = lax.rsqrt(_head_mean(xv * xv) + EPS)
        yv = _rope((xv * r) * kg_ref[...], cos_v, sin_v)
        kh_ref[0] = yv[:, :64].astype(BF16)
        kh_ref[1] = yv[:, 64:].astype(BF16)
        yt = yv.T
        kt_ref[0] = yt[:DH].astype(BF16)
        kt_ref[1] = yt[DH:].astype(BF16)
        vv = kv_ref[:, 128:].astype(F32)
        vh_ref[0] = vv[:, :64].astype(BF16)
        vh_ref[1] = vv[:, 64:].astype(BF16)
        vt = vv.T
        tail = (lax.broadcasted_iota(jnp.int32, (DHA - DH, tm), 0) == 0).astype(BF16)
        for kvh in range(2):
            vta_ref[kvh, 0:DH, :] = vt[DH * kvh:DH * kvh + DH].astype(BF16)
            vta_ref[kvh, DH:DHA, :] = tail
        for g in range(2):
            sl = slice(128 * g, 128 * g + 128)
            qr2_ref[:, sl] = _rope(qr_ref[:, sl].astype(F32), cos_v, sin_v)
            kr2_ref[:, sl] = _rope(kr_ref[:, sl].astype(F32), cos_v, sin_v) * 0.125

    hm = lambda n: pl.BlockSpec((n, tm, DH), lambda i: (0, i, 0))
    ht = lambda n, r: pl.BlockSpec((n, r, tm), lambda i: (0, 0, i))
    return pl.pallas_call(
        body, name="prep",
        out_shape=[jax.ShapeDtypeStruct((8, DH, S), BF16), jax.ShapeDtypeStruct((2, S, DH), BF16), jax.ShapeDtypeStruct((2, DH, S), BF16),
                   jax.ShapeDtypeStruct((2, S, DH), BF16), jax.ShapeDtypeStruct((2, DHA, S), BF16),
                   jax.ShapeDtypeStruct((S, 256), F32), jax.ShapeDtypeStruct((S, 256), F32)],
        grid=(S // tm,),
        in_specs=[pl.BlockSpec((tm, 512), lambda i: (i, O_QA // 512)), pl.BlockSpec((tm, 256), lambda i: (i, O_KA // 256)),
                  pl.BlockSpec((tm, 256), lambda i: (i, O_QR // 256)), pl.BlockSpec((tm, 256), lambda i: (i, O_KR // 256)),
                  pl.BlockSpec((tm, 128), lambda i: (i, 0)), pl.BlockSpec((tm, 128), lambda i: (i, 0)), _full((1, 128)), _full((1, 128))],
        out_specs=[ht(8, DH), hm(2), ht(2, DH), hm(2), ht(2, DHA), pl.BlockSpec((tm, 256), lambda i: (i, 0)), pl.BlockSpec((tm, 256), lambda i: (i, 0))],
        compiler_params=_cp(("parallel",)),
    )(p, p, p, p, cos, sin, qg, kg)


def _attn_fwd(qt, kh, vta):
    S = qt.shape[2]
    tq, tk = min(1024, S), min(512, S)
    nj = S // tk

    def body(q_ref, k_ref, v_ref, o_ref, ot_ref, lse_ref, m_s, acc_s):
        j = pl.program_id(1)

        @pl.when(j == 0)
        def _():
            m_s[...] = jnp.full_like(m_s, -jnp.inf)
            acc_s[...] = jnp.zeros_like(acc_s)

        m_all = m_s[...]
        st = {0: _dot(k_ref[0], q_ref[0])}
        m_new, acc_new = [], []
        for h in range(8):
            if h + 1 < 8:
                st[h + 1] = _dot(k_ref[(h + 1) // 4], q_ref[h + 1])
            m_old = m_all[h:h + 1, :]
            mn = jnp.maximum(m_old, jnp.max(st[h], axis=0, keepdims=True))
            pt = jnp.exp2(st[h] - mn).astype(BF16)
            acc_new.append(jnp.exp2(m_old - mn) * acc_s[h] + _dot(v_ref[h // 4], pt))
            m_new.append(mn)
            del st[h]
        for h in range(8):
            acc_s[h] = acc_new[h]
            m_s[h:h + 1, :] = m_new[h]

        @pl.when(j == nj - 1)
        def _():
            for h in range(8):
                ot = acc_s[h, 0:DH, :] / acc_s[h, DH:DH + 1, :]
                ot_ref[h] = ot
                o_ref[:, DH * h:DH * h + DH] = ot.T
                lse_ref[h // 4, h % 4:h % 4 + 1, :] = m_s[h:h + 1, :] + jnp.log2(acc_s[h, DH:DH + 1, :])

    return pl.pallas_call(
        body, name="attn_fwd",
        out_shape=[jax.ShapeDtypeStruct((S, 512), F32), jax.ShapeDtypeStruct((8, DH, S), F32), jax.ShapeDtypeStruct((2, 4, S), F32)],
        grid=(S // tq, nj),
        in_specs=[pl.BlockSpec((8, DH, tq), lambda i, j: (0, 0, i)), pl.BlockSpec((2, tk, DH), lambda i, j: (0, j, 0)),
                  pl.BlockSpec((2, DHA, tk), lambda i, j: (0, 0, j))],
        out_specs=[pl.BlockSpec((tq, 512), lambda i, j: (i, 0)), pl.BlockSpec((8, DH, tq), lambda i, j: (0, 0, i)),
                   pl.BlockSpec((2, 4, tq), lambda i, j: (0, 0, i))],
        scratch_shapes=[pltpu.VMEM((8, tq), F32), pltpu.VMEM((8, DHA, tq), F32)],
        compiler_params=_cp(("parallel", "arbitrary"), VMEM_BIG),
    )(qt, kh, vta)


def _ret_tables(wf, wb):
    C = CH

    def body(wf_ref, wb_ref, dc_ref, qdf_ref, qdb_ref, kdf_ref, kdb_ref, a_ref):
        def logsig(w):
            z = jnp.exp(-jnp.abs(w))
            u = 1.0 + z
            l1p = jnp.where(u == 1.0, z, jnp.log(u) * (z / jnp.where(u == 1.0, 1.0, u - 1.0)))
            return jnp.minimum(w, 0.0) - l1p

        lgf, lgb = logsig(wf_ref[...]), logsig(wb_ref[...])
        lane4 = lax.broadcasted_iota(jnp.int32, (1, 4), 1)

        def pick(lg, h):
            return jnp.sum(jnp.where(lane4 == h, lg, 0.0), axis=-1, keepdims=True)

        ii = lax.broadcasted_iota(jnp.int32, (C, C), 0).astype(F32)
        jj = lax.broadcasted_iota(jnp.int32, (C, C), 1).astype(F32)
        dif = ii - jj
        hd = lax.broadcasted_iota(jnp.int32, (C, 256), 1) // DH
        lf_l = jnp.zeros((C, 256), F32)
        lb_l = jnp.zeros((C, 256), F32)
        for h in range(HR):
            lf, lb = pick(lgf, h), pick(lgb, h)
            dc_ref[h] = jnp.where(dif >= 0, jnp.exp(lf * jnp.maximum(dif, 0.0)), jnp.exp(lb * jnp.maximum(-dif, 0.0)))
            lf_l = jnp.where(hd == h, lf, lf_l)
            lb_l = jnp.where(hd == h, lb, lb_l)
            a_ref[h:h + 1, :] = jnp.broadcast_to(jnp.exp(lf * C), (1, 128))
            a_ref[HR + h:HR + h + 1, :] = jnp.broadcast_to(jnp.exp(lb * C), (1, 128))
        ri = lax.broadcasted_iota(jnp.int32, (C, 256), 0).astype(F32)
        qdf_ref[...] = jnp.exp(lf_l * (ri + 1.0))
        qdb_ref[...] = jnp.exp(lb_l * (C - ri))
        kdf_ref[...] = jnp.exp(lf_l * (C - 1.0 - ri))
        kdb_ref[...] = jnp.exp(lb_l * ri)

    t = jax.ShapeDtypeStruct((C, 256), F32)
    return pl.pallas_call(body, name="ret_tables",
                          out_shape=[jax.ShapeDtypeStruct((HR, C, C), F32), t, t, t, t, jax.ShapeDtypeStruct((8, 128), F32)])(wf, wb)


def _ret_states(kr2, p, kdf, kdb, adec):
    S = kr2.shape[0]
    C, N = CH, S // CH
    G = _scan_group(N)
    NG = N // G

    def body(kf_ref, vf_ref, kb_ref, vb_ref, kdf_ref, kdb_ref, a_ref, rf_ref, rb_ref, sf, sb):
        @pl.when(pl.program_id(0) == 0)
        def _():
            sf[...] = jnp.zeros_like(sf)
            sb[...] = jnp.zeros_like(sb)

        kvf, kvb = [], []
        for u in range(G):
            rows = slice(C * u, C * u + C)
            kdfw = (kf_ref[rows, :] * kdf_ref[...]).astype(BF16)
            kdbw = (kb_ref[rows, :] * kdb_ref[...]).astype(BF16)
            vf, vb = vf_ref[rows, :].astype(BF16), vb_ref[rows, :].astype(BF16)
            kvf.append([_dot(kdfw[:, _ks(h)], vf[:, _vs(h)], TN) for h in range(HR)])
            kvb.append([_dot(kdbw[:, _ks(h)], vb[:, _vs(h)], TN) for h in range(HR)])
        for u in range(G):
            rf_ref[u] = sf[...]
            for h in range(HR):
                sf[h] = a_ref[h:h + 1, :] * sf[h] + kvf[u][h]
        for u in reversed(range(G)):
            rb_ref[u] = sb[...]
            for h in range(HR):
                sb[h] = a_ref[HR + h:HR + h + 1, :] * sb[h] + kvb[u][h]

    st = jax.ShapeDtypeStruct((N, HR, DH, DV), F32)
    return pl.pallas_call(
        body, name="ret_states", out_shape=[st, st], grid=(NG,),
        in_specs=[pl.BlockSpec((G * C, 256), lambda t: (t, 0)), pl.BlockSpec((G * C, 512), lambda t: (t, O_VR // 512)),
                  pl.BlockSpec((G * C, 256), lambda t: (NG - 1 - t, 0)), pl.BlockSpec((G * C, 512), lambda t: (NG - 1 - t, O_VR // 512)),
                  _full((C, 256)), _full((C, 256)), _full((8, 128))],
        out_specs=[pl.BlockSpec((G, HR, DH, DV), lambda t: (t, 0, 0, 0)), pl.BlockSpec((G, HR, DH, DV), lambda t: (NG - 1 - t, 0, 0, 0))],
        scratch_shapes=[pltpu.VMEM((HR, DH, DV), F32), pltpu.VMEM((HR, DH, DV), F32)],
        compiler_params=_cp(("arbitrary",)),
    )(kr2, p, kr2, p, kdf, kdb, adec)


def _scan_group(n):
    return 4 if n % 4 == 0 else (2 if n % 2 == 0 else 1)


def _ks(h):
    return slice(DH * h, DH * h + DH)


def _vs(h):
    return slice(DV * h, DV * h + DV)


def _ret_heads_fwd(qb, kb, vb, qfw, qbw, dc_ref, rf_ref, rb_ref, u=0):
    hs = range(HR)
    s = [_dot(qb[:, _ks(h)], kb[:, _ks(h)], NT) for h in hs]
    inter = [_dot(qfw[:, _ks(h)], rf_ref[u, h].astype(BF16)) + _dot(qbw[:, _ks(h)], rb_ref[u, h].astype(BF16)) for h in hs]
    sd = [s[h] * dc_ref[h] for h in hs]
    o = [_dot(sd[h].astype(BF16), vb[:, _vs(h)]) + inter[h] for h in hs]
    return sd, o


def _ret_out(qr2, kr2, p, rf, rb, dc, qdf, qdb, gn):
    S = qr2.shape[0]
    C, N = CH, S // CH
    G = _scan_group(N)

    def body(q_ref, k_ref, v_ref, z_ref, rf_ref, rb_ref, dc_ref, qdf_ref, qdb_ref, gn_ref, yr_ref):
        outs = []
        for u in range(G):
            rows = slice(C * u, C * u + C)
            qv = q_ref[rows, :]
            qb, kb, vb = qv.astype(BF16), k_ref[rows, :].astype(BF16), v_ref[rows, :].astype(BF16)
            qfw, qbw = (qv * qdf_ref[...]).astype(BF16), (qv * qdb_ref[...]).astype(BF16)
            outs.append(_ret_heads_fwd(qb, kb, vb, qfw, qbw, dc_ref, rf_ref, rb_ref, u)[1])
        for u in range(G):
            rows = slice(C * u, C * u + C)
            for h in range(HR):
                vs = _vs(h)
                o = outs[u][h]
                mu = jnp.mean(o, axis=-1, keepdims=True)
                var = jnp.mean(jnp.square(o - mu), axis=-1, keepdims=True)
                on = (o - mu) * lax.rsqrt(var + EPS)
                z = z_ref[rows, vs].astype(F32)
                yr_ref[rows, vs] = ((on * gn_ref[:, vs]) * (z * _sigmoid(z))).astype(BF16)

    row = lambda w, off=0: pl.BlockSpec((G * C, w), lambda t: (t, off))
    stb = lambda: pl.BlockSpec((G, HR, DH, DV), lambda t: (t, 0, 0, 0))
    return pl.pallas_call(
        body, name="ret_out", out_shape=jax.ShapeDtypeStruct((S, 512), BF16), grid=(N // G,),
        in_specs=[row(256), row(256), row(512, O_VR // 512), row(512, O_ZR // 512), stb(), stb(),
                  _full((HR, C, C)), _full((C, 256)), _full((C, 256)), _full((1, 512))],
        out_specs=row(512),
        compiler_params=_cp(("parallel",)),
    )(qr2, kr2, p, p, rf, rb, dc, qdf, qdb, gn)


def _mid(x, tgt, mod, g_post, o_att, p, yr, w_pa, w_pr, w_out):
    S = x.shape[0]
    tm = min(256, S)

    def body(x_ref, t_ref, mod_ref, gp_ref, o_ref, za_ref, gl_ref, yr_ref, wpa_ref, wpr_ref, wout_ref,
             dout_ref, do_ref, dpm_ref, dyr_ref, mb_ref, dub_ref, yab_ref, dab_ref, drb_ref, sums_ref):
        @pl.when(pl.program_id(0) == 0)
        def _():
            sums_ref[...] = jnp.zeros_like(sums_ref)

        za = za_ref[...].astype(F32)
        sa = _sigmoid(za)
        sil = za * sa
        ov = o_ref[...]
        ya_b = (ov * sil).astype(BF16)
        yr_b = yr_ref[...]
        av = _dot(ya_b, wpa_ref[...])
        rv = _dot(yr_b, wpr_ref[...])
        ga = _sigmoid(gl_ref[:, :D].astype(F32))
        gr = _sigmoid(gl_ref[:, D:].astype(F32))
        mb = (ga * av + gr * rv).astype(BF16)
        u = _dot(mb, wout_ref[...])
        r2 = lax.rsqrt(jnp.mean(u * u, axis=-1, keepdims=True) + EPS)
        un = u * r2
        gp = gp_ref[...]
        yv = un * gp
        gate = mod_ref[2:3, :]
        err = (x_ref[...] + gate * yv) - t_ref[...]
        dout = err * (1.0 / D)
        dout_ref[...] = dout
        dy = dout * gate
        sums_ref[0:1, :] += jnp.sum(dout * yv, axis=0, keepdims=True)
        sums_ref[1:2, :] += jnp.sum(dy * un, axis=0, keepdims=True)
        sums_ref[2:3, :] += jnp.sum(err * err, axis=0, keepdims=True)
        dyg = dy * gp
        du_b = (r2 * (dyg - un * jnp.mean(dyg * un, axis=-1, keepdims=True))).astype(BF16)
        dm = _dot(du_b, wout_ref[...], NT)
        da_b = (dm * ga).astype(BF16)
        dr_b = (dm * gr).astype(BF16)
        dpm_ref[:, :D] = (dm * av * (ga * (1.0 - ga))).astype(BF16)
        dpm_ref[:, D:2 * D] = (dm * rv * (gr * (1.0 - gr))).astype(BF16)
        dya = _dot(da_b, wpa_ref[...], NT)
        dyr_ref[...] = _dot(dr_b, wpr_ref[...], NT)
        dov = dya * sil
        for g in range(4):
            dt = dov[:, 128 * g:128 * g + 128].T
            do_ref[2 * g] = dt[:DH].astype(BF16)
            do_ref[2 * g + 1] = dt[DH:].astype(BF16)
        dpm_ref[:, 2 * D:] = (dya * ov * (sa * (1.0 + za * (1.0 - sa)))).astype(BF16)
        mb_ref[...] = mb
        dub_ref[...] = du_b
        yab_ref[...] = ya_b
        dab_ref[...] = da_b
        drb_ref[...] = dr_b

    row = lambda w: pl.BlockSpec((tm, w), lambda i: (i, 0))
    sd = lambda w, dt: jax.ShapeDtypeStruct((S, w), dt)
    return pl.pallas_call(
        body, name="mid",
        out_shape=[sd(D, F32), jax.ShapeDtypeStruct((8, DH, S), BF16), sd(2560, BF16), sd(512, F32), sd(D, BF16), sd(D, BF16), sd(512, BF16),
                   sd(D, BF16), sd(D, BF16), jax.ShapeDtypeStruct((8, D), F32)],
        grid=(S // tm,),
        in_specs=[row(D), row(D), _full((3, D)), _full((1, D)), row(512), pl.BlockSpec((tm, 512), lambda i: (i, O_ZA // 512)),
                  pl.BlockSpec((tm, 2048), lambda i: (i, 0)), row(512), _full((512, D)), _full((512, D)), _full((D, D))],
        out_specs=[row(D), pl.BlockSpec((8, DH, tm), lambda i: (0, 0, i)), row(2560), row(512), row(D), row(D), row(512), row(D), row(D),
                   _full((8, D))],
        compiler_params=_cp(("arbitrary",), VMEM_BIG),
    )(x, tgt, mod, g_post, o_att, p, p, yr, w_pa, w_pr, w_out)


def _attn_bwd(qt, kh, kt, vh, dot_, ot, lse):
    S = qt.shape[2]
    tq, tk = min(1024, S), min(1024, S)

    def body(q_ref, k_ref, kt_ref, v_ref, do_ref, o_ref, lse_ref, dq_ref, dk_ref, dv_ref):
        j, i = pl.program_id(0), pl.program_id(1)
        cols = pl.ds(pl.multiple_of(i * tq, tq), tq)
        st = {0: _dot(k_ref[0], q_ref[0])}
        dpt = {0: _dot(v_ref[0], do_ref[0])}
        dk_acc, dv_acc, dqs = [None, None], [None, None], []
        for h in range(8):
            g = h // 4
            if h + 1 < 8:
                st[h + 1] = _dot(k_ref[(h + 1) // 4], q_ref[h + 1])
                dpt[h + 1] = _dot(v_ref[(h + 1) // 4], do_ref[h + 1])
            qt_h, dot_h = q_ref[h], do_ref[h]
            delta = jnp.sum(dot_h.astype(F32) * o_ref[h], axis=0, keepdims=True)
            pt = jnp.exp2(st[h] - lse_ref[g, h % 4:h % 4 + 1, :])
            dst = (pt * (dpt[h] - delta)).astype(BF16)
            dv_h = _dot(dot_h, pt.astype(BF16), NT)
            dk_h = _dot(qt_h, dst, NT)
            dqs.append(_dot(kt_ref[g], dst))
            dv_acc[g] = dv_h if dv_acc[g] is None else dv_acc[g] + dv_h
            dk_acc[g] = dk_h if dk_acc[g] is None else dk_acc[g] + dk_h
            del st[h], dpt[h]

        @pl.when(i == 0)
        def _():
            for g in range(2):
                dk_ref[g] = dk_acc[g]
                dv_ref[g] = dv_acc[g]

        @pl.when(i > 0)
        def _():
            for g in range(2):
                dk_ref[g] += dk_acc[g]
                dv_ref[g] += dv_acc[g]

        @pl.when(j == 0)
        def _():
            for h in range(8):
                dq_ref[h, :, cols] = dqs[h]

        @pl.when(j > 0)
        def _():
            for h in range(8):
                dq_ref[h, :, cols] += dqs[h]

    return pl.pallas_call(
        body, name="attn_bwd",
        out_shape=[jax.ShapeDtypeStruct((8, DH, S), F32), jax.ShapeDtypeStruct((2, DH, S), F32), jax.ShapeDtypeStruct((2, DH, S), F32)],
        grid=(S // tk, S // tq),
        in_specs=[pl.BlockSpec((8, DH, tq), lambda j, i: (0, 0, i)), pl.BlockSpec((2, tk, DH), lambda j, i: (0, j, 0)),
                  pl.BlockSpec((2, DH, tk), lambda j, i: (0, 0, j)), pl.BlockSpec((2, tk, DH), lambda j, i: (0, j, 0)),
                  pl.BlockSpec((8, DH, tq), lambda j, i: (0, 0, i)), pl.BlockSpec((8, DH, tq), lambda j, i: (0, 0, i)),
                  pl.BlockSpec((2, 4, tq), lambda j, i: (0, 0, i))],
        out_specs=[pl.BlockSpec((8, DH, S), lambda j, i: (0, 0, 0)), pl.BlockSpec((2, DH, tk), lambda j, i: (0, 0, j)),
                   pl.BlockSpec((2, DH, tk), lambda j, i: (0, 0, j))],
        compiler_params=_cp(("arbitrary", "arbitrary"), VMEM_BIG),
    )(qt, kh, kt, vh, dot_, ot, lse)


def _attn_prep_bwd(dqt, dkt, dvt, p, cos, sin, qg, kg):
    S = dqt.shape[2]
    tm = min(512, S)

    def body(dq_ref, dk_ref, dv_ref, qa_ref, ka_ref, cos_ref, sin_ref, qg_ref, kg_ref, dp_ref, gs_ref):
        @pl.when(pl.program_id(0) == 0)
        def _():
            gs_ref[...] = jnp.zeros_like(gs_ref)

        cos_v, sin_v = cos_ref[...], sin_ref[...]

        def pair(ref, a):
            return jnp.concatenate([ref[a], ref[a + 1]], axis=0).T

        def norm_bwd(dyv, xv, gv, row):
            r = lax.rsqrt(_head_mean(xv * xv) + EPS)
            xn = xv * r
            dxh = _rope_t(dyv, cos_v, sin_v)
            gs_ref[row:row + 1, :] += jnp.sum(dxh * xn, axis=0, keepdims=True)
            dg = dxh * gv
            return r * (dg - xn * _head_mean(dg * xn))

        for g in range(4):
            sl = slice(128 * g, 128 * g + 128)
            dp_ref[:, sl] = norm_bwd(pair(dq_ref, 2 * g) * 0.125, qa_ref[:, sl].astype(F32), qg_ref[...], 0).astype(BF16)
        dp_ref[:, 512:640] = norm_bwd(pair(dk_ref, 0) * LN2, ka_ref[...].astype(F32), kg_ref[...], 1).astype(BF16)
        dp_ref[:, 640:768] = pair(dv_ref, 0).astype(BF16)

    ht = lambda n: pl.BlockSpec((n, DH, tm), lambda i: (0, 0, i))
    return pl.pallas_call(
        body, name="attn_prep_bwd", out_shape=[jax.ShapeDtypeStruct((S, 768), BF16), jax.ShapeDtypeStruct((8, 128), F32)],
        grid=(S // tm,),
        in_specs=[ht(8), ht(2), ht(2),
                  pl.BlockSpec((tm, 512), lambda i: (i, O_QA // 512)), pl.BlockSpec((tm, 128), lambda i: (i, O_KA // 128)),
                  pl.BlockSpec((tm, 128), lambda i: (i, 0)), pl.BlockSpec((tm, 128), lambda i: (i, 0)), _full((1, 128)), _full((1, 128))],
        out_specs=[pl.BlockSpec((tm, 768), lambda i: (i, 0)), _full((8, 128))],
        compiler_params=_cp(("arbitrary",)),
    )(dqt, dkt, dvt, p, p, cos, sin, qg, kg)


def _ret_bwd_chunk(qr2, kr2, p, rf, rb, dc, qdf, qdb, gn, dyr, cos, sin):
    S = qr2.shape[0]
    C, N = CH, S // CH
    G = 1

    def body(q_ref, k_ref, v_ref, z_ref, rf_ref, rb_ref, dc_ref, qdf_ref, qdb_ref, gn_ref, dyr_ref, cos_ref, sin_ref,
             dpa_ref, dk_ref, dv_ref, drf_ref, drb_ref, dgn_ref, dlg_ref, dqs):
        @pl.when(pl.program_id(0) == 0)
        def _():
            dgn_ref[...] = jnp.zeros_like(dgn_ref)
            dlg_ref[...] = jnp.zeros_like(dlg_ref)

        ii = lax.broadcasted_iota(jnp.int32, (C, C), 0).astype(F32)
        jj = lax.broadcasted_iota(jnp.int32, (C, C), 1).astype(F32)
        dif = ii - jj
        ri = lax.broadcasted_iota(jnp.int32, (C, 1), 0).astype(F32)
        hs = range(HR)
        for u in range(G):
            rows = slice(C * u, C * u + C)
            qv = q_ref[rows, :]
            qb, kb, vb = qv.astype(BF16), k_ref[rows, :].astype(BF16), v_ref[rows, :].astype(BF16)
            qf32, qb32 = qv * qdf_ref[...], qv * qdb_ref[...]
            qfw, qbw = qf32.astype(BF16), qb32.astype(BF16)
            sd, o = _ret_heads_fwd(qb, kb, vb, qfw, qbw, dc_ref, rf_ref, rb_ref, u)
            do_b = []
            for h in hs:
                vs = _vs(h)
                mu = jnp.mean(o[h], axis=-1, keepdims=True)
                rstd = lax.rsqrt(jnp.mean(jnp.square(o[h] - mu), axis=-1, keepdims=True) + EPS)
                on = (o[h] - mu) * rstd
                z = z_ref[rows, vs].astype(F32)
                sz = _sigmoid(z)
                dy = dyr_ref[rows, vs]
                gnv = gn_ref[:, vs]
                dpa_ref[rows, 256 + DV * h:256 + DV * h + DV] = (dy * (on * gnv) * (sz * (1.0 + z * (1.0 - sz)))).astype(BF16)
                dys = dy * (z * sz)
                dgn_ref[:, vs] += jnp.sum(dys * on, axis=0, keepdims=True)
                don = dys * gnv
                do = rstd * (don - jnp.mean(don, axis=-1, keepdims=True) - on * jnp.mean(don * on, axis=-1, keepdims=True))
                do_b.append(do.astype(BF16))
            dpm = [_dot(do_b[h], vb[:, _vs(h)], NT) for h in hs]
            dqf = [_dot(do_b[h], rf_ref[u, h].astype(BF16), NT) for h in hs]
            dqb = [_dot(do_b[h], rb_ref[u, h].astype(BF16), NT) for h in hs]
            for h in hs:
                dv_ref[rows, _vs(h)] = _dot(sd[h].astype(BF16), do_b[h], TN)
                drf_ref[u, h] = _dot(qfw[:, _ks(h)], do_b[h], TN)
                drb_ref[u, h] = _dot(qbw[:, _ks(h)], do_b[h], TN)
            dsd = [(dpm[h] * dc_ref[h]).astype(BF16) for h in hs]
            for h in hs:
                ks = _ks(h)
                dqs[rows, ks] = _dot(dsd[h], kb[:, ks]) + dqf[h] * qdf_ref[:, ks] + dqb[h] * qdb_ref[:, ks]
                dk_ref[rows, ks] = _dot(dsd[h], qb[:, ks], TN)
            for h in hs:
                ks = _ks(h)
                e = dpm[h] * sd[h]
                lf = _sum11(e * jnp.maximum(dif, 0.0)) + _sum11(jnp.sum(qf32[:, ks] * dqf[h], axis=-1, keepdims=True) * (ri + 1.0))
                lb = _sum11(e * jnp.maximum(-dif, 0.0)) + _sum11(jnp.sum(qb32[:, ks] * dqb[h], axis=-1, keepdims=True) * (C - ri))
                dlg_ref[h:h + 1, :] += jnp.broadcast_to(lf, (1, 128))
                dlg_ref[HR + h:HR + h + 1, :] += jnp.broadcast_to(lb, (1, 128))
            for g in range(2):
                sl = slice(128 * g, 128 * g + 128)
                dpa_ref[rows, sl] = _rope_t(dqs[rows, sl], cos_ref[rows, :], sin_ref[rows, :]).astype(BF16)

    st = jax.ShapeDtypeStruct((N, HR, DH, DV), F32)
    stb = lambda: pl.BlockSpec((G, HR, DH, DV), lambda t: (t, 0, 0, 0))
    row = lambda w, off=0: pl.BlockSpec((G * C, w), lambda t: (t, off))
    return pl.pallas_call(
        body, name="ret_bwd_chunk",
        out_shape=[jax.ShapeDtypeStruct((S, 768), BF16), jax.ShapeDtypeStruct((S, 256), F32), jax.ShapeDtypeStruct((S, 512), F32), st, st,
                   jax.ShapeDtypeStruct((1, 512), F32), jax.ShapeDtypeStruct((8, 128), F32)],
        grid=(N // G,),
        in_specs=[row(256), row(256), row(512, O_VR // 512), row(512, O_ZR // 512),
                  stb(), stb(), _full((HR, C, C)), _full((C, 256)), _full((C, 256)), _full((1, 512)), row(512), row(128), row(128)],
        out_specs=[row(768), row(256), row(512), stb(), stb(), _full((1, 512)), _full((8, 128))],
        scratch_shapes=[pltpu.VMEM((G * C, 256), F32)],
        compiler_params=_cp(("arbitrary",)),
    )(qr2, kr2, p, p, rf, rb, dc, qdf, qdb, gn, dyr, cos, sin)


def _ret_bwd_scan(kr2, p, rf, rb, drf, drb, kdf, kdb, adec):
    S = kr2.shape[0]
    C, N = CH, S // CH
    G = _scan_group(N)
    NG = N // G

    def body(kf_ref, vf_ref, kb_ref, vb_ref, rf_ref, rb_ref, drf_ref, drb_ref, kdf_ref, kdb_ref, a_ref,
             dkf_ref, dkb_ref, dvf_ref, dvb_ref, dlg_ref, gf, gb):
        @pl.when(pl.program_id(0) == 0)
        def _():
            gf[...] = jnp.zeros_like(gf)
            gb[...] = jnp.zeros_like(gb)
            dlg_ref[...] = jnp.zeros_like(dlg_ref)

        ri = lax.broadcasted_iota(jnp.int32, (C, 1), 0).astype(F32)

        def one(k_ref, v_ref, r_ref, dr_ref, kd_ref, g_s, dk_ref, dv_ref, row0, wexp, order):
            g = [g_s[h] for h in range(HR)]
            lgs = [jnp.zeros((1, 1), F32) for _ in range(HR)]
            for u in order:
                rows = slice(C * u, C * u + C)
                kd32 = k_ref[rows, :] * kd_ref[...]
                kdw = kd32.astype(BF16)
                vb = v_ref[rows, :].astype(BF16)
                for h in range(HR):
                    ks, vs = _ks(h), _vs(h)
                    g_b = g[h].astype(BF16)
                    dkd = _dot(vb[:, vs], g_b, NT)
                    dk_ref[rows, ks] = dkd * kd_ref[:, ks]
                    dv_ref[rows, vs] = _dot(kdw[:, ks], g_b)
                    av = a_ref[row0 + h:row0 + h + 1, :]
                    lgs[h] = lgs[h] + (_sum11(jnp.sum(kd32[:, ks] * dkd, axis=-1, keepdims=True) * wexp)
                                       + C * av[:, 0:1] * _sum11(r_ref[u, h] * g[h]))
                    g[h] = dr_ref[u, h] + av * g[h]
            for h in range(HR):
                g_s[h] = g[h]
                dlg_ref[row0 + h:row0 + h + 1, :] += jnp.broadcast_to(lgs[h], (1, 128))

        one(kf_ref, vf_ref, rf_ref, drf_ref, kdf_ref, gf, dkf_ref, dvf_ref, 0, C - 1.0 - ri, list(reversed(range(G))))
        one(kb_ref, vb_ref, rb_ref, drb_ref, kdb_ref, gb, dkb_ref, dvb_ref, HR, ri, list(range(G)))

    fwd = lambda w, off=0: pl.BlockSpec((G * C, w), lambda t: (NG - 1 - t, off))
    bwd = lambda w, off=0: pl.BlockSpec((G * C, w), lambda t: (t, off))
    stf = lambda: pl.BlockSpec((G, HR, DH, DV), lambda t: (NG - 1 - t, 0, 0, 0))
    stb = lambda: pl.BlockSpec((G, HR, DH, DV), lambda t: (t, 0, 0, 0))
    return pl.pallas_call(
        body, name="ret_bwd_scan",
        out_shape=[jax.ShapeDtypeStruct((S, 256), F32), jax.ShapeDtypeStruct((S, 256), F32), jax.ShapeDtypeStruct((S, 512), F32),
                   jax.ShapeDtypeStruct((S, 512), F32), jax.ShapeDtypeStruct((8, 128), F32)],
        grid=(NG,),
        in_specs=[fwd(256), fwd(512, O_VR // 512), bwd(256), bwd(512, O_VR // 512), stf(), stb(), stf(), stb(),
                  _full((C, 256)), _full((C, 256)), _full((8, 128))],
        out_specs=[fwd(256), bwd(256), fwd(512), bwd(512), _full((8, 128))],
        scratch_shapes=[pltpu.VMEM((HR, DH, DV), F32), pltpu.VMEM((HR, DH, DV), F32)],
        compiler_params=_cp(("arbitrary",)),
    )(kr2, p, kr2, p, rf, rb, drf, drb, kdf, kdb, adec)


def _ret_bwd_final(dk_i, dkf, dkb, dv_i, dvf, dvb, cos, sin):
    S = dk_i.shape[0]
    tm = min(512, S)

    def body(a_ref, b_ref, c_ref, d_ref, e_ref, f_ref, cos_ref, sin_ref, o_ref):
        o_ref[:, :512] = (d_ref[...] + e_ref[...] + f_ref[...]).astype(BF16)
        cos_v, sin_v = cos_ref[...], sin_ref[...]
        for g in range(2):
            sl = slice(128 * g, 128 * g + 128)
            dk = a_ref[:, sl] + b_ref[:, sl] + c_ref[:, sl]
            o_ref[:, 512 + 128 * g:512 + 128 * g + 128] = (_rope_t(dk, cos_v, sin_v) * 0.125).astype(BF16)

    row = lambda w: pl.BlockSpec((tm, w), lambda i: (i, 0))
    return pl.pallas_call(
        body, name="ret_bwd_final", out_shape=jax.ShapeDtypeStruct((S, 768), BF16), grid=(S // tm,),
        in_specs=[row(256), row(256), row(256), row(512), row(512), row(512), row(128), row(128)], out_specs=row(768),
        compiler_params=_cp(("parallel",)),
    )(dk_i, dkf, dkb, dv_i, dvf, dvb, cos, sin)


def _bwd_in(dpm, dpa, dpra, dprb, w_p, x, dout, mod, g_pre):
    S = x.shape[0]
    tm = min(256, S)

    def body(a_ref, b_ref, c_ref, d_ref, w_ref, x_ref, dout_ref, mod_ref, g_ref, gx_ref, sums_ref):
        @pl.when(pl.program_id(0) == 0)
        def _():
            sums_ref[...] = jnp.zeros_like(sums_ref)

        dh = (_dot(a_ref[...], w_ref[:, :O_QA], NT) + _dot(b_ref[...], w_ref[:, O_QA:O_QR], NT)
              + _dot(c_ref[...], w_ref[:, O_QR:O_VR], NT) + _dot(d_ref[...], w_ref[:, O_VR:], NT))
        xv = x_ref[...]
        r = lax.rsqrt(jnp.mean(xv * xv, axis=-1, keepdims=True) + EPS)
        xn = xv * r
        gv = g_ref[...]
        sc1 = 1.0 + mod_ref[1:2, :]
        sums_ref[0:1, :] += jnp.sum(dh, axis=0, keepdims=True)
        sums_ref[1:2, :] += jnp.sum(dh * (xn * gv), axis=0, keepdims=True)
        sums_ref[2:3, :] += jnp.sum(dh * xn, axis=0, keepdims=True) * sc1
        dxn = dh * (gv * sc1)
        gx_ref[...] = dout_ref[...] + r * (dxn - xn * jnp.mean(dxn * xn, axis=-1, keepdims=True))

    row = lambda w: pl.BlockSpec((tm, w), lambda i: (i, 0))
    return pl.pallas_call(
        body, name="bwd_in", out_shape=[jax.ShapeDtypeStruct((S, D), F32), jax.ShapeDtypeStruct((8, D), F32)], grid=(S // tm,),
        in_specs=[row(2560), row(768), row(768), row(768), _full((D, P_W)), row(D), row(D), _full((3, D)), _full((1, D))],
        out_specs=[row(D), _full((8, D))],
        compiler_params=_cp(("arbitrary",), VMEM_BIG),
    )(dpm, dpa, dpra, dprb, w_p, x, dout, mod, g_pre)


SMALL = ("b_ada", "g_pre", "qn_g", "kn_g", "w_dec_f", "w_dec_b", "gn_g", "g_post")


def _small_update(gathered, wmv):
    ns = len(SMALL)

    def body(*refs):
        gin_ref, gmid_ref, ggn_ref, gatt_ref, gl1_ref, gl2_ref = refs[:6]
        wmv_refs = refs[6:6 + 3 * ns]
        loss_ref = refs[6 + 3 * ns]
        out_refs = refs[7 + 3 * ns:]

        def dsum(ref, r=None):
            rows = slice(None) if r is None else slice(r, r + 1)
            acc = ref[0, rows, :]
            for d in range(1, NDEV):
                acc = acc + ref[d, rows, :]
            return acc

        s_lg = dsum(gl1_ref) + dsum(gl2_ref)
        loss_ref[...] = (0.5 / D) * jnp.sum(dsum(gmid_ref, 2), axis=-1, keepdims=True)
        eye = lax.broadcasted_iota(jnp.int32, (8, 128), 0) == lax.broadcasted_iota(jnp.int32, (8, 128), 1)
        dlg = jnp.sum(jnp.where(eye, s_lg, 0.0), axis=0, keepdims=True)
        w_f, w_b = wmv_refs[3 * SMALL.index("w_dec_f")][...], wmv_refs[3 * SMALL.index("w_dec_b")][...]
        s_q, s_k = dsum(gatt_ref, 0), dsum(gatt_ref, 1)
        grads = dict(
            b_ada=jnp.concatenate([dsum(gin_ref, 0), dsum(gin_ref, 1), dsum(gmid_ref, 0)], axis=1),
            g_pre=dsum(gin_ref, 2), g_post=dsum(gmid_ref, 1), gn_g=dsum(ggn_ref),
            qn_g=s_q[:, :DH] + s_q[:, DH:], kn_g=s_k[:, :DH] + s_k[:, DH:],
            w_dec_f=dlg[:, 0:HR] * _sigmoid(-w_f), w_dec_b=dlg[:, HR:2 * HR] * _sigmoid(-w_b))
        for i, nme in enumerate(SMALL):
            g = grads[nme]
            w_ref, m_ref, v_ref = wmv_refs[3 * i:3 * i + 3]
            g_ref, d_ref, nm_ref, nv_ref = out_refs[4 * i:4 * i + 4]
            g_ref[...] = g
            m2 = ADAM_B1 * m_ref[...] + (1.0 - ADAM_B1) * g
            v2 = ADAM_B2 * v_ref[...] + (1.0 - ADAM_B2) * jnp.square(g)
            m_hat = m2 / (1.0 - ADAM_B1 ** ADAM_STEP)
            v_hat = v2 / (1.0 - ADAM_B2 ** ADAM_STEP)
            d_ref[...] = -ADAM_LR * (m_hat / (jnp.sqrt(v_hat) + ADAM_EPS) + ADAM_WD * w_ref[...])
            nm_ref[...] = m2
            nv_ref[...] = v2

    out_shape = [jax.ShapeDtypeStruct((1, 1), F32)]
    for i in range(ns):
        out_shape += [jax.ShapeDtypeStruct(wmv[3 * i].shape, F32)] * 4
    return pl.pallas_call(body, name="small_update", out_shape=out_shape)(*gathered, *wmv)


def _adamw(parts, w, m, v, name):
    n, R, L = parts.shape
    tr = 256 if (R % 256 == 0 and R > 256) else R

    def body(p_ref, w_ref, m_ref, v_ref, g_ref, d_ref, nm_ref, nv_ref):
        g = p_ref[0].astype(F32)
        for k in range(1, n):
            g = g + p_ref[k].astype(F32)
        g_ref[...] = g
        m2 = ADAM_B1 * m_ref[...] + (1.0 - ADAM_B1) * g
        v2 = ADAM_B2 * v_ref[...] + (1.0 - ADAM_B2) * jnp.square(g)
        m_hat = m2 / (1.0 - ADAM_B1 ** ADAM_STEP)
        v_hat = v2 / (1.0 - ADAM_B2 ** ADAM_STEP)
        d_ref[...] = -ADAM_LR * (m_hat / (jnp.sqrt(v_hat) + ADAM_EPS) + ADAM_WD * w_ref[...])
        nm_ref[...] = m2
        nv_ref[...] = v2

    blk = pl.BlockSpec((tr, L), lambda i: (i, 0))
    o = jax.ShapeDtypeStruct((R, L), F32)
    return pl.pallas_call(
        body, name=name, out_shape=[o, o, o, o], grid=(R // tr,),
        in_specs=[pl.BlockSpec((n, tr, L), lambda i: (0, i, 0)), blk, blk, blk], out_specs=[blk, blk, blk, blk],
        compiler_params=_cp(("parallel",), VMEM_BIG),
    )(parts, w, m, v)


def _rope_tables(S):
    f = np.float32
    t = np.arange(S)
    row, col = (t // 64).astype(f), (t % 64).astype(f)
    half = DH // 2
    inv = np.power(f(ROPE_THETA), -np.arange(0, half, 2, dtype=f) / f(half)).astype(f)
    ar, ac = (row[:, None] * inv[None, :]).astype(f), (col[:, None] * inv[None, :]).astype(f)
    cos64 = np.concatenate([np.cos(ar), np.cos(ar), np.cos(ac), np.cos(ac)], axis=1).astype(f)
    sin64 = np.concatenate([-np.sin(ar), np.sin(ar), -np.sin(ac), np.sin(ac)], axis=1).astype(f)
    return jnp.asarray(np.tile(cos64, (1, 2))), jnp.asarray(np.tile(sin64, (1, 2)))


def _to_p_order(w_orig):
    return jnp.concatenate([w_orig[:, ORIG[n][0]:ORIG[n][1]] for n in P_ORDER], axis=1)


def _pad_lanes(v, n):
    return jnp.pad(v, ((0, 0), (0, n - v.shape[1])))


def kernel(x, c, w_ada, b_ada, g_pre, w_in, qn_g, kn_g, w_dec_f, w_dec_b, gn_g, w_pa, w_pr, w_out, g_post, loss_target, m_w_ada, m_b_ada, m_g_pre, m_w_in, m_qn_g, m_kn_g, m_w_dec_f, m_w_dec_b, m_gn_g, m_w_pa, m_w_pr, m_w_out, m_g_post, v_w_ada, v_b_ada, v_g_pre, v_w_in, v_qn_g, v_kn_g, v_w_dec_f, v_w_dec_b, v_gn_g, v_w_pa, v_w_pr, v_w_out, v_g_post):
    S = x.shape[1]
    me = 4 * lax.axis_index("x") + 2 * lax.axis_index("y") + lax.axis_index("c")
    xs, tgt = x[0], loss_target[0]
    ncol_ada = w_ada.shape[2]
    ncol_in = w_in.shape[2]

    b_ada_s = lax.dynamic_slice(b_ada, (0, me * ncol_ada), (1, ncol_ada))
    mod_all, c_act, (wg_in,) = _prologue(jnp.pad(c, ((0, 7), (0, 0))), w_ada[0], b_ada_s, [w_in[0].astype(BF16)])
    mod = lax.dynamic_index_in_dim(mod_all, me, axis=1, keepdims=False).reshape(3, D)
    w_p = _to_p_order(wg_in.transpose(1, 0, 2).reshape(D, NDEV * ncol_in))
    all_dev = tuple(range(NDEV))
    st_w, tok_w = _xchg_start([(w_pa[0].astype(BF16)[None], all_dev), (w_pr[0].astype(BF16)[None], all_dev),
                               (w_out[0].astype(BF16)[None], all_dev)], "wgather_start")

    cos, sin = _rope_tables(S)
    qg, kg = jnp.tile(qn_g, (1, 2)), jnp.tile(kn_g, (1, 2))

    p, h = _fwd_in(xs, mod, g_pre + tok_w[0:1, 0:1], w_p)
    qt, kh, kt, vh, vta, qr2, kr2 = _prep(p, cos, sin, qg, kg)
    o_att, o_t, lse = _attn_fwd(qt, kh, vta)
    dc, qdf, qdb, kdf, kdb, adec = _ret_tables(w_dec_f, w_dec_b)
    rf, rb = _ret_states(kr2, p, kdf, kdb, adec)
    yr = _ret_out(qr2, kr2, p, rf, rb, dc, qdf, qdb, gn_g)
    wg_pa, wg_pr, wg_out = _xchg_wait([st_w], st_w["lands"], [[0, 1, 2]], yr, "wgather_wait")
    w_pa_f = wg_pa.transpose(1, 0, 2).reshape(512, D)
    w_pr_f = wg_pr.transpose(1, 0, 2).reshape(512, D)
    w_out_f = wg_out.reshape(D, D)

    dout, do, dpm, dyr, mb, dub, yab, dab, drb_, sums_mid = _mid(xs, tgt, mod, g_post, o_att, p, yr, w_pa_f, w_pr_f, w_out_f)
    gw_out = _mm_tn(mb, dub, "gw_out", BF16)
    gw_pa = _mm_tn(yab, dab, "gw_pa", BF16)
    gw_pr = _mm_tn(yr, drb_, "gw_pr", BF16)
    gi_m = _mm_tn(h, dpm, "gw_in_mid", BF16)

    def shards(cols, nd):
        return cols.reshape(D, nd, ncol_in).transpose(1, 0, 2)

    st_a, tok_a = _xchg_start([
        (gw_out.reshape(NDEV, 128, D), all_dev),
        (gw_pa.reshape(512, NDEV, 128).transpose(1, 0, 2), all_dev),
        (gw_pr.reshape(512, NDEV, 128).transpose(1, 0, 2), all_dev),
        (shards(gi_m[:, 224:2048], 3), (5, 6, 7))], "xchg_start_a",
        lands=[None, None, None, jnp.zeros((NDEV, D, ncol_in), BF16)])
    dqt, dkt, dvt = _attn_bwd(qt, kh, kt, vh, do, o_t, lse + tok_a[0, 0])
    dpa, gs_att = _attn_prep_bwd(dqt, dkt, dvt, p, cos, sin, qg, kg)
    gi_a = _mm_tn(h, dpa, "gw_in_att", BF16)
    st_b, tok_b = _xchg_start([(shards(jnp.concatenate([gi_a, gi_m[:, 2048:2496]], axis=1), 2), (0, 1))], "xchg_start_b",
                              lands=[st_a["lands"][3]])
    dpra, dk_i, dv_i, drf, drb, dgn, dlg1 = _ret_bwd_chunk(qr2, kr2, p, rf, rb, dc, qdf, qdb, gn_g + tok_b[0:1, 0:1], dyr, cos, sin)
    dkf, dkb, dvf, dvb, dlg2 = _ret_bwd_scan(kr2, p, rf, rb, drf, drb, kdf, kdb, adec)
    dprb = _ret_bwd_final(dk_i, dkf, dkb, dv_i, dvf, dvb, cos, sin)
    gi_ra = _mm_tn(h, dpra, "gw_in_reta", BF16)
    gi_rb = _mm_tn(h, dprb, "gw_in_retb", BF16)
    chip_c = _pair_reduce(shards(jnp.concatenate([gi_m[:, 2496:2560], gi_ra[:, :256], gi_rb[:, 512:768], gi_rb[:, :512],
                                                  gi_ra[:, 256:768], gi_m[:, :224]], axis=1), 3), (2, 3, 4), "pair_reduce_c")
    st_c, tok_c = _xchg_start([(chip_c, (2, 3, 4, "same core"))], "xchg_start_c", lands=[st_b["lands"][0]])
    grad_x, sums_in = _bwd_in(dpm, dpa, dpra, dprb, w_p, xs, dout, mod, g_pre + tok_c[0:1, 0:1])

    gathered = _small_allgather([sums_in, sums_mid, dgn, gs_att, dlg1, dlg2], "ag_small")
    given = dict(b_ada=(b_ada, m_b_ada, v_b_ada), g_pre=(g_pre, m_g_pre, v_g_pre), qn_g=(qn_g, m_qn_g, v_qn_g), kn_g=(kn_g, m_kn_g, v_kn_g),
                 w_dec_f=(w_dec_f, m_w_dec_f, v_w_dec_f), w_dec_b=(w_dec_b, m_w_dec_b, v_w_dec_b), gn_g=(gn_g, m_gn_g, v_gn_g),
                 g_post=(g_post, m_g_post, v_g_post))
    small = _small_update(gathered, [a for nme in SMALL for a in given[nme]])
    loss = small[0][0, 0]

    g_in_all, g_mid_all = gathered[0], gathered[1]
    dmod_all = lax.dynamic_slice(jnp.concatenate([g_in_all[:, 0, :], g_in_all[:, 1, :], g_mid_all[:, 0, :]], axis=1),
                                 (0, me * ncol_ada), (NDEV, ncol_ada))
    g_ada = _mm_tn(c_act, jnp.pad(dmod_all, ((0, 8), (0, 0))).astype(BF16), "gw_ada")

    ada = _adamw(g_ada[None], w_ada[0], m_w_ada[0], v_w_ada[0], "adamw_ada")
    rs_out, rs_pa, rs_pr, rs_in = _xchg_wait([st_a, st_b, st_c], list(st_a["lands"][:3]) + [st_c["lands"][0]],
                                             [[0, 1, 2, 3], [3], [3]], ada[1], "xchg_wait")
    res = dict(
        w_ada=ada,
        w_in=_adamw(rs_in, w_in[0], m_w_in[0], v_w_in[0], "adamw_in"),
        w_pa=_adamw(rs_pa, w_pa[0], m_w_pa[0], v_w_pa[0], "adamw_pa"),
        w_pr=_adamw(rs_pr, w_pr[0], m_w_pr[0], v_w_pr[0], "adamw_pr"),
        w_out=_adamw(rs_out, w_out[0], m_w_out[0], v_w_out[0], "adamw_out"),
    )
    names = ["w_ada", "b_ada", "g_pre", "w_in", "qn_g", "kn_g", "w_dec_f", "w_dec_b", "gn_g", "w_pa", "w_pr", "w_out", "g_post"]
    outs = [[], [], [], []]
    for nme in names:
        for q in range(4):
            if nme in res:
                outs[q].append(res[nme][q][None])
            else:
                outs[q].append(small[1 + 4 * SMALL.index(nme) + q])
    return (loss, grad_x[None], *outs[0], *outs[1], *outs[2], *outs[3])
```

```python
import jax
import jax.numpy as jnp
import numpy as np
from jax import lax
from jax.experimental import pallas as pl
from jax.experimental.pallas import tpu as pltpu

F32, BF16 = jnp.float32, jnp.bfloat16
D = 1024
DH = 64
DHA = 80
DV = 128
LOG2E = 1.4426950408889634
LN2 = 0.6931471805599453
HR = 4
CH = 128
EPS = 1e-6
ROPE_THETA = 10000.0
NDEV = 8
O_GL, O_ZA, O_QA, O_KA, O_VA, O_QR, O_ZR, O_VR, O_KR, P_W = 0, 2048, 2560, 3072, 3200, 3328, 3584, 4096, 4608, 4864
ORIG = dict(qa=(0, 512), ka=(512, 640), va=(640, 768), za=(768, 1280), qr=(1280, 1536), kr=(1536, 1792),
            vr=(1792, 2304), zr=(2304, 2816), gl=(2816, 4864))
P_ORDER = ("gl", "za", "qa", "ka", "va", "qr", "zr", "vr", "kr")
ADAM_LR, ADAM_B1, ADAM_B2, ADAM_EPS, ADAM_WD, ADAM_STEP = 0.001, 0.9, 0.999, 1e-08, 0.01, 10
VMEM_BIG = 56 * 1024 * 1024
MESH = pl.DeviceIdType.MESH

NT = (((1,), (1,)), ((), ()))
TN = (((0,), (0,)), ((), ()))


def _dot(a, b, dims=None):
    if dims is None:
        return jnp.dot(a, b, preferred_element_type=F32)
    return lax.dot_general(a, b, dims, preferred_element_type=F32)


def _cp(sem=None, vmem=None):
    kw = {}
    if sem is not None:
        kw["dimension_semantics"] = sem
    if vmem is not None:
        kw["vmem_limit_bytes"] = vmem
    return pltpu.CompilerParams(**kw)


def _sigmoid(z):
    return 1.0 / (1.0 + jnp.exp(-z))


def _sum11(m):
    return jnp.sum(jnp.sum(m, axis=-1, keepdims=True), axis=0, keepdims=True)


def _full(shape):
    n = len(shape)
    return pl.BlockSpec(shape, lambda *_: (0,) * n)


def _my_pos():
    return lax.axis_index("x"), lax.axis_index("y"), lax.axis_index("c")


def _peer(k, x, y, c):
    return ((1 - x) if k & 4 else x, (1 - y) if k & 2 else y, (1 - c) if k & 1 else c)


def _small_allgather(vs, name):
    n = len(vs)

    def body(*refs):
        v_refs, out_refs = refs[:n], refs[n:2 * n]
        send_sems, recv_sems = refs[2 * n:]
        x, y, c = _my_pos()
        me = 4 * x + 2 * y + c
        cps = []
        for a in range(n):
            out_refs[a][me] = v_refs[a][...]
            for k in range(1, NDEV):
                cp = pltpu.make_async_remote_copy(src_ref=v_refs[a], dst_ref=out_refs[a].at[me], send_sem=send_sems.at[a, k - 1],
                                                  recv_sem=recv_sems.at[a, k - 1], device_id=_peer(k, x, y, c), device_id_type=MESH)
                cp.start()
                cps.append(cp)
        for cp in cps:
            cp.wait()

    vm = pl.BlockSpec(memory_space=pltpu.VMEM)
    return pl.pallas_call(
        body, name=name, out_shape=[jax.ShapeDtypeStruct((NDEV,) + v.shape, v.dtype) for v in vs],
        in_specs=[vm] * n, out_specs=[vm] * n,
        scratch_shapes=[pltpu.SemaphoreType.DMA((n, NDEV - 1)), pltpu.SemaphoreType.DMA((n, NDEV - 1))],
    )(*vs)


def _prologue(c8, w_ada_s, b_ada_s, arrs):
    n = len(arrs)
    ncol = w_ada_s.shape[1]

    def body(*refs):
        c_ref, wa_ref, ba_ref = refs[:3]
        ins = refs[3:3 + n]
        mod_ref, cact_ref = refs[3 + n:5 + n]
        outs = refs[5 + n:5 + 2 * n]
        call_ref, send_sems, recv_sems, local_sems, s_send, s_recv = refs[5 + 2 * n:]
        x, y, c = _my_pos()
        me, sibling = (x, y, c), (x, y, 1 - c)
        chips = [(1 - x, y), (x, 1 - y), (1 - x, 1 - y)]
        me_i = 4 * x + 2 * y + c

        def small_gather(src_ref, dst_ref, row):
            cps = []
            for k in range(1, NDEV):
                cp = pltpu.make_async_remote_copy(src_ref=src_ref, dst_ref=dst_ref.at[me_i], send_sem=s_send.at[row, k - 1],
                                                  recv_sem=s_recv.at[row, k - 1], device_id=_peer(k, x, y, c), device_id_type=MESH)
                cp.start()
                cps.append(cp)
            return cps

        def blk(a, px, py, pc):
            return outs[a].at[4 * px + 2 * py + pc]

        def copy(a, k, block, to, src=None):
            return pltpu.make_async_remote_copy(src_ref=blk(a, *block) if src is None else src, dst_ref=blk(a, *block),
                                                send_sem=send_sems.at[a, k], recv_sem=recv_sems.at[a, k], device_id=to, device_id_type=MESH)

        call_ref[me_i] = c_ref[...]
        for cp in small_gather(c_ref, call_ref, 0):
            cp.wait()

        local, sent = [], []
        for a in range(n):
            mine = pltpu.make_async_copy(ins[a], blk(a, *me), local_sems.at[a])
            mine.start()
            local.append(mine)
            first = [copy(a, 0, me, sibling, src=ins[a])] + [copy(a, 1 + j, me, (*chip, c), src=ins[a]) for j, chip in enumerate(chips)]
            for cp in first:
                cp.start()
            sent += first

        cv = call_ref[:, 0, :]
        ca = jnp.concatenate([cv * _sigmoid(cv), jnp.zeros_like(cv)], axis=0).astype(BF16)
        cact_ref[...] = ca
        mod_ref[me_i] = (_dot(ca, wa_ref[...].astype(BF16)) + ba_ref[...])[:8]
        mod_copies = small_gather(mod_ref.at[me_i], mod_ref, 1)

        for j, chip in enumerate(chips):
            for a in range(n):
                copy(a, 1 + j, (*chip, c), me).wait_recv()
                cp = copy(a, 4 + j, (*chip, c), sibling)
                cp.start()
                sent.append(cp)
        for a in range(n):
            copy(a, 0, sibling, me).wait_recv()
            for j, chip in enumerate(chips):
                copy(a, 4 + j, (*chip, 1 - c), me).wait_recv()
        for cp in sent:
            cp.wait_send()
        for cp in local + mod_copies:
            cp.wait()

    vm, hbm = pl.BlockSpec(memory_space=pltpu.VMEM), pl.BlockSpec(memory_space=pl.ANY)
    res = pl.pallas_call(
        body, name="prologue",
        out_shape=[jax.ShapeDtypeStruct((NDEV, 8, ncol), F32), jax.ShapeDtypeStruct((16, D), BF16)]
        + [jax.ShapeDtypeStruct((NDEV,) + a.shape, a.dtype) for a in arrs],
        in_specs=[vm, vm, vm] + [hbm] * n, out_specs=[vm, vm] + [hbm] * n,
        scratch_shapes=[pltpu.VMEM((NDEV, 8, D), F32), pltpu.SemaphoreType.DMA((n, NDEV - 1)), pltpu.SemaphoreType.DMA((n, NDEV - 1)),
                        pltpu.SemaphoreType.DMA((n,)), pltpu.SemaphoreType.DMA((2, NDEV - 1)), pltpu.SemaphoreType.DMA((2, NDEV - 1))],
    )(c8, w_ada_s, b_ada_s, *arrs)
    return res[0], res[1], res[2:]


def _in_set(idx, dests):
    p = idx == dests[0]
    for d in dests[1:]:
        p = jnp.logical_or(p, idx == d)
    return p


_HBM = pl.BlockSpec(memory_space=pltpu.HBM)
_SEM = pl.BlockSpec(memory_space=pltpu.SEMAPHORE)


def _pair_reduce(send, dests, name):
    nd = send.shape[0]

    def body(s_ref, o_ref, land, ssem, rsem):
        x, y, c = _my_pos()
        cps = []
        for i in range(nd):
            cp = pltpu.make_async_remote_copy(src_ref=s_ref.at[i], dst_ref=land.at[i], send_sem=ssem.at[i], recv_sem=rsem.at[i],
                                              device_id=(x, y, 1 - c), device_id_type=MESH)
            pl.when(c != (dests[i] & 1))(cp.start)
            cps.append(cp)
        for i in range(nd):
            mine = c == (dests[i] & 1)

            @pl.when(mine)
            def _():
                cps[i].wait_recv()
                o_ref[i] = (s_ref[i].astype(F32) + land[i].astype(F32)).astype(BF16)

            pl.when(jnp.logical_not(mine))(cps[i].wait_send)

    vm = pl.BlockSpec(memory_space=pltpu.VMEM)
    return pl.pallas_call(
        body, name=name, out_shape=jax.ShapeDtypeStruct(send.shape, send.dtype), in_specs=[vm], out_specs=vm,
        scratch_shapes=[pltpu.VMEM(send.shape, send.dtype), pltpu.SemaphoreType.DMA((nd,)), pltpu.SemaphoreType.DMA((nd,))],
        compiler_params=_cp(None, VMEM_BIG),
    )(send)


def _xchg_copies(xs_dests, sends, lands, ssem, rsem, lsem):
    x, y, c = _my_pos()
    me = 4 * x + 2 * y + c
    remote, local = [], []
    for a, dests in enumerate(xs_dests):
        same_core = dests[-1] == "same core"
        dests = dests[:-1] if same_core else dests
        lo, nd = dests[0], sends[a].shape[0]
        for k in range(1, NDEV):
            if same_core and k & 1:
                continue
            px, py, pc = _peer(k, x, y, c)
            pidx = 4 * px + 2 * py + pc
            cp = pltpu.make_async_remote_copy(src_ref=sends[a].at[jnp.clip(pidx - lo, 0, nd - 1)], dst_ref=lands[a].at[me],
                                              send_sem=ssem.at[a * (NDEV - 1) + k - 1], recv_sem=rsem.at[a * (NDEV - 1) + k - 1],
                                              device_id=(px, py, pc), device_id_type=MESH)
            remote.append((cp, _in_set(pidx, dests), _in_set(me, dests)))
        lc = pltpu.make_async_copy(sends[a].at[jnp.clip(me - lo, 0, nd - 1)], lands[a].at[me], lsem.at[a])
        local.append((lc, _in_set(me, dests)))
    return remote, local


def _xchg_start(xs, name, lands=None):
    n = len(xs)
    dests = [d for _, d in xs]
    sends = [pltpu.with_memory_space_constraint(s, pltpu.HBM) for s, _ in xs]
    lands = [None] * n if lands is None else lands
    lands = [pltpu.with_memory_space_constraint(lax.empty((NDEV,) + s.shape[1:], s.dtype) if l is None else l, pltpu.HBM)
             for (s, _), l in zip(xs, lands)]

    def body(*refs):
        send_refs, land_refs = refs[:n], refs[n:2 * n]
        ssem, rsem, lsem = refs[2 * n:2 * n + 3]
        token = refs[-1]
        remote, local = _xchg_copies(dests, send_refs, land_refs, ssem, rsem, lsem)
        for cp, to_dest, _ in remote:
            pl.when(to_dest)(cp.start)
        for lc, i_am_dest in local:
            pl.when(i_am_dest)(lc.start)
        token[...] = jnp.zeros_like(token)

    res = pl.pallas_call(
        body, name=name,
        out_shape=[pltpu.SemaphoreType.DMA((n * (NDEV - 1),)), pltpu.SemaphoreType.DMA((n * (NDEV - 1),)), pltpu.SemaphoreType.DMA((n,))]
        + [pltpu.HBM(a.shape, a.dtype) for a in list(sends) + list(lands)] + [jax.ShapeDtypeStruct((8, 128), F32)],
        in_specs=[_HBM] * (2 * n), out_specs=[_SEM, _SEM, _SEM] + [_HBM] * (2 * n) + [pl.BlockSpec(memory_space=pltpu.VMEM)],
        input_output_aliases={i: 3 + i for i in range(2 * n)},
        compiler_params=pltpu.CompilerParams(has_side_effects=pltpu.SideEffectType.DATAFLOW_SIDE_EFFECTING),
    )(*sends, *lands)
    return dict(sems=res[0:3], sends=res[3:3 + n], lands=res[3 + n:3 + 2 * n], dests=dests), res[-1]


def _xchg_wait(states, lands, land_of, after, name):
    flat = []
    for st in states:
        flat += list(st["sends"]) + list(st["sems"])
    nl = len(lands)

    def body(*refs):
        land_refs = refs[:nl]
        pos = nl
        for s, st in enumerate(states):
            n = len(st["dests"])
            send_refs = refs[pos:pos + n]
            ssem, rsem, lsem = refs[pos + n:pos + n + 3]
            pos += n + 3
            remote, local = _xchg_copies(st["dests"], send_refs, [land_refs[i] for i in land_of[s]], ssem, rsem, lsem)
            for cp, to_dest, i_am_dest in remote:
                pl.when(to_dest)(cp.wait_send)
                pl.when(i_am_dest)(cp.wait_recv)
            for lc, i_am_dest in local:
                pl.when(i_am_dest)(lc.wait)

    in_specs = [_HBM] * nl
    for st in states:
        in_specs += [_HBM] * len(st["dests"]) + [_SEM, _SEM, _SEM]
    return pl.pallas_call(
        body, name=name, out_shape=[pltpu.HBM(a.shape, a.dtype) for a in lands],
        in_specs=in_specs + [pl.BlockSpec(memory_space=pl.ANY)], out_specs=[_HBM] * nl,
        input_output_aliases={i: i for i in range(nl)},
        compiler_params=pltpu.CompilerParams(has_side_effects=pltpu.SideEffectType.DATAFLOW_SIDE_EFFECTING),
    )(*lands, *flat, after)


def _mm_tn(a, b, name, out_dtype=F32):
    S, M = a.shape
    N = b.shape[1]
    tk = min(2048, S)
    tn = N if N <= 768 else (640 if N % 640 == 0 else 512)
    nk = S // tk

    def body(a_ref, b_ref, o_ref, acc):
        k = pl.program_id(1)

        @pl.when(k == 0)
        def _():
            acc[...] = _dot(a_ref[...], b_ref[...], TN)

        @pl.when(k > 0)
        def _():
            acc[...] += _dot(a_ref[...], b_ref[...], TN)

        @pl.when(k == nk - 1)
        def _():
            o_ref[...] = acc[...].astype(out_dtype)

    return pl.pallas_call(
        body, name=name, out_shape=jax.ShapeDtypeStruct((M, N), out_dtype), grid=(N // tn, nk),
        in_specs=[pl.BlockSpec((tk, M), lambda j, k: (k, 0)), pl.BlockSpec((tk, tn), lambda j, k: (k, j))],
        out_specs=pl.BlockSpec((M, tn), lambda j, k: (0, j)), scratch_shapes=[pltpu.VMEM((M, tn), F32)],
        compiler_params=_cp(("parallel", "arbitrary"), VMEM_BIG),
    )(a, b)


def _fwd_in(x, mod, g_pre, w_p):
    S = x.shape[0]
    tm = min(512, S)

    def body(x_ref, mod_ref, g_ref, w_ref, p_ref, h_ref):
        xv = x_ref[...]
        r = lax.rsqrt(jnp.mean(xv * xv, axis=-1, keepdims=True) + EPS)
        h = (((xv * r) * g_ref[...]) * (1.0 + mod_ref[1:2, :]) + mod_ref[0:1, :]).astype(BF16)
        h_ref[...] = h
        p_ref[...] = _dot(h, w_ref[...]).astype(BF16)

    return pl.pallas_call(
        body, name="fwd_in", out_shape=[jax.ShapeDtypeStruct((S, P_W), BF16), jax.ShapeDtypeStruct((S, D), BF16)],
        grid=(S // tm,),
        in_specs=[pl.BlockSpec((tm, D), lambda i: (i, 0)), _full((3, D)), _full((1, D)), _full((D, P_W))],
        out_specs=[pl.BlockSpec((tm, P_W), lambda i: (i, 0)), pl.BlockSpec((tm, D), lambda i: (i, 0))],
        compiler_params=_cp(("parallel",), VMEM_BIG),
    )(x, mod, g_pre, w_p)


def _swap16(v):
    lane = lax.broadcasted_iota(jnp.int32, v.shape, 1)
    return jnp.where((lane % 32) < 16, pltpu.roll(v, 112, 1), pltpu.roll(v, 16, 1))


def _rope(v, cos, sin):
    return v * cos + _swap16(v) * sin


def _rope_t(v, cos, sin):
    return v * cos - _swap16(v) * sin


def _head_mean(v):
    lo = lax.broadcasted_iota(jnp.int32, v.shape, 1) < 64
    m0 = jnp.sum(jnp.where(lo, v, 0.0), axis=-1, keepdims=True)
    m1 = jnp.sum(jnp.where(lo, 0.0, v), axis=-1, keepdims=True)
    return jnp.where(lo, m0, m1) * (1.0 / 64.0)


def _prep(p, cos, sin, qg, kg):
    S = p.shape[0]
    tm = min(512, S)

    def body(qa_ref, kv_ref, qr_ref, kr_ref, cos_ref, sin_ref, qg_ref, kg_ref, qt_ref, kh_ref, kt_ref, vh_ref, vta_ref, qr2_ref, kr2_ref):
        cos_v, sin_v = cos_ref[...], sin_ref[...]
        for g in range(4):
            xv = qa_ref[:, 128 * g:128 * g + 128].astype(F32)
            r = lax.rsqrt(_head_mean(xv * xv) + EPS)
            yt = (_rope((xv * r) * qg_ref[...], cos_v, sin_v) * (0.125 * LOG2E)).T
            qt_ref[2 * g] = yt[:DH].astype(BF16)
            qt_ref[2 * g + 1] = yt[DH:].astype(BF16)
        xv = kv_ref[:, :128].astype(F32)
        r = lax.rsqrt(_head_mean(xv * xv) + EPS)
        yv = _rope((xv * r) * kg_ref[...], cos_v, sin_v)
        kh_ref[0] = yv[:, :64].astype(BF16)
        kh_ref[1] = yv[:, 64:].astype(BF16)
        yt = yv.T
        kt_ref[0] = yt[:DH].astype(BF16)
        kt_ref[1] = yt[DH:].astype(BF16)
        vv = kv_ref[:, 128:].astype(F32)
        vh_ref[0] = vv[:, :64].astype(BF16)
        vh_ref[1] = vv[:, 64:].astype(BF16)
        vt = vv.T
        tail = (lax.broadcasted_iota(jnp.int32, (DHA - DH, tm), 0) == 0).astype(BF16)
        for kvh in range(2):
            vta_ref[kvh, 0:DH, :] = vt[DH * kvh:DH * kvh + DH].astype(BF16)
            vta_ref[kvh, DH:DHA, :] = tail
        for g in range(2):
            sl = slice(128 * g, 128 * g + 128)
            qr2_ref[:, sl] = _rope(qr_ref[:, sl].astype(F32), cos_v, sin_v)
            kr2_ref[:, sl] = _rope(kr_ref[:, sl].astype(F32), cos_v, sin_v) * 0.125

    hm = lambda n: pl.BlockSpec((n, tm, DH), lambda i: (0, i, 0))
    ht = lambda n, r: pl.BlockSpec((n, r, tm), lambda i: (0, 0, i))
    return pl.pallas_call(
        body, name="prep",
        out_shape=[jax.ShapeDtypeStruct((8, DH, S), BF16), jax.ShapeDtypeStruct((2, S, DH), BF16), jax.ShapeDtypeStruct((2, DH, S), BF16),
                   jax.ShapeDtypeStruct((2, S, DH), BF16), jax.ShapeDtypeStruct((2, DHA, S), BF16),
                   jax.ShapeDtypeStruct((S, 256), F32), jax.ShapeDtypeStruct((S, 256), F32)],
        grid=(S // tm,),
        in_specs=[pl.BlockSpec((tm, 512), lambda i: (i, O_QA // 512)), pl.BlockSpec((tm, 256), lambda i: (i, O_KA // 256)),
                  pl.BlockSpec((tm, 256), lambda i: (i, O_QR // 256)), pl.BlockSpec((tm, 256), lambda i: (i, O_KR // 256)),
                  pl.BlockSpec((tm, 128), lambda i: (i, 0)), pl.BlockSpec((tm, 128), lambda i: (i, 0)), _full((1, 128)), _full((1, 128))],
        out_specs=[ht(8, DH), hm(2), ht(2, DH), hm(2), ht(2, DHA), pl.BlockSpec((tm, 256), lambda i: (i, 0)), pl.BlockSpec((tm, 256), lambda i: (i, 0))],
        compiler_params=_cp(("parallel",)),
    )(p, p, p, p, cos, sin, qg, kg)


def _attn_fwd(qt, kh, vta):
    S = qt.shape[2]
    tq, tk = min(1024, S), min(512, S)
    nj = S // tk

    def body(q_ref, k_ref, v_ref, o_ref, ot_ref, lse_ref, m_s, acc_s):
        j = pl.program_id(1)

        @pl.when(j == 0)
        def _():
            m_s[...] = jnp.full_like(m_s, -jnp.inf)
            acc_s[...] = jnp.zeros_like(acc_s)

        m_all = m_s[...]
        st = {0: _dot(k_ref[0], q_ref[0])}
        m_new, acc_new = [], []
        for h in range(8):
            if h + 1 < 8:
                st[h + 1] = _dot(k_ref[(h + 1) // 4], q_ref[h + 1])
            m_old = m_all[h:h + 1, :]
            mn = jnp.maximum(m_old, jnp.max(st[h], axis=0, keepdims=True))
            pt = jnp.exp2(st[h] - mn).astype(BF16)
            acc_new.append(jnp.exp2(m_old - mn) * acc_s[h] + _dot(v_ref[h // 4], pt))
            m_new.append(mn)
            del st[h]
        for h in range(8):
            acc_s[h] = acc_new[h]
            m_s[h:h + 1, :] = m_new[h]

        @pl.when(j == nj - 1)
        def _():
            for h in range(8):
                ot = acc_s[h, 0:DH, :] / acc_s[h, DH:DH + 1, :]
                ot_ref[h] = ot
                o_ref[:, DH * h:DH * h + DH] = ot.T
                lse_ref[h // 4, h % 4:h % 4 + 1, :] = m_s[h:h + 1, :] + jnp.log2(acc_s[h, DH:DH + 1, :])

    return pl.pallas_call(
        body, name="attn_fwd",
        out_shape=[jax.ShapeDtypeStruct((S, 512), F32), jax.ShapeDtypeStruct((8, DH, S), F32), jax.ShapeDtypeStruct((2, 4, S), F32)],
        grid=(S // tq, nj),
        in_specs=[pl.BlockSpec((8, DH, tq), lambda i, j: (0, 0, i)), pl.BlockSpec((2, tk, DH), lambda i, j: (0, j, 0)),
                  pl.BlockSpec((2, DHA, tk), lambda i, j: (0, 0, j))],
        out_specs=[pl.BlockSpec((tq, 512), lambda i, j: (i, 0)), pl.BlockSpec((8, DH, tq), lambda i, j: (0, 0, i)),
                   pl.BlockSpec((2, 4, tq), lambda i, j: (0, 0, i))],
        scratch_shapes=[pltpu.VMEM((8, tq), F32), pltpu.VMEM((8, DHA, tq), F32)],
        compiler_params=_cp(("parallel", "arbitrary"), VMEM_BIG),
    )(qt, kh, vta)


def _ret_tables(wf, wb):
    C = CH

    def body(wf_ref, wb_ref, dc_ref, qdf_ref, qdb_ref, kdf_ref, kdb_ref, a_ref):
        def logsig(w):
            z = jnp.exp(-jnp.abs(w))
            u = 1.0 + z
            l1p = jnp.where(u == 1.0, z, jnp.log(u) * (z / jnp.where(u == 1.0, 1.0, u - 1.0)))
            return jnp.minimum(w, 0.0) - l1p

        lgf, lgb = logsig(wf_ref[...]), logsig(wb_ref[...])
        lane4 = lax.broadcasted_iota(jnp.int32, (1, 4), 1)

        def pick(lg, h):
            return jnp.sum(jnp.where(lane4 == h, lg, 0.0), axis=-1, keepdims=True)

        ii = lax.broadcasted_iota(jnp.int32, (C, C), 0).astype(F32)
        jj = lax.broadcasted_iota(jnp.int32, (C, C), 1).astype(F32)
        dif = ii - jj
        hd = lax.broadcasted_iota(jnp.int32, (C, 256), 1) // DH
        lf_l = jnp.zeros((C, 256), F32)
        lb_l = jnp.zeros((C, 256), F32)
        for h in range(HR):
            lf, lb = pick(lgf, h), pick(lgb, h)
            dc_ref[h] = jnp.where(dif >= 0, jnp.exp(lf * jnp.maximum(dif, 0.0)), jnp.exp(lb * jnp.maximum(-dif, 0.0)))
            lf_l = jnp.where(hd == h, lf, lf_l)
            lb_l = jnp.where(hd == h, lb, lb_l)
            a_ref[h:h + 1, :] = jnp.broadcast_to(jnp.exp(lf * C), (1, 128))
            a_ref[HR + h:HR + h + 1, :] = jnp.broadcast_to(jnp.exp(lb * C), (1, 128))
        ri = lax.broadcasted_iota(jnp.int32, (C, 256), 0).astype(F32)
        qdf_ref[...] = jnp.exp(lf_l * (ri + 1.0))
        qdb_ref[...] = jnp.exp(lb_l * (C - ri))
        kdf_ref[...] = jnp.exp(lf_l * (C - 1.0 - ri))
        kdb_ref[...] = jnp.exp(lb_l * ri)

    t = jax.ShapeDtypeStruct((C, 256), F32)
    return pl.pallas_call(body, name="ret_tables",
                          out_shape=[jax.ShapeDtypeStruct((HR, C, C), F32), t, t, t, t, jax.ShapeDtypeStruct((8, 128), F32)])(wf, wb)


def _ret_states(kr2, p, kdf, kdb, adec):
    S = kr2.shape[0]
    C, N = CH, S // CH
    G = _scan_group(N)
    NG = N // G

    def body(kf_ref, vf_ref, kb_ref, vb_ref, kdf_ref, kdb_ref, a_ref, rf_ref, rb_ref, sf, sb):
        @pl.when(pl.program_id(0) == 0)
        def _():
            sf[...] = jnp.zeros_like(sf)
            sb[...] = jnp.zeros_like(sb)

        kvf, kvb = [], []
        for u in range(G):
            rows = slice(C * u, C * u + C)
            kdfw = (kf_ref[rows, :] * kdf_ref[...]).astype(BF16)
            kdbw = (kb_ref[rows, :] * kdb_ref[...]).astype(BF16)
            vf, vb = vf_ref[rows, :].astype(BF16), vb_ref[rows, :].astype(BF16)
            kvf.append([_dot(kdfw[:, _ks(h)], vf[:, _vs(h)], TN) for h in range(HR)])
            kvb.append([_dot(kdbw[:, _ks(h)], vb[:, _vs(h)], TN) for h in range(HR)])
        for u in range(G):
            rf_ref[u] = sf[...]
            for h in range(HR):
                sf[h] = a_ref[h:h + 1, :] * sf[h] + kvf[u][h]
        for u in reversed(range(G)):
            rb_ref[u] = sb[...]
            for h in range(HR):
                sb[h] = a_ref[HR + h:HR + h + 1, :] * sb[h] + kvb[u][h]

    st = jax.ShapeDtypeStruct((N, HR, DH, DV), F32)
    return pl.pallas_call(
        body, name="ret_states", out_shape=[st, st], grid=(NG,),
        in_specs=[pl.BlockSpec((G * C, 256), lambda t: (t, 0)), pl.BlockSpec((G * C, 512), lambda t: (t, O_VR // 512)),
                  pl.BlockSpec((G * C, 256), lambda t: (NG - 1 - t, 0)), pl.BlockSpec((G * C, 512), lambda t: (NG - 1 - t, O_VR // 512)),
                  _full((C, 256)), _full((C, 256)), _full((8, 128))],
        out_specs=[pl.BlockSpec((G, HR, DH, DV), lambda t: (t, 0, 0, 0)), pl.BlockSpec((G, HR, DH, DV), lambda t: (NG - 1 - t, 0, 0, 0))],
        scratch_shapes=[pltpu.VMEM((HR, DH, DV), F32), pltpu.VMEM((HR, DH, DV), F32)],
        compiler_params=_cp(("arbitrary",)),
    )(kr2, p, kr2, p, kdf, kdb, adec)


def _scan_group(n):
    return 4 if n % 4 == 0 else (2 if n % 2 == 0 else 1)


def _ks(h):
    return slice(DH * h, DH * h + DH)


def _vs(h):
    return slice(DV * h, DV * h + DV)


def _ret_heads_fwd(qb, kb, vb, qfw, qbw, dc_ref, rf_ref, rb_ref, u=0):
    hs = range(HR)
    s = [_dot(qb[:, _ks(h)], kb[:, _ks(h)], NT) for h in hs]
    inter = [_dot(qfw[:, _ks(h)], rf_ref[u, h].astype(BF16)) + _dot(qbw[:, _ks(h)], rb_ref[u, h].astype(BF16)) for h in hs]
    sd = [s[h] * dc_ref[h] for h in hs]
    o = [_dot(sd[h].astype(BF16), vb[:, _vs(h)]) + inter[h] for h in hs]
    return sd, o


def _ret_out(qr2, kr2, p, rf, rb, dc, qdf, qdb, gn):
    S = qr2.shape[0]
    C, N = CH, S // CH
    G = _scan_group(N)

    def body(q_ref, k_ref, v_ref, z_ref, rf_ref, rb_ref, dc_ref, qdf_ref, qdb_ref, gn_ref, yr_ref):
        outs = []
        for u in range(G):
            rows = slice(C * u, C * u + C)
            qv = q_ref[rows, :]
            qb, kb, vb = qv.astype(BF16), k_ref[rows, :].astype(BF16), v_ref[rows, :].astype(BF16)
            qfw, qbw = (qv * qdf_ref[...]).astype(BF16), (qv * qdb_ref[...]).astype(BF16)
            outs.append(_ret_heads_fwd(qb, kb, vb, qfw, qbw, dc_ref, rf_ref, rb_ref, u)[1])
        for u in range(G):
            rows = slice(C * u, C * u + C)
            for h in range(HR):
                vs = _vs(h)
                o = outs[u][h]
                mu = jnp.mean(o, axis=-1, keepdims=True)
                var = jnp.mean(jnp.square(o - mu), axis=-1, keepdims=True)
                on = (o - mu) * lax.rsqrt(var + EPS)
                z = z_ref[rows, vs].astype(F32)
                yr_ref[rows, vs] = ((on * gn_ref[:, vs]) * (z * _sigmoid(z))).astype(BF16)

    row = lambda w, off=0: pl.BlockSpec((G * C, w), lambda t: (t, off))
    stb = lambda: pl.BlockSpec((G, HR, DH, DV), lambda t: (t, 0, 0, 0))
    return pl.pallas_call(
        body, name="ret_out", out_shape=jax.ShapeDtypeStruct((S, 512), BF16), grid=(N // G,),
        in_specs=[row(256), row(256), row(512, O_VR // 512), row(512, O_ZR // 512), stb(), stb(),
                  _full((HR, C, C)), _full((C, 256)), _full((C, 256)), _full((1, 512))],
        out_specs=row(512),
        compiler_params=_cp(("parallel",)),
    )(qr2, kr2, p, p, rf, rb, dc, qdf, qdb, gn)


def _mid(x, tgt, mod, g_post, o_att, p, yr, w_pa, w_pr, w_out):
    S = x.shape[0]
    tm = min(256, S)

    def body(x_ref, t_ref, mod_ref, gp_ref, o_ref, za_ref, gl_ref, yr_ref, wpa_ref, wpr_ref, wout_ref,
             dout_ref, do_ref, dpm_ref, dyr_ref, mb_ref, dub_ref, yab_ref, dab_ref, drb_ref, sums_ref):
        @pl.when(pl.program_id(0) == 0)
        def _():
            sums_ref[...] = jnp.zeros_like(sums_ref)

        za = za_ref[...].astype(F32)
        sa = _sigmoid(za)
        sil = za * sa
        ov = o_ref[...]
        ya_b = (ov * sil).astype(BF16)
        yr_b = yr_ref[...]
        av = _dot(ya_b, wpa_ref[...])
        rv = _dot(yr_b, wpr_ref[...])
        ga = _sigmoid(gl_ref[:, :D].astype(F32))
        gr = _sigmoid(gl_ref[:, D:].astype(F32))
        mb = (ga * av + gr * rv).astype(BF16)
        u = _dot(mb, wout_ref[...])
        r2 = lax.rsqrt(jnp.mean(u * u, axis=-1, keepdims=True) + EPS)
        un = u * r2
        gp = gp_ref[...]
        yv = un * gp
        gate = mod_ref[2:3, :]
        err = (x_ref[...] + gate * yv) - t_ref[...]
        dout = err * (1.0 / D)
        dout_ref[...] = dout
        dy = dout * gate
        sums_ref[0:1, :] += jnp.sum(dout * yv, axis=0, keepdims=True)
        sums_ref[1:2, :] += jnp.sum(dy * un, axis=0, keepdims=True)
        sums_ref[2:3, :] += jnp.sum(err * err, axis=0, keepdims=True)
        dyg = dy * gp
        du_b = (r2 * (dyg - un * jnp.mean(dyg * un, axis=-1, keepdims=True))).astype(BF16)
        dm = _dot(du_b, wout_ref[...], NT)
        da_b = (dm * ga).astype(BF16)
        dr_b = (dm * gr).astype(BF16)
        dpm_ref[:, :D] = (dm * av * (ga * (1.0 - ga))).astype(BF16)
        dpm_ref[:, D:2 * D] = (dm * rv * (gr * (1.0 - gr))).astype(BF16)
        dya = _dot(da_b, wpa_ref[...], NT)
        dyr_ref[...] = _dot(dr_b, wpr_ref[...], NT)
        dov = dya * sil
        for g in range(4):
            dt = dov[:, 128 * g:128 * g + 128].T
            do_ref[2 * g] = dt[:DH].astype(BF16)
            do_ref[2 * g + 1] = dt[DH:].astype(BF16)
        dpm_ref[:, 2 * D:] = (dya * ov * (sa * (1.0 + za * (1.0 - sa)))).astype(BF16)
        mb_ref[...] = mb
        dub_ref[...] = du_b
        yab_ref[...] = ya_b
        dab_ref[...] = da_b
        drb_ref[...] = dr_b

    row = lambda w: pl.BlockSpec((tm, w), lambda i: (i, 0))
    sd = lambda w, dt: jax.ShapeDtypeStruct((S, w), dt)
    return pl.pallas_call(
        body, name="mid",
        out_shape=[sd(D, F32), jax.ShapeDtypeStruct((8, DH, S), BF16), sd(2560, BF16), sd(512, F32), sd(D, BF16), sd(D, BF16), sd(512, BF16),
                   sd(D, BF16), sd(D, BF16), jax.ShapeDtypeStruct((8, D), F32)],
        grid=(S // tm,),
        in_specs=[row(D), row(D), _full((3, D)), _full((1, D)), row(512), pl.BlockSpec((tm, 512), lambda i: (i, O_ZA // 512)),
                  pl.BlockSpec((tm, 2048), lambda i: (i, 0)), row(512), _full((512, D)), _full((512, D)), _full((D, D))],
        out_specs=[row(D), pl.BlockSpec((8, DH, tm), lambda i: (0, 0, i)), row(2560), row(512), row(D), row(D), row(512), row(D), row(D),
                   _full((8, D))],
        compiler_params=_cp(("arbitrary",), VMEM_BIG),
    )(x, tgt, mod, g_post, o_att, p, p, yr, w_pa, w_pr, w_out)


def _attn_bwd(qt, kh, kt, vh, dot_, ot, lse):
    S = qt.shape[2]
    tq, tk = min(1024, S), min(1024, S)

    def body(q_ref, k_ref, kt_ref, v_ref, do_ref, o_ref, lse_ref, dq_ref, dk_ref, dv_ref):
        j, i = pl.program_id(0), pl.program_id(1)
        cols = pl.ds(pl.multiple_of(i * tq, tq), tq)
        st = {0: _dot(k_ref[0], q_ref[0])}
        dpt = {0: _dot(v_ref[0], do_ref[0])}
        dk_acc, dv_acc, dqs = [None, None], [None, None], []
        for h in range(8):
            g = h // 4
            if h + 1 < 8:
                st[h + 1] = _dot(k_ref[(h + 1) // 4], q_ref[h + 1])
                dpt[h + 1] = _dot(v_ref[(h + 1) // 4], do_ref[h + 1])
            qt_h, dot_h = q_ref[h], do_ref[h]
            delta = jnp.sum(dot_h.astype(F32) * o_ref[h], axis=0, keepdims=True)
            pt = jnp.exp2(st[h] - lse_ref[g, h % 4:h % 4 + 1, :])
            dst = (pt * (dpt[h] - delta)).astype(BF16)
            dv_h = _dot(dot_h, pt.astype(BF16), NT)
            dk_h = _dot(qt_h, dst, NT)
            dqs.append(_dot(kt_ref[g], dst))
            dv_acc[g] = dv_h if dv_acc[g] is None else dv_acc[g] + dv_h
            dk_acc[g] = dk_h if dk_acc[g] is None else dk_acc[g] + dk_h
            del st[h], dpt[h]

        @pl.when(i == 0)
        def _():
            for g in range(2):
                dk_ref[g] = dk_acc[g]
                dv_ref[g] = dv_acc[g]

        @pl.when(i > 0)
        def _():
            for g in range(2):
                dk_ref[g] += dk_acc[g]
                dv_ref[g] += dv_acc[g]

        @pl.when(j == 0)
        def _():
            for h in range(8):
                dq_ref[h, :, cols] = dqs[h]

        @pl.when(j > 0)
        def _():
            for h in range(8):
                dq_ref[h, :, cols] += dqs[h]

    return pl.pallas_call(
        body, name="attn_bwd",
        out_shape=[jax.ShapeDtypeStruct((8, DH, S), F32), jax.ShapeDtypeStruct((2, DH, S), F32), jax.ShapeDtypeStruct((2, DH, S), F32)],
        grid=(S // tk, S // tq),
        in_specs=[pl.BlockSpec((8, DH, tq), lambda j, i: (0, 0, i)), pl.BlockSpec((2, tk, DH), lambda j, i: (0, j, 0)),
                  pl.BlockSpec((2, DH, tk), lambda j, i: (0, 0, j)), pl.BlockSpec((2, tk, DH), lambda j, i: (0, j, 0)),
                  pl.BlockSpec((8, DH, tq), lambda j, i: (0, 0, i)), pl.BlockSpec((8, DH, tq), lambda j, i: (0, 0, i)),
                  pl.BlockSpec((2, 4, tq), lambda j, i: (0, 0, i))],
        out_specs=[pl.BlockSpec((8, DH, S), lambda j, i: (0, 0, 0)), pl.BlockSpec((2, DH, tk), lambda j, i: (0, 0, j)),
                   pl.BlockSpec((2, DH, tk), lambda j, i: (0, 0, j))],
        compiler_params=_cp(("arbitrary", "arbitrary"), VMEM_BIG),
    )(qt, kh, kt, vh, dot_, ot, lse)


def _attn_prep_bwd(dqt, dkt, dvt, p, cos, sin, qg, kg):
    S = dqt.shape[2]
    tm = min(512, S)

    def body(dq_ref, dk_ref, dv_ref, qa_ref, ka_ref, cos_ref, sin_ref, qg_ref, kg_ref, dp_ref, gs_ref):
        @pl.when(pl.program_id(0) == 0)
        def _():
            gs_ref[...] = jnp.zeros_like(gs_ref)

        cos_v, sin_v = cos_ref[...], sin_ref[...]

        def pair(ref, a):
            return jnp.concatenate([ref[a], ref[a + 1]], axis=0).T

        def norm_bwd(dyv, xv, gv, row):
            r = lax.rsqrt(_head_mean(xv * xv) + EPS)
            xn = xv * r
            dxh = _rope_t(dyv, cos_v, sin_v)
            gs_ref[row:row + 1, :] += jnp.sum(dxh * xn, axis=0, keepdims=True)
            dg = dxh * gv
            return r * (dg - xn * _head_mean(dg * xn))

        for g in range(4):
            sl = slice(128 * g, 128 * g + 128)
            dp_ref[:, sl] = norm_bwd(pair(dq_ref, 2 * g) * 0.125, qa_ref[:, sl].astype(F32), qg_ref[...], 0).astype(BF16)
        dp_ref[:, 512:640] = norm_bwd(pair(dk_ref, 0) * LN2, ka_ref[...].astype(F32), kg_ref[...], 1).astype(BF16)
        dp_ref[:, 640:768] = pair(dv_ref, 0).astype(BF16)

    ht = lambda n: pl.BlockSpec((n, DH, tm), lambda i: (0, 0, i))
    return pl.pallas_call(
        body, name="attn_prep_bwd", out_shape=[jax.ShapeDtypeStruct((S, 768), BF16), jax.ShapeDtypeStruct((8, 128), F32)],
        grid=(S // tm,),
        in_specs=[ht(8), ht(2), ht(2),
                  pl.BlockSpec((tm, 512), lambda i: (i, O_QA // 512)), pl.BlockSpec((tm, 128), lambda i: (i, O_KA // 128)),
                  pl.BlockSpec((tm, 128), lambda i: (i, 0)), pl.BlockSpec((tm, 128), lambda i: (i, 0)), _full((1, 128)), _full((1, 128))],
        out_specs=[pl.BlockSpec((tm, 768), lambda i: (i, 0)), _full((8, 128))],
        compiler_params=_cp(("arbitrary",)),
    )(dqt, dkt, dvt, p, p, cos, sin, qg, kg)


def _ret_bwd_chunk(qr2, kr2, p, rf, rb, dc, qdf, qdb, gn, dyr, cos, sin):
    S = qr2.shape[0]
    C, N = CH, S // CH
    G = 2 if N % 2 == 0 else 1

    def body(q_ref, k_ref, v_ref, z_ref, rf_ref, rb_ref, dc_ref, qdf_ref, qdb_ref, gn_ref, dyr_ref, cos_ref, sin_ref,
             dpa_ref, dk_ref, dv_ref, drf_ref, drb_ref, dgn_ref, dlg_ref, dqs):
        @pl.when(pl.program_id(0) == 0)
        def _():
            dgn_ref[...] = jnp.zeros_like(dgn_ref)
            dlg_ref[...] = jnp.zeros_like(dlg_ref)

        ii = lax.broadcasted_iota(jnp.int32, (C, C), 0).astype(F32)
        jj = lax.broadcasted_iota(jnp.int32, (C, C), 1).astype(F32)
        dif = ii - jj
        ri = lax.broadcasted_iota(jnp.int32, (C, 1), 0).astype(F32)
        hs, us = range(HR), range(G)
        rows = [slice(C * u, C * u + C) for u in us]
        qv = [q_ref[rows[u], :] for u in us]
        qb = [qv[u].astype(BF16) for u in us]
        kb = [k_ref[rows[u], :].astype(BF16) for u in us]
        vb = [v_ref[rows[u], :].astype(BF16) for u in us]
        qf32 = [qv[u] * qdf_ref[...] for u in us]
        qb32 = [qv[u] * qdb_ref[...] for u in us]
        qfw = [qf32[u].astype(BF16) for u in us]
        qbw = [qb32[u].astype(BF16) for u in us]
        fwd = [_ret_heads_fwd(qb[u], kb[u], vb[u], qfw[u], qbw[u], dc_ref, rf_ref, rb_ref, u) for u in us]
        sd = [f[0] for f in fwd]
        do_b = [[] for _ in us]
        for u in us:
            for h in hs:
                vs = _vs(h)
                o = fwd[u][1][h]
                mu = jnp.mean(o, axis=-1, keepdims=True)
                rstd = lax.rsqrt(jnp.mean(jnp.square(o - mu), axis=-1, keepdims=True) + EPS)
                on = (o - mu) * rstd
                z = z_ref[rows[u], vs].astype(F32)
                sz = _sigmoid(z)
                dy = dyr_ref[rows[u], vs]
                gnv = gn_ref[:, vs]
                dpa_ref[rows[u], 256 + DV * h:256 + DV * h + DV] = (dy * (on * gnv) * (sz * (1.0 + z * (1.0 - sz)))).astype(BF16)
                dys = dy * (z * sz)
                dgn_ref[:, vs] += jnp.sum(dys * on, axis=0, keepdims=True)
                don = dys * gnv
                do = rstd * (don - jnp.mean(don, axis=-1, keepdims=True) - on * jnp.mean(don * on, axis=-1, keepdims=True))
                do_b[u].append(do.astype(BF16))
        dpm = [[_dot(do_b[u][h], vb[u][:, _vs(h)], NT) for h in hs] for u in us]
        dqf = [[_dot(do_b[u][h], rf_ref[u, h].astype(BF16), NT) for h in hs] for u in us]
        dqb = [[_dot(do_b[u][h], rb_ref[u, h].astype(BF16), NT) for h in hs] for u in us]
        for u in us:
            for h in hs:
                dv_ref[rows[u], _vs(h)] = _dot(sd[u][h].astype(BF16), do_b[u][h], TN)
                drf_ref[u, h] = _dot(qfw[u][:, _ks(h)], do_b[u][h], TN)
                drb_ref[u, h] = _dot(qbw[u][:, _ks(h)], do_b[u][h], TN)
        dsd = [[(dpm[u][h] * dc_ref[h]).astype(BF16) for h in hs] for u in us]
        for u in us:
            for h in hs:
                ks = _ks(h)
                dqs[rows[u], ks] = _dot(dsd[u][h], kb[u][:, ks]) + dqf[u][h] * qdf_ref[:, ks] + dqb[u][h] * qdb_ref[:, ks]
                dk_ref[rows[u], ks] = _dot(dsd[u][h], qb[u][:, ks], TN)
        for u in us:
            for h in hs:
                ks = _ks(h)
                e = dpm[u][h] * sd[u][h]
                lf = (_sum11(e * jnp.maximum(dif, 0.0))
                      + _sum11(jnp.sum(qf32[u][:, ks] * dqf[u][h], axis=-1, keepdims=True) * (ri + 1.0)))
                lb = (_sum11(e * jnp.maximum(-dif, 0.0))
                      + _sum11(jnp.sum(qb32[u][:, ks] * dqb[u][h], axis=-1, keepdims=True) * (C - ri)))
                dlg_ref[h:h + 1, :] += jnp.broadcast_to(lf, (1, 128))
                dlg_ref[HR + h:HR + h + 1, :] += jnp.broadcast_to(lb, (1, 128))
            for g in range(2):
                sl = slice(128 * g, 128 * g + 128)
                dpa_ref[rows[u], sl] = _rope_t(dqs[rows[u], sl], cos_ref[rows[u], :], sin_ref[rows[u], :]).astype(BF16)

    st = jax.ShapeDtypeStruct((N, HR, DH, DV), F32)
    stb = lambda: pl.BlockSpec((G, HR, DH, DV), lambda t: (t, 0, 0, 0))
    row = lambda w, off=0: pl.BlockSpec((G * C, w), lambda t: (t, off))
    return pl.pallas_call(
        body, name="ret_bwd_chunk",
        out_shape=[jax.ShapeDtypeStruct((S, 768), BF16), jax.ShapeDtypeStruct((S, 256), F32), jax.ShapeDtypeStruct((S, 512), F32), st, st,
                   jax.ShapeDtypeStruct((1, 512), F32), jax.ShapeDtypeStruct((8, 128), F32)],
        grid=(N // G,),
        in_specs=[row(256), row(256), row(512, O_VR // 512), row(512, O_ZR // 512),
                  stb(), stb(), _full((HR, C, C)), _full((C, 256)), _full((C, 256)), _full((1, 512)), row(512), row(128), row(128)],
        out_specs=[row(768), row(256), row(512), stb(), stb(), _full((1, 512)), _full((8, 128))],
        scratch_shapes=[pltpu.VMEM((G * C, 256), F32)],
        compiler_params=_cp(("arbitrary",)),
    )(qr2, kr2, p, p, rf, rb, dc, qdf, qdb, gn, dyr, cos, sin)


def _ret_bwd_scan(kr2, p, rf, rb, drf, drb, kdf, kdb, adec):
    S = kr2.shape[0]
    C, N = CH, S // CH
    G = _scan_group(N)
    NG = N // G

    def body(kf_ref, vf_ref, kb_ref, vb_ref, rf_ref, rb_ref, drf_ref, drb_ref, kdf_ref, kdb_ref, a_ref,
             dkf_ref, dkb_ref, dvf_ref, dvb_ref, dlg_ref, gf, gb):
        @pl.when(pl.program_id(0) == 0)
        def _():
            gf[...] = jnp.zeros_like(gf)
            gb[...] = jnp.zeros_like(gb)
            dlg_ref[...] = jnp.zeros_like(dlg_ref)

        ri = lax.broadcasted_iota(jnp.int32, (C, 1), 0).astype(F32)

        def one(k_ref, v_ref, r_ref, dr_ref, kd_ref, g_s, dk_ref, dv_ref, row0, wexp, order):
            g = [g_s[h] for h in range(HR)]
            lgs = [jnp.zeros((1, 1), F32) for _ in range(HR)]
            for u in order:
                rows = slice(C * u, C * u + C)
                kd32 = k_ref[rows, :] * kd_ref[...]
                kdw = kd32.astype(BF16)
                vb = v_ref[rows, :].astype(BF16)
                for h in range(HR):
                    ks, vs = _ks(h), _vs(h)
                    g_b = g[h].astype(BF16)
                    dkd = _dot(vb[:, vs], g_b, NT)
                    dk_ref[rows, ks] = dkd * kd_ref[:, ks]
                    dv_ref[rows, vs] = _dot(kdw[:, ks], g_b)
                    av = a_ref[row0 + h:row0 + h + 1, :]
                    lgs[h] = lgs[h] + (_sum11(jnp.sum(kd32[:, ks] * dkd, axis=-1, keepdims=True) * wexp)
                                       + C * av[:, 0:1] * _sum11(r_ref[u, h] * g[h]))
                    g[h] = dr_ref[u, h] + av * g[h]
            for h in range(HR):
                g_s[h] = g[h]
                dlg_ref[row0 + h:row0 + h + 1, :] += jnp.broadcast_to(lgs[h], (1, 128))

        one(kf_ref, vf_ref, rf_ref, drf_ref, kdf_ref, gf, dkf_ref, dvf_ref, 0, C - 1.0 - ri, list(reversed(range(G))))
        one(kb_ref, vb_ref, rb_ref, drb_ref, kdb_ref, gb, dkb_ref, dvb_ref, HR, ri, list(range(G)))

    fwd = lambda w, off=0: pl.BlockSpec((G * C, w), lambda t: (NG - 1 - t, off))
    bwd = lambda w, off=0: pl.BlockSpec((G * C, w), lambda t: (t, off))
    stf = lambda: pl.BlockSpec((G, HR, DH, DV), lambda t: (NG - 1 - t, 0, 0, 0))
    stb = lambda: pl.BlockSpec((G, HR, DH, DV), lambda t: (t, 0, 0, 0))
    return pl.pallas_call(
        body, name="ret_bwd_scan",
        out_shape=[jax.ShapeDtypeStruct((S, 256), F32), jax.ShapeDtypeStruct((S, 256), F32), jax.ShapeDtypeStruct((S, 512), F32),
                   jax.ShapeDtypeStruct((S, 512), F32), jax.ShapeDtypeStruct((8, 128), F32)],
        grid=(NG,),
        in_specs=[fwd(256), fwd(512, O_VR // 512), bwd(256), bwd(512, O_VR // 512), stf(), stb(), stf(), stb(),
                  _full((C, 256)), _full((C, 256)), _full((8, 128))],
        out_specs=[fwd(256), bwd(256), fwd(512), bwd(512), _full((8, 128))],
        scratch_shapes=[pltpu.VMEM((HR, DH, DV), F32), pltpu.VMEM((HR, DH, DV), F32)],
        compiler_params=_cp(("arbitrary",)),
    )(kr2, p, kr2, p, rf, rb, drf, drb, kdf, kdb, adec)


def _ret_bwd_final(dk_i, dkf, dkb, dv_i, dvf, dvb, cos, sin):
    S = dk_i.shape[0]
    tm = min(512, S)

    def body(a_ref, b_ref, c_ref, d_ref, e_ref, f_ref, cos_ref, sin_ref, o_ref):
        o_ref[:, :512] = (d_ref[...] + e_ref[...] + f_ref[...]).astype(BF16)
        cos_v, sin_v = cos_ref[...], sin_ref[...]
        for g in range(2):
            sl = slice(128 * g, 128 * g + 128)
            dk = a_ref[:, sl] + b_ref[:, sl] + c_ref[:, sl]
            o_ref[:, 512 + 128 * g:512 + 128 * g + 128] = (_rope_t(dk, cos_v, sin_v) * 0.125).astype(BF16)

    row = lambda w: pl.BlockSpec((tm, w), lambda i: (i, 0))
    return pl.pallas_call(
        body, name="ret_bwd_final", out_shape=jax.ShapeDtypeStruct((S, 768), BF16), grid=(S // tm,),
        in_specs=[row(256), row(256), row(256), row(512), row(512), row(512), row(128), row(128)], out_specs=row(768),
        compiler_params=_cp(("parallel",)),
    )(dk_i, dkf, dkb, dv_i, dvf, dvb, cos, sin)


def _bwd_in(dpm, dpa, dpra, dprb, w_p, x, dout, mod, g_pre):
    S = x.shape[0]
    tm = min(256, S)

    def body(a_ref, b_ref, c_ref, d_ref, w_ref, x_ref, dout_ref, mod_ref, g_ref, gx_ref, sums_ref):
        @pl.when(pl.program_id(0) == 0)
        def _():
            sums_ref[...] = jnp.zeros_like(sums_ref)

        dh = (_dot(a_ref[...], w_ref[:, :O_QA], NT) + _dot(b_ref[...], w_ref[:, O_QA:O_QR], NT)
              + _dot(c_ref[...], w_ref[:, O_QR:O_VR], NT) + _dot(d_ref[...], w_ref[:, O_VR:], NT))
        xv = x_ref[...]
        r = lax.rsqrt(jnp.mean(xv * xv, axis=-1, keepdims=True) + EPS)
        xn = xv * r
        gv = g_ref[...]
        sc1 = 1.0 + mod_ref[1:2, :]
        sums_ref[0:1, :] += jnp.sum(dh, axis=0, keepdims=True)
        sums_ref[1:2, :] += jnp.sum(dh * (xn * gv), axis=0, keepdims=True)
        sums_ref[2:3, :] += jnp.sum(dh * xn, axis=0, keepdims=True) * sc1
        dxn = dh * (gv * sc1)
        gx_ref[...] = dout_ref[...] + r * (dxn - xn * jnp.mean(dxn * xn, axis=-1, keepdims=True))

    row = lambda w: pl.BlockSpec((tm, w), lambda i: (i, 0))
    return pl.pallas_call(
        body, name="bwd_in", out_shape=[jax.ShapeDtypeStruct((S, D), F32), jax.ShapeDtypeStruct((8, D), F32)], grid=(S // tm,),
        in_specs=[row(2560), row(768), row(768), row(768), _full((D, P_W)), row(D), row(D), _full((3, D)), _full((1, D))],
        out_specs=[row(D), _full((8, D))],
        compiler_params=_cp(("arbitrary",), VMEM_BIG),
    )(dpm, dpa, dpra, dprb, w_p, x, dout, mod, g_pre)


SMALL = ("b_ada", "g_pre", "qn_g", "kn_g", "w_dec_f", "w_dec_b", "gn_g", "g_post")


def _small_update(gathered, wmv):
    ns = len(SMALL)

    def body(*refs):
        gin_ref, gmid_ref, ggn_ref, gatt_ref, gl1_ref, gl2_ref = refs[:6]
        wmv_refs = refs[6:6 + 3 * ns]
        loss_ref = refs[6 + 3 * ns]
        out_refs = refs[7 + 3 * ns:]

        def dsum(ref, r=None):
            rows = slice(None) if r is None else slice(r, r + 1)
            acc = ref[0, rows, :]
            for d in range(1, NDEV):
                acc = acc + ref[d, rows, :]
            return acc

        s_lg = dsum(gl1_ref) + dsum(gl2_ref)
        loss_ref[...] = (0.5 / D) * jnp.sum(dsum(gmid_ref, 2), axis=-1, keepdims=True)
        eye = lax.broadcasted_iota(jnp.int32, (8, 128), 0) == lax.broadcasted_iota(jnp.int32, (8, 128), 1)
        dlg = jnp.sum(jnp.where(eye, s_lg, 0.0), axis=0, keepdims=True)
        w_f, w_b = wmv_refs[3 * SMALL.index("w_dec_f")][...], wmv_refs[3 * SMALL.index("w_dec_b")][...]
        s_q, s_k = dsum(gatt_ref, 0), dsum(gatt_ref, 1)
        grads = dict(
            b_ada=jnp.concatenate([dsum(gin_ref, 0), dsum(gin_ref, 1), dsum(gmid_ref, 0)], axis=1),
            g_pre=dsum(gin_ref, 2), g_post=dsum(gmid_ref, 1), gn_g=dsum(ggn_ref),
            qn_g=s_q[:, :DH] + s_q[:, DH:], kn_g=s_k[:, :DH] + s_k[:, DH:],
            w_dec_f=dlg[:, 0:HR] * _sigmoid(-w_f), w_dec_b=dlg[:, HR:2 * HR] * _sigmoid(-w_b))
        for i, nme in enumerate(SMALL):
            g = grads[nme]
            w_ref, m_ref, v_ref = wmv_refs[3 * i:3 * i + 3]
            g_ref, d_ref, nm_ref, nv_ref = out_refs[4 * i:4 * i + 4]
            g_ref[...] = g
            m2 = ADAM_B1 * m_ref[...] + (1.0 - ADAM_B1) * g
            v2 = ADAM_B2 * v_ref[...] + (1.0 - ADAM_B2) * jnp.square(g)
            m_hat = m2 / (1.0 - ADAM_B1 ** ADAM_STEP)
            v_hat = v2 / (1.0 - ADAM_B2 ** ADAM_STEP)
            d_ref[...] = -ADAM_LR * (m_hat / (jnp.sqrt(v_hat) + ADAM_EPS) + ADAM_WD * w_ref[...])
            nm_ref[...] = m2
            nv_ref[...] = v2

    out_shape = [jax.ShapeDtypeStruct((1, 1), F32)]
    for i in range(ns):
        out_shape += [jax.ShapeDtypeStruct(wmv[3 * i].shape, F32)] * 4
    return pl.pallas_call(body, name="small_update", out_shape=out_shape)(*gathered, *wmv)


def _adamw(parts, w, m, v, name):
    n, R, L = parts.shape
    tr = 256 if (R % 256 == 0 and R > 256) else R

    def body(p_ref, w_ref, m_ref, v_ref, g_ref, d_ref, nm_ref, nv_ref):
        g = p_ref[0].astype(F32)
        for k in range(1, n):
            g = g + p_ref[k].astype(F32)
        g_ref[...] = g
        m2 = ADAM_B1 * m_ref[...] + (1.0 - ADAM_B1) * g
        v2 = ADAM_B2 * v_ref[...] + (1.0 - ADAM_B2) * jnp.square(g)
        m_hat = m2 / (1.0 - ADAM_B1 ** ADAM_STEP)
        v_hat = v2 / (1.0 - ADAM_B2 ** ADAM_STEP)
        d_ref[...] = -ADAM_LR * (m_hat / (jnp.sqrt(v_hat) + ADAM_EPS) + ADAM_WD * w_ref[...])
        nm_ref[...] = m2
        nv_ref[...] = v2

    blk = pl.BlockSpec((tr, L), lambda i: (i, 0))
    o = jax.ShapeDtypeStruct((R, L), F32)
    return pl.pallas_call(
        body, name=name, out_shape=[o, o, o, o], grid=(R // tr,),
        in_specs=[pl.BlockSpec((n, tr, L), lambda i: (0, i, 0)), blk, blk, blk], out_specs=[blk, blk, blk, blk],
        compiler_params=_cp(("parallel",), VMEM_BIG),
    )(parts, w, m, v)


def _rope_tables(S):
    f = np.float32
    t = np.arange(S)
    row, col = (t // 64).astype(f), (t % 64).astype(f)
    half = DH // 2
    inv = np.power(f(ROPE_THETA), -np.arange(0, half, 2, dtype=f) / f(half)).astype(f)
    ar, ac = (row[:, None] * inv[None, :]).astype(f), (col[:, None] * inv[None, :]).astype(f)
    cos64 = np.concatenate([np.cos(ar), np.cos(ar), np.cos(ac), np.cos(ac)], axis=1).astype(f)
    sin64 = np.concatenate([-np.sin(ar), np.sin(ar), -np.sin(ac), np.sin(ac)], axis=1).astype(f)
    return jnp.asarray(np.tile(cos64, (1, 2))), jnp.asarray(np.tile(sin64, (1, 2)))


def _to_p_order(w_orig):
    return jnp.concatenate([w_orig[:, ORIG[n][0]:ORIG[n][1]] for n in P_ORDER], axis=1)


def _pad_lanes(v, n):
    return jnp.pad(v, ((0, 0), (0, n - v.shape[1])))


def kernel(x, c, w_ada, b_ada, g_pre, w_in, qn_g, kn_g, w_dec_f, w_dec_b, gn_g, w_pa, w_pr, w_out, g_post, loss_target, m_w_ada, m_b_ada, m_g_pre, m_w_in, m_qn_g, m_kn_g, m_w_dec_f, m_w_dec_b, m_gn_g, m_w_pa, m_w_pr, m_w_out, m_g_post, v_w_ada, v_b_ada, v_g_pre, v_w_in, v_qn_g, v_kn_g, v_w_dec_f, v_w_dec_b, v_gn_g, v_w_pa, v_w_pr, v_w_out, v_g_post):
    S = x.shape[1]
    me = 4 * lax.axis_index("x") + 2 * lax.axis_index("y") + lax.axis_index("c")
    xs, tgt = x[0], loss_target[0]
    ncol_ada = w_ada.shape[2]
    ncol_in = w_in.shape[2]

    b_ada_s = lax.dynamic_slice(b_ada, (0, me * ncol_ada), (1, ncol_ada))
    mod_all, c_act, (wg_in,) = _prologue(jnp.pad(c, ((0, 7), (0, 0))), w_ada[0], b_ada_s, [w_in[0].astype(BF16)])
    mod = lax.dynamic_index_in_dim(mod_all, me, axis=1, keepdims=False).reshape(3, D)
    w_p = _to_p_order(wg_in.transpose(1, 0, 2).reshape(D, NDEV * ncol_in))
    all_dev = tuple(range(NDEV))
    st_w, tok_w = _xchg_start([(w_pa[0].astype(BF16)[None], all_dev), (w_pr[0].astype(BF16)[None], all_dev),
                               (w_out[0].astype(BF16)[None], all_dev)], "wgather_start")

    cos, sin = _rope_tables(S)
    qg, kg = jnp.tile(qn_g, (1, 2)), jnp.tile(kn_g, (1, 2))

    p, h = _fwd_in(xs, mod, g_pre + tok_w[0:1, 0:1], w_p)
    qt, kh, kt, vh, vta, qr2, kr2 = _prep(p, cos, sin, qg, kg)
    o_att, o_t, lse = _attn_fwd(qt, kh, vta)
    dc, qdf, qdb, kdf, kdb, adec = _ret_tables(w_dec_f, w_dec_b)
    rf, rb = _ret_states(kr2, p, kdf, kdb, adec)
    yr = _ret_out(qr2, kr2, p, rf, rb, dc, qdf, qdb, gn_g)
    wg_pa, wg_pr, wg_out = _xchg_wait([st_w], st_w["lands"], [[0, 1, 2]], yr, "wgather_wait")
    w_pa_f = wg_pa.transpose(1, 0, 2).reshape(512, D)
    w_pr_f = wg_pr.transpose(1, 0, 2).reshape(512, D)
    w_out_f = wg_out.reshape(D, D)

    dout, do, dpm, dyr, mb, dub, yab, dab, drb_, sums_mid = _mid(xs, tgt, mod, g_post, o_att, p, yr, w_pa_f, w_pr_f, w_out_f)
    gw_out = _mm_tn(mb, dub, "gw_out", BF16)
    gw_pa = _mm_tn(yab, dab, "gw_pa", BF16)
    gw_pr = _mm_tn(yr, drb_, "gw_pr", BF16)
    gi_m = _mm_tn(h, dpm, "gw_in_mid", BF16)

    def shards(cols, nd):
        return cols.reshape(D, nd, ncol_in).transpose(1, 0, 2)

    st_a, tok_a = _xchg_start([
        (gw_out.reshape(NDEV, 128, D), all_dev),
        (gw_pa.reshape(512, NDEV, 128).transpose(1, 0, 2), all_dev),
        (gw_pr.reshape(512, NDEV, 128).transpose(1, 0, 2), all_dev),
        (shards(gi_m[:, 224:2048], 3), (5, 6, 7))], "xchg_start_a",
        lands=[None, None, None, jnp.zeros((NDEV, D, ncol_in), BF16)])
    dqt, dkt, dvt = _attn_bwd(qt, kh, kt, vh, do, o_t, lse + tok_a[0, 0])
    dpa, gs_att = _attn_prep_bwd(dqt, dkt, dvt, p, cos, sin, qg, kg)
    gi_a = _mm_tn(h, dpa, "gw_in_att", BF16)
    st_b, tok_b = _xchg_start([(shards(jnp.concatenate([gi_a, gi_m[:, 2048:2496]], axis=1), 2), (0, 1))], "xchg_start_b",
                              lands=[st_a["lands"][3]])
    dpra, dk_i, dv_i, drf, drb, dgn, dlg1 = _ret_bwd_chunk(qr2, kr2, p, rf, rb, dc, qdf, qdb, gn_g + tok_b[0:1, 0:1], dyr, cos, sin)
    dkf, dkb, dvf, dvb, dlg2 = _ret_bwd_scan(kr2, p, rf, rb, drf, drb, kdf, kdb, adec)
    dprb = _ret_bwd_final(dk_i, dkf, dkb, dv_i, dvf, dvb, cos, sin)
    gi_ra = _mm_tn(h, dpra, "gw_in_reta", BF16)
    gi_rb = _mm_tn(h, dprb, "gw_in_retb", BF16)
    chip_c = _pair_reduce(shards(jnp.concatenate([gi_m[:, 2496:2560], gi_ra[:, :256], gi_rb[:, 512:768], gi_rb[:, :512],
                                                  gi_ra[:, 256:768], gi_m[:, :224]], axis=1), 3), (2, 3, 4), "pair_reduce_c")
    st_c, tok_c = _xchg_start([(chip_c, (2, 3, 4, "same core"))], "xchg_start_c", lands=[st_b["lands"][0]])
    grad_x, sums_in = _bwd_in(dpm, dpa, dpra, dprb, w_p, xs, dout, mod, g_pre + tok_c[0:1, 0:1])

    gathered = _small_allgather([sums_in, sums_mid, dgn, gs_att, dlg1, dlg2], "ag_small")
    given = dict(b_ada=(b_ada, m_b_ada, v_b_ada), g_pre=(g_pre, m_g_pre, v_g_pre), qn_g=(qn_g, m_qn_g, v_qn_g), kn_g=(kn_g, m_kn_g, v_kn_g),
                 w_dec_f=(w_dec_f, m_w_dec_f, v_w_dec_f), w_dec_b=(w_dec_b, m_w_dec_b, v_w_dec_b), gn_g=(gn_g, m_gn_g, v_gn_g),
                 g_post=(g_post, m_g_post, v_g_post))
    small = _small_update(gathered, [a for nme in SMALL for a in given[nme]])
    loss = small[0][0, 0]

    g_in_all, g_mid_all = gathered[0], gathered[1]
    dmod_all = lax.dynamic_slice(jnp.concatenate([g_in_all[:, 0, :], g_in_all[:, 1, :], g_mid_all[:, 0, :]], axis=1),
                                 (0, me * ncol_ada), (NDEV, ncol_ada))
    g_ada = _mm_tn(c_act, jnp.pad(dmod_all, ((0, 8), (0, 0))).astype(BF16), "gw_ada")

    ada = _adamw(g_ada[None], w_ada[0], m_w_ada[0], v_w_ada[0], "adamw_ada")
    rs_out, rs_pa, rs_pr, rs_in = _xchg_wait([st_a, st_b, st_c], list(st_a["lands"][:3]) + [st_c["lands"][0]],
                                             [[0, 1, 2, 3], [3], [3]], ada[1], "xchg_wait")
    res = dict(
        w_ada=ada,
        w_in=_adamw(rs_in, w_in[0], m_w_in[0], v_w_in[0], "adamw_in"),
        w_pa=_adamw(rs_pa, w_pa[0], m_w_pa[0], v_w_pa[0], "adamw_pa"),
        w_pr=_adamw(rs_pr, w_pr[0], m_w_pr[0], v_w_pr[0], "adamw_pr"),
        w_out=_adamw(rs_out, w_out[0], m_w_out[0], v_w_out[0], "adamw_out"),
    )
    names = ["w_ada", "b_ada", "g_pre", "w_in", "qn_g", "kn_g", "w_dec_f", "w_dec_b", "gn_g", "w_pa", "w_pr", "w_out", "g_post"]
    outs = [[], [], [], []]
    for nme in names:
        for q in range(4):
            if nme in res:
                outs[q].append(res[nme][q][None])
            else:
                outs[q].append(small[1 + 4 * SMALL.index(nme) + q])
    return (loss, grad_x[None], *outs[0], *outs[1], *outs[2], *outs[3])
```

```python
import jax
import jax.numpy as jnp
import numpy as np
from jax import lax
from jax.experimental import pallas as pl
from jax.experimental.pallas import tpu as pltpu

F32, BF16 = jnp.float32, jnp.bfloat16
D = 1024
DH = 64
DHA = 80
DV = 128
LOG2E = 1.4426950408889634
LN2 = 0.6931471805599453
HR = 4
CH = 128
EPS = 1e-6
ROPE_THETA = 10000.0
NDEV = 8
O_GL, O_ZA, O_QA, O_KA, O_VA, O_QR, O_ZR, O_VR, O_KR, P_W = 0, 2048, 2560, 3072, 3200, 3328, 3584, 4096, 4608, 4864
ORIG = dict(qa=(0, 512), ka=(512, 640), va=(640, 768), za=(768, 1280), qr=(1280, 1536), kr=(1536, 1792),
            vr=(1792, 2304), zr=(2304, 2816), gl=(2816, 4864))
P_ORDER = ("gl", "za", "qa", "ka", "va", "qr", "zr", "vr", "kr")
ADAM_LR, ADAM_B1, ADAM_B2, ADAM_EPS, ADAM_WD, ADAM_STEP = 0.001, 0.9, 0.999, 1e-08, 0.01, 10
VMEM_BIG = 56 * 1024 * 1024
MESH = pl.DeviceIdType.MESH

NT = (((1,), (1,)), ((), ()))
TN = (((0,), (0,)), ((), ()))


def _dot(a, b, dims=None):
    if dims is None:
        return jnp.dot(a, b, preferred_element_type=F32)
    return lax.dot_general(a, b, dims, preferred_element_type=F32)


def _cp(sem=None, vmem=None):
    kw = {}
    if sem is not None:
        kw["dimension_semantics"] = sem
    if vmem is not None:
        kw["vmem_limit_bytes"] = vmem
    return pltpu.CompilerParams(**kw)


def _sigmoid(z):
    return 1.0 / (1.0 + jnp.exp(-z))


def _sum11(m):
    return jnp.sum(jnp.sum(m, axis=-1, keepdims=True), axis=0, keepdims=True)


def _full(shape):
    n = len(shape)
    return pl.BlockSpec(shape, lambda *_: (0,) * n)


def _my_pos():
    return lax.axis_index("x"), lax.axis_index("y"), lax.axis_index("c")


def _peer(k, x, y, c):
    return ((1 - x) if k & 4 else x, (1 - y) if k & 2 else y, (1 - c) if k & 1 else c)


def _small_allgather(vs, name):
    n = len(vs)

    def body(*refs):
        v_refs, out_refs = refs[:n], refs[n:2 * n]
        send_sems, recv_sems = refs[2 * n:]
        x, y, c = _my_pos()
        me = 4 * x + 2 * y + c
        cps = []
        for a in range(n):
            out_refs[a][me] = v_refs[a][...]
            for k in range(1, NDEV):
                cp = pltpu.make_async_remote_copy(src_ref=v_refs[a], dst_ref=out_refs[a].at[me], send_sem=send_sems.at[a, k - 1],
                                                  recv_sem=recv_sems.at[a, k - 1], device_id=_peer(k, x, y, c), device_id_type=MESH)
                cp.start()
                cps.append(cp)
        for cp in cps:
            cp.wait()

    vm = pl.BlockSpec(memory_space=pltpu.VMEM)
    return pl.pallas_call(
        body, name=name, out_shape=[jax.ShapeDtypeStruct((NDEV,) + v.shape, v.dtype) for v in vs],
        in_specs=[vm] * n, out_specs=[vm] * n,
        scratch_shapes=[pltpu.SemaphoreType.DMA((n, NDEV - 1)), pltpu.SemaphoreType.DMA((n, NDEV - 1))],
    )(*vs)


def _prologue(c8, w_ada_s, b_ada_s, arrs):
    n = len(arrs)
    ncol = w_ada_s.shape[1]

    def body(*refs):
        c_ref, wa_ref, ba_ref = refs[:3]
        ins = refs[3:3 + n]
        mod_ref, cact_ref = refs[3 + n:5 + n]
        outs = refs[5 + n:5 + 2 * n]
        call_ref, send_sems, recv_sems, local_sems, s_send, s_recv = refs[5 + 2 * n:]
        x, y, c = _my_pos()
        me, sibling = (x, y, c), (x, y, 1 - c)
        chips = [(1 - x, y), (x, 1 - y), (1 - x, 1 - y)]
        me_i = 4 * x + 2 * y + c

        def small_gather(src_ref, dst_ref, row):
            cps = []
            for k in range(1, NDEV):
                cp = pltpu.make_async_remote_copy(src_ref=src_ref, dst_ref=dst_ref.at[me_i], send_sem=s_send.at[row, k - 1],
                                                  recv_sem=s_recv.at[row, k - 1], device_id=_peer(k, x, y, c), device_id_type=MESH)
                cp.start()
                cps.append(cp)
            return cps

        def blk(a, px, py, pc):
            return outs[a].at[4 * px + 2 * py + pc]

        def copy(a, k, block, to, src=None):
            return pltpu.make_async_remote_copy(src_ref=blk(a, *block) if src is None else src, dst_ref=blk(a, *block),
                                                send_sem=send_sems.at[a, k], recv_sem=recv_sems.at[a, k], device_id=to, device_id_type=MESH)

        call_ref[me_i] = c_ref[...]
        for cp in small_gather(c_ref, call_ref, 0):
            cp.wait()

        local, sent = [], []
        for a in range(n):
            mine = pltpu.make_async_copy(ins[a], blk(a, *me), local_sems.at[a])
            mine.start()
            local.append(mine)
            first = [copy(a, 0, me, sibling, src=ins[a])] + [copy(a, 1 + j, me, (*chip, c), src=ins[a]) for j, chip in enumerate(chips)]
            for cp in first:
                cp.start()
            sent += first

        cv = call_ref[:, 0, :]
        ca = jnp.concatenate([cv * _sigmoid(cv), jnp.zeros_like(cv)], axis=0).astype(BF16)
        cact_ref[...] = ca
        mod_ref[me_i] = (_dot(ca, wa_ref[...].astype(BF16)) + ba_ref[...])[:8]
        mod_copies = small_gather(mod_ref.at[me_i], mod_ref, 1)

        for j, chip in enumerate(chips):
            for a in range(n):
                copy(a, 1 + j, (*chip, c), me).wait_recv()
                cp = copy(a, 4 + j, (*chip, c), sibling)
                cp.start()
                sent.append(cp)
        for a in range(n):
            copy(a, 0, sibling, me).wait_recv()
            for j, chip in enumerate(chips):
                copy(a, 4 + j, (*chip, 1 - c), me).wait_recv()
        for cp in sent:
            cp.wait_send()
        for cp in local + mod_copies:
            cp.wait()

    vm, hbm = pl.BlockSpec(memory_space=pltpu.VMEM), pl.BlockSpec(memory_space=pl.ANY)
    res = pl.pallas_call(
        body, name="prologue",
        out_shape=[jax.ShapeDtypeStruct((NDEV, 8, ncol), F32), jax.ShapeDtypeStruct((16, D), BF16)]
        + [jax.ShapeDtypeStruct((NDEV,) + a.shape, a.dtype) for a in arrs],
        in_specs=[vm, vm, vm] + [hbm] * n, out_specs=[vm, vm] + [hbm] * n,
        scratch_shapes=[pltpu.VMEM((NDEV, 8, D), F32), pltpu.SemaphoreType.DMA((n, NDEV - 1)), pltpu.SemaphoreType.DMA((n, NDEV - 1)),
                        pltpu.SemaphoreType.DMA((n,)), pltpu.SemaphoreType.DMA((2, NDEV - 1)), pltpu.SemaphoreType.DMA((2, NDEV - 1))],
    )(c8, w_ada_s, b_ada_s, *arrs)
    return res[0], res[1], res[2:]


def _in_set(idx, dests):
    p = idx == dests[0]
    for d in dests[1:]:
        p = jnp.logical_or(p, idx == d)
    return p


_HBM = pl.BlockSpec(memory_space=pltpu.HBM)
_SEM = pl.BlockSpec(memory_space=pltpu.SEMAPHORE)


def _pair_reduce(send, dests, name):
    nd = send.shape[0]

    def body(s_ref, o_ref, land, ssem, rsem):
        x, y, c = _my_pos()
        cps = []
        for i in range(nd):
            cp = pltpu.make_async_remote_copy(src_ref=s_ref.at[i], dst_ref=land.at[i], send_sem=ssem.at[i], recv_sem=rsem.at[i],
                                              device_id=(x, y, 1 - c), device_id_type=MESH)
            pl.when(c != (dests[i] & 1))(cp.start)
            cps.append(cp)
        for i in range(nd):
            mine = c == (dests[i] & 1)

            @pl.when(mine)
            def _():
                cps[i].wait_recv()
                o_ref[i] = (s_ref[i].astype(F32) + land[i].astype(F32)).astype(BF16)

            pl.when(jnp.logical_not(mine))(cps[i].wait_send)

    vm = pl.BlockSpec(memory_space=pltpu.VMEM)
    return pl.pallas_call(
        body, name=name, out_shape=jax.ShapeDtypeStruct(send.shape, send.dtype), in_specs=[vm], out_specs=vm,
        scratch_shapes=[pltpu.VMEM(send.shape, send.dtype), pltpu.SemaphoreType.DMA((nd,)), pltpu.SemaphoreType.DMA((nd,))],
        compiler_params=_cp(None, VMEM_BIG),
    )(send)


def _xchg_copies(xs_dests, sends, lands, ssem, rsem, lsem):
    x, y, c = _my_pos()
    me = 4 * x + 2 * y + c
    remote, local = [], []
    for a, dests in enumerate(xs_dests):
        same_core = dests[-1] == "same core"
        dests = dests[:-1] if same_core else dests
        lo, nd = dests[0], sends[a].shape[0]
        for k in range(1, NDEV):
            if same_core and k & 1:
                continue
            px, py, pc = _peer(k, x, y, c)
            pidx = 4 * px + 2 * py + pc
            cp = pltpu.make_async_remote_copy(src_ref=sends[a].at[jnp.clip(pidx - lo, 0, nd - 1)], dst_ref=lands[a].at[me],
                                              send_sem=ssem.at[a * (NDEV - 1) + k - 1], recv_sem=rsem.at[a * (NDEV - 1) + k - 1],
                                              device_id=(px, py, pc), device_id_type=MESH)
            remote.append((cp, _in_set(pidx, dests), _in_set(me, dests)))
        lc = pltpu.make_async_copy(sends[a].at[jnp.clip(me - lo, 0, nd - 1)], lands[a].at[me], lsem.at[a])
        local.append((lc, _in_set(me, dests)))
    return remote, local


def _xchg_start(xs, name, lands=None):
    n = len(xs)
    dests = [d for _, d in xs]
    sends = [pltpu.with_memory_space_constraint(s, pltpu.HBM) for s, _ in xs]
    lands = [None] * n if lands is None else lands
    lands = [pltpu.with_memory_space_constraint(lax.empty((NDEV,) + s.shape[1:], s.dtype) if l is None else l, pltpu.HBM)
             for (s, _), l in zip(xs, lands)]

    def body(*refs):
        send_refs, land_refs = refs[:n], refs[n:2 * n]
        ssem, rsem, lsem = refs[2 * n:2 * n + 3]
        token = refs[-1]
        remote, local = _xchg_copies(dests, send_refs, land_refs, ssem, rsem, lsem)
        for cp, to_dest, _ in remote:
            pl.when(to_dest)(cp.start)
        for lc, i_am_dest in local:
            pl.when(i_am_dest)(lc.start)
        token[...] = jnp.zeros_like(token)

    res = pl.pallas_call(
        body, name=name,
        out_shape=[pltpu.SemaphoreType.DMA((n * (NDEV - 1),)), pltpu.SemaphoreType.DMA((n * (NDEV - 1),)), pltpu.SemaphoreType.DMA((n,))]
        + [pltpu.HBM(a.shape, a.dtype) for a in list(sends) + list(lands)] + [jax.ShapeDtypeStruct((8, 128), F32)],
        in_specs=[_HBM] * (2 * n), out_specs=[_SEM, _SEM, _SEM] + [_HBM] * (2 * n) + [pl.BlockSpec(memory_space=pltpu.VMEM)],
        input_output_aliases={i: 3 + i for i in range(2 * n)},
        compiler_params=pltpu.CompilerParams(has_side_effects=pltpu.SideEffectType.DATAFLOW_SIDE_EFFECTING),
    )(*sends, *lands)
    return dict(sems=res[0:3], sends=res[3:3 + n], lands=res[3 + n:3 + 2 * n], dests=dests), res[-1]


def _xchg_wait(states, lands, land_of, after, name):
    flat = []
    for st in states:
        flat += list(st["sends"]) + list(st["sems"])
    nl = len(lands)

    def body(*refs):
        land_refs = refs[:nl]
        pos = nl
        for s, st in enumerate(states):
            n = len(st["dests"])
            send_refs = refs[pos:pos + n]
            ssem, rsem, lsem = refs[pos + n:pos + n + 3]
            pos += n + 3
            remote, local = _xchg_copies(st["dests"], send_refs, [land_refs[i] for i in land_of[s]], ssem, rsem, lsem)
            for cp, to_dest, i_am_dest in remote:
                pl.when(to_dest)(cp.wait_send)
                pl.when(i_am_dest)(cp.wait_recv)
            for lc, i_am_dest in local:
                pl.when(i_am_dest)(lc.wait)

    in_specs = [_HBM] * nl
    for st in states:
        in_specs += [_HBM] * len(st["dests"]) + [_SEM, _SEM, _SEM]
    return pl.pallas_call(
        body, name=name, out_shape=[pltpu.HBM(a.shape, a.dtype) for a in lands],
        in_specs=in_specs + [pl.BlockSpec(memory_space=pl.ANY)], out_specs=[_HBM] * nl,
        input_output_aliases={i: i for i in range(nl)},
        compiler_params=pltpu.CompilerParams(has_side_effects=pltpu.SideEffectType.DATAFLOW_SIDE_EFFECTING),
    )(*lands, *flat, after)


def _mm_tn(a, b, name, out_dtype=F32):
    S, M = a.shape
    N = b.shape[1]
    tn = N if N <= 768 else (640 if N % 640 == 0 else 512)
    tk = min(4096 if N > tn else 2048, S)
    nk = S // tk

    def body(a_ref, b_ref, o_ref, acc):
        k = pl.program_id(1)

        @pl.when(k == 0)
        def _():
            acc[...] = _dot(a_ref[...], b_ref[...], TN)

        @pl.when(k > 0)
        def _():
            acc[...] += _dot(a_ref[...], b_ref[...], TN)

        @pl.when(k == nk - 1)
        def _():
            o_ref[...] = acc[...].astype(out_dtype)

    return pl.pallas_call(
        body, name=name, out_shape=jax.ShapeDtypeStruct((M, N), out_dtype), grid=(N // tn, nk),
        in_specs=[pl.BlockSpec((tk, M), lambda j, k: (k, 0)), pl.BlockSpec((tk, tn), lambda j, k: (k, j))],
        out_specs=pl.BlockSpec((M, tn), lambda j, k: (0, j)), scratch_shapes=[pltpu.VMEM((M, tn), F32)],
        compiler_params=_cp(("parallel", "arbitrary"), VMEM_BIG),
    )(a, b)


def _fwd_in(x, mod, g_pre, w_p):
    S = x.shape[0]
    tm = min(512, S)

    def body(x_ref, mod_ref, g_ref, w_ref, p_ref, h_ref):
        xv = x_ref[...]
        r = lax.rsqrt(jnp.mean(xv * xv, axis=-1, keepdims=True) + EPS)
        h = (((xv * r) * g_ref[...]) * (1.0 + mod_ref[1:2, :]) + mod_ref[0:1, :]).astype(BF16)
        h_ref[...] = h
        p_ref[...] = _dot(h, w_ref[...]).astype(BF16)

    return pl.pallas_call(
        body, name="fwd_in", out_shape=[jax.ShapeDtypeStruct((S, P_W), BF16), jax.ShapeDtypeStruct((S, D), BF16)],
        grid=(S // tm,),
        in_specs=[pl.BlockSpec((tm, D), lambda i: (i, 0)), _full((3, D)), _full((1, D)), _full((D, P_W))],
        out_specs=[pl.BlockSpec((tm, P_W), lambda i: (i, 0)), pl.BlockSpec((tm, D), lambda i: (i, 0))],
        compiler_params=_cp(("parallel",), VMEM_BIG),
    )(x, mod, g_pre, w_p)


def _swap16(v):
    lane = lax.broadcasted_iota(jnp.int32, v.shape, 1)
    return jnp.where((lane % 32) < 16, pltpu.roll(v, 112, 1), pltpu.roll(v, 16, 1))


def _rope(v, cos, sin):
    return v * cos + _swap16(v) * sin


def _rope_t(v, cos, sin):
    return v * cos - _swap16(v) * sin


def _head_mean(v):
    lo = lax.broadcasted_iota(jnp.int32, v.shape, 1) < 64
    m0 = jnp.sum(jnp.where(lo, v, 0.0), axis=-1, keepdims=True)
    m1 = jnp.sum(jnp.where(lo, 0.0, v), axis=-1, keepdims=True)
    return jnp.where(lo, m0, m1) * (1.0 / 64.0)


def _prep(p, cos, sin, qg, kg):
    S = p.shape[0]
    tm = min(512, S)

    def body(qa_ref, kv_ref, qr_ref, kr_ref, cos_ref, sin_ref, qg_ref, kg_ref, qt_ref, kh_ref, kt_ref, vh_ref, vta_ref, qr2_ref, kr2_ref):
        cos_v, sin_v = cos_ref[...], sin_ref[...]
        for g in range(4):
            xv = qa_ref[:, 128 * g:128 * g + 128].astype(F32)
            r = lax.rsqrt(_head_mean(xv * xv) + EPS)
            yt = (_rope((xv * r) * qg_ref[...], cos_v, sin_v) * (0.125 * LOG2E)).T
            qt_ref[2 * g] = yt[:DH].astype(BF16)
            qt_ref[2 * g + 1] = yt[DH:].astype(BF16)
        xv = kv_ref[:, :128].astype(F32)
        r = lax.rsqrt(_head_mean(xv * xv) + EPS)
        yv = _rope((xv * r) * kg_ref[...], cos_v, sin_v)
        kh_ref[0] = yv[:, :64].astype(BF16)
        kh_ref[1] = yv[:, 64:].astype(BF16)
        yt = yv.T
        kt_ref[0] = yt[:DH].astype(BF16)
        kt_ref[1] = yt[DH:].astype(BF16)
        vv = kv_ref[:, 128:].astype(F32)
        vh_ref[0] = vv[:, :64].astype(BF16)
        vh_ref[1] = vv[:, 64:].astype(BF16)
        vt = vv.T
        tail = (lax.broadcasted_iota(jnp.int32, (DHA - DH, tm), 0) == 0).astype(BF16)
        for kvh in range(2):
            vta_ref[kvh, 0:DH, :] = vt[DH * kvh:DH * kvh + DH].astype(BF16)
            vta_ref[kvh, DH:DHA, :] = tail
        for g in range(2):
            sl = slice(128 * g, 128 * g + 128)
            qr2_ref[:, sl] = _rope(qr_ref[:, sl].astype(F32), cos_v, sin_v)
            kr2_ref[:, sl] = _rope(kr_ref[:, sl].astype(F32), cos_v, sin_v) * 0.125

    hm = lambda n: pl.BlockSpec((n, tm, DH), lambda i: (0, i, 0))
    ht = lambda n, r: pl.BlockSpec((n, r, tm), lambda i: (0, 0, i))
    return pl.pallas_call(
        body, name="prep",
        out_shape=[jax.ShapeDtypeStruct((8, DH, S), BF16), jax.ShapeDtypeStruct((2, S, DH), BF16), jax.ShapeDtypeStruct((2, DH, S), BF16),
                   jax.ShapeDtypeStruct((2, S, DH), BF16), jax.ShapeDtypeStruct((2, DHA, S), BF16),
                   jax.ShapeDtypeStruct((S, 256), F32), jax.ShapeDtypeStruct((S, 256), F32)],
        grid=(S // tm,),
        in_specs=[pl.BlockSpec((tm, 512), lambda i: (i, O_QA // 512)), pl.BlockSpec((tm, 256), lambda i: (i, O_KA // 256)),
                  pl.BlockSpec((tm, 256), lambda i: (i, O_QR // 256)), pl.BlockSpec((tm, 256), lambda i: (i, O_KR // 256)),
                  pl.BlockSpec((tm, 128), lambda i: (i, 0)), pl.BlockSpec((tm, 128), lambda i: (i, 0)), _full((1, 128)), _full((1, 128))],
        out_specs=[ht(8, DH), hm(2), ht(2, DH), hm(2), ht(2, DHA), pl.BlockSpec((tm, 256), lambda i: (i, 0)), pl.BlockSpec((tm, 256), lambda i: (i, 0))],
        compiler_params=_cp(("parallel",)),
    )(p, p, p, p, cos, sin, qg, kg)


def _attn_fwd(qt, kh, vta):
    S = qt.shape[2]
    tq, tk = min(1024, S), min(512, S)
    nj = S // tk

    def body(q_ref, k_ref, v_ref, o_ref, ot_ref, lse_ref, m_s, acc_s):
        j = pl.program_id(1)

        @pl.when(j == 0)
        def _():
            m_s[...] = jnp.full_like(m_s, -jnp.inf)
            acc_s[...] = jnp.zeros_like(acc_s)

        m_all = m_s[...]
        st = {0: _dot(k_ref[0], q_ref[0])}
        m_new, acc_new = [], []
        for h in range(8):
            if h + 1 < 8:
                st[h + 1] = _dot(k_ref[(h + 1) // 4], q_ref[h + 1])
            m_old = m_all[h:h + 1, :]
            mn = jnp.maximum(m_old, jnp.max(st[h], axis=0, keepdims=True))
            pt = jnp.exp2(st[h] - mn).astype(BF16)
            acc_new.append(jnp.exp2(m_old - mn) * acc_s[h] + _dot(v_ref[h // 4], pt))
            m_new.append(mn)
            del st[h]
        for h in range(8):
            acc_s[h] = acc_new[h]
            m_s[h:h + 1, :] = m_new[h]

        @pl.when(j == nj - 1)
        def _():
            for h in range(8):
                ot = acc_s[h, 0:DH, :] / acc_s[h, DH:DH + 1, :]
                ot_ref[h] = ot
                o_ref[:, DH * h:DH * h + DH] = ot.T
                lse_ref[h // 4, h % 4:h % 4 + 1, :] = m_s[h:h + 1, :] + jnp.log2(acc_s[h, DH:DH + 1, :])

    return pl.pallas_call(
        body, name="attn_fwd",
        out_shape=[jax.ShapeDtypeStruct((S, 512), F32), jax.ShapeDtypeStruct((8, DH, S), F32), jax.ShapeDtypeStruct((2, 4, S), F32)],
        grid=(S // tq, nj),
        in_specs=[pl.BlockSpec((8, DH, tq), lambda i, j: (0, 0, i)), pl.BlockSpec((2, tk, DH), lambda i, j: (0, j, 0)),
                  pl.BlockSpec((2, DHA, tk), lambda i, j: (0, 0, j))],
        out_specs=[pl.BlockSpec((tq, 512), lambda i, j: (i, 0)), pl.BlockSpec((8, DH, tq), lambda i, j: (0, 0, i)),
                   pl.BlockSpec((2, 4, tq), lambda i, j: (0, 0, i))],
        scratch_shapes=[pltpu.VMEM((8, tq), F32), pltpu.VMEM((8, DHA, tq), F32)],
        compiler_params=_cp(("parallel", "arbitrary"), VMEM_BIG),
    )(qt, kh, vta)


def _ret_tables(wf, wb):
    C = CH

    def body(wf_ref, wb_ref, dc_ref, qdf_ref, qdb_ref, kdf_ref, kdb_ref, a_ref):
        def logsig(w):
            z = jnp.exp(-jnp.abs(w))
            u = 1.0 + z
            l1p = jnp.where(u == 1.0, z, jnp.log(u) * (z / jnp.where(u == 1.0, 1.0, u - 1.0)))
            return jnp.minimum(w, 0.0) - l1p

        lgf, lgb = logsig(wf_ref[...]), logsig(wb_ref[...])
        lane4 = lax.broadcasted_iota(jnp.int32, (1, 4), 1)

        def pick(lg, h):
            return jnp.sum(jnp.where(lane4 == h, lg, 0.0), axis=-1, keepdims=True)

        ii = lax.broadcasted_iota(jnp.int32, (C, C), 0).astype(F32)
        jj = lax.broadcasted_iota(jnp.int32, (C, C), 1).astype(F32)
        dif = ii - jj
        hd = lax.broadcasted_iota(jnp.int32, (C, 256), 1) // DH
        lf_l = jnp.zeros((C, 256), F32)
        lb_l = jnp.zeros((C, 256), F32)
        for h in range(HR):
            lf, lb = pick(lgf, h), pick(lgb, h)
            dc_ref[h] = jnp.where(dif >= 0, jnp.exp(lf * jnp.maximum(dif, 0.0)), jnp.exp(lb * jnp.maximum(-dif, 0.0)))
            lf_l = jnp.where(hd == h, lf, lf_l)
            lb_l = jnp.where(hd == h, lb, lb_l)
            a_ref[h:h + 1, :] = jnp.broadcast_to(jnp.exp(lf * C), (1, 128))
            a_ref[HR + h:HR + h + 1, :] = jnp.broadcast_to(jnp.exp(lb * C), (1, 128))
        ri = lax.broadcasted_iota(jnp.int32, (C, 256), 0).astype(F32)
        qdf_ref[...] = jnp.exp(lf_l * (ri + 1.0))
        qdb_ref[...] = jnp.exp(lb_l * (C - ri))
        kdf_ref[...] = jnp.exp(lf_l * (C - 1.0 - ri))
        kdb_ref[...] = jnp.exp(lb_l * ri)

    t = jax.ShapeDtypeStruct((C, 256), F32)
    return pl.pallas_call(body, name="ret_tables",
                          out_shape=[jax.ShapeDtypeStruct((HR, C, C), F32), t, t, t, t, jax.ShapeDtypeStruct((8, 128), F32)])(wf, wb)


def _ret_states(kr2, p, kdf, kdb, adec):
    S = kr2.shape[0]
    C, N = CH, S // CH
    G = _scan_group(N)
    NG = N // G

    def body(kf_ref, vf_ref, kb_ref, vb_ref, kdf_ref, kdb_ref, a_ref, rf_ref, rb_ref, sf, sb):
        @pl.when(pl.program_id(0) == 0)
        def _():
            sf[...] = jnp.zeros_like(sf)
            sb[...] = jnp.zeros_like(sb)

        kvf, kvb = [], []
        for u in range(G):
            rows = slice(C * u, C * u + C)
            kdfw = (kf_ref[rows, :] * kdf_ref[...]).astype(BF16)
            kdbw = (kb_ref[rows, :] * kdb_ref[...]).astype(BF16)
            vf, vb = vf_ref[rows, :].astype(BF16), vb_ref[rows, :].astype(BF16)
            kvf.append([_dot(kdfw[:, _ks(h)], vf[:, _vs(h)], TN) for h in range(HR)])
            kvb.append([_dot(kdbw[:, _ks(h)], vb[:, _vs(h)], TN) for h in range(HR)])
        for u in range(G):
            rf_ref[u] = sf[...]
            for h in range(HR):
                sf[h] = a_ref[h:h + 1, :] * sf[h] + kvf[u][h]
        for u in reversed(range(G)):
            rb_ref[u] = sb[...]
            for h in range(HR):
                sb[h] = a_ref[HR + h:HR + h + 1, :] * sb[h] + kvb[u][h]

    st = jax.ShapeDtypeStruct((N, HR, DH, DV), F32)
    return pl.pallas_call(
        body, name="ret_states", out_shape=[st, st], grid=(NG,),
        in_specs=[pl.BlockSpec((G * C, 256), lambda t: (t, 0)), pl.BlockSpec((G * C, 512), lambda t: (t, O_VR // 512)),
                  pl.BlockSpec((G * C, 256), lambda t: (NG - 1 - t, 0)), pl.BlockSpec((G * C, 512), lambda t: (NG - 1 - t, O_VR // 512)),
                  _full((C, 256)), _full((C, 256)), _full((8, 128))],
        out_specs=[pl.BlockSpec((G, HR, DH, DV), lambda t: (t, 0, 0, 0)), pl.BlockSpec((G, HR, DH, DV), lambda t: (NG - 1 - t, 0, 0, 0))],
        scratch_shapes=[pltpu.VMEM((HR, DH, DV), F32), pltpu.VMEM((HR, DH, DV), F32)],
        compiler_params=_cp(("arbitrary",)),
    )(kr2, p, kr2, p, kdf, kdb, adec)


def _scan_group(n):
    return 4 if n % 4 == 0 else (2 if n % 2 == 0 else 1)


def _ks(h):
    return slice(DH * h, DH * h + DH)


def _vs(h):
    return slice(DV * h, DV * h + DV)


def _ret_heads_fwd(qb, kb, vb, qfw, qbw, dc_ref, rf_ref, rb_ref, u=0):
    hs = range(HR)
    s = [_dot(qb[:, _ks(h)], kb[:, _ks(h)], NT) for h in hs]
    inter = [_dot(qfw[:, _ks(h)], rf_ref[u, h].astype(BF16)) + _dot(qbw[:, _ks(h)], rb_ref[u, h].astype(BF16)) for h in hs]
    sd = [s[h] * dc_ref[h] for h in hs]
    o = [_dot(sd[h].astype(BF16), vb[:, _vs(h)]) + inter[h] for h in hs]
    return sd, o


def _ret_out(qr2, kr2, p, rf, rb, dc, qdf, qdb, gn):
    S = qr2.shape[0]
    C, N = CH, S // CH
    G = _scan_group(N)

    def body(q_ref, k_ref, v_ref, z_ref, rf_ref, rb_ref, dc_ref, qdf_ref, qdb_ref, gn_ref, yr_ref):
        outs = []
        for u in range(G):
            rows = slice(C * u, C * u + C)
            qv = q_ref[rows, :]
            qb, kb, vb = qv.astype(BF16), k_ref[rows, :].astype(BF16), v_ref[rows, :].astype(BF16)
            qfw, qbw = (qv * qdf_ref[...]).astype(BF16), (qv * qdb_ref[...]).astype(BF16)
            outs.append(_ret_heads_fwd(qb, kb, vb, qfw, qbw, dc_ref, rf_ref, rb_ref, u)[1])
        for u in range(G):
            rows = slice(C * u, C * u + C)
            for h in range(HR):
                vs = _vs(h)
                o = outs[u][h]
                mu = jnp.mean(o, axis=-1, keepdims=True)
                var = jnp.mean(jnp.square(o - mu), axis=-1, keepdims=True)
                on = (o - mu) * lax.rsqrt(var + EPS)
                z = z_ref[rows, vs].astype(F32)
                yr_ref[rows, vs] = ((on * gn_ref[:, vs]) * (z * _sigmoid(z))).astype(BF16)

    row = lambda w, off=0: pl.BlockSpec((G * C, w), lambda t: (t, off))
    stb = lambda: pl.BlockSpec((G, HR, DH, DV), lambda t: (t, 0, 0, 0))
    return pl.pallas_call(
        body, name="ret_out", out_shape=jax.ShapeDtypeStruct((S, 512), BF16), grid=(N // G,),
        in_specs=[row(256), row(256), row(512, O_VR // 512), row(512, O_ZR // 512), stb(), stb(),
                  _full((HR, C, C)), _full((C, 256)), _full((C, 256)), _full((1, 512))],
        out_specs=row(512),
        compiler_params=_cp(("parallel",)),
    )(qr2, kr2, p, p, rf, rb, dc, qdf, qdb, gn)


def _mid(x, tgt, mod, g_post, o_att, p, yr, w_pa, w_pr, w_out):
    S = x.shape[0]
    tm = min(256, S)

    def body(x_ref, t_ref, mod_ref, gp_ref, o_ref, za_ref, gl_ref, yr_ref, wpa_ref, wpr_ref, wout_ref,
             dout_ref, do_ref, dpm_ref, dyr_ref, mb_ref, dub_ref, yab_ref, dab_ref, drb_ref, sums_ref):
        @pl.when(pl.program_id(0) == 0)
        def _():
            sums_ref[...] = jnp.zeros_like(sums_ref)

        za = za_ref[...].astype(F32)
        sa = _sigmoid(za)
        sil = za * sa
        ov = o_ref[...]
        ya_b = (ov * sil).astype(BF16)
        yr_b = yr_ref[...]
        av = _dot(ya_b, wpa_ref[...])
        rv = _dot(yr_b, wpr_ref[...])
        ga = _sigmoid(gl_ref[:, :D].astype(F32))
        gr = _sigmoid(gl_ref[:, D:].astype(F32))
        mb = (ga * av + gr * rv).astype(BF16)
        u = _dot(mb, wout_ref[...])
        r2 = lax.rsqrt(jnp.mean(u * u, axis=-1, keepdims=True) + EPS)
        un = u * r2
        gp = gp_ref[...]
        yv = un * gp
        gate = mod_ref[2:3, :]
        err = (x_ref[...] + gate * yv) - t_ref[...]
        dout = err * (1.0 / D)
        dout_ref[...] = dout
        dy = dout * gate
        sums_ref[0:1, :] += jnp.sum(dout * yv, axis=0, keepdims=True)
        sums_ref[1:2, :] += jnp.sum(dy * un, axis=0, keepdims=True)
        sums_ref[2:3, :] += jnp.sum(err * err, axis=0, keepdims=True)
        dyg = dy * gp
        du_b = (r2 * (dyg - un * jnp.mean(dyg * un, axis=-1, keepdims=True))).astype(BF16)
        dm = _dot(du_b, wout_ref[...], NT)
        da_b = (dm * ga).astype(BF16)
        dr_b = (dm * gr).astype(BF16)
        dpm_ref[:, :D] = (dm * av * (ga * (1.0 - ga))).astype(BF16)
        dpm_ref[:, D:2 * D] = (dm * rv * (gr * (1.0 - gr))).astype(BF16)
        dya = _dot(da_b, wpa_ref[...], NT)
        dyr_ref[...] = _dot(dr_b, wpr_ref[...], NT)
        dov = dya * sil
        for g in range(4):
            dt = dov[:, 128 * g:128 * g + 128].T
            do_ref[2 * g] = dt[:DH].astype(BF16)
            do_ref[2 * g + 1] = dt[DH:].astype(BF16)
        dpm_ref[:, 2 * D:] = (dya * ov * (sa * (1.0 + za * (1.0 - sa)))).astype(BF16)
        mb_ref[...] = mb
        dub_ref[...] = du_b
        yab_ref[...] = ya_b
        dab_ref[...] = da_b
        drb_ref[...] = dr_b

    row = lambda w: pl.BlockSpec((tm, w), lambda i: (i, 0))
    sd = lambda w, dt: jax.ShapeDtypeStruct((S, w), dt)
    return pl.pallas_call(
        body, name="mid",
        out_shape=[sd(D, F32), jax.ShapeDtypeStruct((8, DH, S), BF16), sd(2560, BF16), sd(512, F32), sd(D, BF16), sd(D, BF16), sd(512, BF16),
                   sd(D, BF16), sd(D, BF16), jax.ShapeDtypeStruct((8, D), F32)],
        grid=(S // tm,),
        in_specs=[row(D), row(D), _full((3, D)), _full((1, D)), row(512), pl.BlockSpec((tm, 512), lambda i: (i, O_ZA // 512)),
                  pl.BlockSpec((tm, 2048), lambda i: (i, 0)), row(512), _full((512, D)), _full((512, D)), _full((D, D))],
        out_specs=[row(D), pl.BlockSpec((8, DH, tm), lambda i: (0, 0, i)), row(2560), row(512), row(D), row(D), row(512), row(D), row(D),
                   _full((8, D))],
        compiler_params=_cp(("arbitrary",), VMEM_BIG),
    )(x, tgt, mod, g_post, o_att, p, p, yr, w_pa, w_pr, w_out)


def _attn_bwd(qt, kh, kt, vh, dot_, ot, lse):
    S = qt.shape[2]
    tq, tk = min(1024, S), min(1024, S)

    def body(q_ref, k_ref, kt_ref, v_ref, do_ref, o_ref, lse_ref, dq_ref, dk_ref, dv_ref):
        j, i = pl.program_id(0), pl.program_id(1)
        cols = pl.ds(pl.multiple_of(i * tq, tq), tq)
        st = {0: _dot(k_ref[0], q_ref[0])}
        dpt = {0: _dot(v_ref[0], do_ref[0])}
        dk_acc, dv_acc, dqs = [None, None], [None, None], []
        for h in range(8):
            g = h // 4
            if h + 1 < 8:
                st[h + 1] = _dot(k_ref[(h + 1) // 4], q_ref[h + 1])
                dpt[h + 1] = _dot(v_ref[(h + 1) // 4], do_ref[h + 1])
            qt_h, dot_h = q_ref[h], do_ref[h]
            delta = jnp.sum(dot_h.astype(F32) * o_ref[h], axis=0, keepdims=True)
            pt = jnp.exp2(st[h] - lse_ref[g, h % 4:h % 4 + 1, :])
            dst = (pt * (dpt[h] - delta)).astype(BF16)
            dv_h = _dot(dot_h, pt.astype(BF16), NT)
            dk_h = _dot(qt_h, dst, NT)
            dqs.append(_dot(kt_ref[g], dst))
            dv_acc[g] = dv_h if dv_acc[g] is None else dv_acc[g] + dv_h
            dk_acc[g] = dk_h if dk_acc[g] is None else dk_acc[g] + dk_h
            del st[h], dpt[h]

        @pl.when(i == 0)
        def _():
            for g in range(2):
                dk_ref[g] = dk_acc[g]
                dv_ref[g] = dv_acc[g]

        @pl.when(i > 0)
        def _():
            for g in range(2):
                dk_ref[g] += dk_acc[g]
                dv_ref[g] += dv_acc[g]

        @pl.when(j == 0)
        def _():
            for h in range(8):
                dq_ref[h, :, cols] = dqs[h]

        @pl.when(j > 0)
        def _():
            for h in range(8):
                dq_ref[h, :, cols] += dqs[h]

    return pl.pallas_call(
        body, name="attn_bwd",
        out_shape=[jax.ShapeDtypeStruct((8, DH, S), F32), jax.ShapeDtypeStruct((2, DH, S), F32), jax.ShapeDtypeStruct((2, DH, S), F32)],
        grid=(S // tk, S // tq),
        in_specs=[pl.BlockSpec((8, DH, tq), lambda j, i: (0, 0, i)), pl.BlockSpec((2, tk, DH), lambda j, i: (0, j, 0)),
                  pl.BlockSpec((2, DH, tk), lambda j, i: (0, 0, j)), pl.BlockSpec((2, tk, DH), lambda j, i: (0, j, 0)),
                  pl.BlockSpec((8, DH, tq), lambda j, i: (0, 0, i)), pl.BlockSpec((8, DH, tq), lambda j, i: (0, 0, i)),
                  pl.BlockSpec((2, 4, tq), lambda j, i: (0, 0, i))],
        out_specs=[pl.BlockSpec((8, DH, S), lambda j, i: (0, 0, 0)), pl.BlockSpec((2, DH, tk), lambda j, i: (0, 0, j)),
                   pl.BlockSpec((2, DH, tk), lambda j, i: (0, 0, j))],
        compiler_params=_cp(("arbitrary", "arbitrary"), VMEM_BIG),
    )(qt, kh, kt, vh, dot_, ot, lse)


def _attn_prep_bwd(dqt, dkt, dvt, p, cos, sin, qg, kg):
    S = dqt.shape[2]
    tm = min(512, S)

    def body(dq_ref, dk_ref, dv_ref, qa_ref, ka_ref, cos_ref, sin_ref, qg_ref, kg_ref, dp_ref, gs_ref):
        @pl.when(pl.program_id(0) == 0)
        def _():
            gs_ref[...] = jnp.zeros_like(gs_ref)

        cos_v, sin_v = cos_ref[...], sin_ref[...]

        def pair(ref, a):
            return jnp.concatenate([ref[a], ref[a + 1]], axis=0).T

        def norm_bwd(dyv, xv, gv, row):
            r = lax.rsqrt(_head_mean(xv * xv) + EPS)
            xn = xv * r
            dxh = _rope_t(dyv, cos_v, sin_v)
            gs_ref[row:row + 1, :] += jnp.sum(dxh * xn, axis=0, keepdims=True)
            dg = dxh * gv
            return r * (dg - xn * _head_mean(dg * xn))

        for g in range(4):
            sl = slice(128 * g, 128 * g + 128)
            dp_ref[:, sl] = norm_bwd(pair(dq_ref, 2 * g) * 0.125, qa_ref[:, sl].astype(F32), qg_ref[...], 0).astype(BF16)
        dp_ref[:, 512:640] = norm_bwd(pair(dk_ref, 0) * LN2, ka_ref[...].astype(F32), kg_ref[...], 1).astype(BF16)
        dp_ref[:, 640:768] = pair(dv_ref, 0).astype(BF16)

    ht = lambda n: pl.BlockSpec((n, DH, tm), lambda i: (0, 0, i))
    return pl.pallas_call(
        body, name="attn_prep_bwd", out_shape=[jax.ShapeDtypeStruct((S, 768), BF16), jax.ShapeDtypeStruct((8, 128), F32)],
        grid=(S // tm,),
        in_specs=[ht(8), ht(2), ht(2),
                  pl.BlockSpec((tm, 512), lambda i: (i, O_QA // 512)), pl.BlockSpec((tm, 128), lambda i: (i, O_KA // 128)),
                  pl.BlockSpec((tm, 128), lambda i: (i, 0)), pl.BlockSpec((tm, 128), lambda i: (i, 0)), _full((1, 128)), _full((1, 128))],
        out_specs=[pl.BlockSpec((tm, 768), lambda i: (i, 0)), _full((8, 128))],
        compiler_params=_cp(("arbitrary",)),
    )(dqt, dkt, dvt, p, p, cos, sin, qg, kg)


def _ret_bwd_chunk(qr2, kr2, p, rf, rb, dc, qdf, qdb, gn, dyr, cos, sin):
    S = qr2.shape[0]
    C, N = CH, S // CH
    G = 2 if N % 2 == 0 else 1

    def body(q_ref, k_ref, v_ref, z_ref, rf_ref, rb_ref, dc_ref, qdf_ref, qdb_ref, gn_ref, dyr_ref, cos_ref, sin_ref,
             dpa_ref, dk_ref, dv_ref, drf_ref, drb_ref, dgn_ref, dlg_ref, dqs):
        @pl.when(pl.program_id(0) == 0)
        def _():
            dgn_ref[...] = jnp.zeros_like(dgn_ref)
            dlg_ref[...] = jnp.zeros_like(dlg_ref)

        ii = lax.broadcasted_iota(jnp.int32, (C, C), 0).astype(F32)
        jj = lax.broadcasted_iota(jnp.int32, (C, C), 1).astype(F32)
        dif = ii - jj
        ri = lax.broadcasted_iota(jnp.int32, (C, 1), 0).astype(F32)
        hs, us = range(HR), range(G)
        rows = [slice(C * u, C * u + C) for u in us]
        qv = [q_ref[rows[u], :] for u in us]
        qb = [qv[u].astype(BF16) for u in us]
        kb = [k_ref[rows[u], :].astype(BF16) for u in us]
        vb = [v_ref[rows[u], :].astype(BF16) for u in us]
        qf32 = [qv[u] * qdf_ref[...] for u in us]
        qb32 = [qv[u] * qdb_ref[...] for u in us]
        qfw = [qf32[u].astype(BF16) for u in us]
        qbw = [qb32[u].astype(BF16) for u in us]
        fwd = [_ret_heads_fwd(qb[u], kb[u], vb[u], qfw[u], qbw[u], dc_ref, rf_ref, rb_ref, u) for u in us]
        sd = [f[0] for f in fwd]
        do_b = [[] for _ in us]
        for u in us:
            for h in hs:
                vs = _vs(h)
                o = fwd[u][1][h]
                mu = jnp.mean(o, axis=-1, keepdims=True)
                rstd = lax.rsqrt(jnp.mean(jnp.square(o - mu), axis=-1, keepdims=True) + EPS)
                on = (o - mu) * rstd
                z = z_ref[rows[u], vs].astype(F32)
                sz = _sigmoid(z)
                dy = dyr_ref[rows[u], vs]
                gnv = gn_ref[:, vs]
                dpa_ref[rows[u], 256 + DV * h:256 + DV * h + DV] = (dy * (on * gnv) * (sz * (1.0 + z * (1.0 - sz)))).astype(BF16)
                dys = dy * (z * sz)
                dgn_ref[:, vs] += jnp.sum(dys * on, axis=0, keepdims=True)
                don = dys * gnv
                do = rstd * (don - jnp.mean(don, axis=-1, keepdims=True) - on * jnp.mean(don * on, axis=-1, keepdims=True))
                do_b[u].append(do.astype(BF16))
        dpm = [[_dot(do_b[u][h], vb[u][:, _vs(h)], NT) for h in hs] for u in us]
        dqf = [[_dot(do_b[u][h], rf_ref[u, h].astype(BF16), NT) for h in hs] for u in us]
        dqb = [[_dot(do_b[u][h], rb_ref[u, h].astype(BF16), NT) for h in hs] for u in us]
        for u in us:
            for h in hs:
                dv_ref[rows[u], _vs(h)] = _dot(sd[u][h].astype(BF16), do_b[u][h], TN)
                drf_ref[u, h] = _dot(qfw[u][:, _ks(h)], do_b[u][h], TN)
                drb_ref[u, h] = _dot(qbw[u][:, _ks(h)], do_b[u][h], TN)
        dsd = [[(dpm[u][h] * dc_ref[h]).astype(BF16) for h in hs] for u in us]
        for u in us:
            for h in hs:
                ks = _ks(h)
                dqs[rows[u], ks] = _dot(dsd[u][h], kb[u][:, ks]) + dqf[u][h] * qdf_ref[:, ks] + dqb[u][h] * qdb_ref[:, ks]
                dk_ref[rows[u], ks] = _dot(dsd[u][h], qb[u][:, ks], TN)
        for u in us:
            for h in hs:
                ks = _ks(h)
                e = dpm[u][h] * sd[u][h]
                lf = (_sum11(e * jnp.maximum(dif, 0.0))
                      + _sum11(jnp.sum(qf32[u][:, ks] * dqf[u][h], axis=-1, keepdims=True) * (ri + 1.0)))
                lb = (_sum11(e * jnp.maximum(-dif, 0.0))
                      + _sum11(jnp.sum(qb32[u][:, ks] * dqb[u][h], axis=-1, keepdims=True) * (C - ri)))
                dlg_ref[h:h + 1, :] += jnp.broadcast_to(lf, (1, 128))
                dlg_ref[HR + h:HR + h + 1, :] += jnp.broadcast_to(lb, (1, 128))
            for g in range(2):
                sl = slice(128 * g, 128 * g + 128)
                dpa_ref[rows[u], sl] = _rope_t(dqs[rows[u], sl], cos_ref[rows[u], :], sin_ref[rows[u], :]).astype(BF16)

    st = jax.ShapeDtypeStruct((N, HR, DH, DV), F32)
    stb = lambda: pl.BlockSpec((G, HR, DH, DV), lambda t: (t, 0, 0, 0))
    row = lambda w, off=0: pl.BlockSpec((G * C, w), lambda t: (t, off))
    return pl.pallas_call(
        body, name="ret_bwd_chunk",
        out_shape=[jax.ShapeDtypeStruct((S, 768), BF16), jax.ShapeDtypeStruct((S, 256), F32), jax.ShapeDtypeStruct((S, 512), F32), st, st,
                   jax.ShapeDtypeStruct((1, 512), F32), jax.ShapeDtypeStruct((8, 128), F32)],
        grid=(N // G,),
        in_specs=[row(256), row(256), row(512, O_VR // 512), row(512, O_ZR // 512),
                  stb(), stb(), _full((HR, C, C)), _full((C, 256)), _full((C, 256)), _full((1, 512)), row(512), row(128), row(128)],
        out_specs=[row(768), row(256), row(512), stb(), stb(), _full((1, 512)), _full((8, 128))],
        scratch_shapes=[pltpu.VMEM((G * C, 256), F32)],
        compiler_params=_cp(("arbitrary",)),
    )(qr2, kr2, p, p, rf, rb, dc, qdf, qdb, gn, dyr, cos, sin)


def _ret_bwd_scan(kr2, p, rf, rb, drf, drb, kdf, kdb, adec):
    S = kr2.shape[0]
    C, N = CH, S // CH
    G = _scan_group(N)
    NG = N // G

    def body(kf_ref, vf_ref, kb_ref, vb_ref, rf_ref, rb_ref, drf_ref, drb_ref, kdf_ref, kdb_ref, a_ref,
             dkf_ref, dkb_ref, dvf_ref, dvb_ref, dlg_ref, gf, gb):
        @pl.when(pl.program_id(0) == 0)
        def _():
            gf[...] = jnp.zeros_like(gf)
            gb[...] = jnp.zeros_like(gb)
            dlg_ref[...] = jnp.zeros_like(dlg_ref)

        ri = lax.broadcasted_iota(jnp.int32, (C, 1), 0).astype(F32)

        def one(k_ref, v_ref, r_ref, dr_ref, kd_ref, g_s, dk_ref, dv_ref, row0, wexp, order):
            g = [g_s[h] for h in range(HR)]
            lgs = [jnp.zeros((1, 1), F32) for _ in range(HR)]
            for u in order:
                rows = slice(C * u, C * u + C)
                kd32 = k_ref[rows, :] * kd_ref[...]
                kdw = kd32.astype(BF16)
                vb = v_ref[rows, :].astype(BF16)
                for h in range(HR):
                    ks, vs = _ks(h), _vs(h)
                    g_b = g[h].astype(BF16)
                    dkd = _dot(vb[:, vs], g_b, NT)
                    dk_ref[rows, ks] = dkd * kd_ref[:, ks]
                    dv_ref[rows, vs] = _dot(kdw[:, ks], g_b)
                    av = a_ref[row0 + h:row0 + h + 1, :]
                    lgs[h] = lgs[h] + (_sum11(jnp.sum(kd32[:, ks] * dkd, axis=-1, keepdims=True) * wexp)
                                       + C * av[:, 0:1] * _sum11(r_ref[u, h] * g[h]))
                    g[h] = dr_ref[u, h] + av * g[h]
            for h in range(HR):
                g_s[h] = g[h]
                dlg_ref[row0 + h:row0 + h + 1, :] += jnp.broadcast_to(lgs[h], (1, 128))

        one(kf_ref, vf_ref, rf_ref, drf_ref, kdf_ref, gf, dkf_ref, dvf_ref, 0, C - 1.0 - ri, list(reversed(range(G))))
        one(kb_ref, vb_ref, rb_ref, drb_ref, kdb_ref, gb, dkb_ref, dvb_ref, HR, ri, list(range(G)))

    fwd = lambda w, off=0: pl.BlockSpec((G * C, w), lambda t: (NG - 1 - t, off))
    bwd = lambda w, off=0: pl.BlockSpec((G * C, w), lambda t: (t, off))
    stf = lambda: pl.BlockSpec((G, HR, DH, DV), lambda t: (NG - 1 - t, 0, 0, 0))
    stb = lambda: pl.BlockSpec((G, HR, DH, DV), lambda t: (t, 0, 0, 0))
    return pl.pallas_call(
        body, name="ret_bwd_scan",
        out_shape=[jax.ShapeDtypeStruct((S, 256), F32), jax.ShapeDtypeStruct((S, 256), F32), jax.ShapeDtypeStruct((S, 512), F32),
                   jax.ShapeDtypeStruct((S, 512), F32), jax.ShapeDtypeStruct((8, 128), F32)],
        grid=(NG,),
        in_specs=[fwd(256), fwd(512, O_VR // 512), bwd(256), bwd(512, O_VR // 512), stf(), stb(), stf(), stb(),
                  _full((C, 256)), _full((C, 256)), _full((8, 128))],
        out_specs=[fwd(256), bwd(256), fwd(512), bwd(512), _full((8, 128))],
        scratch_shapes=[pltpu.VMEM((HR, DH, DV), F32), pltpu.VMEM((HR, DH, DV), F32)],
        compiler_params=_cp(("arbitrary",)),
    )(kr2, p, kr2, p, rf, rb, drf, drb, kdf, kdb, adec)


def _ret_bwd_final(dk_i, dkf, dkb, dv_i, dvf, dvb, cos, sin):
    S = dk_i.shape[0]
    tm = min(512, S)

    def body(a_ref, b_ref, c_ref, d_ref, e_ref, f_ref, cos_ref, sin_ref, o_ref):
        o_ref[:, :512] = (d_ref[...] + e_ref[...] + f_ref[...]).astype(BF16)
        cos_v, sin_v = cos_ref[...], sin_ref[...]
        for g in range(2):
            sl = slice(128 * g, 128 * g + 128)
            dk = a_ref[:, sl] + b_ref[:, sl] + c_ref[:, sl]
            o_ref[:, 512 + 128 * g:512 + 128 * g + 128] = (_rope_t(dk, cos_v, sin_v) * 0.125).astype(BF16)

    row = lambda w: pl.BlockSpec((tm, w), lambda i: (i, 0))
    return pl.pallas_call(
        body, name="ret_bwd_final", out_shape=jax.ShapeDtypeStruct((S, 768), BF16), grid=(S // tm,),
        in_specs=[row(256), row(256), row(256), row(512), row(512), row(512), row(128), row(128)], out_specs=row(768),
        compiler_params=_cp(("parallel",)),
    )(dk_i, dkf, dkb, dv_i, dvf, dvb, cos, sin)


def _bwd_in(dpm, dpa, dpra, dprb, w_p, x, dout, mod, g_pre):
    S = x.shape[0]
    tm = min(256, S)

    def body(a_ref, b_ref, c_ref, d_ref, w_ref, x_ref, dout_ref, mod_ref, g_ref, gx_ref, sums_ref):
        @pl.when(pl.program_id(0) == 0)
        def _():
            sums_ref[...] = jnp.zeros_like(sums_ref)

        dh = (_dot(a_ref[...], w_ref[:, :O_QA], NT) + _dot(b_ref[...], w_ref[:, O_QA:O_QR], NT)
              + _dot(c_ref[...], w_ref[:, O_QR:O_VR], NT) + _dot(d_ref[...], w_ref[:, O_VR:], NT))
        xv = x_ref[...]
        r = lax.rsqrt(jnp.mean(xv * xv, axis=-1, keepdims=True) + EPS)
        xn = xv * r
        gv = g_ref[...]
        sc1 = 1.0 + mod_ref[1:2, :]
        sums_ref[0:1, :] += jnp.sum(dh, axis=0, keepdims=True)
        sums_ref[1:2, :] += jnp.sum(dh * (xn * gv), axis=0, keepdims=True)
        sums_ref[2:3, :] += jnp.sum(dh * xn, axis=0, keepdims=True) * sc1
        dxn = dh * (gv * sc1)
        gx_ref[...] = dout_ref[...] + r * (dxn - xn * jnp.mean(dxn * xn, axis=-1, keepdims=True))

    row = lambda w: pl.BlockSpec((tm, w), lambda i: (i, 0))
    return pl.pallas_call(
        body, name="bwd_in", out_shape=[jax.ShapeDtypeStruct((S, D), F32), jax.ShapeDtypeStruct((8, D), F32)], grid=(S // tm,),
        in_specs=[row(2560), row(768), row(768), row(768), _full((D, P_W)), row(D), row(D), _full((3, D)), _full((1, D))],
        out_specs=[row(D), _full((8, D))],
        compiler_params=_cp(("arbitrary",), VMEM_BIG),
    )(dpm, dpa, dpra, dprb, w_p, x, dout, mod, g_pre)


SMALL = ("b_ada", "g_pre", "qn_g", "kn_g", "w_dec_f", "w_dec_b", "gn_g", "g_post")


def _small_update(gathered, wmv):
    ns = len(SMALL)

    def body(*refs):
        gin_ref, gmid_ref, ggn_ref, gatt_ref, gl1_ref, gl2_ref = refs[:6]
        wmv_refs = refs[6:6 + 3 * ns]
        loss_ref = refs[6 + 3 * ns]
        out_refs = refs[7 + 3 * ns:]

        def dsum(ref, r=None):
            rows = slice(None) if r is None else slice(r, r + 1)
            acc = ref[0, rows, :]
            for d in range(1, NDEV):
                acc = acc + ref[d, rows, :]
            return acc

        s_lg = dsum(gl1_ref) + dsum(gl2_ref)
        loss_ref[...] = (0.5 / D) * jnp.sum(dsum(gmid_ref, 2), axis=-1, keepdims=True)
        eye = lax.broadcasted_iota(jnp.int32, (8, 128), 0) == lax.broadcasted_iota(jnp.int32, (8, 128), 1)
        dlg = jnp.sum(jnp.where(eye, s_lg, 0.0), axis=0, keepdims=True)
        w_f, w_b = wmv_refs[3 * SMALL.index("w_dec_f")][...], wmv_refs[3 * SMALL.index("w_dec_b")][...]
        s_q, s_k = dsum(gatt_ref, 0), dsum(gatt_ref, 1)
        grads = dict(
            b_ada=jnp.concatenate([dsum(gin_ref, 0), dsum(gin_ref, 1), dsum(gmid_ref, 0)], axis=1),
            g_pre=dsum(gin_ref, 2), g_post=dsum(gmid_ref, 1), gn_g=dsum(ggn_ref),
            qn_g=s_q[:, :DH] + s_q[:, DH:], kn_g=s_k[:, :DH] + s_k[:, DH:],
            w_dec_f=dlg[:, 0:HR] * _sigmoid(-w_f), w_dec_b=dlg[:, HR:2 * HR] * _sigmoid(-w_b))
        for i, nme in enumerate(SMALL):
            g = grads[nme]
            w_ref, m_ref, v_ref = wmv_refs[3 * i:3 * i + 3]
            g_ref, d_ref, nm_ref, nv_ref = out_refs[4 * i:4 * i + 4]
            g_ref[...] = g
            m2 = ADAM_B1 * m_ref[...] + (1.0 - ADAM_B1) * g
            v2 = ADAM_B2 * v_ref[...] + (1.0 - ADAM_B2) * jnp.square(g)
            m_hat = m2 / (1.0 - ADAM_B1 ** ADAM_STEP)
            v_hat = v2 / (1.0 - ADAM_B2 ** ADAM_STEP)
            d_ref[...] = -ADAM_LR * (m_hat / (jnp.sqrt(v_hat) + ADAM_EPS) + ADAM_WD * w_ref[...])
            nm_ref[...] = m2
            nv_ref[...] = v2

    out_shape = [jax.ShapeDtypeStruct((1, 1), F32)]
    for i in range(ns):
        out_shape += [jax.ShapeDtypeStruct(wmv[3 * i].shape, F32)] * 4
    return pl.pallas_call(body, name="small_update", out_shape=out_shape)(*gathered, *wmv)


def _adamw(parts, w, m, v, name):
    n, R, L = parts.shape
    tr = 256 if (R % 256 == 0 and R > 256) else R

    def body(p_ref, w_ref, m_ref, v_ref, g_ref, d_ref, nm_ref, nv_ref):
        g = p_ref[0].astype(F32)
        for k in range(1, n):
            g = g + p_ref[k].astype(F32)
        g_ref[...] = g
        m2 = ADAM_B1 * m_ref[...] + (1.0 - ADAM_B1) * g
        v2 = ADAM_B2 * v_ref[...] + (1.0 - ADAM_B2) * jnp.square(g)
        m_hat = m2 / (1.0 - ADAM_B1 ** ADAM_STEP)
        v_hat = v2 / (1.0 - ADAM_B2 ** ADAM_STEP)
        d_ref[...] = -ADAM_LR * (m_hat / (jnp.sqrt(v_hat) + ADAM_EPS) + ADAM_WD * w_ref[...])
        nm_ref[...] = m2
        nv_ref[...] = v2

    blk = pl.BlockSpec((tr, L), lambda i: (i, 0))
    o = jax.ShapeDtypeStruct((R, L), F32)
    return pl.pallas_call(
        body, name=name, out_shape=[o, o, o, o], grid=(R // tr,),
        in_specs=[pl.BlockSpec((n, tr, L), lambda i: (0, i, 0)), blk, blk, blk], out_specs=[blk, blk, blk, blk],
        compiler_params=_cp(("parallel",), VMEM_BIG),
    )(parts, w, m, v)


def _rope_tables(S):
    f = np.float32
    t = np.arange(S)
    row, col = (t // 64).astype(f), (t % 64).astype(f)
    half = DH // 2
    inv = np.power(f(ROPE_THETA), -np.arange(0, half, 2, dtype=f) / f(half)).astype(f)
    ar, ac = (row[:, None] * inv[None, :]).astype(f), (col[:, None] * inv[None, :]).astype(f)
    cos64 = np.concatenate([np.cos(ar), np.cos(ar), np.cos(ac), np.cos(ac)], axis=1).astype(f)
    sin64 = np.concatenate([-np.sin(ar), np.sin(ar), -np.sin(ac), np.sin(ac)], axis=1).astype(f)
    return jnp.asarray(np.tile(cos64, (1, 2))), jnp.asarray(np.tile(sin64, (1, 2)))


def _to_p_order(w_orig):
    return jnp.concatenate([w_orig[:, ORIG[n][0]:ORIG[n][1]] for n in P_ORDER], axis=1)


def _pad_lanes(v, n):
    return jnp.pad(v, ((0, 0), (0, n - v.shape[1])))


def kernel(x, c, w_ada, b_ada, g_pre, w_in, qn_g, kn_g, w_dec_f, w_dec_b, gn_g, w_pa, w_pr, w_out, g_post, loss_target, m_w_ada, m_b_ada, m_g_pre, m_w_in, m_qn_g, m_kn_g, m_w_dec_f, m_w_dec_b, m_gn_g, m_w_pa, m_w_pr, m_w_out, m_g_post, v_w_ada, v_b_ada, v_g_pre, v_w_in, v_qn_g, v_kn_g, v_w_dec_f, v_w_dec_b, v_gn_g, v_w_pa, v_w_pr, v_w_out, v_g_post):
    S = x.shape[1]
    me = 4 * lax.axis_index("x") + 2 * lax.axis_index("y") + lax.axis_index("c")
    xs, tgt = x[0], loss_target[0]
    ncol_ada = w_ada.shape[2]
    ncol_in = w_in.shape[2]

    b_ada_s = lax.dynamic_slice(b_ada, (0, me * ncol_ada), (1, ncol_ada))
    mod_all, c_act, (wg_in,) = _prologue(jnp.pad(c, ((0, 7), (0, 0))), w_ada[0], b_ada_s, [w_in[0].astype(BF16)])
    mod = lax.dynamic_index_in_dim(mod_all, me, axis=1, keepdims=False).reshape(3, D)
    w_p = _to_p_order(wg_in.transpose(1, 0, 2).reshape(D, NDEV * ncol_in))
    all_dev = tuple(range(NDEV))
    st_w, tok_w = _xchg_start([(w_pa[0].astype(BF16)[None], all_dev), (w_pr[0].astype(BF16)[None], all_dev),
                               (w_out[0].astype(BF16)[None], all_dev)], "wgather_start")

    cos, sin = _rope_tables(S)
    qg, kg = jnp.tile(qn_g, (1, 2)), jnp.tile(kn_g, (1, 2))

    p, h = _fwd_in(xs, mod, g_pre + tok_w[0:1, 0:1], w_p)
    qt, kh, kt, vh, vta, qr2, kr2 = _prep(p, cos, sin, qg, kg)
    o_att, o_t, lse = _attn_fwd(qt, kh, vta)
    dc, qdf, qdb, kdf, kdb, adec = _ret_tables(w_dec_f, w_dec_b)
    rf, rb = _ret_states(kr2, p, kdf, kdb, adec)
    yr = _ret_out(qr2, kr2, p, rf, rb, dc, qdf, qdb, gn_g)
    wg_pa, wg_pr, wg_out = _xchg_wait([st_w], st_w["lands"], [[0, 1, 2]], yr, "wgather_wait")
    w_pa_f = wg_pa.transpose(1, 0, 2).reshape(512, D)
    w_pr_f = wg_pr.transpose(1, 0, 2).reshape(512, D)
    w_out_f = wg_out.reshape(D, D)

    dout, do, dpm, dyr, mb, dub, yab, dab, drb_, sums_mid = _mid(xs, tgt, mod, g_post, o_att, p, yr, w_pa_f, w_pr_f, w_out_f)
    gw_out = _mm_tn(mb, dub, "gw_out", BF16)
    gw_pa = _mm_tn(yab, dab, "gw_pa", BF16)
    gw_pr = _mm_tn(yr, drb_, "gw_pr", BF16)
    gi_m = _mm_tn(h, dpm, "gw_in_mid", BF16)

    def shards(cols, nd):
        return cols.reshape(D, nd, ncol_in).transpose(1, 0, 2)

    st_a, tok_a = _xchg_start([
        (gw_out.reshape(NDEV, 128, D), all_dev),
        (gw_pa.reshape(512, NDEV, 128).transpose(1, 0, 2), all_dev),
        (gw_pr.reshape(512, NDEV, 128).transpose(1, 0, 2), all_dev),
        (shards(gi_m[:, 224:2048], 3), (5, 6, 7))], "xchg_start_a",
        lands=[None, None, None, jnp.zeros((NDEV, D, ncol_in), BF16)])
    dqt, dkt, dvt = _attn_bwd(qt, kh, kt, vh, do, o_t, lse + tok_a[0, 0])
    dpa, gs_att = _attn_prep_bwd(dqt, dkt, dvt, p, cos, sin, qg, kg)
    gi_a = _mm_tn(h, dpa, "gw_in_att", BF16)
    st_b, tok_b = _xchg_start([(shards(jnp.concatenate([gi_a, gi_m[:, 2048:2496]], axis=1), 2), (0, 1))], "xchg_start_b",
                              lands=[st_a["lands"][3]])
    dpra, dk_i, dv_i, drf, drb, dgn, dlg1 = _ret_bwd_chunk(qr2, kr2, p, rf, rb, dc, qdf, qdb, gn_g + tok_b[0:1, 0:1], dyr, cos, sin)
    dkf, dkb, dvf, dvb, dlg2 = _ret_bwd_scan(kr2, p, rf, rb, drf, drb, kdf, kdb, adec)
    dprb = _ret_bwd_final(dk_i, dkf, dkb, dv_i, dvf, dvb, cos, sin)
    gi_ra = _mm_tn(h, dpra, "gw_in_reta", BF16)
    gi_rb = _mm_tn(h, dprb, "gw_in_retb", BF16)
    chip_c = _pair_reduce(shards(jnp.concatenate([gi_m[:, 2496:2560], gi_ra[:, :256], gi_rb[:, 512:768], gi_rb[:, :512],
                                                  gi_ra[:, 256:768], gi_m[:, :224]], axis=1), 3), (2, 3, 4), "pair_reduce_c")
    st_c, tok_c = _xchg_start([(chip_c, (2, 3, 4, "same core"))], "xchg_start_c", lands=[st_b["lands"][0]])
    grad_x, sums_in = _bwd_in(dpm, dpa, dpra, dprb, w_p, xs, dout, mod, g_pre + tok_c[0:1, 0:1])

    gathered = _small_allgather([sums_in, sums_mid, dgn, gs_att, dlg1, dlg2], "ag_small")
    given = dict(b_ada=(b_ada, m_b_ada, v_b_ada), g_pre=(g_pre, m_g_pre, v_g_pre), qn_g=(qn_g, m_qn_g, v_qn_g), kn_g=(kn_g, m_kn_g, v_kn_g),
                 w_dec_f=(w_dec_f, m_w_dec_f, v_w_dec_f), w_dec_b=(w_dec_b, m_w_dec_b, v_w_dec_b), gn_g=(gn_g, m_gn_g, v_gn_g),
                 g_post=(g_post, m_g_post, v_g_post))
    small = _small_update(gathered, [a for nme in SMALL for a in given[nme]])
    loss = small[0][0, 0]

    g_in_all, g_mid_all = gathered[0], gathered[1]
    dmod_all = lax.dynamic_slice(jnp.concatenate([g_in_all[:, 0, :], g_in_all[:, 1, :], g_mid_all[:, 0, :]], axis=1),
                                 (0, me * ncol_ada), (NDEV, ncol_ada))
    g_ada = _mm_tn(c_act, jnp.pad(dmod_all, ((0, 8), (0, 0))).astype(BF16), "gw_ada")

    ada = _adamw(g_ada[None], w_ada[0], m_w_ada[0], v_w_ada[0], "adamw_ada")
    rs_out, rs_pa, rs_pr, rs_in = _xchg_wait([st_a, st_b, st_c], list(st_a["lands"][:3]) + [st_c["lands"][0]],
                                             [[0, 1, 2, 3], [3], [3]], ada[1], "xchg_wait")
    res = dict(
        w_ada=ada,
        w_in=_adamw(rs_in, w_in[0], m_w_in[0], v_w_in[0], "adamw_in"),
        w_pa=_adamw(rs_pa, w_pa[0], m_w_pa[0], v_w_pa[0], "adamw_pa"),
        w_pr=_adamw(rs_pr, w_pr[0], m_w_pr[0], v_w_pr[0], "adamw_pr"),
        w_out=_adamw(rs_out, w_out[0], m_w_out[0], v_w_out[0], "adamw_out"),
    )
    names = ["w_ada", "b_ada", "g_pre", "w_in", "qn_g", "kn_g", "w_dec_f", "w_dec_b", "gn_g", "w_pa", "w_pr", "w_out", "g_post"]
    outs = [[], [], [], []]
    for nme in names:
        for q in range(4):
            if nme in res:
                outs[q].append(res[nme][q][None])
            else:
                outs[q].append(small[1 + 4 * SMALL.index(nme) + q])
    return (loss, grad_x[None], *outs[0], *outs[1], *outs[2], *outs[3])
```

```python
import jax
import jax.numpy as jnp
import numpy as np
from jax import lax
from jax.experimental import pallas as pl
from jax.experimental.pallas import tpu as pltpu

F32, BF16 = jnp.float32, jnp.bfloat16
D = 1024
DH = 64
DHA = 80
DV = 128
LOG2E = 1.4426950408889634
LN2 = 0.6931471805599453
HR = 4
CH = 128
EPS = 1e-6
ROPE_THETA = 10000.0
NDEV = 8
O_GL, O_ZA, O_QA, O_KA, O_VA, O_QR, O_ZR, O_VR, O_KR, P_W = 0, 2048, 2560, 3072, 3200, 3328, 3584, 4096, 4608, 4864
ORIG = dict(qa=(0, 512), ka=(512, 640), va=(640, 768), za=(768, 1280), qr=(1280, 1536), kr=(1536, 1792),
            vr=(1792, 2304), zr=(2304, 2816), gl=(2816, 4864))
P_ORDER = ("gl", "za", "qa", "ka", "va", "qr", "zr", "vr", "kr")
ADAM_LR, ADAM_B1, ADAM_B2, ADAM_EPS, ADAM_WD, ADAM_STEP = 0.001, 0.9, 0.999, 1e-08, 0.01, 10
VMEM_BIG = 56 * 1024 * 1024
MESH = pl.DeviceIdType.MESH

NT = (((1,), (1,)), ((), ()))
TN = (((0,), (0,)), ((), ()))


def _dot(a, b, dims=None):
    if dims is None:
        return jnp.dot(a, b, preferred_element_type=F32)
    return lax.dot_general(a, b, dims, preferred_element_type=F32)


def _cp(sem=None, vmem=None):
    kw = {}
    if sem is not None:
        kw["dimension_semantics"] = sem
    if vmem is not None:
        kw["vmem_limit_bytes"] = vmem
    return pltpu.CompilerParams(**kw)


def _sigmoid(z):
    return 1.0 / (1.0 + jnp.exp(-z))


def _sum11(m):
    return jnp.sum(jnp.sum(m, axis=-1, keepdims=True), axis=0, keepdims=True)


def _full(shape):
    n = len(shape)
    return pl.BlockSpec(shape, lambda *_: (0,) * n)


def _my_pos():
    return lax.axis_index("x"), lax.axis_index("y"), lax.axis_index("c")


def _peer(k, x, y, c):
    return ((1 - x) if k & 4 else x, (1 - y) if k & 2 else y, (1 - c) if k & 1 else c)


def _small_allgather(vs, name):
    n = len(vs)

    def body(*refs):
        v_refs, out_refs = refs[:n], refs[n:2 * n]
        send_sems, recv_sems = refs[2 * n:]
        x, y, c = _my_pos()
        me = 4 * x + 2 * y + c
        cps = []
        for a in range(n):
            out_refs[a][me] = v_refs[a][...]
            for k in range(1, NDEV):
                cp = pltpu.make_async_remote_copy(src_ref=v_refs[a], dst_ref=out_refs[a].at[me], send_sem=send_sems.at[a, k - 1],
                                                  recv_sem=recv_sems.at[a, k - 1], device_id=_peer(k, x, y, c), device_id_type=MESH)
                cp.start()
                cps.append(cp)
        for cp in cps:
            cp.wait()

    vm = pl.BlockSpec(memory_space=pltpu.VMEM)
    return pl.pallas_call(
        body, name=name, out_shape=[jax.ShapeDtypeStruct((NDEV,) + v.shape, v.dtype) for v in vs],
        in_specs=[vm] * n, out_specs=[vm] * n,
        scratch_shapes=[pltpu.SemaphoreType.DMA((n, NDEV - 1)), pltpu.SemaphoreType.DMA((n, NDEV - 1))],
    )(*vs)


def _prologue(c8, w_ada_s, b_ada_s, arrs):
    n = len(arrs)
    ncol = w_ada_s.shape[1]

    def body(*refs):
        c_ref, wa_ref, ba_ref = refs[:3]
        ins = refs[3:3 + n]
        mod_ref, cact_ref = refs[3 + n:5 + n]
        outs = refs[5 + n:5 + 2 * n]
        call_ref, send_sems, recv_sems, local_sems, s_send, s_recv = refs[5 + 2 * n:]
        x, y, c = _my_pos()
        me, sibling = (x, y, c), (x, y, 1 - c)
        chips = [(1 - x, y), (x, 1 - y), (1 - x, 1 - y)]
        me_i = 4 * x + 2 * y + c

        def small_gather(src_ref, dst_ref, row):
            cps = []
            for k in range(1, NDEV):
                cp = pltpu.make_async_remote_copy(src_ref=src_ref, dst_ref=dst_ref.at[me_i], send_sem=s_send.at[row, k - 1],
                                                  recv_sem=s_recv.at[row, k - 1], device_id=_peer(k, x, y, c), device_id_type=MESH)
                cp.start()
                cps.append(cp)
            return cps

        def blk(a, px, py, pc):
            return outs[a].at[4 * px + 2 * py + pc]

        def copy(a, k, block, to, src=None):
            return pltpu.make_async_remote_copy(src_ref=blk(a, *block) if src is None else src, dst_ref=blk(a, *block),
                                                send_sem=send_sems.at[a, k], recv_sem=recv_sems.at[a, k], device_id=to, device_id_type=MESH)

        call_ref[me_i] = c_ref[...]
        for cp in small_gather(c_ref, call_ref, 0):
            cp.wait()

        local, sent = [], []
        for a in range(n):
            mine = pltpu.make_async_copy(ins[a], blk(a, *me), local_sems.at[a])
            mine.start()
            local.append(mine)
            first = [copy(a, 0, me, sibling, src=ins[a])] + [copy(a, 1 + j, me, (*chip, c), src=ins[a]) for j, chip in enumerate(chips)]
            for cp in first:
                cp.start()
            sent += first

        cv = call_ref[:, 0, :]
        ca = jnp.concatenate([cv * _sigmoid(cv), jnp.zeros_like(cv)], axis=0).astype(BF16)
        cact_ref[...] = ca
        mod_ref[me_i] = (_dot(ca, wa_ref[...].astype(BF16)) + ba_ref[...])[:8]
        mod_copies = small_gather(mod_ref.at[me_i], mod_ref, 1)

        for j, chip in enumerate(chips):
            for a in range(n):
                copy(a, 1 + j, (*chip, c), me).wait_recv()
                cp = copy(a, 4 + j, (*chip, c), sibling)
                cp.start()
                sent.append(cp)
        for a in range(n):
            copy(a, 0, sibling, me).wait_recv()
            for j, chip in enumerate(chips):
                copy(a, 4 + j, (*chip, 1 - c), me).wait_recv()
        for cp in sent:
            cp.wait_send()
        for cp in local + mod_copies:
            cp.wait()

    vm, hbm = pl.BlockSpec(memory_space=pltpu.VMEM), pl.BlockSpec(memory_space=pl.ANY)
    res = pl.pallas_call(
        body, name="prologue",
        out_shape=[jax.ShapeDtypeStruct((NDEV, 8, ncol), F32), jax.ShapeDtypeStruct((16, D), BF16)]
        + [jax.ShapeDtypeStruct((NDEV,) + a.shape, a.dtype) for a in arrs],
        in_specs=[vm, vm, vm] + [hbm] * n, out_specs=[vm, vm] + [hbm] * n,
        scratch_shapes=[pltpu.VMEM((NDEV, 8, D), F32), pltpu.SemaphoreType.DMA((n, NDEV - 1)), pltpu.SemaphoreType.DMA((n, NDEV - 1)),
                        pltpu.SemaphoreType.DMA((n,)), pltpu.SemaphoreType.DMA((2, NDEV - 1)), pltpu.SemaphoreType.DMA((2, NDEV - 1))],
    )(c8, w_ada_s, b_ada_s, *arrs)
    return res[0], res[1], res[2:]


def _in_set(idx, dests):
    p = idx == dests[0]
    for d in dests[1:]:
        p = jnp.logical_or(p, idx == d)
    return p


_HBM = pl.BlockSpec(memory_space=pltpu.HBM)
_SEM = pl.BlockSpec(memory_space=pltpu.SEMAPHORE)


def _pair_reduce(send, dests, name):
    nd = send.shape[0]

    def body(s_ref, o_ref, land, ssem, rsem):
        x, y, c = _my_pos()
        cps = []
        for i in range(nd):
            cp = pltpu.make_async_remote_copy(src_ref=s_ref.at[i], dst_ref=land.at[i], send_sem=ssem.at[i], recv_sem=rsem.at[i],
                                              device_id=(x, y, 1 - c), device_id_type=MESH)
            pl.when(c != (dests[i] & 1))(cp.start)
            cps.append(cp)
        for i in range(nd):
            mine = c == (dests[i] & 1)

            @pl.when(mine)
            def _():
                cps[i].wait_recv()
                o_ref[i] = (s_ref[i].astype(F32) + land[i].astype(F32)).astype(BF16)

            pl.when(jnp.logical_not(mine))(cps[i].wait_send)

    vm = pl.BlockSpec(memory_space=pltpu.VMEM)
    return pl.pallas_call(
        body, name=name, out_shape=jax.ShapeDtypeStruct(send.shape, send.dtype), in_specs=[vm], out_specs=vm,
        scratch_shapes=[pltpu.VMEM(send.shape, send.dtype), pltpu.SemaphoreType.DMA((nd,)), pltpu.SemaphoreType.DMA((nd,))],
        compiler_params=_cp(None, VMEM_BIG),
    )(send)


def _xchg_copies(xs_dests, sends, lands, ssem, rsem, lsem):
    x, y, c = _my_pos()
    me = 4 * x + 2 * y + c
    remote, local = [], []
    for a, dests in enumerate(xs_dests):
        same_core = dests[-1] == "same core"
        dests = dests[:-1] if same_core else dests
        lo, nd = dests[0], sends[a].shape[0]
        for k in range(1, NDEV):
            if same_core and k & 1:
                continue
            px, py, pc = _peer(k, x, y, c)
            pidx = 4 * px + 2 * py + pc
            cp = pltpu.make_async_remote_copy(src_ref=sends[a].at[jnp.clip(pidx - lo, 0, nd - 1)], dst_ref=lands[a].at[me],
                                              send_sem=ssem.at[a * (NDEV - 1) + k - 1], recv_sem=rsem.at[a * (NDEV - 1) + k - 1],
                                              device_id=(px, py, pc), device_id_type=MESH)
            remote.append((cp, _in_set(pidx, dests), _in_set(me, dests)))
        lc = pltpu.make_async_copy(sends[a].at[jnp.clip(me - lo, 0, nd - 1)], lands[a].at[me], lsem.at[a])
        local.append((lc, _in_set(me, dests)))
    return remote, local


def _xchg_start(xs, name, lands=None):
    n = len(xs)
    dests = [d for _, d in xs]
    sends = [pltpu.with_memory_space_constraint(s, pltpu.HBM) for s, _ in xs]
    lands = [None] * n if lands is None else lands
    lands = [pltpu.with_memory_space_constraint(lax.empty((NDEV,) + s.shape[1:], s.dtype) if l is None else l, pltpu.HBM)
             for (s, _), l in zip(xs, lands)]

    def body(*refs):
        send_refs, land_refs = refs[:n], refs[n:2 * n]
        ssem, rsem, lsem = refs[2 * n:2 * n + 3]
        token = refs[-1]
        remote, local = _xchg_copies(dests, send_refs, land_refs, ssem, rsem, lsem)
        for cp, to_dest, _ in remote:
            pl.when(to_dest)(cp.start)
        for lc, i_am_dest in local:
            pl.when(i_am_dest)(lc.start)
        token[...] = jnp.zeros_like(token)

    res = pl.pallas_call(
        body, name=name,
        out_shape=[pltpu.SemaphoreType.DMA((n * (NDEV - 1),)), pltpu.SemaphoreType.DMA((n * (NDEV - 1),)), pltpu.SemaphoreType.DMA((n,))]
        + [pltpu.HBM(a.shape, a.dtype) for a in list(sends) + list(lands)] + [jax.ShapeDtypeStruct((8, 128), F32)],
        in_specs=[_HBM] * (2 * n), out_specs=[_SEM, _SEM, _SEM] + [_HBM] * (2 * n) + [pl.BlockSpec(memory_space=pltpu.VMEM)],
        input_output_aliases={i: 3 + i for i in range(2 * n)},
        compiler_params=pltpu.CompilerParams(has_side_effects=pltpu.SideEffectType.DATAFLOW_SIDE_EFFECTING),
    )(*sends, *lands)
    return dict(sems=res[0:3], sends=res[3:3 + n], lands=res[3 + n:3 + 2 * n], dests=dests), res[-1]


def _xchg_wait(states, lands, land_of, after, name):
    flat = []
    for st in states:
        flat += list(st["sends"]) + list(st["sems"])
    nl = len(lands)

    def body(*refs):
        land_refs = refs[:nl]
        pos = nl
        for s, st in enumerate(states):
            n = len(st["dests"])
            send_refs = refs[pos:pos + n]
            ssem, rsem, lsem = refs[pos + n:pos + n + 3]
            pos += n + 3
            remote, local = _xchg_copies(st["dests"], send_refs, [land_refs[i] for i in land_of[s]], ssem, rsem, lsem)
            for cp, to_dest, i_am_dest in remote:
                pl.when(to_dest)(cp.wait_send)
                pl.when(i_am_dest)(cp.wait_recv)
            for lc, i_am_dest in local:
                pl.when(i_am_dest)(lc.wait)

    in_specs = [_HBM] * nl
    for st in states:
        in_specs += [_HBM] * len(st["dests"]) + [_SEM, _SEM, _SEM]
    return pl.pallas_call(
        body, name=name, out_shape=[pltpu.HBM(a.shape, a.dtype) for a in lands],
        in_specs=in_specs + [pl.BlockSpec(memory_space=pl.ANY)], out_specs=[_HBM] * nl,
        input_output_aliases={i: i for i in range(nl)},
        compiler_params=pltpu.CompilerParams(has_side_effects=pltpu.SideEffectType.DATAFLOW_SIDE_EFFECTING),
    )(*lands, *flat, after)


def _mm_tn(a, b, name, out_dtype=F32):
    S, M = a.shape
    N = b.shape[1]
    tn = N if N <= 768 else (640 if N % 640 == 0 else 512)
    tk = min(4096 if N > tn else 2048, S)
    nk = S // tk

    def body(a_ref, b_ref, o_ref, acc):
        k = pl.program_id(1)

        @pl.when(k == 0)
        def _():
            acc[...] = _dot(a_ref[...], b_ref[...], TN)

        @pl.when(k > 0)
        def _():
            acc[...] += _dot(a_ref[...], b_ref[...], TN)

        @pl.when(k == nk - 1)
        def _():
            o_ref[...] = acc[...].astype(out_dtype)

    return pl.pallas_call(
        body, name=name, out_shape=jax.ShapeDtypeStruct((M, N), out_dtype), grid=(N // tn, nk),
        in_specs=[pl.BlockSpec((tk, M), lambda j, k: (k, 0)), pl.BlockSpec((tk, tn), lambda j, k: (k, j))],
        out_specs=pl.BlockSpec((M, tn), lambda j, k: (0, j)), scratch_shapes=[pltpu.VMEM((M, tn), F32)],
        compiler_params=_cp(("parallel", "arbitrary"), VMEM_BIG),
    )(a, b)


def _fwd_in(x, mod, g_pre, w_p):
    S = x.shape[0]
    tm = min(512, S)

    def body(x_ref, mod_ref, g_ref, w_ref, p_ref, h_ref):
        xv = x_ref[...]
        r = lax.rsqrt(jnp.mean(xv * xv, axis=-1, keepdims=True) + EPS)
        h = (((xv * r) * g_ref[...]) * (1.0 + mod_ref[1:2, :]) + mod_ref[0:1, :]).astype(BF16)
        h_ref[...] = h
        p_ref[...] = _dot(h, w_ref[...]).astype(BF16)

    return pl.pallas_call(
        body, name="fwd_in", out_shape=[jax.ShapeDtypeStruct((S, P_W), BF16), jax.ShapeDtypeStruct((S, D), BF16)],
        grid=(S // tm,),
        in_specs=[pl.BlockSpec((tm, D), lambda i: (i, 0)), _full((3, D)), _full((1, D)), _full((D, P_W))],
        out_specs=[pl.BlockSpec((tm, P_W), lambda i: (i, 0)), pl.BlockSpec((tm, D), lambda i: (i, 0))],
        compiler_params=_cp(("parallel",), VMEM_BIG),
    )(x, mod, g_pre, w_p)


def _swap16(v):
    lane = lax.broadcasted_iota(jnp.int32, v.shape, 1)
    return jnp.where((lane % 32) < 16, pltpu.roll(v, 112, 1), pltpu.roll(v, 16, 1))


def _rope(v, cos, sin):
    return v * cos + _swap16(v) * sin


def _rope_t(v, cos, sin):
    return v * cos - _swap16(v) * sin


def _head_mean(v):
    lo = lax.broadcasted_iota(jnp.int32, v.shape, 1) < 64
    m0 = jnp.sum(jnp.where(lo, v, 0.0), axis=-1, keepdims=True)
    m1 = jnp.sum(jnp.where(lo, 0.0, v), axis=-1, keepdims=True)
    return jnp.where(lo, m0, m1) * (1.0 / 64.0)


def _prep(p, cos, sin, qg, kg):
    S = p.shape[0]
    tm = min(512, S)

    def body(qa_ref, kv_ref, qr_ref, kr_ref, cos_ref, sin_ref, qg_ref, kg_ref, qt_ref, kh_ref, kt_ref, vh_ref, vta_ref, qr2_ref, kr2_ref):
        cos_v, sin_v = cos_ref[...], sin_ref[...]
        for g in range(4):
            xv = qa_ref[:, 128 * g:128 * g + 128].astype(F32)
            r = lax.rsqrt(_head_mean(xv * xv) + EPS)
            yt = (_rope((xv * r) * qg_ref[...], cos_v, sin_v) * (0.125 * LOG2E)).T
            qt_ref[2 * g] = yt[:DH].astype(BF16)
            qt_ref[2 * g + 1] = yt[DH:].astype(BF16)
        xv = kv_ref[:, :128].astype(F32)
        r = lax.rsqrt(_head_mean(xv * xv) + EPS)
        yv = _rope((xv * r) * kg_ref[...], cos_v, sin_v)
        kh_ref[0] = yv[:, :64].astype(BF16)
        kh_ref[1] = yv[:, 64:].astype(BF16)
        yt = yv.T
        kt_ref[0] = yt[:DH].astype(BF16)
        kt_ref[1] = yt[DH:].astype(BF16)
        vv = kv_ref[:, 128:].astype(F32)
        vh_ref[0] = vv[:, :64].astype(BF16)
        vh_ref[1] = vv[:, 64:].astype(BF16)
        vt = vv.T
        tail = (lax.broadcasted_iota(jnp.int32, (DHA - DH, tm), 0) == 0).astype(BF16)
        for kvh in range(2):
            vta_ref[kvh, 0:DH, :] = vt[DH * kvh:DH * kvh + DH].astype(BF16)
            vta_ref[kvh, DH:DHA, :] = tail
        for g in range(2):
            sl = slice(128 * g, 128 * g + 128)
            qr2_ref[:, sl] = _rope(qr_ref[:, sl].astype(F32), cos_v, sin_v)
            kr2_ref[:, sl] = _rope(kr_ref[:, sl].astype(F32), cos_v, sin_v) * 0.125

    hm = lambda n: pl.BlockSpec((n, tm, DH), lambda i: (0, i, 0))
    ht = lambda n, r: pl.BlockSpec((n, r, tm), lambda i: (0, 0, i))
    return pl.pallas_call(
        body, name="prep",
        out_shape=[jax.ShapeDtypeStruct((8, DH, S), BF16), jax.ShapeDtypeStruct((2, S, DH), BF16), jax.ShapeDtypeStruct((2, DH, S), BF16),
                   jax.ShapeDtypeStruct((2, S, DH), BF16), jax.ShapeDtypeStruct((2, DHA, S), BF16),
                   jax.ShapeDtypeStruct((S, 256), F32), jax.ShapeDtypeStruct((S, 256), F32)],
        grid=(S // tm,),
        in_specs=[pl.BlockSpec((tm, 512), lambda i: (i, O_QA // 512)), pl.BlockSpec((tm, 256), lambda i: (i, O_KA // 256)),
                  pl.BlockSpec((tm, 256), lambda i: (i, O_QR // 256)), pl.BlockSpec((tm, 256), lambda i: (i, O_KR // 256)),
                  pl.BlockSpec((tm, 128), lambda i: (i, 0)), pl.BlockSpec((tm, 128), lambda i: (i, 0)), _full((1, 128)), _full((1, 128))],
        out_specs=[ht(8, DH), hm(2), ht(2, DH), hm(2), ht(2, DHA), pl.BlockSpec((tm, 256), lambda i: (i, 0)), pl.BlockSpec((tm, 256), lambda i: (i, 0))],
        compiler_params=_cp(("parallel",)),
    )(p, p, p, p, cos, sin, qg, kg)


def _attn_fwd(qt, kh, vta):
    S = qt.shape[2]
    tq, tk = min(1024, S), min(512, S)
    nj = S // tk

    def body(q_ref, k_ref, v_ref, o_ref, ot_ref, lse_ref, m_s, acc_s):
        j = pl.program_id(1)

        @pl.when(j == 0)
        def _():
            m_s[...] = jnp.full_like(m_s, -jnp.inf)
            acc_s[...] = jnp.zeros_like(acc_s)

        m_all = m_s[...]
        st = {0: _dot(k_ref[0], q_ref[0])}
        m_new, acc_new = [], []
        for h in range(8):
            if h + 1 < 8:
                st[h + 1] = _dot(k_ref[(h + 1) // 4], q_ref[h + 1])
            m_old = m_all[h:h + 1, :]
            mn = jnp.maximum(m_old, jnp.max(st[h], axis=0, keepdims=True))
            pt = jnp.exp2(st[h] - mn).astype(BF16)
            acc_new.append(jnp.exp2(m_old - mn) * acc_s[h] + _dot(v_ref[h // 4], pt))
            m_new.append(mn)
            del st[h]
        for h in range(8):
            acc_s[h] = acc_new[h]
            m_s[h:h + 1, :] = m_new[h]

        @pl.when(j == nj - 1)
        def _():
            for h in range(8):
                ot = acc_s[h, 0:DH, :] / acc_s[h, DH:DH + 1, :]
                ot_ref[h] = ot
                o_ref[:, DH * h:DH * h + DH] = ot.T
                lse_ref[h // 4, h % 4:h % 4 + 1, :] = m_s[h:h + 1, :] + jnp.log2(acc_s[h, DH:DH + 1, :])

    return pl.pallas_call(
        body, name="attn_fwd",
        out_shape=[jax.ShapeDtypeStruct((S, 512), F32), jax.ShapeDtypeStruct((8, DH, S), F32), jax.ShapeDtypeStruct((2, 4, S), F32)],
        grid=(S // tq, nj),
        in_specs=[pl.BlockSpec((8, DH, tq), lambda i, j: (0, 0, i)), pl.BlockSpec((2, tk, DH), lambda i, j: (0, j, 0)),
                  pl.BlockSpec((2, DHA, tk), lambda i, j: (0, 0, j))],
        out_specs=[pl.BlockSpec((tq, 512), lambda i, j: (i, 0)), pl.BlockSpec((8, DH, tq), lambda i, j: (0, 0, i)),
                   pl.BlockSpec((2, 4, tq), lambda i, j: (0, 0, i))],
        scratch_shapes=[pltpu.VMEM((8, tq), F32), pltpu.VMEM((8, DHA, tq), F32)],
        compiler_params=_cp(("parallel", "arbitrary"), VMEM_BIG),
    )(qt, kh, vta)


def _ret_tables(wf, wb):
    C = CH

    def body(wf_ref, wb_ref, dc_ref, qdf_ref, qdb_ref, kdf_ref, kdb_ref, a_ref):
        def logsig(w):
            z = jnp.exp(-jnp.abs(w))
            u = 1.0 + z
            l1p = jnp.where(u == 1.0, z, jnp.log(u) * (z / jnp.where(u == 1.0, 1.0, u - 1.0)))
            return jnp.minimum(w, 0.0) - l1p

        lgf, lgb = logsig(wf_ref[...]), logsig(wb_ref[...])
        lane4 = lax.broadcasted_iota(jnp.int32, (1, 4), 1)

        def pick(lg, h):
            return jnp.sum(jnp.where(lane4 == h, lg, 0.0), axis=-1, keepdims=True)

        ii = lax.broadcasted_iota(jnp.int32, (C, C), 0).astype(F32)
        jj = lax.broadcasted_iota(jnp.int32, (C, C), 1).astype(F32)
        dif = ii - jj
        hd = lax.broadcasted_iota(jnp.int32, (C, 256), 1) // DH
        lf_l = jnp.zeros((C, 256), F32)
        lb_l = jnp.zeros((C, 256), F32)
        for h in range(HR):
            lf, lb = pick(lgf, h), pick(lgb, h)
            dc_ref[h] = jnp.where(dif >= 0, jnp.exp(lf * jnp.maximum(dif, 0.0)), jnp.exp(lb * jnp.maximum(-dif, 0.0)))
            lf_l = jnp.where(hd == h, lf, lf_l)
            lb_l = jnp.where(hd == h, lb, lb_l)
            a_ref[h:h + 1, :] = jnp.broadcast_to(jnp.exp(lf * C), (1, 128))
            a_ref[HR + h:HR + h + 1, :] = jnp.broadcast_to(jnp.exp(lb * C), (1, 128))
        ri = lax.broadcasted_iota(jnp.int32, (C, 256), 0).astype(F32)
        qdf_ref[...] = jnp.exp(lf_l * (ri + 1.0))
        qdb_ref[...] = jnp.exp(lb_l * (C - ri))
        kdf_ref[...] = jnp.exp(lf_l * (C - 1.0 - ri))
        kdb_ref[...] = jnp.exp(lb_l * ri)

    t = jax.ShapeDtypeStruct((C, 256), F32)
    return pl.pallas_call(body, name="ret_tables",
                          out_shape=[jax.ShapeDtypeStruct((HR, C, C), F32), t, t, t, t, jax.ShapeDtypeStruct((8, 128), F32)])(wf, wb)


def _ret_states(kr2, p, kdf, kdb, adec):
    S = kr2.shape[0]
    C, N = CH, S // CH
    G = _scan_group(N)
    NG = N // G

    def body(kf_ref, vf_ref, kb_ref, vb_ref, kdf_ref, kdb_ref, a_ref, rf_ref, rb_ref, sf, sb):
        @pl.when(pl.program_id(0) == 0)
        def _():
            sf[...] = jnp.zeros_like(sf)
            sb[...] = jnp.zeros_like(sb)

        kvf, kvb = [], []
        for u in range(G):
            rows = slice(C * u, C * u + C)
            kdfw = (kf_ref[rows, :] * kdf_ref[...]).astype(BF16)
            kdbw = (kb_ref[rows, :] * kdb_ref[...]).astype(BF16)
            vf, vb = vf_ref[rows, :].astype(BF16), vb_ref[rows, :].astype(BF16)
            kvf.append([_dot(kdfw[:, _ks(h)], vf[:, _vs(h)], TN) for h in range(HR)])
            kvb.append([_dot(kdbw[:, _ks(h)], vb[:, _vs(h)], TN) for h in range(HR)])
        for u in range(G):
            rf_ref[u] = sf[...]
            for h in range(HR):
                sf[h] = a_ref[h:h + 1, :] * sf[h] + kvf[u][h]
        for u in reversed(range(G)):
            rb_ref[u] = sb[...]
            for h in range(HR):
                sb[h] = a_ref[HR + h:HR + h + 1, :] * sb[h] + kvb[u][h]

    st = jax.ShapeDtypeStruct((N, HR, DH, DV), F32)
    return pl.pallas_call(
        body, name="ret_states", out_shape=[st, st], grid=(NG,),
        in_specs=[pl.BlockSpec((G * C, 256), lambda t: (t, 0)), pl.BlockSpec((G * C, 512), lambda t: (t, O_VR // 512)),
                  pl.BlockSpec((G * C, 256), lambda t: (NG - 1 - t, 0)), pl.BlockSpec((G * C, 512), lambda t: (NG - 1 - t, O_VR // 512)),
                  _full((C, 256)), _full((C, 256)), _full((8, 128))],
        out_specs=[pl.BlockSpec((G, HR, DH, DV), lambda t: (t, 0, 0, 0)), pl.BlockSpec((G, HR, DH, DV), lambda t: (NG - 1 - t, 0, 0, 0))],
        scratch_shapes=[pltpu.VMEM((HR, DH, DV), F32), pltpu.VMEM((HR, DH, DV), F32)],
        compiler_params=_cp(("arbitrary",)),
    )(kr2, p, kr2, p, kdf, kdb, adec)


def _scan_group(n):
    return 4 if n % 4 == 0 else (2 if n % 2 == 0 else 1)


def _ks(h):
    return slice(DH * h, DH * h + DH)


def _vs(h):
    return slice(DV * h, DV * h + DV)


def _ret_heads_fwd(qb, kb, vb, qfw, qbw, dc_ref, rf_ref, rb_ref, u=0):
    hs = range(HR)
    s = [_dot(qb[:, _ks(h)], kb[:, _ks(h)], NT) for h in hs]
    inter = [_dot(qfw[:, _ks(h)], rf_ref[u, h].astype(BF16)) + _dot(qbw[:, _ks(h)], rb_ref[u, h].astype(BF16)) for h in hs]
    sd = [s[h] * dc_ref[h] for h in hs]
    o = [_dot(sd[h].astype(BF16), vb[:, _vs(h)]) + inter[h] for h in hs]
    return sd, o


def _ret_out(qr2, kr2, p, rf, rb, dc, qdf, qdb, gn):
    S = qr2.shape[0]
    C, N = CH, S // CH
    G = _scan_group(N)

    def body(q_ref, k_ref, v_ref, z_ref, rf_ref, rb_ref, dc_ref, qdf_ref, qdb_ref, gn_ref, yr_ref):
        outs = []
        for u in range(G):
            rows = slice(C * u, C * u + C)
            qv = q_ref[rows, :]
            qb, kb, vb = qv.astype(BF16), k_ref[rows, :].astype(BF16), v_ref[rows, :].astype(BF16)
            qfw, qbw = (qv * qdf_ref[...]).astype(BF16), (qv * qdb_ref[...]).astype(BF16)
            outs.append(_ret_heads_fwd(qb, kb, vb, qfw, qbw, dc_ref, rf_ref, rb_ref, u)[1])
        for u in range(G):
            rows = slice(C * u, C * u + C)
            for h in range(HR):
                vs = _vs(h)
                o = outs[u][h]
                mu = jnp.mean(o, axis=-1, keepdims=True)
                var = jnp.mean(jnp.square(o - mu), axis=-1, keepdims=True)
                on = (o - mu) * lax.rsqrt(var + EPS)
                z = z_ref[rows, vs].astype(F32)
                yr_ref[rows, vs] = ((on * gn_ref[:, vs]) * (z * _sigmoid(z))).astype(BF16)

    row = lambda w, off=0: pl.BlockSpec((G * C, w), lambda t: (t, off))
    stb = lambda: pl.BlockSpec((G, HR, DH, DV), lambda t: (t, 0, 0, 0))
    return pl.pallas_call(
        body, name="ret_out", out_shape=jax.ShapeDtypeStruct((S, 512), BF16), grid=(N // G,),
        in_specs=[row(256), row(256), row(512, O_VR // 512), row(512, O_ZR // 512), stb(), stb(),
                  _full((HR, C, C)), _full((C, 256)), _full((C, 256)), _full((1, 512))],
        out_specs=row(512),
        compiler_params=_cp(("parallel",)),
    )(qr2, kr2, p, p, rf, rb, dc, qdf, qdb, gn)


def _mid(x, tgt, mod, g_post, o_att, p, yr, w_pa, w_pr, w_out):
    S = x.shape[0]
    tm = min(256, S)

    def body(x_ref, t_ref, mod_ref, gp_ref, o_ref, za_ref, gl_ref, yr_ref, wpa_ref, wpr_ref, wout_ref,
             dout_ref, do_ref, dpm_ref, dyr_ref, mb_ref, dub_ref, yab_ref, dab_ref, drb_ref, sums_ref):
        @pl.when(pl.program_id(0) == 0)
        def _():
            sums_ref[...] = jnp.zeros_like(sums_ref)

        za = za_ref[...].astype(F32)
        sa = _sigmoid(za)
        sil = za * sa
        ov = o_ref[...]
        ya_b = (ov * sil).astype(BF16)
        yr_b = yr_ref[...]
        av = _dot(ya_b, wpa_ref[...])
        rv = _dot(yr_b, wpr_ref[...])
        ga = _sigmoid(gl_ref[:, :D].astype(F32))
        gr = _sigmoid(gl_ref[:, D:].astype(F32))
        mb = (ga * av + gr * rv).astype(BF16)
        u = _dot(mb, wout_ref[...])
        r2 = lax.rsqrt(jnp.mean(u * u, axis=-1, keepdims=True) + EPS)
        un = u * r2
        gp = gp_ref[...]
        yv = un * gp
        gate = mod_ref[2:3, :]
        err = (x_ref[...] + gate * yv) - t_ref[...]
        dout = err * (1.0 / D)
        dout_ref[...] = dout
        dy = dout * gate
        sums_ref[0:1, :] += jnp.sum(dout * yv, axis=0, keepdims=True)
        sums_ref[1:2, :] += jnp.sum(dy * un, axis=0, keepdims=True)
        sums_ref[2:3, :] += jnp.sum(err * err, axis=0, keepdims=True)
        dyg = dy * gp
        du_b = (r2 * (dyg - un * jnp.mean(dyg * un, axis=-1, keepdims=True))).astype(BF16)
        dm = _dot(du_b, wout_ref[...], NT)
        da_b = (dm * ga).astype(BF16)
        dr_b = (dm * gr).astype(BF16)
        dpm_ref[:, :D] = (dm * av * (ga * (1.0 - ga))).astype(BF16)
        dpm_ref[:, D:2 * D] = (dm * rv * (gr * (1.0 - gr))).astype(BF16)
        dya = _dot(da_b, wpa_ref[...], NT)
        dyr_ref[...] = _dot(dr_b, wpr_ref[...], NT)
        dov = dya * sil
        for g in range(4):
            dt = dov[:, 128 * g:128 * g + 128].T
            do_ref[2 * g] = dt[:DH].astype(BF16)
            do_ref[2 * g + 1] = dt[DH:].astype(BF16)
        dpm_ref[:, 2 * D:] = (dya * ov * (sa * (1.0 + za * (1.0 - sa)))).astype(BF16)
        mb_ref[...] = mb
        dub_ref[...] = du_b
        yab_ref[...] = ya_b
        dab_ref[...] = da_b
        drb_ref[...] = dr_b

    row = lambda w: pl.BlockSpec((tm, w), lambda i: (i, 0))
    sd = lambda w, dt: jax.ShapeDtypeStruct((S, w), dt)
    return pl.pallas_call(
        body, name="mid",
        out_shape=[sd(D, F32), jax.ShapeDtypeStruct((8, DH, S), BF16), sd(2560, BF16), sd(512, F32), sd(D, BF16), sd(D, BF16), sd(512, BF16),
                   sd(D, BF16), sd(D, BF16), jax.ShapeDtypeStruct((8, D), F32)],
        grid=(S // tm,),
        in_specs=[row(D), row(D), _full((3, D)), _full((1, D)), row(512), pl.BlockSpec((tm, 512), lambda i: (i, O_ZA // 512)),
                  pl.BlockSpec((tm, 2048), lambda i: (i, 0)), row(512), _full((512, D)), _full((512, D)), _full((D, D))],
        out_specs=[row(D), pl.BlockSpec((8, DH, tm), lambda i: (0, 0, i)), row(2560), row(512), row(D), row(D), row(512), row(D), row(D),
                   _full((8, D))],
        compiler_params=_cp(("arbitrary",), VMEM_BIG),
    )(x, tgt, mod, g_post, o_att, p, p, yr, w_pa, w_pr, w_out)


def _attn_bwd(qt, kh, kt, vh, dot_, ot, lse):
    S = qt.shape[2]
    tq, tk = min(1024, S), min(1024, S)

    def body(q_ref, k_ref, kt_ref, v_ref, do_ref, o_ref, lse_ref, dq_ref, dk_ref, dv_ref):
        j, i = pl.program_id(0), pl.program_id(1)
        cols = pl.ds(pl.multiple_of(i * tq, tq), tq)
        st = {0: _dot(k_ref[0], q_ref[0])}
        dpt = {0: _dot(v_ref[0], do_ref[0])}
        dk_acc, dv_acc, dqs = [None, None], [None, None], []
        for h in range(8):
            g = h // 4
            if h + 1 < 8:
                st[h + 1] = _dot(k_ref[(h + 1) // 4], q_ref[h + 1])
                dpt[h + 1] = _dot(v_ref[(h + 1) // 4], do_ref[h + 1])
            qt_h, dot_h = q_ref[h], do_ref[h]
            delta = jnp.sum(dot_h.astype(F32) * o_ref[h], axis=0, keepdims=True)
            pt = jnp.exp2(st[h] - lse_ref[g, h % 4:h % 4 + 1, :])
            dst = (pt * (dpt[h] - delta)).astype(BF16)
            dv_h = _dot(dot_h, pt.astype(BF16), NT)
            dk_h = _dot(qt_h, dst, NT)
            dqs.append(_dot(kt_ref[g], dst))
            dv_acc[g] = dv_h if dv_acc[g] is None else dv_acc[g] + dv_h
            dk_acc[g] = dk_h if dk_acc[g] is None else dk_acc[g] + dk_h
            del st[h], dpt[h]

        @pl.when(i == 0)
        def _():
            for g in range(2):
                dk_ref[g] = dk_acc[g]
                dv_ref[g] = dv_acc[g]

        @pl.when(i > 0)
        def _():
            for g in range(2):
                dk_ref[g] += dk_acc[g]
                dv_ref[g] += dv_acc[g]

        @pl.when(j == 0)
        def _():
            for h in range(8):
                dq_ref[h, :, cols] = dqs[h]

        @pl.when(j > 0)
        def _():
            for h in range(8):
                dq_ref[h, :, cols] += dqs[h]

    return pl.pallas_call(
        body, name="attn_bwd",
        out_shape=[jax.ShapeDtypeStruct((8, DH, S), F32), jax.ShapeDtypeStruct((2, DH, S), F32), jax.ShapeDtypeStruct((2, DH, S), F32)],
        grid=(S // tk, S // tq),
        in_specs=[pl.BlockSpec((8, DH, tq), lambda j, i: (0, 0, i)), pl.BlockSpec((2, tk, DH), lambda j, i: (0, j, 0)),
                  pl.BlockSpec((2, DH, tk), lambda j, i: (0, 0, j)), pl.BlockSpec((2, tk, DH), lambda j, i: (0, j, 0)),
                  pl.BlockSpec((8, DH, tq), lambda j, i: (0, 0, i)), pl.BlockSpec((8, DH, tq), lambda j, i: (0, 0, i)),
                  pl.BlockSpec((2, 4, tq), lambda j, i: (0, 0, i))],
        out_specs=[pl.BlockSpec((8, DH, S), lambda j, i: (0, 0, 0)), pl.BlockSpec((2, DH, tk), lambda j, i: (0, 0, j)),
                   pl.BlockSpec((2, DH, tk), lambda j, i: (0, 0, j))],
        compiler_params=_cp(("arbitrary", "arbitrary"), VMEM_BIG),
    )(qt, kh, kt, vh, dot_, ot, lse)


def _attn_prep_bwd(dqt, dkt, dvt, p, cos, sin, qg, kg):
    S = dqt.shape[2]
    tm = min(512, S)

    def body(dq_ref, dk_ref, dv_ref, qa_ref, ka_ref, cos_ref, sin_ref, qg_ref, kg_ref, dp_ref, gs_ref):
        @pl.when(pl.program_id(0) == 0)
        def _():
            gs_ref[...] = jnp.zeros_like(gs_ref)

        cos_v, sin_v = cos_ref[...], sin_ref[...]

        def pair(ref, a):
            return jnp.concatenate([ref[a], ref[a + 1]], axis=0).T

        def norm_bwd(dyv, xv, gv, row):
            r = lax.rsqrt(_head_mean(xv * xv) + EPS)
            xn = xv * r
            dxh = _rope_t(dyv, cos_v, sin_v)
            gs_ref[row:row + 1, :] += jnp.sum(dxh * xn, axis=0, keepdims=True)
            dg = dxh * gv
            return r * (dg - xn * _head_mean(dg * xn))

        for g in range(4):
            sl = slice(128 * g, 128 * g + 128)
            dp_ref[:, sl] = norm_bwd(pair(dq_ref, 2 * g) * 0.125, qa_ref[:, sl].astype(F32), qg_ref[...], 0).astype(BF16)
        dp_ref[:, 512:640] = norm_bwd(pair(dk_ref, 0) * LN2, ka_ref[...].astype(F32), kg_ref[...], 1).astype(BF16)
        dp_ref[:, 640:768] = pair(dv_ref, 0).astype(BF16)

    ht = lambda n: pl.BlockSpec((n, DH, tm), lambda i: (0, 0, i))
    return pl.pallas_call(
        body, name="attn_prep_bwd", out_shape=[jax.ShapeDtypeStruct((S, 768), BF16), jax.ShapeDtypeStruct((8, 128), F32)],
        grid=(S // tm,),
        in_specs=[ht(8), ht(2), ht(2),
                  pl.BlockSpec((tm, 512), lambda i: (i, O_QA // 512)), pl.BlockSpec((tm, 128), lambda i: (i, O_KA // 128)),
                  pl.BlockSpec((tm, 128), lambda i: (i, 0)), pl.BlockSpec((tm, 128), lambda i: (i, 0)), _full((1, 128)), _full((1, 128))],
        out_specs=[pl.BlockSpec((tm, 768), lambda i: (i, 0)), _full((8, 128))],
        compiler_params=_cp(("arbitrary",)),
    )(dqt, dkt, dvt, p, p, cos, sin, qg, kg)


def _ret_bwd_chunk(qr2, kr2, p, rf, rb, dc, qdf, qdb, gn, dyr, cos, sin):
    S = qr2.shape[0]
    C, N = CH, S // CH
    G = 2 if N % 2 == 0 else 1

    def body(q_ref, k_ref, v_ref, z_ref, rf_ref, rb_ref, dc_ref, qdf_ref, qdb_ref, gn_ref, dyr_ref, cos_ref, sin_ref,
             dpa_ref, dk_ref, dv_ref, drf_ref, drb_ref, dgn_ref, dlg_ref, dqs):
        @pl.when(pl.program_id(0) == 0)
        def _():
            dgn_ref[...] = jnp.zeros_like(dgn_ref)
            dlg_ref[...] = jnp.zeros_like(dlg_ref)

        ii = lax.broadcasted_iota(jnp.int32, (C, C), 0).astype(F32)
        jj = lax.broadcasted_iota(jnp.int32, (C, C), 1).astype(F32)
        dif = ii - jj
        ri = lax.broadcasted_iota(jnp.int32, (C, 1), 0).astype(F32)
        hs, us = range(HR), range(G)
        rows = [slice(C * u, C * u + C) for u in us]
        qv = [q_ref[rows[u], :] for u in us]
        qb = [qv[u].astype(BF16) for u in us]
        kb = [k_ref[rows[u], :].astype(BF16) for u in us]
        vb = [v_ref[rows[u], :].astype(BF16) for u in us]
        qf32 = [qv[u] * qdf_ref[...] for u in us]
        qb32 = [qv[u] * qdb_ref[...] for u in us]
        qfw = [qf32[u].astype(BF16) for u in us]
        qbw = [qb32[u].astype(BF16) for u in us]
        fwd = [_ret_heads_fwd(qb[u], kb[u], vb[u], qfw[u], qbw[u], dc_ref, rf_ref, rb_ref, u) for u in us]
        sd = [f[0] for f in fwd]
        do_b = [[] for _ in us]
        for u in us:
            for h in hs:
                vs = _vs(h)
                o = fwd[u][1][h]
                mu = jnp.mean(o, axis=-1, keepdims=True)
                rstd = lax.rsqrt(jnp.mean(jnp.square(o - mu), axis=-1, keepdims=True) + EPS)
                on = (o - mu) * rstd
                z = z_ref[rows[u], vs].astype(F32)
                sz = _sigmoid(z)
                dy = dyr_ref[rows[u], vs]
                gnv = gn_ref[:, vs]
                dpa_ref[rows[u], 256 + DV * h:256 + DV * h + DV] = (dy * (on * gnv) * (sz * (1.0 + z * (1.0 - sz)))).astype(BF16)
                dys = dy * (z * sz)
                dgn_ref[:, vs] += jnp.sum(dys * on, axis=0, keepdims=True)
                don = dys * gnv
                do = rstd * (don - jnp.mean(don, axis=-1, keepdims=True) - on * jnp.mean(don * on, axis=-1, keepdims=True))
                do_b[u].append(do.astype(BF16))
        dpm = [[_dot(do_b[u][h], vb[u][:, _vs(h)], NT) for h in hs] for u in us]
        dqf = [[_dot(do_b[u][h], rf_ref[u, h].astype(BF16), NT) for h in hs] for u in us]
        dqb = [[_dot(do_b[u][h], rb_ref[u, h].astype(BF16), NT) for h in hs] for u in us]
        for u in us:
            for h in hs:
                dv_ref[rows[u], _vs(h)] = _dot(sd[u][h].astype(BF16), do_b[u][h], TN)
                drf_ref[u, h] = _dot(qfw[u][:, _ks(h)], do_b[u][h], TN)
                drb_ref[u, h] = _dot(qbw[u][:, _ks(h)], do_b[u][h], TN)
        dsd = [[(dpm[u][h] * dc_ref[h]).astype(BF16) for h in hs] for u in us]
        for u in us:
            for h in hs:
                ks = _ks(h)
                dqs[rows[u], ks] = _dot(dsd[u][h], kb[u][:, ks]) + dqf[u][h] * qdf_ref[:, ks] + dqb[u][h] * qdb_ref[:, ks]
                dk_ref[rows[u], ks] = _dot(dsd[u][h], qb[u][:, ks], TN)
        for u in us:
            for h in hs:
                ks = _ks(h)
                e = dpm[u][h] * sd[u][h]
                lf = (_sum11(e * jnp.maximum(dif, 0.0))
                      + _sum11(jnp.sum(qf32[u][:, ks] * dqf[u][h], axis=-1, keepdims=True) * (ri + 1.0)))
                lb = (_sum11(e * jnp.maximum(-dif, 0.0))
                      + _sum11(jnp.sum(qb32[u][:, ks] * dqb[u][h], axis=-1, keepdims=True) * (C - ri)))
                dlg_ref[h:h + 1, :] += jnp.broadcast_to(lf, (1, 128))
                dlg_ref[HR + h:HR + h + 1, :] += jnp.broadcast_to(lb, (1, 128))
            for g in range(2):
                sl = slice(128 * g, 128 * g + 128)
                dpa_ref[rows[u], sl] = _rope_t(dqs[rows[u], sl], cos_ref[rows[u], :], sin_ref[rows[u], :]).astype(BF16)

    st = jax.ShapeDtypeStruct((N, HR, DH, DV), F32)
    stb = lambda: pl.BlockSpec((G, HR, DH, DV), lambda t: (t, 0, 0, 0))
    row = lambda w, off=0: pl.BlockSpec((G * C, w), lambda t: (t, off))
    return pl.pallas_call(
        body, name="ret_bwd_chunk",
        out_shape=[jax.ShapeDtypeStruct((S, 768), BF16), jax.ShapeDtypeStruct((S, 256), F32), jax.ShapeDtypeStruct((S, 512), F32), st, st,
                   jax.ShapeDtypeStruct((1, 512), F32), jax.ShapeDtypeStruct((8, 128), F32)],
        grid=(N // G,),
        in_specs=[row(256), row(256), row(512, O_VR // 512), row(512, O_ZR // 512),
                  stb(), stb(), _full((HR, C, C)), _full((C, 256)), _full((C, 256)), _full((1, 512)), row(512), row(128), row(128)],
        out_specs=[row(768), row(256), row(512), stb(), stb(), _full((1, 512)), _full((8, 128))],
        scratch_shapes=[pltpu.VMEM((G * C, 256), F32)],
        compiler_params=_cp(("arbitrary",)),
    )(qr2, kr2, p, p, rf, rb, dc, qdf, qdb, gn, dyr, cos, sin)


def _ret_bwd_scan(kr2, p, rf, rb, drf, drb, kdf, kdb, adec):
    S = kr2.shape[0]
    C, N = CH, S // CH
    G = _scan_group(N)
    NG = N // G

    def body(kf_ref, vf_ref, kb_ref, vb_ref, rf_ref, rb_ref, drf_ref, drb_ref, kdf_ref, kdb_ref, a_ref,
             dkf_ref, dkb_ref, dvf_ref, dvb_ref, dlg_ref, gf, gb):
        @pl.when(pl.program_id(0) == 0)
        def _():
            gf[...] = jnp.zeros_like(gf)
            gb[...] = jnp.zeros_like(gb)
            dlg_ref[...] = jnp.zeros_like(dlg_ref)

        ri = lax.broadcasted_iota(jnp.int32, (C, 1), 0).astype(F32)

        def one(k_ref, v_ref, r_ref, dr_ref, kd_ref, g_s, dk_ref, dv_ref, row0, wexp, order):
            g = [g_s[h] for h in range(HR)]
            lgs = [jnp.zeros((1, 1), F32) for _ in range(HR)]
            for u in order:
                rows = slice(C * u, C * u + C)
                kd32 = k_ref[rows, :] * kd_ref[...]
                kdw = kd32.astype(BF16)
                vb = v_ref[rows, :].astype(BF16)
                for h in range(HR):
                    ks, vs = _ks(h), _vs(h)
                    g_b = g[h].astype(BF16)
                    dkd = _dot(vb[:, vs], g_b, NT)
                    dk_ref[rows, ks] = dkd * kd_ref[:, ks]
                    dv_ref[rows, vs] = _dot(kdw[:, ks], g_b)
                    av = a_ref[row0 + h:row0 + h + 1, :]
                    lgs[h] = lgs[h] + (_sum11(jnp.sum(kd32[:, ks] * dkd, axis=-1, keepdims=True) * wexp)
                                       + C * av[:, 0:1] * _sum11(r_ref[u, h] * g[h]))
                    g[h] = dr_ref[u, h] + av * g[h]
            for h in range(HR):
                g_s[h] = g[h]
                dlg_ref[row0 + h:row0 + h + 1, :] += jnp.broadcast_to(lgs[h], (1, 128))

        one(kf_ref, vf_ref, rf_ref, drf_ref, kdf_ref, gf, dkf_ref, dvf_ref, 0, C - 1.0 - ri, list(reversed(range(G))))
        one(kb_ref, vb_ref, rb_ref, drb_ref, kdb_ref, gb, dkb_ref, dvb_ref, HR, ri, list(range(G)))

    fwd = lambda w, off=0: pl.BlockSpec((G * C, w), lambda t: (NG - 1 - t, off))
    bwd = lambda w, off=0: pl.BlockSpec((G * C, w), lambda t: (t, off))
    stf = lambda: pl.BlockSpec((G, HR, DH, DV), lambda t: (NG - 1 - t, 0, 0, 0))
    stb = lambda: pl.BlockSpec((G, HR, DH, DV), lambda t: (t, 0, 0, 0))
    return pl.pallas_call(
        body, name="ret_bwd_scan",
        out_shape=[jax.ShapeDtypeStruct((S, 256), F32), jax.ShapeDtypeStruct((S, 256), F32), jax.ShapeDtypeStruct((S, 512), F32),
                   jax.ShapeDtypeStruct((S, 512), F32), jax.ShapeDtypeStruct((8, 128), F32)],
        grid=(NG,),
        in_specs=[fwd(256), fwd(512, O_VR // 512), bwd(256), bwd(512, O_VR // 512), stf(), stb(), stf(), stb(),
                  _full((C, 256)), _full((C, 256)), _full((8, 128))],
        out_specs=[fwd(256), bwd(256), fwd(512), bwd(512), _full((8, 128))],
        scratch_shapes=[pltpu.VMEM((HR, DH, DV), F32), pltpu.VMEM((HR, DH, DV), F32)],
        compiler_params=_cp(("arbitrary",)),
    )(kr2, p, kr2, p, rf, rb, drf, drb, kdf, kdb, adec)


def _ret_bwd_final(dk_i, dkf, dkb, dv_i, dvf, dvb, cos, sin):
    S = dk_i.shape[0]
    tm = min(512, S)

    def body(a_ref, b_ref, c_ref, d_ref, e_ref, f_ref, cos_ref, sin_ref, o_ref):
        o_ref[:, :512] = (d_ref[...] + e_ref[...] + f_ref[...]).astype(BF16)
        cos_v, sin_v = cos_ref[...], sin_ref[...]
        for g in range(2):
            sl = slice(128 * g, 128 * g + 128)
            dk = a_ref[:, sl] + b_ref[:, sl] + c_ref[:, sl]
            o_ref[:, 512 + 128 * g:512 + 128 * g + 128] = (_rope_t(dk, cos_v, sin_v) * 0.125).astype(BF16)

    row = lambda w: pl.BlockSpec((tm, w), lambda i: (i, 0))
    return pl.pallas_call(
        body, name="ret_bwd_final", out_shape=jax.ShapeDtypeStruct((S, 768), BF16), grid=(S // tm,),
        in_specs=[row(256), row(256), row(256), row(512), row(512), row(512), row(128), row(128)], out_specs=row(768),
        compiler_params=_cp(("parallel",)),
    )(dk_i, dkf, dkb, dv_i, dvf, dvb, cos, sin)


def _bwd_in(dpm, dpa, dpra, dprb, w_p, x, dout, mod, g_pre):
    S = x.shape[0]
    tm = min(512, S)

    def body(a_ref, b_ref, c_ref, d_ref, w_ref, x_ref, dout_ref, mod_ref, g_ref, gx_ref, sums_ref):
        @pl.when(pl.program_id(0) == 0)
        def _():
            sums_ref[...] = jnp.zeros_like(sums_ref)

        dh = (_dot(a_ref[...], w_ref[:, :O_QA], NT) + _dot(b_ref[...], w_ref[:, O_QA:O_QR], NT)
              + _dot(c_ref[...], w_ref[:, O_QR:O_VR], NT) + _dot(d_ref[...], w_ref[:, O_VR:], NT))
        xv = x_ref[...]
        r = lax.rsqrt(jnp.mean(xv * xv, axis=-1, keepdims=True) + EPS)
        xn = xv * r
        gv = g_ref[...]
        sc1 = 1.0 + mod_ref[1:2, :]
        sums_ref[0:1, :] += jnp.sum(dh, axis=0, keepdims=True)
        sums_ref[1:2, :] += jnp.sum(dh * (xn * gv), axis=0, keepdims=True)
        sums_ref[2:3, :] += jnp.sum(dh * xn, axis=0, keepdims=True) * sc1
        dxn = dh * (gv * sc1)
        gx_ref[...] = dout_ref[...] + r * (dxn - xn * jnp.mean(dxn * xn, axis=-1, keepdims=True))

    row = lambda w: pl.BlockSpec((tm, w), lambda i: (i, 0))
    return pl.pallas_call(
        body, name="bwd_in", out_shape=[jax.ShapeDtypeStruct((S, D), F32), jax.ShapeDtypeStruct((8, D), F32)], grid=(S // tm,),
        in_specs=[row(2560), row(768), row(768), row(768), _full((D, P_W)), row(D), row(D), _full((3, D)), _full((1, D))],
        out_specs=[row(D), _full((8, D))],
        compiler_params=_cp(("arbitrary",), VMEM_BIG),
    )(dpm, dpa, dpra, dprb, w_p, x, dout, mod, g_pre)


SMALL = ("b_ada", "g_pre", "qn_g", "kn_g", "w_dec_f", "w_dec_b", "gn_g", "g_post")


def _small_update(gathered, wmv):
    ns = len(SMALL)

    def body(*refs):
        gin_ref, gmid_ref, ggn_ref, gatt_ref, gl1_ref, gl2_ref = refs[:6]
        wmv_refs = refs[6:6 + 3 * ns]
        loss_ref = refs[6 + 3 * ns]
        out_refs = refs[7 + 3 * ns:]

        def dsum(ref, r=None):
            rows = slice(None) if r is None else slice(r, r + 1)
            acc = ref[0, rows, :]
            for d in range(1, NDEV):
                acc = acc + ref[d, rows, :]
            return acc

        s_lg = dsum(gl1_ref) + dsum(gl2_ref)
        loss_ref[...] = (0.5 / D) * jnp.sum(dsum(gmid_ref, 2), axis=-1, keepdims=True)
        eye = lax.broadcasted_iota(jnp.int32, (8, 128), 0) == lax.broadcasted_iota(jnp.int32, (8, 128), 1)
        dlg = jnp.sum(jnp.where(eye, s_lg, 0.0), axis=0, keepdims=True)
        w_f, w_b = wmv_refs[3 * SMALL.index("w_dec_f")][...], wmv_refs[3 * SMALL.index("w_dec_b")][...]
        s_q, s_k = dsum(gatt_ref, 0), dsum(gatt_ref, 1)
        grads = dict(
            b_ada=jnp.concatenate([dsum(gin_ref, 0), dsum(gin_ref, 1), dsum(gmid_ref, 0)], axis=1),
            g_pre=dsum(gin_ref, 2), g_post=dsum(gmid_ref, 1), gn_g=dsum(ggn_ref),
            qn_g=s_q[:, :DH] + s_q[:, DH:], kn_g=s_k[:, :DH] + s_k[:, DH:],
            w_dec_f=dlg[:, 0:HR] * _sigmoid(-w_f), w_dec_b=dlg[:, HR:2 * HR] * _sigmoid(-w_b))
        for i, nme in enumerate(SMALL):
            g = grads[nme]
            w_ref, m_ref, v_ref = wmv_refs[3 * i:3 * i + 3]
            g_ref, d_ref, nm_ref, nv_ref = out_refs[4 * i:4 * i + 4]
            g_ref[...] = g
            m2 = ADAM_B1 * m_ref[...] + (1.0 - ADAM_B1) * g
            v2 = ADAM_B2 * v_ref[...] + (1.0 - ADAM_B2) * jnp.square(g)
            m_hat = m2 / (1.0 - ADAM_B1 ** ADAM_STEP)
            v_hat = v2 / (1.0 - ADAM_B2 ** ADAM_STEP)
            d_ref[...] = -ADAM_LR * (m_hat / (jnp.sqrt(v_hat) + ADAM_EPS) + ADAM_WD * w_ref[...])
            nm_ref[...] = m2
            nv_ref[...] = v2

    out_shape = [jax.ShapeDtypeStruct((1, 1), F32)]
    for i in range(ns):
        out_shape += [jax.ShapeDtypeStruct(wmv[3 * i].shape, F32)] * 4
    return pl.pallas_call(body, name="small_update", out_shape=out_shape)(*gathered, *wmv)


def _adamw(parts, w, m, v, name):
    n, R, L = parts.shape
    tr = 256 if (R % 256 == 0 and R > 256) else R

    def body(p_ref, w_ref, m_ref, v_ref, g_ref, d_ref, nm_ref, nv_ref):
        g = p_ref[0].astype(F32)
        for k in range(1, n):
            g = g + p_ref[k].astype(F32)
        g_ref[...] = g
        m2 = ADAM_B1 * m_ref[...] + (1.0 - ADAM_B1) * g
        v2 = ADAM_B2 * v_ref[...] + (1.0 - ADAM_B2) * jnp.square(g)
        m_hat = m2 / (1.0 - ADAM_B1 ** ADAM_STEP)
        v_hat = v2 / (1.0 - ADAM_B2 ** ADAM_STEP)
        d_ref[...] = -ADAM_LR * (m_hat / (jnp.sqrt(v_hat) + ADAM_EPS) + ADAM_WD * w_ref[...])
        nm_ref[...] = m2
        nv_ref[...] = v2

    blk = pl.BlockSpec((tr, L), lambda i: (i, 0))
    o = jax.ShapeDtypeStruct((R, L), F32)
    return pl.pallas_call(
        body, name=name, out_shape=[o, o, o, o], grid=(R // tr,),
        in_specs=[pl.BlockSpec((n, tr, L), lambda i: (0, i, 0)), blk, blk, blk], out_specs=[blk, blk, blk, blk],
        compiler_params=_cp(("parallel",), VMEM_BIG),
    )(parts, w, m, v)


def _rope_tables(S):
    f = np.float32
    t = np.arange(S)
    row, col = (t // 64).astype(f), (t % 64).astype(f)
    half = DH // 2
    inv = np.power(f(ROPE_THETA), -np.arange(0, half, 2, dtype=f) / f(half)).astype(f)
    ar, ac = (row[:, None] * inv[None, :]).astype(f), (col[:, None] * inv[None, :]).astype(f)
    cos64 = np.concatenate([np.cos(ar), np.cos(ar), np.cos(ac), np.cos(ac)], axis=1).astype(f)
    sin64 = np.concatenate([-np.sin(ar), np.sin(ar), -np.sin(ac), np.sin(ac)], axis=1).astype(f)
    return jnp.asarray(np.tile(cos64, (1, 2))), jnp.asarray(np.tile(sin64, (1, 2)))


def _to_p_order(w_orig):
    return jnp.concatenate([w_orig[:, ORIG[n][0]:ORIG[n][1]] for n in P_ORDER], axis=1)


def _pad_lanes(v, n):
    return jnp.pad(v, ((0, 0), (0, n - v.shape[1])))


def kernel(x, c, w_ada, b_ada, g_pre, w_in, qn_g, kn_g, w_dec_f, w_dec_b, gn_g, w_pa, w_pr, w_out, g_post, loss_target, m_w_ada, m_b_ada, m_g_pre, m_w_in, m_qn_g, m_kn_g, m_w_dec_f, m_w_dec_b, m_gn_g, m_w_pa, m_w_pr, m_w_out, m_g_post, v_w_ada, v_b_ada, v_g_pre, v_w_in, v_qn_g, v_kn_g, v_w_dec_f, v_w_dec_b, v_gn_g, v_w_pa, v_w_pr, v_w_out, v_g_post):
    S = x.shape[1]
    me = 4 * lax.axis_index("x") + 2 * lax.axis_index("y") + lax.axis_index("c")
    xs, tgt = x[0], loss_target[0]
    ncol_ada = w_ada.shape[2]
    ncol_in = w_in.shape[2]

    b_ada_s = lax.dynamic_slice(b_ada, (0, me * ncol_ada), (1, ncol_ada))
    mod_all, c_act, (wg_in,) = _prologue(jnp.pad(c, ((0, 7), (0, 0))), w_ada[0], b_ada_s, [w_in[0].astype(BF16)])
    mod = lax.dynamic_index_in_dim(mod_all, me, axis=1, keepdims=False).reshape(3, D)
    w_p = _to_p_order(wg_in.transpose(1, 0, 2).reshape(D, NDEV * ncol_in))
    all_dev = tuple(range(NDEV))
    st_w, tok_w = _xchg_start([(w_pa[0].astype(BF16)[None], all_dev), (w_pr[0].astype(BF16)[None], all_dev),
                               (w_out[0].astype(BF16)[None], all_dev)], "wgather_start")

    cos, sin = _rope_tables(S)
    qg, kg = jnp.tile(qn_g, (1, 2)), jnp.tile(kn_g, (1, 2))

    p, h = _fwd_in(xs, mod, g_pre + tok_w[0:1, 0:1], w_p)
    qt, kh, kt, vh, vta, qr2, kr2 = _prep(p, cos, sin, qg, kg)
    o_att, o_t, lse = _attn_fwd(qt, kh, vta)
    dc, qdf, qdb, kdf, kdb, adec = _ret_tables(w_dec_f, w_dec_b)
    rf, rb = _ret_states(kr2, p, kdf, kdb, adec)
    yr = _ret_out(qr2, kr2, p, rf, rb, dc, qdf, qdb, gn_g)
    wg_pa, wg_pr, wg_out = _xchg_wait([st_w], st_w["lands"], [[0, 1, 2]], yr, "wgather_wait")
    w_pa_f = wg_pa.transpose(1, 0, 2).reshape(512, D)
    w_pr_f = wg_pr.transpose(1, 0, 2).reshape(512, D)
    w_out_f = wg_out.reshape(D, D)

    dout, do, dpm, dyr, mb, dub, yab, dab, drb_, sums_mid = _mid(xs, tgt, mod, g_post, o_att, p, yr, w_pa_f, w_pr_f, w_out_f)
    gw_out = _mm_tn(mb, dub, "gw_out", BF16)
    gw_pa = _mm_tn(yab, dab, "gw_pa", BF16)
    gw_pr = _mm_tn(yr, drb_, "gw_pr", BF16)
    gi_m = _mm_tn(h, dpm, "gw_in_mid", BF16)

    def shards(cols, nd):
        return cols.reshape(D, nd, ncol_in).transpose(1, 0, 2)

    st_a, tok_a = _xchg_start([
        (gw_out.reshape(NDEV, 128, D), all_dev),
        (gw_pa.reshape(512, NDEV, 128).transpose(1, 0, 2), all_dev),
        (gw_pr.reshape(512, NDEV, 128).transpose(1, 0, 2), all_dev),
        (shards(gi_m[:, 224:2048], 3), (5, 6, 7))], "xchg_start_a",
        lands=[None, None, None, jnp.zeros((NDEV, D, ncol_in), BF16)])
    dqt, dkt, dvt = _attn_bwd(qt, kh, kt, vh, do, o_t, lse + tok_a[0, 0])
    dpa, gs_att = _attn_prep_bwd(dqt, dkt, dvt, p, cos, sin, qg, kg)
    gi_a = _mm_tn(h, dpa, "gw_in_att", BF16)
    st_b, tok_b = _xchg_start([(shards(jnp.concatenate([gi_a, gi_m[:, 2048:2496]], axis=1), 2), (0, 1))], "xchg_start_b",
                              lands=[st_a["lands"][3]])
    dpra, dk_i, dv_i, drf, drb, dgn, dlg1 = _ret_bwd_chunk(qr2, kr2, p, rf, rb, dc, qdf, qdb, gn_g + tok_b[0:1, 0:1], dyr, cos, sin)
    dkf, dkb, dvf, dvb, dlg2 = _ret_bwd_scan(kr2, p, rf, rb, drf, drb, kdf, kdb, adec)
    dprb = _ret_bwd_final(dk_i, dkf, dkb, dv_i, dvf, dvb, cos, sin)
    gi_ra = _mm_tn(h, dpra, "gw_in_reta", BF16)
    gi_rb = _mm_tn(h, dprb, "gw_in_retb", BF16)
    chip_c = _pair_reduce(shards(jnp.concatenate([gi_m[:, 2496:2560], gi_ra[:, :256], gi_rb[:, 512:768], gi_rb[:, :512],
                                                  gi_ra[:, 256:768], gi_m[:, :224]], axis=1), 3), (2, 3, 4), "pair_reduce_c")
    st_c, tok_c = _xchg_start([(chip_c, (2, 3, 4, "same core"))], "xchg_start_c", lands=[st_b["lands"][0]])
    grad_x, sums_in = _bwd_in(dpm, dpa, dpra, dprb, w_p, xs, dout, mod, g_pre + tok_c[0:1, 0:1])

    gathered = _small_allgather([sums_in, sums_mid, dgn, gs_att, dlg1, dlg2], "ag_small")
    given = dict(b_ada=(b_ada, m_b_ada, v_b_ada), g_pre=(g_pre, m_g_pre, v_g_pre), qn_g=(qn_g, m_qn_g, v_qn_g), kn_g=(kn_g, m_kn_g, v_kn_g),
                 w_dec_f=(w_dec_f, m_w_dec_f, v_w_dec_f), w_dec_b=(w_dec_b, m_w_dec_b, v_w_dec_b), gn_g=(gn_g, m_gn_g, v_gn_g),
                 g_post=(g_post, m_g_post, v_g_post))
    small = _small_update(gathered, [a for nme in SMALL for a in given[nme]])
    loss = small[0][0, 0]

    g_in_all, g_mid_all = gathered[0], gathered[1]
    dmod_all = lax.dynamic_slice(jnp.concatenate([g_in_all[:, 0, :], g_in_all[:, 1, :], g_mid_all[:, 0, :]], axis=1),
                                 (0, me * ncol_ada), (NDEV, ncol_ada))
    g_ada = _mm_tn(c_act, jnp.pad(dmod_all, ((0, 8), (0, 0))).astype(BF16), "gw_ada")

    ada = _adamw(g_ada[None], w_ada[0], m_w_ada[0], v_w_ada[0], "adamw_ada")
    rs_out, rs_pa, rs_pr, rs_in = _xchg_wait([st_a, st_b, st_c], list(st_a["lands"][:3]) + [st_c["lands"][0]],
                                             [[0, 1, 2, 3], [3], [3]], ada[1], "xchg_wait")
    res = dict(
        w_ada=ada,
        w_in=_adamw(rs_in, w_in[0], m_w_in[0], v_w_in[0], "adamw_in"),
        w_pa=_adamw(rs_pa, w_pa[0], m_w_pa[0], v_w_pa[0], "adamw_pa"),
        w_pr=_adamw(rs_pr, w_pr[0], m_w_pr[0], v_w_pr[0], "adamw_pr"),
        w_out=_adamw(rs_out, w_out[0], m_w_out[0], v_w_out[0], "adamw_out"),
    )
    names = ["w_ada", "b_ada", "g_pre", "w_in", "qn_g", "kn_g", "w_dec_f", "w_dec_b", "gn_g", "w_pa", "w_pr", "w_out", "g_post"]
    outs = [[], [], [], []]
    for nme in names:
        for q in range(4):
            if nme in res:
                outs[q].append(res[nme][q][None])
            else:
                outs[q].append(small[1 + 4 * SMALL.index(nme) + q])
    return (loss, grad_x[None], *outs[0], *outs[1], *outs[2], *outs[3])
```

```python
import jax
import jax.numpy as jnp
import numpy as np
from jax import lax
from jax.experimental import pallas as pl
from jax.experimental.pallas import tpu as pltpu

F32, BF16 = jnp.float32, jnp.bfloat16
D = 1024
DH = 64
DHA = 80
DV = 128
LOG2E = 1.4426950408889634
LN2 = 0.6931471805599453
HR = 4
CH = 128
EPS = 1e-6
ROPE_THETA = 10000.0
NDEV = 8
O_GL, O_ZA, O_QA, O_KA, O_VA, O_QR, O_ZR, O_VR, O_KR, P_W = 0, 2048, 2560, 3072, 3200, 3328, 3584, 4096, 4608, 4864
ORIG = dict(qa=(0, 512), ka=(512, 640), va=(640, 768), za=(768, 1280), qr=(1280, 1536), kr=(1536, 1792),
            vr=(1792, 2304), zr=(2304, 2816), gl=(2816, 4864))
P_ORDER = ("gl", "za", "qa", "ka", "va", "qr", "zr", "vr", "kr")
ADAM_LR, ADAM_B1, ADAM_B2, ADAM_EPS, ADAM_WD, ADAM_STEP = 0.001, 0.9, 0.999, 1e-08, 0.01, 10
VMEM_BIG = 56 * 1024 * 1024
MESH = pl.DeviceIdType.MESH

NT = (((1,), (1,)), ((), ()))
TN = (((0,), (0,)), ((), ()))


def _dot(a, b, dims=None):
    if dims is None:
        return jnp.dot(a, b, preferred_element_type=F32)
    return lax.dot_general(a, b, dims, preferred_element_type=F32)


def _cp(sem=None, vmem=None):
    kw = {}
    if sem is not None:
        kw["dimension_semantics"] = sem
    if vmem is not None:
        kw["vmem_limit_bytes"] = vmem
    return pltpu.CompilerParams(**kw)


def _sigmoid(z):
    return 1.0 / (1.0 + jnp.exp(-z))


def _sum11(m):
    return jnp.sum(jnp.sum(m, axis=-1, keepdims=True), axis=0, keepdims=True)


def _full(shape):
    n = len(shape)
    return pl.BlockSpec(shape, lambda *_: (0,) * n)


def _my_pos():
    return lax.axis_index("x"), lax.axis_index("y"), lax.axis_index("c")


def _peer(k, x, y, c):
    return ((1 - x) if k & 4 else x, (1 - y) if k & 2 else y, (1 - c) if k & 1 else c)


def _small_allgather(vs, name):
    n = len(vs)

    def body(*refs):
        v_refs, out_refs = refs[:n], refs[n:2 * n]
        send_sems, recv_sems = refs[2 * n:]
        x, y, c = _my_pos()
        me = 4 * x + 2 * y + c
        cps = []
        for a in range(n):
            out_refs[a][me] = v_refs[a][...]
            for k in range(1, NDEV):
                cp = pltpu.make_async_remote_copy(src_ref=v_refs[a], dst_ref=out_refs[a].at[me], send_sem=send_sems.at[a, k - 1],
                                                  recv_sem=recv_sems.at[a, k - 1], device_id=_peer(k, x, y, c), device_id_type=MESH)
                cp.start()
                cps.append(cp)
        for cp in cps:
            cp.wait()

    vm = pl.BlockSpec(memory_space=pltpu.VMEM)
    return pl.pallas_call(
        body, name=name, out_shape=[jax.ShapeDtypeStruct((NDEV,) + v.shape, v.dtype) for v in vs],
        in_specs=[vm] * n, out_specs=[vm] * n,
        scratch_shapes=[pltpu.SemaphoreType.DMA((n, NDEV - 1)), pltpu.SemaphoreType.DMA((n, NDEV - 1))],
    )(*vs)


def _prologue(c8, w_ada_s, b_ada_s, arrs):
    n = len(arrs)
    ncol = w_ada_s.shape[1]

    def body(*refs):
        c_ref, wa_ref, ba_ref = refs[:3]
        ins = refs[3:3 + n]
        mod_ref, cact_ref = refs[3 + n:5 + n]
        outs = refs[5 + n:5 + 2 * n]
        call_ref, send_sems, recv_sems, local_sems, s_send, s_recv = refs[5 + 2 * n:]
        x, y, c = _my_pos()
        me, sibling = (x, y, c), (x, y, 1 - c)
        chips = [(1 - x, y), (x, 1 - y), (1 - x, 1 - y)]
        me_i = 4 * x + 2 * y + c

        def small_gather(src_ref, dst_ref, row):
            cps = []
            for k in range(1, NDEV):
                cp = pltpu.make_async_remote_copy(src_ref=src_ref, dst_ref=dst_ref.at[me_i], send_sem=s_send.at[row, k - 1],
                                                  recv_sem=s_recv.at[row, k - 1], device_id=_peer(k, x, y, c), device_id_type=MESH)
                cp.start()
                cps.append(cp)
            return cps

        def blk(a, px, py, pc):
            return outs[a].at[4 * px + 2 * py + pc]

        def copy(a, k, block, to, src=None):
            return pltpu.make_async_remote_copy(src_ref=blk(a, *block) if src is None else src, dst_ref=blk(a, *block),
                                                send_sem=send_sems.at[a, k], recv_sem=recv_sems.at[a, k], device_id=to, device_id_type=MESH)

        call_ref[me_i] = c_ref[...]
        for cp in small_gather(c_ref, call_ref, 0):
            cp.wait()

        local, sent = [], []
        for a in range(n):
            mine = pltpu.make_async_copy(ins[a], blk(a, *me), local_sems.at[a])
            mine.start()
            local.append(mine)
            first = [copy(a, 0, me, sibling, src=ins[a])] + [copy(a, 1 + j, me, (*chip, c), src=ins[a]) for j, chip in enumerate(chips)]
            for cp in first:
                cp.start()
            sent += first

        cv = call_ref[:, 0, :]
        ca = jnp.concatenate([cv * _sigmoid(cv), jnp.zeros_like(cv)], axis=0).astype(BF16)
        cact_ref[...] = ca
        mod_ref[me_i] = (_dot(ca, wa_ref[...].astype(BF16)) + ba_ref[...])[:8]
        mod_copies = small_gather(mod_ref.at[me_i], mod_ref, 1)

        for j, chip in enumerate(chips):
            for a in range(n):
                copy(a, 1 + j, (*chip, c), me).wait_recv()
                cp = copy(a, 4 + j, (*chip, c), sibling)
                cp.start()
                sent.append(cp)
        for a in range(n):
            copy(a, 0, sibling, me).wait_recv()
            for j, chip in enumerate(chips):
                copy(a, 4 + j, (*chip, 1 - c), me).wait_recv()
        for cp in sent:
            cp.wait_send()
        for cp in local + mod_copies:
            cp.wait()

    vm, hbm = pl.BlockSpec(memory_space=pltpu.VMEM), pl.BlockSpec(memory_space=pl.ANY)
    res = pl.pallas_call(
        body, name="prologue",
        out_shape=[jax.ShapeDtypeStruct((NDEV, 8, ncol), F32), jax.ShapeDtypeStruct((16, D), BF16)]
        + [jax.ShapeDtypeStruct((NDEV,) + a.shape, a.dtype) for a in arrs],
        in_specs=[vm, vm, vm] + [hbm] * n, out_specs=[vm, vm] + [hbm] * n,
        scratch_shapes=[pltpu.VMEM((NDEV, 8, D), F32), pltpu.SemaphoreType.DMA((n, NDEV - 1)), pltpu.SemaphoreType.DMA((n, NDEV - 1)),
                        pltpu.SemaphoreType.DMA((n,)), pltpu.SemaphoreType.DMA((2, NDEV - 1)), pltpu.SemaphoreType.DMA((2, NDEV - 1))],
    )(c8, w_ada_s, b_ada_s, *arrs)
    return res[0], res[1], res[2:]


def _in_set(idx, dests):
    p = idx == dests[0]
    for d in dests[1:]:
        p = jnp.logical_or(p, idx == d)
    return p


_HBM = pl.BlockSpec(memory_space=pltpu.HBM)
_SEM = pl.BlockSpec(memory_space=pltpu.SEMAPHORE)


def _pair_reduce(send, dests, name):
    nd = send.shape[0]

    def body(s_ref, o_ref, land, ssem, rsem):
        x, y, c = _my_pos()
        cps = []
        for i in range(nd):
            cp = pltpu.make_async_remote_copy(src_ref=s_ref.at[i], dst_ref=land.at[i], send_sem=ssem.at[i], recv_sem=rsem.at[i],
                                              device_id=(x, y, 1 - c), device_id_type=MESH)
            pl.when(c != (dests[i] & 1))(cp.start)
            cps.append(cp)
        for i in range(nd):
            mine = c == (dests[i] & 1)

            @pl.when(mine)
            def _():
                cps[i].wait_recv()
                o_ref[i] = (s_ref[i].astype(F32) + land[i].astype(F32)).astype(BF16)

            pl.when(jnp.logical_not(mine))(cps[i].wait_send)

    vm = pl.BlockSpec(memory_space=pltpu.VMEM)
    return pl.pallas_call(
        body, name=name, out_shape=jax.ShapeDtypeStruct(send.shape, send.dtype), in_specs=[vm], out_specs=vm,
        scratch_shapes=[pltpu.VMEM(send.shape, send.dtype), pltpu.SemaphoreType.DMA((nd,)), pltpu.SemaphoreType.DMA((nd,))],
        compiler_params=_cp(None, VMEM_BIG),
    )(send)


def _xchg_copies(xs_dests, sends, lands, ssem, rsem, lsem):
    x, y, c = _my_pos()
    me = 4 * x + 2 * y + c
    remote, local = [], []
    for a, dests in enumerate(xs_dests):
        same_core = dests[-1] == "same core"
        dests = dests[:-1] if same_core else dests
        lo, nd = dests[0], sends[a].shape[0]
        for k in range(1, NDEV):
            if same_core and k & 1:
                continue
            px, py, pc = _peer(k, x, y, c)
            pidx = 4 * px + 2 * py + pc
            cp = pltpu.make_async_remote_copy(src_ref=sends[a].at[jnp.clip(pidx - lo, 0, nd - 1)], dst_ref=lands[a].at[me],
                                              send_sem=ssem.at[a * (NDEV - 1) + k - 1], recv_sem=rsem.at[a * (NDEV - 1) + k - 1],
                                              device_id=(px, py, pc), device_id_type=MESH)
            remote.append((cp, _in_set(pidx, dests), _in_set(me, dests)))
        lc = pltpu.make_async_copy(sends[a].at[jnp.clip(me - lo, 0, nd - 1)], lands[a].at[me], lsem.at[a])
        local.append((lc, _in_set(me, dests)))
    return remote, local


def _xchg_start(xs, name, lands=None):
    n = len(xs)
    dests = [d for _, d in xs]
    sends = [pltpu.with_memory_space_constraint(s, pltpu.HBM) for s, _ in xs]
    lands = [None] * n if lands is None else lands
    lands = [pltpu.with_memory_space_constraint(lax.empty((NDEV,) + s.shape[1:], s.dtype) if l is None else l, pltpu.HBM)
             for (s, _), l in zip(xs, lands)]

    def body(*refs):
        send_refs, land_refs = refs[:n], refs[n:2 * n]
        ssem, rsem, lsem = refs[2 * n:2 * n + 3]
        token = refs[-1]
        remote, local = _xchg_copies(dests, send_refs, land_refs, ssem, rsem, lsem)
        for cp, to_dest, _ in remote:
            pl.when(to_dest)(cp.start)
        for lc, i_am_dest in local:
            pl.when(i_am_dest)(lc.start)
        token[...] = jnp.zeros_like(token)

    res = pl.pallas_call(
        body, name=name,
        out_shape=[pltpu.SemaphoreType.DMA((n * (NDEV - 1),)), pltpu.SemaphoreType.DMA((n * (NDEV - 1),)), pltpu.SemaphoreType.DMA((n,))]
        + [pltpu.HBM(a.shape, a.dtype) for a in list(sends) + list(lands)] + [jax.ShapeDtypeStruct((8, 128), F32)],
        in_specs=[_HBM] * (2 * n), out_specs=[_SEM, _SEM, _SEM] + [_HBM] * (2 * n) + [pl.BlockSpec(memory_space=pltpu.VMEM)],
        input_output_aliases={i: 3 + i for i in range(2 * n)},
        compiler_params=pltpu.CompilerParams(has_side_effects=pltpu.SideEffectType.DATAFLOW_SIDE_EFFECTING),
    )(*sends, *lands)
    return dict(sems=res[0:3], sends=res[3:3 + n], lands=res[3 + n:3 + 2 * n], dests=dests), res[-1]


def _xchg_wait(states, lands, land_of, after, name):
    flat = []
    for st in states:
        flat += list(st["sends"]) + list(st["sems"])
    nl = len(lands)

    def body(*refs):
        land_refs = refs[:nl]
        pos = nl
        for s, st in enumerate(states):
            n = len(st["dests"])
            send_refs = refs[pos:pos + n]
            ssem, rsem, lsem = refs[pos + n:pos + n + 3]
            pos += n + 3
            remote, local = _xchg_copies(st["dests"], send_refs, [land_refs[i] for i in land_of[s]], ssem, rsem, lsem)
            for cp, to_dest, i_am_dest in remote:
                pl.when(to_dest)(cp.wait_send)
                pl.when(i_am_dest)(cp.wait_recv)
            for lc, i_am_dest in local:
                pl.when(i_am_dest)(lc.wait)

    in_specs = [_HBM] * nl
    for st in states:
        in_specs += [_HBM] * len(st["dests"]) + [_SEM, _SEM, _SEM]
    return pl.pallas_call(
        body, name=name, out_shape=[pltpu.HBM(a.shape, a.dtype) for a in lands],
        in_specs=in_specs + [pl.BlockSpec(memory_space=pl.ANY)], out_specs=[_HBM] * nl,
        input_output_aliases={i: i for i in range(nl)},
        compiler_params=pltpu.CompilerParams(has_side_effects=pltpu.SideEffectType.DATAFLOW_SIDE_EFFECTING),
    )(*lands, *flat, after)


def _mm_tn(a, b, name, out_dtype=F32):
    S, M = a.shape
    N = b.shape[1]
    tn = N if N <= 768 else (640 if N % 640 == 0 else 512)
    tk = min(4096 if N > tn else 2048, S)
    nk = S // tk

    def body(a_ref, b_ref, o_ref, acc):
        k = pl.program_id(1)

        @pl.when(k == 0)
        def _():
            acc[...] = _dot(a_ref[...], b_ref[...], TN)

        @pl.when(k > 0)
        def _():
            acc[...] += _dot(a_ref[...], b_ref[...], TN)

        @pl.when(k == nk - 1)
        def _():
            o_ref[...] = acc[...].astype(out_dtype)

    return pl.pallas_call(
        body, name=name, out_shape=jax.ShapeDtypeStruct((M, N), out_dtype), grid=(N // tn, nk),
        in_specs=[pl.BlockSpec((tk, M), lambda j, k: (k, 0)), pl.BlockSpec((tk, tn), lambda j, k: (k, j))],
        out_specs=pl.BlockSpec((M, tn), lambda j, k: (0, j)), scratch_shapes=[pltpu.VMEM((M, tn), F32)],
        compiler_params=_cp(("parallel", "arbitrary"), VMEM_BIG),
    )(a, b)


def _fwd_in(x, mod, g_pre, w_p):
    S = x.shape[0]
    tm = min(512, S)

    def body(x_ref, mod_ref, g_ref, w_ref, p_ref, h_ref):
        xv = x_ref[...]
        r = lax.rsqrt(jnp.mean(xv * xv, axis=-1, keepdims=True) + EPS)
        h = (((xv * r) * g_ref[...]) * (1.0 + mod_ref[1:2, :]) + mod_ref[0:1, :]).astype(BF16)
        h_ref[...] = h
        p_ref[...] = _dot(h, w_ref[...]).astype(BF16)

    return pl.pallas_call(
        body, name="fwd_in", out_shape=[jax.ShapeDtypeStruct((S, P_W), BF16), jax.ShapeDtypeStruct((S, D), BF16)],
        grid=(S // tm,),
        in_specs=[pl.BlockSpec((tm, D), lambda i: (i, 0)), _full((3, D)), _full((1, D)), _full((D, P_W))],
        out_specs=[pl.BlockSpec((tm, P_W), lambda i: (i, 0)), pl.BlockSpec((tm, D), lambda i: (i, 0))],
        compiler_params=_cp(("parallel",), VMEM_BIG),
    )(x, mod, g_pre, w_p)


def _swap16(v):
    lane = lax.broadcasted_iota(jnp.int32, v.shape, 1)
    return jnp.where((lane % 32) < 16, pltpu.roll(v, 112, 1), pltpu.roll(v, 16, 1))


def _rope(v, cos, sin):
    return v * cos + _swap16(v) * sin


def _rope_t(v, cos, sin):
    return v * cos - _swap16(v) * sin


def _head_mean(v):
    lo = lax.broadcasted_iota(jnp.int32, v.shape, 1) < 64
    m0 = jnp.sum(jnp.where(lo, v, 0.0), axis=-1, keepdims=True)
    m1 = jnp.sum(jnp.where(lo, 0.0, v), axis=-1, keepdims=True)
    return jnp.where(lo, m0, m1) * (1.0 / 64.0)


def _prep(p, cos, sin, qg, kg):
    S = p.shape[0]
    tm = min(512, S)

    def body(qa_ref, kv_ref, qr_ref, kr_ref, cos_ref, sin_ref, qg_ref, kg_ref, qt_ref, kh_ref, kt_ref, vh_ref, vta_ref, qr2_ref, kr2_ref):
        cos_v, sin_v = cos_ref[...], sin_ref[...]
        for g in range(4):
            xv = qa_ref[:, 128 * g:128 * g + 128].astype(F32)
            r = lax.rsqrt(_head_mean(xv * xv) + EPS)
            yt = (_rope((xv * r) * qg_ref[...], cos_v, sin_v) * (0.125 * LOG2E)).T
            qt_ref[2 * g] = yt[:DH].astype(BF16)
            qt_ref[2 * g + 1] = yt[DH:].astype(BF16)
        xv = kv_ref[:, :128].astype(F32)
        r = lax.rsqrt(_head_mean(xv * xv) + EPS)
        yv = _rope((xv * r) * kg_ref[...], cos_v, sin_v)
        kh_ref[0] = yv[:, :64].astype(BF16)
        kh_ref[1] = yv[:, 64:].astype(BF16)
        yt = yv.T
        kt_ref[0] = yt[:DH].astype(BF16)
        kt_ref[1] = yt[DH:].astype(BF16)
        vv = kv_ref[:, 128:].astype(F32)
        vh_ref[0] = vv[:, :64].astype(BF16)
        vh_ref[1] = vv[:, 64:].astype(BF16)
        vt = vv.T
        tail = (lax.broadcasted_iota(jnp.int32, (DHA - DH, tm), 0) == 0).astype(BF16)
        for kvh in range(2):
            vta_ref[kvh, 0:DH, :] = vt[DH * kvh:DH * kvh + DH].astype(BF16)
            vta_ref[kvh, DH:DHA, :] = tail
        for g in range(2):
            sl = slice(128 * g, 128 * g + 128)
            qr2_ref[:, sl] = _rope(qr_ref[:, sl].astype(F32), cos_v, sin_v)
            kr2_ref[:, sl] = _rope(kr_ref[:, sl].astype(F32), cos_v, sin_v) * 0.125

    hm = lambda n: pl.BlockSpec((n, tm, DH), lambda i: (0, i, 0))
    ht = lambda n, r: pl.BlockSpec((n, r, tm), lambda i: (0, 0, i))
    return pl.pallas_call(
        body, name="prep",
        out_shape=[jax.ShapeDtypeStruct((8, DH, S), BF16), jax.ShapeDtypeStruct((2, S, DH), BF16), jax.ShapeDtypeStruct((2, DH, S), BF16),
                   jax.ShapeDtypeStruct((2, S, DH), BF16), jax.ShapeDtypeStruct((2, DHA, S), BF16),
                   jax.ShapeDtypeStruct((S, 256), F32), jax.ShapeDtypeStruct((S, 256), F32)],
        grid=(S // tm,),
        in_specs=[pl.BlockSpec((tm, 512), lambda i: (i, O_QA // 512)), pl.BlockSpec((tm, 256), lambda i: (i, O_KA // 256)),
                  pl.BlockSpec((tm, 256), lambda i: (i, O_QR // 256)), pl.BlockSpec((tm, 256), lambda i: (i, O_KR // 256)),
                  pl.BlockSpec((tm, 128), lambda i: (i, 0)), pl.BlockSpec((tm, 128), lambda i: (i, 0)), _full((1, 128)), _full((1, 128))],
        out_specs=[ht(8, DH), hm(2), ht(2, DH), hm(2), ht(2, DHA), pl.BlockSpec((tm, 256), lambda i: (i, 0)), pl.BlockSpec((tm, 256), lambda i: (i, 0))],
        compiler_params=_cp(("parallel",)),
    )(p, p, p, p, cos, sin, qg, kg)


def _attn_fwd(qt, kh, vta):
    S = qt.shape[2]
    tq, tk = min(1024, S), min(512, S)
    nj = S // tk

    def body(q_ref, k_ref, v_ref, o_ref, ot_ref, lse_ref, m_s, acc_s):
        m_s[...] = jnp.full_like(m_s, -jnp.inf)
        acc_s[...] = jnp.zeros_like(acc_s)

        def key_block(j, carry):
            rows = pl.ds(pl.multiple_of(j * tk, tk), tk)
            m_all = m_s[...]
            st = {0: _dot(k_ref[0, rows, :], q_ref[0])}
            m_new, acc_new = [], []
            for h in range(8):
                if h + 1 < 8:
                    st[h + 1] = _dot(k_ref[(h + 1) // 4, rows, :], q_ref[h + 1])
                m_old = m_all[h:h + 1, :]
                mn = jnp.maximum(m_old, jnp.max(st[h], axis=0, keepdims=True))
                pt = jnp.exp2(st[h] - mn).astype(BF16)
                acc_new.append(jnp.exp2(m_old - mn) * acc_s[h] + _dot(v_ref[h // 4, :, rows], pt))
                m_new.append(mn)
                del st[h]
            for h in range(8):
                acc_s[h] = acc_new[h]
                m_s[h:h + 1, :] = m_new[h]
            return carry

        lax.fori_loop(0, nj, key_block, 0)
        for h in range(8):
            ot = acc_s[h, 0:DH, :] / acc_s[h, DH:DH + 1, :]
            ot_ref[h] = ot
            o_ref[:, DH * h:DH * h + DH] = ot.T
            lse_ref[h // 4, h % 4:h % 4 + 1, :] = m_s[h:h + 1, :] + jnp.log2(acc_s[h, DH:DH + 1, :])

    return pl.pallas_call(
        body, name="attn_fwd",
        out_shape=[jax.ShapeDtypeStruct((S, 512), F32), jax.ShapeDtypeStruct((8, DH, S), F32), jax.ShapeDtypeStruct((2, 4, S), F32)],
        grid=(S // tq,),
        in_specs=[pl.BlockSpec((8, DH, tq), lambda i: (0, 0, i)), _full((2, S, DH)), _full((2, DHA, S))],
        out_specs=[pl.BlockSpec((tq, 512), lambda i: (i, 0)), pl.BlockSpec((8, DH, tq), lambda i: (0, 0, i)),
                   pl.BlockSpec((2, 4, tq), lambda i: (0, 0, i))],
        scratch_shapes=[pltpu.VMEM((8, tq), F32), pltpu.VMEM((8, DHA, tq), F32)],
        compiler_params=_cp(("parallel",), VMEM_BIG),
    )(qt, kh, vta)


def _ret_tables(wf, wb):
    C = CH

    def body(wf_ref, wb_ref, dc_ref, qdf_ref, qdb_ref, kdf_ref, kdb_ref, a_ref):
        def logsig(w):
            z = jnp.exp(-jnp.abs(w))
            u = 1.0 + z
            l1p = jnp.where(u == 1.0, z, jnp.log(u) * (z / jnp.where(u == 1.0, 1.0, u - 1.0)))
            return jnp.minimum(w, 0.0) - l1p

        lgf, lgb = logsig(wf_ref[...]), logsig(wb_ref[...])
        lane4 = lax.broadcasted_iota(jnp.int32, (1, 4), 1)

        def pick(lg, h):
            return jnp.sum(jnp.where(lane4 == h, lg, 0.0), axis=-1, keepdims=True)

        ii = lax.broadcasted_iota(jnp.int32, (C, C), 0).astype(F32)
        jj = lax.broadcasted_iota(jnp.int32, (C, C), 1).astype(F32)
        dif = ii - jj
        hd = lax.broadcasted_iota(jnp.int32, (C, 256), 1) // DH
        lf_l = jnp.zeros((C, 256), F32)
        lb_l = jnp.zeros((C, 256), F32)
        for h in range(HR):
            lf, lb = pick(lgf, h), pick(lgb, h)
            dc_ref[h] = jnp.where(dif >= 0, jnp.exp(lf * jnp.maximum(dif, 0.0)), jnp.exp(lb * jnp.maximum(-dif, 0.0)))
            lf_l = jnp.where(hd == h, lf, lf_l)
            lb_l = jnp.where(hd == h, lb, lb_l)
            a_ref[h:h + 1, :] = jnp.broadcast_to(jnp.exp(lf * C), (1, 128))
            a_ref[HR + h:HR + h + 1, :] = jnp.broadcast_to(jnp.exp(lb * C), (1, 128))
        ri = lax.broadcasted_iota(jnp.int32, (C, 256), 0).astype(F32)
        qdf_ref[...] = jnp.exp(lf_l * (ri + 1.0))
        qdb_ref[...] = jnp.exp(lb_l * (C - ri))
        kdf_ref[...] = jnp.exp(lf_l * (C - 1.0 - ri))
        kdb_ref[...] = jnp.exp(lb_l * ri)

    t = jax.ShapeDtypeStruct((C, 256), F32)
    return pl.pallas_call(body, name="ret_tables",
                          out_shape=[jax.ShapeDtypeStruct((HR, C, C), F32), t, t, t, t, jax.ShapeDtypeStruct((8, 128), F32)])(wf, wb)


def _ret_states(kr2, p, kdf, kdb, adec):
    S = kr2.shape[0]
    C, N = CH, S // CH
    G = _scan_group(N)
    NG = N // G

    def body(kf_ref, vf_ref, kb_ref, vb_ref, kdf_ref, kdb_ref, a_ref, rf_ref, rb_ref, sf, sb):
        @pl.when(pl.program_id(0) == 0)
        def _():
            sf[...] = jnp.zeros_like(sf)
            sb[...] = jnp.zeros_like(sb)

        kvf, kvb = [], []
        for u in range(G):
            rows = slice(C * u, C * u + C)
            kdfw = (kf_ref[rows, :] * kdf_ref[...]).astype(BF16)
            kdbw = (kb_ref[rows, :] * kdb_ref[...]).astype(BF16)
            vf, vb = vf_ref[rows, :].astype(BF16), vb_ref[rows, :].astype(BF16)
            kvf.append([_dot(kdfw[:, _ks(h)], vf[:, _vs(h)], TN) for h in range(HR)])
            kvb.append([_dot(kdbw[:, _ks(h)], vb[:, _vs(h)], TN) for h in range(HR)])
        for u in range(G):
            rf_ref[u] = sf[...]
            for h in range(HR):
                sf[h] = a_ref[h:h + 1, :] * sf[h] + kvf[u][h]
        for u in reversed(range(G)):
            rb_ref[u] = sb[...]
            for h in range(HR):
                sb[h] = a_ref[HR + h:HR + h + 1, :] * sb[h] + kvb[u][h]

    st = jax.ShapeDtypeStruct((N, HR, DH, DV), F32)
    return pl.pallas_call(
        body, name="ret_states", out_shape=[st, st], grid=(NG,),
        in_specs=[pl.BlockSpec((G * C, 256), lambda t: (t, 0)), pl.BlockSpec((G * C, 512), lambda t: (t, O_VR // 512)),
                  pl.BlockSpec((G * C, 256), lambda t: (NG - 1 - t, 0)), pl.BlockSpec((G * C, 512), lambda t: (NG - 1 - t, O_VR // 512)),
                  _full((C, 256)), _full((C, 256)), _full((8, 128))],
        out_specs=[pl.BlockSpec((G, HR, DH, DV), lambda t: (t, 0, 0, 0)), pl.BlockSpec((G, HR, DH, DV), lambda t: (NG - 1 - t, 0, 0, 0))],
        scratch_shapes=[pltpu.VMEM((HR, DH, DV), F32), pltpu.VMEM((HR, DH, DV), F32)],
        compiler_params=_cp(("arbitrary",)),
    )(kr2, p, kr2, p, kdf, kdb, adec)


def _scan_group(n):
    return 4 if n % 4 == 0 else (2 if n % 2 == 0 else 1)


def _ks(h):
    return slice(DH * h, DH * h + DH)


def _vs(h):
    return slice(DV * h, DV * h + DV)


def _ret_heads_fwd(qb, kb, vb, qfw, qbw, dc_ref, rf_ref, rb_ref, u=0):
    hs = range(HR)
    s = [_dot(qb[:, _ks(h)], kb[:, _ks(h)], NT) for h in hs]
    inter = [_dot(qfw[:, _ks(h)], rf_ref[u, h].astype(BF16)) + _dot(qbw[:, _ks(h)], rb_ref[u, h].astype(BF16)) for h in hs]
    sd = [s[h] * dc_ref[h] for h in hs]
    o = [_dot(sd[h].astype(BF16), vb[:, _vs(h)]) + inter[h] for h in hs]
    return sd, o


def _ret_out(qr2, kr2, p, rf, rb, dc, qdf, qdb, gn):
    S = qr2.shape[0]
    C, N = CH, S // CH
    G = _scan_group(N)

    def body(q_ref, k_ref, v_ref, z_ref, rf_ref, rb_ref, dc_ref, qdf_ref, qdb_ref, gn_ref, yr_ref):
        outs = []
        for u in range(G):
            rows = slice(C * u, C * u + C)
            qv = q_ref[rows, :]
            qb, kb, vb = qv.astype(BF16), k_ref[rows, :].astype(BF16), v_ref[rows, :].astype(BF16)
            qfw, qbw = (qv * qdf_ref[...]).astype(BF16), (qv * qdb_ref[...]).astype(BF16)
            outs.append(_ret_heads_fwd(qb, kb, vb, qfw, qbw, dc_ref, rf_ref, rb_ref, u)[1])
        for u in range(G):
            rows = slice(C * u, C * u + C)
            for h in range(HR):
                vs = _vs(h)
                o = outs[u][h]
                mu = jnp.mean(o, axis=-1, keepdims=True)
                var = jnp.mean(jnp.square(o - mu), axis=-1, keepdims=True)
                on = (o - mu) * lax.rsqrt(var + EPS)
                z = z_ref[rows, vs].astype(F32)
                yr_ref[rows, vs] = ((on * gn_ref[:, vs]) * (z * _sigmoid(z))).astype(BF16)

    row = lambda w, off=0: pl.BlockSpec((G * C, w), lambda t: (t, off))
    stb = lambda: pl.BlockSpec((G, HR, DH, DV), lambda t: (t, 0, 0, 0))
    return pl.pallas_call(
        body, name="ret_out", out_shape=jax.ShapeDtypeStruct((S, 512), BF16), grid=(N // G,),
        in_specs=[row(256), row(256), row(512, O_VR // 512), row(512, O_ZR // 512), stb(), stb(),
                  _full((HR, C, C)), _full((C, 256)), _full((C, 256)), _full((1, 512))],
        out_specs=row(512),
        compiler_params=_cp(("parallel",)),
    )(qr2, kr2, p, p, rf, rb, dc, qdf, qdb, gn)


def _mid(x, tgt, mod, g_post, o_att, p, yr, w_pa, w_pr, w_out):
    S = x.shape[0]
    tm = min(256, S)

    def body(x_ref, t_ref, mod_ref, gp_ref, o_ref, za_ref, gl_ref, yr_ref, wpa_ref, wpr_ref, wout_ref,
             dout_ref, do_ref, dpm_ref, dyr_ref, mb_ref, dub_ref, yab_ref, dab_ref, drb_ref, sums_ref):
        @pl.when(pl.program_id(0) == 0)
        def _():
            sums_ref[...] = jnp.zeros_like(sums_ref)

        za = za_ref[...].astype(F32)
        sa = _sigmoid(za)
        sil = za * sa
        ov = o_ref[...]
        ya_b = (ov * sil).astype(BF16)
        yr_b = yr_ref[...]
        av = _dot(ya_b, wpa_ref[...])
        rv = _dot(yr_b, wpr_ref[...])
        ga = _sigmoid(gl_ref[:, :D].astype(F32))
        gr = _sigmoid(gl_ref[:, D:].astype(F32))
        mb = (ga * av + gr * rv).astype(BF16)
        u = _dot(mb, wout_ref[...])
        r2 = lax.rsqrt(jnp.mean(u * u, axis=-1, keepdims=True) + EPS)
        un = u * r2
        gp = gp_ref[...]
        yv = un * gp
        gate = mod_ref[2:3, :]
        err = (x_ref[...] + gate * yv) - t_ref[...]
        dout = err * (1.0 / D)
        dout_ref[...] = dout
        dy = dout * gate
        sums_ref[0:1, :] += jnp.sum(dout * yv, axis=0, keepdims=True)
        sums_ref[1:2, :] += jnp.sum(dy * un, axis=0, keepdims=True)
        sums_ref[2:3, :] += jnp.sum(err * err, axis=0, keepdims=True)
        dyg = dy * gp
        du_b = (r2 * (dyg - un * jnp.mean(dyg * un, axis=-1, keepdims=True))).astype(BF16)
        dm = _dot(du_b, wout_ref[...], NT)
        da_b = (dm * ga).astype(BF16)
        dr_b = (dm * gr).astype(BF16)
        dpm_ref[:, :D] = (dm * av * (ga * (1.0 - ga))).astype(BF16)
        dpm_ref[:, D:2 * D] = (dm * rv * (gr * (1.0 - gr))).astype(BF16)
        dya = _dot(da_b, wpa_ref[...], NT)
        dyr_ref[...] = _dot(dr_b, wpr_ref[...], NT)
        dov = dya * sil
        for g in range(4):
            dt = dov[:, 128 * g:128 * g + 128].T
            do_ref[2 * g] = dt[:DH].astype(BF16)
            do_ref[2 * g + 1] = dt[DH:].astype(BF16)
        dpm_ref[:, 2 * D:] = (dya * ov * (sa * (1.0 + za * (1.0 - sa)))).astype(BF16)
        mb_ref[...] = mb
        dub_ref[...] = du_b
        yab_ref[...] = ya_b
        dab_ref[...] = da_b
        drb_ref[...] = dr_b

    row = lambda w: pl.BlockSpec((tm, w), lambda i: (i, 0))
    sd = lambda w, dt: jax.ShapeDtypeStruct((S, w), dt)
    return pl.pallas_call(
        body, name="mid",
        out_shape=[sd(D, F32), jax.ShapeDtypeStruct((8, DH, S), BF16), sd(2560, BF16), sd(512, F32), sd(D, BF16), sd(D, BF16), sd(512, BF16),
                   sd(D, BF16), sd(D, BF16), jax.ShapeDtypeStruct((8, D), F32)],
        grid=(S // tm,),
        in_specs=[row(D), row(D), _full((3, D)), _full((1, D)), row(512), pl.BlockSpec((tm, 512), lambda i: (i, O_ZA // 512)),
                  pl.BlockSpec((tm, 2048), lambda i: (i, 0)), row(512), _full((512, D)), _full((512, D)), _full((D, D))],
        out_specs=[row(D), pl.BlockSpec((8, DH, tm), lambda i: (0, 0, i)), row(2560), row(512), row(D), row(D), row(512), row(D), row(D),
                   _full((8, D))],
        compiler_params=_cp(("arbitrary",), VMEM_BIG),
    )(x, tgt, mod, g_post, o_att, p, p, yr, w_pa, w_pr, w_out)


def _attn_bwd(qt, kh, kt, vh, dot_, ot, lse):
    S = qt.shape[2]
    tq, tk = min(1024, S), min(1024, S)

    def body(q_ref, k_ref, kt_ref, v_ref, do_ref, o_ref, lse_ref, dq_ref, dk_ref, dv_ref):
        j, i = pl.program_id(0), pl.program_id(1)
        cols = pl.ds(pl.multiple_of(i * tq, tq), tq)
        st = {0: _dot(k_ref[0], q_ref[0])}
        dpt = {0: _dot(v_ref[0], do_ref[0])}
        dk_acc, dv_acc, dqs = [None, None], [None, None], []
        for h in range(8):
            g = h // 4
            if h + 1 < 8:
                st[h + 1] = _dot(k_ref[(h + 1) // 4], q_ref[h + 1])
                dpt[h + 1] = _dot(v_ref[(h + 1) // 4], do_ref[h + 1])
            qt_h, dot_h = q_ref[h], do_ref[h]
            delta = jnp.sum(dot_h.astype(F32) * o_ref[h], axis=0, keepdims=True)
            pt = jnp.exp2(st[h] - lse_ref[g, h % 4:h % 4 + 1, :])
            dst = (pt * (dpt[h] - delta)).astype(BF16)
            dv_h = _dot(dot_h, pt.astype(BF16), NT)
            dk_h = _dot(qt_h, dst, NT)
            dqs.append(_dot(kt_ref[g], dst))
            dv_acc[g] = dv_h if dv_acc[g] is None else dv_acc[g] + dv_h
            dk_acc[g] = dk_h if dk_acc[g] is None else dk_acc[g] + dk_h
            del st[h], dpt[h]

        @pl.when(i == 0)
        def _():
            for g in range(2):
                dk_ref[g] = dk_acc[g]
                dv_ref[g] = dv_acc[g]

        @pl.when(i > 0)
        def _():
            for g in range(2):
                dk_ref[g] += dk_acc[g]
                dv_ref[g] += dv_acc[g]

        @pl.when(j == 0)
        def _():
            for h in range(8):
                dq_ref[h, :, cols] = dqs[h]

        @pl.when(j > 0)
        def _():
            for h in range(8):
                dq_ref[h, :, cols] += dqs[h]

    return pl.pallas_call(
        body, name="attn_bwd",
        out_shape=[jax.ShapeDtypeStruct((8, DH, S), F32), jax.ShapeDtypeStruct((2, DH, S), F32), jax.ShapeDtypeStruct((2, DH, S), F32)],
        grid=(S // tk, S // tq),
        in_specs=[pl.BlockSpec((8, DH, tq), lambda j, i: (0, 0, i)), pl.BlockSpec((2, tk, DH), lambda j, i: (0, j, 0)),
                  pl.BlockSpec((2, DH, tk), lambda j, i: (0, 0, j)), pl.BlockSpec((2, tk, DH), lambda j, i: (0, j, 0)),
                  pl.BlockSpec((8, DH, tq), lambda j, i: (0, 0, i)), pl.BlockSpec((8, DH, tq), lambda j, i: (0, 0, i)),
                  pl.BlockSpec((2, 4, tq), lambda j, i: (0, 0, i))],
        out_specs=[pl.BlockSpec((8, DH, S), lambda j, i: (0, 0, 0)), pl.BlockSpec((2, DH, tk), lambda j, i: (0, 0, j)),
                   pl.BlockSpec((2, DH, tk), lambda j, i: (0, 0, j))],
        compiler_params=_cp(("arbitrary", "arbitrary"), VMEM_BIG),
    )(qt, kh, kt, vh, dot_, ot, lse)


def _attn_prep_bwd(dqt, dkt, dvt, p, cos, sin, qg, kg):
    S = dqt.shape[2]
    tm = min(512, S)

    def body(dq_ref, dk_ref, dv_ref, qa_ref, ka_ref, cos_ref, sin_ref, qg_ref, kg_ref, dp_ref, gs_ref):
        @pl.when(pl.program_id(0) == 0)
        def _():
            gs_ref[...] = jnp.zeros_like(gs_ref)

        cos_v, sin_v = cos_ref[...], sin_ref[...]

        def pair(ref, a):
            return jnp.concatenate([ref[a], ref[a + 1]], axis=0).T

        def norm_bwd(dyv, xv, gv, row):
            r = lax.rsqrt(_head_mean(xv * xv) + EPS)
            xn = xv * r
            dxh = _rope_t(dyv, cos_v, sin_v)
            gs_ref[row:row + 1, :] += jnp.sum(dxh * xn, axis=0, keepdims=True)
            dg = dxh * gv
            return r * (dg - xn * _head_mean(dg * xn))

        for g in range(4):
            sl = slice(128 * g, 128 * g + 128)
            dp_ref[:, sl] = norm_bwd(pair(dq_ref, 2 * g) * 0.125, qa_ref[:, sl].astype(F32), qg_ref[...], 0).astype(BF16)
        dp_ref[:, 512:640] = norm_bwd(pair(dk_ref, 0) * LN2, ka_ref[...].astype(F32), kg_ref[...], 1).astype(BF16)
        dp_ref[:, 640:768] = pair(dv_ref, 0).astype(BF16)

    ht = lambda n: pl.BlockSpec((n, DH, tm), lambda i: (0, 0, i))
    return pl.pallas_call(
        body, name="attn_prep_bwd", out_shape=[jax.ShapeDtypeStruct((S, 768), BF16), jax.ShapeDtypeStruct((8, 128), F32)],
        grid=(S // tm,),
        in_specs=[ht(8), ht(2), ht(2),
                  pl.BlockSpec((tm, 512), lambda i: (i, O_QA // 512)), pl.BlockSpec((tm, 128), lambda i: (i, O_KA // 128)),
                  pl.BlockSpec((tm, 128), lambda i: (i, 0)), pl.BlockSpec((tm, 128), lambda i: (i, 0)), _full((1, 128)), _full((1, 128))],
        out_specs=[pl.BlockSpec((tm, 768), lambda i: (i, 0)), _full((8, 128))],
        compiler_params=_cp(("arbitrary",)),
    )(dqt, dkt, dvt, p, p, cos, sin, qg, kg)


def _ret_bwd_chunk(qr2, kr2, p, rf, rb, dc, qdf, qdb, gn, dyr, cos, sin):
    S = qr2.shape[0]
    C, N = CH, S // CH
    G = 2 if N % 2 == 0 else 1

    def body(q_ref, k_ref, v_ref, z_ref, rf_ref, rb_ref, dc_ref, qdf_ref, qdb_ref, gn_ref, dyr_ref, cos_ref, sin_ref,
             dpa_ref, dk_ref, dv_ref, drf_ref, drb_ref, dgn_ref, dlg_ref, dqs):
        @pl.when(pl.program_id(0) == 0)
        def _():
            dgn_ref[...] = jnp.zeros_like(dgn_ref)
            dlg_ref[...] = jnp.zeros_like(dlg_ref)

        ii = lax.broadcasted_iota(jnp.int32, (C, C), 0).astype(F32)
        jj = lax.broadcasted_iota(jnp.int32, (C, C), 1).astype(F32)
        dif = ii - jj
        ri = lax.broadcasted_iota(jnp.int32, (C, 1), 0).astype(F32)
        hs, us = range(HR), range(G)
        rows = [slice(C * u, C * u + C) for u in us]
        qv = [q_ref[rows[u], :] for u in us]
        qb = [qv[u].astype(BF16) for u in us]
        kb = [k_ref[rows[u], :].astype(BF16) for u in us]
        vb = [v_ref[rows[u], :].astype(BF16) for u in us]
        qf32 = [qv[u] * qdf_ref[...] for u in us]
        qb32 = [qv[u] * qdb_ref[...] for u in us]
        qfw = [qf32[u].astype(BF16) for u in us]
        qbw = [qb32[u].astype(BF16) for u in us]
        fwd = [_ret_heads_fwd(qb[u], kb[u], vb[u], qfw[u], qbw[u], dc_ref, rf_ref, rb_ref, u) for u in us]
        sd = [f[0] for f in fwd]
        do_b = [[] for _ in us]
        for u in us:
            for h in hs:
                vs = _vs(h)
                o = fwd[u][1][h]
                mu = jnp.mean(o, axis=-1, keepdims=True)
                rstd = lax.rsqrt(jnp.mean(jnp.square(o - mu), axis=-1, keepdims=True) + EPS)
                on = (o - mu) * rstd
                z = z_ref[rows[u], vs].astype(F32)
                sz = _sigmoid(z)
                dy = dyr_ref[rows[u], vs]
                gnv = gn_ref[:, vs]
                dpa_ref[rows[u], 256 + DV * h:256 + DV * h + DV] = (dy * (on * gnv) * (sz * (1.0 + z * (1.0 - sz)))).astype(BF16)
                dys = dy * (z * sz)
                dgn_ref[:, vs] += jnp.sum(dys * on, axis=0, keepdims=True)
                don = dys * gnv
                do = rstd * (don - jnp.mean(don, axis=-1, keepdims=True) - on * jnp.mean(don * on, axis=-1, keepdims=True))
                do_b[u].append(do.astype(BF16))
        dpm = [[_dot(do_b[u][h], vb[u][:, _vs(h)], NT) for h in hs] for u in us]
        dqf = [[_dot(do_b[u][h], rf_ref[u, h].astype(BF16), NT) for h in hs] for u in us]
        dqb = [[_dot(do_b[u][h], rb_ref[u, h].astype(BF16), NT) for h in hs] for u in us]
        for u in us:
            for h in hs:
                dv_ref[rows[u], _vs(h)] = _dot(sd[u][h].astype(BF16), do_b[u][h], TN)
                drf_ref[u, h] = _dot(qfw[u][:, _ks(h)], do_b[u][h], TN)
                drb_ref[u, h] = _dot(qbw[u][:, _ks(h)], do_b[u][h], TN)
        dsd = [[(dpm[u][h] * dc_ref[h]).astype(BF16) for h in hs] for u in us]
        for u in us:
            for h in hs:
                ks = _ks(h)
                dqs[rows[u], ks] = _dot(dsd[u][h], kb[u][:, ks]) + dqf[u][h] * qdf_ref[:, ks] + dqb[u][h] * qdb_ref[:, ks]
                dk_ref[rows[u], ks] = _dot(dsd[u][h], qb[u][:, ks], TN)
        for u in us:
            for h in hs:
                ks = _ks(h)
                e = dpm[u][h] * sd[u][h]
                lf = (_sum11(e * jnp.maximum(dif, 0.0))
                      + _sum11(jnp.sum(qf32[u][:, ks] * dqf[u][h], axis=-1, keepdims=True) * (ri + 1.0)))
                lb = (_sum11(e * jnp.maximum(-dif, 0.0))
                      + _sum11(jnp.sum(qb32[u][:, ks] * dqb[u][h], axis=-1, keepdims=True) * (C - ri)))
                dlg_ref[h:h + 1, :] += jnp.broadcast_to(lf, (1, 128))
                dlg_ref[HR + h:HR + h + 1, :] += jnp.broadcast_to(lb, (1, 128))
            for g in range(2):
                sl = slice(128 * g, 128 * g + 128)
                dpa_ref[rows[u], sl] = _rope_t(dqs[rows[u], sl], cos_ref[rows[u], :], sin_ref[rows[u], :]).astype(BF16)

    st = jax.ShapeDtypeStruct((N, HR, DH, DV), F32)
    stb = lambda: pl.BlockSpec((G, HR, DH, DV), lambda t: (t, 0, 0, 0))
    row = lambda w, off=0: pl.BlockSpec((G * C, w), lambda t: (t, off))
    return pl.pallas_call(
        body, name="ret_bwd_chunk",
        out_shape=[jax.ShapeDtypeStruct((S, 768), BF16), jax.ShapeDtypeStruct((S, 256), F32), jax.ShapeDtypeStruct((S, 512), F32), st, st,
                   jax.ShapeDtypeStruct((1, 512), F32), jax.ShapeDtypeStruct((8, 128), F32)],
        grid=(N // G,),
        in_specs=[row(256), row(256), row(512, O_VR // 512), row(512, O_ZR // 512),
                  stb(), stb(), _full((HR, C, C)), _full((C, 256)), _full((C, 256)), _full((1, 512)), row(512), row(128), row(128)],
        out_specs=[row(768), row(256), row(512), stb(), stb(), _full((1, 512)), _full((8, 128))],
        scratch_shapes=[pltpu.VMEM((G * C, 256), F32)],
        compiler_params=_cp(("arbitrary",)),
    )(qr2, kr2, p, p, rf, rb, dc, qdf, qdb, gn, dyr, cos, sin)


def _ret_bwd_scan(kr2, p, rf, rb, drf, drb, kdf, kdb, adec):
    S = kr2.shape[0]
    C, N = CH, S // CH
    G = _scan_group(N)
    NG = N // G

    def body(kf_ref, vf_ref, kb_ref, vb_ref, rf_ref, rb_ref, drf_ref, drb_ref, kdf_ref, kdb_ref, a_ref,
             dkf_ref, dkb_ref, dvf_ref, dvb_ref, dlg_ref, gf, gb):
        @pl.when(pl.program_id(0) == 0)
        def _():
            gf[...] = jnp.zeros_like(gf)
            gb[...] = jnp.zeros_like(gb)
            dlg_ref[...] = jnp.zeros_like(dlg_ref)

        ri = lax.broadcasted_iota(jnp.int32, (C, 1), 0).astype(F32)

        def one(k_ref, v_ref, r_ref, dr_ref, kd_ref, g_s, dk_ref, dv_ref, row0, wexp, order):
            g = [g_s[h] for h in range(HR)]
            lgs = [jnp.zeros((1, 1), F32) for _ in range(HR)]
            for u in order:
                rows = slice(C * u, C * u + C)
                kd32 = k_ref[rows, :] * kd_ref[...]
                kdw = kd32.astype(BF16)
                vb = v_ref[rows, :].astype(BF16)
                for h in range(HR):
                    ks, vs = _ks(h), _vs(h)
                    g_b = g[h].astype(BF16)
                    dkd = _dot(vb[:, vs], g_b, NT)
                    dk_ref[rows, ks] = dkd * kd_ref[:, ks]
                    dv_ref[rows, vs] = _dot(kdw[:, ks], g_b)
                    av = a_ref[row0 + h:row0 + h + 1, :]
                    lgs[h] = lgs[h] + (_sum11(jnp.sum(kd32[:, ks] * dkd, axis=-1, keepdims=True) * wexp)
                                       + C * av[:, 0:1] * _sum11(r_ref[u, h] * g[h]))
                    g[h] = dr_ref[u, h] + av * g[h]
            for h in range(HR):
                g_s[h] = g[h]
                dlg_ref[row0 + h:row0 + h + 1, :] += jnp.broadcast_to(lgs[h], (1, 128))

        one(kf_ref, vf_ref, rf_ref, drf_ref, kdf_ref, gf, dkf_ref, dvf_ref, 0, C - 1.0 - ri, list(reversed(range(G))))
        one(kb_ref, vb_ref, rb_ref, drb_ref, kdb_ref, gb, dkb_ref, dvb_ref, HR, ri, list(range(G)))

    fwd = lambda w, off=0: pl.BlockSpec((G * C, w), lambda t: (NG - 1 - t, off))
    bwd = lambda w, off=0: pl.BlockSpec((G * C, w), lambda t: (t, off))
    stf = lambda: pl.BlockSpec((G, HR, DH, DV), lambda t: (NG - 1 - t, 0, 0, 0))
    stb = lambda: pl.BlockSpec((G, HR, DH, DV), lambda t: (t, 0, 0, 0))
    return pl.pallas_call(
        body, name="ret_bwd_scan",
        out_shape=[jax.ShapeDtypeStruct((S, 256), F32), jax.ShapeDtypeStruct((S, 256), F32), jax.ShapeDtypeStruct((S, 512), F32),
                   jax.ShapeDtypeStruct((S, 512), F32), jax.ShapeDtypeStruct((8, 128), F32)],
        grid=(NG,),
        in_specs=[fwd(256), fwd(512, O_VR // 512), bwd(256), bwd(512, O_VR // 512), stf(), stb(), stf(), stb(),
                  _full((C, 256)), _full((C, 256)), _full((8, 128))],
        out_specs=[fwd(256), bwd(256), fwd(512), bwd(512), _full((8, 128))],
        scratch_shapes=[pltpu.VMEM((HR, DH, DV), F32), pltpu.VMEM((HR, DH, DV), F32)],
        compiler_params=_cp(("arbitrary",)),
    )(kr2, p, kr2, p, rf, rb, drf, drb, kdf, kdb, adec)


def _ret_bwd_final(dk_i, dkf, dkb, dv_i, dvf, dvb, cos, sin):
    S = dk_i.shape[0]
    tm = min(512, S)

    def body(a_ref, b_ref, c_ref, d_ref, e_ref, f_ref, cos_ref, sin_ref, o_ref):
        o_ref[:, :512] = (d_ref[...] + e_ref[...] + f_ref[...]).astype(BF16)
        cos_v, sin_v = cos_ref[...], sin_ref[...]
        for g in range(2):
            sl = slice(128 * g, 128 * g + 128)
            dk = a_ref[:, sl] + b_ref[:, sl] + c_ref[:, sl]
            o_ref[:, 512 + 128 * g:512 + 128 * g + 128] = (_rope_t(dk, cos_v, sin_v) * 0.125).astype(BF16)

    row = lambda w: pl.BlockSpec((tm, w), lambda i: (i, 0))
    return pl.pallas_call(
        body, name="ret_bwd_final", out_shape=jax.ShapeDtypeStruct((S, 768), BF16), grid=(S // tm,),
        in_specs=[row(256), row(256), row(256), row(512), row(512), row(512), row(128), row(128)], out_specs=row(768),
        compiler_params=_cp(("parallel",)),
    )(dk_i, dkf, dkb, dv_i, dvf, dvb, cos, sin)


def _bwd_in(dpm, dpa, dpra, dprb, w_p, x, dout, mod, g_pre):
    S = x.shape[0]
    tm = min(256, S)

    def body(a_ref, b_ref, c_ref, d_ref, w_ref, x_ref, dout_ref, mod_ref, g_ref, gx_ref, sums_ref):
        @pl.when(pl.program_id(0) == 0)
        def _():
            sums_ref[...] = jnp.zeros_like(sums_ref)

        dh = (_dot(a_ref[...], w_ref[:, :O_QA], NT) + _dot(b_ref[...], w_ref[:, O_QA:O_QR], NT)
              + _dot(c_ref[...], w_ref[:, O_QR:O_VR], NT) + _dot(d_ref[...], w_ref[:, O_VR:], NT))
        xv = x_ref[...]
        r = lax.rsqrt(jnp.mean(xv * xv, axis=-1, keepdims=True) + EPS)
        xn = xv * r
        gv = g_ref[...]
        sc1 = 1.0 + mod_ref[1:2, :]
        sums_ref[0:1, :] += jnp.sum(dh, axis=0, keepdims=True)
        sums_ref[1:2, :] += jnp.sum(dh * (xn * gv), axis=0, keepdims=True)
        sums_ref[2:3, :] += jnp.sum(dh * xn, axis=0, keepdims=True) * sc1
        dxn = dh * (gv * sc1)
        gx_ref[...] = dout_ref[...] + r * (dxn - xn * jnp.mean(dxn * xn, axis=-1, keepdims=True))

    row = lambda w: pl.BlockSpec((tm, w), lambda i: (i, 0))
    return pl.pallas_call(
        body, name="bwd_in", out_shape=[jax.ShapeDtypeStruct((S, D), F32), jax.ShapeDtypeStruct((8, D), F32)], grid=(S // tm,),
        in_specs=[row(2560), row(768), row(768), row(768), _full((D, P_W)), row(D), row(D), _full((3, D)), _full((1, D))],
        out_specs=[row(D), _full((8, D))],
        compiler_params=_cp(("arbitrary",), VMEM_BIG),
    )(dpm, dpa, dpra, dprb, w_p, x, dout, mod, g_pre)


SMALL = ("b_ada", "g_pre", "qn_g", "kn_g", "w_dec_f", "w_dec_b", "gn_g", "g_post")


def _small_update(gathered, wmv):
    ns = len(SMALL)

    def body(*refs):
        gin_ref, gmid_ref, ggn_ref, gatt_ref, gl1_ref, gl2_ref = refs[:6]
        wmv_refs = refs[6:6 + 3 * ns]
        loss_ref = refs[6 + 3 * ns]
        out_refs = refs[7 + 3 * ns:]

        def dsum(ref, r=None):
            rows = slice(None) if r is None else slice(r, r + 1)
            acc = ref[0, rows, :]
            for d in range(1, NDEV):
                acc = acc + ref[d, rows, :]
            return acc

        s_lg = dsum(gl1_ref) + dsum(gl2_ref)
        loss_ref[...] = (0.5 / D) * jnp.sum(dsum(gmid_ref, 2), axis=-1, keepdims=True)
        eye = lax.broadcasted_iota(jnp.int32, (8, 128), 0) == lax.broadcasted_iota(jnp.int32, (8, 128), 1)
        dlg = jnp.sum(jnp.where(eye, s_lg, 0.0), axis=0, keepdims=True)
        w_f, w_b = wmv_refs[3 * SMALL.index("w_dec_f")][...], wmv_refs[3 * SMALL.index("w_dec_b")][...]
        s_q, s_k = dsum(gatt_ref, 0), dsum(gatt_ref, 1)
        grads = dict(
            b_ada=jnp.concatenate([dsum(gin_ref, 0), dsum(gin_ref, 1), dsum(gmid_ref, 0)], axis=1),
            g_pre=dsum(gin_ref, 2), g_post=dsum(gmid_ref, 1), gn_g=dsum(ggn_ref),
            qn_g=s_q[:, :DH] + s_q[:, DH:], kn_g=s_k[:, :DH] + s_k[:, DH:],
            w_dec_f=dlg[:, 0:HR] * _sigmoid(-w_f), w_dec_b=dlg[:, HR:2 * HR] * _sigmoid(-w_b))
        for i, nme in enumerate(SMALL):
            g = grads[nme]
            w_ref, m_ref, v_ref = wmv_refs[3 * i:3 * i + 3]
            g_ref, d_ref, nm_ref, nv_ref = out_refs[4 * i:4 * i + 4]
            g_ref[...] = g
            m2 = ADAM_B1 * m_ref[...] + (1.0 - ADAM_B1) * g
            v2 = ADAM_B2 * v_ref[...] + (1.0 - ADAM_B2) * jnp.square(g)
            m_hat = m2 / (1.0 - ADAM_B1 ** ADAM_STEP)
            v_hat = v2 / (1.0 - ADAM_B2 ** ADAM_STEP)
            d_ref[...] = -ADAM_LR * (m_hat / (jnp.sqrt(v_hat) + ADAM_EPS) + ADAM_WD * w_ref[...])
            nm_ref[...] = m2
            nv_ref[...] = v2

    out_shape = [jax.ShapeDtypeStruct((1, 1), F32)]
    for i in range(ns):
        out_shape += [jax.ShapeDtypeStruct(wmv[3 * i].shape, F32)] * 4
    return pl.pallas_call(body, name="small_update", out_shape=out_shape)(*gathered, *wmv)


def _adamw(parts, w, m, v, name):
    n, R, L = parts.shape
    tr = 256 if (R % 256 == 0 and R > 256) else R

    def body(p_ref, w_ref, m_ref, v_ref, g_ref, d_ref, nm_ref, nv_ref):
        g = p_ref[0].astype(F32)
        for k in range(1, n):
            g = g + p_ref[k].astype(F32)
        g_ref[...] = g
        m2 = ADAM_B1 * m_ref[...] + (1.0 - ADAM_B1) * g
        v2 = ADAM_B2 * v_ref[...] + (1.0 - ADAM_B2) * jnp.square(g)
        m_hat = m2 / (1.0 - ADAM_B1 ** ADAM_STEP)
        v_hat = v2 / (1.0 - ADAM_B2 ** ADAM_STEP)
        d_ref[...] = -ADAM_LR * (m_hat / (jnp.sqrt(v_hat) + ADAM_EPS) + ADAM_WD * w_ref[...])
        nm_ref[...] = m2
        nv_ref[...] = v2

    blk = pl.BlockSpec((tr, L), lambda i: (i, 0))
    o = jax.ShapeDtypeStruct((R, L), F32)
    return pl.pallas_call(
        body, name=name, out_shape=[o, o, o, o], grid=(R // tr,),
        in_specs=[pl.BlockSpec((n, tr, L), lambda i: (0, i, 0)), blk, blk, blk], out_specs=[blk, blk, blk, blk],
        compiler_params=_cp(("parallel",), VMEM_BIG),
    )(parts, w, m, v)


def _rope_tables(S):
    f = np.float32
    t = np.arange(S)
    row, col = (t // 64).astype(f), (t % 64).astype(f)
    half = DH // 2
    inv = np.power(f(ROPE_THETA), -np.arange(0, half, 2, dtype=f) / f(half)).astype(f)
    ar, ac = (row[:, None] * inv[None, :]).astype(f), (col[:, None] * inv[None, :]).astype(f)
    cos64 = np.concatenate([np.cos(ar), np.cos(ar), np.cos(ac), np.cos(ac)], axis=1).astype(f)
    sin64 = np.concatenate([-np.sin(ar), np.sin(ar), -np.sin(ac), np.sin(ac)], axis=1).astype(f)
    return jnp.asarray(np.tile(cos64, (1, 2))), jnp.asarray(np.tile(sin64, (1, 2)))


def _to_p_order(w_orig):
    return jnp.concatenate([w_orig[:, ORIG[n][0]:ORIG[n][1]] for n in P_ORDER], axis=1)


def _pad_lanes(v, n):
    return jnp.pad(v, ((0, 0), (0, n - v.shape[1])))


def kernel(x, c, w_ada, b_ada, g_pre, w_in, qn_g, kn_g, w_dec_f, w_dec_b, gn_g, w_pa, w_pr, w_out, g_post, loss_target, m_w_ada, m_b_ada, m_g_pre, m_w_in, m_qn_g, m_kn_g, m_w_dec_f, m_w_dec_b, m_gn_g, m_w_pa, m_w_pr, m_w_out, m_g_post, v_w_ada, v_b_ada, v_g_pre, v_w_in, v_qn_g, v_kn_g, v_w_dec_f, v_w_dec_b, v_gn_g, v_w_pa, v_w_pr, v_w_out, v_g_post):
    S = x.shape[1]
    me = 4 * lax.axis_index("x") + 2 * lax.axis_index("y") + lax.axis_index("c")
    xs, tgt = x[0], loss_target[0]
    ncol_ada = w_ada.shape[2]
    ncol_in = w_in.shape[2]

    b_ada_s = lax.dynamic_slice(b_ada, (0, me * ncol_ada), (1, ncol_ada))
    mod_all, c_act, (wg_in,) = _prologue(jnp.pad(c, ((0, 7), (0, 0))), w_ada[0], b_ada_s, [w_in[0].astype(BF16)])
    mod = lax.dynamic_index_in_dim(mod_all, me, axis=1, keepdims=False).reshape(3, D)
    w_p = _to_p_order(wg_in.transpose(1, 0, 2).reshape(D, NDEV * ncol_in))
    all_dev = tuple(range(NDEV))
    st_w, tok_w = _xchg_start([(w_pa[0].astype(BF16)[None], all_dev), (w_pr[0].astype(BF16)[None], all_dev),
                               (w_out[0].astype(BF16)[None], all_dev)], "wgather_start")

    cos, sin = _rope_tables(S)
    qg, kg = jnp.tile(qn_g, (1, 2)), jnp.tile(kn_g, (1, 2))

    p, h = _fwd_in(xs, mod, g_pre + tok_w[0:1, 0:1], w_p)
    qt, kh, kt, vh, vta, qr2, kr2 = _prep(p, cos, sin, qg, kg)
    o_att, o_t, lse = _attn_fwd(qt, kh, vta)
    dc, qdf, qdb, kdf, kdb, adec = _ret_tables(w_dec_f, w_dec_b)
    rf, rb = _ret_states(kr2, p, kdf, kdb, adec)
    yr = _ret_out(qr2, kr2, p, rf, rb, dc, qdf, qdb, gn_g)
    wg_pa, wg_pr, wg_out = _xchg_wait([st_w], st_w["lands"], [[0, 1, 2]], yr, "wgather_wait")
    w_pa_f = wg_pa.transpose(1, 0, 2).reshape(512, D)
    w_pr_f = wg_pr.transpose(1, 0, 2).reshape(512, D)
    w_out_f = wg_out.reshape(D, D)

    dout, do, dpm, dyr, mb, dub, yab, dab, drb_, sums_mid = _mid(xs, tgt, mod, g_post, o_att, p, yr, w_pa_f, w_pr_f, w_out_f)
    gw_out = _mm_tn(mb, dub, "gw_out", BF16)
    gw_pa = _mm_tn(yab, dab, "gw_pa", BF16)
    gw_pr = _mm_tn(yr, drb_, "gw_pr", BF16)
    gi_m = _mm_tn(h, dpm, "gw_in_mid", BF16)

    def shards(cols, nd):
        return cols.reshape(D, nd, ncol_in).transpose(1, 0, 2)

    st_a, tok_a = _xchg_start([
        (gw_out.reshape(NDEV, 128, D), all_dev),
        (gw_pa.reshape(512, NDEV, 128).transpose(1, 0, 2), all_dev),
        (gw_pr.reshape(512, NDEV, 128).transpose(1, 0, 2), all_dev),
        (shards(gi_m[:, 224:2048], 3), (5, 6, 7))], "xchg_start_a",
        lands=[None, None, None, jnp.zeros((NDEV, D, ncol_in), BF16)])
    dqt, dkt, dvt = _attn_bwd(qt, kh, kt, vh, do, o_t, lse + tok_a[0, 0])
    dpa, gs_att = _attn_prep_bwd(dqt, dkt, dvt, p, cos, sin, qg, kg)
    gi_a = _mm_tn(h, dpa, "gw_in_att", BF16)
    st_b, tok_b = _xchg_start([(shards(jnp.concatenate([gi_a, gi_m[:, 2048:2496]], axis=1), 2), (0, 1))], "xchg_start_b",
                              lands=[st_a["lands"][3]])
    dpra, dk_i, dv_i, drf, drb, dgn, dlg1 = _ret_bwd_chunk(qr2, kr2, p, rf, rb, dc, qdf, qdb, gn_g + tok_b[0:1, 0:1], dyr, cos, sin)
    dkf, dkb, dvf, dvb, dlg2 = _ret_bwd_scan(kr2, p, rf, rb, drf, drb, kdf, kdb, adec)
    dprb = _ret_bwd_final(dk_i, dkf, dkb, dv_i, dvf, dvb, cos, sin)
    gi_ra = _mm_tn(h, dpra, "gw_in_reta", BF16)
    gi_rb = _mm_tn(h, dprb, "gw_in_retb", BF16)
    chip_c = _pair_reduce(shards(jnp.concatenate([gi_m[:, 2496:2560], gi_ra[:, :256], gi_rb[:, 512:768], gi_rb[:, :512],
                                                  gi_ra[:, 256:768], gi_m[:, :224]], axis=1), 3), (2, 3, 4), "pair_reduce_c")
    st_c, tok_c = _xchg_start([(chip_c, (2, 3, 4, "same core"))], "xchg_start_c", lands=[st_b["lands"][0]])
    grad_x, sums_in = _bwd_in(dpm, dpa, dpra, dprb, w_p, xs, dout, mod, g_pre + tok_c[0:1, 0:1])

    gathered = _small_allgather([sums_in, sums_mid, dgn, gs_att, dlg1, dlg2], "ag_small")
    given = dict(b_ada=(b_ada, m_b_ada, v_b_ada), g_pre=(g_pre, m_g_pre, v_g_pre), qn_g=(qn_g, m_qn_g, v_qn_g), kn_g=(kn_g, m_kn_g, v_kn_g),
                 w_dec_f=(w_dec_f, m_w_dec_f, v_w_dec_f), w_dec_b=(w_dec_b, m_w_dec_b, v_w_dec_b), gn_g=(gn_g, m_gn_g, v_gn_g),
                 g_post=(g_post, m_g_post, v_g_post))
    small = _small_update(gathered, [a for nme in SMALL for a in given[nme]])
    loss = small[0][0, 0]

    g_in_all, g_mid_all = gathered[0], gathered[1]
    dmod_all = lax.dynamic_slice(jnp.concatenate([g_in_all[:, 0, :], g_in_all[:, 1, :], g_mid_all[:, 0, :]], axis=1),
                                 (0, me * ncol_ada), (NDEV, ncol_ada))
    g_ada = _mm_tn(c_act, jnp.pad(dmod_all, ((0, 8), (0, 0))).astype(BF16), "gw_ada")

    ada = _adamw(g_ada[None], w_ada[0], m_w_ada[0], v_w_ada[0], "adamw_ada")
    rs_out, rs_pa, rs_pr, rs_in = _xchg_wait([st_a, st_b, st_c], list(st_a["lands"][:3]) + [st_c["lands"][0]],
                                             [[0, 1, 2, 3], [3], [3]], ada[1], "xchg_wait")
    res = dict(
        w_ada=ada,
        w_in=_adamw(rs_in, w_in[0], m_w_in[0], v_w_in[0], "adamw_in"),
        w_pa=_adamw(rs_pa, w_pa[0], m_w_pa[0], v_w_pa[0], "adamw_pa"),
        w_pr=_adamw(rs_pr, w_pr[0], m_w_pr[0], v_w_pr[0], "adamw_pr"),
        w_out=_adamw(rs_out, w_out[0], m_w_out[0], v_w_out[0], "adamw_out"),
    )
    names = ["w_ada", "b_ada", "g_pre", "w_in", "qn_g", "kn_g", "w_dec_f", "w_dec_b", "gn_g", "w_pa", "w_pr", "w_out", "g_post"]
    outs = [[], [], [], []]
    for nme in names:
        for q in range(4):
            if nme in res:
                outs[q].append(res[nme][q][None])
            else:
                outs[q].append(small[1 + 4 * SMALL.index(nme) + q])
    return (loss, grad_x[None], *outs[0], *outs[1], *outs[2], *outs[3])
```

```python
import jax
import jax.numpy as jnp
import numpy as np
from jax import lax
from jax.experimental import pallas as pl
from jax.experimental.pallas import tpu as pltpu

F32, BF16 = jnp.float32, jnp.bfloat16
D = 1024
DH = 64
DHA = 80
DV = 128
LOG2E = 1.4426950408889634
LN2 = 0.6931471805599453
HR = 4
CH = 128
EPS = 1e-6
ROPE_THETA = 10000.0
NDEV = 8
O_GL, O_ZA, O_QA, O_KA, O_VA, O_QR, O_ZR, O_VR, O_KR, P_W = 0, 2048, 2560, 3072, 3200, 3328, 3584, 4096, 4608, 4864
ORIG = dict(qa=(0, 512), ka=(512, 640), va=(640, 768), za=(768, 1280), qr=(1280, 1536), kr=(1536, 1792),
            vr=(1792, 2304), zr=(2304, 2816), gl=(2816, 4864))
P_ORDER = ("gl", "za", "qa", "ka", "va", "qr", "zr", "vr", "kr")
ADAM_LR, ADAM_B1, ADAM_B2, ADAM_EPS, ADAM_WD, ADAM_STEP = 0.001, 0.9, 0.999, 1e-08, 0.01, 10
VMEM_BIG = 56 * 1024 * 1024
VMEM_SMALL = 24 * 1024 * 1024
MESH = pl.DeviceIdType.MESH

NT = (((1,), (1,)), ((), ()))
TN = (((0,), (0,)), ((), ()))


def _dot(a, b, dims=None):
    if dims is None:
        return jnp.dot(a, b, preferred_element_type=F32)
    return lax.dot_general(a, b, dims, preferred_element_type=F32)


def _cp(sem=None, vmem=None):
    kw = {}
    if sem is not None:
        kw["dimension_semantics"] = sem
    if vmem is not None:
        kw["vmem_limit_bytes"] = vmem
    return pltpu.CompilerParams(**kw)


def _sigmoid(z):
    return 1.0 / (1.0 + jnp.exp(-z))


def _sum11(m):
    return jnp.sum(jnp.sum(m, axis=-1, keepdims=True), axis=0, keepdims=True)


def _full(shape):
    n = len(shape)
    return pl.BlockSpec(shape, lambda *_: (0,) * n)


def _my_pos():
    return lax.axis_index("x"), lax.axis_index("y"), lax.axis_index("c")


def _peer(k, x, y, c):
    return ((1 - x) if k & 4 else x, (1 - y) if k & 2 else y, (1 - c) if k & 1 else c)


def _small_allgather(vs, name):
    n = len(vs)

    def body(*refs):
        v_refs, out_refs = refs[:n], refs[n:2 * n]
        send_sems, recv_sems = refs[2 * n:]
        x, y, c = _my_pos()
        me = 4 * x + 2 * y + c
        cps = []
        for a in range(n):
            out_refs[a][me] = v_refs[a][...]
            for k in range(1, NDEV):
                cp = pltpu.make_async_remote_copy(src_ref=v_refs[a], dst_ref=out_refs[a].at[me], send_sem=send_sems.at[a, k - 1],
                                                  recv_sem=recv_sems.at[a, k - 1], device_id=_peer(k, x, y, c), device_id_type=MESH)
                cp.start()
                cps.append(cp)
        for cp in cps:
            cp.wait()

    vm = pl.BlockSpec(memory_space=pltpu.VMEM)
    return pl.pallas_call(
        body, name=name, out_shape=[jax.ShapeDtypeStruct((NDEV,) + v.shape, v.dtype) for v in vs],
        in_specs=[vm] * n, out_specs=[vm] * n,
        scratch_shapes=[pltpu.SemaphoreType.DMA((n, NDEV - 1)), pltpu.SemaphoreType.DMA((n, NDEV - 1))],
    )(*vs)


def _prologue(c8, w_ada_s, b_ada_s, arrs):
    n = len(arrs)
    ncol = w_ada_s.shape[1]

    def body(*refs):
        c_ref, wa_ref, ba_ref = refs[:3]
        ins = refs[3:3 + n]
        mod_ref, cact_ref = refs[3 + n:5 + n]
        outs = refs[5 + n:5 + 2 * n]
        call_ref, send_sems, recv_sems, local_sems, s_send, s_recv = refs[5 + 2 * n:]
        x, y, c = _my_pos()
        me, sibling = (x, y, c), (x, y, 1 - c)
        chips = [(1 - x, y), (x, 1 - y), (1 - x, 1 - y)]
        me_i = 4 * x + 2 * y + c

        def small_gather(src_ref, dst_ref, row):
            cps = []
            for k in range(1, NDEV):
                cp = pltpu.make_async_remote_copy(src_ref=src_ref, dst_ref=dst_ref.at[me_i], send_sem=s_send.at[row, k - 1],
                                                  recv_sem=s_recv.at[row, k - 1], device_id=_peer(k, x, y, c), device_id_type=MESH)
                cp.start()
                cps.append(cp)
            return cps

        def blk(a, px, py, pc):
            return outs[a].at[4 * px + 2 * py + pc]

        def copy(a, k, block, to, src=None):
            return pltpu.make_async_remote_copy(src_ref=blk(a, *block) if src is None else src, dst_ref=blk(a, *block),
                                                send_sem=send_sems.at[a, k], recv_sem=recv_sems.at[a, k], device_id=to, device_id_type=MESH)

        call_ref[me_i] = c_ref[...]
        for cp in small_gather(c_ref, call_ref, 0):
            cp.wait()

        local, sent = [], []
        for a in range(n):
            mine = pltpu.make_async_copy(ins[a], blk(a, *me), local_sems.at[a])
            mine.start()
            local.append(mine)
            first = [copy(a, 0, me, sibling, src=ins[a])] + [copy(a, 1 + j, me, (*chip, c), src=ins[a]) for j, chip in enumerate(chips)]
            for cp in first:
                cp.start()
            sent += first

        cv = call_ref[:, 0, :]
        ca = jnp.concatenate([cv * _sigmoid(cv), jnp.zeros_like(cv)], axis=0).astype(BF16)
        cact_ref[...] = ca
        mod_ref[me_i] = (_dot(ca, wa_ref[...].astype(BF16)) + ba_ref[...])[:8]
        mod_copies = small_gather(mod_ref.at[me_i], mod_ref, 1)

        for j, chip in enumerate(chips):
            for a in range(n):
                copy(a, 1 + j, (*chip, c), me).wait_recv()
                cp = copy(a, 4 + j, (*chip, c), sibling)
                cp.start()
                sent.append(cp)
        for a in range(n):
            copy(a, 0, sibling, me).wait_recv()
            for j, chip in enumerate(chips):
                copy(a, 4 + j, (*chip, 1 - c), me).wait_recv()
        for cp in sent:
            cp.wait_send()
        for cp in local + mod_copies:
            cp.wait()

    vm, hbm = pl.BlockSpec(memory_space=pltpu.VMEM), pl.BlockSpec(memory_space=pl.ANY)
    res = pl.pallas_call(
        body, name="prologue",
        out_shape=[jax.ShapeDtypeStruct((NDEV, 8, ncol), F32), jax.ShapeDtypeStruct((16, D), BF16)]
        + [jax.ShapeDtypeStruct((NDEV,) + a.shape, a.dtype) for a in arrs],
        in_specs=[vm, vm, vm] + [hbm] * n, out_specs=[vm, vm] + [hbm] * n,
        scratch_shapes=[pltpu.VMEM((NDEV, 8, D), F32), pltpu.SemaphoreType.DMA((n, NDEV - 1)), pltpu.SemaphoreType.DMA((n, NDEV - 1)),
                        pltpu.SemaphoreType.DMA((n,)), pltpu.SemaphoreType.DMA((2, NDEV - 1)), pltpu.SemaphoreType.DMA((2, NDEV - 1))],
    )(c8, w_ada_s, b_ada_s, *arrs)
    return res[0], res[1], res[2:]


def _in_set(idx, dests):
    p = idx == dests[0]
    for d in dests[1:]:
        p = jnp.logical_or(p, idx == d)
    return p


_HBM = pl.BlockSpec(memory_space=pltpu.HBM)
_SEM = pl.BlockSpec(memory_space=pltpu.SEMAPHORE)


def _pair_reduce(send, dests, name):
    nd = send.shape[0]

    def body(s_ref, o_ref, land, ssem, rsem):
        x, y, c = _my_pos()
        cps = []
        for i in range(nd):
            cp = pltpu.make_async_remote_copy(src_ref=s_ref.at[i], dst_ref=land.at[i], send_sem=ssem.at[i], recv_sem=rsem.at[i],
                                              device_id=(x, y, 1 - c), device_id_type=MESH)
            pl.when(c != (dests[i] & 1))(cp.start)
            cps.append(cp)
        for i in range(nd):
            mine = c == (dests[i] & 1)

            @pl.when(mine)
            def _():
                cps[i].wait_recv()
                o_ref[i] = (s_ref[i].astype(F32) + land[i].astype(F32)).astype(BF16)

            pl.when(jnp.logical_not(mine))(cps[i].wait_send)

    vm = pl.BlockSpec(memory_space=pltpu.VMEM)
    return pl.pallas_call(
        body, name=name, out_shape=jax.ShapeDtypeStruct(send.shape, send.dtype), in_specs=[vm], out_specs=vm,
        scratch_shapes=[pltpu.VMEM(send.shape, send.dtype), pltpu.SemaphoreType.DMA((nd,)), pltpu.SemaphoreType.DMA((nd,))],
        compiler_params=_cp(None, VMEM_SMALL),
    )(send)


def _xchg_copies(xs_dests, sends, lands, ssem, rsem, lsem):
    x, y, c = _my_pos()
    me = 4 * x + 2 * y + c
    remote, local = [], []
    for a, dests in enumerate(xs_dests):
        same_core = dests[-1] == "same core"
        dests = dests[:-1] if same_core else dests
        lo, nd = dests[0], sends[a].shape[0]
        for k in range(1, NDEV):
            if same_core and k & 1:
                continue
            px, py, pc = _peer(k, x, y, c)
            pidx = 4 * px + 2 * py + pc
            cp = pltpu.make_async_remote_copy(src_ref=sends[a].at[jnp.clip(pidx - lo, 0, nd - 1)], dst_ref=lands[a].at[me],
                                              send_sem=ssem.at[a * (NDEV - 1) + k - 1], recv_sem=rsem.at[a * (NDEV - 1) + k - 1],
                                              device_id=(px, py, pc), device_id_type=MESH)
            remote.append((cp, _in_set(pidx, dests), _in_set(me, dests)))
        lc = pltpu.make_async_copy(sends[a].at[jnp.clip(me - lo, 0, nd - 1)], lands[a].at[me], lsem.at[a])
        local.append((lc, _in_set(me, dests)))
    return remote, local


def _xchg_start(xs, name, lands=None):
    n = len(xs)
    dests = [d for _, d in xs]
    sends = [pltpu.with_memory_space_constraint(s, pltpu.HBM) for s, _ in xs]
    lands = [None] * n if lands is None else lands
    lands = [pltpu.with_memory_space_constraint(lax.empty((NDEV,) + s.shape[1:], s.dtype) if l is None else l, pltpu.HBM)
             for (s, _), l in zip(xs, lands)]

    def body(*refs):
        send_refs, land_refs = refs[:n], refs[n:2 * n]
        ssem, rsem, lsem = refs[2 * n:2 * n + 3]
        token = refs[-1]
        remote, local = _xchg_copies(dests, send_refs, land_refs, ssem, rsem, lsem)
        for cp, to_dest, _ in remote:
            pl.when(to_dest)(cp.start)
        for lc, i_am_dest in local:
            pl.when(i_am_dest)(lc.start)
        token[...] = jnp.zeros_like(token)

    res = pl.pallas_call(
        body, name=name,
        out_shape=[pltpu.SemaphoreType.DMA((n * (NDEV - 1),)), pltpu.SemaphoreType.DMA((n * (NDEV - 1),)), pltpu.SemaphoreType.DMA((n,))]
        + [pltpu.HBM(a.shape, a.dtype) for a in list(sends) + list(lands)] + [jax.ShapeDtypeStruct((8, 128), F32)],
        in_specs=[_HBM] * (2 * n), out_specs=[_SEM, _SEM, _SEM] + [_HBM] * (2 * n) + [pl.BlockSpec(memory_space=pltpu.VMEM)],
        input_output_aliases={i: 3 + i for i in range(2 * n)},
        compiler_params=pltpu.CompilerParams(has_side_effects=pltpu.SideEffectType.DATAFLOW_SIDE_EFFECTING),
    )(*sends, *lands)
    return dict(sems=res[0:3], sends=res[3:3 + n], lands=res[3 + n:3 + 2 * n], dests=dests), res[-1]


def _xchg_wait(states, lands, land_of, after, name):
    flat = []
    for st in states:
        flat += list(st["sends"]) + list(st["sems"])
    nl = len(lands)

    def body(*refs):
        land_refs = refs[:nl]
        pos = nl
        for s, st in enumerate(states):
            n = len(st["dests"])
            send_refs = refs[pos:pos + n]
            ssem, rsem, lsem = refs[pos + n:pos + n + 3]
            pos += n + 3
            remote, local = _xchg_copies(st["dests"], send_refs, [land_refs[i] for i in land_of[s]], ssem, rsem, lsem)
            for cp, to_dest, i_am_dest in remote:
                pl.when(to_dest)(cp.wait_send)
                pl.when(i_am_dest)(cp.wait_recv)
            for lc, i_am_dest in local:
                pl.when(i_am_dest)(lc.wait)

    in_specs = [_HBM] * nl
    for st in states:
        in_specs += [_HBM] * len(st["dests"]) + [_SEM, _SEM, _SEM]
    return pl.pallas_call(
        body, name=name, out_shape=[pltpu.HBM(a.shape, a.dtype) for a in lands],
        in_specs=in_specs + [pl.BlockSpec(memory_space=pl.ANY)], out_specs=[_HBM] * nl,
        input_output_aliases={i: i for i in range(nl)},
        compiler_params=pltpu.CompilerParams(has_side_effects=pltpu.SideEffectType.DATAFLOW_SIDE_EFFECTING),
    )(*lands, *flat, after)


def _mm_tn(a, b, name, out_dtype=F32):
    S, M = a.shape
    N = b.shape[1]
    tn = N if N <= 768 else (640 if N % 640 == 0 else 512)
    tk = min(4096 if N > tn else 2048, S)
    nk = S // tk

    def body(a_ref, b_ref, o_ref, acc):
        k = pl.program_id(1)

        @pl.when(k == 0)
        def _():
            acc[...] = _dot(a_ref[...], b_ref[...], TN)

        @pl.when(k > 0)
        def _():
            acc[...] += _dot(a_ref[...], b_ref[...], TN)

        @pl.when(k == nk - 1)
        def _():
            o_ref[...] = acc[...].astype(out_dtype)

    return pl.pallas_call(
        body, name=name, out_shape=jax.ShapeDtypeStruct((M, N), out_dtype), grid=(N // tn, nk),
        in_specs=[pl.BlockSpec((tk, M), lambda j, k: (k, 0)), pl.BlockSpec((tk, tn), lambda j, k: (k, j))],
        out_specs=pl.BlockSpec((M, tn), lambda j, k: (0, j)), scratch_shapes=[pltpu.VMEM((M, tn), F32)],
        compiler_params=_cp(("parallel", "arbitrary"), VMEM_BIG),
    )(a, b)


def _fwd_in(x, mod, g_pre, w_p):
    S = x.shape[0]
    tm = min(512, S)

    def body(x_ref, mod_ref, g_ref, w_ref, p_ref, h_ref):
        xv = x_ref[...]
        r = lax.rsqrt(jnp.mean(xv * xv, axis=-1, keepdims=True) + EPS)
        h = (((xv * r) * g_ref[...]) * (1.0 + mod_ref[1:2, :]) + mod_ref[0:1, :]).astype(BF16)
        h_ref[...] = h
        p_ref[...] = _dot(h, w_ref[...]).astype(BF16)

    return pl.pallas_call(
        body, name="fwd_in", out_shape=[jax.ShapeDtypeStruct((S, P_W), BF16), jax.ShapeDtypeStruct((S, D), BF16)],
        grid=(S // tm,),
        in_specs=[pl.BlockSpec((tm, D), lambda i: (i, 0)), _full((3, D)), _full((1, D)), _full((D, P_W))],
        out_specs=[pl.BlockSpec((tm, P_W), lambda i: (i, 0)), pl.BlockSpec((tm, D), lambda i: (i, 0))],
        compiler_params=_cp(("parallel",), VMEM_BIG),
    )(x, mod, g_pre, w_p)


def _swap16(v):
    lane = lax.broadcasted_iota(jnp.int32, v.shape, 1)
    return jnp.where((lane % 32) < 16, pltpu.roll(v, 112, 1), pltpu.roll(v, 16, 1))


def _rope(v, cos, sin):
    return v * cos + _swap16(v) * sin


def _rope_t(v, cos, sin):
    return v * cos - _swap16(v) * sin


def _head_mean(v):
    lo = lax.broadcasted_iota(jnp.int32, v.shape, 1) < 64
    m0 = jnp.sum(jnp.where(lo, v, 0.0), axis=-1, keepdims=True)
    m1 = jnp.sum(jnp.where(lo, 0.0, v), axis=-1, keepdims=True)
    return jnp.where(lo, m0, m1) * (1.0 / 64.0)


def _prep(p, cos, sin, qg, kg):
    S = p.shape[0]
    tm = min(512, S)

    def body(qa_ref, kv_ref, qr_ref, kr_ref, cos_ref, sin_ref, qg_ref, kg_ref, qt_ref, kh_ref, kt_ref, vh_ref, vta_ref, qr2_ref, kr2_ref):
        cos_v, sin_v = cos_ref[...], sin_ref[...]
        for g in range(4):
            xv = qa_ref[:, 128 * g:128 * g + 128].astype(F32)
            r = lax.rsqrt(_head_mean(xv * xv) + EPS)
            yt = (_rope((xv * r) * qg_ref[...], cos_v, sin_v) * (0.125 * LOG2E)).T
            qt_ref[2 * g] = yt[:DH].astype(BF16)
            qt_ref[2 * g + 1] = yt[DH:].astype(BF16)
        xv = kv_ref[:, :128].astype(F32)
        r = lax.rsqrt(_head_mean(xv * xv) + EPS)
        yv = _rope((xv * r) * kg_ref[...], cos_v, sin_v)
        kh_ref[0] = yv[:, :64].astype(BF16)
        kh_ref[1] = yv[:, 64:].astype(BF16)
        yt = yv.T
        kt_ref[0] = yt[:DH].astype(BF16)
        kt_ref[1] = yt[DH:].astype(BF16)
        vv = kv_ref[:, 128:].astype(F32)
        vh_ref[0] = vv[:, :64].astype(BF16)
        vh_ref[1] = vv[:, 64:].astype(BF16)
        vt = vv.T
        tail = (lax.broadcasted_iota(jnp.int32, (DHA - DH, tm), 0) == 0).astype(BF16)
        for kvh in range(2):
            vta_ref[kvh, 0:DH, :] = vt[DH * kvh:DH * kvh + DH].astype(BF16)
            vta_ref[kvh, DH:DHA, :] = tail
        for g in range(2):
            sl = slice(128 * g, 128 * g + 128)
            qr2_ref[:, sl] = _rope(qr_ref[:, sl].astype(F32), cos_v, sin_v)
            kr2_ref[:, sl] = _rope(kr_ref[:, sl].astype(F32), cos_v, sin_v) * 0.125

    hm = lambda n: pl.BlockSpec((n, tm, DH), lambda i: (0, i, 0))
    ht = lambda n, r: pl.BlockSpec((n, r, tm), lambda i: (0, 0, i))
    return pl.pallas_call(
        body, name="prep",
        out_shape=[jax.ShapeDtypeStruct((8, DH, S), BF16), jax.ShapeDtypeStruct((2, S, DH), BF16), jax.ShapeDtypeStruct((2, DH, S), BF16),
                   jax.ShapeDtypeStruct((2, S, DH), BF16), jax.ShapeDtypeStruct((2, DHA, S), BF16),
                   jax.ShapeDtypeStruct((S, 256), F32), jax.ShapeDtypeStruct((S, 256), F32)],
        grid=(S // tm,),
        in_specs=[pl.BlockSpec((tm, 512), lambda i: (i, O_QA // 512)), pl.BlockSpec((tm, 256), lambda i: (i, O_KA // 256)),
                  pl.BlockSpec((tm, 256), lambda i: (i, O_QR // 256)), pl.BlockSpec((tm, 256), lambda i: (i, O_KR // 256)),
                  pl.BlockSpec((tm, 128), lambda i: (i, 0)), pl.BlockSpec((tm, 128), lambda i: (i, 0)), _full((1, 128)), _full((1, 128))],
        out_specs=[ht(8, DH), hm(2), ht(2, DH), hm(2), ht(2, DHA), pl.BlockSpec((tm, 256), lambda i: (i, 0)), pl.BlockSpec((tm, 256), lambda i: (i, 0))],
        compiler_params=_cp(("parallel",)),
    )(p, p, p, p, cos, sin, qg, kg)


def _attn_fwd(qt, kh, vta):
    S = qt.shape[2]
    tq, tk = min(1024, S), min(512, S)
    nj = S // tk

    def body(q_ref, k_ref, v_ref, o_ref, ot_ref, lse_ref, m_s, acc_s):
        m_s[...] = jnp.full_like(m_s, -jnp.inf)
        acc_s[...] = jnp.zeros_like(acc_s)

        def key_block(j, carry):
            rows = pl.ds(pl.multiple_of(j * tk, tk), tk)
            m_all = m_s[...]
            st = {0: _dot(k_ref[0, rows, :], q_ref[0])}
            m_new, acc_new = [], []
            for h in range(8):
                if h + 1 < 8:
                    st[h + 1] = _dot(k_ref[(h + 1) // 4, rows, :], q_ref[h + 1])
                m_old = m_all[h:h + 1, :]
                mn = jnp.maximum(m_old, jnp.max(st[h], axis=0, keepdims=True))
                pt = jnp.exp2(st[h] - mn).astype(BF16)
                acc_new.append(jnp.exp2(m_old - mn) * acc_s[h] + _dot(v_ref[h // 4, :, rows], pt))
                m_new.append(mn)
                del st[h]
            for h in range(8):
                acc_s[h] = acc_new[h]
                m_s[h:h + 1, :] = m_new[h]
            return carry

        lax.fori_loop(0, nj, key_block, 0)
        for h in range(8):
            ot = acc_s[h, 0:DH, :] / acc_s[h, DH:DH + 1, :]
            ot_ref[h] = ot
            o_ref[:, DH * h:DH * h + DH] = ot.T
            lse_ref[h // 4, h % 4:h % 4 + 1, :] = m_s[h:h + 1, :] + jnp.log2(acc_s[h, DH:DH + 1, :])

    return pl.pallas_call(
        body, name="attn_fwd",
        out_shape=[jax.ShapeDtypeStruct((S, 512), F32), jax.ShapeDtypeStruct((8, DH, S), F32), jax.ShapeDtypeStruct((2, 4, S), F32)],
        grid=(S // tq,),
        in_specs=[pl.BlockSpec((8, DH, tq), lambda i: (0, 0, i)), _full((2, S, DH)), _full((2, DHA, S))],
        out_specs=[pl.BlockSpec((tq, 512), lambda i: (i, 0)), pl.BlockSpec((8, DH, tq), lambda i: (0, 0, i)),
                   pl.BlockSpec((2, 4, tq), lambda i: (0, 0, i))],
        scratch_shapes=[pltpu.VMEM((8, tq), F32), pltpu.VMEM((8, DHA, tq), F32)],
        compiler_params=_cp(("parallel",), VMEM_BIG),
    )(qt, kh, vta)


def _ret_tables(wf, wb):
    C = CH

    def body(wf_ref, wb_ref, dc_ref, qdf_ref, qdb_ref, kdf_ref, kdb_ref, a_ref):
        def logsig(w):
            z = jnp.exp(-jnp.abs(w))
            u = 1.0 + z
            l1p = jnp.where(u == 1.0, z, jnp.log(u) * (z / jnp.where(u == 1.0, 1.0, u - 1.0)))
            return jnp.minimum(w, 0.0) - l1p

        lgf, lgb = logsig(wf_ref[...]), logsig(wb_ref[...])
        lane4 = lax.broadcasted_iota(jnp.int32, (1, 4), 1)

        def pick(lg, h):
            return jnp.sum(jnp.where(lane4 == h, lg, 0.0), axis=-1, keepdims=True)

        ii = lax.broadcasted_iota(jnp.int32, (C, C), 0).astype(F32)
        jj = lax.broadcasted_iota(jnp.int32, (C, C), 1).astype(F32)
        dif = ii - jj
        hd = lax.broadcasted_iota(jnp.int32, (C, 256), 1) // DH
        lf_l = jnp.zeros((C, 256), F32)
        lb_l = jnp.zeros((C, 256), F32)
        for h in range(HR):
            lf, lb = pick(lgf, h), pick(lgb, h)
            dc_ref[h] = jnp.where(dif >= 0, jnp.exp(lf * jnp.maximum(dif, 0.0)), jnp.exp(lb * jnp.maximum(-dif, 0.0)))
            lf_l = jnp.where(hd == h, lf, lf_l)
            lb_l = jnp.where(hd == h, lb, lb_l)
            a_ref[h:h + 1, :] = jnp.broadcast_to(jnp.exp(lf * C), (1, 128))
            a_ref[HR + h:HR + h + 1, :] = jnp.broadcast_to(jnp.exp(lb * C), (1, 128))
        ri = lax.broadcasted_iota(jnp.int32, (C, 256), 0).astype(F32)
        qdf_ref[...] = jnp.exp(lf_l * (ri + 1.0))
        qdb_ref[...] = jnp.exp(lb_l * (C - ri))
        kdf_ref[...] = jnp.exp(lf_l * (C - 1.0 - ri))
        kdb_ref[...] = jnp.exp(lb_l * ri)

    t = jax.ShapeDtypeStruct((C, 256), F32)
    return pl.pallas_call(body, name="ret_tables",
                          out_shape=[jax.ShapeDtypeStruct((HR, C, C), F32), t, t, t, t, jax.ShapeDtypeStruct((8, 128), F32)])(wf, wb)


def _ret_states(kr2, p, kdf, kdb, adec):
    S = kr2.shape[0]
    C, N = CH, S // CH
    G = _scan_group(N)
    NG = N // G

    def body(kf_ref, vf_ref, kb_ref, vb_ref, kdf_ref, kdb_ref, a_ref, rf_ref, rb_ref, sf, sb):
        @pl.when(pl.program_id(0) == 0)
        def _():
            sf[...] = jnp.zeros_like(sf)
            sb[...] = jnp.zeros_like(sb)

        kvf, kvb = [], []
        for u in range(G):
            rows = slice(C * u, C * u + C)
            kdfw = (kf_ref[rows, :] * kdf_ref[...]).astype(BF16)
            kdbw = (kb_ref[rows, :] * kdb_ref[...]).astype(BF16)
            vf, vb = vf_ref[rows, :].astype(BF16), vb_ref[rows, :].astype(BF16)
            kvf.append([_dot(kdfw[:, _ks(h)], vf[:, _vs(h)], TN) for h in range(HR)])
            kvb.append([_dot(kdbw[:, _ks(h)], vb[:, _vs(h)], TN) for h in range(HR)])
        for u in range(G):
            rf_ref[u] = sf[...]
            for h in range(HR):
                sf[h] = a_ref[h:h + 1, :] * sf[h] + kvf[u][h]
        for u in reversed(range(G)):
            rb_ref[u] = sb[...]
            for h in range(HR):
                sb[h] = a_ref[HR + h:HR + h + 1, :] * sb[h] + kvb[u][h]

    st = jax.ShapeDtypeStruct((N, HR, DH, DV), F32)
    return pl.pallas_call(
        body, name="ret_states", out_shape=[st, st], grid=(NG,),
        in_specs=[pl.BlockSpec((G * C, 256), lambda t: (t, 0)), pl.BlockSpec((G * C, 512), lambda t: (t, O_VR // 512)),
                  pl.BlockSpec((G * C, 256), lambda t: (NG - 1 - t, 0)), pl.BlockSpec((G * C, 512), lambda t: (NG - 1 - t, O_VR // 512)),
                  _full((C, 256)), _full((C, 256)), _full((8, 128))],
        out_specs=[pl.BlockSpec((G, HR, DH, DV), lambda t: (t, 0, 0, 0)), pl.BlockSpec((G, HR, DH, DV), lambda t: (NG - 1 - t, 0, 0, 0))],
        scratch_shapes=[pltpu.VMEM((HR, DH, DV), F32), pltpu.VMEM((HR, DH, DV), F32)],
        compiler_params=_cp(("arbitrary",)),
    )(kr2, p, kr2, p, kdf, kdb, adec)


def _scan_group(n):
    return 4 if n % 4 == 0 else (2 if n % 2 == 0 else 1)


def _ks(h):
    return slice(DH * h, DH * h + DH)


def _vs(h):
    return slice(DV * h, DV * h + DV)


def _ret_heads_fwd(qb, kb, vb, qfw, qbw, dc_ref, rf_ref, rb_ref, u=0):
    hs = range(HR)
    s = [_dot(qb[:, _ks(h)], kb[:, _ks(h)], NT) for h in hs]
    inter = [_dot(qfw[:, _ks(h)], rf_ref[u, h].astype(BF16)) + _dot(qbw[:, _ks(h)], rb_ref[u, h].astype(BF16)) for h in hs]
    sd = [s[h] * dc_ref[h] for h in hs]
    o = [_dot(sd[h].astype(BF16), vb[:, _vs(h)]) + inter[h] for h in hs]
    return sd, o


def _ret_out(qr2, kr2, p, rf, rb, dc, qdf, qdb, gn):
    S = qr2.shape[0]
    C, N = CH, S // CH
    G = _scan_group(N)

    def body(q_ref, k_ref, v_ref, z_ref, rf_ref, rb_ref, dc_ref, qdf_ref, qdb_ref, gn_ref, yr_ref):
        outs = []
        for u in range(G):
            rows = slice(C * u, C * u + C)
            qv = q_ref[rows, :]
            qb, kb, vb = qv.astype(BF16), k_ref[rows, :].astype(BF16), v_ref[rows, :].astype(BF16)
            qfw, qbw = (qv * qdf_ref[...]).astype(BF16), (qv * qdb_ref[...]).astype(BF16)
            outs.append(_ret_heads_fwd(qb, kb, vb, qfw, qbw, dc_ref, rf_ref, rb_ref, u)[1])
        for u in range(G):
            rows = slice(C * u, C * u + C)
            for h in range(HR):
                vs = _vs(h)
                o = outs[u][h]
                mu = jnp.mean(o, axis=-1, keepdims=True)
                var = jnp.mean(jnp.square(o - mu), axis=-1, keepdims=True)
                on = (o - mu) * lax.rsqrt(var + EPS)
                z = z_ref[rows, vs].astype(F32)
                yr_ref[rows, vs] = ((on * gn_ref[:, vs]) * (z * _sigmoid(z))).astype(BF16)

    row = lambda w, off=0: pl.BlockSpec((G * C, w), lambda t: (t, off))
    stb = lambda: pl.BlockSpec((G, HR, DH, DV), lambda t: (t, 0, 0, 0))
    return pl.pallas_call(
        body, name="ret_out", out_shape=jax.ShapeDtypeStruct((S, 512), BF16), grid=(N // G,),
        in_specs=[row(256), row(256), row(512, O_VR // 512), row(512, O_ZR // 512), stb(), stb(),
                  _full((HR, C, C)), _full((C, 256)), _full((C, 256)), _full((1, 512))],
        out_specs=row(512),
        compiler_params=_cp(("parallel",)),
    )(qr2, kr2, p, p, rf, rb, dc, qdf, qdb, gn)


def _mid(x, tgt, mod, g_post, o_att, p, yr, w_pa, w_pr, w_out):
    S = x.shape[0]
    tm = min(256, S)

    def body(x_ref, t_ref, mod_ref, gp_ref, o_ref, za_ref, gl_ref, yr_ref, wpa_ref, wpr_ref, wout_ref,
             dout_ref, do_ref, dpm_ref, dyr_ref, mb_ref, dub_ref, yab_ref, dab_ref, drb_ref, sums_ref):
        @pl.when(pl.program_id(0) == 0)
        def _():
            sums_ref[...] = jnp.zeros_like(sums_ref)

        za = za_ref[...].astype(F32)
        sa = _sigmoid(za)
        sil = za * sa
        ov = o_ref[...]
        ya_b = (ov * sil).astype(BF16)
        yr_b = yr_ref[...]
        av = _dot(ya_b, wpa_ref[...])
        rv = _dot(yr_b, wpr_ref[...])
        ga = _sigmoid(gl_ref[:, :D].astype(F32))
        gr = _sigmoid(gl_ref[:, D:].astype(F32))
        mb = (ga * av + gr * rv).astype(BF16)
        u = _dot(mb, wout_ref[...])
        r2 = lax.rsqrt(jnp.mean(u * u, axis=-1, keepdims=True) + EPS)
        un = u * r2
        gp = gp_ref[...]
        yv = un * gp
        gate = mod_ref[2:3, :]
        err = (x_ref[...] + gate * yv) - t_ref[...]
        dout = err * (1.0 / D)
        dout_ref[...] = dout
        dy = dout * gate
        sums_ref[0:1, :] += jnp.sum(dout * yv, axis=0, keepdims=True)
        sums_ref[1:2, :] += jnp.sum(dy * un, axis=0, keepdims=True)
        sums_ref[2:3, :] += jnp.sum(err * err, axis=0, keepdims=True)
        dyg = dy * gp
        du_b = (r2 * (dyg - un * jnp.mean(dyg * un, axis=-1, keepdims=True))).astype(BF16)
        dm = _dot(du_b, wout_ref[...], NT)
        da_b = (dm * ga).astype(BF16)
        dr_b = (dm * gr).astype(BF16)
        dpm_ref[:, :D] = (dm * av * (ga * (1.0 - ga))).astype(BF16)
        dpm_ref[:, D:2 * D] = (dm * rv * (gr * (1.0 - gr))).astype(BF16)
        dya = _dot(da_b, wpa_ref[...], NT)
        dyr_ref[...] = _dot(dr_b, wpr_ref[...], NT)
        dov = dya * sil
        for g in range(4):
            dt = dov[:, 128 * g:128 * g + 128].T
            do_ref[2 * g] = dt[:DH].astype(BF16)
            do_ref[2 * g + 1] = dt[DH:].astype(BF16)
        dpm_ref[:, 2 * D:] = (dya * ov * (sa * (1.0 + za * (1.0 - sa)))).astype(BF16)
        mb_ref[...] = mb
        dub_ref[...] = du_b
        yab_ref[...] = ya_b
        dab_ref[...] = da_b
        drb_ref[...] = dr_b

    row = lambda w: pl.BlockSpec((tm, w), lambda i: (i, 0))
    sd = lambda w, dt: jax.ShapeDtypeStruct((S, w), dt)
    return pl.pallas_call(
        body, name="mid",
        out_shape=[sd(D, F32), jax.ShapeDtypeStruct((8, DH, S), BF16), sd(2560, BF16), sd(512, F32), sd(D, BF16), sd(D, BF16), sd(512, BF16),
                   sd(D, BF16), sd(D, BF16), jax.ShapeDtypeStruct((8, D), F32)],
        grid=(S // tm,),
        in_specs=[row(D), row(D), _full((3, D)), _full((1, D)), row(512), pl.BlockSpec((tm, 512), lambda i: (i, O_ZA // 512)),
                  pl.BlockSpec((tm, 2048), lambda i: (i, 0)), row(512), _full((512, D)), _full((512, D)), _full((D, D))],
        out_specs=[row(D), pl.BlockSpec((8, DH, tm), lambda i: (0, 0, i)), row(2560), row(512), row(D), row(D), row(512), row(D), row(D),
                   _full((8, D))],
        compiler_params=_cp(("arbitrary",), VMEM_BIG),
    )(x, tgt, mod, g_post, o_att, p, p, yr, w_pa, w_pr, w_out)


def _attn_bwd(qt, kh, kt, vh, dot_, ot, lse):
    S = qt.shape[2]
    tq, tk = min(1024, S), min(1024, S)

    def body(q_ref, k_ref, kt_ref, v_ref, do_ref, o_ref, lse_ref, dq_ref, dk_ref, dv_ref):
        j, i = pl.program_id(0), pl.program_id(1)
        cols = pl.ds(pl.multiple_of(i * tq, tq), tq)
        st = {0: _dot(k_ref[0], q_ref[0])}
        dpt = {0: _dot(v_ref[0], do_ref[0])}
        dk_acc, dv_acc, dqs = [None, None], [None, None], []
        for h in range(8):
            g = h // 4
            if h + 1 < 8:
                st[h + 1] = _dot(k_ref[(h + 1) // 4], q_ref[h + 1])
                dpt[h + 1] = _dot(v_ref[(h + 1) // 4], do_ref[h + 1])
            qt_h, dot_h = q_ref[h], do_ref[h]
            delta = jnp.sum(dot_h.astype(F32) * o_ref[h], axis=0, keepdims=True)
            pt = jnp.exp2(st[h] - lse_ref[g, h % 4:h % 4 + 1, :])
            dst = (pt * (dpt[h] - delta)).astype(BF16)
            dv_h = _dot(dot_h, pt.astype(BF16), NT)
            dk_h = _dot(qt_h, dst, NT)
            dqs.append(_dot(kt_ref[g], dst))
            dv_acc[g] = dv_h if dv_acc[g] is None else dv_acc[g] + dv_h
            dk_acc[g] = dk_h if dk_acc[g] is None else dk_acc[g] + dk_h
            del st[h], dpt[h]

        @pl.when(i == 0)
        def _():
            for g in range(2):
                dk_ref[g] = dk_acc[g]
                dv_ref[g] = dv_acc[g]

        @pl.when(i > 0)
        def _():
            for g in range(2):
                dk_ref[g] += dk_acc[g]
                dv_ref[g] += dv_acc[g]

        @pl.when(j == 0)
        def _():
            for h in range(8):
                dq_ref[h, :, cols] = dqs[h]

        @pl.when(j > 0)
        def _():
            for h in range(8):
                dq_ref[h, :, cols] += dqs[h]

    return pl.pallas_call(
        body, name="attn_bwd",
        out_shape=[jax.ShapeDtypeStruct((8, DH, S), F32), jax.ShapeDtypeStruct((2, DH, S), F32), jax.ShapeDtypeStruct((2, DH, S), F32)],
        grid=(S // tk, S // tq),
        in_specs=[pl.BlockSpec((8, DH, tq), lambda j, i: (0, 0, i)), pl.BlockSpec((2, tk, DH), lambda j, i: (0, j, 0)),
                  pl.BlockSpec((2, DH, tk), lambda j, i: (0, 0, j)), pl.BlockSpec((2, tk, DH), lambda j, i: (0, j, 0)),
                  pl.BlockSpec((8, DH, tq), lambda j, i: (0, 0, i)), pl.BlockSpec((8, DH, tq), lambda j, i: (0, 0, i)),
                  pl.BlockSpec((2, 4, tq), lambda j, i: (0, 0, i))],
        out_specs=[pl.BlockSpec((8, DH, S), lambda j, i: (0, 0, 0)), pl.BlockSpec((2, DH, tk), lambda j, i: (0, 0, j)),
                   pl.BlockSpec((2, DH, tk), lambda j, i: (0, 0, j))],
        compiler_params=_cp(("arbitrary", "arbitrary"), VMEM_BIG),
    )(qt, kh, kt, vh, dot_, ot, lse)


def _attn_prep_bwd(dqt, dkt, dvt, p, cos, sin, qg, kg):
    S = dqt.shape[2]
    tm = min(512, S)

    def body(dq_ref, dk_ref, dv_ref, qa_ref, ka_ref, cos_ref, sin_ref, qg_ref, kg_ref, dp_ref, gs_ref):
        @pl.when(pl.program_id(0) == 0)
        def _():
            gs_ref[...] = jnp.zeros_like(gs_ref)

        cos_v, sin_v = cos_ref[...], sin_ref[...]

        def pair(ref, a):
            return jnp.concatenate([ref[a], ref[a + 1]], axis=0).T

        def norm_bwd(dyv, xv, gv, row):
            r = lax.rsqrt(_head_mean(xv * xv) + EPS)
            xn = xv * r
            dxh = _rope_t(dyv, cos_v, sin_v)
            gs_ref[row:row + 1, :] += jnp.sum(dxh * xn, axis=0, keepdims=True)
            dg = dxh * gv
            return r * (dg - xn * _head_mean(dg * xn))

        for g in range(4):
            sl = slice(128 * g, 128 * g + 128)
            dp_ref[:, sl] = norm_bwd(pair(dq_ref, 2 * g) * 0.125, qa_ref[:, sl].astype(F32), qg_ref[...], 0).astype(BF16)
        dp_ref[:, 512:640] = norm_bwd(pair(dk_ref, 0) * LN2, ka_ref[...].astype(F32), kg_ref[...], 1).astype(BF16)
        dp_ref[:, 640:768] = pair(dv_ref, 0).astype(BF16)

    ht = lambda n: pl.BlockSpec((n, DH, tm), lambda i: (0, 0, i))
    return pl.pallas_call(
        body, name="attn_prep_bwd", out_shape=[jax.ShapeDtypeStruct((S, 768), BF16), jax.ShapeDtypeStruct((8, 128), F32)],
        grid=(S // tm,),
        in_specs=[ht(8), ht(2), ht(2),
                  pl.BlockSpec((tm, 512), lambda i: (i, O_QA // 512)), pl.BlockSpec((tm, 128), lambda i: (i, O_KA // 128)),
                  pl.BlockSpec((tm, 128), lambda i: (i, 0)), pl.BlockSpec((tm, 128), lambda i: (i, 0)), _full((1, 128)), _full((1, 128))],
        out_specs=[pl.BlockSpec((tm, 768), lambda i: (i, 0)), _full((8, 128))],
        compiler_params=_cp(("arbitrary",)),
    )(dqt, dkt, dvt, p, p, cos, sin, qg, kg)


def _ret_bwd_chunk(qr2, kr2, p, rf, rb, dc, qdf, qdb, gn, dyr, cos, sin):
    S = qr2.shape[0]
    C, N = CH, S // CH
    G = 2 if N % 2 == 0 else 1

    def body(q_ref, k_ref, v_ref, z_ref, rf_ref, rb_ref, dc_ref, qdf_ref, qdb_ref, gn_ref, dyr_ref, cos_ref, sin_ref,
             dpa_ref, dk_ref, dv_ref, drf_ref, drb_ref, dgn_ref, dlg_ref, dqs):
        @pl.when(pl.program_id(0) == 0)
        def _():
            dgn_ref[...] = jnp.zeros_like(dgn_ref)
            dlg_ref[...] = jnp.zeros_like(dlg_ref)

        ii = lax.broadcasted_iota(jnp.int32, (C, C), 0).astype(F32)
        jj = lax.broadcasted_iota(jnp.int32, (C, C), 1).astype(F32)
        dif = ii - jj
        ri = lax.broadcasted_iota(jnp.int32, (C, 1), 0).astype(F32)
        hs, us = range(HR), range(G)
        rows = [slice(C * u, C * u + C) for u in us]
        qv = [q_ref[rows[u], :] for u in us]
        qb = [qv[u].astype(BF16) for u in us]
        kb = [k_ref[rows[u], :].astype(BF16) for u in us]
        vb = [v_ref[rows[u], :].astype(BF16) for u in us]
        qf32 = [qv[u] * qdf_ref[...] for u in us]
        qb32 = [qv[u] * qdb_ref[...] for u in us]
        qfw = [qf32[u].astype(BF16) for u in us]
        qbw = [qb32[u].astype(BF16) for u in us]
        fwd = [_ret_heads_fwd(qb[u], kb[u], vb[u], qfw[u], qbw[u], dc_ref, rf_ref, rb_ref, u) for u in us]
        sd = [f[0] for f in fwd]
        do_b = [[] for _ in us]
        for u in us:
            for h in hs:
                vs = _vs(h)
                o = fwd[u][1][h]
                mu = jnp.mean(o, axis=-1, keepdims=True)
                rstd = lax.rsqrt(jnp.mean(jnp.square(o - mu), axis=-1, keepdims=True) + EPS)
                on = (o - mu) * rstd
                z = z_ref[rows[u], vs].astype(F32)
                sz = _sigmoid(z)
                dy = dyr_ref[rows[u], vs]
                gnv = gn_ref[:, vs]
                dpa_ref[rows[u], 256 + DV * h:256 + DV * h + DV] = (dy * (on * gnv) * (sz * (1.0 + z * (1.0 - sz)))).astype(BF16)
                dys = dy * (z * sz)
                dgn_ref[:, vs] += jnp.sum(dys * on, axis=0, keepdims=True)
                don = dys * gnv
                do = rstd * (don - jnp.mean(don, axis=-1, keepdims=True) - on * jnp.mean(don * on, axis=-1, keepdims=True))
                do_b[u].append(do.astype(BF16))
        dpm = [[_dot(do_b[u][h], vb[u][:, _vs(h)], NT) for h in hs] for u in us]
        dqf = [[_dot(do_b[u][h], rf_ref[u, h].astype(BF16), NT) for h in hs] for u in us]
        dqb = [[_dot(do_b[u][h], rb_ref[u, h].astype(BF16), NT) for h in hs] for u in us]
        for u in us:
            for h in hs:
                dv_ref[rows[u], _vs(h)] = _dot(sd[u][h].astype(BF16), do_b[u][h], TN)
                drf_ref[u, h] = _dot(qfw[u][:, _ks(h)], do_b[u][h], TN)
                drb_ref[u, h] = _dot(qbw[u][:, _ks(h)], do_b[u][h], TN)
        dsd = [[(dpm[u][h] * dc_ref[h]).astype(BF16) for h in hs] for u in us]
        for u in us:
            for h in hs:
                ks = _ks(h)
                dqs[rows[u], ks] = _dot(dsd[u][h], kb[u][:, ks]) + dqf[u][h] * qdf_ref[:, ks] + dqb[u][h] * qdb_ref[:, ks]
                dk_ref[rows[u], ks] = _dot(dsd[u][h], qb[u][:, ks], TN)
        for u in us:
            for h in hs:
                ks = _ks(h)
                e = dpm[u][h] * sd[u][h]
                lf = (_sum11(e * jnp.maximum(dif, 0.0))
                      + _sum11(jnp.sum(qf32[u][:, ks] * dqf[u][h], axis=-1, keepdims=True) * (ri + 1.0)))
                lb = (_sum11(e * jnp.maximum(-dif, 0.0))
                      + _sum11(jnp.sum(qb32[u][:, ks] * dqb[u][h], axis=-1, keepdims=True) * (C - ri)))
                dlg_ref[h:h + 1, :] += jnp.broadcast_to(lf, (1, 128))
                dlg_ref[HR + h:HR + h + 1, :] += jnp.broadcast_to(lb, (1, 128))
            for g in range(2):
                sl = slice(128 * g, 128 * g + 128)
                dpa_ref[rows[u], sl] = _rope_t(dqs[rows[u], sl], cos_ref[rows[u], :], sin_ref[rows[u], :]).astype(BF16)

    st = jax.ShapeDtypeStruct((N, HR, DH, DV), F32)
    stb = lambda: pl.BlockSpec((G, HR, DH, DV), lambda t: (t, 0, 0, 0))
    row = lambda w, off=0: pl.BlockSpec((G * C, w), lambda t: (t, off))
    return pl.pallas_call(
        body, name="ret_bwd_chunk",
        out_shape=[jax.ShapeDtypeStruct((S, 768), BF16), jax.ShapeDtypeStruct((S, 256), F32), jax.ShapeDtypeStruct((S, 512), F32), st, st,
                   jax.ShapeDtypeStruct((1, 512), F32), jax.ShapeDtypeStruct((8, 128), F32)],
        grid=(N // G,),
        in_specs=[row(256), row(256), row(512, O_VR // 512), row(512, O_ZR // 512),
                  stb(), stb(), _full((HR, C, C)), _full((C, 256)), _full((C, 256)), _full((1, 512)), row(512), row(128), row(128)],
        out_specs=[row(768), row(256), row(512), stb(), stb(), _full((1, 512)), _full((8, 128))],
        scratch_shapes=[pltpu.VMEM((G * C, 256), F32)],
        compiler_params=_cp(("arbitrary",)),
    )(qr2, kr2, p, p, rf, rb, dc, qdf, qdb, gn, dyr, cos, sin)


def _ret_bwd_scan(kr2, p, rf, rb, drf, drb, kdf, kdb, adec):
    S = kr2.shape[0]
    C, N = CH, S // CH
    G = _scan_group(N)
    NG = N // G

    def body(kf_ref, vf_ref, kb_ref, vb_ref, rf_ref, rb_ref, drf_ref, drb_ref, kdf_ref, kdb_ref, a_ref,
             dkf_ref, dkb_ref, dvf_ref, dvb_ref, dlg_ref, gf, gb):
        @pl.when(pl.program_id(0) == 0)
        def _():
            gf[...] = jnp.zeros_like(gf)
            gb[...] = jnp.zeros_like(gb)
            dlg_ref[...] = jnp.zeros_like(dlg_ref)

        ri = lax.broadcasted_iota(jnp.int32, (C, 1), 0).astype(F32)

        def one(k_ref, v_ref, r_ref, dr_ref, kd_ref, g_s, dk_ref, dv_ref, row0, wexp, order):
            g = [g_s[h] for h in range(HR)]
            lgs = [jnp.zeros((1, 1), F32) for _ in range(HR)]
            for u in order:
                rows = slice(C * u, C * u + C)
                kd32 = k_ref[rows, :] * kd_ref[...]
                kdw = kd32.astype(BF16)
                vb = v_ref[rows, :].astype(BF16)
                for h in range(HR):
                    ks, vs = _ks(h), _vs(h)
                    g_b = g[h].astype(BF16)
                    dkd = _dot(vb[:, vs], g_b, NT)
                    dk_ref[rows, ks] = dkd * kd_ref[:, ks]
                    dv_ref[rows, vs] = _dot(kdw[:, ks], g_b)
                    av = a_ref[row0 + h:row0 + h + 1, :]
                    lgs[h] = lgs[h] + (_sum11(jnp.sum(kd32[:, ks] * dkd, axis=-1, keepdims=True) * wexp)
                                       + C * av[:, 0:1] * _sum11(r_ref[u, h] * g[h]))
                    g[h] = dr_ref[u, h] + av * g[h]
            for h in range(HR):
                g_s[h] = g[h]
                dlg_ref[row0 + h:row0 + h + 1, :] += jnp.broadcast_to(lgs[h], (1, 128))

        one(kf_ref, vf_ref, rf_ref, drf_ref, kdf_ref, gf, dkf_ref, dvf_ref, 0, C - 1.0 - ri, list(reversed(range(G))))
        one(kb_ref, vb_ref, rb_ref, drb_ref, kdb_ref, gb, dkb_ref, dvb_ref, HR, ri, list(range(G)))

    fwd = lambda w, off=0: pl.BlockSpec((G * C, w), lambda t: (NG - 1 - t, off))
    bwd = lambda w, off=0: pl.BlockSpec((G * C, w), lambda t: (t, off))
    stf = lambda: pl.BlockSpec((G, HR, DH, DV), lambda t: (NG - 1 - t, 0, 0, 0))
    stb = lambda: pl.BlockSpec((G, HR, DH, DV), lambda t: (t, 0, 0, 0))
    return pl.pallas_call(
        body, name="ret_bwd_scan",
        out_shape=[jax.ShapeDtypeStruct((S, 256), F32), jax.ShapeDtypeStruct((S, 256), F32), jax.ShapeDtypeStruct((S, 512), F32),
                   jax.ShapeDtypeStruct((S, 512), F32), jax.ShapeDtypeStruct((8, 128), F32)],
        grid=(NG,),
        in_specs=[fwd(256), fwd(512, O_VR // 512), bwd(256), bwd(512, O_VR // 512), stf(), stb(), stf(), stb(),
                  _full((C, 256)), _full((C, 256)), _full((8, 128))],
        out_specs=[fwd(256), bwd(256), fwd(512), bwd(512), _full((8, 128))],
        scratch_shapes=[pltpu.VMEM((HR, DH, DV), F32), pltpu.VMEM((HR, DH, DV), F32)],
        compiler_params=_cp(("arbitrary",)),
    )(kr2, p, kr2, p, rf, rb, drf, drb, kdf, kdb, adec)


def _ret_bwd_final(dk_i, dkf, dkb, dv_i, dvf, dvb, cos, sin):
    S = dk_i.shape[0]
    tm = min(512, S)

    def body(a_ref, b_ref, c_ref, d_ref, e_ref, f_ref, cos_ref, sin_ref, o_ref):
        o_ref[:, :512] = (d_ref[...] + e_ref[...] + f_ref[...]).astype(BF16)
        cos_v, sin_v = cos_ref[...], sin_ref[...]
        for g in range(2):
            sl = slice(128 * g, 128 * g + 128)
            dk = a_ref[:, sl] + b_ref[:, sl] + c_ref[:, sl]
            o_ref[:, 512 + 128 * g:512 + 128 * g + 128] = (_rope_t(dk, cos_v, sin_v) * 0.125).astype(BF16)

    row = lambda w: pl.BlockSpec((tm, w), lambda i: (i, 0))
    return pl.pallas_call(
        body, name="ret_bwd_final", out_shape=jax.ShapeDtypeStruct((S, 768), BF16), grid=(S // tm,),
        in_specs=[row(256), row(256), row(256), row(512), row(512), row(512), row(128), row(128)], out_specs=row(768),
        compiler_params=_cp(("parallel",)),
    )(dk_i, dkf, dkb, dv_i, dvf, dvb, cos, sin)


def _bwd_in(dpm, dpa, dpra, dprb, w_p, x, dout, mod, g_pre):
    S = x.shape[0]
    tm = min(256, S)

    def body(a_ref, b_ref, c_ref, d_ref, w_ref, x_ref, dout_ref, mod_ref, g_ref, gx_ref, sums_ref):
        @pl.when(pl.program_id(0) == 0)
        def _():
            sums_ref[...] = jnp.zeros_like(sums_ref)

        dh = (_dot(a_ref[...], w_ref[:, :O_QA], NT) + _dot(b_ref[...], w_ref[:, O_QA:O_QR], NT)
              + _dot(c_ref[...], w_ref[:, O_QR:O_VR], NT) + _dot(d_ref[...], w_ref[:, O_VR:], NT))
        xv = x_ref[...]
        r = lax.rsqrt(jnp.mean(xv * xv, axis=-1, keepdims=True) + EPS)
        xn = xv * r
        gv = g_ref[...]
        sc1 = 1.0 + mod_ref[1:2, :]
        sums_ref[0:1, :] += jnp.sum(dh, axis=0, keepdims=True)
        sums_ref[1:2, :] += jnp.sum(dh * (xn * gv), axis=0, keepdims=True)
        sums_ref[2:3, :] += jnp.sum(dh * xn, axis=0, keepdims=True) * sc1
        dxn = dh * (gv * sc1)
        gx_ref[...] = dout_ref[...] + r * (dxn - xn * jnp.mean(dxn * xn, axis=-1, keepdims=True))

    row = lambda w: pl.BlockSpec((tm, w), lambda i: (i, 0))
    return pl.pallas_call(
        body, name="bwd_in", out_shape=[jax.ShapeDtypeStruct((S, D), F32), jax.ShapeDtypeStruct((8, D), F32)], grid=(S // tm,),
        in_specs=[row(2560), row(768), row(768), row(768), _full((D, P_W)), row(D), row(D), _full((3, D)), _full((1, D))],
        out_specs=[row(D), _full((8, D))],
        compiler_params=_cp(("arbitrary",), VMEM_BIG),
    )(dpm, dpa, dpra, dprb, w_p, x, dout, mod, g_pre)


SMALL = ("b_ada", "g_pre", "qn_g", "kn_g", "w_dec_f", "w_dec_b", "gn_g", "g_post")


def _small_update(gathered, wmv):
    ns = len(SMALL)

    def body(*refs):
        gin_ref, gmid_ref, ggn_ref, gatt_ref, gl1_ref, gl2_ref = refs[:6]
        wmv_refs = refs[6:6 + 3 * ns]
        loss_ref = refs[6 + 3 * ns]
        out_refs = refs[7 + 3 * ns:]

        def dsum(ref, r=None):
            rows = slice(None) if r is None else slice(r, r + 1)
            acc = ref[0, rows, :]
            for d in range(1, NDEV):
                acc = acc + ref[d, rows, :]
            return acc

        s_lg = dsum(gl1_ref) + dsum(gl2_ref)
        loss_ref[...] = (0.5 / D) * jnp.sum(dsum(gmid_ref, 2), axis=-1, keepdims=True)
        eye = lax.broadcasted_iota(jnp.int32, (8, 128), 0) == lax.broadcasted_iota(jnp.int32, (8, 128), 1)
        dlg = jnp.sum(jnp.where(eye, s_lg, 0.0), axis=0, keepdims=True)
        w_f, w_b = wmv_refs[3 * SMALL.index("w_dec_f")][...], wmv_refs[3 * SMALL.index("w_dec_b")][...]
        s_q, s_k = dsum(gatt_ref, 0), dsum(gatt_ref, 1)
        grads = dict(
            b_ada=jnp.concatenate([dsum(gin_ref, 0), dsum(gin_ref, 1), dsum(gmid_ref, 0)], axis=1),
            g_pre=dsum(gin_ref, 2), g_post=dsum(gmid_ref, 1), gn_g=dsum(ggn_ref),
            qn_g=s_q[:, :DH] + s_q[:, DH:], kn_g=s_k[:, :DH] + s_k[:, DH:],
            w_dec_f=dlg[:, 0:HR] * _sigmoid(-w_f), w_dec_b=dlg[:, HR:2 * HR] * _sigmoid(-w_b))
        for i, nme in enumerate(SMALL):
            g = grads[nme]
            w_ref, m_ref, v_ref = wmv_refs[3 * i:3 * i + 3]
            g_ref, d_ref, nm_ref, nv_ref = out_refs[4 * i:4 * i + 4]
            g_ref[...] = g
            m2 = ADAM_B1 * m_ref[...] + (1.0 - ADAM_B1) * g
            v2 = ADAM_B2 * v_ref[...] + (1.0 - ADAM_B2) * jnp.square(g)
            m_hat = m2 / (1.0 - ADAM_B1 ** ADAM_STEP)
            v_hat = v2 / (1.0 - ADAM_B2 ** ADAM_STEP)
            d_ref[...] = -ADAM_LR * (m_hat / (jnp.sqrt(v_hat) + ADAM_EPS) + ADAM_WD * w_ref[...])
            nm_ref[...] = m2
            nv_ref[...] = v2

    out_shape = [jax.ShapeDtypeStruct((1, 1), F32)]
    for i in range(ns):
        out_shape += [jax.ShapeDtypeStruct(wmv[3 * i].shape, F32)] * 4
    return pl.pallas_call(body, name="small_update", out_shape=out_shape)(*gathered, *wmv)


def _adamw(parts, w, m, v, name):
    n, R, L = parts.shape
    tr = 256 if (R % 256 == 0 and R > 256) else R

    def body(p_ref, w_ref, m_ref, v_ref, g_ref, d_ref, nm_ref, nv_ref):
        g = p_ref[0].astype(F32)
        for k in range(1, n):
            g = g + p_ref[k].astype(F32)
        g_ref[...] = g
        m2 = ADAM_B1 * m_ref[...] + (1.0 - ADAM_B1) * g
        v2 = ADAM_B2 * v_ref[...] + (1.0 - ADAM_B2) * jnp.square(g)
        m_hat = m2 / (1.0 - ADAM_B1 ** ADAM_STEP)
        v_hat = v2 / (1.0 - ADAM_B2 ** ADAM_STEP)
        d_ref[...] = -ADAM_LR * (m_hat / (jnp.sqrt(v_hat) + ADAM_EPS) + ADAM_WD * w_ref[...])
        nm_ref[...] = m2
        nv_ref[...] = v2

    blk = pl.BlockSpec((tr, L), lambda i: (i, 0))
    o = jax.ShapeDtypeStruct((R, L), F32)
    return pl.pallas_call(
        body, name=name, out_shape=[o, o, o, o], grid=(R // tr,),
        in_specs=[pl.BlockSpec((n, tr, L), lambda i: (0, i, 0)), blk, blk, blk], out_specs=[blk, blk, blk, blk],
        compiler_params=_cp(("parallel",), VMEM_SMALL),
    )(parts, w, m, v)


def _rope_tables(S):
    f = np.float32
    t = np.arange(S)
    row, col = (t // 64).astype(f), (t % 64).astype(f)
    half = DH // 2
    inv = np.power(f(ROPE_THETA), -np.arange(0, half, 2, dtype=f) / f(half)).astype(f)
    ar, ac = (row[:, None] * inv[None, :]).astype(f), (col[:, None] * inv[None, :]).astype(f)
    cos64 = np.concatenate([np.cos(ar), np.cos(ar), np.cos(ac), np.cos(ac)], axis=1).astype(f)
    sin64 = np.concatenate([-np.sin(ar), np.sin(ar), -np.sin(ac), np.sin(ac)], axis=1).astype(f)
    return jnp.asarray(np.tile(cos64, (1, 2))), jnp.asarray(np.tile(sin64, (1, 2)))


def _to_p_order(w_orig):
    return jnp.concatenate([w_orig[:, ORIG[n][0]:ORIG[n][1]] for n in P_ORDER], axis=1)


def _pad_lanes(v, n):
    return jnp.pad(v, ((0, 0), (0, n - v.shape[1])))


def kernel(x, c, w_ada, b_ada, g_pre, w_in, qn_g, kn_g, w_dec_f, w_dec_b, gn_g, w_pa, w_pr, w_out, g_post, loss_target, m_w_ada, m_b_ada, m_g_pre, m_w_in, m_qn_g, m_kn_g, m_w_dec_f, m_w_dec_b, m_gn_g, m_w_pa, m_w_pr, m_w_out, m_g_post, v_w_ada, v_b_ada, v_g_pre, v_w_in, v_qn_g, v_kn_g, v_w_dec_f, v_w_dec_b, v_gn_g, v_w_pa, v_w_pr, v_w_out, v_g_post):
    S = x.shape[1]
    me = 4 * lax.axis_index("x") + 2 * lax.axis_index("y") + lax.axis_index("c")
    xs, tgt = x[0], loss_target[0]
    ncol_ada = w_ada.shape[2]
    ncol_in = w_in.shape[2]

    b_ada_s = lax.dynamic_slice(b_ada, (0, me * ncol_ada), (1, ncol_ada))
    mod_all, c_act, (wg_in,) = _prologue(jnp.pad(c, ((0, 7), (0, 0))), w_ada[0], b_ada_s, [w_in[0].astype(BF16)])
    mod = lax.dynamic_index_in_dim(mod_all, me, axis=1, keepdims=False).reshape(3, D)
    w_p = _to_p_order(wg_in.transpose(1, 0, 2).reshape(D, NDEV * ncol_in))
    all_dev = tuple(range(NDEV))
    st_w, tok_w = _xchg_start([(w_pa[0].astype(BF16)[None], all_dev), (w_pr[0].astype(BF16)[None], all_dev),
                               (w_out[0].astype(BF16)[None], all_dev)], "wgather_start")

    cos, sin = _rope_tables(S)
    qg, kg = jnp.tile(qn_g, (1, 2)), jnp.tile(kn_g, (1, 2))

    p, h = _fwd_in(xs, mod, g_pre + tok_w[0:1, 0:1], w_p)
    qt, kh, kt, vh, vta, qr2, kr2 = _prep(p, cos, sin, qg, kg)
    o_att, o_t, lse = _attn_fwd(qt, kh, vta)
    dc, qdf, qdb, kdf, kdb, adec = _ret_tables(w_dec_f, w_dec_b)
    rf, rb = _ret_states(kr2, p, kdf, kdb, adec)
    yr = _ret_out(qr2, kr2, p, rf, rb, dc, qdf, qdb, gn_g)
    wg_pa, wg_pr, wg_out = _xchg_wait([st_w], st_w["lands"], [[0, 1, 2]], yr, "wgather_wait")
    w_pa_f = wg_pa.transpose(1, 0, 2).reshape(512, D)
    w_pr_f = wg_pr.transpose(1, 0, 2).reshape(512, D)
    w_out_f = wg_out.reshape(D, D)

    dout, do, dpm, dyr, mb, dub, yab, dab, drb_, sums_mid = _mid(xs, tgt, mod, g_post, o_att, p, yr, w_pa_f, w_pr_f, w_out_f)
    gw_out = _mm_tn(mb, dub, "gw_out", BF16)
    gw_pa = _mm_tn(yab, dab, "gw_pa", BF16)
    gw_pr = _mm_tn(yr, drb_, "gw_pr", BF16)
    gi_m = _mm_tn(h, dpm, "gw_in_mid", BF16)

    def shards(cols, nd):
        return cols.reshape(D, nd, ncol_in).transpose(1, 0, 2)

    st_a, tok_a = _xchg_start([
        (gw_out.reshape(NDEV, 128, D), all_dev),
        (gw_pa.reshape(512, NDEV, 128).transpose(1, 0, 2), all_dev),
        (gw_pr.reshape(512, NDEV, 128).transpose(1, 0, 2), all_dev),
        (shards(gi_m[:, 224:2048], 3), (5, 6, 7))], "xchg_start_a",
        lands=[None, None, None, jnp.zeros((NDEV, D, ncol_in), BF16)])
    dqt, dkt, dvt = _attn_bwd(qt, kh, kt, vh, do, o_t, lse + tok_a[0, 0])
    dpa, gs_att = _attn_prep_bwd(dqt, dkt, dvt, p, cos, sin, qg, kg)
    gi_a = _mm_tn(h, dpa, "gw_in_att", BF16)
    st_b, tok_b = _xchg_start([(shards(jnp.concatenate([gi_a, gi_m[:, 2048:2496]], axis=1), 2), (0, 1))], "xchg_start_b",
                              lands=[st_a["lands"][3]])
    dpra, dk_i, dv_i, drf, drb, dgn, dlg1 = _ret_bwd_chunk(qr2, kr2, p, rf, rb, dc, qdf, qdb, gn_g + tok_b[0:1, 0:1], dyr, cos, sin)
    dkf, dkb, dvf, dvb, dlg2 = _ret_bwd_scan(kr2, p, rf, rb, drf, drb, kdf, kdb, adec)
    dprb = _ret_bwd_final(dk_i, dkf, dkb, dv_i, dvf, dvb, cos, sin)
    gi_ra = _mm_tn(h, dpra, "gw_in_reta", BF16)
    gi_rb = _mm_tn(h, dprb, "gw_in_retb", BF16)
    chip_c = _pair_reduce(shards(jnp.concatenate([gi_m[:, 2496:2560], gi_ra[:, :256], gi_rb[:, 512:768], gi_rb[:, :512],
                                                  gi_ra[:, 256:768], gi_m[:, :224]], axis=1), 3), (2, 3, 4), "pair_reduce_c")
    st_c, tok_c = _xchg_start([(chip_c, (2, 3, 4, "same core"))], "xchg_start_c", lands=[st_b["lands"][0]])
    grad_x, sums_in = _bwd_in(dpm, dpa, dpra, dprb, w_p, xs, dout, mod, g_pre + tok_c[0:1, 0:1])

    gathered = _small_allgather([sums_in, sums_mid, dgn, gs_att, dlg1, dlg2], "ag_small")
    given = dict(b_ada=(b_ada, m_b_ada, v_b_ada), g_pre=(g_pre, m_g_pre, v_g_pre), qn_g=(qn_g, m_qn_g, v_qn_g), kn_g=(kn_g, m_kn_g, v_kn_g),
                 w_dec_f=(w_dec_f, m_w_dec_f, v_w_dec_f), w_dec_b=(w_dec_b, m_w_dec_b, v_w_dec_b), gn_g=(gn_g, m_gn_g, v_gn_g),
                 g_post=(g_post, m_g_post, v_g_post))
    small = _small_update(gathered, [a for nme in SMALL for a in given[nme]])
    loss = small[0][0, 0]

    g_in_all, g_mid_all = gathered[0], gathered[1]
    dmod_all = lax.dynamic_slice(jnp.concatenate([g_in_all[:, 0, :], g_in_all[:, 1, :], g_mid_all[:, 0, :]], axis=1),
                                 (0, me * ncol_ada), (NDEV, ncol_ada))
    g_ada = _mm_tn(c_act, jnp.pad(dmod_all, ((0, 8), (0, 0))).astype(BF16), "gw_ada")

    ada = _adamw(g_ada[None], w_ada[0], m_w_ada[0], v_w_ada[0], "adamw_ada")
    rs_out, rs_pa, rs_pr, rs_in = _xchg_wait([st_a, st_b, st_c], list(st_a["lands"][:3]) + [st_c["lands"][0]],
                                             [[0, 1, 2, 3], [3], [3]], ada[1], "xchg_wait")
    res = dict(
        w_ada=ada,
        w_in=_adamw(rs_in, w_in[0], m_w_in[0], v_w_in[0], "adamw_in"),
        w_pa=_adamw(rs_pa, w_pa[0], m_w_pa[0], v_w_pa[0], "adamw_pa"),
        w_pr=_adamw(rs_pr, w_pr[0], m_w_pr[0], v_w_pr[0], "adamw_pr"),
        w_out=_adamw(rs_out, w_out[0], m_w_out[0], v_w_out[0], "adamw_out"),
    )
    names = ["w_ada", "b_ada", "g_pre", "w_in", "qn_g", "kn_g", "w_dec_f", "w_dec_b", "gn_g", "w_pa", "w_pr", "w_out", "g_post"]
    outs = [[], [], [], []]
    for nme in names:
        for q in range(4):
            if nme in res:
                outs[q].append(res[nme][q][None])
            else:
                outs[q].append(small[1 + 4 * SMALL.index(nme) + q])
    return (loss, grad_x[None], *outs[0], *outs[1], *outs[2], *outs[3])
```
